```python
import math
import jax, jax.numpy as jnp
from jax import lax
import numpy as np

D_MODEL = 2048
BATCH = 8
SEQ = 8192
DEPTH = 1

BLOCK = 128
WINDOW = 128
A_HEADS = 8
A_HEAD_DIM = 128
A_WIDTH = A_HEADS * A_HEAD_DIM
B_HEADS = 16
B_KV_HEADS = 2
HEAD_DIM = 64
B_WIDTH = B_HEADS * HEAD_DIM
KV_WIDTH = B_KV_HEADS * HEAD_DIM
MIX_WIDTH = A_WIDTH + B_WIDTH
IN_COLS = 2 * A_WIDTH + B_WIDTH + 2 * KV_WIDTH
D_FF = 5632
N_BUCKETS = 32
MAX_DISTANCE = 128
N_MOD = 9
EPS = 1e-6

kernel_name = "hybrid_gmlp_swa_macaron_adaln_layer"


def rms_norm(x, g):
    xf = x.astype(jnp.float32)
    y = xf * lax.rsqrt(jnp.mean(xf * xf, axis=-1, keepdims=True) + EPS)
    return (y * g.astype(jnp.float32)).astype(x.dtype)


def modulate(h, shift, scale):
    return h * (1 + scale[:, None, :]) + shift[:, None, :]


def swiglu(h, w1, w3, w2):
    return (jax.nn.silu(h @ w1) * (h @ w3)) @ w2


def t5_bucket(n):
    max_exact = N_BUCKETS // 2
    nf = jnp.maximum(n, 1).astype(jnp.float32)
    large = max_exact + (jnp.log(nf / max_exact) / math.log(MAX_DISTANCE / max_exact)
                         * (N_BUCKETS - max_exact)).astype(jnp.int32)
    large = jnp.minimum(large, N_BUCKETS - 1)
    return jnp.where(n < max_exact, n, large)


def gmlp_mixer(u, v, spatial_w, spatial_b, g_v):
    bsz, seq, _ = u.shape
    nb = seq // BLOCK
    u = u.reshape(bsz, nb, BLOCK, A_HEADS, A_HEAD_DIM)
    v = rms_norm(v.reshape(bsz, nb, BLOCK, A_HEADS, A_HEAD_DIM), g_v)
    causal = jnp.tril(jnp.ones((BLOCK, BLOCK), dtype=bool))
    w = jnp.where(causal[None], spatial_w, 0)
    mixed = jnp.einsum('hij,bnjhd->bnihd', w, v) + spatial_b.T[None, None, :, :, None]
    return (u * mixed).reshape(bsz, seq, A_WIDTH)


def swa_mixer(q, k, v, g_q, g_k, sinks, rel_bias):
    bsz, seq, _ = q.shape
    nb = seq // BLOCK
    grp = B_HEADS // B_KV_HEADS
    q = rms_norm(q.reshape(bsz, seq, B_HEADS, HEAD_DIM), g_q)
    k = rms_norm(k.reshape(bsz, seq, B_KV_HEADS, HEAD_DIM), g_k)
    q = q.reshape(bsz, nb, BLOCK, B_KV_HEADS, grp, HEAD_DIM)
    k = k.reshape(bsz, nb, BLOCK, B_KV_HEADS, HEAD_DIM)
    v = v.reshape(bsz, nb, BLOCK, B_KV_HEADS, HEAD_DIM)

    def band(t):
        prev = jnp.pad(t[:, :-1], ((0, 0), (1, 0), (0, 0), (0, 0), (0, 0)))
        return jnp.concatenate([prev, t], axis=2)

    kb, vb = band(k), band(v)
    s = jnp.einsum('bnikgd,bnjkd->bnkgij', q, kb).astype(jnp.float32) * (HEAD_DIM ** -0.5)

    qi = jnp.arange(BLOCK)[:, None]
    kj = jnp.arange(2 * BLOCK)[None, :]
    dist = qi + BLOCK - kj
    in_window = (dist >= 0) & (dist < WINDOW)
    bucket = t5_bucket(jnp.clip(dist, 0, None))
    bias = jnp.transpose(rel_bias[bucket], (2, 0, 1)).astype(jnp.float32)
    bias = bias.reshape(B_KV_HEADS, grp, BLOCK, 2 * BLOCK)
    key_pos = jnp.arange(nb)[:, None] * BLOCK - BLOCK + kj
    valid = in_window[None] & (key_pos >= 0)[:, None, :]

    s = jnp.where(valid[None, :, None, None], s + bias, -jnp.inf)
    sink = sinks.astype(jnp.float32).reshape(B_KV_HEADS, grp)[:, :, None, None]
    m = jnp.maximum(jnp.max(s, axis=-1, keepdims=True), sink)
    p = jnp.exp(s - m)
    w = p / (jnp.sum(p, axis=-1, keepdims=True) + jnp.exp(sink - m))
    o = jnp.einsum('bnkgij,bnjkd->bnikgd', w.astype(vb.dtype), vb)
    return o.reshape(bsz, seq, B_WIDTH)


def _fwd_setup_inputs(seed: int = 0) -> dict:
    key = jax.random.key(seed)
    ks = jax.random.split(key, 24)
    f32 = jnp.float32
    nrm = lambda k, shape, s: jax.random.normal(k, shape, f32) * s
    gain = lambda k, shape: 1.0 + 0.02 * jax.random.normal(k, shape, f32)
    L, D = DEPTH, D_MODEL
    return {
        "x": nrm(ks[0], (BATCH, SEQ, D), 1.0),
        "c": nrm(ks[1], (BATCH, D), 1.0),
        "w_ada": nrm(ks[2], (L, D, N_MOD * D), 0.5 * D ** -0.5),
        "b_ada": nrm(ks[3], (L, N_MOD * D), 0.01),
        "g_ffn1": gain(ks[4], (L, D)),
        "w1_ffn1": nrm(ks[5], (L, D, D_FF), D ** -0.5),
        "w3_ffn1": nrm(ks[6], (L, D, D_FF), D ** -0.5),
        "w2_ffn1": nrm(ks[7], (L, D_FF, D), D_FF ** -0.5),
        "g_mix": gain(ks[8], (L, D)),
        "w_in": nrm(ks[9], (L, D, IN_COLS), D ** -0.5),
        "spatial_w": nrm(ks[10], (L, A_HEADS, BLOCK, BLOCK), BLOCK ** -0.5),
        "spatial_b": 1.0 + nrm(ks[11], (L, A_HEADS, BLOCK), 0.01),
        "g_v": gain(ks[12], (L, A_HEADS, A_HEAD_DIM)),
        "g_q": gain(ks[13], (L, HEAD_DIM)),
        "g_k": gain(ks[14], (L, HEAD_DIM)),
        "sinks": nrm(ks[15], (L, B_HEADS), 1.0),
        "rel_bias": nrm(ks[16], (N_BUCKETS, B_HEADS), 0.5),
        "w_out": nrm(ks[17], (L, MIX_WIDTH, D), MIX_WIDTH ** -0.5),
        "g_ffn2": gain(ks[18], (L, D)),
        "w1_ffn2": nrm(ks[19], (L, D, D_FF), D ** -0.5),
        "w3_ffn2": nrm(ks[20], (L, D, D_FF), D ** -0.5),
        "w2_ffn2": nrm(ks[21], (L, D_FF, D), D_FF ** -0.5),
    }


def _fwd_reference(x, c, w_ada, b_ada, g_ffn1, w1_ffn1, w3_ffn1, w2_ffn1, g_mix, w_in,
              spatial_w, spatial_b, g_v, g_q, g_k, sinks, rel_bias, w_out,
              g_ffn2, w1_ffn2, w3_ffn2, w2_ffn2):
    c_act = jax.nn.silu(c)
    split_pts = [A_WIDTH, 2 * A_WIDTH, 2 * A_WIDTH + B_WIDTH, 2 * A_WIDTH + B_WIDTH + KV_WIDTH]
    for l in range(DEPTH):
        mod = c_act @ w_ada[l] + b_ada[l]
        sh1, sc1, gt1, sh2, sc2, gt2, sh3, sc3, gt3 = jnp.split(mod, N_MOD, axis=-1)

        h = modulate(rms_norm(x, g_ffn1[l]), sh1, sc1)
        x = x + 0.5 * gt1[:, None, :] * swiglu(h, w1_ffn1[l], w3_ffn1[l], w2_ffn1[l])

        h = modulate(rms_norm(x, g_mix[l]), sh2, sc2)
        z = h @ w_in[l]
        za_u, za_v, zq, zk, zv = jnp.split(z, split_pts, axis=-1)
        ya = gmlp_mixer(jax.nn.gelu(za_u, approximate=False), jax.nn.gelu(za_v, approximate=False),
                        spatial_w[l], spatial_b[l], g_v[l])
        yb = swa_mixer(zq, zk, zv, g_q[l], g_k[l], sinks[l], rel_bias)
        y = jnp.concatenate([ya, yb], axis=-1) @ w_out[l]
        x = x + gt2[:, None, :] * y

        h = modulate(rms_norm(x, g_ffn2[l]), sh3, sc3)
        x = x + 0.5 * gt3[:, None, :] * swiglu(h, w1_ffn2[l], w3_ffn2[l], w2_ffn2[l])
    return x


import jax as _jax
import jax.numpy as _jnp

TWIN_FORMAT = 'train_step'
FWD_PARAMS = ['x', 'c', 'w_ada', 'b_ada', 'g_ffn1', 'w1_ffn1', 'w3_ffn1', 'w2_ffn1', 'g_mix', 'w_in', 'spatial_w', 'spatial_b', 'g_v', 'g_q', 'g_k', 'sinks', 'rel_bias', 'w_out', 'g_ffn2', 'w1_ffn2', 'w3_ffn2', 'w2_ffn2']
TWIN_WEIGHTS = ['w_ada', 'b_ada', 'g_ffn1', 'w1_ffn1', 'w3_ffn1', 'w2_ffn1', 'g_mix', 'w_in', 'spatial_w', 'spatial_b', 'g_v', 'g_q', 'g_k', 'sinks', 'rel_bias', 'w_out', 'g_ffn2', 'w1_ffn2', 'w3_ffn2', 'w2_ffn2']
TWIN_DIFF_INPUT = 'x'
TWIN_INPUTS = ['x', 'c', 'w_ada', 'b_ada', 'g_ffn1', 'w1_ffn1', 'w3_ffn1', 'w2_ffn1', 'g_mix', 'w_in', 'spatial_w', 'spatial_b', 'g_v', 'g_q', 'g_k', 'sinks', 'rel_bias', 'w_out', 'g_ffn2', 'w1_ffn2', 'w3_ffn2', 'w2_ffn2', 'loss_target', 'm_w_ada', 'm_b_ada', 'm_g_ffn1', 'm_w1_ffn1', 'm_w3_ffn1', 'm_w2_ffn1', 'm_g_mix', 'm_w_in', 'm_spatial_w', 'm_spatial_b', 'm_g_v', 'm_g_q', 'm_g_k', 'm_sinks', 'm_rel_bias', 'm_w_out', 'm_g_ffn2', 'm_w1_ffn2', 'm_w3_ffn2', 'm_w2_ffn2', 'v_w_ada', 'v_b_ada', 'v_g_ffn1', 'v_w1_ffn1', 'v_w3_ffn1', 'v_w2_ffn1', 'v_g_mix', 'v_w_in', 'v_spatial_w', 'v_spatial_b', 'v_g_v', 'v_g_q', 'v_g_k', 'v_sinks', 'v_rel_bias', 'v_w_out', 'v_g_ffn2', 'v_w1_ffn2', 'v_w3_ffn2', 'v_w2_ffn2']
TWIN_OUTPUTS = ['loss', 'grad_x', 'grad_w_ada', 'grad_b_ada', 'grad_g_ffn1', 'grad_w1_ffn1', 'grad_w3_ffn1', 'grad_w2_ffn1', 'grad_g_mix', 'grad_w_in', 'grad_spatial_w', 'grad_spatial_b', 'grad_g_v', 'grad_g_q', 'grad_g_k', 'grad_sinks', 'grad_rel_bias', 'grad_w_out', 'grad_g_ffn2', 'grad_w1_ffn2', 'grad_w3_ffn2', 'grad_w2_ffn2', 'delta_w_ada', 'delta_b_ada', 'delta_g_ffn1', 'delta_w1_ffn1', 'delta_w3_ffn1', 'delta_w2_ffn1', 'delta_g_mix', 'delta_w_in', 'delta_spatial_w', 'delta_spatial_b', 'delta_g_v', 'delta_g_q', 'delta_g_k', 'delta_sinks', 'delta_rel_bias', 'delta_w_out', 'delta_g_ffn2', 'delta_w1_ffn2', 'delta_w3_ffn2', 'delta_w2_ffn2', 'new_m_w_ada', 'new_m_b_ada', 'new_m_g_ffn1', 'new_m_w1_ffn1', 'new_m_w3_ffn1', 'new_m_w2_ffn1', 'new_m_g_mix', 'new_m_w_in', 'new_m_spatial_w', 'new_m_spatial_b', 'new_m_g_v', 'new_m_g_q', 'new_m_g_k', 'new_m_sinks', 'new_m_rel_bias', 'new_m_w_out', 'new_m_g_ffn2', 'new_m_w1_ffn2', 'new_m_w3_ffn2', 'new_m_w2_ffn2', 'new_v_w_ada', 'new_v_b_ada', 'new_v_g_ffn1', 'new_v_w1_ffn1', 'new_v_w3_ffn1', 'new_v_w2_ffn1', 'new_v_g_mix', 'new_v_w_in', 'new_v_spatial_w', 'new_v_spatial_b', 'new_v_g_v', 'new_v_g_q', 'new_v_g_k', 'new_v_sinks', 'new_v_rel_bias', 'new_v_w_out', 'new_v_g_ffn2', 'new_v_w1_ffn2', 'new_v_w3_ffn2', 'new_v_w2_ffn2']
TWIN_LEAF_KINDS = {'loss': 'loss', 'grad_x': 'grad_x', 'grad_w_ada': 'grad_w', 'grad_b_ada': 'grad_w', 'grad_g_ffn1': 'grad_w', 'grad_w1_ffn1': 'grad_w', 'grad_w3_ffn1': 'grad_w', 'grad_w2_ffn1': 'grad_w', 'grad_g_mix': 'grad_w', 'grad_w_in': 'grad_w', 'grad_spatial_w': 'grad_w', 'grad_spatial_b': 'grad_w', 'grad_g_v': 'grad_w', 'grad_g_q': 'grad_w', 'grad_g_k': 'grad_w', 'grad_sinks': 'grad_w', 'grad_rel_bias': 'grad_w', 'grad_w_out': 'grad_w', 'grad_g_ffn2': 'grad_w', 'grad_w1_ffn2': 'grad_w', 'grad_w3_ffn2': 'grad_w', 'grad_w2_ffn2': 'grad_w', 'delta_w_ada': 'delta_w', 'delta_b_ada': 'delta_w', 'delta_g_ffn1': 'delta_w', 'delta_w1_ffn1': 'delta_w', 'delta_w3_ffn1': 'delta_w', 'delta_w2_ffn1': 'delta_w', 'delta_g_mix': 'delta_w', 'delta_w_in': 'delta_w', 'delta_spatial_w': 'delta_w', 'delta_spatial_b': 'delta_w', 'delta_g_v': 'delta_w', 'delta_g_q': 'delta_w', 'delta_g_k': 'delta_w', 'delta_sinks': 'delta_w', 'delta_rel_bias': 'delta_w', 'delta_w_out': 'delta_w', 'delta_g_ffn2': 'delta_w', 'delta_w1_ffn2': 'delta_w', 'delta_w3_ffn2': 'delta_w', 'delta_w2_ffn2': 'delta_w', 'new_m_w_ada': 'new_m', 'new_m_b_ada': 'new_m', 'new_m_g_ffn1': 'new_m', 'new_m_w1_ffn1': 'new_m', 'new_m_w3_ffn1': 'new_m', 'new_m_w2_ffn1': 'new_m', 'new_m_g_mix': 'new_m', 'new_m_w_in': 'new_m', 'new_m_spatial_w': 'new_m', 'new_m_spatial_b': 'new_m', 'new_m_g_v': 'new_m', 'new_m_g_q': 'new_m', 'new_m_g_k': 'new_m', 'new_m_sinks': 'new_m', 'new_m_rel_bias': 'new_m', 'new_m_w_out': 'new_m', 'new_m_g_ffn2': 'new_m', 'new_m_w1_ffn2': 'new_m', 'new_m_w3_ffn2': 'new_m', 'new_m_w2_ffn2': 'new_m', 'new_v_w_ada': 'new_v', 'new_v_b_ada': 'new_v', 'new_v_g_ffn1': 'new_v', 'new_v_w1_ffn1': 'new_v', 'new_v_w3_ffn1': 'new_v', 'new_v_w2_ffn1': 'new_v', 'new_v_g_mix': 'new_v', 'new_v_w_in': 'new_v', 'new_v_spatial_w': 'new_v', 'new_v_spatial_b': 'new_v', 'new_v_g_v': 'new_v', 'new_v_g_q': 'new_v', 'new_v_g_k': 'new_v', 'new_v_sinks': 'new_v', 'new_v_rel_bias': 'new_v', 'new_v_w_out': 'new_v', 'new_v_g_ffn2': 'new_v', 'new_v_w1_ffn2': 'new_v', 'new_v_w3_ffn2': 'new_v', 'new_v_w2_ffn2': 'new_v'}


def _forward(args):
    return _fwd_reference(*[args[k] for k in FWD_PARAMS])


def _output_shape():
    def fwd():
        inp = _fwd_setup_inputs(0)
        return _fwd_reference(*[inp[k] for k in FWD_PARAMS])
    out = _jax.eval_shape(fwd)
    return out.shape, out.dtype

N_MICROBATCH = 1
ADAM_LR = 0.001
ADAM_B1 = 0.9
ADAM_B2 = 0.999
ADAM_EPS = 1e-08
ADAM_WD = 0.01
ADAM_STEP = 10
PER_EXAMPLE_BATCH_AXIS = {'x': 0, 'c': 0, 'loss_target': 0}
SHARED_INPUTS = []
_WEIGHT_DTYPES = {'w_ada': _jnp.float32, 'b_ada': _jnp.float32, 'g_ffn1': _jnp.float32, 'w1_ffn1': _jnp.float32, 'w3_ffn1': _jnp.float32, 'w2_ffn1': _jnp.float32, 'g_mix': _jnp.float32, 'w_in': _jnp.float32, 'spatial_w': _jnp.float32, 'spatial_b': _jnp.float32, 'g_v': _jnp.float32, 'g_q': _jnp.float32, 'g_k': _jnp.float32, 'sinks': _jnp.float32, 'rel_bias': _jnp.float32, 'w_out': _jnp.float32, 'g_ffn2': _jnp.float32, 'w1_ffn2': _jnp.float32, 'w3_ffn2': _jnp.float32, 'w2_ffn2': _jnp.float32}
MOMENT_SCALE = {'w_ada': 3.975769e-01, 'b_ada': 1.001154e+00, 'g_ffn1': 7.592217e-01, 'w1_ffn1': 1.595280e-02, 'w3_ffn1': 1.493516e-02, 'w2_ffn1': 2.431761e-02, 'g_mix': 1.189364e+00, 'w_in': 1.891888e-01, 'spatial_w': 4.906043e-01, 'spatial_b': 1.518038e+00, 'g_v': 7.600923e-01, 'g_q': 6.586590e-01, 'g_k': 6.616420e-01, 'sinks': 1.214298e-01, 'rel_bias': 3.350184e-02, 'w_out': 1.994265e-01, 'g_ffn2': 7.398677e-01, 'w1_ffn2': 1.558513e-02, 'w3_ffn2': 1.417988e-02, 'w2_ffn2': 2.306372e-02}


def _to_microbatches(a, axis):
    t = _jnp.moveaxis(a, axis, 0)
    t = t.reshape((N_MICROBATCH, t.shape[0] // N_MICROBATCH) + t.shape[1:])
    return _jnp.moveaxis(t, 1, axis + 1)


def setup_inputs(seed: int = 0) -> dict:
    inp = _fwd_setup_inputs(seed)
    key = _jax.random.fold_in(_jax.random.key(seed), 7919)
    shape, _ = _output_shape()
    out = dict(inp)
    out["loss_target"] = _jax.random.normal(_jax.random.fold_in(key, 0), shape, _jnp.float32)
    for i, name in enumerate(TWIN_WEIGHTS):
        w = inp[name].astype(_jnp.float32)
        if MOMENT_SCALE is None:
            s = _jnp.sqrt(_jnp.mean(_jnp.square(w)) + 1e-30)
        else:
            s = MOMENT_SCALE[name]
        km, kv = _jax.random.split(_jax.random.fold_in(key, i + 1))
        out[name] = w
        out["m_" + name] = s * _jax.random.normal(km, w.shape, _jnp.float32)
        out["v_" + name] = (s * s) * _jax.random.uniform(kv, w.shape, _jnp.float32, 0.5, 1.5)
    if N_MICROBATCH > 1:
        for name, axis in PER_EXAMPLE_BATCH_AXIS.items():
            out[name] = _to_microbatches(out[name], axis)
    return {'x': out['x'], 'c': out['c'], 'w_ada': out['w_ada'], 'b_ada': out['b_ada'], 'g_ffn1': out['g_ffn1'], 'w1_ffn1': out['w1_ffn1'], 'w3_ffn1': out['w3_ffn1'], 'w2_ffn1': out['w2_ffn1'], 'g_mix': out['g_mix'], 'w_in': out['w_in'], 'spatial_w': out['spatial_w'], 'spatial_b': out['spatial_b'], 'g_v': out['g_v'], 'g_q': out['g_q'], 'g_k': out['g_k'], 'sinks': out['sinks'], 'rel_bias': out['rel_bias'], 'w_out': out['w_out'], 'g_ffn2': out['g_ffn2'], 'w1_ffn2': out['w1_ffn2'], 'w3_ffn2': out['w3_ffn2'], 'w2_ffn2': out['w2_ffn2'], 'loss_target': out['loss_target'], 'm_w_ada': out['m_w_ada'], 'm_b_ada': out['m_b_ada'], 'm_g_ffn1': out['m_g_ffn1'], 'm_w1_ffn1': out['m_w1_ffn1'], 'm_w3_ffn1': out['m_w3_ffn1'], 'm_w2_ffn1': out['m_w2_ffn1'], 'm_g_mix': out['m_g_mix'], 'm_w_in': out['m_w_in'], 'm_spatial_w': out['m_spatial_w'], 'm_spatial_b': out['m_spatial_b'], 'm_g_v': out['m_g_v'], 'm_g_q': out['m_g_q'], 'm_g_k': out['m_g_k'], 'm_sinks': out['m_sinks'], 'm_rel_bias': out['m_rel_bias'], 'm_w_out': out['m_w_out'], 'm_g_ffn2': out['m_g_ffn2'], 'm_w1_ffn2': out['m_w1_ffn2'], 'm_w3_ffn2': out['m_w3_ffn2'], 'm_w2_ffn2': out['m_w2_ffn2'], 'v_w_ada': out['v_w_ada'], 'v_b_ada': out['v_b_ada'], 'v_g_ffn1': out['v_g_ffn1'], 'v_w1_ffn1': out['v_w1_ffn1'], 'v_w3_ffn1': out['v_w3_ffn1'], 'v_w2_ffn1': out['v_w2_ffn1'], 'v_g_mix': out['v_g_mix'], 'v_w_in': out['v_w_in'], 'v_spatial_w': out['v_spatial_w'], 'v_spatial_b': out['v_spatial_b'], 'v_g_v': out['v_g_v'], 'v_g_q': out['v_g_q'], 'v_g_k': out['v_g_k'], 'v_sinks': out['v_sinks'], 'v_rel_bias': out['v_rel_bias'], 'v_w_out': out['v_w_out'], 'v_g_ffn2': out['v_g_ffn2'], 'v_w1_ffn2': out['v_w1_ffn2'], 'v_w3_ffn2': out['v_w3_ffn2'], 'v_w2_ffn2': out['v_w2_ffn2']}


def _loss(weights, diff, rest, loss_target):
    with _jax.named_scope("forward"):
        args = {**rest, TWIN_DIFF_INPUT: diff, **{k: w.astype(_WEIGHT_DTYPES[k]) for k, w in weights.items()}}
        y = _forward(args)
    with _jax.named_scope("loss_head"):
        err = _jnp.square(y.astype(_jnp.float32) - loss_target)
        return 0.5 * _jnp.sum(_jnp.mean(err, axis=-1)) if err.ndim else 0.5 * err


def _adamw(w, g, m, v):
    m = ADAM_B1 * m + (1.0 - ADAM_B1) * g
    v = ADAM_B2 * v + (1.0 - ADAM_B2) * _jnp.square(g)
    m_hat = m / (1.0 - ADAM_B1 ** ADAM_STEP)
    v_hat = v / (1.0 - ADAM_B2 ** ADAM_STEP)
    delta = -ADAM_LR * (m_hat / (_jnp.sqrt(v_hat) + ADAM_EPS) + ADAM_WD * w)
    return delta, m, v


def reference(x, c, w_ada, b_ada, g_ffn1, w1_ffn1, w3_ffn1, w2_ffn1, g_mix, w_in, spatial_w, spatial_b, g_v, g_q, g_k, sinks, rel_bias, w_out, g_ffn2, w1_ffn2, w3_ffn2, w2_ffn2, loss_target, m_w_ada, m_b_ada, m_g_ffn1, m_w1_ffn1, m_w3_ffn1, m_w2_ffn1, m_g_mix, m_w_in, m_spatial_w, m_spatial_b, m_g_v, m_g_q, m_g_k, m_sinks, m_rel_bias, m_w_out, m_g_ffn2, m_w1_ffn2, m_w3_ffn2, m_w2_ffn2, v_w_ada, v_b_ada, v_g_ffn1, v_w1_ffn1, v_w3_ffn1, v_w2_ffn1, v_g_mix, v_w_in, v_spatial_w, v_spatial_b, v_g_v, v_g_q, v_g_k, v_sinks, v_rel_bias, v_w_out, v_g_ffn2, v_w1_ffn2, v_w3_ffn2, v_w2_ffn2):
    given = dict(x=x, c=c, w_ada=w_ada, b_ada=b_ada, g_ffn1=g_ffn1, w1_ffn1=w1_ffn1, w3_ffn1=w3_ffn1, w2_ffn1=w2_ffn1, g_mix=g_mix, w_in=w_in, spatial_w=spatial_w, spatial_b=spatial_b, g_v=g_v, g_q=g_q, g_k=g_k, sinks=sinks, rel_bias=rel_bias, w_out=w_out, g_ffn2=g_ffn2, w1_ffn2=w1_ffn2, w3_ffn2=w3_ffn2, w2_ffn2=w2_ffn2, loss_target=loss_target, m_w_ada=m_w_ada, m_b_ada=m_b_ada, m_g_ffn1=m_g_ffn1, m_w1_ffn1=m_w1_ffn1, m_w3_ffn1=m_w3_ffn1, m_w2_ffn1=m_w2_ffn1, m_g_mix=m_g_mix, m_w_in=m_w_in, m_spatial_w=m_spatial_w, m_spatial_b=m_spatial_b, m_g_v=m_g_v, m_g_q=m_g_q, m_g_k=m_g_k, m_sinks=m_sinks, m_rel_bias=m_rel_bias, m_w_out=m_w_out, m_g_ffn2=m_g_ffn2, m_w1_ffn2=m_w1_ffn2, m_w3_ffn2=m_w3_ffn2, m_w2_ffn2=m_w2_ffn2, v_w_ada=v_w_ada, v_b_ada=v_b_ada, v_g_ffn1=v_g_ffn1, v_w1_ffn1=v_w1_ffn1, v_w3_ffn1=v_w3_ffn1, v_w2_ffn1=v_w2_ffn1, v_g_mix=v_g_mix, v_w_in=v_w_in, v_spatial_w=v_spatial_w, v_spatial_b=v_spatial_b, v_g_v=v_g_v, v_g_q=v_g_q, v_g_k=v_g_k, v_sinks=v_sinks, v_rel_bias=v_rel_bias, v_w_out=v_w_out, v_g_ffn2=v_g_ffn2, v_w1_ffn2=v_w1_ffn2, v_w3_ffn2=v_w3_ffn2, v_w2_ffn2=v_w2_ffn2)
    weights = {n: given[n] for n in TWIN_WEIGHTS}
    shared = {n: given[n] for n in SHARED_INPUTS}
    per_example = {n: given[n] for n in ['x', 'c']}
    grad_fn = _jax.value_and_grad(_loss, argnums=(0, 1))

    def one_microbatch(ex, loss_target):
        ex = dict(ex)
        diff = ex.pop(TWIN_DIFF_INPUT)
        return grad_fn(weights, diff, {**shared, **ex}, loss_target)

    if N_MICROBATCH == 1:
        loss, (grad_w, grad_x) = one_microbatch(per_example, given["loss_target"])
    else:
        def body(carry, xs):
            loss_sum, grad_sum = carry
            l_k, (gw_k, gx_k) = one_microbatch(xs[0], xs[1])
            with _jax.named_scope("update"):
                return (loss_sum + l_k, _jax.tree.map(_jnp.add, grad_sum, gw_k)), gx_k

        init = (_jnp.zeros((), _jnp.float32), _jax.tree.map(_jnp.zeros_like, weights))
        (loss, grad_w), grad_x = _jax.lax.scan(body, init, (per_example, given["loss_target"]))
    with _jax.named_scope("update"):
        delta_w, new_m, new_v = {}, {}, {}
        for n in TWIN_WEIGHTS:
            delta_w[n], new_m[n], new_v[n] = _adamw(weights[n], grad_w[n], given["m_" + n], given["v_" + n])
    return (loss, grad_x, *[grad_w[n] for n in TWIN_WEIGHTS], *[delta_w[n] for n in TWIN_WEIGHTS],
            *[new_m[n] for n in TWIN_WEIGHTS], *[new_v[n] for n in TWIN_WEIGHTS])
```

```python
import functools
import math

import numpy as np
import jax
import jax.numpy as jnp
from jax import lax
from jax.experimental import pallas as pl
from jax.experimental.pallas import tpu as pltpu

F32 = jnp.float32
BF16 = jnp.bfloat16

D_MODEL = 2048
D_FF = 5632
BLOCK = 128
A_HEADS = 8
A_WIDTH = 1024
B_HEADS = 16
B_KV_HEADS = 2
GROUP = B_HEADS // B_KV_HEADS
HEAD_DIM = 64
B_WIDTH = 1024
KV_WIDTH = 128
IN_COLS = 3328
Q_OFF = 2 * A_WIDTH
K_OFF = Q_OFF + B_WIDTH
V_OFF = K_OFF + KV_WIDTH
N_BUCKETS = 32
N_MOD = 9
EPS = 1e-6
N_DEV = 8
MOD_COLS = N_MOD * D_MODEL // N_DEV

ADAM_LR = 0.001
ADAM_B1 = 0.9
ADAM_B2 = 0.999
ADAM_EPS = 1e-08
ADAM_WD = 0.01
ADAM_STEP = 10

MASK_VALUE = -1e30
VMEM_LIMIT = 56 * 1024 * 1024
MESH_ID = pl.DeviceIdType.MESH
ANY = pl.BlockSpec(memory_space=pl.ANY)

_SQRT_HALF = 0.7071067811865476
_INV_SQRT_2PI = 0.3989422804014327


def _pc(body, **kw):
    return pl.pallas_call(body, **kw)


def _params(sem=None):
    return pltpu.CompilerParams(dimension_semantics=sem, vmem_limit_bytes=VMEM_LIMIT)


def _dot(a, b):
    return lax.dot_general(a, b, (((1,), (0,)), ((), ())), preferred_element_type=F32)


def _dot_nt(a, b):
    return lax.dot_general(a, b, (((1,), (1,)), ((), ())), preferred_element_type=F32)


def _dot_tn(a, b):
    return lax.dot_general(a, b, (((0,), (0,)), ((), ())), preferred_element_type=F32)


def _gelu(x):
    return 0.5 * x * (1.0 + lax.erf(x * _SQRT_HALF))


def _gelu_grad(x):
    return 0.5 * (1.0 + lax.erf(x * _SQRT_HALF)) + x * jnp.exp(-0.5 * x * x) * _INV_SQRT_2PI


def _rms(x):
    r = lax.rsqrt(jnp.mean(x * x, axis=-1, keepdims=True) + EPS)
    return x * r, r


def _rms_bwd(dy, y, r):
    return r * (dy - y * jnp.mean(dy * y, axis=-1, keepdims=True))


def _row_spec(tm, cols):
    return pl.BlockSpec((tm, cols), lambda i: (i, 0))


def _vec_spec(cols):
    return pl.BlockSpec((1, cols), lambda i: (0, 0))


def _position():
    x, y, c = lax.axis_index("x"), lax.axis_index("y"), lax.axis_index("c")
    return x, y, c


def _linear(p):
    return 4 * p[0] + 2 * p[1] + p[2]


def _all_gather_two_level(arrs, name):
    n = len(arrs)

    def body(*refs):
        srcs, dsts = refs[:n], refs[n:2 * n]
        send_sems, recv_sems, local_sems = refs[2 * n:]
        x, y, c = _position()
        me, sibling = (x, y, c), (x, y, 1 - c)
        chips = [(1 - x, y), (x, 1 - y), (1 - x, 1 - y)]

        def copy(a, k, block, to, src=None):
            rows = dsts[a].at[_linear(block)]
            return pltpu.make_async_remote_copy(
                src_ref=rows if src is None else src, dst_ref=rows,
                send_sem=send_sems.at[7 * a + k], recv_sem=recv_sems.at[7 * a + k],
                device_id=to, device_id_type=MESH_ID)

        local, sent = [], []
        for a in range(n):
            mine = pltpu.make_async_copy(srcs[a], dsts[a].at[_linear(me)], local_sems.at[a])
            mine.start()
            local.append(mine)
            first = [copy(a, 0, me, sibling, srcs[a])]
            first += [copy(a, 1 + j, me, (*chip, c), srcs[a]) for j, chip in enumerate(chips)]
            for cp in first:
                cp.start()
            sent += first
        for a in range(n):
            for j, chip in enumerate(chips):
                copy(a, 1 + j, (*chip, c), me).wait_recv()
                passed = copy(a, 4 + j, (*chip, c), sibling)
                passed.start()
                sent.append(passed)
        for a in range(n):
            copy(a, 0, sibling, me).wait_recv()
            for j, chip in enumerate(chips):
                copy(a, 4 + j, (*chip, 1 - c), me).wait_recv()
        for cp in sent:
            cp.wait_send()
        for cp in local:
            cp.wait()

    return _pc(
        body, name=name,
        out_shape=[jax.ShapeDtypeStruct((N_DEV,) + a.shape, a.dtype) for a in arrs],
        in_specs=[ANY] * n, out_specs=[ANY] * n,
        scratch_shapes=[pltpu.SemaphoreType.DMA((7 * n,)), pltpu.SemaphoreType.DMA((7 * n,)),
                        pltpu.SemaphoreType.DMA((n,))],
    )(*arrs)


def _exchange_direct(arrs, name, broadcast):
    n = len(arrs)

    def body(*refs):
        srcs, dsts = refs[:n], refs[n:2 * n]
        send_sems, recv_sems, local_sems = refs[2 * n:]
        x, y, c = _position()
        me = (x, y, c)
        peers = []
        for k in range(1, N_DEV):
            peers.append((1 - x if k & 4 else x, 1 - y if k & 2 else y, 1 - c if k & 1 else c))

        def outgoing(a, to):
            return srcs[a] if broadcast else srcs[a].at[_linear(to)]

        def copy(a, k, sender, to):
            return pltpu.make_async_remote_copy(
                src_ref=outgoing(a, to), dst_ref=dsts[a].at[_linear(sender)],
                send_sem=send_sems.at[7 * a + k], recv_sem=recv_sems.at[7 * a + k],
                device_id=to, device_id_type=MESH_ID)

        local, sent = [], []
        for a in range(n):
            cp = pltpu.make_async_copy(outgoing(a, me), dsts[a].at[_linear(me)], local_sems.at[a])
            cp.start()
            local.append(cp)
            for k, peer in enumerate(peers):
                cp = copy(a, k, me, peer)
                cp.start()
                sent.append(cp)
        for a in range(n):
            for k, peer in enumerate(peers):
                copy(a, k, peer, me).wait_recv()
        for cp in sent:
            cp.wait_send()
        for cp in local:
            cp.wait()

    out_shape = [jax.ShapeDtypeStruct(((N_DEV,) + a.shape) if broadcast else a.shape, a.dtype) for a in arrs]
    return _pc(
        body, name=name, out_shape=out_shape,
        in_specs=[ANY] * n, out_specs=[ANY] * n,
        scratch_shapes=[pltpu.SemaphoreType.DMA((7 * n,)), pltpu.SemaphoreType.DMA((7 * n,)),
                        pltpu.SemaphoreType.DMA((n,))],
    )(*arrs)


def _matmul(a, b, mode, tm, tn, tk, out_dtype, name):
    if mode == "nn":
        (m, kk), nn = a.shape, b.shape[1]
        a_spec = pl.BlockSpec((tm, tk), lambda i, j, k: (i, k))
        b_spec = pl.BlockSpec((tk, tn), lambda i, j, k: (k, j))
        dot = _dot
    elif mode == "nt":
        (m, kk), nn = a.shape, b.shape[0]
        a_spec = pl.BlockSpec((tm, tk), lambda i, j, k: (i, k))
        b_spec = pl.BlockSpec((tn, tk), lambda i, j, k: (j, k))
        dot = _dot_nt
    else:
        (kk, m), nn = a.shape, b.shape[1]
        a_spec = pl.BlockSpec((tk, tm), lambda i, j, k: (k, i))
        b_spec = pl.BlockSpec((tk, tn), lambda i, j, k: (k, j))
        dot = _dot_tn
    assert m % tm == 0 and nn % tn == 0 and kk % tk == 0, (a.shape, b.shape, tm, tn, tk)
    nk = kk // tk
    assert nk == 1 or out_dtype == F32

    def body(a_ref, b_ref, o_ref):
        p = dot(a_ref[...], b_ref[...])
        if nk == 1:
            o_ref[...] = p.astype(o_ref.dtype)
        else:
            k = pl.program_id(2)

            @pl.when(k == 0)
            def _():
                o_ref[...] = p

            @pl.when(k > 0)
            def _():
                o_ref[...] += p

    return _pc(
        body, name=name, grid=(m // tm, nn // tn, nk),
        in_specs=[a_spec, b_spec], out_specs=pl.BlockSpec((tm, tn), lambda i, j, k: (i, j)),
        out_shape=jax.ShapeDtypeStruct((m, nn), out_dtype),
        compiler_params=_params(("parallel", "parallel", "arbitrary")),
    )(a, b)


def _norm_fwd(x_prev, f_prev, gate_prev, coef, g, sc, sh, name, tm=256):
    t, d = x_prev.shape
    residual = f_prev is not None

    def body(*refs):
        if residual:
            xp_ref, fp_ref, gp_ref, g_ref, sc_ref, sh_ref, x_ref, h_ref = refs
            x = xp_ref[...] + coef * gp_ref[...] * fp_ref[...]
            x_ref[...] = x
        else:
            xp_ref, g_ref, sc_ref, sh_ref, h_ref = refs
            x = xp_ref[...]
        y, _ = _rms(x)
        h_ref[...] = ((y * g_ref[...]) * (1.0 + sc_ref[...]) + sh_ref[...]).astype(BF16)

    row, vec = _row_spec(tm, d), _vec_spec(d)
    if residual:
        ins, in_specs = (x_prev, f_prev, gate_prev, g, sc, sh), [row, row, vec, vec, vec, vec]
        out_shape = [jax.ShapeDtypeStruct((t, d), F32), jax.ShapeDtypeStruct((t, d), BF16)]
        out_specs = [row, row]
    else:
        ins, in_specs = (x_prev, g, sc, sh), [row, vec, vec, vec]
        out_shape = [jax.ShapeDtypeStruct((t, d), BF16)]
        out_specs = [row]
    out = _pc(body, name=name, grid=(t // tm,), in_specs=in_specs, out_specs=out_specs, out_shape=out_shape,
              compiler_params=_params(("parallel",)))(*ins)
    return out if residual else out[0]


def _norm_bwd(dh, x, dxo, fo, coef_o, g, sc, gate_prev, coef_prev, name, tm=256):
    t, d = x.shape
    with_prev = gate_prev is not None

    def body(*refs):
        if with_prev:
            dh_ref, x_ref, dxo_ref, fo_ref, g_ref, sc_ref, gp_ref, dx_ref, dfp_ref, sums_ref = refs
        else:
            dh_ref, x_ref, dxo_ref, fo_ref, g_ref, sc_ref, dx_ref, sums_ref = refs
        dh_v, dxo_v = dh_ref[...], dxo_ref[...]
        y, r = _rms(x_ref[...])
        n = y * g_ref[...]
        dn = dh_v * (1.0 + sc_ref[...])
        dx = dxo_v + _rms_bwd(dn * g_ref[...], y, r)
        dx_ref[...] = dx
        if with_prev:
            dfp_ref[...] = (coef_prev * gp_ref[...] * dx).astype(BF16)
        @pl.when(pl.program_id(0) == 0)
        def _():
            sums_ref[...] = jnp.zeros_like(sums_ref)

        sums_ref[0:1, :] += jnp.sum(dh_v * n, axis=0, keepdims=True)
        sums_ref[1:2, :] += jnp.sum(dh_v, axis=0, keepdims=True)
        sums_ref[2:3, :] += jnp.sum(dn * y, axis=0, keepdims=True)
        sums_ref[3:4, :] += jnp.sum(coef_o * dxo_v * fo_ref[...], axis=0, keepdims=True)

    row, vec = _row_spec(tm, d), _vec_spec(d)
    sums_spec = pl.BlockSpec((8, d), lambda i: (0, 0))
    ins, in_specs = [dh, x, dxo, fo, g, sc], [row, row, row, row, vec, vec]
    out_shape, out_specs = [jax.ShapeDtypeStruct((t, d), F32)], [row]
    if with_prev:
        ins.append(gate_prev)
        in_specs.append(vec)
        out_shape.append(jax.ShapeDtypeStruct((t, d), BF16))
        out_specs.append(row)
    out_shape.append(jax.ShapeDtypeStruct((8, d), F32))
    out_specs.append(sums_spec)
    return _pc(body, name=name, grid=(t // tm,), in_specs=in_specs, out_specs=out_specs, out_shape=out_shape,
               compiler_params=_params(("arbitrary",)))(*ins)


def _ffn_fwd(h, w1, w3, w2, name, tm=512, tf=512):
    t, d = h.shape
    ff = w1.shape[1]

    def body(h_ref, w1_ref, w3_ref, w2_ref, a_ref, b_ref, f_ref):
        j = pl.program_id(1)
        hv = h_ref[...]
        a = _dot(hv, w1_ref[...])
        b = _dot(hv, w3_ref[...])
        a_ref[...] = a
        b_ref[...] = b
        s = (a * jax.nn.sigmoid(a) * b).astype(BF16)
        p = _dot(s, w2_ref[...])

        @pl.when(j == 0)
        def _():
            f_ref[...] = p

        @pl.when(j > 0)
        def _():
            f_ref[...] += p

    tile = pl.BlockSpec((tm, tf), lambda i, j: (i, j))
    rows = pl.BlockSpec((tm, d), lambda i, j: (i, 0))
    return _pc(
        body, name=name, grid=(t // tm, ff // tf),
        in_specs=[rows, pl.BlockSpec((d, tf), lambda i, j: (0, j)), pl.BlockSpec((d, tf), lambda i, j: (0, j)),
                  pl.BlockSpec((tf, d), lambda i, j: (j, 0))],
        out_specs=[tile, tile, rows],
        out_shape=[jax.ShapeDtypeStruct((t, ff), F32), jax.ShapeDtypeStruct((t, ff), F32),
                   jax.ShapeDtypeStruct((t, d), F32)],
        compiler_params=_params(("parallel", "arbitrary")),
    )(h, w1, w3, w2)


def _ffn_bwd_dx(df, a, b, w1, w3, w2, name, tm=512, tf=512):
    t, d = df.shape
    ff = a.shape[1]

    def body(df_ref, a_ref, b_ref, w1_ref, w3_ref, w2_ref, da_ref, db_ref, s_ref, dh_ref):
        j = pl.program_id(1)
        ds = _dot_nt(df_ref[...], w2_ref[...])
        av, bv = a_ref[...], b_ref[...]
        sg = jax.nn.sigmoid(av)
        sil = av * sg
        s_ref[...] = (sil * bv).astype(BF16)
        da = (ds * bv * (sg * (1.0 + av * (1.0 - sg)))).astype(BF16)
        db = (ds * sil).astype(BF16)
        da_ref[...] = da
        db_ref[...] = db
        p = _dot_nt(da, w1_ref[...]) + _dot_nt(db, w3_ref[...])

        @pl.when(j == 0)
        def _():
            dh_ref[...] = p

        @pl.when(j > 0)
        def _():
            dh_ref[...] += p

    tile = pl.BlockSpec((tm, tf), lambda i, j: (i, j))
    rows = pl.BlockSpec((tm, d), lambda i, j: (i, 0))
    return _pc(
        body, name=name, grid=(t // tm, ff // tf),
        in_specs=[rows, tile, tile, pl.BlockSpec((d, tf), lambda i, j: (0, j)),
                  pl.BlockSpec((d, tf), lambda i, j: (0, j)), pl.BlockSpec((tf, d), lambda i, j: (j, 0))],
        out_specs=[tile, tile, tile, rows],
        out_shape=[jax.ShapeDtypeStruct((t, ff), BF16)] * 3 + [jax.ShapeDtypeStruct((t, d), F32)],
        compiler_params=_params(("parallel", "arbitrary")),
    )(df, a, b, w1, w3, w2)


def _loss_head(x_prev, f_prev, gate_prev, coef, target, name, tm=256):
    t, d = x_prev.shape
    steps = t // tm

    def body(xp_ref, fp_ref, gp_ref, tg_ref, dy_ref, df_ref, loss_ref, acc_ref):
        i = pl.program_id(0)
        e = xp_ref[...] + coef * gp_ref[...] * fp_ref[...] - tg_ref[...]
        dy = e * (1.0 / d)
        dy_ref[...] = dy
        df_ref[...] = (coef * gp_ref[...] * dy).astype(BF16)
        part = jnp.sum(e * e, axis=0, keepdims=True)

        @pl.when(i == 0)
        def _():
            acc_ref[...] = part

        @pl.when(i > 0)
        def _():
            acc_ref[...] += part

        @pl.when(i == steps - 1)
        def _():
            loss_ref[...] = jnp.sum(acc_ref[...], axis=1, keepdims=True) * (0.5 / d)

    row, vec = _row_spec(tm, d), _vec_spec(d)
    return _pc(
        body, name=name, grid=(steps,),
        in_specs=[row, row, vec, row], out_specs=[row, row, pl.BlockSpec((1, 1), lambda i: (0, 0))],
        out_shape=[jax.ShapeDtypeStruct((t, d), F32), jax.ShapeDtypeStruct((t, d), BF16),
                   jax.ShapeDtypeStruct((1, 1), F32)],
        scratch_shapes=[pltpu.VMEM((1, d), F32)],
        compiler_params=_params(("arbitrary",)),
    )(x_prev, f_prev, gate_prev, target)


def _split3(x):
    hi = x.astype(BF16)
    r1 = x - hi.astype(F32)
    mid = r1.astype(BF16)
    lo = (r1 - mid.astype(F32)).astype(BF16)
    return hi, mid, lo


def _select_matmul(a, onehot, name):
    m, n = a.shape[0], onehot.shape[1]

    def body(a_ref, oh_ref, o_ref):
        hi, mid, lo = _split3(a_ref[...])
        oh = oh_ref[...]
        o_ref[...] = (_dot(hi, oh) + _dot(mid, oh)) + _dot(lo, oh)

    return _pc(body, name=name, out_shape=jax.ShapeDtypeStruct((m, n), F32), compiler_params=_params())(a, onehot)


def _bucket_onehot():
    qi = np.arange(BLOCK)[:, None]
    kj = np.arange(2 * BLOCK)[None, :]
    dist = np.clip(qi + BLOCK - kj, 0, None)
    nf = np.maximum(dist, 1).astype(np.float32)
    large = 16 + (np.log(nf / np.float32(16)) / np.float32(math.log(128 / 16)) * np.float32(16)).astype(np.int32)
    bucket = np.where(dist < 16, dist, np.minimum(large, N_BUCKETS - 1)).reshape(-1)
    return (bucket[None, :] == np.arange(N_BUCKETS)[:, None]).astype(np.float32)


def _window_mask(n):
    qi = lax.broadcasted_iota(jnp.int32, (BLOCK, 2 * BLOCK), 0)
    kj = lax.broadcasted_iota(jnp.int32, (BLOCK, 2 * BLOCK), 1)
    return (kj > qi) & (kj <= qi + BLOCK) & ((kj >= BLOCK) | (n > 0))


def _mix_specs(nb, order):
    cur = pl.BlockSpec((BLOCK, IN_COLS), lambda n: (order(n), 0))
    prev = pl.BlockSpec((BLOCK, 2 * KV_WIDTH), lambda n: (jnp.maximum(order(n) - 1, 0), K_OFF // (2 * KV_WIDTH)))
    full = lambda shape: pl.BlockSpec(shape, lambda n: (0,) * len(shape))
    params = [full((A_HEADS, BLOCK, BLOCK)), full((BLOCK, A_HEADS)), full((A_HEADS, BLOCK)),
              full((1, HEAD_DIM)), full((1, HEAD_DIM)), pl.BlockSpec(memory_space=pltpu.SMEM),
              full((B_HEADS, BLOCK, 2 * BLOCK))]
    return cur, prev, params, full


def _mix_fwd(z, sw, sb_t, gv, gq, gk, sinks, bias, name):
    t = z.shape[0]
    nb = t // BLOCK

    def body(zc_ref, zp_ref, sw_ref, sbt_ref, gv_ref, gq_ref, gk_ref, sink_ref, bias_ref, y_ref):
        n = pl.program_id(0)
        ri = lax.broadcasted_iota(jnp.int32, (BLOCK, BLOCK), 0)
        ci = lax.broadcasted_iota(jnp.int32, (BLOCK, BLOCK), 1)
        tril = ri >= ci
        for h in range(A_HEADS):
            u = _gelu(zc_ref[:, h * BLOCK:(h + 1) * BLOCK])
            vv = _gelu(zc_ref[:, A_WIDTH + h * BLOCK:A_WIDTH + (h + 1) * BLOCK])
            vhat, _ = _rms(vv)
            vn = (vhat * gv_ref[h:h + 1, :]).astype(BF16)
            w = jnp.where(tril, sw_ref[h], 0.0).astype(BF16)
            mixed = _dot(w, vn) + sbt_ref[:, h:h + 1]
            y_ref[:, h * BLOCK:(h + 1) * BLOCK] = (u * mixed).astype(BF16)
        valid = _window_mask(n)
        for kvh in range(B_KV_HEADS):
            ko, vo = K_OFF + kvh * HEAD_DIM, V_OFF + kvh * HEAD_DIM
            k_raw = jnp.concatenate([zp_ref[:, kvh * HEAD_DIM:(kvh + 1) * HEAD_DIM],
                                     zc_ref[:, ko:ko + HEAD_DIM]], axis=0)
            v_raw = jnp.concatenate([zp_ref[:, KV_WIDTH + kvh * HEAD_DIM:KV_WIDTH + (kvh + 1) * HEAD_DIM],
                                     zc_ref[:, vo:vo + HEAD_DIM]], axis=0)
            khat, _ = _rms(k_raw)
            kband = (khat * gk_ref[...]).astype(BF16)
            vband = v_raw.astype(BF16)
            for g in range(GROUP):
                hd = kvh * GROUP + g
                qo = Q_OFF + hd * HEAD_DIM
                qhat, _ = _rms(zc_ref[:, qo:qo + HEAD_DIM])
                qn = (qhat * gq_ref[...]).astype(BF16)
                s = _dot_nt(qn, kband) * (HEAD_DIM ** -0.5) + bias_ref[hd]
                s = jnp.where(valid, s, MASK_VALUE)
                sink = sink_ref[0, hd]
                m = jnp.maximum(jnp.max(s, axis=-1, keepdims=True), sink)
                p = jnp.exp(s - m)
                den = jnp.sum(p, axis=-1, keepdims=True) + jnp.exp(sink - m)
                o = _dot((p / den).astype(BF16), vband)
                yo = A_WIDTH + hd * HEAD_DIM
                y_ref[:, yo:yo + HEAD_DIM] = o.astype(BF16)

    cur, prev, params, _ = _mix_specs(nb, lambda n: n)
    return _pc(
        body, name=name, grid=(nb,), in_specs=[cur, prev] + params,
        out_specs=pl.BlockSpec((BLOCK, A_WIDTH + B_WIDTH), lambda n: (n, 0)),
        out_shape=jax.ShapeDtypeStruct((t, A_WIDTH + B_WIDTH), BF16),
        compiler_params=_params(("arbitrary",)),
    )(z, z, sw, sb_t, gv, gq, gk, sinks, bias)


def _mix_bwd(z, dy, sw, sb_t, gv, gq, gk, sinks, bias, name):
    t = z.shape[0]
    nb = t // BLOCK

    def body(zc_ref, zp_ref, dy_ref, sw_ref, sbt_ref, gv_ref, gq_ref, gk_ref, sink_ref, bias_ref,
             dz_ref, dsw_ref, dsb_ref, dgv_ref, dgq_ref, dgk_ref, dsink_ref, dbias_ref, carry_ref):
        step = pl.program_id(0)
        n = nb - 1 - step

        @pl.when(step == 0)
        def _():
            dsw_ref[...] = jnp.zeros_like(dsw_ref)
            dsb_ref[...] = jnp.zeros_like(dsb_ref)
            dgv_ref[...] = jnp.zeros_like(dgv_ref)
            dgq_ref[...] = jnp.zeros_like(dgq_ref)
            dgk_ref[...] = jnp.zeros_like(dgk_ref)
            dsink_ref[...] = jnp.zeros_like(dsink_ref)
            dbias_ref[...] = jnp.zeros_like(dbias_ref)
            carry_ref[...] = jnp.zeros_like(carry_ref)

        ri = lax.broadcasted_iota(jnp.int32, (BLOCK, BLOCK), 0)
        ci = lax.broadcasted_iota(jnp.int32, (BLOCK, BLOCK), 1)
        tril = ri >= ci
        dsb = jnp.zeros((BLOCK, BLOCK), F32)
        for h in range(A_HEADS):
            uo, vo = h * BLOCK, A_WIDTH + h * BLOCK
            u_raw, v_raw = zc_ref[:, uo:uo + BLOCK], zc_ref[:, vo:vo + BLOCK]
            u, vv = _gelu(u_raw), _gelu(v_raw)
            vhat, r = _rms(vv)
            vn = (vhat * gv_ref[h:h + 1, :]).astype(BF16)
            w = jnp.where(tril, sw_ref[h], 0.0).astype(BF16)
            mixed = _dot(w, vn) + sbt_ref[:, h:h + 1]
            dya = dy_ref[:, uo:uo + BLOCK]
            dmixed = dya * u
            dm16 = dmixed.astype(BF16)
            dsw_ref[h] += jnp.where(tril, _dot_nt(dm16, vn), 0.0)
            dsb = dsb + jnp.where(ci == h, jnp.sum(dmixed, axis=1, keepdims=True), 0.0)
            dvn = _dot_tn(w, dm16)
            dgv_ref[h:h + 1, :] += jnp.sum(dvn * vhat, axis=0, keepdims=True)
            dvv = _rms_bwd(dvn * gv_ref[h:h + 1, :], vhat, r)
            dz_ref[:, uo:uo + BLOCK] = (dya * mixed * _gelu_grad(u_raw)).astype(BF16)
            dz_ref[:, vo:vo + BLOCK] = (dvv * _gelu_grad(v_raw)).astype(BF16)
        dsb_ref[...] += dsb

        valid = _window_mask(n)
        lane = lax.broadcasted_iota(jnp.int32, (BLOCK, BLOCK), 1)
        dsink = jnp.zeros((BLOCK, BLOCK), F32)
        dgq = jnp.zeros((1, HEAD_DIM), F32)
        for kvh in range(B_KV_HEADS):
            ko, vo = K_OFF + kvh * HEAD_DIM, V_OFF + kvh * HEAD_DIM
            k_raw = jnp.concatenate([zp_ref[:, kvh * HEAD_DIM:(kvh + 1) * HEAD_DIM],
                                     zc_ref[:, ko:ko + HEAD_DIM]], axis=0)
            v_raw = jnp.concatenate([zp_ref[:, KV_WIDTH + kvh * HEAD_DIM:KV_WIDTH + (kvh + 1) * HEAD_DIM],
                                     zc_ref[:, vo:vo + HEAD_DIM]], axis=0)
            khat, kr = _rms(k_raw)
            kband = (khat * gk_ref[...]).astype(BF16)
            vband = v_raw.astype(BF16)
            dkn = jnp.zeros((2 * BLOCK, HEAD_DIM), F32)
            dvb = jnp.zeros((2 * BLOCK, HEAD_DIM), F32)
            for g in range(GROUP):
                hd = kvh * GROUP + g
                qo = Q_OFF + hd * HEAD_DIM
                qhat, qr = _rms(zc_ref[:, qo:qo + HEAD_DIM])
                qn = (qhat * gq_ref[...]).astype(BF16)
                s = _dot_nt(qn, kband) * (HEAD_DIM ** -0.5) + bias_ref[hd]
                s = jnp.where(valid, s, MASK_VALUE)
                sink = sink_ref[0, hd]
                m = jnp.maximum(jnp.max(s, axis=-1, keepdims=True), sink)
                p = jnp.exp(s - m)
                esink = jnp.exp(sink - m)
                den = jnp.sum(p, axis=-1, keepdims=True) + esink
                w = p / den
                w16 = w.astype(BF16)
                do = dy_ref[:, A_WIDTH + hd * HEAD_DIM:A_WIDTH + (hd + 1) * HEAD_DIM].astype(BF16)
                dvb = dvb + _dot_tn(w16, do)
                dw = _dot_nt(do, vband)
                rowdot = jnp.sum(w * dw, axis=-1, keepdims=True)
                ds = w * (dw - rowdot)
                dsink = dsink + jnp.where(lane == hd, -(esink / den) * rowdot, 0.0)
                dbias_ref[hd] += ds
                ds16 = (ds * (HEAD_DIM ** -0.5)).astype(BF16)
                dqn = _dot(ds16, kband)
                dkn = dkn + _dot_tn(ds16, qn)
                dgq = dgq + jnp.sum(dqn * qhat, axis=0, keepdims=True)
                dz_ref[:, qo:qo + HEAD_DIM] = _rms_bwd(dqn * gq_ref[...], qhat, qr).astype(BF16)
            dgk_ref[...] += jnp.sum(dkn * khat, axis=0, keepdims=True)
            dk = _rms_bwd(dkn * gk_ref[...], khat, kr)
            co = kvh * HEAD_DIM
            dz_ref[:, ko:ko + HEAD_DIM] = (dk[BLOCK:] + carry_ref[:, co:co + HEAD_DIM]).astype(BF16)
            dz_ref[:, vo:vo + HEAD_DIM] = (dvb[BLOCK:] + carry_ref[:, KV_WIDTH + co:KV_WIDTH + co + HEAD_DIM]).astype(BF16)
            carry_ref[:, co:co + HEAD_DIM] = dk[:BLOCK]
            carry_ref[:, KV_WIDTH + co:KV_WIDTH + co + HEAD_DIM] = dvb[:BLOCK]
        dgq_ref[...] += dgq
        dsink_ref[...] += jnp.sum(dsink, axis=0, keepdims=True)

    order = lambda n: nb - 1 - n
    cur, prev, params, full = _mix_specs(nb, order)
    dy_spec = pl.BlockSpec((BLOCK, A_WIDTH + B_WIDTH), lambda n: (order(n), 0))
    return _pc(
        body, name=name, grid=(nb,), in_specs=[cur, prev, dy_spec] + params,
        out_specs=[cur, full((A_HEADS, BLOCK, BLOCK)), full((BLOCK, BLOCK)), full((A_HEADS, BLOCK)),
                   full((1, HEAD_DIM)), full((1, HEAD_DIM)), full((1, BLOCK)), full((B_HEADS, BLOCK, 2 * BLOCK))],
        out_shape=[jax.ShapeDtypeStruct((t, IN_COLS), BF16), jax.ShapeDtypeStruct((A_HEADS, BLOCK, BLOCK), F32),
                   jax.ShapeDtypeStruct((BLOCK, BLOCK), F32), jax.ShapeDtypeStruct((A_HEADS, BLOCK), F32),
                   jax.ShapeDtypeStruct((1, HEAD_DIM), F32), jax.ShapeDtypeStruct((1, HEAD_DIM), F32),
                   jax.ShapeDtypeStruct((1, BLOCK), F32), jax.ShapeDtypeStruct((B_HEADS, BLOCK, 2 * BLOCK), F32)],
        scratch_shapes=[pltpu.VMEM((BLOCK, 2 * KV_WIDTH), F32)],
        compiler_params=_params(("arbitrary",)),
    )(z, z, dy, sw, sb_t, gv, gq, gk, sinks, bias)


def _ada_fwd(c_all, w_ada, b_cols, name, tn=768):
    nb, d = c_all.shape
    cols = w_ada.shape[1]

    def body(c_ref, w_ref, b_ref, act_ref, mod_ref):
        cv = c_ref[...]
        act = cv * jax.nn.sigmoid(cv)
        act_ref[...] = act
        mod_ref[...] = _dot(act.astype(BF16), w_ref[...].astype(BF16)) + b_ref[...]

    return _pc(
        body, name=name, grid=(cols // tn,),
        in_specs=[pl.BlockSpec((nb, d), lambda j: (0, 0)), pl.BlockSpec((d, tn), lambda j: (0, j)),
                  pl.BlockSpec((1, tn), lambda j: (0, j))],
        out_specs=[pl.BlockSpec((nb, d), lambda j: (0, 0)), pl.BlockSpec((nb, tn), lambda j: (0, j))],
        out_shape=[jax.ShapeDtypeStruct((nb, d), F32), jax.ShapeDtypeStruct((nb, cols), F32)],
        compiler_params=_params(("arbitrary",)),
    )(c_all, w_ada, b_cols)


def _adamw(w, g, m, v):
    m = ADAM_B1 * m + (1.0 - ADAM_B1) * g
    v = ADAM_B2 * v + (1.0 - ADAM_B2) * (g * g)
    m_hat = m / (1.0 - ADAM_B1 ** ADAM_STEP)
    v_hat = v / (1.0 - ADAM_B2 ** ADAM_STEP)
    delta = -ADAM_LR * (m_hat / (jnp.sqrt(v_hat) + ADAM_EPS) + ADAM_WD * w)
    return delta, m, v


def _adam_outer(act_t, dmod, w, m, v, name, tr=256):
    rows, cols = w.shape
    nb = dmod.shape[0]

    def body(act_ref, dm_ref, w_ref, m_ref, v_ref, g_ref, d_ref, nm_ref, nv_ref):
        act = act_ref[...].astype(BF16).astype(F32)
        dm = dm_ref[...].astype(BF16).astype(F32)
        g = act[:, 0:1] * dm[0:1, :]
        for b in range(1, nb):
            g = g + act[:, b:b + 1] * dm[b:b + 1, :]
        g_ref[...] = g
        d_ref[...], nm_ref[...], nv_ref[...] = _adamw(w_ref[...], g, m_ref[...], v_ref[...])

    tile = pl.BlockSpec((tr, cols), lambda i: (i, 0))
    return _pc(
        body, name=name, grid=(rows // tr,),
        in_specs=[pl.BlockSpec((tr, nb), lambda i: (i, 0)), pl.BlockSpec((nb, cols), lambda i: (0, 0)), tile, tile, tile],
        out_specs=[tile] * 4, out_shape=[jax.ShapeDtypeStruct((rows, cols), F32)] * 4,
        compiler_params=_params(("parallel",)),
    )(act_t, dmod, w, m, v)


def _adam_reduce(parts, w, m, v, name, tr):
    rows, cols = w.shape

    def body(p_ref, w_ref, m_ref, v_ref, g_ref, d_ref, nm_ref, nv_ref):
        g = p_ref[0].astype(F32)
        for i in range(1, N_DEV):
            g = g + p_ref[i].astype(F32)
        g_ref[...] = g
        d_ref[...], nm_ref[...], nv_ref[...] = _adamw(w_ref[...], g, m_ref[...], v_ref[...])

    tile = pl.BlockSpec((tr, cols), lambda i: (i, 0))
    return _pc(
        body, name=name, grid=(rows // tr,),
        in_specs=[pl.BlockSpec((N_DEV, tr, cols), lambda i: (0, i, 0)), tile, tile, tile],
        out_specs=[tile] * 4, out_shape=[jax.ShapeDtypeStruct((rows, cols), F32)] * 4,
        compiler_params=_params(("parallel",)),
    )(parts, w, m, v)


def _ungather_cols(g):
    n, r, c = g.shape
    return jnp.transpose(g, (1, 0, 2)).reshape(r, n * c)


def _scatter_cols(w):
    r, c = w.shape
    return jnp.transpose(w.reshape(r, N_DEV, c // N_DEV), (1, 0, 2)).astype(BF16)


def _pad_rows(v):
    flat = v.reshape(-1)
    pad = (-flat.shape[0]) % (8 * BLOCK)
    return jnp.pad(flat, (0, pad)).reshape(-1, BLOCK)


def kernel(x, c, w_ada, b_ada, g_ffn1, w1_ffn1, w3_ffn1, w2_ffn1, g_mix, w_in, spatial_w, spatial_b, g_v, g_q, g_k, sinks, rel_bias, w_out, g_ffn2, w1_ffn2, w3_ffn2, w2_ffn2, loss_target, m_w_ada, m_b_ada, m_g_ffn1, m_w1_ffn1, m_w3_ffn1, m_w2_ffn1, m_g_mix, m_w_in, m_spatial_w, m_spatial_b, m_g_v, m_g_q, m_g_k, m_sinks, m_rel_bias, m_w_out, m_g_ffn2, m_w1_ffn2, m_w3_ffn2, m_w2_ffn2, v_w_ada, v_b_ada, v_g_ffn1, v_w1_ffn1, v_w3_ffn1, v_w2_ffn1, v_g_mix, v_w_in, v_spatial_w, v_spatial_b, v_g_v, v_g_q, v_g_k, v_sinks, v_rel_bias, v_w_out, v_g_ffn2, v_w1_ffn2, v_w3_ffn2, v_w2_ffn2):
    me = _linear(_position())
    d = D_MODEL
    x0 = x[0]
    target = loss_target[0]
    t = x0.shape[0]

    big = dict(w1a=w1_ffn1[0], w3a=w3_ffn1[0], w2a=w2_ffn1[0], w_in=w_in[0], w_out=w_out[0],
               w1b=w1_ffn2[0], w3b=w3_ffn2[0], w2b=w2_ffn2[0])
    col_sharded = ("w1a", "w3a", "w_in", "w1b", "w3b")
    gathered = _all_gather_two_level([big[k].astype(BF16) for k in big], "gather_weights")
    full = {}
    for k, gth in zip(big, gathered):
        full[k] = _ungather_cols(gth) if k in col_sharded else gth.reshape(-1, gth.shape[-1])

    c_all = _exchange_direct([c], "gather_c", broadcast=True)[0].reshape(N_DEV, d)
    b_cols = lax.dynamic_slice(b_ada, (0, me * MOD_COLS), (1, MOD_COLS))
    act_all, mod_cols = _ada_fwd(c_all, w_ada[0], b_cols, "ada_fwd")
    mod = _exchange_direct([mod_cols.reshape(N_DEV, 1, MOD_COLS)], "scatter_mod", broadcast=False)[0]
    mod = mod.reshape(N_MOD, 1, d)
    sh1, sc1, gt1, sh2, sc2, gt2, sh3, sc3, gt3 = [mod[i] for i in range(N_MOD)]

    h1 = _norm_fwd(x0, None, None, 0.0, g_ffn1, sc1, sh1, "norm1_fwd")
    a1, b1, f1 = _ffn_fwd(h1, full["w1a"], full["w3a"], full["w2a"], "ffn1_fwd")
    x1, h2 = _norm_fwd(x0, f1, gt1, 0.5, g_mix, sc2, sh2, "norm2_fwd")
    z = _matmul(h2, full["w_in"], "nn", 512, IN_COLS // 2, d, F32, "mix_in_fwd")
    onehot = _bucket_onehot()
    bias = _select_matmul(rel_bias.T, jnp.asarray(onehot, BF16), "bias_table").reshape(B_HEADS, BLOCK, 2 * BLOCK)
    sb_t = spatial_b[0].T
    mix_params = (spatial_w[0], sb_t, g_v[0], g_q, g_k, sinks, bias)
    ycat = _mix_fwd(z, *mix_params, "mix_core_fwd")
    y = _matmul(ycat, full["w_out"], "nn", 512, d, d, F32, "mix_out_fwd")
    x2, h3 = _norm_fwd(x1, y, gt2, 1.0, g_ffn2, sc3, sh3, "norm3_fwd")
    a2, b2, f2 = _ffn_fwd(h3, full["w1b"], full["w3b"], full["w2b"], "ffn2_fwd")
    dx3, df2, loss_part = _loss_head(x2, f2, gt3, 0.5, target, "loss_head")
    loss = lax.psum(loss_part[0, 0], ("x", "y", "c"))

    grads = {}
    da, db, s, dh3 = _ffn_bwd_dx(df2, a2, b2, full["w1b"], full["w3b"], full["w2b"], "ffn2_bwd_dx")
    grads["w1b"] = _matmul(h3, da, "tn", d, 512, 512, F32, "ffn2_dw1")
    grads["w3b"] = _matmul(h3, db, "tn", d, 512, 512, F32, "ffn2_dw3")
    grads["w2b"] = _matmul(s, df2, "tn", 512, d, 512, F32, "ffn2_dw2")
    dx2, dyg, sums3 = _norm_bwd(dh3, x2, dx3, f2, 0.5, g_ffn2, sc3, gt2, 1.0, "norm3_bwd")
    dycat = _matmul(dyg, full["w_out"], "nt", 512, d, d, F32, "mix_out_bwd")
    grads["w_out"] = _matmul(ycat, dyg, "tn", 1024, d, 512, F32, "mix_out_dw")
    dz, d_sw, d_sb, d_gv, d_gq, d_gk, d_sink, d_bias = _mix_bwd(z, dycat, *mix_params, "mix_core_bwd")
    d_rel = _select_matmul(d_bias.reshape(B_HEADS, BLOCK * 2 * BLOCK), jnp.asarray(onehot.T, BF16), "bias_table_bwd")
    dh2 = _matmul(dz, full["w_in"], "nt", 512, d, IN_COLS // 2, F32, "mix_in_bwd")
    grads["w_in"] = _matmul(h2, dz, "tn", 1024, IN_COLS // 2, 512, F32, "mix_in_dw")
    dx1, df1, sums2 = _norm_bwd(dh2, x1, dx2, y, 1.0, g_mix, sc2, gt1, 0.5, "norm2_bwd")
    da, db, s, dh1 = _ffn_bwd_dx(df1, a1, b1, full["w1a"], full["w3a"], full["w2a"], "ffn1_bwd_dx")
    grads["w1a"] = _matmul(h1, da, "tn", d, 512, 512, F32, "ffn1_dw1")
    grads["w3a"] = _matmul(h1, db, "tn", d, 512, 512, F32, "ffn1_dw3")
    grads["w2a"] = _matmul(s, df1, "tn", 512, d, 512, F32, "ffn1_dw2")
    dx0, sums1 = _norm_bwd(dh1, x0, dx1, f1, 0.5, g_ffn1, sc1, None, 0.0, "norm1_bwd")

    shares = []
    for k in big:
        gk_ = grads[k]
        shares.append(_scatter_cols(gk_) if k in col_sharded
                      else gk_.reshape(N_DEV, gk_.shape[0] // N_DEV, gk_.shape[1]).astype(BF16))
    parts = _exchange_direct(shares, "exchange_grads", broadcast=False)
    moments = dict(w1a=(m_w1_ffn1, v_w1_ffn1), w3a=(m_w3_ffn1, v_w3_ffn1), w2a=(m_w2_ffn1, v_w2_ffn1),
                   w_in=(m_w_in, v_w_in), w_out=(m_w_out, v_w_out),
                   w1b=(m_w1_ffn2, v_w1_ffn2), w3b=(m_w3_ffn2, v_w3_ffn2), w2b=(m_w2_ffn2, v_w2_ffn2))
    upd = {}
    for k, p in zip(big, parts):
        rows = big[k].shape[0]
        tr = 256 if rows % 256 == 0 else rows // 4
        upd[k] = [o[None] for o in _adam_reduce(p, big[k], moments[k][0][0], moments[k][1][0], "adam_" + k, tr)]

    dmod = jnp.concatenate([sums1[1:2], sums1[0:1], sums1[3:4], sums2[1:2], sums2[0:1], sums2[3:4],
                            sums3[1:2], sums3[0:1], sums3[3:4]], axis=1)
    small = [("b_ada", dmod, b_ada, m_b_ada, v_b_ada),
             ("g_ffn1", sums1[2:3], g_ffn1, m_g_ffn1, v_g_ffn1),
             ("g_mix", sums2[2:3], g_mix, m_g_mix, v_g_mix),
             ("spatial_w", d_sw[None], spatial_w, m_spatial_w, v_spatial_w),
             ("spatial_b", d_sb[:, :A_HEADS].T[None], spatial_b, m_spatial_b, v_spatial_b),
             ("g_v", d_gv[None], g_v, m_g_v, v_g_v),
             ("g_q", d_gq, g_q, m_g_q, v_g_q),
             ("g_k", d_gk, g_k, m_g_k, v_g_k),
             ("sinks", d_sink[:, :B_HEADS], sinks, m_sinks, v_sinks),
             ("rel_bias", d_rel.T, rel_bias, m_rel_bias, v_rel_bias),
             ("g_ffn2", sums3[2:3], g_ffn2, m_g_ffn2, v_g_ffn2)]
    packed = [jnp.concatenate([_pad_rows(item[i]) for item in small], axis=0) for i in range(1, 5)]
    g_all = _exchange_direct([packed[0]], "gather_small", broadcast=True)[0]
    small_out = _adam_reduce(g_all, packed[1], packed[2], packed[3], "adam_small", packed[0].shape[0])
    off = 0
    for name, g_, w_, _, _ in small:
        n_rows = _pad_rows(w_).shape[0]
        upd[name] = [o[off:off + n_rows].reshape(-1)[:w_.size].reshape(w_.shape) for o in small_out]
        off += n_rows

    dmod_rows = g_all[:, :N_MOD * d // BLOCK, :].reshape(N_DEV, N_MOD * d)
    dmod_cols = lax.dynamic_slice(dmod_rows, (0, me * MOD_COLS), (N_DEV, MOD_COLS))
    upd["w_ada"] = [o[None] for o in _adam_outer(act_all.T, dmod_cols, w_ada[0], m_w_ada[0], v_w_ada[0], "adam_w_ada")]

    order = [("w_ada", "w_ada"), ("b_ada", "b_ada"), ("g_ffn1", "g_ffn1"), ("w1_ffn1", "w1a"), ("w3_ffn1", "w3a"),
             ("w2_ffn1", "w2a"), ("g_mix", "g_mix"), ("w_in", "w_in"), ("spatial_w", "spatial_w"),
             ("spatial_b", "spatial_b"), ("g_v", "g_v"), ("g_q", "g_q"), ("g_k", "g_k"), ("sinks", "sinks"),
             ("rel_bias", "rel_bias"), ("w_out", "w_out"), ("g_ffn2", "g_ffn2"), ("w1_ffn2", "w1b"),
             ("w3_ffn2", "w3b"), ("w2_ffn2", "w2b")]
    outs = [loss, dx0[None]]
    for i in range(4):
        outs += [upd[key][i] for _, key in order]
    return tuple(outs)
```

```python
import functools
import math

import numpy as np
import jax
import jax.numpy as jnp
from jax import lax
from jax.experimental import pallas as pl
from jax.experimental.pallas import tpu as pltpu

F32 = jnp.float32
BF16 = jnp.bfloat16

D_MODEL = 2048
D_FF = 5632
BLOCK = 128
A_HEADS = 8
A_WIDTH = 1024
B_HEADS = 16
B_KV_HEADS = 2
GROUP = B_HEADS // B_KV_HEADS
HEAD_DIM = 64
B_WIDTH = 1024
KV_WIDTH = 128
IN_COLS = 3328
Q_OFF = 2 * A_WIDTH
K_OFF = Q_OFF + B_WIDTH
V_OFF = K_OFF + KV_WIDTH
N_BUCKETS = 32
N_MOD = 9
EPS = 1e-6
N_DEV = 8
MOD_COLS = N_MOD * D_MODEL // N_DEV

ADAM_LR = 0.001
ADAM_B1 = 0.9
ADAM_B2 = 0.999
ADAM_EPS = 1e-08
ADAM_WD = 0.01
ADAM_STEP = 10

MASK_VALUE = -1e30
VMEM_LIMIT = 56 * 1024 * 1024
MESH_ID = pl.DeviceIdType.MESH
ANY = pl.BlockSpec(memory_space=pl.ANY)

_SQRT_HALF = 0.7071067811865476
_INV_SQRT_2PI = 0.3989422804014327


def _pc(body, **kw):
    return pl.pallas_call(body, **kw)


def _params(sem=None):
    return pltpu.CompilerParams(dimension_semantics=sem, vmem_limit_bytes=VMEM_LIMIT)


def _dot(a, b):
    return lax.dot_general(a, b, (((1,), (0,)), ((), ())), preferred_element_type=F32)


def _dot_nt(a, b):
    return lax.dot_general(a, b, (((1,), (1,)), ((), ())), preferred_element_type=F32)


def _dot_tn(a, b):
    return lax.dot_general(a, b, (((0,), (0,)), ((), ())), preferred_element_type=F32)


def _gelu(x):
    return 0.5 * x * (1.0 + lax.erf(x * _SQRT_HALF))


def _gelu_grad(x):
    return 0.5 * (1.0 + lax.erf(x * _SQRT_HALF)) + x * jnp.exp(-0.5 * x * x) * _INV_SQRT_2PI


def _rms(x):
    r = lax.rsqrt(jnp.mean(x * x, axis=-1, keepdims=True) + EPS)
    return x * r, r


def _rms_bwd(dy, y, r):
    return r * (dy - y * jnp.mean(dy * y, axis=-1, keepdims=True))


def _row_spec(tm, cols):
    return pl.BlockSpec((tm, cols), lambda i: (i, 0))


def _vec_spec(cols):
    return pl.BlockSpec((1, cols), lambda i: (0, 0))


def _position():
    x, y, c = lax.axis_index("x"), lax.axis_index("y"), lax.axis_index("c")
    return x, y, c


def _linear(p):
    return 4 * p[0] + 2 * p[1] + p[2]


class _Comm:
    PEER_COPIES = N_DEV - 1

    def __init__(self, arrs):
        self.arrs = list(arrs)
        n = len(self.arrs)
        self.scratch_shapes = [pltpu.SemaphoreType.DMA((self.PEER_COPIES * n,)),
                               pltpu.SemaphoreType.DMA((self.PEER_COPIES * n,)),
                               pltpu.SemaphoreType.DMA((n,))]

    def bind(self, srcs, dsts, sems):
        self.srcs, self.dsts = srcs, dsts
        self.send_sems, self.recv_sems, self.local_sems = sems
        x, y, c = _position()
        self.me, self.sibling, self.core = (x, y, c), (x, y, 1 - c), c
        self.chips = [(1 - x, y), (x, 1 - y), (1 - x, 1 - y)]
        self.peers = [(1 - x if k & 4 else x, 1 - y if k & 2 else y, 1 - c if k & 1 else c)
                      for k in range(1, N_DEV)]

    def relay(self):
        pass


class _GatherTwoLevel(_Comm):
    def __init__(self, arrs):
        super().__init__(arrs)
        self.out_shape = [jax.ShapeDtypeStruct((N_DEV,) + a.shape, a.dtype) for a in self.arrs]

    def _copy(self, a, k, block, to, from_input=False):
        rows = self.dsts[a].at[_linear(block)]
        return pltpu.make_async_remote_copy(
            src_ref=self.srcs[a] if from_input else rows, dst_ref=rows,
            send_sem=self.send_sems.at[self.PEER_COPIES * a + k], recv_sem=self.recv_sems.at[self.PEER_COPIES * a + k],
            device_id=to, device_id_type=MESH_ID)

    def _local(self, a):
        return pltpu.make_async_copy(self.srcs[a], self.dsts[a].at[_linear(self.me)], self.local_sems.at[a])

    def _first(self, a):
        return [self._copy(a, 0, self.me, self.sibling, True)] + [
            self._copy(a, 1 + j, self.me, (*chip, self.core), True) for j, chip in enumerate(self.chips)]

    def _passed(self, a):
        return [self._copy(a, 4 + j, (*chip, self.core), self.sibling) for j, chip in enumerate(self.chips)]

    def start(self):
        for a in range(len(self.arrs)):
            self._local(a).start()
            for cp in self._first(a):
                cp.start()

    def relay(self):
        for a in range(len(self.arrs)):
            for j, chip in enumerate(self.chips):
                self._copy(a, 1 + j, (*chip, self.core), self.me).wait_recv()
                self._passed(a)[j].start()

    def finish(self):
        for a in range(len(self.arrs)):
            self._copy(a, 0, self.sibling, self.me).wait_recv()
            for j, chip in enumerate(self.chips):
                self._copy(a, 4 + j, (*chip, 1 - self.core), self.me).wait_recv()
        for a in range(len(self.arrs)):
            for cp in self._first(a) + self._passed(a):
                cp.wait_send()
            self._local(a).wait()


class _Direct(_Comm):
    def __init__(self, arrs, broadcast):
        super().__init__(arrs)
        self.broadcast = broadcast
        self.out_shape = [jax.ShapeDtypeStruct(((N_DEV,) + a.shape) if broadcast else a.shape, a.dtype)
                          for a in self.arrs]

    def _outgoing(self, a, to):
        return self.srcs[a] if self.broadcast else self.srcs[a].at[_linear(to)]

    def _copy(self, a, k, sender, to):
        return pltpu.make_async_remote_copy(
            src_ref=self._outgoing(a, to), dst_ref=self.dsts[a].at[_linear(sender)],
            send_sem=self.send_sems.at[self.PEER_COPIES * a + k], recv_sem=self.recv_sems.at[self.PEER_COPIES * a + k],
            device_id=to, device_id_type=MESH_ID)

    def _local(self, a):
        return pltpu.make_async_copy(self._outgoing(a, self.me), self.dsts[a].at[_linear(self.me)],
                                     self.local_sems.at[a])

    def start(self):
        for a in range(len(self.arrs)):
            self._local(a).start()
            for k, peer in enumerate(self.peers):
                self._copy(a, k, self.me, peer).start()

    def finish(self):
        for a in range(len(self.arrs)):
            for k, peer in enumerate(self.peers):
                self._copy(a, k, peer, self.me).wait_recv()
        for a in range(len(self.arrs)):
            for k, peer in enumerate(self.peers):
                self._copy(a, k, self.me, peer).wait_send()
            self._local(a).wait()


def _run_comm(comm, name):
    n = len(comm.arrs)

    def body(*refs):
        comm.bind(refs[:n], refs[n:2 * n], refs[2 * n:])
        comm.start()
        comm.relay()
        comm.finish()

    return _pc(body, name=name, out_shape=comm.out_shape, in_specs=[ANY] * n, out_specs=[ANY] * n,
               scratch_shapes=comm.scratch_shapes)(*comm.arrs)


def _call(body, inputs, *, name, grid, in_specs, out_specs, out_shape, scratch_shapes=(), semantics,
          comm=None, relay_at=0.5):
    if comm is None:
        out = _pc(body, name=name, grid=grid, in_specs=in_specs, out_specs=out_specs, out_shape=out_shape,
                  scratch_shapes=list(scratch_shapes), compiler_params=_params(semantics))(*inputs)
        return list(out), []
    n_in, n_out, n_sc, k = len(in_specs), len(out_specs), len(scratch_shapes), len(comm.arrs)
    steps = math.prod(grid)
    relay_step = min(steps - 1, int(steps * relay_at))

    def carrier(*refs):
        ins, rest = refs[:n_in], refs[n_in:]
        csrc, rest = rest[:k], rest[k:]
        outs, rest = rest[:n_out], rest[n_out:]
        cdst, rest = rest[:k], rest[k:]
        scratch, csems = rest[:n_sc], rest[n_sc:]
        step = 0
        for axis, size in enumerate(grid):
            step = step * size + pl.program_id(axis)
        comm.bind(csrc, cdst, csems)
        pl.when(step == 0)(comm.start)
        pl.when(step == relay_step)(comm.relay)
        body(*ins, *outs, *scratch)
        pl.when(step == steps - 1)(comm.finish)

    out = _pc(carrier, name=name, grid=grid, in_specs=list(in_specs) + [ANY] * k,
              out_specs=list(out_specs) + [ANY] * k, out_shape=list(out_shape) + comm.out_shape,
              scratch_shapes=list(scratch_shapes) + comm.scratch_shapes,
              compiler_params=_params(("arbitrary",) * len(grid)))(*inputs, *comm.arrs)
    return list(out[:n_out]), list(out[n_out:])


def _matmul(a, b, mode, tm, tn, tk, out_dtype, name, comm=None):
    if mode == "nn":
        (m, kk), nn = a.shape, b.shape[1]
        a_spec = pl.BlockSpec((tm, tk), lambda i, j, k: (i, k))
        b_spec = pl.BlockSpec((tk, tn), lambda i, j, k: (k, j))
        dot = _dot
    elif mode == "nt":
        (m, kk), nn = a.shape, b.shape[0]
        a_spec = pl.BlockSpec((tm, tk), lambda i, j, k: (i, k))
        b_spec = pl.BlockSpec((tn, tk), lambda i, j, k: (j, k))
        dot = _dot_nt
    else:
        (kk, m), nn = a.shape, b.shape[1]
        a_spec = pl.BlockSpec((tk, tm), lambda i, j, k: (k, i))
        b_spec = pl.BlockSpec((tk, tn), lambda i, j, k: (k, j))
        dot = _dot_tn
    assert m % tm == 0 and nn % tn == 0 and kk % tk == 0, (a.shape, b.shape, tm, tn, tk)
    nk = kk // tk
    assert nk == 1 or out_dtype == F32

    def body(a_ref, b_ref, o_ref):
        p = dot(a_ref[...], b_ref[...])
        if nk == 1:
            o_ref[...] = p.astype(o_ref.dtype)
        else:
            k = pl.program_id(2)

            @pl.when(k == 0)
            def _():
                o_ref[...] = p

            @pl.when(k > 0)
            def _():
                o_ref[...] += p

    out, moved = _call(
        body, (a, b), name=name, grid=(m // tm, nn // tn, nk),
        in_specs=[a_spec, b_spec], out_specs=[pl.BlockSpec((tm, tn), lambda i, j, k: (i, j))],
        out_shape=[jax.ShapeDtypeStruct((m, nn), out_dtype)],
        semantics=("parallel", "parallel", "arbitrary"), comm=comm)
    return out[0] if comm is None else (out[0], moved)


def _norm_fwd(x_prev, f_prev, gate_prev, coef, g, sc, sh, name, tm=256):
    t, d = x_prev.shape
    residual = f_prev is not None

    def body(*refs):
        if residual:
            xp_ref, fp_ref, gp_ref, g_ref, sc_ref, sh_ref, x_ref, h_ref = refs
            x = xp_ref[...] + coef * gp_ref[...] * fp_ref[...]
            x_ref[...] = x
        else:
            xp_ref, g_ref, sc_ref, sh_ref, h_ref = refs
            x = xp_ref[...]
        y, _ = _rms(x)
        h_ref[...] = ((y * g_ref[...]) * (1.0 + sc_ref[...]) + sh_ref[...]).astype(BF16)

    row, vec = _row_spec(tm, d), _vec_spec(d)
    if residual:
        ins, in_specs = (x_prev, f_prev, gate_prev, g, sc, sh), [row, row, vec, vec, vec, vec]
        out_shape = [jax.ShapeDtypeStruct((t, d), F32), jax.ShapeDtypeStruct((t, d), BF16)]
        out_specs = [row, row]
    else:
        ins, in_specs = (x_prev, g, sc, sh), [row, vec, vec, vec]
        out_shape = [jax.ShapeDtypeStruct((t, d), BF16)]
        out_specs = [row]
    out = _pc(body, name=name, grid=(t // tm,), in_specs=in_specs, out_specs=out_specs, out_shape=out_shape,
              compiler_params=_params(("parallel",)))(*ins)
    return out if residual else out[0]


def _norm_bwd(dh, x, dxo, fo, coef_o, g, sc, gate_prev, coef_prev, name, tm=256, comm=None):
    t, d = x.shape
    with_prev = gate_prev is not None

    def body(*refs):
        if with_prev:
            dh_ref, x_ref, dxo_ref, fo_ref, g_ref, sc_ref, gp_ref, dx_ref, dfp_ref, sums_ref = refs
        else:
            dh_ref, x_ref, dxo_ref, fo_ref, g_ref, sc_ref, dx_ref, sums_ref = refs
        dh_v, dxo_v = dh_ref[...], dxo_ref[...]
        y, r = _rms(x_ref[...])
        n = y * g_ref[...]
        dn = dh_v * (1.0 + sc_ref[...])
        dx = dxo_v + _rms_bwd(dn * g_ref[...], y, r)
        dx_ref[...] = dx
        if with_prev:
            dfp_ref[...] = (coef_prev * gp_ref[...] * dx).astype(BF16)
        @pl.when(pl.program_id(0) == 0)
        def _():
            sums_ref[...] = jnp.zeros_like(sums_ref)

        sums_ref[0:1, :] += jnp.sum(dh_v * n, axis=0, keepdims=True)
        sums_ref[1:2, :] += jnp.sum(dh_v, axis=0, keepdims=True)
        sums_ref[2:3, :] += jnp.sum(dn * y, axis=0, keepdims=True)
        sums_ref[3:4, :] += jnp.sum(coef_o * dxo_v * fo_ref[...], axis=0, keepdims=True)

    row, vec = _row_spec(tm, d), _vec_spec(d)
    sums_spec = pl.BlockSpec((8, d), lambda i: (0, 0))
    ins, in_specs = [dh, x, dxo, fo, g, sc], [row, row, row, row, vec, vec]
    out_shape, out_specs = [jax.ShapeDtypeStruct((t, d), F32)], [row]
    if with_prev:
        ins.append(gate_prev)
        in_specs.append(vec)
        out_shape.append(jax.ShapeDtypeStruct((t, d), BF16))
        out_specs.append(row)
    out_shape.append(jax.ShapeDtypeStruct((8, d), F32))
    out_specs.append(sums_spec)
    out, moved = _call(body, ins, name=name, grid=(t // tm,), in_specs=in_specs, out_specs=out_specs,
                       out_shape=out_shape, semantics=("arbitrary",), comm=comm)
    return out if comm is None else (out, moved)


def _ffn_fwd(h, w1, w3, w2, name, tm=512, tf=512, comm=None, relay_at=0.5):
    t, d = h.shape
    ff = w1.shape[1]

    def body(h_ref, w1_ref, w3_ref, w2_ref, a_ref, b_ref, f_ref):
        j = pl.program_id(1)
        hv = h_ref[...]
        a = _dot(hv, w1_ref[...])
        b = _dot(hv, w3_ref[...])
        a_ref[...] = a
        b_ref[...] = b
        s = (a * jax.nn.sigmoid(a) * b).astype(BF16)
        p = _dot(s, w2_ref[...])

        @pl.when(j == 0)
        def _():
            f_ref[...] = p

        @pl.when(j > 0)
        def _():
            f_ref[...] += p

    tile = pl.BlockSpec((tm, tf), lambda i, j: (i, j))
    rows = pl.BlockSpec((tm, d), lambda i, j: (i, 0))
    out, moved = _call(
        body, (h, w1, w3, w2), name=name, grid=(t // tm, ff // tf),
        in_specs=[rows, pl.BlockSpec((d, tf), lambda i, j: (0, j)), pl.BlockSpec((d, tf), lambda i, j: (0, j)),
                  pl.BlockSpec((tf, d), lambda i, j: (j, 0))],
        out_specs=[tile, tile, rows],
        out_shape=[jax.ShapeDtypeStruct((t, ff), F32), jax.ShapeDtypeStruct((t, ff), F32),
                   jax.ShapeDtypeStruct((t, d), F32)],
        semantics=("parallel", "arbitrary"), comm=comm, relay_at=relay_at)
    return out if comm is None else (out, moved)


def _ffn_bwd_dx(df, a, b, w1, w3, w2, name, tm=512, tf=512, comm=None):
    t, d = df.shape
    ff = a.shape[1]

    def body(df_ref, a_ref, b_ref, w1_ref, w3_ref, w2_ref, da_ref, db_ref, s_ref, dh_ref):
        j = pl.program_id(1)
        ds = _dot_nt(df_ref[...], w2_ref[...])
        av, bv = a_ref[...], b_ref[...]
        sg = jax.nn.sigmoid(av)
        sil = av * sg
        s_ref[...] = (sil * bv).astype(BF16)
        da = (ds * bv * (sg * (1.0 + av * (1.0 - sg)))).astype(BF16)
        db = (ds * sil).astype(BF16)
        da_ref[...] = da
        db_ref[...] = db
        p = _dot_nt(da, w1_ref[...]) + _dot_nt(db, w3_ref[...])

        @pl.when(j == 0)
        def _():
            dh_ref[...] = p

        @pl.when(j > 0)
        def _():
            dh_ref[...] += p

    tile = pl.BlockSpec((tm, tf), lambda i, j: (i, j))
    rows = pl.BlockSpec((tm, d), lambda i, j: (i, 0))
    out, moved = _call(
        body, (df, a, b, w1, w3, w2), name=name, grid=(t // tm, ff // tf),
        in_specs=[rows, tile, tile, pl.BlockSpec((d, tf), lambda i, j: (0, j)),
                  pl.BlockSpec((d, tf), lambda i, j: (0, j)), pl.BlockSpec((tf, d), lambda i, j: (j, 0))],
        out_specs=[tile, tile, tile, rows],
        out_shape=[jax.ShapeDtypeStruct((t, ff), BF16)] * 3 + [jax.ShapeDtypeStruct((t, d), F32)],
        semantics=("parallel", "arbitrary"), comm=comm)
    return out if comm is None else (out, moved)


def _loss_head(x_prev, f_prev, gate_prev, coef, target, name, tm=256):
    t, d = x_prev.shape
    steps = t // tm

    def body(xp_ref, fp_ref, gp_ref, tg_ref, dy_ref, df_ref, loss_ref, acc_ref):
        i = pl.program_id(0)
        e = xp_ref[...] + coef * gp_ref[...] * fp_ref[...] - tg_ref[...]
        dy = e * (1.0 / d)
        dy_ref[...] = dy
        df_ref[...] = (coef * gp_ref[...] * dy).astype(BF16)
        part = jnp.sum(e * e, axis=0, keepdims=True)

        @pl.when(i == 0)
        def _():
            acc_ref[...] = part

        @pl.when(i > 0)
        def _():
            acc_ref[...] += part

        @pl.when(i == steps - 1)
        def _():
            loss_ref[...] = jnp.sum(acc_ref[...], axis=1, keepdims=True) * (0.5 / d)

    row, vec = _row_spec(tm, d), _vec_spec(d)
    return _pc(
        body, name=name, grid=(steps,),
        in_specs=[row, row, vec, row], out_specs=[row, row, pl.BlockSpec((1, 1), lambda i: (0, 0))],
        out_shape=[jax.ShapeDtypeStruct((t, d), F32), jax.ShapeDtypeStruct((t, d), BF16),
                   jax.ShapeDtypeStruct((1, 1), F32)],
        scratch_shapes=[pltpu.VMEM((1, d), F32)],
        compiler_params=_params(("arbitrary",)),
    )(x_prev, f_prev, gate_prev, target)


def _split3(x):
    hi = x.astype(BF16)
    r1 = x - hi.astype(F32)
    mid = r1.astype(BF16)
    lo = (r1 - mid.astype(F32)).astype(BF16)
    return hi, mid, lo


def _select_matmul(a, onehot, name):
    m, n = a.shape[0], onehot.shape[1]

    def body(a_ref, oh_ref, o_ref):
        hi, mid, lo = _split3(a_ref[...])
        oh = oh_ref[...]
        o_ref[...] = (_dot(hi, oh) + _dot(mid, oh)) + _dot(lo, oh)

    return _pc(body, name=name, out_shape=jax.ShapeDtypeStruct((m, n), F32), compiler_params=_params())(a, onehot)


def _bucket_onehot():
    qi = np.arange(BLOCK)[:, None]
    kj = np.arange(2 * BLOCK)[None, :]
    dist = np.clip(qi + BLOCK - kj, 0, None)
    nf = np.maximum(dist, 1).astype(np.float32)
    large = 16 + (np.log(nf / np.float32(16)) / np.float32(math.log(128 / 16)) * np.float32(16)).astype(np.int32)
    bucket = np.where(dist < 16, dist, np.minimum(large, N_BUCKETS - 1)).reshape(-1)
    return (bucket[None, :] == np.arange(N_BUCKETS)[:, None]).astype(np.float32)


def _window_mask(n):
    qi = lax.broadcasted_iota(jnp.int32, (BLOCK, 2 * BLOCK), 0)
    kj = lax.broadcasted_iota(jnp.int32, (BLOCK, 2 * BLOCK), 1)
    return (kj > qi) & (kj <= qi + BLOCK) & ((kj >= BLOCK) | (n > 0))


def _mix_specs(nb, order):
    cur = pl.BlockSpec((BLOCK, IN_COLS), lambda n: (order(n), 0))
    prev = pl.BlockSpec((BLOCK, 2 * KV_WIDTH), lambda n: (jnp.maximum(order(n) - 1, 0), K_OFF // (2 * KV_WIDTH)))
    full = lambda shape: pl.BlockSpec(shape, lambda n: (0,) * len(shape))
    params = [full((A_HEADS, BLOCK, BLOCK)), full((BLOCK, A_HEADS)), full((A_HEADS, BLOCK)),
              full((1, HEAD_DIM)), full((1, HEAD_DIM)), pl.BlockSpec(memory_space=pltpu.SMEM),
              full((B_HEADS, BLOCK, 2 * BLOCK))]
    return cur, prev, params, full


def _mix_fwd(z, sw, sb_t, gv, gq, gk, sinks, bias, name):
    t = z.shape[0]
    nb = t // BLOCK

    def body(zc_ref, zp_ref, sw_ref, sbt_ref, gv_ref, gq_ref, gk_ref, sink_ref, bias_ref, y_ref):
        n = pl.program_id(0)
        ri = lax.broadcasted_iota(jnp.int32, (BLOCK, BLOCK), 0)
        ci = lax.broadcasted_iota(jnp.int32, (BLOCK, BLOCK), 1)
        tril = ri >= ci
        for h in range(A_HEADS):
            u = _gelu(zc_ref[:, h * BLOCK:(h + 1) * BLOCK])
            vv = _gelu(zc_ref[:, A_WIDTH + h * BLOCK:A_WIDTH + (h + 1) * BLOCK])
            vhat, _ = _rms(vv)
            vn = (vhat * gv_ref[h:h + 1, :]).astype(BF16)
            w = jnp.where(tril, sw_ref[h], 0.0).astype(BF16)
            mixed = _dot(w, vn) + sbt_ref[:, h:h + 1]
            y_ref[:, h * BLOCK:(h + 1) * BLOCK] = (u * mixed).astype(BF16)
        valid = _window_mask(n)
        for kvh in range(B_KV_HEADS):
            ko, vo = K_OFF + kvh * HEAD_DIM, V_OFF + kvh * HEAD_DIM
            k_raw = jnp.concatenate([zp_ref[:, kvh * HEAD_DIM:(kvh + 1) * HEAD_DIM],
                                     zc_ref[:, ko:ko + HEAD_DIM]], axis=0)
            v_raw = jnp.concatenate([zp_ref[:, KV_WIDTH + kvh * HEAD_DIM:KV_WIDTH + (kvh + 1) * HEAD_DIM],
                                     zc_ref[:, vo:vo + HEAD_DIM]], axis=0)
            khat, _ = _rms(k_raw)
            kband = (khat * gk_ref[...]).astype(BF16)
            vband = v_raw.astype(BF16)
            for g in range(GROUP):
                hd = kvh * GROUP + g
                qo = Q_OFF + hd * HEAD_DIM
                qhat, _ = _rms(zc_ref[:, qo:qo + HEAD_DIM])
                qn = (qhat * gq_ref[...]).astype(BF16)
                s = _dot_nt(qn, kband) * (HEAD_DIM ** -0.5) + bias_ref[hd]
                s = jnp.where(valid, s, MASK_VALUE)
                sink = sink_ref[0, hd]
                m = jnp.maximum(jnp.max(s, axis=-1, keepdims=True), sink)
                p = jnp.exp(s - m)
                den = jnp.sum(p, axis=-1, keepdims=True) + jnp.exp(sink - m)
                o = _dot((p / den).astype(BF16), vband)
                yo = A_WIDTH + hd * HEAD_DIM
                y_ref[:, yo:yo + HEAD_DIM] = o.astype(BF16)

    cur, prev, params, _ = _mix_specs(nb, lambda n: n)
    return _pc(
        body, name=name, grid=(nb,), in_specs=[cur, prev] + params,
        out_specs=pl.BlockSpec((BLOCK, A_WIDTH + B_WIDTH), lambda n: (n, 0)),
        out_shape=jax.ShapeDtypeStruct((t, A_WIDTH + B_WIDTH), BF16),
        compiler_params=_params(("arbitrary",)),
    )(z, z, sw, sb_t, gv, gq, gk, sinks, bias)


def _mix_bwd(z, dy, sw, sb_t, gv, gq, gk, sinks, bias, name, comm=None):
    t = z.shape[0]
    nb = t // BLOCK

    def body(zc_ref, zp_ref, dy_ref, sw_ref, sbt_ref, gv_ref, gq_ref, gk_ref, sink_ref, bias_ref,
             dz_ref, dsw_ref, dsb_ref, dgv_ref, dgq_ref, dgk_ref, dsink_ref, dbias_ref, carry_ref):
        step = pl.program_id(0)
        n = nb - 1 - step

        @pl.when(step == 0)
        def _():
            dsw_ref[...] = jnp.zeros_like(dsw_ref)
            dsb_ref[...] = jnp.zeros_like(dsb_ref)
            dgv_ref[...] = jnp.zeros_like(dgv_ref)
            dgq_ref[...] = jnp.zeros_like(dgq_ref)
            dgk_ref[...] = jnp.zeros_like(dgk_ref)
            dsink_ref[...] = jnp.zeros_like(dsink_ref)
            dbias_ref[...] = jnp.zeros_like(dbias_ref)
            carry_ref[...] = jnp.zeros_like(carry_ref)

        ri = lax.broadcasted_iota(jnp.int32, (BLOCK, BLOCK), 0)
        ci = lax.broadcasted_iota(jnp.int32, (BLOCK, BLOCK), 1)
        tril = ri >= ci
        dsb = jnp.zeros((BLOCK, BLOCK), F32)
        for h in range(A_HEADS):
            uo, vo = h * BLOCK, A_WIDTH + h * BLOCK
            u_raw, v_raw = zc_ref[:, uo:uo + BLOCK], zc_ref[:, vo:vo + BLOCK]
            u, vv = _gelu(u_raw), _gelu(v_raw)
            vhat, r = _rms(vv)
            vn = (vhat * gv_ref[h:h + 1, :]).astype(BF16)
            w = jnp.where(tril, sw_ref[h], 0.0).astype(BF16)
            mixed = _dot(w, vn) + sbt_ref[:, h:h + 1]
            dya = dy_ref[:, uo:uo + BLOCK]
            dmixed = dya * u
            dm16 = dmixed.astype(BF16)
            dsw_ref[h] += jnp.where(tril, _dot_nt(dm16, vn), 0.0)
            dsb = dsb + jnp.where(ci == h, jnp.sum(dmixed, axis=1, keepdims=True), 0.0)
            dvn = _dot_tn(w, dm16)
            dgv_ref[h:h + 1, :] += jnp.sum(dvn * vhat, axis=0, keepdims=True)
            dvv = _rms_bwd(dvn * gv_ref[h:h + 1, :], vhat, r)
            dz_ref[:, uo:uo + BLOCK] = (dya * mixed * _gelu_grad(u_raw)).astype(BF16)
            dz_ref[:, vo:vo + BLOCK] = (dvv * _gelu_grad(v_raw)).astype(BF16)
        dsb_ref[...] += dsb

        valid = _window_mask(n)
        lane = lax.broadcasted_iota(jnp.int32, (BLOCK, BLOCK), 1)
        dsink = jnp.zeros((BLOCK, BLOCK), F32)
        dgq = jnp.zeros((1, HEAD_DIM), F32)
        for kvh in range(B_KV_HEADS):
            ko, vo = K_OFF + kvh * HEAD_DIM, V_OFF + kvh * HEAD_DIM
            k_raw = jnp.concatenate([zp_ref[:, kvh * HEAD_DIM:(kvh + 1) * HEAD_DIM],
                                     zc_ref[:, ko:ko + HEAD_DIM]], axis=0)
            v_raw = jnp.concatenate([zp_ref[:, KV_WIDTH + kvh * HEAD_DIM:KV_WIDTH + (kvh + 1) * HEAD_DIM],
                                     zc_ref[:, vo:vo + HEAD_DIM]], axis=0)
            khat, kr = _rms(k_raw)
            kband = (khat * gk_ref[...]).astype(BF16)
            vband = v_raw.astype(BF16)
            dkn = jnp.zeros((2 * BLOCK, HEAD_DIM), F32)
            dvb = jnp.zeros((2 * BLOCK, HEAD_DIM), F32)
            for g in range(GROUP):
                hd = kvh * GROUP + g
                qo = Q_OFF + hd * HEAD_DIM
                qhat, qr = _rms(zc_ref[:, qo:qo + HEAD_DIM])
                qn = (qhat * gq_ref[...]).astype(BF16)
                s = _dot_nt(qn, kband) * (HEAD_DIM ** -0.5) + bias_ref[hd]
                s = jnp.where(valid, s, MASK_VALUE)
                sink = sink_ref[0, hd]
                m = jnp.maximum(jnp.max(s, axis=-1, keepdims=True), sink)
                p = jnp.exp(s - m)
                esink = jnp.exp(sink - m)
                den = jnp.sum(p, axis=-1, keepdims=True) + esink
                w = p / den
                w16 = w.astype(BF16)
                do = dy_ref[:, A_WIDTH + hd * HEAD_DIM:A_WIDTH + (hd + 1) * HEAD_DIM].astype(BF16)
                dvb = dvb + _dot_tn(w16, do)
                dw = _dot_nt(do, vband)
                rowdot = jnp.sum(w * dw, axis=-1, keepdims=True)
                ds = w * (dw - rowdot)
                dsink = dsink + jnp.where(lane == hd, -(esink / den) * rowdot, 0.0)
                dbias_ref[hd] += ds
                ds16 = (ds * (HEAD_DIM ** -0.5)).astype(BF16)
                dqn = _dot(ds16, kband)
                dkn = dkn + _dot_tn(ds16, qn)
                dgq = dgq + jnp.sum(dqn * qhat, axis=0, keepdims=True)
                dz_ref[:, qo:qo + HEAD_DIM] = _rms_bwd(dqn * gq_ref[...], qhat, qr).astype(BF16)
            dgk_ref[...] += jnp.sum(dkn * khat, axis=0, keepdims=True)
            dk = _rms_bwd(dkn * gk_ref[...], khat, kr)
            co = kvh * HEAD_DIM
            dz_ref[:, ko:ko + HEAD_DIM] = (dk[BLOCK:] + carry_ref[:, co:co + HEAD_DIM]).astype(BF16)
            dz_ref[:, vo:vo + HEAD_DIM] = (dvb[BLOCK:] + carry_ref[:, KV_WIDTH + co:KV_WIDTH + co + HEAD_DIM]).astype(BF16)
            carry_ref[:, co:co + HEAD_DIM] = dk[:BLOCK]
            carry_ref[:, KV_WIDTH + co:KV_WIDTH + co + HEAD_DIM] = dvb[:BLOCK]
        dgq_ref[...] += dgq
        dsink_ref[...] += jnp.sum(dsink, axis=0, keepdims=True)

    order = lambda n: nb - 1 - n
    cur, prev, params, full = _mix_specs(nb, order)
    dy_spec = pl.BlockSpec((BLOCK, A_WIDTH + B_WIDTH), lambda n: (order(n), 0))
    out, moved = _call(
        body, (z, z, dy, sw, sb_t, gv, gq, gk, sinks, bias), name=name, grid=(nb,),
        in_specs=[cur, prev, dy_spec] + params,
        out_specs=[cur, full((A_HEADS, BLOCK, BLOCK)), full((BLOCK, BLOCK)), full((A_HEADS, BLOCK)),
                   full((1, HEAD_DIM)), full((1, HEAD_DIM)), full((1, BLOCK)), full((B_HEADS, BLOCK, 2 * BLOCK))],
        out_shape=[jax.ShapeDtypeStruct((t, IN_COLS), BF16), jax.ShapeDtypeStruct((A_HEADS, BLOCK, BLOCK), F32),
                   jax.ShapeDtypeStruct((BLOCK, BLOCK), F32), jax.ShapeDtypeStruct((A_HEADS, BLOCK), F32),
                   jax.ShapeDtypeStruct((1, HEAD_DIM), F32), jax.ShapeDtypeStruct((1, HEAD_DIM), F32),
                   jax.ShapeDtypeStruct((1, BLOCK), F32), jax.ShapeDtypeStruct((B_HEADS, BLOCK, 2 * BLOCK), F32)],
        scratch_shapes=[pltpu.VMEM((BLOCK, 2 * KV_WIDTH), F32)],
        semantics=("arbitrary",), comm=comm)
    return out if comm is None else (out, moved)


def _ada_fwd(c_all, w_ada, b_cols, name, tn=768):
    nb, d = c_all.shape
    cols = w_ada.shape[1]

    def body(c_ref, w_ref, b_ref, act_ref, mod_ref):
        cv = c_ref[...]
        act = cv * jax.nn.sigmoid(cv)
        act_ref[...] = act
        mod_ref[...] = _dot(act.astype(BF16), w_ref[...].astype(BF16)) + b_ref[...]

    return _pc(
        body, name=name, grid=(cols // tn,),
        in_specs=[pl.BlockSpec((nb, d), lambda j: (0, 0)), pl.BlockSpec((d, tn), lambda j: (0, j)),
                  pl.BlockSpec((1, tn), lambda j: (0, j))],
        out_specs=[pl.BlockSpec((nb, d), lambda j: (0, 0)), pl.BlockSpec((nb, tn), lambda j: (0, j))],
        out_shape=[jax.ShapeDtypeStruct((nb, d), F32), jax.ShapeDtypeStruct((nb, cols), F32)],
        compiler_params=_params(("arbitrary",)),
    )(c_all, w_ada, b_cols)


def _adamw(w, g, m, v):
    m = ADAM_B1 * m + (1.0 - ADAM_B1) * g
    v = ADAM_B2 * v + (1.0 - ADAM_B2) * (g * g)
    m_hat = m / (1.0 - ADAM_B1 ** ADAM_STEP)
    v_hat = v / (1.0 - ADAM_B2 ** ADAM_STEP)
    delta = -ADAM_LR * (m_hat / (jnp.sqrt(v_hat) + ADAM_EPS) + ADAM_WD * w)
    return delta, m, v


def _adam_outer(act_t, dmod, w, m, v, name, tr=256):
    rows, cols = w.shape
    nb = dmod.shape[0]

    def body(act_ref, dm_ref, w_ref, m_ref, v_ref, g_ref, d_ref, nm_ref, nv_ref):
        act = act_ref[...].astype(BF16).astype(F32)
        dm = dm_ref[...].astype(BF16).astype(F32)
        g = act[:, 0:1] * dm[0:1, :]
        for b in range(1, nb):
            g = g + act[:, b:b + 1] * dm[b:b + 1, :]
        g_ref[...] = g
        d_ref[...], nm_ref[...], nv_ref[...] = _adamw(w_ref[...], g, m_ref[...], v_ref[...])

    tile = pl.BlockSpec((tr, cols), lambda i: (i, 0))
    return _pc(
        body, name=name, grid=(rows // tr,),
        in_specs=[pl.BlockSpec((tr, nb), lambda i: (i, 0)), pl.BlockSpec((nb, cols), lambda i: (0, 0)), tile, tile, tile],
        out_specs=[tile] * 4, out_shape=[jax.ShapeDtypeStruct((rows, cols), F32)] * 4,
        compiler_params=_params(("parallel",)),
    )(act_t, dmod, w, m, v)


def _adam_reduce(parts, w, m, v, name, tr):
    rows, cols = w.shape

    def body(p_ref, w_ref, m_ref, v_ref, g_ref, d_ref, nm_ref, nv_ref):
        g = p_ref[0].astype(F32)
        for i in range(1, N_DEV):
            g = g + p_ref[i].astype(F32)
        g_ref[...] = g
        d_ref[...], nm_ref[...], nv_ref[...] = _adamw(w_ref[...], g, m_ref[...], v_ref[...])

    tile = pl.BlockSpec((tr, cols), lambda i: (i, 0))
    return _pc(
        body, name=name, grid=(rows // tr,),
        in_specs=[pl.BlockSpec((N_DEV, tr, cols), lambda i: (0, i, 0)), tile, tile, tile],
        out_specs=[tile] * 4, out_shape=[jax.ShapeDtypeStruct((rows, cols), F32)] * 4,
        compiler_params=_params(("parallel",)),
    )(parts, w, m, v)


def _ungather_cols(g):
    n, r, c = g.shape
    return jnp.transpose(g, (1, 0, 2)).reshape(r, n * c)


def _scatter_cols(w):
    r, c = w.shape
    return jnp.transpose(w.reshape(r, N_DEV, c // N_DEV), (1, 0, 2)).astype(BF16)


def _pad_rows(v):
    flat = v.reshape(-1)
    pad = (-flat.shape[0]) % (8 * BLOCK)
    return jnp.pad(flat, (0, pad)).reshape(-1, BLOCK)


def kernel(x, c, w_ada, b_ada, g_ffn1, w1_ffn1, w3_ffn1, w2_ffn1, g_mix, w_in, spatial_w, spatial_b, g_v, g_q, g_k, sinks, rel_bias, w_out, g_ffn2, w1_ffn2, w3_ffn2, w2_ffn2, loss_target, m_w_ada, m_b_ada, m_g_ffn1, m_w1_ffn1, m_w3_ffn1, m_w2_ffn1, m_g_mix, m_w_in, m_spatial_w, m_spatial_b, m_g_v, m_g_q, m_g_k, m_sinks, m_rel_bias, m_w_out, m_g_ffn2, m_w1_ffn2, m_w3_ffn2, m_w2_ffn2, v_w_ada, v_b_ada, v_g_ffn1, v_w1_ffn1, v_w3_ffn1, v_w2_ffn1, v_g_mix, v_w_in, v_spatial_w, v_spatial_b, v_g_v, v_g_q, v_g_k, v_sinks, v_rel_bias, v_w_out, v_g_ffn2, v_w1_ffn2, v_w3_ffn2, v_w2_ffn2):
    me = _linear(_position())
    d = D_MODEL
    x0 = x[0]
    target = loss_target[0]
    t = x0.shape[0]

    big = dict(w1a=w1_ffn1[0], w3a=w3_ffn1[0], w2a=w2_ffn1[0], w_in=w_in[0], w_out=w_out[0],
               w1b=w1_ffn2[0], w3b=w3_ffn2[0], w2b=w2_ffn2[0])
    col_sharded = ("w1a", "w3a", "w_in", "w1b", "w3b")
    first, later = ("w1a", "w3a", "w2a"), ("w_in", "w_out", "w1b", "w3b", "w2b")
    full = {}

    def unpack(keys, gathered):
        for k, gth in zip(keys, gathered):
            full[k] = _ungather_cols(gth) if k in col_sharded else gth.reshape(-1, gth.shape[-1])

    def share(k):
        gk_ = grads[k]
        if k in col_sharded:
            return _scatter_cols(gk_)
        return gk_.reshape(N_DEV, gk_.shape[0] // N_DEV, gk_.shape[1]).astype(BF16)

    unpack(first, _run_comm(_GatherTwoLevel([big[k].astype(BF16) for k in first]), "gather_ffn1"))

    c_all = _run_comm(_Direct([c], True), "gather_c")[0].reshape(N_DEV, d)
    b_cols = lax.dynamic_slice(b_ada, (0, me * MOD_COLS), (1, MOD_COLS))
    act_all, mod_cols = _ada_fwd(c_all, w_ada[0], b_cols, "ada_fwd")
    mod = _run_comm(_Direct([mod_cols.reshape(N_DEV, 1, MOD_COLS)], False), "scatter_mod")[0]
    mod = mod.reshape(N_MOD, 1, d)
    sh1, sc1, gt1, sh2, sc2, gt2, sh3, sc3, gt3 = [mod[i] for i in range(N_MOD)]

    h1 = _norm_fwd(x0, None, None, 0.0, g_ffn1, sc1, sh1, "norm1_fwd")
    (a1, b1, f1), gathered = _ffn_fwd(h1, full["w1a"], full["w3a"], full["w2a"], "ffn1_fwd",
                                      comm=_GatherTwoLevel([big[k].astype(BF16) for k in later]), relay_at=0.6)
    unpack(later, gathered)
    x1, h2 = _norm_fwd(x0, f1, gt1, 0.5, g_mix, sc2, sh2, "norm2_fwd")
    z = _matmul(h2, full["w_in"], "nn", 512, IN_COLS // 2, d, F32, "mix_in_fwd")
    onehot = _bucket_onehot()
    bias = _select_matmul(rel_bias.T, jnp.asarray(onehot, BF16), "bias_table").reshape(B_HEADS, BLOCK, 2 * BLOCK)
    sb_t = spatial_b[0].T
    mix_params = (spatial_w[0], sb_t, g_v[0], g_q, g_k, sinks, bias)
    ycat = _mix_fwd(z, *mix_params, "mix_core_fwd")
    y = _matmul(ycat, full["w_out"], "nn", 512, d, d, F32, "mix_out_fwd")
    x2, h3 = _norm_fwd(x1, y, gt2, 1.0, g_ffn2, sc3, sh3, "norm3_fwd")
    a2, b2, f2 = _ffn_fwd(h3, full["w1b"], full["w3b"], full["w2b"], "ffn2_fwd")
    dx3, df2, loss_part = _loss_head(x2, f2, gt3, 0.5, target, "loss_head")
    loss = lax.psum(loss_part[0, 0], ("x", "y", "c"))

    grads, parts = {}, {}
    da, db, s, dh3 = _ffn_bwd_dx(df2, a2, b2, full["w1b"], full["w3b"], full["w2b"], "ffn2_bwd_dx")
    grads["w1b"] = _matmul(h3, da, "tn", d, 512, 512, F32, "ffn2_dw1")
    grads["w3b"] = _matmul(h3, db, "tn", d, 512, 512, F32, "ffn2_dw3")
    grads["w2b"] = _matmul(s, df2, "tn", 512, d, 512, F32, "ffn2_dw2")
    dx2, dyg, sums3 = _norm_bwd(dh3, x2, dx3, f2, 0.5, g_ffn2, sc3, gt2, 1.0, "norm3_bwd")
    dycat = _matmul(dyg, full["w_out"], "nt", 512, d, d, F32, "mix_out_bwd")
    grads["w_out"] = _matmul(ycat, dyg, "tn", 1024, d, 512, F32, "mix_out_dw")
    mix_grads, moved = _mix_bwd(z, dycat, *mix_params, "mix_core_bwd",
                                comm=_Direct([share(k) for k in ("w1b", "w3b", "w2b")], False))
    parts["w1b"], parts["w3b"], parts["w2b"] = moved
    dz, d_sw, d_sb, d_gv, d_gq, d_gk, d_sink, d_bias = mix_grads
    d_rel = _select_matmul(d_bias.reshape(B_HEADS, BLOCK * 2 * BLOCK), jnp.asarray(onehot.T, BF16), "bias_table_bwd")
    dh2 = _matmul(dz, full["w_in"], "nt", 512, d, IN_COLS // 2, F32, "mix_in_bwd")
    grads["w_in"] = _matmul(h2, dz, "tn", 1024, IN_COLS // 2, 512, F32, "mix_in_dw")
    dx1, df1, sums2 = _norm_bwd(dh2, x1, dx2, y, 1.0, g_mix, sc2, gt1, 0.5, "norm2_bwd")
    (da, db, s, dh1), moved = _ffn_bwd_dx(df1, a1, b1, full["w1a"], full["w3a"], full["w2a"], "ffn1_bwd_dx",
                                          comm=_Direct([share("w_in"), share("w_out")], False))
    parts["w_in"], parts["w_out"] = moved
    grads["w1a"] = _matmul(h1, da, "tn", d, 512, 512, F32, "ffn1_dw1")
    grads["w3a"], moved = _matmul(h1, db, "tn", d, 512, 512, F32, "ffn1_dw3", comm=_Direct([share("w1a")], False))
    parts["w1a"] = moved[0]
    grads["w2a"], moved = _matmul(s, df1, "tn", 512, d, 512, F32, "ffn1_dw2", comm=_Direct([share("w3a")], False))
    parts["w3a"] = moved[0]
    (dx0, sums1), moved = _norm_bwd(dh1, x0, dx1, f1, 0.5, g_ffn1, sc1, None, 0.0, "norm1_bwd",
                                    comm=_Direct([share("w2a")], False))
    parts["w2a"] = moved[0]

    moments = dict(w1a=(m_w1_ffn1, v_w1_ffn1), w3a=(m_w3_ffn1, v_w3_ffn1), w2a=(m_w2_ffn1, v_w2_ffn1),
                   w_in=(m_w_in, v_w_in), w_out=(m_w_out, v_w_out),
                   w1b=(m_w1_ffn2, v_w1_ffn2), w3b=(m_w3_ffn2, v_w3_ffn2), w2b=(m_w2_ffn2, v_w2_ffn2))
    upd = {}
    for k in big:
        rows = big[k].shape[0]
        tr = 256 if rows % 256 == 0 else rows // 4
        upd[k] = [o[None] for o in _adam_reduce(parts[k], big[k], moments[k][0][0], moments[k][1][0], "adam_" + k, tr)]

    dmod = jnp.concatenate([sums1[1:2], sums1[0:1], sums1[3:4], sums2[1:2], sums2[0:1], sums2[3:4],
                            sums3[1:2], sums3[0:1], sums3[3:4]], axis=1)
    small = [("b_ada", dmod, b_ada, m_b_ada, v_b_ada),
             ("g_ffn1", sums1[2:3], g_ffn1, m_g_ffn1, v_g_ffn1),
             ("g_mix", sums2[2:3], g_mix, m_g_mix, v_g_mix),
             ("spatial_w", d_sw[None], spatial_w, m_spatial_w, v_spatial_w),
             ("spatial_b", d_sb[:, :A_HEADS].T[None], spatial_b, m_spatial_b, v_spatial_b),
             ("g_v", d_gv[None], g_v, m_g_v, v_g_v),
             ("g_q", d_gq, g_q, m_g_q, v_g_q),
             ("g_k", d_gk, g_k, m_g_k, v_g_k),
             ("sinks", d_sink[:, :B_HEADS], sinks, m_sinks, v_sinks),
             ("rel_bias", d_rel.T, rel_bias, m_rel_bias, v_rel_bias),
             ("g_ffn2", sums3[2:3], g_ffn2, m_g_ffn2, v_g_ffn2)]
    packed = [jnp.concatenate([_pad_rows(item[i]) for item in small], axis=0) for i in range(1, 5)]
    g_all = _run_comm(_Direct([packed[0]], True), "gather_small")[0]
    small_out = _adam_reduce(g_all, packed[1], packed[2], packed[3], "adam_small", packed[0].shape[0])
    off = 0
    for name, g_, w_, _, _ in small:
        n_rows = _pad_rows(w_).shape[0]
        upd[name] = [o[off:off + n_rows].reshape(-1)[:w_.size].reshape(w_.shape) for o in small_out]
        off += n_rows

    dmod_rows = g_all[:, :N_MOD * d // BLOCK, :].reshape(N_DEV, N_MOD * d)
    dmod_cols = lax.dynamic_slice(dmod_rows, (0, me * MOD_COLS), (N_DEV, MOD_COLS))
    upd["w_ada"] = [o[None] for o in _adam_outer(act_all.T, dmod_cols, w_ada[0], m_w_ada[0], v_w_ada[0], "adam_w_ada")]

    order = [("w_ada", "w_ada"), ("b_ada", "b_ada"), ("g_ffn1", "g_ffn1"), ("w1_ffn1", "w1a"), ("w3_ffn1", "w3a"),
             ("w2_ffn1", "w2a"), ("g_mix", "g_mix"), ("w_in", "w_in"), ("spatial_w", "spatial_w"),
             ("spatial_b", "spatial_b"), ("g_v", "g_v"), ("g_q", "g_q"), ("g_k", "g_k"), ("sinks", "sinks"),
             ("rel_bias", "rel_bias"), ("w_out", "w_out"), ("g_ffn2", "g_ffn2"), ("w1_ffn2", "w1b"),
             ("w3_ffn2", "w3b"), ("w2_ffn2", "w2b")]
    outs = [loss, dx0[None]]
    for i in range(4):
        outs += [upd[key][i] for _, key in order]
    return tuple(outs)
```

```python
import functools
import math

import numpy as np
import jax
import jax.numpy as jnp
from jax import lax
from jax.experimental import pallas as pl
from jax.experimental.pallas import tpu as pltpu

F32 = jnp.float32
BF16 = jnp.bfloat16

D_MODEL = 2048
D_FF = 5632
BLOCK = 128
A_HEADS = 8
A_WIDTH = 1024
B_HEADS = 16
B_KV_HEADS = 2
GROUP = B_HEADS // B_KV_HEADS
HEAD_DIM = 64
B_WIDTH = 1024
KV_WIDTH = 128
IN_COLS = 3328
Q_OFF = 2 * A_WIDTH
K_OFF = Q_OFF + B_WIDTH
V_OFF = K_OFF + KV_WIDTH
N_BUCKETS = 32
N_MOD = 9
EPS = 1e-6
N_DEV = 8
MOD_COLS = N_MOD * D_MODEL // N_DEV

ADAM_LR = 0.001
ADAM_B1 = 0.9
ADAM_B2 = 0.999
ADAM_EPS = 1e-08
ADAM_WD = 0.01
ADAM_STEP = 10

DW_TOKENS = 4096
MASK_VALUE = -1e30
VMEM_LIMIT = 56 * 1024 * 1024
MESH_ID = pl.DeviceIdType.MESH
ANY = pl.BlockSpec(memory_space=pl.ANY)

_SQRT_HALF = 0.7071067811865476
_INV_SQRT_2PI = 0.3989422804014327


def _pc(body, **kw):
    return pl.pallas_call(body, **kw)


def _params(sem=None):
    return pltpu.CompilerParams(dimension_semantics=sem, vmem_limit_bytes=VMEM_LIMIT)


def _dot(a, b):
    return lax.dot_general(a, b, (((1,), (0,)), ((), ())), preferred_element_type=F32)


def _dot_nt(a, b):
    return lax.dot_general(a, b, (((1,), (1,)), ((), ())), preferred_element_type=F32)


def _dot_tn(a, b):
    return lax.dot_general(a, b, (((0,), (0,)), ((), ())), preferred_element_type=F32)


def _gelu(x):
    return 0.5 * x * (1.0 + lax.erf(x * _SQRT_HALF))


def _gelu_grad(x):
    return 0.5 * (1.0 + lax.erf(x * _SQRT_HALF)) + x * jnp.exp(-0.5 * x * x) * _INV_SQRT_2PI


def _rms(x):
    r = lax.rsqrt(jnp.mean(x * x, axis=-1, keepdims=True) + EPS)
    return x * r, r


def _rms_bwd(dy, y, r):
    return r * (dy - y * jnp.mean(dy * y, axis=-1, keepdims=True))


def _row_spec(tm, cols):
    return pl.BlockSpec((tm, cols), lambda i: (i, 0))


def _vec_spec(cols):
    return pl.BlockSpec((1, cols), lambda i: (0, 0))


def _position():
    x, y, c = lax.axis_index("x"), lax.axis_index("y"), lax.axis_index("c")
    return x, y, c


def _linear(p):
    return 4 * p[0] + 2 * p[1] + p[2]


class _Comm:
    PEER_COPIES = N_DEV - 1

    def __init__(self, arrs):
        self.arrs = list(arrs)
        n = len(self.arrs)
        self.scratch_shapes = [pltpu.SemaphoreType.DMA((self.PEER_COPIES * n,)),
                               pltpu.SemaphoreType.DMA((self.PEER_COPIES * n,)),
                               pltpu.SemaphoreType.DMA((n,))]

    def bind(self, srcs, dsts, sems):
        self.srcs, self.dsts = srcs, dsts
        self.send_sems, self.recv_sems, self.local_sems = sems
        x, y, c = _position()
        self.me, self.sibling, self.core = (x, y, c), (x, y, 1 - c), c
        self.chips = [(1 - x, y), (x, 1 - y), (1 - x, 1 - y)]
        self.peers = [(1 - x if k & 4 else x, 1 - y if k & 2 else y, 1 - c if k & 1 else c)
                      for k in range(1, N_DEV)]

    def relay(self):
        pass


class _GatherTwoLevel(_Comm):
    def __init__(self, arrs):
        super().__init__(arrs)
        self.out_shape = [jax.ShapeDtypeStruct((N_DEV,) + a.shape, a.dtype) for a in self.arrs]

    def _copy(self, a, k, block, to, from_input=False):
        rows = self.dsts[a].at[_linear(block)]
        return pltpu.make_async_remote_copy(
            src_ref=self.srcs[a] if from_input else rows, dst_ref=rows,
            send_sem=self.send_sems.at[self.PEER_COPIES * a + k], recv_sem=self.recv_sems.at[self.PEER_COPIES * a + k],
            device_id=to, device_id_type=MESH_ID)

    def _local(self, a):
        return pltpu.make_async_copy(self.srcs[a], self.dsts[a].at[_linear(self.me)], self.local_sems.at[a])

    def _first(self, a):
        return [self._copy(a, 0, self.me, self.sibling, True)] + [
            self._copy(a, 1 + j, self.me, (*chip, self.core), True) for j, chip in enumerate(self.chips)]

    def _passed(self, a):
        return [self._copy(a, 4 + j, (*chip, self.core), self.sibling) for j, chip in enumerate(self.chips)]

    def start(self):
        for a in range(len(self.arrs)):
            self._local(a).start()
            for cp in self._first(a):
                cp.start()

    def relay(self):
        for a in range(len(self.arrs)):
            for j, chip in enumerate(self.chips):
                self._copy(a, 1 + j, (*chip, self.core), self.me).wait_recv()
                self._passed(a)[j].start()

    def finish(self):
        for a in range(len(self.arrs)):
            self._copy(a, 0, self.sibling, self.me).wait_recv()
            for j, chip in enumerate(self.chips):
                self._copy(a, 4 + j, (*chip, 1 - self.core), self.me).wait_recv()
        for a in range(len(self.arrs)):
            for cp in self._first(a) + self._passed(a):
                cp.wait_send()
            self._local(a).wait()


class _Direct(_Comm):
    def __init__(self, arrs, broadcast):
        super().__init__(arrs)
        self.broadcast = broadcast
        self.out_shape = [jax.ShapeDtypeStruct(((N_DEV,) + a.shape) if broadcast else a.shape, a.dtype)
                          for a in self.arrs]

    def _outgoing(self, a, to):
        return self.srcs[a] if self.broadcast else self.srcs[a].at[_linear(to)]

    def _copy(self, a, k, sender, to):
        return pltpu.make_async_remote_copy(
            src_ref=self._outgoing(a, to), dst_ref=self.dsts[a].at[_linear(sender)],
            send_sem=self.send_sems.at[self.PEER_COPIES * a + k], recv_sem=self.recv_sems.at[self.PEER_COPIES * a + k],
            device_id=to, device_id_type=MESH_ID)

    def _local(self, a):
        return pltpu.make_async_copy(self._outgoing(a, self.me), self.dsts[a].at[_linear(self.me)],
                                     self.local_sems.at[a])

    def start(self):
        for a in range(len(self.arrs)):
            self._local(a).start()
            for k, peer in enumerate(self.peers):
                self._copy(a, k, self.me, peer).start()

    def finish(self):
        for a in range(len(self.arrs)):
            for k, peer in enumerate(self.peers):
                self._copy(a, k, peer, self.me).wait_recv()
        for a in range(len(self.arrs)):
            for k, peer in enumerate(self.peers):
                self._copy(a, k, self.me, peer).wait_send()
            self._local(a).wait()


def _run_comm(comm, name):
    n = len(comm.arrs)

    def body(*refs):
        comm.bind(refs[:n], refs[n:2 * n], refs[2 * n:])
        comm.start()
        comm.relay()
        comm.finish()

    return _pc(body, name=name, out_shape=comm.out_shape, in_specs=[ANY] * n, out_specs=[ANY] * n,
               scratch_shapes=comm.scratch_shapes)(*comm.arrs)


def _call(body, inputs, *, name, grid, in_specs, out_specs, out_shape, scratch_shapes=(), semantics,
          comm=None, relay_at=0.5):
    if comm is None:
        out = _pc(body, name=name, grid=grid, in_specs=in_specs, out_specs=out_specs, out_shape=out_shape,
                  scratch_shapes=list(scratch_shapes), compiler_params=_params(semantics))(*inputs)
        return list(out), []
    n_in, n_out, n_sc, k = len(in_specs), len(out_specs), len(scratch_shapes), len(comm.arrs)
    steps = math.prod(grid)
    relay_step = min(steps - 1, int(steps * relay_at))

    def carrier(*refs):
        ins, rest = refs[:n_in], refs[n_in:]
        csrc, rest = rest[:k], rest[k:]
        outs, rest = rest[:n_out], rest[n_out:]
        cdst, rest = rest[:k], rest[k:]
        scratch, csems = rest[:n_sc], rest[n_sc:]
        step = 0
        for axis, size in enumerate(grid):
            step = step * size + pl.program_id(axis)
        comm.bind(csrc, cdst, csems)
        pl.when(step == 0)(comm.start)
        pl.when(step == relay_step)(comm.relay)
        body(*ins, *outs, *scratch)
        pl.when(step == steps - 1)(comm.finish)

    out = _pc(carrier, name=name, grid=grid, in_specs=list(in_specs) + [ANY] * k,
              out_specs=list(out_specs) + [ANY] * k, out_shape=list(out_shape) + comm.out_shape,
              scratch_shapes=list(scratch_shapes) + comm.scratch_shapes,
              compiler_params=_params(("arbitrary",) * len(grid)))(*inputs, *comm.arrs)
    return list(out[:n_out]), list(out[n_out:])


def _matmul(a, b, mode, tm, tn, tk, out_dtype, name, comm=None):
    if mode == "nn":
        (m, kk), nn = a.shape, b.shape[1]
        a_spec = pl.BlockSpec((tm, tk), lambda i, j, k: (i, k))
        b_spec = pl.BlockSpec((tk, tn), lambda i, j, k: (k, j))
        dot = _dot
    elif mode == "nt":
        (m, kk), nn = a.shape, b.shape[0]
        a_spec = pl.BlockSpec((tm, tk), lambda i, j, k: (i, k))
        b_spec = pl.BlockSpec((tn, tk), lambda i, j, k: (j, k))
        dot = _dot_nt
    else:
        (kk, m), nn = a.shape, b.shape[1]
        a_spec = pl.BlockSpec((tk, tm), lambda i, j, k: (k, i))
        b_spec = pl.BlockSpec((tk, tn), lambda i, j, k: (k, j))
        dot = _dot_tn
    assert m % tm == 0 and nn % tn == 0 and kk % tk == 0, (a.shape, b.shape, tm, tn, tk)
    nk = kk // tk
    assert nk == 1 or out_dtype == F32

    def body(a_ref, b_ref, o_ref):
        p = dot(a_ref[...], b_ref[...])
        if nk == 1:
            o_ref[...] = p.astype(o_ref.dtype)
        else:
            k = pl.program_id(2)

            @pl.when(k == 0)
            def _():
                o_ref[...] = p

            @pl.when(k > 0)
            def _():
                o_ref[...] += p

    out, moved = _call(
        body, (a, b), name=name, grid=(m // tm, nn // tn, nk),
        in_specs=[a_spec, b_spec], out_specs=[pl.BlockSpec((tm, tn), lambda i, j, k: (i, j))],
        out_shape=[jax.ShapeDtypeStruct((m, nn), out_dtype)],
        semantics=("parallel", "parallel", "arbitrary"), comm=comm)
    return out[0] if comm is None else (out[0], moved)


def _norm_fwd(x_prev, f_prev, gate_prev, coef, g, sc, sh, name, tm=256):
    t, d = x_prev.shape
    residual = f_prev is not None

    def body(*refs):
        if residual:
            xp_ref, fp_ref, gp_ref, g_ref, sc_ref, sh_ref, x_ref, h_ref = refs
            x = xp_ref[...] + coef * gp_ref[...] * fp_ref[...]
            x_ref[...] = x
        else:
            xp_ref, g_ref, sc_ref, sh_ref, h_ref = refs
            x = xp_ref[...]
        y, _ = _rms(x)
        h_ref[...] = ((y * g_ref[...]) * (1.0 + sc_ref[...]) + sh_ref[...]).astype(BF16)

    row, vec = _row_spec(tm, d), _vec_spec(d)
    if residual:
        ins, in_specs = (x_prev, f_prev, gate_prev, g, sc, sh), [row, row, vec, vec, vec, vec]
        out_shape = [jax.ShapeDtypeStruct((t, d), F32), jax.ShapeDtypeStruct((t, d), BF16)]
        out_specs = [row, row]
    else:
        ins, in_specs = (x_prev, g, sc, sh), [row, vec, vec, vec]
        out_shape = [jax.ShapeDtypeStruct((t, d), BF16)]
        out_specs = [row]
    out = _pc(body, name=name, grid=(t // tm,), in_specs=in_specs, out_specs=out_specs, out_shape=out_shape,
              compiler_params=_params(("parallel",)))(*ins)
    return out if residual else out[0]


def _norm_bwd(dh, x, dxo, fo, coef_o, g, sc, gate_prev, coef_prev, name, tm=256, comm=None):
    t, d = x.shape
    with_prev = gate_prev is not None

    def body(*refs):
        if with_prev:
            dh_ref, x_ref, dxo_ref, fo_ref, g_ref, sc_ref, gp_ref, dx_ref, dfp_ref, sums_ref = refs
        else:
            dh_ref, x_ref, dxo_ref, fo_ref, g_ref, sc_ref, dx_ref, sums_ref = refs
        dh_v, dxo_v = dh_ref[...], dxo_ref[...]
        y, r = _rms(x_ref[...])
        n = y * g_ref[...]
        dn = dh_v * (1.0 + sc_ref[...])
        dx = dxo_v + _rms_bwd(dn * g_ref[...], y, r)
        dx_ref[...] = dx
        if with_prev:
            dfp_ref[...] = (coef_prev * gp_ref[...] * dx).astype(BF16)
        @pl.when(pl.program_id(0) == 0)
        def _():
            sums_ref[...] = jnp.zeros_like(sums_ref)

        sums_ref[0:1, :] += jnp.sum(dh_v * n, axis=0, keepdims=True)
        sums_ref[1:2, :] += jnp.sum(dh_v, axis=0, keepdims=True)
        sums_ref[2:3, :] += jnp.sum(dn * y, axis=0, keepdims=True)
        sums_ref[3:4, :] += jnp.sum(coef_o * dxo_v * fo_ref[...], axis=0, keepdims=True)

    row, vec = _row_spec(tm, d), _vec_spec(d)
    sums_spec = pl.BlockSpec((8, d), lambda i: (0, 0))
    ins, in_specs = [dh, x, dxo, fo, g, sc], [row, row, row, row, vec, vec]
    out_shape, out_specs = [jax.ShapeDtypeStruct((t, d), F32)], [row]
    if with_prev:
        ins.append(gate_prev)
        in_specs.append(vec)
        out_shape.append(jax.ShapeDtypeStruct((t, d), BF16))
        out_specs.append(row)
    out_shape.append(jax.ShapeDtypeStruct((8, d), F32))
    out_specs.append(sums_spec)
    out, moved = _call(body, ins, name=name, grid=(t // tm,), in_specs=in_specs, out_specs=out_specs,
                       out_shape=out_shape, semantics=("arbitrary",), comm=comm)
    return out if comm is None else (out, moved)


def _ffn_fwd(h, w1, w3, w2, name, tm=512, tf=512, comm=None, relay_at=0.5):
    t, d = h.shape
    ff = w1.shape[1]

    def body(h_ref, w1_ref, w3_ref, w2_ref, a_ref, b_ref, f_ref):
        j = pl.program_id(1)
        hv = h_ref[...]
        a = _dot(hv, w1_ref[...])
        b = _dot(hv, w3_ref[...])
        a_ref[...] = a
        b_ref[...] = b
        s = (a * jax.nn.sigmoid(a) * b).astype(BF16)
        p = _dot(s, w2_ref[...])

        @pl.when(j == 0)
        def _():
            f_ref[...] = p

        @pl.when(j > 0)
        def _():
            f_ref[...] += p

    tile = pl.BlockSpec((tm, tf), lambda i, j: (i, j))
    rows = pl.BlockSpec((tm, d), lambda i, j: (i, 0))
    out, moved = _call(
        body, (h, w1, w3, w2), name=name, grid=(t // tm, ff // tf),
        in_specs=[rows, pl.BlockSpec((d, tf), lambda i, j: (0, j)), pl.BlockSpec((d, tf), lambda i, j: (0, j)),
                  pl.BlockSpec((tf, d), lambda i, j: (j, 0))],
        out_specs=[tile, tile, rows],
        out_shape=[jax.ShapeDtypeStruct((t, ff), F32), jax.ShapeDtypeStruct((t, ff), F32),
                   jax.ShapeDtypeStruct((t, d), F32)],
        semantics=("parallel", "arbitrary"), comm=comm, relay_at=relay_at)
    return out if comm is None else (out, moved)


def _ffn_bwd_dx(df, a, b, w1, w3, w2, name, tm=512, tf=512, comm=None):
    t, d = df.shape
    ff = a.shape[1]

    def body(df_ref, a_ref, b_ref, w1_ref, w3_ref, w2_ref, da_ref, db_ref, s_ref, dh_ref):
        j = pl.program_id(1)
        ds = _dot_nt(df_ref[...], w2_ref[...])
        av, bv = a_ref[...], b_ref[...]
        sg = jax.nn.sigmoid(av)
        sil = av * sg
        s_ref[...] = (sil * bv).astype(BF16)
        da = (ds * bv * (sg * (1.0 + av * (1.0 - sg)))).astype(BF16)
        db = (ds * sil).astype(BF16)
        da_ref[...] = da
        db_ref[...] = db
        p = _dot_nt(da, w1_ref[...]) + _dot_nt(db, w3_ref[...])

        @pl.when(j == 0)
        def _():
            dh_ref[...] = p

        @pl.when(j > 0)
        def _():
            dh_ref[...] += p

    tile = pl.BlockSpec((tm, tf), lambda i, j: (i, j))
    rows = pl.BlockSpec((tm, d), lambda i, j: (i, 0))
    out, moved = _call(
        body, (df, a, b, w1, w3, w2), name=name, grid=(t // tm, ff // tf),
        in_specs=[rows, tile, tile, pl.BlockSpec((d, tf), lambda i, j: (0, j)),
                  pl.BlockSpec((d, tf), lambda i, j: (0, j)), pl.BlockSpec((tf, d), lambda i, j: (j, 0))],
        out_specs=[tile, tile, tile, rows],
        out_shape=[jax.ShapeDtypeStruct((t, ff), BF16)] * 3 + [jax.ShapeDtypeStruct((t, d), F32)],
        semantics=("parallel", "arbitrary"), comm=comm)
    return out if comm is None else (out, moved)


def _loss_head(x_prev, f_prev, gate_prev, coef, target, name, tm=256):
    t, d = x_prev.shape
    steps = t // tm

    def body(xp_ref, fp_ref, gp_ref, tg_ref, dy_ref, df_ref, loss_ref, acc_ref):
        i = pl.program_id(0)
        e = xp_ref[...] + coef * gp_ref[...] * fp_ref[...] - tg_ref[...]
        dy = e * (1.0 / d)
        dy_ref[...] = dy
        df_ref[...] = (coef * gp_ref[...] * dy).astype(BF16)
        part = jnp.sum(e * e, axis=0, keepdims=True)

        @pl.when(i == 0)
        def _():
            acc_ref[...] = part

        @pl.when(i > 0)
        def _():
            acc_ref[...] += part

        @pl.when(i == steps - 1)
        def _():
            loss_ref[...] = jnp.sum(acc_ref[...], axis=1, keepdims=True) * (0.5 / d)

    row, vec = _row_spec(tm, d), _vec_spec(d)
    return _pc(
        body, name=name, grid=(steps,),
        in_specs=[row, row, vec, row], out_specs=[row, row, pl.BlockSpec((1, 1), lambda i: (0, 0))],
        out_shape=[jax.ShapeDtypeStruct((t, d), F32), jax.ShapeDtypeStruct((t, d), BF16),
                   jax.ShapeDtypeStruct((1, 1), F32)],
        scratch_shapes=[pltpu.VMEM((1, d), F32)],
        compiler_params=_params(("arbitrary",)),
    )(x_prev, f_prev, gate_prev, target)


def _split3(x):
    hi = x.astype(BF16)
    r1 = x - hi.astype(F32)
    mid = r1.astype(BF16)
    lo = (r1 - mid.astype(F32)).astype(BF16)
    return hi, mid, lo


def _select_matmul(a, onehot, name):
    m, n = a.shape[0], onehot.shape[1]

    def body(a_ref, oh_ref, o_ref):
        hi, mid, lo = _split3(a_ref[...])
        oh = oh_ref[...]
        o_ref[...] = (_dot(hi, oh) + _dot(mid, oh)) + _dot(lo, oh)

    return _pc(body, name=name, out_shape=jax.ShapeDtypeStruct((m, n), F32), compiler_params=_params())(a, onehot)


def _bucket_onehot():
    qi = np.arange(BLOCK)[:, None]
    kj = np.arange(2 * BLOCK)[None, :]
    dist = np.clip(qi + BLOCK - kj, 0, None)
    nf = np.maximum(dist, 1).astype(np.float32)
    large = 16 + (np.log(nf / np.float32(16)) / np.float32(math.log(128 / 16)) * np.float32(16)).astype(np.int32)
    bucket = np.where(dist < 16, dist, np.minimum(large, N_BUCKETS - 1)).reshape(-1)
    return (bucket[None, :] == np.arange(N_BUCKETS)[:, None]).astype(np.float32)


def _window_mask(n):
    qi = lax.broadcasted_iota(jnp.int32, (BLOCK, 2 * BLOCK), 0)
    kj = lax.broadcasted_iota(jnp.int32, (BLOCK, 2 * BLOCK), 1)
    return (kj > qi) & (kj <= qi + BLOCK) & ((kj >= BLOCK) | (n > 0))


def _mix_specs(nb, order):
    cur = pl.BlockSpec((BLOCK, IN_COLS), lambda n: (order(n), 0))
    prev = pl.BlockSpec((BLOCK, 2 * KV_WIDTH), lambda n: (jnp.maximum(order(n) - 1, 0), K_OFF // (2 * KV_WIDTH)))
    full = lambda shape: pl.BlockSpec(shape, lambda n: (0,) * len(shape))
    params = [full((A_HEADS, BLOCK, BLOCK)), full((BLOCK, A_HEADS)), full((A_HEADS, BLOCK)),
              full((1, HEAD_DIM)), full((1, HEAD_DIM)), pl.BlockSpec(memory_space=pltpu.SMEM),
              full((B_HEADS, BLOCK, 2 * BLOCK))]
    return cur, prev, params, full


def _mix_fwd(z, sw, sb_t, gv, gq, gk, sinks, bias, name):
    t = z.shape[0]
    nb = t // BLOCK

    def body(zc_ref, zp_ref, sw_ref, sbt_ref, gv_ref, gq_ref, gk_ref, sink_ref, bias_ref, y_ref):
        n = pl.program_id(0)
        ri = lax.broadcasted_iota(jnp.int32, (BLOCK, BLOCK), 0)
        ci = lax.broadcasted_iota(jnp.int32, (BLOCK, BLOCK), 1)
        tril = ri >= ci
        for h in range(A_HEADS):
            u = _gelu(zc_ref[:, h * BLOCK:(h + 1) * BLOCK])
            vv = _gelu(zc_ref[:, A_WIDTH + h * BLOCK:A_WIDTH + (h + 1) * BLOCK])
            vhat, _ = _rms(vv)
            vn = (vhat * gv_ref[h:h + 1, :]).astype(BF16)
            w = jnp.where(tril, sw_ref[h], 0.0).astype(BF16)
            mixed = _dot(w, vn) + sbt_ref[:, h:h + 1]
            y_ref[:, h * BLOCK:(h + 1) * BLOCK] = (u * mixed).astype(BF16)
        valid = _window_mask(n)
        for kvh in range(B_KV_HEADS):
            ko, vo = K_OFF + kvh * HEAD_DIM, V_OFF + kvh * HEAD_DIM
            k_raw = jnp.concatenate([zp_ref[:, kvh * HEAD_DIM:(kvh + 1) * HEAD_DIM],
                                     zc_ref[:, ko:ko + HEAD_DIM]], axis=0)
            v_raw = jnp.concatenate([zp_ref[:, KV_WIDTH + kvh * HEAD_DIM:KV_WIDTH + (kvh + 1) * HEAD_DIM],
                                     zc_ref[:, vo:vo + HEAD_DIM]], axis=0)
            khat, _ = _rms(k_raw)
            kband = (khat * gk_ref[...]).astype(BF16)
            vband = v_raw.astype(BF16)
            for g in range(GROUP):
                hd = kvh * GROUP + g
                qo = Q_OFF + hd * HEAD_DIM
                qhat, _ = _rms(zc_ref[:, qo:qo + HEAD_DIM])
                qn = (qhat * gq_ref[...]).astype(BF16)
                s = _dot_nt(qn, kband) * (HEAD_DIM ** -0.5) + bias_ref[hd]
                s = jnp.where(valid, s, MASK_VALUE)
                sink = sink_ref[0, hd]
                m = jnp.maximum(jnp.max(s, axis=-1, keepdims=True), sink)
                p = jnp.exp(s - m)
                den = jnp.sum(p, axis=-1, keepdims=True) + jnp.exp(sink - m)
                o = _dot((p / den).astype(BF16), vband)
                yo = A_WIDTH + hd * HEAD_DIM
                y_ref[:, yo:yo + HEAD_DIM] = o.astype(BF16)

    cur, prev, params, _ = _mix_specs(nb, lambda n: n)
    return _pc(
        body, name=name, grid=(nb,), in_specs=[cur, prev] + params,
        out_specs=pl.BlockSpec((BLOCK, A_WIDTH + B_WIDTH), lambda n: (n, 0)),
        out_shape=jax.ShapeDtypeStruct((t, A_WIDTH + B_WIDTH), BF16),
        compiler_params=_params(("arbitrary",)),
    )(z, z, sw, sb_t, gv, gq, gk, sinks, bias)


def _mix_bwd(z, dy, sw, sb_t, gv, gq, gk, sinks, bias, name, comm=None):
    t = z.shape[0]
    nb = t // BLOCK

    def body(zc_ref, zp_ref, dy_ref, sw_ref, sbt_ref, gv_ref, gq_ref, gk_ref, sink_ref, bias_ref,
             dz_ref, dsw_ref, dsb_ref, dgv_ref, dgq_ref, dgk_ref, dsink_ref, dbias_ref, carry_ref):
        step = pl.program_id(0)
        n = nb - 1 - step

        @pl.when(step == 0)
        def _():
            dsw_ref[...] = jnp.zeros_like(dsw_ref)
            dsb_ref[...] = jnp.zeros_like(dsb_ref)
            dgv_ref[...] = jnp.zeros_like(dgv_ref)
            dgq_ref[...] = jnp.zeros_like(dgq_ref)
            dgk_ref[...] = jnp.zeros_like(dgk_ref)
            dsink_ref[...] = jnp.zeros_like(dsink_ref)
            dbias_ref[...] = jnp.zeros_like(dbias_ref)
            carry_ref[...] = jnp.zeros_like(carry_ref)

        ri = lax.broadcasted_iota(jnp.int32, (BLOCK, BLOCK), 0)
        ci = lax.broadcasted_iota(jnp.int32, (BLOCK, BLOCK), 1)
        tril = ri >= ci
        dsb = jnp.zeros((BLOCK, BLOCK), F32)
        for h in range(A_HEADS):
            uo, vo = h * BLOCK, A_WIDTH + h * BLOCK
            u_raw, v_raw = zc_ref[:, uo:uo + BLOCK], zc_ref[:, vo:vo + BLOCK]
            u, vv = _gelu(u_raw), _gelu(v_raw)
            vhat, r = _rms(vv)
            vn = (vhat * gv_ref[h:h + 1, :]).astype(BF16)
            w = jnp.where(tril, sw_ref[h], 0.0).astype(BF16)
            mixed = _dot(w, vn) + sbt_ref[:, h:h + 1]
            dya = dy_ref[:, uo:uo + BLOCK]
            dmixed = dya * u
            dm16 = dmixed.astype(BF16)
            dsw_ref[h] += jnp.where(tril, _dot_nt(dm16, vn), 0.0)
            dsb = dsb + jnp.where(ci == h, jnp.sum(dmixed, axis=1, keepdims=True), 0.0)
            dvn = _dot_tn(w, dm16)
            dgv_ref[h:h + 1, :] += jnp.sum(dvn * vhat, axis=0, keepdims=True)
            dvv = _rms_bwd(dvn * gv_ref[h:h + 1, :], vhat, r)
            dz_ref[:, uo:uo + BLOCK] = (dya * mixed * _gelu_grad(u_raw)).astype(BF16)
            dz_ref[:, vo:vo + BLOCK] = (dvv * _gelu_grad(v_raw)).astype(BF16)
        dsb_ref[...] += dsb

        valid = _window_mask(n)
        lane = lax.broadcasted_iota(jnp.int32, (BLOCK, BLOCK), 1)
        dsink = jnp.zeros((BLOCK, BLOCK), F32)
        dgq = jnp.zeros((1, HEAD_DIM), F32)
        for kvh in range(B_KV_HEADS):
            ko, vo = K_OFF + kvh * HEAD_DIM, V_OFF + kvh * HEAD_DIM
            k_raw = jnp.concatenate([zp_ref[:, kvh * HEAD_DIM:(kvh + 1) * HEAD_DIM],
                                     zc_ref[:, ko:ko + HEAD_DIM]], axis=0)
            v_raw = jnp.concatenate([zp_ref[:, KV_WIDTH + kvh * HEAD_DIM:KV_WIDTH + (kvh + 1) * HEAD_DIM],
                                     zc_ref[:, vo:vo + HEAD_DIM]], axis=0)
            khat, kr = _rms(k_raw)
            kband = (khat * gk_ref[...]).astype(BF16)
            vband = v_raw.astype(BF16)
            dkn = jnp.zeros((2 * BLOCK, HEAD_DIM), F32)
            dvb = jnp.zeros((2 * BLOCK, HEAD_DIM), F32)
            for g in range(GROUP):
                hd = kvh * GROUP + g
                qo = Q_OFF + hd * HEAD_DIM
                qhat, qr = _rms(zc_ref[:, qo:qo + HEAD_DIM])
                qn = (qhat * gq_ref[...]).astype(BF16)
                s = _dot_nt(qn, kband) * (HEAD_DIM ** -0.5) + bias_ref[hd]
                s = jnp.where(valid, s, MASK_VALUE)
                sink = sink_ref[0, hd]
                m = jnp.maximum(jnp.max(s, axis=-1, keepdims=True), sink)
                p = jnp.exp(s - m)
                esink = jnp.exp(sink - m)
                den = jnp.sum(p, axis=-1, keepdims=True) + esink
                w = p / den
                w16 = w.astype(BF16)
                do = dy_ref[:, A_WIDTH + hd * HEAD_DIM:A_WIDTH + (hd + 1) * HEAD_DIM].astype(BF16)
                dvb = dvb + _dot_tn(w16, do)
                dw = _dot_nt(do, vband)
                rowdot = jnp.sum(w * dw, axis=-1, keepdims=True)
                ds = w * (dw - rowdot)
                dsink = dsink + jnp.where(lane == hd, -(esink / den) * rowdot, 0.0)
                dbias_ref[hd] += ds
                ds16 = (ds * (HEAD_DIM ** -0.5)).astype(BF16)
                dqn = _dot(ds16, kband)
                dkn = dkn + _dot_tn(ds16, qn)
                dgq = dgq + jnp.sum(dqn * qhat, axis=0, keepdims=True)
                dz_ref[:, qo:qo + HEAD_DIM] = _rms_bwd(dqn * gq_ref[...], qhat, qr).astype(BF16)
            dgk_ref[...] += jnp.sum(dkn * khat, axis=0, keepdims=True)
            dk = _rms_bwd(dkn * gk_ref[...], khat, kr)
            co = kvh * HEAD_DIM
            dz_ref[:, ko:ko + HEAD_DIM] = (dk[BLOCK:] + carry_ref[:, co:co + HEAD_DIM]).astype(BF16)
            dz_ref[:, vo:vo + HEAD_DIM] = (dvb[BLOCK:] + carry_ref[:, KV_WIDTH + co:KV_WIDTH + co + HEAD_DIM]).astype(BF16)
            carry_ref[:, co:co + HEAD_DIM] = dk[:BLOCK]
            carry_ref[:, KV_WIDTH + co:KV_WIDTH + co + HEAD_DIM] = dvb[:BLOCK]
        dgq_ref[...] += dgq
        dsink_ref[...] += jnp.sum(dsink, axis=0, keepdims=True)

    order = lambda n: nb - 1 - n
    cur, prev, params, full = _mix_specs(nb, order)
    dy_spec = pl.BlockSpec((BLOCK, A_WIDTH + B_WIDTH), lambda n: (order(n), 0))
    out, moved = _call(
        body, (z, z, dy, sw, sb_t, gv, gq, gk, sinks, bias), name=name, grid=(nb,),
        in_specs=[cur, prev, dy_spec] + params,
        out_specs=[cur, full((A_HEADS, BLOCK, BLOCK)), full((BLOCK, BLOCK)), full((A_HEADS, BLOCK)),
                   full((1, HEAD_DIM)), full((1, HEAD_DIM)), full((1, BLOCK)), full((B_HEADS, BLOCK, 2 * BLOCK))],
        out_shape=[jax.ShapeDtypeStruct((t, IN_COLS), BF16), jax.ShapeDtypeStruct((A_HEADS, BLOCK, BLOCK), F32),
                   jax.ShapeDtypeStruct((BLOCK, BLOCK), F32), jax.ShapeDtypeStruct((A_HEADS, BLOCK), F32),
                   jax.ShapeDtypeStruct((1, HEAD_DIM), F32), jax.ShapeDtypeStruct((1, HEAD_DIM), F32),
                   jax.ShapeDtypeStruct((1, BLOCK), F32), jax.ShapeDtypeStruct((B_HEADS, BLOCK, 2 * BLOCK), F32)],
        scratch_shapes=[pltpu.VMEM((BLOCK, 2 * KV_WIDTH), F32)],
        semantics=("arbitrary",), comm=comm)
    return out if comm is None else (out, moved)


def _ada_fwd(c_all, w_ada, b_cols, name, tn=768):
    nb, d = c_all.shape
    cols = w_ada.shape[1]

    def body(c_ref, w_ref, b_ref, act_ref, mod_ref):
        cv = c_ref[...]
        act = cv * jax.nn.sigmoid(cv)
        act_ref[...] = act
        mod_ref[...] = _dot(act.astype(BF16), w_ref[...].astype(BF16)) + b_ref[...]

    return _pc(
        body, name=name, grid=(cols // tn,),
        in_specs=[pl.BlockSpec((nb, d), lambda j: (0, 0)), pl.BlockSpec((d, tn), lambda j: (0, j)),
                  pl.BlockSpec((1, tn), lambda j: (0, j))],
        out_specs=[pl.BlockSpec((nb, d), lambda j: (0, 0)), pl.BlockSpec((nb, tn), lambda j: (0, j))],
        out_shape=[jax.ShapeDtypeStruct((nb, d), F32), jax.ShapeDtypeStruct((nb, cols), F32)],
        compiler_params=_params(("arbitrary",)),
    )(c_all, w_ada, b_cols)


def _adamw(w, g, m, v):
    m = ADAM_B1 * m + (1.0 - ADAM_B1) * g
    v = ADAM_B2 * v + (1.0 - ADAM_B2) * (g * g)
    m_hat = m / (1.0 - ADAM_B1 ** ADAM_STEP)
    v_hat = v / (1.0 - ADAM_B2 ** ADAM_STEP)
    delta = -ADAM_LR * (m_hat / (jnp.sqrt(v_hat) + ADAM_EPS) + ADAM_WD * w)
    return delta, m, v


def _adam_outer(act_t, dmod, w, m, v, name, tr=256):
    rows, cols = w.shape
    nb = dmod.shape[0]

    def body(act_ref, dm_ref, w_ref, m_ref, v_ref, g_ref, d_ref, nm_ref, nv_ref):
        act = act_ref[...].astype(BF16).astype(F32)
        dm = dm_ref[...].astype(BF16).astype(F32)
        g = act[:, 0:1] * dm[0:1, :]
        for b in range(1, nb):
            g = g + act[:, b:b + 1] * dm[b:b + 1, :]
        g_ref[...] = g
        d_ref[...], nm_ref[...], nv_ref[...] = _adamw(w_ref[...], g, m_ref[...], v_ref[...])

    tile = pl.BlockSpec((tr, cols), lambda i: (i, 0))
    return _pc(
        body, name=name, grid=(rows // tr,),
        in_specs=[pl.BlockSpec((tr, nb), lambda i: (i, 0)), pl.BlockSpec((nb, cols), lambda i: (0, 0)), tile, tile, tile],
        out_specs=[tile] * 4, out_shape=[jax.ShapeDtypeStruct((rows, cols), F32)] * 4,
        compiler_params=_params(("parallel",)),
    )(act_t, dmod, w, m, v)


def _adam_reduce(parts, w, m, v, name, tr):
    rows, cols = w.shape

    def body(p_ref, w_ref, m_ref, v_ref, g_ref, d_ref, nm_ref, nv_ref):
        g = p_ref[0].astype(F32)
        for i in range(1, N_DEV):
            g = g + p_ref[i].astype(F32)
        g_ref[...] = g
        d_ref[...], nm_ref[...], nv_ref[...] = _adamw(w_ref[...], g, m_ref[...], v_ref[...])

    tile = pl.BlockSpec((tr, cols), lambda i: (i, 0))
    return _pc(
        body, name=name, grid=(rows // tr,),
        in_specs=[pl.BlockSpec((N_DEV, tr, cols), lambda i: (0, i, 0)), tile, tile, tile],
        out_specs=[tile] * 4, out_shape=[jax.ShapeDtypeStruct((rows, cols), F32)] * 4,
        compiler_params=_params(("parallel",)),
    )(parts, w, m, v)


def _ungather_cols(g):
    n, r, c = g.shape
    return jnp.transpose(g, (1, 0, 2)).reshape(r, n * c)


def _scatter_cols(w):
    r, c = w.shape
    return jnp.transpose(w.reshape(r, N_DEV, c // N_DEV), (1, 0, 2)).astype(BF16)


def _pad_rows(v):
    flat = v.reshape(-1)
    pad = (-flat.shape[0]) % (8 * BLOCK)
    return jnp.pad(flat, (0, pad)).reshape(-1, BLOCK)


def kernel(x, c, w_ada, b_ada, g_ffn1, w1_ffn1, w3_ffn1, w2_ffn1, g_mix, w_in, spatial_w, spatial_b, g_v, g_q, g_k, sinks, rel_bias, w_out, g_ffn2, w1_ffn2, w3_ffn2, w2_ffn2, loss_target, m_w_ada, m_b_ada, m_g_ffn1, m_w1_ffn1, m_w3_ffn1, m_w2_ffn1, m_g_mix, m_w_in, m_spatial_w, m_spatial_b, m_g_v, m_g_q, m_g_k, m_sinks, m_rel_bias, m_w_out, m_g_ffn2, m_w1_ffn2, m_w3_ffn2, m_w2_ffn2, v_w_ada, v_b_ada, v_g_ffn1, v_w1_ffn1, v_w3_ffn1, v_w2_ffn1, v_g_mix, v_w_in, v_spatial_w, v_spatial_b, v_g_v, v_g_q, v_g_k, v_sinks, v_rel_bias, v_w_out, v_g_ffn2, v_w1_ffn2, v_w3_ffn2, v_w2_ffn2):
    me = _linear(_position())
    d = D_MODEL
    x0 = x[0]
    target = loss_target[0]
    t = x0.shape[0]

    big = dict(w1a=w1_ffn1[0], w3a=w3_ffn1[0], w2a=w2_ffn1[0], w_in=w_in[0], w_out=w_out[0],
               w1b=w1_ffn2[0], w3b=w3_ffn2[0], w2b=w2_ffn2[0])
    col_sharded = ("w1a", "w3a", "w_in", "w1b", "w3b")
    first, later = ("w1a", "w3a", "w2a"), ("w_in", "w_out", "w1b", "w3b", "w2b")
    full = {}

    def unpack(keys, gathered):
        for k, gth in zip(keys, gathered):
            full[k] = _ungather_cols(gth) if k in col_sharded else gth.reshape(-1, gth.shape[-1])

    def share(k):
        gk_ = grads[k]
        if k in col_sharded:
            return _scatter_cols(gk_)
        return gk_.reshape(N_DEV, gk_.shape[0] // N_DEV, gk_.shape[1]).astype(BF16)

    unpack(first, _run_comm(_GatherTwoLevel([big[k].astype(BF16) for k in first]), "gather_ffn1"))

    c_all = _run_comm(_Direct([c], True), "gather_c")[0].reshape(N_DEV, d)
    b_cols = lax.dynamic_slice(b_ada, (0, me * MOD_COLS), (1, MOD_COLS))
    act_all, mod_cols = _ada_fwd(c_all, w_ada[0], b_cols, "ada_fwd")
    mod = _run_comm(_Direct([mod_cols.reshape(N_DEV, 1, MOD_COLS)], False), "scatter_mod")[0]
    mod = mod.reshape(N_MOD, 1, d)
    sh1, sc1, gt1, sh2, sc2, gt2, sh3, sc3, gt3 = [mod[i] for i in range(N_MOD)]

    h1 = _norm_fwd(x0, None, None, 0.0, g_ffn1, sc1, sh1, "norm1_fwd")
    (a1, b1, f1), gathered = _ffn_fwd(h1, full["w1a"], full["w3a"], full["w2a"], "ffn1_fwd",
                                      comm=_GatherTwoLevel([big[k].astype(BF16) for k in later]), relay_at=0.6)
    unpack(later, gathered)
    x1, h2 = _norm_fwd(x0, f1, gt1, 0.5, g_mix, sc2, sh2, "norm2_fwd")
    z = _matmul(h2, full["w_in"], "nn", 512, IN_COLS // 2, d, F32, "mix_in_fwd")
    onehot = _bucket_onehot()
    bias = _select_matmul(rel_bias.T, jnp.asarray(onehot, BF16), "bias_table").reshape(B_HEADS, BLOCK, 2 * BLOCK)
    sb_t = spatial_b[0].T
    mix_params = (spatial_w[0], sb_t, g_v[0], g_q, g_k, sinks, bias)
    ycat = _mix_fwd(z, *mix_params, "mix_core_fwd")
    y = _matmul(ycat, full["w_out"], "nn", 512, d, d, F32, "mix_out_fwd")
    x2, h3 = _norm_fwd(x1, y, gt2, 1.0, g_ffn2, sc3, sh3, "norm3_fwd")
    a2, b2, f2 = _ffn_fwd(h3, full["w1b"], full["w3b"], full["w2b"], "ffn2_fwd")
    dx3, df2, loss_part = _loss_head(x2, f2, gt3, 0.5, target, "loss_head")
    loss = lax.psum(loss_part[0, 0], ("x", "y", "c"))

    grads, parts = {}, {}
    da, db, s, dh3 = _ffn_bwd_dx(df2, a2, b2, full["w1b"], full["w3b"], full["w2b"], "ffn2_bwd_dx")
    tk = min(DW_TOKENS, t)
    grads["w1b"] = _matmul(h3, da, "tn", 1024, 512, tk, F32, "ffn2_dw1")
    grads["w3b"] = _matmul(h3, db, "tn", 1024, 512, tk, F32, "ffn2_dw3")
    grads["w2b"] = _matmul(s, df2, "tn", 512, 1024, tk, F32, "ffn2_dw2")
    dx2, dyg, sums3 = _norm_bwd(dh3, x2, dx3, f2, 0.5, g_ffn2, sc3, gt2, 1.0, "norm3_bwd")
    dycat = _matmul(dyg, full["w_out"], "nt", 512, d, d, F32, "mix_out_bwd")
    grads["w_out"] = _matmul(ycat, dyg, "tn", 1024, 512, tk, F32, "mix_out_dw")
    mix_grads, moved = _mix_bwd(z, dycat, *mix_params, "mix_core_bwd",
                                comm=_Direct([share(k) for k in ("w1b", "w3b", "w2b")], False))
    parts["w1b"], parts["w3b"], parts["w2b"] = moved
    dz, d_sw, d_sb, d_gv, d_gq, d_gk, d_sink, d_bias = mix_grads
    d_rel = _select_matmul(d_bias.reshape(B_HEADS, BLOCK * 2 * BLOCK), jnp.asarray(onehot.T, BF16), "bias_table_bwd")
    dh2 = _matmul(dz, full["w_in"], "nt", 512, d, IN_COLS // 2, F32, "mix_in_bwd")
    grads["w_in"] = _matmul(h2, dz, "tn", 1024, 256, tk, F32, "mix_in_dw")
    dx1, df1, sums2 = _norm_bwd(dh2, x1, dx2, y, 1.0, g_mix, sc2, gt1, 0.5, "norm2_bwd")
    (da, db, s, dh1), moved = _ffn_bwd_dx(df1, a1, b1, full["w1a"], full["w3a"], full["w2a"], "ffn1_bwd_dx",
                                          comm=_Direct([share("w_in"), share("w_out")], False))
    parts["w_in"], parts["w_out"] = moved
    grads["w1a"] = _matmul(h1, da, "tn", 1024, 512, tk, F32, "ffn1_dw1")
    grads["w3a"], moved = _matmul(h1, db, "tn", 1024, 512, tk, F32, "ffn1_dw3", comm=_Direct([share("w1a")], False))
    parts["w1a"] = moved[0]
    grads["w2a"], moved = _matmul(s, df1, "tn", 512, 1024, tk, F32, "ffn1_dw2", comm=_Direct([share("w3a")], False))
    parts["w3a"] = moved[0]
    (dx0, sums1), moved = _norm_bwd(dh1, x0, dx1, f1, 0.5, g_ffn1, sc1, None, 0.0, "norm1_bwd",
                                    comm=_Direct([share("w2a")], False))
    parts["w2a"] = moved[0]

    moments = dict(w1a=(m_w1_ffn1, v_w1_ffn1), w3a=(m_w3_ffn1, v_w3_ffn1), w2a=(m_w2_ffn1, v_w2_ffn1),
                   w_in=(m_w_in, v_w_in), w_out=(m_w_out, v_w_out),
                   w1b=(m_w1_ffn2, v_w1_ffn2), w3b=(m_w3_ffn2, v_w3_ffn2), w2b=(m_w2_ffn2, v_w2_ffn2))
    upd = {}
    for k in big:
        rows = big[k].shape[0]
        tr = 256 if rows % 256 == 0 else rows // 4
        upd[k] = [o[None] for o in _adam_reduce(parts[k], big[k], moments[k][0][0], moments[k][1][0], "adam_" + k, tr)]

    dmod = jnp.concatenate([sums1[1:2], sums1[0:1], sums1[3:4], sums2[1:2], sums2[0:1], sums2[3:4],
                            sums3[1:2], sums3[0:1], sums3[3:4]], axis=1)
    small = [("b_ada", dmod, b_ada, m_b_ada, v_b_ada),
             ("g_ffn1", sums1[2:3], g_ffn1, m_g_ffn1, v_g_ffn1),
             ("g_mix", sums2[2:3], g_mix, m_g_mix, v_g_mix),
             ("spatial_w", d_sw[None], spatial_w, m_spatial_w, v_spatial_w),
             ("spatial_b", d_sb[:, :A_HEADS].T[None], spatial_b, m_spatial_b, v_spatial_b),
             ("g_v", d_gv[None], g_v, m_g_v, v_g_v),
             ("g_q", d_gq, g_q, m_g_q, v_g_q),
             ("g_k", d_gk, g_k, m_g_k, v_g_k),
             ("sinks", d_sink[:, :B_HEADS], sinks, m_sinks, v_sinks),
             ("rel_bias", d_rel.T, rel_bias, m_rel_bias, v_rel_bias),
             ("g_ffn2", sums3[2:3], g_ffn2, m_g_ffn2, v_g_ffn2)]
    packed = [jnp.concatenate([_pad_rows(item[i]) for item in small], axis=0) for i in range(1, 5)]
    g_all = _run_comm(_Direct([packed[0]], True), "gather_small")[0]
    small_out = _adam_reduce(g_all, packed[1], packed[2], packed[3], "adam_small", packed[0].shape[0])
    off = 0
    for name, g_, w_, _, _ in small:
        n_rows = _pad_rows(w_).shape[0]
        upd[name] = [o[off:off + n_rows].reshape(-1)[:w_.size].reshape(w_.shape) for o in small_out]
        off += n_rows

    dmod_rows = g_all[:, :N_MOD * d // BLOCK, :].reshape(N_DEV, N_MOD * d)
    dmod_cols = lax.dynamic_slice(dmod_rows, (0, me * MOD_COLS), (N_DEV, MOD_COLS))
    upd["w_ada"] = [o[None] for o in _adam_outer(act_all.T, dmod_cols, w_ada[0], m_w_ada[0], v_w_ada[0], "adam_w_ada")]

    order = [("w_ada", "w_ada"), ("b_ada", "b_ada"), ("g_ffn1", "g_ffn1"), ("w1_ffn1", "w1a"), ("w3_ffn1", "w3a"),
             ("w2_ffn1", "w2a"), ("g_mix", "g_mix"), ("w_in", "w_in"), ("spatial_w", "spatial_w"),
             ("spatial_b", "spatial_b"), ("g_v", "g_v"), ("g_q", "g_q"), ("g_k", "g_k"), ("sinks", "sinks"),
             ("rel_bias", "rel_bias"), ("w_out", "w_out"), ("g_ffn2", "g_ffn2"), ("w1_ffn2", "w1b"),
             ("w3_ffn2", "w3b"), ("w2_ffn2", "w2b")]
    outs = [loss, dx0[None]]
    for i in range(4):
        outs += [upd[key][i] for _, key in order]
    return tuple(outs)
```

```python
import functools
import math

import numpy as np
import jax
import jax.numpy as jnp
from jax import lax
from jax.experimental import pallas as pl
from jax.experimental.pallas import tpu as pltpu

F32 = jnp.float32
BF16 = jnp.bfloat16

D_MODEL = 2048
D_FF = 5632
BLOCK = 128
A_HEADS = 8
A_WIDTH = 1024
B_HEADS = 16
B_KV_HEADS = 2
GROUP = B_HEADS // B_KV_HEADS
HEAD_DIM = 64
B_WIDTH = 1024
KV_WIDTH = 128
IN_COLS = 3328
Q_OFF = 2 * A_WIDTH
K_OFF = Q_OFF + B_WIDTH
V_OFF = K_OFF + KV_WIDTH
N_BUCKETS = 32
N_MOD = 9
EPS = 1e-6
N_DEV = 8
MOD_COLS = N_MOD * D_MODEL // N_DEV

ADAM_LR = 0.001
ADAM_B1 = 0.9
ADAM_B2 = 0.999
ADAM_EPS = 1e-08
ADAM_WD = 0.01
ADAM_STEP = 10

DW_TOKENS = 4096
MASK_VALUE = -1e30
VMEM_LIMIT = 56 * 1024 * 1024
MESH_ID = pl.DeviceIdType.MESH
ANY = pl.BlockSpec(memory_space=pl.ANY)

_SQRT_HALF = 0.7071067811865476
_INV_SQRT_2PI = 0.3989422804014327


def _pc(body, **kw):
    return pl.pallas_call(body, **kw)


def _params(sem=None):
    return pltpu.CompilerParams(dimension_semantics=sem, vmem_limit_bytes=VMEM_LIMIT)


def _dot(a, b):
    return lax.dot_general(a, b, (((1,), (0,)), ((), ())), preferred_element_type=F32)


def _dot_nt(a, b):
    return lax.dot_general(a, b, (((1,), (1,)), ((), ())), preferred_element_type=F32)


def _dot_tn(a, b):
    return lax.dot_general(a, b, (((0,), (0,)), ((), ())), preferred_element_type=F32)


def _gelu(x):
    return 0.5 * x * (1.0 + lax.erf(x * _SQRT_HALF))


def _gelu_and_grad(x):
    cdf = 0.5 * (1.0 + lax.erf(x * _SQRT_HALF))
    return x * cdf, cdf + x * jnp.exp(-0.5 * x * x) * _INV_SQRT_2PI


def _rms(x):
    r = lax.rsqrt(jnp.mean(x * x, axis=-1, keepdims=True) + EPS)
    return x * r, r


def _rms_bwd(dy, y, r):
    return r * (dy - y * jnp.mean(dy * y, axis=-1, keepdims=True))


def _row_spec(tm, cols):
    return pl.BlockSpec((tm, cols), lambda i: (i, 0))


def _vec_spec(cols):
    return pl.BlockSpec((1, cols), lambda i: (0, 0))


def _position():
    x, y, c = lax.axis_index("x"), lax.axis_index("y"), lax.axis_index("c")
    return x, y, c


def _linear(p):
    return 4 * p[0] + 2 * p[1] + p[2]


class _Comm:
    PEER_COPIES = N_DEV - 1

    def __init__(self, arrs):
        self.arrs = list(arrs)
        n = len(self.arrs)
        self.scratch_shapes = [pltpu.SemaphoreType.DMA((self.PEER_COPIES * n,)),
                               pltpu.SemaphoreType.DMA((self.PEER_COPIES * n,)),
                               pltpu.SemaphoreType.DMA((n,))]

    def bind(self, srcs, dsts, sems):
        self.srcs, self.dsts = srcs, dsts
        self.send_sems, self.recv_sems, self.local_sems = sems
        x, y, c = _position()
        self.me, self.sibling, self.core = (x, y, c), (x, y, 1 - c), c
        self.chips = [(1 - x, y), (x, 1 - y), (1 - x, 1 - y)]
        self.peers = [(1 - x if k & 4 else x, 1 - y if k & 2 else y, 1 - c if k & 1 else c)
                      for k in range(1, N_DEV)]

    def relay(self):
        pass


class _GatherTwoLevel(_Comm):
    def __init__(self, arrs):
        super().__init__(arrs)
        self.out_shape = [jax.ShapeDtypeStruct((N_DEV,) + a.shape, a.dtype) for a in self.arrs]

    def _copy(self, a, k, block, to, from_input=False):
        rows = self.dsts[a].at[_linear(block)]
        return pltpu.make_async_remote_copy(
            src_ref=self.srcs[a] if from_input else rows, dst_ref=rows,
            send_sem=self.send_sems.at[self.PEER_COPIES * a + k], recv_sem=self.recv_sems.at[self.PEER_COPIES * a + k],
            device_id=to, device_id_type=MESH_ID)

    def _local(self, a):
        return pltpu.make_async_copy(self.srcs[a], self.dsts[a].at[_linear(self.me)], self.local_sems.at[a])

    def _first(self, a):
        return [self._copy(a, 0, self.me, self.sibling, True)] + [
            self._copy(a, 1 + j, self.me, (*chip, self.core), True) for j, chip in enumerate(self.chips)]

    def _passed(self, a):
        return [self._copy(a, 4 + j, (*chip, self.core), self.sibling) for j, chip in enumerate(self.chips)]

    def start(self):
        for a in range(len(self.arrs)):
            self._local(a).start()
            for cp in self._first(a):
                cp.start()

    def relay(self):
        for a in range(len(self.arrs)):
            for j, chip in enumerate(self.chips):
                self._copy(a, 1 + j, (*chip, self.core), self.me).wait_recv()
                self._passed(a)[j].start()

    def finish(self):
        for a in range(len(self.arrs)):
            self._copy(a, 0, self.sibling, self.me).wait_recv()
            for j, chip in enumerate(self.chips):
                self._copy(a, 4 + j, (*chip, 1 - self.core), self.me).wait_recv()
        for a in range(len(self.arrs)):
            for cp in self._first(a) + self._passed(a):
                cp.wait_send()
            self._local(a).wait()


class _Direct(_Comm):
    def __init__(self, arrs, broadcast):
        super().__init__(arrs)
        self.broadcast = broadcast
        self.out_shape = [jax.ShapeDtypeStruct(((N_DEV,) + a.shape) if broadcast else a.shape, a.dtype)
                          for a in self.arrs]

    def _outgoing(self, a, to):
        return self.srcs[a] if self.broadcast else self.srcs[a].at[_linear(to)]

    def _copy(self, a, k, sender, to):
        return pltpu.make_async_remote_copy(
            src_ref=self._outgoing(a, to), dst_ref=self.dsts[a].at[_linear(sender)],
            send_sem=self.send_sems.at[self.PEER_COPIES * a + k], recv_sem=self.recv_sems.at[self.PEER_COPIES * a + k],
            device_id=to, device_id_type=MESH_ID)

    def _local(self, a):
        return pltpu.make_async_copy(self._outgoing(a, self.me), self.dsts[a].at[_linear(self.me)],
                                     self.local_sems.at[a])

    def start(self):
        for a in range(len(self.arrs)):
            self._local(a).start()
            for k, peer in enumerate(self.peers):
                self._copy(a, k, self.me, peer).start()

    def finish(self):
        for a in range(len(self.arrs)):
            for k, peer in enumerate(self.peers):
                self._copy(a, k, peer, self.me).wait_recv()
        for a in range(len(self.arrs)):
            for k, peer in enumerate(self.peers):
                self._copy(a, k, self.me, peer).wait_send()
            self._local(a).wait()


def _run_comm(comm, name):
    n = len(comm.arrs)

    def body(*refs):
        comm.bind(refs[:n], refs[n:2 * n], refs[2 * n:])
        comm.start()
        comm.relay()
        comm.finish()

    return _pc(body, name=name, out_shape=comm.out_shape, in_specs=[ANY] * n, out_specs=[ANY] * n,
               scratch_shapes=comm.scratch_shapes)(*comm.arrs)


def _call(body, inputs, *, name, grid, in_specs, out_specs, out_shape, scratch_shapes=(), semantics,
          comm=None, relay_at=0.5):
    if comm is None:
        out = _pc(body, name=name, grid=grid, in_specs=in_specs, out_specs=out_specs, out_shape=out_shape,
                  scratch_shapes=list(scratch_shapes), compiler_params=_params(semantics))(*inputs)
        return list(out), []
    n_in, n_out, n_sc, k = len(in_specs), len(out_specs), len(scratch_shapes), len(comm.arrs)
    steps = math.prod(grid)
    relay_step = min(steps - 1, int(steps * relay_at))

    def carrier(*refs):
        ins, rest = refs[:n_in], refs[n_in:]
        csrc, rest = rest[:k], rest[k:]
        outs, rest = rest[:n_out], rest[n_out:]
        cdst, rest = rest[:k], rest[k:]
        scratch, csems = rest[:n_sc], rest[n_sc:]
        step = 0
        for axis, size in enumerate(grid):
            step = step * size + pl.program_id(axis)
        comm.bind(csrc, cdst, csems)
        pl.when(step == 0)(comm.start)
        pl.when(step == relay_step)(comm.relay)
        body(*ins, *outs, *scratch)
        pl.when(step == steps - 1)(comm.finish)

    out = _pc(carrier, name=name, grid=grid, in_specs=list(in_specs) + [ANY] * k,
              out_specs=list(out_specs) + [ANY] * k, out_shape=list(out_shape) + comm.out_shape,
              scratch_shapes=list(scratch_shapes) + comm.scratch_shapes,
              compiler_params=_params(("arbitrary",) * len(grid)))(*inputs, *comm.arrs)
    return list(out[:n_out]), list(out[n_out:])


def _matmul(a, b, mode, tm, tn, tk, out_dtype, name, comm=None):
    if mode == "nn":
        (m, kk), nn = a.shape, b.shape[1]
        a_spec = pl.BlockSpec((tm, tk), lambda i, j, k: (i, k))
        b_spec = pl.BlockSpec((tk, tn), lambda i, j, k: (k, j))
        dot = _dot
    elif mode == "nt":
        (m, kk), nn = a.shape, b.shape[0]
        a_spec = pl.BlockSpec((tm, tk), lambda i, j, k: (i, k))
        b_spec = pl.BlockSpec((tn, tk), lambda i, j, k: (j, k))
        dot = _dot_nt
    else:
        (kk, m), nn = a.shape, b.shape[1]
        a_spec = pl.BlockSpec((tk, tm), lambda i, j, k: (k, i))
        b_spec = pl.BlockSpec((tk, tn), lambda i, j, k: (k, j))
        dot = _dot_tn
    assert m % tm == 0 and nn % tn == 0 and kk % tk == 0, (a.shape, b.shape, tm, tn, tk)
    nk = kk // tk
    assert nk == 1 or out_dtype == F32

    def body(a_ref, b_ref, o_ref):
        p = dot(a_ref[...], b_ref[...])
        if nk == 1:
            o_ref[...] = p.astype(o_ref.dtype)
        else:
            k = pl.program_id(2)

            @pl.when(k == 0)
            def _():
                o_ref[...] = p

            @pl.when(k > 0)
            def _():
                o_ref[...] += p

    out, moved = _call(
        body, (a, b), name=name, grid=(m // tm, nn // tn, nk),
        in_specs=[a_spec, b_spec], out_specs=[pl.BlockSpec((tm, tn), lambda i, j, k: (i, j))],
        out_shape=[jax.ShapeDtypeStruct((m, nn), out_dtype)],
        semantics=("parallel", "parallel", "arbitrary"), comm=comm)
    return out[0] if comm is None else (out[0], moved)


def _norm_fwd(x_prev, f_prev, gate_prev, coef, g, sc, sh, name, tm=256):
    t, d = x_prev.shape
    residual = f_prev is not None

    def body(*refs):
        if residual:
            xp_ref, fp_ref, gp_ref, g_ref, sc_ref, sh_ref, x_ref, h_ref = refs
            x = xp_ref[...] + coef * gp_ref[...] * fp_ref[...]
            x_ref[...] = x
        else:
            xp_ref, g_ref, sc_ref, sh_ref, h_ref = refs
            x = xp_ref[...]
        y, _ = _rms(x)
        h_ref[...] = ((y * g_ref[...]) * (1.0 + sc_ref[...]) + sh_ref[...]).astype(BF16)

    row, vec = _row_spec(tm, d), _vec_spec(d)
    if residual:
        ins, in_specs = (x_prev, f_prev, gate_prev, g, sc, sh), [row, row, vec, vec, vec, vec]
        out_shape = [jax.ShapeDtypeStruct((t, d), F32), jax.ShapeDtypeStruct((t, d), BF16)]
        out_specs = [row, row]
    else:
        ins, in_specs = (x_prev, g, sc, sh), [row, vec, vec, vec]
        out_shape = [jax.ShapeDtypeStruct((t, d), BF16)]
        out_specs = [row]
    out = _pc(body, name=name, grid=(t // tm,), in_specs=in_specs, out_specs=out_specs, out_shape=out_shape,
              compiler_params=_params(("parallel",)))(*ins)
    return out if residual else out[0]


def _norm_bwd(dh, x, dxo, fo, coef_o, g, sc, gate_prev, coef_prev, name, tm=256, comm=None):
    t, d = x.shape
    with_prev = gate_prev is not None

    def body(*refs):
        if with_prev:
            dh_ref, x_ref, dxo_ref, fo_ref, g_ref, sc_ref, gp_ref, dx_ref, dfp_ref, sums_ref = refs
        else:
            dh_ref, x_ref, dxo_ref, fo_ref, g_ref, sc_ref, dx_ref, sums_ref = refs
        dh_v, dxo_v = dh_ref[...], dxo_ref[...]
        y, r = _rms(x_ref[...])
        n = y * g_ref[...]
        dn = dh_v * (1.0 + sc_ref[...])
        dx = dxo_v + _rms_bwd(dn * g_ref[...], y, r)
        dx_ref[...] = dx
        if with_prev:
            dfp_ref[...] = (coef_prev * gp_ref[...] * dx).astype(BF16)
        @pl.when(pl.program_id(0) == 0)
        def _():
            sums_ref[...] = jnp.zeros_like(sums_ref)

        sums_ref[0:1, :] += jnp.sum(dh_v * n, axis=0, keepdims=True)
        sums_ref[1:2, :] += jnp.sum(dh_v, axis=0, keepdims=True)
        sums_ref[2:3, :] += jnp.sum(dn * y, axis=0, keepdims=True)
        sums_ref[3:4, :] += jnp.sum(coef_o * dxo_v * fo_ref[...], axis=0, keepdims=True)

    row, vec = _row_spec(tm, d), _vec_spec(d)
    sums_spec = pl.BlockSpec((8, d), lambda i: (0, 0))
    ins, in_specs = [dh, x, dxo, fo, g, sc], [row, row, row, row, vec, vec]
    out_shape, out_specs = [jax.ShapeDtypeStruct((t, d), F32)], [row]
    if with_prev:
        ins.append(gate_prev)
        in_specs.append(vec)
        out_shape.append(jax.ShapeDtypeStruct((t, d), BF16))
        out_specs.append(row)
    out_shape.append(jax.ShapeDtypeStruct((8, d), F32))
    out_specs.append(sums_spec)
    out, moved = _call(body, ins, name=name, grid=(t // tm,), in_specs=in_specs, out_specs=out_specs,
                       out_shape=out_shape, semantics=("arbitrary",), comm=comm)
    return out if comm is None else (out, moved)


def _ffn_fwd(h, w1, w3, w2, name, tm=512, tf=512, comm=None, relay_at=0.5):
    t, d = h.shape
    ff = w1.shape[1]

    def body(h_ref, w1_ref, w3_ref, w2_ref, a_ref, b_ref, f_ref):
        j = pl.program_id(1)
        hv = h_ref[...]
        a = _dot(hv, w1_ref[...])
        b = _dot(hv, w3_ref[...])
        a_ref[...] = a
        b_ref[...] = b
        s = (a * jax.nn.sigmoid(a) * b).astype(BF16)
        p = _dot(s, w2_ref[...])

        @pl.when(j == 0)
        def _():
            f_ref[...] = p

        @pl.when(j > 0)
        def _():
            f_ref[...] += p

    tile = pl.BlockSpec((tm, tf), lambda i, j: (i, j))
    rows = pl.BlockSpec((tm, d), lambda i, j: (i, 0))
    out, moved = _call(
        body, (h, w1, w3, w2), name=name, grid=(t // tm, ff // tf),
        in_specs=[rows, pl.BlockSpec((d, tf), lambda i, j: (0, j)), pl.BlockSpec((d, tf), lambda i, j: (0, j)),
                  pl.BlockSpec((tf, d), lambda i, j: (j, 0))],
        out_specs=[tile, tile, rows],
        out_shape=[jax.ShapeDtypeStruct((t, ff), F32), jax.ShapeDtypeStruct((t, ff), F32),
                   jax.ShapeDtypeStruct((t, d), F32)],
        semantics=("parallel", "arbitrary"), comm=comm, relay_at=relay_at)
    return out if comm is None else (out, moved)


def _ffn_bwd_dx(df, a, b, w1, w3, w2, name, tm=512, tf=512, comm=None):
    t, d = df.shape
    ff = a.shape[1]

    def body(df_ref, a_ref, b_ref, w1_ref, w3_ref, w2_ref, da_ref, db_ref, s_ref, dh_ref):
        j = pl.program_id(1)
        ds = _dot_nt(df_ref[...], w2_ref[...])
        av, bv = a_ref[...], b_ref[...]
        sg = jax.nn.sigmoid(av)
        sil = av * sg
        s_ref[...] = (sil * bv).astype(BF16)
        da = (ds * bv * (sg * (1.0 + av * (1.0 - sg)))).astype(BF16)
        db = (ds * sil).astype(BF16)
        da_ref[...] = da
        db_ref[...] = db
        p = _dot_nt(da, w1_ref[...]) + _dot_nt(db, w3_ref[...])

        @pl.when(j == 0)
        def _():
            dh_ref[...] = p

        @pl.when(j > 0)
        def _():
            dh_ref[...] += p

    tile = pl.BlockSpec((tm, tf), lambda i, j: (i, j))
    rows = pl.BlockSpec((tm, d), lambda i, j: (i, 0))
    out, moved = _call(
        body, (df, a, b, w1, w3, w2), name=name, grid=(t // tm, ff // tf),
        in_specs=[rows, tile, tile, pl.BlockSpec((d, tf), lambda i, j: (0, j)),
                  pl.BlockSpec((d, tf), lambda i, j: (0, j)), pl.BlockSpec((tf, d), lambda i, j: (j, 0))],
        out_specs=[tile, tile, tile, rows],
        out_shape=[jax.ShapeDtypeStruct((t, ff), BF16)] * 3 + [jax.ShapeDtypeStruct((t, d), F32)],
        semantics=("parallel", "arbitrary"), comm=comm)
    return out if comm is None else (out, moved)


def _loss_head(x_prev, f_prev, gate_prev, coef, target, name, tm=256):
    t, d = x_prev.shape
    steps = t // tm

    def body(xp_ref, fp_ref, gp_ref, tg_ref, dy_ref, df_ref, loss_ref, acc_ref):
        i = pl.program_id(0)
        e = xp_ref[...] + coef * gp_ref[...] * fp_ref[...] - tg_ref[...]
        dy = e * (1.0 / d)
        dy_ref[...] = dy
        df_ref[...] = (coef * gp_ref[...] * dy).astype(BF16)
        part = jnp.sum(e * e, axis=0, keepdims=True)

        @pl.when(i == 0)
        def _():
            acc_ref[...] = part

        @pl.when(i > 0)
        def _():
            acc_ref[...] += part

        @pl.when(i == steps - 1)
        def _():
            loss_ref[...] = jnp.sum(acc_ref[...], axis=1, keepdims=True) * (0.5 / d)

    row, vec = _row_spec(tm, d), _vec_spec(d)
    return _pc(
        body, name=name, grid=(steps,),
        in_specs=[row, row, vec, row], out_specs=[row, row, pl.BlockSpec((1, 1), lambda i: (0, 0))],
        out_shape=[jax.ShapeDtypeStruct((t, d), F32), jax.ShapeDtypeStruct((t, d), BF16),
                   jax.ShapeDtypeStruct((1, 1), F32)],
        scratch_shapes=[pltpu.VMEM((1, d), F32)],
        compiler_params=_params(("arbitrary",)),
    )(x_prev, f_prev, gate_prev, target)


def _split3(x):
    hi = x.astype(BF16)
    r1 = x - hi.astype(F32)
    mid = r1.astype(BF16)
    lo = (r1 - mid.astype(F32)).astype(BF16)
    return hi, mid, lo


def _select_matmul(a, onehot, name):
    m, n = a.shape[0], onehot.shape[1]

    def body(a_ref, oh_ref, o_ref):
        hi, mid, lo = _split3(a_ref[...])
        oh = oh_ref[...]
        o_ref[...] = (_dot(hi, oh) + _dot(mid, oh)) + _dot(lo, oh)

    return _pc(body, name=name, out_shape=jax.ShapeDtypeStruct((m, n), F32), compiler_params=_params())(a, onehot)


def _bucket_onehot():
    qi = np.arange(BLOCK)[:, None]
    kj = np.arange(2 * BLOCK)[None, :]
    dist = np.clip(qi + BLOCK - kj, 0, None)
    nf = np.maximum(dist, 1).astype(np.float32)
    large = 16 + (np.log(nf / np.float32(16)) / np.float32(math.log(128 / 16)) * np.float32(16)).astype(np.int32)
    bucket = np.where(dist < 16, dist, np.minimum(large, N_BUCKETS - 1)).reshape(-1)
    return (bucket[None, :] == np.arange(N_BUCKETS)[:, None]).astype(np.float32)


def _window_mask(n):
    row = lax.broadcasted_iota(jnp.int32, (GROUP * BLOCK, 2 * BLOCK), 0)
    qi = row & (BLOCK - 1)
    kj = lax.broadcasted_iota(jnp.int32, (GROUP * BLOCK, 2 * BLOCK), 1)
    return (kj > qi) & (kj <= qi + BLOCK) & ((kj >= BLOCK) | (n > 0))


def _kv_band(zc_ref, zp_ref, kvh):
    lo, hi = kvh * HEAD_DIM, (kvh + 1) * HEAD_DIM
    k_raw = jnp.concatenate([zp_ref[:, lo:hi], zc_ref[:, K_OFF + lo:K_OFF + hi]], axis=0)
    v_raw = jnp.concatenate([zp_ref[:, KV_WIDTH + lo:KV_WIDTH + hi], zc_ref[:, V_OFF + lo:V_OFF + hi]], axis=0)
    return k_raw, v_raw


def _stack_heads(ref, off):
    return jnp.concatenate([ref[:, off + g * HEAD_DIM:off + (g + 1) * HEAD_DIM] for g in range(GROUP)], axis=0)


def _unstack_heads(ref, off, stacked):
    for g in range(GROUP):
        ref[:, off + g * HEAD_DIM:off + (g + 1) * HEAD_DIM] = stacked[g * BLOCK:(g + 1) * BLOCK]


def _group_softmax(qn, kband, bias_ref, sink_ref, kvh, valid):
    bias = bias_ref[kvh * GROUP:(kvh + 1) * GROUP].reshape(GROUP * BLOCK, 2 * BLOCK)
    s = _dot_nt(qn, kband) * (HEAD_DIM ** -0.5) + bias
    s = jnp.where(valid, s, MASK_VALUE)
    sink = jnp.concatenate([jnp.full((BLOCK, 1), sink_ref[0, kvh * GROUP + g], F32) for g in range(GROUP)], axis=0)
    m = jnp.maximum(jnp.max(s, axis=-1, keepdims=True), sink)
    p = jnp.exp(s - m)
    esink = jnp.exp(sink - m)
    inv = 1.0 / (jnp.sum(p, axis=-1, keepdims=True) + esink)
    return p * inv, esink * inv, inv


def _mix_specs(nb, order):
    cur = pl.BlockSpec((BLOCK, IN_COLS), lambda n: (order(n), 0))
    prev = pl.BlockSpec((BLOCK, 2 * KV_WIDTH), lambda n: (jnp.maximum(order(n) - 1, 0), K_OFF // (2 * KV_WIDTH)))
    full = lambda shape: pl.BlockSpec(shape, lambda n: (0,) * len(shape))
    params = [full((A_HEADS, BLOCK, BLOCK)), full((BLOCK, A_HEADS)), full((A_HEADS, BLOCK)),
              full((1, HEAD_DIM)), full((1, HEAD_DIM)), pl.BlockSpec(memory_space=pltpu.SMEM),
              full((B_HEADS, BLOCK, 2 * BLOCK))]
    return cur, prev, params, full


def _mix_fwd(z, sw, sb_t, gv, gq, gk, sinks, bias, name):
    t = z.shape[0]
    nb = t // BLOCK

    def body(zc_ref, zp_ref, sw_ref, sbt_ref, gv_ref, gq_ref, gk_ref, sink_ref, bias_ref, y_ref):
        n = pl.program_id(0)
        ri = lax.broadcasted_iota(jnp.int32, (BLOCK, BLOCK), 0)
        ci = lax.broadcasted_iota(jnp.int32, (BLOCK, BLOCK), 1)
        tril = ri >= ci
        for h in range(A_HEADS):
            u = _gelu(zc_ref[:, h * BLOCK:(h + 1) * BLOCK])
            vv = _gelu(zc_ref[:, A_WIDTH + h * BLOCK:A_WIDTH + (h + 1) * BLOCK])
            vhat, _ = _rms(vv)
            vn = (vhat * gv_ref[h:h + 1, :]).astype(BF16)
            w = jnp.where(tril, sw_ref[h], 0.0).astype(BF16)
            mixed = _dot(w, vn) + sbt_ref[:, h:h + 1]
            y_ref[:, h * BLOCK:(h + 1) * BLOCK] = (u * mixed).astype(BF16)
        valid = _window_mask(n)
        for kvh in range(B_KV_HEADS):
            k_raw, v_raw = _kv_band(zc_ref, zp_ref, kvh)
            khat, _ = _rms(k_raw)
            kband = (khat * gk_ref[...]).astype(BF16)
            vband = v_raw.astype(BF16)
            qhat, _ = _rms(_stack_heads(zc_ref, Q_OFF + kvh * GROUP * HEAD_DIM))
            qn = (qhat * gq_ref[...]).astype(BF16)
            w, _, _ = _group_softmax(qn, kband, bias_ref, sink_ref, kvh, valid)
            o = _dot(w.astype(BF16), vband)
            _unstack_heads(y_ref, A_WIDTH + kvh * GROUP * HEAD_DIM, o.astype(BF16))

    cur, prev, params, _ = _mix_specs(nb, lambda n: n)
    return _pc(
        body, name=name, grid=(nb,), in_specs=[cur, prev] + params,
        out_specs=pl.BlockSpec((BLOCK, A_WIDTH + B_WIDTH), lambda n: (n, 0)),
        out_shape=jax.ShapeDtypeStruct((t, A_WIDTH + B_WIDTH), BF16),
        compiler_params=_params(("arbitrary",)),
    )(z, z, sw, sb_t, gv, gq, gk, sinks, bias)


def _mix_bwd(z, dy, sw, sb_t, gv, gq, gk, sinks, bias, name, comm=None):
    t = z.shape[0]
    nb = t // BLOCK

    def body(zc_ref, zp_ref, dy_ref, sw_ref, sbt_ref, gv_ref, gq_ref, gk_ref, sink_ref, bias_ref,
             dz_ref, dsw_ref, dsb_ref, dgv_ref, dgq_ref, dgk_ref, dsink_ref, dbias_ref, carry_ref):
        step = pl.program_id(0)
        n = nb - 1 - step

        @pl.when(step == 0)
        def _():
            dsw_ref[...] = jnp.zeros_like(dsw_ref)
            dsb_ref[...] = jnp.zeros_like(dsb_ref)
            dgv_ref[...] = jnp.zeros_like(dgv_ref)
            dgq_ref[...] = jnp.zeros_like(dgq_ref)
            dgk_ref[...] = jnp.zeros_like(dgk_ref)
            dsink_ref[...] = jnp.zeros_like(dsink_ref)
            dbias_ref[...] = jnp.zeros_like(dbias_ref)
            carry_ref[...] = jnp.zeros_like(carry_ref)

        ri = lax.broadcasted_iota(jnp.int32, (BLOCK, BLOCK), 0)
        ci = lax.broadcasted_iota(jnp.int32, (BLOCK, BLOCK), 1)
        tril = ri >= ci
        dsb = jnp.zeros((BLOCK, BLOCK), F32)
        for h in range(A_HEADS):
            uo, vo = h * BLOCK, A_WIDTH + h * BLOCK
            u_raw, v_raw = zc_ref[:, uo:uo + BLOCK], zc_ref[:, vo:vo + BLOCK]
            (u, du_raw), (vv, dv_raw) = _gelu_and_grad(u_raw), _gelu_and_grad(v_raw)
            vhat, r = _rms(vv)
            vn = (vhat * gv_ref[h:h + 1, :]).astype(BF16)
            w = jnp.where(tril, sw_ref[h], 0.0).astype(BF16)
            mixed = _dot(w, vn) + sbt_ref[:, h:h + 1]
            dya = dy_ref[:, uo:uo + BLOCK]
            dmixed = dya * u
            dm16 = dmixed.astype(BF16)
            dsw_ref[h] += jnp.where(tril, _dot_nt(dm16, vn), 0.0)
            dsb = dsb + jnp.where(ci == h, jnp.sum(dmixed, axis=1, keepdims=True), 0.0)
            dvn = _dot_tn(w, dm16)
            dgv_ref[h:h + 1, :] += jnp.sum(dvn * vhat, axis=0, keepdims=True)
            dvv = _rms_bwd(dvn * gv_ref[h:h + 1, :], vhat, r)
            dz_ref[:, uo:uo + BLOCK] = (dya * mixed * du_raw).astype(BF16)
            dz_ref[:, vo:vo + BLOCK] = (dvv * dv_raw).astype(BF16)
        dsb_ref[...] += dsb

        valid = _window_mask(n)
        lane = lax.broadcasted_iota(jnp.int32, (1, BLOCK), 1)
        dsink = jnp.zeros((1, BLOCK), F32)
        dgq = jnp.zeros((1, HEAD_DIM), F32)
        for kvh in range(B_KV_HEADS):
            ko, vo = K_OFF + kvh * HEAD_DIM, V_OFF + kvh * HEAD_DIM
            k_raw, v_raw = _kv_band(zc_ref, zp_ref, kvh)
            khat, kr = _rms(k_raw)
            kband = (khat * gk_ref[...]).astype(BF16)
            vband = v_raw.astype(BF16)
            qo = Q_OFF + kvh * GROUP * HEAD_DIM
            qhat, qr = _rms(_stack_heads(zc_ref, qo))
            qn = (qhat * gq_ref[...]).astype(BF16)
            w, wsink, _ = _group_softmax(qn, kband, bias_ref, sink_ref, kvh, valid)
            do = _stack_heads(dy_ref, A_WIDTH + kvh * GROUP * HEAD_DIM).astype(BF16)
            dvb = _dot_tn(w.astype(BF16), do)
            dw = _dot_nt(do, vband)
            rowdot = jnp.sum(w * dw, axis=-1, keepdims=True)
            ds = w * (dw - rowdot)
            dsink_rows = -wsink * rowdot
            for g in range(GROUP):
                head_sum = jnp.sum(dsink_rows[g * BLOCK:(g + 1) * BLOCK], axis=0, keepdims=True)
                dsink = dsink + jnp.where(lane == kvh * GROUP + g, head_sum, 0.0)
            dbias_ref[kvh * GROUP:(kvh + 1) * GROUP] += ds.reshape(GROUP, BLOCK, 2 * BLOCK)
            ds16 = (ds * (HEAD_DIM ** -0.5)).astype(BF16)
            dqn = _dot(ds16, kband)
            dkn = _dot_tn(ds16, qn)
            dgq = dgq + jnp.sum(dqn * qhat, axis=0, keepdims=True)
            _unstack_heads(dz_ref, qo, _rms_bwd(dqn * gq_ref[...], qhat, qr).astype(BF16))
            dgk_ref[...] += jnp.sum(dkn * khat, axis=0, keepdims=True)
            dk = _rms_bwd(dkn * gk_ref[...], khat, kr)
            co = kvh * HEAD_DIM
            dz_ref[:, ko:ko + HEAD_DIM] = (dk[BLOCK:] + carry_ref[:, co:co + HEAD_DIM]).astype(BF16)
            dz_ref[:, vo:vo + HEAD_DIM] = (dvb[BLOCK:] + carry_ref[:, KV_WIDTH + co:KV_WIDTH + co + HEAD_DIM]).astype(BF16)
            carry_ref[:, co:co + HEAD_DIM] = dk[:BLOCK]
            carry_ref[:, KV_WIDTH + co:KV_WIDTH + co + HEAD_DIM] = dvb[:BLOCK]
        dgq_ref[...] += dgq
        dsink_ref[...] += dsink

    order = lambda n: nb - 1 - n
    cur, prev, params, full = _mix_specs(nb, order)
    dy_spec = pl.BlockSpec((BLOCK, A_WIDTH + B_WIDTH), lambda n: (order(n), 0))
    out, moved = _call(
        body, (z, z, dy, sw, sb_t, gv, gq, gk, sinks, bias), name=name, grid=(nb,),
        in_specs=[cur, prev, dy_spec] + params,
        out_specs=[cur, full((A_HEADS, BLOCK, BLOCK)), full((BLOCK, BLOCK)), full((A_HEADS, BLOCK)),
                   full((1, HEAD_DIM)), full((1, HEAD_DIM)), full((1, BLOCK)), full((B_HEADS, BLOCK, 2 * BLOCK))],
        out_shape=[jax.ShapeDtypeStruct((t, IN_COLS), BF16), jax.ShapeDtypeStruct((A_HEADS, BLOCK, BLOCK), F32),
                   jax.ShapeDtypeStruct((BLOCK, BLOCK), F32), jax.ShapeDtypeStruct((A_HEADS, BLOCK), F32),
                   jax.ShapeDtypeStruct((1, HEAD_DIM), F32), jax.ShapeDtypeStruct((1, HEAD_DIM), F32),
                   jax.ShapeDtypeStruct((1, BLOCK), F32), jax.ShapeDtypeStruct((B_HEADS, BLOCK, 2 * BLOCK), F32)],
        scratch_shapes=[pltpu.VMEM((BLOCK, 2 * KV_WIDTH), F32)],
        semantics=("arbitrary",), comm=comm)
    return out if comm is None else (out, moved)


def _ada_fwd(c_all, w_ada, b_cols, name, tn=768):
    nb, d = c_all.shape
    cols = w_ada.shape[1]

    def body(c_ref, w_ref, b_ref, act_ref, mod_ref):
        cv = c_ref[...]
        act = cv * jax.nn.sigmoid(cv)
        act_ref[...] = act
        mod_ref[...] = _dot(act.astype(BF16), w_ref[...].astype(BF16)) + b_ref[...]

    return _pc(
        body, name=name, grid=(cols // tn,),
        in_specs=[pl.BlockSpec((nb, d), lambda j: (0, 0)), pl.BlockSpec((d, tn), lambda j: (0, j)),
                  pl.BlockSpec((1, tn), lambda j: (0, j))],
        out_specs=[pl.BlockSpec((nb, d), lambda j: (0, 0)), pl.BlockSpec((nb, tn), lambda j: (0, j))],
        out_shape=[jax.ShapeDtypeStruct((nb, d), F32), jax.ShapeDtypeStruct((nb, cols), F32)],
        compiler_params=_params(("arbitrary",)),
    )(c_all, w_ada, b_cols)


def _adamw(w, g, m, v):
    m = ADAM_B1 * m + (1.0 - ADAM_B1) * g
    v = ADAM_B2 * v + (1.0 - ADAM_B2) * (g * g)
    m_hat = m / (1.0 - ADAM_B1 ** ADAM_STEP)
    v_hat = v / (1.0 - ADAM_B2 ** ADAM_STEP)
    delta = -ADAM_LR * (m_hat / (jnp.sqrt(v_hat) + ADAM_EPS) + ADAM_WD * w)
    return delta, m, v


def _adam_outer(act_t, dmod, w, m, v, name, tr=256):
    rows, cols = w.shape
    nb = dmod.shape[0]

    def body(act_ref, dm_ref, w_ref, m_ref, v_ref, g_ref, d_ref, nm_ref, nv_ref):
        act = act_ref[...].astype(BF16).astype(F32)
        dm = dm_ref[...].astype(BF16).astype(F32)
        g = act[:, 0:1] * dm[0:1, :]
        for b in range(1, nb):
            g = g + act[:, b:b + 1] * dm[b:b + 1, :]
        g_ref[...] = g
        d_ref[...], nm_ref[...], nv_ref[...] = _adamw(w_ref[...], g, m_ref[...], v_ref[...])

    tile = pl.BlockSpec((tr, cols), lambda i: (i, 0))
    return _pc(
        body, name=name, grid=(rows // tr,),
        in_specs=[pl.BlockSpec((tr, nb), lambda i: (i, 0)), pl.BlockSpec((nb, cols), lambda i: (0, 0)), tile, tile, tile],
        out_specs=[tile] * 4, out_shape=[jax.ShapeDtypeStruct((rows, cols), F32)] * 4,
        compiler_params=_params(("parallel",)),
    )(act_t, dmod, w, m, v)


def _adam_reduce(parts, w, m, v, name, tr):
    rows, cols = w.shape

    def body(p_ref, w_ref, m_ref, v_ref, g_ref, d_ref, nm_ref, nv_ref):
        g = p_ref[0].astype(F32)
        for i in range(1, N_DEV):
            g = g + p_ref[i].astype(F32)
        g_ref[...] = g
        d_ref[...], nm_ref[...], nv_ref[...] = _adamw(w_ref[...], g, m_ref[...], v_ref[...])

    tile = pl.BlockSpec((tr, cols), lambda i: (i, 0))
    return _pc(
        body, name=name, grid=(rows // tr,),
        in_specs=[pl.BlockSpec((N_DEV, tr, cols), lambda i: (0, i, 0)), tile, tile, tile],
        out_specs=[tile] * 4, out_shape=[jax.ShapeDtypeStruct((rows, cols), F32)] * 4,
        compiler_params=_params(("parallel",)),
    )(parts, w, m, v)


def _ungather_cols(g):
    n, r, c = g.shape
    return jnp.transpose(g, (1, 0, 2)).reshape(r, n * c)


def _scatter_cols(w):
    r, c = w.shape
    return jnp.transpose(w.reshape(r, N_DEV, c // N_DEV), (1, 0, 2)).astype(BF16)


def _pad_rows(v):
    flat = v.reshape(-1)
    pad = (-flat.shape[0]) % (8 * BLOCK)
    return jnp.pad(flat, (0, pad)).reshape(-1, BLOCK)


def kernel(x, c, w_ada, b_ada, g_ffn1, w1_ffn1, w3_ffn1, w2_ffn1, g_mix, w_in, spatial_w, spatial_b, g_v, g_q, g_k, sinks, rel_bias, w_out, g_ffn2, w1_ffn2, w3_ffn2, w2_ffn2, loss_target, m_w_ada, m_b_ada, m_g_ffn1, m_w1_ffn1, m_w3_ffn1, m_w2_ffn1, m_g_mix, m_w_in, m_spatial_w, m_spatial_b, m_g_v, m_g_q, m_g_k, m_sinks, m_rel_bias, m_w_out, m_g_ffn2, m_w1_ffn2, m_w3_ffn2, m_w2_ffn2, v_w_ada, v_b_ada, v_g_ffn1, v_w1_ffn1, v_w3_ffn1, v_w2_ffn1, v_g_mix, v_w_in, v_spatial_w, v_spatial_b, v_g_v, v_g_q, v_g_k, v_sinks, v_rel_bias, v_w_out, v_g_ffn2, v_w1_ffn2, v_w3_ffn2, v_w2_ffn2):
    me = _linear(_position())
    d = D_MODEL
    x0 = x[0]
    target = loss_target[0]
    t = x0.shape[0]

    big = dict(w1a=w1_ffn1[0], w3a=w3_ffn1[0], w2a=w2_ffn1[0], w_in=w_in[0], w_out=w_out[0],
               w1b=w1_ffn2[0], w3b=w3_ffn2[0], w2b=w2_ffn2[0])
    col_sharded = ("w1a", "w3a", "w_in", "w1b", "w3b")
    first, later = ("w1a", "w3a", "w2a"), ("w_in", "w_out", "w1b", "w3b", "w2b")
    full = {}

    def unpack(keys, gathered):
        for k, gth in zip(keys, gathered):
            full[k] = _ungather_cols(gth) if k in col_sharded else gth.reshape(-1, gth.shape[-1])

    def share(k):
        gk_ = grads[k]
        if k in col_sharded:
            return _scatter_cols(gk_)
        return gk_.reshape(N_DEV, gk_.shape[0] // N_DEV, gk_.shape[1]).astype(BF16)

    unpack(first, _run_comm(_GatherTwoLevel([big[k].astype(BF16) for k in first]), "gather_ffn1"))

    c_all = _run_comm(_Direct([c], True), "gather_c")[0].reshape(N_DEV, d)
    b_cols = lax.dynamic_slice(b_ada, (0, me * MOD_COLS), (1, MOD_COLS))
    act_all, mod_cols = _ada_fwd(c_all, w_ada[0], b_cols, "ada_fwd")
    mod = _run_comm(_Direct([mod_cols.reshape(N_DEV, 1, MOD_COLS)], False), "scatter_mod")[0]
    mod = mod.reshape(N_MOD, 1, d)
    sh1, sc1, gt1, sh2, sc2, gt2, sh3, sc3, gt3 = [mod[i] for i in range(N_MOD)]

    h1 = _norm_fwd(x0, None, None, 0.0, g_ffn1, sc1, sh1, "norm1_fwd")
    (a1, b1, f1), gathered = _ffn_fwd(h1, full["w1a"], full["w3a"], full["w2a"], "ffn1_fwd",
                                      comm=_GatherTwoLevel([big[k].astype(BF16) for k in later]), relay_at=0.6)
    unpack(later, gathered)
    x1, h2 = _norm_fwd(x0, f1, gt1, 0.5, g_mix, sc2, sh2, "norm2_fwd")
    z = _matmul(h2, full["w_in"], "nn", 512, IN_COLS // 2, d, F32, "mix_in_fwd")
    onehot = _bucket_onehot()
    bias = _select_matmul(rel_bias.T, jnp.asarray(onehot, BF16), "bias_table").reshape(B_HEADS, BLOCK, 2 * BLOCK)
    sb_t = spatial_b[0].T
    mix_params = (spatial_w[0], sb_t, g_v[0], g_q, g_k, sinks, bias)
    ycat = _mix_fwd(z, *mix_params, "mix_core_fwd")
    y = _matmul(ycat, full["w_out"], "nn", 512, d, d, F32, "mix_out_fwd")
    x2, h3 = _norm_fwd(x1, y, gt2, 1.0, g_ffn2, sc3, sh3, "norm3_fwd")
    a2, b2, f2 = _ffn_fwd(h3, full["w1b"], full["w3b"], full["w2b"], "ffn2_fwd")
    dx3, df2, loss_part = _loss_head(x2, f2, gt3, 0.5, target, "loss_head")
    loss = lax.psum(loss_part[0, 0], ("x", "y", "c"))

    grads, parts = {}, {}
    da, db, s, dh3 = _ffn_bwd_dx(df2, a2, b2, full["w1b"], full["w3b"], full["w2b"], "ffn2_bwd_dx")
    tk = min(DW_TOKENS, t)
    grads["w1b"] = _matmul(h3, da, "tn", 1024, 512, tk, F32, "ffn2_dw1")
    grads["w3b"], moved = _matmul(h3, db, "tn", 1024, 512, tk, F32, "ffn2_dw3", comm=_Direct([share("w1b")], False))
    parts["w1b"] = moved[0]
    grads["w2b"], moved = _matmul(s, df2, "tn", 512, 1024, tk, F32, "ffn2_dw2", comm=_Direct([share("w3b")], False))
    parts["w3b"] = moved[0]
    dx2, dyg, sums3 = _norm_bwd(dh3, x2, dx3, f2, 0.5, g_ffn2, sc3, gt2, 1.0, "norm3_bwd")
    dycat = _matmul(dyg, full["w_out"], "nt", 512, d, d, F32, "mix_out_bwd")
    grads["w_out"] = _matmul(ycat, dyg, "tn", 1024, 512, tk, F32, "mix_out_dw")
    mix_grads, moved = _mix_bwd(z, dycat, *mix_params, "mix_core_bwd",
                                comm=_Direct([share("w2b")], False))
    parts["w2b"] = moved[0]
    dz, d_sw, d_sb, d_gv, d_gq, d_gk, d_sink, d_bias = mix_grads
    d_rel = _select_matmul(d_bias.reshape(B_HEADS, BLOCK * 2 * BLOCK), jnp.asarray(onehot.T, BF16), "bias_table_bwd")
    dh2 = _matmul(dz, full["w_in"], "nt", 512, d, IN_COLS // 2, F32, "mix_in_bwd")
    grads["w_in"] = _matmul(h2, dz, "tn", 1024, 256, tk, F32, "mix_in_dw")
    dx1, df1, sums2 = _norm_bwd(dh2, x1, dx2, y, 1.0, g_mix, sc2, gt1, 0.5, "norm2_bwd")
    (da, db, s, dh1), moved = _ffn_bwd_dx(df1, a1, b1, full["w1a"], full["w3a"], full["w2a"], "ffn1_bwd_dx",
                                          comm=_Direct([share("w_in"), share("w_out")], False))
    parts["w_in"], parts["w_out"] = moved
    grads["w1a"] = _matmul(h1, da, "tn", 1024, 512, tk, F32, "ffn1_dw1")
    grads["w3a"], moved = _matmul(h1, db, "tn", 1024, 512, tk, F32, "ffn1_dw3", comm=_Direct([share("w1a")], False))
    parts["w1a"] = moved[0]
    grads["w2a"], moved = _matmul(s, df1, "tn", 512, 1024, tk, F32, "ffn1_dw2", comm=_Direct([share("w3a")], False))
    parts["w3a"] = moved[0]
    (dx0, sums1), moved = _norm_bwd(dh1, x0, dx1, f1, 0.5, g_ffn1, sc1, None, 0.0, "norm1_bwd",
                                    comm=_Direct([share("w2a")], False))
    parts["w2a"] = moved[0]

    moments = dict(w1a=(m_w1_ffn1, v_w1_ffn1), w3a=(m_w3_ffn1, v_w3_ffn1), w2a=(m_w2_ffn1, v_w2_ffn1),
                   w_in=(m_w_in, v_w_in), w_out=(m_w_out, v_w_out),
                   w1b=(m_w1_ffn2, v_w1_ffn2), w3b=(m_w3_ffn2, v_w3_ffn2), w2b=(m_w2_ffn2, v_w2_ffn2))
    upd = {}
    for k in big:
        rows = big[k].shape[0]
        tr = 256 if rows % 256 == 0 else rows // 4
        upd[k] = [o[None] for o in _adam_reduce(parts[k], big[k], moments[k][0][0], moments[k][1][0], "adam_" + k, tr)]

    dmod = jnp.concatenate([sums1[1:2], sums1[0:1], sums1[3:4], sums2[1:2], sums2[0:1], sums2[3:4],
                            sums3[1:2], sums3[0:1], sums3[3:4]], axis=1)
    small = [("b_ada", dmod, b_ada, m_b_ada, v_b_ada),
             ("g_ffn1", sums1[2:3], g_ffn1, m_g_ffn1, v_g_ffn1),
             ("g_mix", sums2[2:3], g_mix, m_g_mix, v_g_mix),
             ("spatial_w", d_sw[None], spatial_w, m_spatial_w, v_spatial_w),
             ("spatial_b", d_sb[:, :A_HEADS].T[None], spatial_b, m_spatial_b, v_spatial_b),
             ("g_v", d_gv[None], g_v, m_g_v, v_g_v),
             ("g_q", d_gq, g_q, m_g_q, v_g_q),
             ("g_k", d_gk, g_k, m_g_k, v_g_k),
             ("sinks", d_sink[:, :B_HEADS], sinks, m_sinks, v_sinks),
             ("rel_bias", d_rel.T, rel_bias, m_rel_bias, v_rel_bias),
             ("g_ffn2", sums3[2:3], g_ffn2, m_g_ffn2, v_g_ffn2)]
    packed = [jnp.concatenate([_pad_rows(item[i]) for item in small], axis=0) for i in range(1, 5)]
    g_all = _run_comm(_Direct([packed[0]], True), "gather_small")[0]
    small_out = _adam_reduce(g_all, packed[1], packed[2], packed[3], "adam_small", packed[0].shape[0])
    off = 0
    for name, g_, w_, _, _ in small:
        n_rows = _pad_rows(w_).shape[0]
        upd[name] = [o[off:off + n_rows].reshape(-1)[:w_.size].reshape(w_.shape) for o in small_out]
        off += n_rows

    dmod_rows = g_all[:, :N_MOD * d // BLOCK, :].reshape(N_DEV, N_MOD * d)
    dmod_cols = lax.dynamic_slice(dmod_rows, (0, me * MOD_COLS), (N_DEV, MOD_COLS))
    upd["w_ada"] = [o[None] for o in _adam_outer(act_all.T, dmod_cols, w_ada[0], m_w_ada[0], v_w_ada[0], "adam_w_ada")]

    order = [("w_ada", "w_ada"), ("b_ada", "b_ada"), ("g_ffn1", "g_ffn1"), ("w1_ffn1", "w1a"), ("w3_ffn1", "w3a"),
             ("w2_ffn1", "w2a"), ("g_mix", "g_mix"), ("w_in", "w_in"), ("spatial_w", "spatial_w"),
             ("spatial_b", "spatial_b"), ("g_v", "g_v"), ("g_q", "g_q"), ("g_k", "g_k"), ("sinks", "sinks"),
             ("rel_bias", "rel_bias"), ("w_out", "w_out"), ("g_ffn2", "g_ffn2"), ("w1_ffn2", "w1b"),
             ("w3_ffn2", "w3b"), ("w2_ffn2", "w2b")]
    outs = [loss, dx0[None]]
    for i in range(4):
        outs += [upd[key][i] for _, key in order]
    return tuple(outs)
```

```python
import functools
import math

import numpy as np
import jax
import jax.numpy as jnp
from jax import lax
from jax.experimental import pallas as pl
from jax.experimental.pallas import tpu as pltpu

F32 = jnp.float32
BF16 = jnp.bfloat16

D_MODEL = 2048
D_FF = 5632
BLOCK = 128
A_HEADS = 8
A_WIDTH = 1024
B_HEADS = 16
B_KV_HEADS = 2
GROUP = B_HEADS // B_KV_HEADS
HEAD_DIM = 64
B_WIDTH = 1024
KV_WIDTH = 128
IN_COLS = 3328
Q_OFF = 2 * A_WIDTH
K_OFF = Q_OFF + B_WIDTH
V_OFF = K_OFF + KV_WIDTH
N_BUCKETS = 32
N_MOD = 9
EPS = 1e-6
N_DEV = 8
MOD_COLS = N_MOD * D_MODEL // N_DEV

ADAM_LR = 0.001
ADAM_B1 = 0.9
ADAM_B2 = 0.999
ADAM_EPS = 1e-08
ADAM_WD = 0.01
ADAM_STEP = 10

FFN_ROWS = 1024
DW_TOKENS = 4096
MASK_VALUE = -1e30
VMEM_LIMIT = 56 * 1024 * 1024
MESH_ID = pl.DeviceIdType.MESH
ANY = pl.BlockSpec(memory_space=pl.ANY)

_SQRT_HALF = 0.7071067811865476
_INV_SQRT_2PI = 0.3989422804014327


def _pc(body, **kw):
    return pl.pallas_call(body, **kw)


def _params(sem=None):
    return pltpu.CompilerParams(dimension_semantics=sem, vmem_limit_bytes=VMEM_LIMIT)


def _dot(a, b):
    return lax.dot_general(a, b, (((1,), (0,)), ((), ())), preferred_element_type=F32)


def _dot_nt(a, b):
    return lax.dot_general(a, b, (((1,), (1,)), ((), ())), preferred_element_type=F32)


def _dot_tn(a, b):
    return lax.dot_general(a, b, (((0,), (0,)), ((), ())), preferred_element_type=F32)


def _gelu(x):
    return 0.5 * x * (1.0 + lax.erf(x * _SQRT_HALF))


def _gelu_and_grad(x):
    cdf = 0.5 * (1.0 + lax.erf(x * _SQRT_HALF))
    return x * cdf, cdf + x * jnp.exp(-0.5 * x * x) * _INV_SQRT_2PI


def _rms(x):
    r = lax.rsqrt(jnp.mean(x * x, axis=-1, keepdims=True) + EPS)
    return x * r, r


def _rms_bwd(dy, y, r):
    return r * (dy - y * jnp.mean(dy * y, axis=-1, keepdims=True))


def _row_spec(tm, cols):
    return pl.BlockSpec((tm, cols), lambda i: (i, 0))


def _vec_spec(cols):
    return pl.BlockSpec((1, cols), lambda i: (0, 0))


def _position():
    x, y, c = lax.axis_index("x"), lax.axis_index("y"), lax.axis_index("c")
    return x, y, c


def _linear(p):
    return 4 * p[0] + 2 * p[1] + p[2]


class _Comm:
    PEER_COPIES = N_DEV - 1

    def __init__(self, arrs):
        self.arrs = list(arrs)
        n = len(self.arrs)
        self.scratch_shapes = [pltpu.SemaphoreType.DMA((self.PEER_COPIES * n,)),
                               pltpu.SemaphoreType.DMA((self.PEER_COPIES * n,)),
                               pltpu.SemaphoreType.DMA((n,))]

    def bind(self, srcs, dsts, sems):
        self.srcs, self.dsts = srcs, dsts
        self.send_sems, self.recv_sems, self.local_sems = sems
        x, y, c = _position()
        self.me, self.sibling, self.core = (x, y, c), (x, y, 1 - c), c
        self.chips = [(1 - x, y), (x, 1 - y), (1 - x, 1 - y)]
        self.peers = [(1 - x if k & 4 else x, 1 - y if k & 2 else y, 1 - c if k & 1 else c)
                      for k in range(1, N_DEV)]

    def relay(self):
        pass


class _GatherTwoLevel(_Comm):
    def __init__(self, arrs):
        super().__init__(arrs)
        self.out_shape = [jax.ShapeDtypeStruct((N_DEV,) + a.shape, a.dtype) for a in self.arrs]

    def _copy(self, a, k, block, to, from_input=False):
        rows = self.dsts[a].at[_linear(block)]
        return pltpu.make_async_remote_copy(
            src_ref=self.srcs[a] if from_input else rows, dst_ref=rows,
            send_sem=self.send_sems.at[self.PEER_COPIES * a + k], recv_sem=self.recv_sems.at[self.PEER_COPIES * a + k],
            device_id=to, device_id_type=MESH_ID)

    def _local(self, a):
        return pltpu.make_async_copy(self.srcs[a], self.dsts[a].at[_linear(self.me)], self.local_sems.at[a])

    def _first(self, a):
        return [self._copy(a, 0, self.me, self.sibling, True)] + [
            self._copy(a, 1 + j, self.me, (*chip, self.core), True) for j, chip in enumerate(self.chips)]

    def _passed(self, a):
        return [self._copy(a, 4 + j, (*chip, self.core), self.sibling) for j, chip in enumerate(self.chips)]

    def start(self):
        for a in range(len(self.arrs)):
            self._local(a).start()
            for cp in self._first(a):
                cp.start()

    def relay(self):
        for a in range(len(self.arrs)):
            for j, chip in enumerate(self.chips):
                self._copy(a, 1 + j, (*chip, self.core), self.me).wait_recv()
                self._passed(a)[j].start()

    def finish(self):
        for a in range(len(self.arrs)):
            self._copy(a, 0, self.sibling, self.me).wait_recv()
            for j, chip in enumerate(self.chips):
                self._copy(a, 4 + j, (*chip, 1 - self.core), self.me).wait_recv()
        for a in range(len(self.arrs)):
            for cp in self._first(a) + self._passed(a):
                cp.wait_send()
            self._local(a).wait()


class _Direct(_Comm):
    def __init__(self, arrs, broadcast):
        super().__init__(arrs)
        self.broadcast = broadcast
        self.out_shape = [jax.ShapeDtypeStruct(((N_DEV,) + a.shape) if broadcast else a.shape, a.dtype)
                          for a in self.arrs]

    def _outgoing(self, a, to):
        return self.srcs[a] if self.broadcast else self.srcs[a].at[_linear(to)]

    def _copy(self, a, k, sender, to):
        return pltpu.make_async_remote_copy(
            src_ref=self._outgoing(a, to), dst_ref=self.dsts[a].at[_linear(sender)],
            send_sem=self.send_sems.at[self.PEER_COPIES * a + k], recv_sem=self.recv_sems.at[self.PEER_COPIES * a + k],
            device_id=to, device_id_type=MESH_ID)

    def _local(self, a):
        return pltpu.make_async_copy(self._outgoing(a, self.me), self.dsts[a].at[_linear(self.me)],
                                     self.local_sems.at[a])

    def start(self):
        for a in range(len(self.arrs)):
            self._local(a).start()
            for k, peer in enumerate(self.peers):
                self._copy(a, k, self.me, peer).start()

    def finish(self):
        for a in range(len(self.arrs)):
            for k, peer in enumerate(self.peers):
                self._copy(a, k, peer, self.me).wait_recv()
        for a in range(len(self.arrs)):
            for k, peer in enumerate(self.peers):
                self._copy(a, k, self.me, peer).wait_send()
            self._local(a).wait()


def _run_comm(comm, name):
    n = len(comm.arrs)

    def body(*refs):
        comm.bind(refs[:n], refs[n:2 * n], refs[2 * n:])
        comm.start()
        comm.relay()
        comm.finish()

    return _pc(body, name=name, out_shape=comm.out_shape, in_specs=[ANY] * n, out_specs=[ANY] * n,
               scratch_shapes=comm.scratch_shapes)(*comm.arrs)


def _call(body, inputs, *, name, grid, in_specs, out_specs, out_shape, scratch_shapes=(), semantics,
          comm=None, relay_at=0.5):
    if comm is None:
        out = _pc(body, name=name, grid=grid, in_specs=in_specs, out_specs=out_specs, out_shape=out_shape,
                  scratch_shapes=list(scratch_shapes), compiler_params=_params(semantics))(*inputs)
        return list(out), []
    n_in, n_out, n_sc, k = len(in_specs), len(out_specs), len(scratch_shapes), len(comm.arrs)
    steps = math.prod(grid)
    relay_step = min(steps - 1, int(steps * relay_at))

    def carrier(*refs):
        ins, rest = refs[:n_in], refs[n_in:]
        csrc, rest = rest[:k], rest[k:]
        outs, rest = rest[:n_out], rest[n_out:]
        cdst, rest = rest[:k], rest[k:]
        scratch, csems = rest[:n_sc], rest[n_sc:]
        step = 0
        for axis, size in enumerate(grid):
            step = step * size + pl.program_id(axis)
        comm.bind(csrc, cdst, csems)
        pl.when(step == 0)(comm.start)
        pl.when(step == relay_step)(comm.relay)
        body(*ins, *outs, *scratch)
        pl.when(step == steps - 1)(comm.finish)

    out = _pc(carrier, name=name, grid=grid, in_specs=list(in_specs) + [ANY] * k,
              out_specs=list(out_specs) + [ANY] * k, out_shape=list(out_shape) + comm.out_shape,
              scratch_shapes=list(scratch_shapes) + comm.scratch_shapes,
              compiler_params=_params(("arbitrary",) * len(grid)))(*inputs, *comm.arrs)
    return list(out[:n_out]), list(out[n_out:])


def _matmul(a, b, mode, tm, tn, tk, out_dtype, name, comm=None):
    if mode == "nn":
        (m, kk), nn = a.shape, b.shape[1]
        a_spec = pl.BlockSpec((tm, tk), lambda i, j, k: (i, k))
        b_spec = pl.BlockSpec((tk, tn), lambda i, j, k: (k, j))
        dot = _dot
    elif mode == "nt":
        (m, kk), nn = a.shape, b.shape[0]
        a_spec = pl.BlockSpec((tm, tk), lambda i, j, k: (i, k))
        b_spec = pl.BlockSpec((tn, tk), lambda i, j, k: (j, k))
        dot = _dot_nt
    else:
        (kk, m), nn = a.shape, b.shape[1]
        a_spec = pl.BlockSpec((tk, tm), lambda i, j, k: (k, i))
        b_spec = pl.BlockSpec((tk, tn), lambda i, j, k: (k, j))
        dot = _dot_tn
    assert m % tm == 0 and nn % tn == 0 and kk % tk == 0, (a.shape, b.shape, tm, tn, tk)
    nk = kk // tk
    assert nk == 1 or out_dtype == F32

    def body(a_ref, b_ref, o_ref):
        p = dot(a_ref[...], b_ref[...])
        if nk == 1:
            o_ref[...] = p.astype(o_ref.dtype)
        else:
            k = pl.program_id(2)

            @pl.when(k == 0)
            def _():
                o_ref[...] = p

            @pl.when(k > 0)
            def _():
                o_ref[...] += p

    out, moved = _call(
        body, (a, b), name=name, grid=(m // tm, nn // tn, nk),
        in_specs=[a_spec, b_spec], out_specs=[pl.BlockSpec((tm, tn), lambda i, j, k: (i, j))],
        out_shape=[jax.ShapeDtypeStruct((m, nn), out_dtype)],
        semantics=("parallel", "parallel", "arbitrary"), comm=comm)
    return out[0] if comm is None else (out[0], moved)


def _norm_fwd(x_prev, f_prev, gate_prev, coef, g, sc, sh, name, tm=256):
    t, d = x_prev.shape
    residual = f_prev is not None

    def body(*refs):
        if residual:
            xp_ref, fp_ref, gp_ref, g_ref, sc_ref, sh_ref, x_ref, h_ref = refs
            x = xp_ref[...] + coef * gp_ref[...] * fp_ref[...]
            x_ref[...] = x
        else:
            xp_ref, g_ref, sc_ref, sh_ref, h_ref = refs
            x = xp_ref[...]
        y, _ = _rms(x)
        h_ref[...] = ((y * g_ref[...]) * (1.0 + sc_ref[...]) + sh_ref[...]).astype(BF16)

    row, vec = _row_spec(tm, d), _vec_spec(d)
    if residual:
        ins, in_specs = (x_prev, f_prev, gate_prev, g, sc, sh), [row, row, vec, vec, vec, vec]
        out_shape = [jax.ShapeDtypeStruct((t, d), F32), jax.ShapeDtypeStruct((t, d), BF16)]
        out_specs = [row, row]
    else:
        ins, in_specs = (x_prev, g, sc, sh), [row, vec, vec, vec]
        out_shape = [jax.ShapeDtypeStruct((t, d), BF16)]
        out_specs = [row]
    out = _pc(body, name=name, grid=(t // tm,), in_specs=in_specs, out_specs=out_specs, out_shape=out_shape,
              compiler_params=_params(("parallel",)))(*ins)
    return out if residual else out[0]


def _norm_bwd(dh, x, dxo, fo, coef_o, g, sc, gate_prev, coef_prev, name, tm=256, comm=None):
    t, d = x.shape
    with_prev = gate_prev is not None

    def body(*refs):
        if with_prev:
            dh_ref, x_ref, dxo_ref, fo_ref, g_ref, sc_ref, gp_ref, dx_ref, dfp_ref, sums_ref = refs
        else:
            dh_ref, x_ref, dxo_ref, fo_ref, g_ref, sc_ref, dx_ref, sums_ref = refs
        dh_v, dxo_v = dh_ref[...], dxo_ref[...]
        y, r = _rms(x_ref[...])
        n = y * g_ref[...]
        dn = dh_v * (1.0 + sc_ref[...])
        dx = dxo_v + _rms_bwd(dn * g_ref[...], y, r)
        dx_ref[...] = dx
        if with_prev:
            dfp_ref[...] = (coef_prev * gp_ref[...] * dx).astype(BF16)
        @pl.when(pl.program_id(0) == 0)
        def _():
            sums_ref[...] = jnp.zeros_like(sums_ref)

        sums_ref[0:1, :] += jnp.sum(dh_v * n, axis=0, keepdims=True)
        sums_ref[1:2, :] += jnp.sum(dh_v, axis=0, keepdims=True)
        sums_ref[2:3, :] += jnp.sum(dn * y, axis=0, keepdims=True)
        sums_ref[3:4, :] += jnp.sum(coef_o * dxo_v * fo_ref[...], axis=0, keepdims=True)

    row, vec = _row_spec(tm, d), _vec_spec(d)
    sums_spec = pl.BlockSpec((8, d), lambda i: (0, 0))
    ins, in_specs = [dh, x, dxo, fo, g, sc], [row, row, row, row, vec, vec]
    out_shape, out_specs = [jax.ShapeDtypeStruct((t, d), F32)], [row]
    if with_prev:
        ins.append(gate_prev)
        in_specs.append(vec)
        out_shape.append(jax.ShapeDtypeStruct((t, d), BF16))
        out_specs.append(row)
    out_shape.append(jax.ShapeDtypeStruct((8, d), F32))
    out_specs.append(sums_spec)
    out, moved = _call(body, ins, name=name, grid=(t // tm,), in_specs=in_specs, out_specs=out_specs,
                       out_shape=out_shape, semantics=("arbitrary",), comm=comm)
    return out if comm is None else (out, moved)


def _ffn_up(h, w1, w3, name, tf=512, comm=None, relay_at=0.5):
    t, d = h.shape
    ff = w1.shape[1]
    tm = min(FFN_ROWS, t)

    def body(h_ref, w1_ref, w3_ref, a_ref, b_ref, s_ref):
        hv = h_ref[...]
        a = _dot(hv, w1_ref[...])
        b = _dot(hv, w3_ref[...])
        a_ref[...] = a
        b_ref[...] = b
        s_ref[...] = (a * jax.nn.sigmoid(a) * b).astype(BF16)

    tile = pl.BlockSpec((tm, tf), lambda i, j: (i, j))
    cols = pl.BlockSpec((d, tf), lambda i, j: (0, j))
    out, moved = _call(
        body, (h, w1, w3), name=name, grid=(t // tm, ff // tf),
        in_specs=[pl.BlockSpec((tm, d), lambda i, j: (i, 0)), cols, cols], out_specs=[tile, tile, tile],
        out_shape=[jax.ShapeDtypeStruct((t, ff), F32), jax.ShapeDtypeStruct((t, ff), F32),
                   jax.ShapeDtypeStruct((t, ff), BF16)],
        semantics=("parallel", "parallel"), comm=comm, relay_at=relay_at)
    return out if comm is None else (out, moved)


def _ffn_dact(df, a, b, w2, name, tf=512, comm=None):
    t, d = df.shape
    ff = a.shape[1]
    tm = min(FFN_ROWS, t)

    def body(df_ref, a_ref, b_ref, w2_ref, da_ref, db_ref):
        ds = _dot_nt(df_ref[...], w2_ref[...])
        av, bv = a_ref[...], b_ref[...]
        sg = jax.nn.sigmoid(av)
        da_ref[...] = (ds * bv * (sg * (1.0 + av * (1.0 - sg)))).astype(BF16)
        db_ref[...] = (ds * (av * sg)).astype(BF16)

    tile = pl.BlockSpec((tm, tf), lambda i, j: (i, j))
    out, moved = _call(
        body, (df, a, b, w2), name=name, grid=(t // tm, ff // tf),
        in_specs=[pl.BlockSpec((tm, d), lambda i, j: (i, 0)), tile, tile, pl.BlockSpec((tf, d), lambda i, j: (j, 0))],
        out_specs=[tile, tile], out_shape=[jax.ShapeDtypeStruct((t, ff), BF16)] * 2,
        semantics=("parallel", "parallel"), comm=comm)
    return out if comm is None else (out, moved)


def _ffn_dh(da, db, w1, w3, name, tn=512, comm=None):
    t, ff = da.shape
    d = w1.shape[0]
    tm, tk = min(FFN_ROWS, t), ff // 2

    def body(da_ref, db_ref, w1_ref, w3_ref, dh_ref):
        p = _dot_nt(da_ref[...], w1_ref[...]) + _dot_nt(db_ref[...], w3_ref[...])
        k = pl.program_id(2)

        @pl.when(k == 0)
        def _():
            dh_ref[...] = p

        @pl.when(k > 0)
        def _():
            dh_ref[...] += p

    act = pl.BlockSpec((tm, tk), lambda i, j, k: (i, k))
    wgt = pl.BlockSpec((tn, tk), lambda i, j, k: (j, k))
    out, moved = _call(
        body, (da, db, w1, w3), name=name, grid=(t // tm, d // tn, 2),
        in_specs=[act, act, wgt, wgt], out_specs=[pl.BlockSpec((tm, tn), lambda i, j, k: (i, j))],
        out_shape=[jax.ShapeDtypeStruct((t, d), F32)],
        semantics=("parallel", "parallel", "arbitrary"), comm=comm)
    return out[0] if comm is None else (out[0], moved)


def _loss_head(x_prev, f_prev, gate_prev, coef, target, name, tm=256):
    t, d = x_prev.shape
    steps = t // tm

    def body(xp_ref, fp_ref, gp_ref, tg_ref, dy_ref, df_ref, loss_ref, acc_ref):
        i = pl.program_id(0)
        e = xp_ref[...] + coef * gp_ref[...] * fp_ref[...] - tg_ref[...]
        dy = e * (1.0 / d)
        dy_ref[...] = dy
        df_ref[...] = (coef * gp_ref[...] * dy).astype(BF16)
        part = jnp.sum(e * e, axis=0, keepdims=True)

        @pl.when(i == 0)
        def _():
            acc_ref[...] = part

        @pl.when(i > 0)
        def _():
            acc_ref[...] += part

        @pl.when(i == steps - 1)
        def _():
            loss_ref[...] = jnp.sum(acc_ref[...], axis=1, keepdims=True) * (0.5 / d)

    row, vec = _row_spec(tm, d), _vec_spec(d)
    return _pc(
        body, name=name, grid=(steps,),
        in_specs=[row, row, vec, row], out_specs=[row, row, pl.BlockSpec((1, 1), lambda i: (0, 0))],
        out_shape=[jax.ShapeDtypeStruct((t, d), F32), jax.ShapeDtypeStruct((t, d), BF16),
                   jax.ShapeDtypeStruct((1, 1), F32)],
        scratch_shapes=[pltpu.VMEM((1, d), F32)],
        compiler_params=_params(("arbitrary",)),
    )(x_prev, f_prev, gate_prev, target)


def _split3(x):
    hi = x.astype(BF16)
    r1 = x - hi.astype(F32)
    mid = r1.astype(BF16)
    lo = (r1 - mid.astype(F32)).astype(BF16)
    return hi, mid, lo


def _select_matmul(a, onehot, name):
    m, n = a.shape[0], onehot.shape[1]

    def body(a_ref, oh_ref, o_ref):
        hi, mid, lo = _split3(a_ref[...])
        oh = oh_ref[...]
        o_ref[...] = (_dot(hi, oh) + _dot(mid, oh)) + _dot(lo, oh)

    return _pc(body, name=name, out_shape=jax.ShapeDtypeStruct((m, n), F32), compiler_params=_params())(a, onehot)


def _bucket_onehot():
    qi = np.arange(BLOCK)[:, None]
    kj = np.arange(2 * BLOCK)[None, :]
    dist = np.clip(qi + BLOCK - kj, 0, None)
    nf = np.maximum(dist, 1).astype(np.float32)
    large = 16 + (np.log(nf / np.float32(16)) / np.float32(math.log(128 / 16)) * np.float32(16)).astype(np.int32)
    bucket = np.where(dist < 16, dist, np.minimum(large, N_BUCKETS - 1)).reshape(-1)
    return (bucket[None, :] == np.arange(N_BUCKETS)[:, None]).astype(np.float32)


def _window_mask(n):
    row = lax.broadcasted_iota(jnp.int32, (GROUP * BLOCK, 2 * BLOCK), 0)
    qi = row & (BLOCK - 1)
    kj = lax.broadcasted_iota(jnp.int32, (GROUP * BLOCK, 2 * BLOCK), 1)
    return (kj > qi) & (kj <= qi + BLOCK) & ((kj >= BLOCK) | (n > 0))


def _kv_band(zc_ref, zp_ref, kvh):
    lo, hi = kvh * HEAD_DIM, (kvh + 1) * HEAD_DIM
    k_raw = jnp.concatenate([zp_ref[:, lo:hi], zc_ref[:, K_OFF + lo:K_OFF + hi]], axis=0)
    v_raw = jnp.concatenate([zp_ref[:, KV_WIDTH + lo:KV_WIDTH + hi], zc_ref[:, V_OFF + lo:V_OFF + hi]], axis=0)
    return k_raw, v_raw


def _stack_heads(ref, off):
    return jnp.concatenate([ref[:, off + g * HEAD_DIM:off + (g + 1) * HEAD_DIM] for g in range(GROUP)], axis=0)


def _unstack_heads(ref, off, stacked):
    for g in range(GROUP):
        ref[:, off + g * HEAD_DIM:off + (g + 1) * HEAD_DIM] = stacked[g * BLOCK:(g + 1) * BLOCK]


def _group_softmax(qn, kband, bias_ref, sink_ref, kvh, valid):
    bias = bias_ref[kvh * GROUP:(kvh + 1) * GROUP].reshape(GROUP * BLOCK, 2 * BLOCK)
    s = _dot_nt(qn, kband) * (HEAD_DIM ** -0.5) + bias
    s = jnp.where(valid, s, MASK_VALUE)
    sink = jnp.concatenate([jnp.full((BLOCK, 1), sink_ref[0, kvh * GROUP + g], F32) for g in range(GROUP)], axis=0)
    m = jnp.maximum(jnp.max(s, axis=-1, keepdims=True), sink)
    p = jnp.exp(s - m)
    esink = jnp.exp(sink - m)
    inv = 1.0 / (jnp.sum(p, axis=-1, keepdims=True) + esink)
    return p * inv, esink * inv, inv


def _mix_specs(nb, order):
    cur = pl.BlockSpec((BLOCK, IN_COLS), lambda n: (order(n), 0))
    prev = pl.BlockSpec((BLOCK, 2 * KV_WIDTH), lambda n: (jnp.maximum(order(n) - 1, 0), K_OFF // (2 * KV_WIDTH)))
    full = lambda shape: pl.BlockSpec(shape, lambda n: (0,) * len(shape))
    params = [full((A_HEADS, BLOCK, BLOCK)), full((BLOCK, A_HEADS)), full((A_HEADS, BLOCK)),
              full((1, HEAD_DIM)), full((1, HEAD_DIM)), pl.BlockSpec(memory_space=pltpu.SMEM),
              full((B_HEADS, BLOCK, 2 * BLOCK))]
    return cur, prev, params, full


def _mix_fwd(z, sw, sb_t, gv, gq, gk, sinks, bias, name, comm=None, relay_at=0.5):
    t = z.shape[0]
    nb = t // BLOCK

    def body(zc_ref, zp_ref, sw_ref, sbt_ref, gv_ref, gq_ref, gk_ref, sink_ref, bias_ref, y_ref):
        n = pl.program_id(0)
        ri = lax.broadcasted_iota(jnp.int32, (BLOCK, BLOCK), 0)
        ci = lax.broadcasted_iota(jnp.int32, (BLOCK, BLOCK), 1)
        tril = ri >= ci
        for h in range(A_HEADS):
            u = _gelu(zc_ref[:, h * BLOCK:(h + 1) * BLOCK])
            vv = _gelu(zc_ref[:, A_WIDTH + h * BLOCK:A_WIDTH + (h + 1) * BLOCK])
            vhat, _ = _rms(vv)
            vn = (vhat * gv_ref[h:h + 1, :]).astype(BF16)
            w = jnp.where(tril, sw_ref[h], 0.0).astype(BF16)
            mixed = _dot(w, vn) + sbt_ref[:, h:h + 1]
            y_ref[:, h * BLOCK:(h + 1) * BLOCK] = (u * mixed).astype(BF16)
        valid = _window_mask(n)
        for kvh in range(B_KV_HEADS):
            k_raw, v_raw = _kv_band(zc_ref, zp_ref, kvh)
            khat, _ = _rms(k_raw)
            kband = (khat * gk_ref[...]).astype(BF16)
            vband = v_raw.astype(BF16)
            qhat, _ = _rms(_stack_heads(zc_ref, Q_OFF + kvh * GROUP * HEAD_DIM))
            qn = (qhat * gq_ref[...]).astype(BF16)
            w, _, _ = _group_softmax(qn, kband, bias_ref, sink_ref, kvh, valid)
            o = _dot(w.astype(BF16), vband)
            _unstack_heads(y_ref, A_WIDTH + kvh * GROUP * HEAD_DIM, o.astype(BF16))

    cur, prev, params, _ = _mix_specs(nb, lambda n: n)
    out, moved = _call(
        body, (z, z, sw, sb_t, gv, gq, gk, sinks, bias), name=name, grid=(nb,), in_specs=[cur, prev] + params,
        out_specs=[pl.BlockSpec((BLOCK, A_WIDTH + B_WIDTH), lambda n: (n, 0))],
        out_shape=[jax.ShapeDtypeStruct((t, A_WIDTH + B_WIDTH), BF16)],
        semantics=("parallel",), comm=comm, relay_at=relay_at)
    return out[0] if comm is None else (out[0], moved)


def _mix_bwd(z, dy, sw, sb_t, gv, gq, gk, sinks, bias, name, comm=None):
    t = z.shape[0]
    nb = t // BLOCK

    def body(zc_ref, zp_ref, dy_ref, sw_ref, sbt_ref, gv_ref, gq_ref, gk_ref, sink_ref, bias_ref,
             dz_ref, dsw_ref, dsb_ref, dgv_ref, dgq_ref, dgk_ref, dsink_ref, dbias_ref, carry_ref):
        step = pl.program_id(0)
        n = nb - 1 - step

        @pl.when(step == 0)
        def _():
            dsw_ref[...] = jnp.zeros_like(dsw_ref)
            dsb_ref[...] = jnp.zeros_like(dsb_ref)
            dgv_ref[...] = jnp.zeros_like(dgv_ref)
            dgq_ref[...] = jnp.zeros_like(dgq_ref)
            dgk_ref[...] = jnp.zeros_like(dgk_ref)
            dsink_ref[...] = jnp.zeros_like(dsink_ref)
            dbias_ref[...] = jnp.zeros_like(dbias_ref)
            carry_ref[...] = jnp.zeros_like(carry_ref)

        ri = lax.broadcasted_iota(jnp.int32, (BLOCK, BLOCK), 0)
        ci = lax.broadcasted_iota(jnp.int32, (BLOCK, BLOCK), 1)
        tril = ri >= ci
        dsb = jnp.zeros((BLOCK, BLOCK), F32)
        for h in range(A_HEADS):
            uo, vo = h * BLOCK, A_WIDTH + h * BLOCK
            u_raw, v_raw = zc_ref[:, uo:uo + BLOCK], zc_ref[:, vo:vo + BLOCK]
            (u, du_raw), (vv, dv_raw) = _gelu_and_grad(u_raw), _gelu_and_grad(v_raw)
            vhat, r = _rms(vv)
            vn = (vhat * gv_ref[h:h + 1, :]).astype(BF16)
            w = jnp.where(tril, sw_ref[h], 0.0).astype(BF16)
            mixed = _dot(w, vn) + sbt_ref[:, h:h + 1]
            dya = dy_ref[:, uo:uo + BLOCK]
            dmixed = dya * u
            dm16 = dmixed.astype(BF16)
            dsw_ref[h] += jnp.where(tril, _dot_nt(dm16, vn), 0.0)
            dsb = dsb + jnp.where(ci == h, jnp.sum(dmixed, axis=1, keepdims=True), 0.0)
            dvn = _dot_tn(w, dm16)
            dgv_ref[h:h + 1, :] += jnp.sum(dvn * vhat, axis=0, keepdims=True)
            dvv = _rms_bwd(dvn * gv_ref[h:h + 1, :], vhat, r)
            dz_ref[:, uo:uo + BLOCK] = (dya * mixed * du_raw).astype(BF16)
            dz_ref[:, vo:vo + BLOCK] = (dvv * dv_raw).astype(BF16)
        dsb_ref[...] += dsb

        valid = _window_mask(n)
        lane = lax.broadcasted_iota(jnp.int32, (1, BLOCK), 1)
        dsink = jnp.zeros((1, BLOCK), F32)
        dgq = jnp.zeros((1, HEAD_DIM), F32)
        for kvh in range(B_KV_HEADS):
            ko, vo = K_OFF + kvh * HEAD_DIM, V_OFF + kvh * HEAD_DIM
            k_raw, v_raw = _kv_band(zc_ref, zp_ref, kvh)
            khat, kr = _rms(k_raw)
            kband = (khat * gk_ref[...]).astype(BF16)
            vband = v_raw.astype(BF16)
            qo = Q_OFF + kvh * GROUP * HEAD_DIM
            qhat, qr = _rms(_stack_heads(zc_ref, qo))
            qn = (qhat * gq_ref[...]).astype(BF16)
            w, wsink, _ = _group_softmax(qn, kband, bias_ref, sink_ref, kvh, valid)
            do = _stack_heads(dy_ref, A_WIDTH + kvh * GROUP * HEAD_DIM).astype(BF16)
            dvb = _dot_tn(w.astype(BF16), do)
            dw = _dot_nt(do, vband)
            rowdot = jnp.sum(w * dw, axis=-1, keepdims=True)
            ds = w * (dw - rowdot)
            dsink_rows = -wsink * rowdot
            for g in range(GROUP):
                head_sum = jnp.sum(dsink_rows[g * BLOCK:(g + 1) * BLOCK], axis=0, keepdims=True)
                dsink = dsink + jnp.where(lane == kvh * GROUP + g, head_sum, 0.0)
            dbias_ref[kvh * GROUP:(kvh + 1) * GROUP] += ds.reshape(GROUP, BLOCK, 2 * BLOCK)
            ds16 = (ds * (HEAD_DIM ** -0.5)).astype(BF16)
            dqn = _dot(ds16, kband)
            dkn = _dot_tn(ds16, qn)
            dgq = dgq + jnp.sum(dqn * qhat, axis=0, keepdims=True)
            _unstack_heads(dz_ref, qo, _rms_bwd(dqn * gq_ref[...], qhat, qr).astype(BF16))
            dgk_ref[...] += jnp.sum(dkn * khat, axis=0, keepdims=True)
            dk = _rms_bwd(dkn * gk_ref[...], khat, kr)
            co = kvh * HEAD_DIM
            dz_ref[:, ko:ko + HEAD_DIM] = (dk[BLOCK:] + carry_ref[:, co:co + HEAD_DIM]).astype(BF16)
            dz_ref[:, vo:vo + HEAD_DIM] = (dvb[BLOCK:] + carry_ref[:, KV_WIDTH + co:KV_WIDTH + co + HEAD_DIM]).astype(BF16)
            carry_ref[:, co:co + HEAD_DIM] = dk[:BLOCK]
            carry_ref[:, KV_WIDTH + co:KV_WIDTH + co + HEAD_DIM] = dvb[:BLOCK]
        dgq_ref[...] += dgq
        dsink_ref[...] += dsink

    order = lambda n: nb - 1 - n
    cur, prev, params, full = _mix_specs(nb, order)
    dy_spec = pl.BlockSpec((BLOCK, A_WIDTH + B_WIDTH), lambda n: (order(n), 0))
    out, moved = _call(
        body, (z, z, dy, sw, sb_t, gv, gq, gk, sinks, bias), name=name, grid=(nb,),
        in_specs=[cur, prev, dy_spec] + params,
        out_specs=[cur, full((A_HEADS, BLOCK, BLOCK)), full((BLOCK, BLOCK)), full((A_HEADS, BLOCK)),
                   full((1, HEAD_DIM)), full((1, HEAD_DIM)), full((1, BLOCK)), full((B_HEADS, BLOCK, 2 * BLOCK))],
        out_shape=[jax.ShapeDtypeStruct((t, IN_COLS), BF16), jax.ShapeDtypeStruct((A_HEADS, BLOCK, BLOCK), F32),
                   jax.ShapeDtypeStruct((BLOCK, BLOCK), F32), jax.ShapeDtypeStruct((A_HEADS, BLOCK), F32),
                   jax.ShapeDtypeStruct((1, HEAD_DIM), F32), jax.ShapeDtypeStruct((1, HEAD_DIM), F32),
                   jax.ShapeDtypeStruct((1, BLOCK), F32), jax.ShapeDtypeStruct((B_HEADS, BLOCK, 2 * BLOCK), F32)],
        scratch_shapes=[pltpu.VMEM((BLOCK, 2 * KV_WIDTH), F32)],
        semantics=("arbitrary",), comm=comm)
    return out if comm is None else (out, moved)


def _ada_fwd(c_all, w_ada, b_cols, name, tn=768):
    nb, d = c_all.shape
    cols = w_ada.shape[1]

    def body(c_ref, w_ref, b_ref, act_ref, mod_ref):
        cv = c_ref[...]
        act = cv * jax.nn.sigmoid(cv)
        act_ref[...] = act
        mod_ref[...] = _dot(act.astype(BF16), w_ref[...].astype(BF16)) + b_ref[...]

    return _pc(
        body, name=name, grid=(cols // tn,),
        in_specs=[pl.BlockSpec((nb, d), lambda j: (0, 0)), pl.BlockSpec((d, tn), lambda j: (0, j)),
                  pl.BlockSpec((1, tn), lambda j: (0, j))],
        out_specs=[pl.BlockSpec((nb, d), lambda j: (0, 0)), pl.BlockSpec((nb, tn), lambda j: (0, j))],
        out_shape=[jax.ShapeDtypeStruct((nb, d), F32), jax.ShapeDtypeStruct((nb, cols), F32)],
        compiler_params=_params(("arbitrary",)),
    )(c_all, w_ada, b_cols)


def _adamw(w, g, m, v):
    m = ADAM_B1 * m + (1.0 - ADAM_B1) * g
    v = ADAM_B2 * v + (1.0 - ADAM_B2) * (g * g)
    m_hat = m / (1.0 - ADAM_B1 ** ADAM_STEP)
    v_hat = v / (1.0 - ADAM_B2 ** ADAM_STEP)
    delta = -ADAM_LR * (m_hat / (jnp.sqrt(v_hat) + ADAM_EPS) + ADAM_WD * w)
    return delta, m, v


def _adam_outer(act_t, dmod, w, m, v, name, tr=256):
    rows, cols = w.shape
    nb = dmod.shape[0]

    def body(act_ref, dm_ref, w_ref, m_ref, v_ref, g_ref, d_ref, nm_ref, nv_ref):
        act = act_ref[...].astype(BF16).astype(F32)
        dm = dm_ref[...].astype(BF16).astype(F32)
        g = act[:, 0:1] * dm[0:1, :]
        for b in range(1, nb):
            g = g + act[:, b:b + 1] * dm[b:b + 1, :]
        g_ref[...] = g
        d_ref[...], nm_ref[...], nv_ref[...] = _adamw(w_ref[...], g, m_ref[...], v_ref[...])

    tile = pl.BlockSpec((tr, cols), lambda i: (i, 0))
    return _pc(
        body, name=name, grid=(rows // tr,),
        in_specs=[pl.BlockSpec((tr, nb), lambda i: (i, 0)), pl.BlockSpec((nb, cols), lambda i: (0, 0)), tile, tile, tile],
        out_specs=[tile] * 4, out_shape=[jax.ShapeDtypeStruct((rows, cols), F32)] * 4,
        compiler_params=_params(("parallel",)),
    )(act_t, dmod, w, m, v)


def _adam_reduce(parts, w, m, v, name, tr):
    rows, cols = w.shape

    def body(p_ref, w_ref, m_ref, v_ref, g_ref, d_ref, nm_ref, nv_ref):
        g = p_ref[0].astype(F32)
        for i in range(1, N_DEV):
            g = g + p_ref[i].astype(F32)
        g_ref[...] = g
        d_ref[...], nm_ref[...], nv_ref[...] = _adamw(w_ref[...], g, m_ref[...], v_ref[...])

    tile = pl.BlockSpec((tr, cols), lambda i: (i, 0))
    return _pc(
        body, name=name, grid=(rows // tr,),
        in_specs=[pl.BlockSpec((N_DEV, tr, cols), lambda i: (0, i, 0)), tile, tile, tile],
        out_specs=[tile] * 4, out_shape=[jax.ShapeDtypeStruct((rows, cols), F32)] * 4,
        compiler_params=_params(("parallel",)),
    )(parts, w, m, v)


def _ungather_cols(g):
    n, r, c = g.shape
    return jnp.transpose(g, (1, 0, 2)).reshape(r, n * c)


def _scatter_cols(w):
    r, c = w.shape
    return jnp.transpose(w.reshape(r, N_DEV, c // N_DEV), (1, 0, 2)).astype(BF16)


def _pad_rows(v):
    flat = v.reshape(-1)
    pad = (-flat.shape[0]) % (8 * BLOCK)
    return jnp.pad(flat, (0, pad)).reshape(-1, BLOCK)


def kernel(x, c, w_ada, b_ada, g_ffn1, w1_ffn1, w3_ffn1, w2_ffn1, g_mix, w_in, spatial_w, spatial_b, g_v, g_q, g_k, sinks, rel_bias, w_out, g_ffn2, w1_ffn2, w3_ffn2, w2_ffn2, loss_target, m_w_ada, m_b_ada, m_g_ffn1, m_w1_ffn1, m_w3_ffn1, m_w2_ffn1, m_g_mix, m_w_in, m_spatial_w, m_spatial_b, m_g_v, m_g_q, m_g_k, m_sinks, m_rel_bias, m_w_out, m_g_ffn2, m_w1_ffn2, m_w3_ffn2, m_w2_ffn2, v_w_ada, v_b_ada, v_g_ffn1, v_w1_ffn1, v_w3_ffn1, v_w2_ffn1, v_g_mix, v_w_in, v_spatial_w, v_spatial_b, v_g_v, v_g_q, v_g_k, v_sinks, v_rel_bias, v_w_out, v_g_ffn2, v_w1_ffn2, v_w3_ffn2, v_w2_ffn2):
    me = _linear(_position())
    d = D_MODEL
    x0 = x[0]
    target = loss_target[0]
    t = x0.shape[0]

    big = dict(w1a=w1_ffn1[0], w3a=w3_ffn1[0], w2a=w2_ffn1[0], w_in=w_in[0], w_out=w_out[0],
               w1b=w1_ffn2[0], w3b=w3_ffn2[0], w2b=w2_ffn2[0])
    col_sharded = ("w1a", "w3a", "w_in", "w1b", "w3b")
    full = {}

    def gather(keys):
        return _GatherTwoLevel([big[k].astype(BF16) for k in keys])

    def unpack(keys, gathered):
        for k, gth in zip(keys, gathered):
            full[k] = _ungather_cols(gth) if k in col_sharded else gth.reshape(-1, gth.shape[-1])

    def share(k):
        gk_ = grads[k]
        if k in col_sharded:
            return _scatter_cols(gk_)
        return gk_.reshape(N_DEV, gk_.shape[0] // N_DEV, gk_.shape[1]).astype(BF16)

    unpack(("w1a", "w3a"), _run_comm(gather(("w1a", "w3a")), "gather_ffn1"))

    c_all = _run_comm(_Direct([c], True), "gather_c")[0].reshape(N_DEV, d)
    b_cols = lax.dynamic_slice(b_ada, (0, me * MOD_COLS), (1, MOD_COLS))
    act_all, mod_cols = _ada_fwd(c_all, w_ada[0], b_cols, "ada_fwd")
    mod = _run_comm(_Direct([mod_cols.reshape(N_DEV, 1, MOD_COLS)], False), "scatter_mod")[0]
    mod = mod.reshape(N_MOD, 1, d)
    sh1, sc1, gt1, sh2, sc2, gt2, sh3, sc3, gt3 = [mod[i] for i in range(N_MOD)]

    tm = min(FFN_ROWS, t)
    h1 = _norm_fwd(x0, None, None, 0.0, g_ffn1, sc1, sh1, "norm1_fwd")
    (a1, b1, s1), gathered = _ffn_up(h1, full["w1a"], full["w3a"], "ffn1_up",
                                     comm=gather(("w2a", "w_in", "w_out")), relay_at=0.6)
    unpack(("w2a", "w_in", "w_out"), gathered)
    f1 = _matmul(s1, full["w2a"], "nn", tm, 512, D_FF, F32, "ffn1_down")
    x1, h2 = _norm_fwd(x0, f1, gt1, 0.5, g_mix, sc2, sh2, "norm2_fwd")
    z = _matmul(h2, full["w_in"], "nn", 512, IN_COLS // 2, d, F32, "mix_in_fwd")
    onehot = _bucket_onehot()
    bias = _select_matmul(rel_bias.T, jnp.asarray(onehot, BF16), "bias_table").reshape(B_HEADS, BLOCK, 2 * BLOCK)
    sb_t = spatial_b[0].T
    mix_params = (spatial_w[0], sb_t, g_v[0], g_q, g_k, sinks, bias)
    ycat, gathered = _mix_fwd(z, *mix_params, "mix_core_fwd", comm=gather(("w1b", "w3b")), relay_at=0.7)
    unpack(("w1b", "w3b"), gathered)
    y = _matmul(ycat, full["w_out"], "nn", 512, d, d, F32, "mix_out_fwd")
    x2, h3 = _norm_fwd(x1, y, gt2, 1.0, g_ffn2, sc3, sh3, "norm3_fwd")
    (a2, b2, s2), gathered = _ffn_up(h3, full["w1b"], full["w3b"], "ffn2_up", comm=gather(("w2b",)), relay_at=0.6)
    unpack(("w2b",), gathered)
    f2 = _matmul(s2, full["w2b"], "nn", tm, 512, D_FF, F32, "ffn2_down")
    dx3, df2, loss_part = _loss_head(x2, f2, gt3, 0.5, target, "loss_head")
    loss = lax.psum(loss_part[0, 0], ("x", "y", "c"))

    grads, parts = {}, {}
    da, db = _ffn_dact(df2, a2, b2, full["w2b"], "ffn2_dact")
    dh3 = _ffn_dh(da, db, full["w1b"], full["w3b"], "ffn2_dh")
    tk = min(DW_TOKENS, t)
    grads["w1b"] = _matmul(h3, da, "tn", 1024, 512, tk, F32, "ffn2_dw1")
    grads["w3b"], moved = _matmul(h3, db, "tn", 1024, 512, tk, F32, "ffn2_dw3", comm=_Direct([share("w1b")], False))
    parts["w1b"] = moved[0]
    grads["w2b"], moved = _matmul(s2, df2, "tn", 512, 1024, tk, F32, "ffn2_dw2", comm=_Direct([share("w3b")], False))
    parts["w3b"] = moved[0]
    dx2, dyg, sums3 = _norm_bwd(dh3, x2, dx3, f2, 0.5, g_ffn2, sc3, gt2, 1.0, "norm3_bwd")
    dycat = _matmul(dyg, full["w_out"], "nt", 512, d, d, F32, "mix_out_bwd")
    grads["w_out"] = _matmul(ycat, dyg, "tn", 1024, 512, tk, F32, "mix_out_dw")
    mix_grads, moved = _mix_bwd(z, dycat, *mix_params, "mix_core_bwd",
                                comm=_Direct([share("w2b")], False))
    parts["w2b"] = moved[0]
    dz, d_sw, d_sb, d_gv, d_gq, d_gk, d_sink, d_bias = mix_grads
    d_rel = _select_matmul(d_bias.reshape(B_HEADS, BLOCK * 2 * BLOCK), jnp.asarray(onehot.T, BF16), "bias_table_bwd")
    dh2 = _matmul(dz, full["w_in"], "nt", 512, d, IN_COLS // 2, F32, "mix_in_bwd")
    grads["w_in"] = _matmul(h2, dz, "tn", 1024, 256, tk, F32, "mix_in_dw")
    dx1, df1, sums2 = _norm_bwd(dh2, x1, dx2, y, 1.0, g_mix, sc2, gt1, 0.5, "norm2_bwd")
    da, db = _ffn_dact(df1, a1, b1, full["w2a"], "ffn1_dact")
    dh1, moved = _ffn_dh(da, db, full["w1a"], full["w3a"], "ffn1_dh",
                         comm=_Direct([share("w_in"), share("w_out")], False))
    parts["w_in"], parts["w_out"] = moved
    grads["w1a"] = _matmul(h1, da, "tn", 1024, 512, tk, F32, "ffn1_dw1")
    grads["w3a"], moved = _matmul(h1, db, "tn", 1024, 512, tk, F32, "ffn1_dw3", comm=_Direct([share("w1a")], False))
    parts["w1a"] = moved[0]
    grads["w2a"], moved = _matmul(s1, df1, "tn", 512, 1024, tk, F32, "ffn1_dw2", comm=_Direct([share("w3a")], False))
    parts["w3a"] = moved[0]
    (dx0, sums1), moved = _norm_bwd(dh1, x0, dx1, f1, 0.5, g_ffn1, sc1, None, 0.0, "norm1_bwd",
                                    comm=_Direct([share("w2a")], False))
    parts["w2a"] = moved[0]

    moments = dict(w1a=(m_w1_ffn1, v_w1_ffn1), w3a=(m_w3_ffn1, v_w3_ffn1), w2a=(m_w2_ffn1, v_w2_ffn1),
                   w_in=(m_w_in, v_w_in), w_out=(m_w_out, v_w_out),
                   w1b=(m_w1_ffn2, v_w1_ffn2), w3b=(m_w3_ffn2, v_w3_ffn2), w2b=(m_w2_ffn2, v_w2_ffn2))
    upd = {}
    for k in big:
        rows = big[k].shape[0]
        tr = 256 if rows % 256 == 0 else rows // 4
        upd[k] = [o[None] for o in _adam_reduce(parts[k], big[k], moments[k][0][0], moments[k][1][0], "adam_" + k, tr)]

    dmod = jnp.concatenate([sums1[1:2], sums1[0:1], sums1[3:4], sums2[1:2], sums2[0:1], sums2[3:4],
                            sums3[1:2], sums3[0:1], sums3[3:4]], axis=1)
    small = [("b_ada", dmod, b_ada, m_b_ada, v_b_ada),
             ("g_ffn1", sums1[2:3], g_ffn1, m_g_ffn1, v_g_ffn1),
             ("g_mix", sums2[2:3], g_mix, m_g_mix, v_g_mix),
             ("spatial_w", d_sw[None], spatial_w, m_spatial_w, v_spatial_w),
             ("spatial_b", d_sb[:, :A_HEADS].T[None], spatial_b, m_spatial_b, v_spatial_b),
             ("g_v", d_gv[None], g_v, m_g_v, v_g_v),
             ("g_q", d_gq, g_q, m_g_q, v_g_q),
             ("g_k", d_gk, g_k, m_g_k, v_g_k),
             ("sinks", d_sink[:, :B_HEADS], sinks, m_sinks, v_sinks),
             ("rel_bias", d_rel.T, rel_bias, m_rel_bias, v_rel_bias),
             ("g_ffn2", sums3[2:3], g_ffn2, m_g_ffn2, v_g_ffn2)]
    packed = [jnp.concatenate([_pad_rows(item[i]) for item in small], axis=0) for i in range(1, 5)]
    g_all = _run_comm(_Direct([packed[0]], True), "gather_small")[0]
    small_out = _adam_reduce(g_all, packed[1], packed[2], packed[3], "adam_small", packed[0].shape[0])
    off = 0
    for name, g_, w_, _, _ in small:
        n_rows = _pad_rows(w_).shape[0]
        upd[name] = [o[off:off + n_rows].reshape(-1)[:w_.size].reshape(w_.shape) for o in small_out]
        off += n_rows

    dmod_rows = g_all[:, :N_MOD * d // BLOCK, :].reshape(N_DEV, N_MOD * d)
    dmod_cols = lax.dynamic_slice(dmod_rows, (0, me * MOD_COLS), (N_DEV, MOD_COLS))
    upd["w_ada"] = [o[None] for o in _adam_outer(act_all.T, dmod_cols, w_ada[0], m_w_ada[0], v_w_ada[0], "adam_w_ada")]

    order = [("w_ada", "w_ada"), ("b_ada", "b_ada"), ("g_ffn1", "g_ffn1"), ("w1_ffn1", "w1a"), ("w3_ffn1", "w3a"),
             ("w2_ffn1", "w2a"), ("g_mix", "g_mix"), ("w_in", "w_in"), ("spatial_w", "spatial_w"),
             ("spatial_b", "spatial_b"), ("g_v", "g_v"), ("g_q", "g_q"), ("g_k", "g_k"), ("sinks", "sinks"),
             ("rel_bias", "rel_bias"), ("w_out", "w_out"), ("g_ffn2", "g_ffn2"), ("w1_ffn2", "w1b"),
             ("w3_ffn2", "w3b"), ("w2_ffn2", "w2b")]
    outs = [loss, dx0[None]]
    for i in range(4):
        outs += [upd[key][i] for _, key in order]
    return tuple(outs)
```

```python
import functools
import math

import numpy as np
import jax
import jax.numpy as jnp
from jax import lax
from jax.experimental import pallas as pl
from jax.experimental.pallas import tpu as pltpu

F32 = jnp.float32
BF16 = jnp.bfloat16

D_MODEL = 2048
D_FF = 5632
BLOCK = 128
A_HEADS = 8
A_WIDTH = 1024
B_HEADS = 16
B_KV_HEADS = 2
GROUP = B_HEADS // B_KV_HEADS
HEAD_DIM = 64
B_WIDTH = 1024
KV_WIDTH = 128
IN_COLS = 3328
Q_OFF = 2 * A_WIDTH
K_OFF = Q_OFF + B_WIDTH
V_OFF = K_OFF + KV_WIDTH
N_BUCKETS = 32
N_MOD = 9
EPS = 1e-6
N_DEV = 8
MOD_COLS = N_MOD * D_MODEL // N_DEV

ADAM_LR = 0.001
ADAM_B1 = 0.9
ADAM_B2 = 0.999
ADAM_EPS = 1e-08
ADAM_WD = 0.01
ADAM_STEP = 10

FFN_ROWS = 1024
DW_TOKENS = 4096
MASK_VALUE = -1e30
VMEM_LIMIT = 56 * 1024 * 1024
MESH_ID = pl.DeviceIdType.MESH
ANY = pl.BlockSpec(memory_space=pl.ANY)

_SQRT_HALF = 0.7071067811865476
_INV_SQRT_2PI = 0.3989422804014327


def _pc(body, **kw):
    return pl.pallas_call(body, **kw)


def _params(sem=None):
    return pltpu.CompilerParams(dimension_semantics=sem, vmem_limit_bytes=VMEM_LIMIT)


def _dot(a, b):
    return lax.dot_general(a, b, (((1,), (0,)), ((), ())), preferred_element_type=F32)


def _dot_nt(a, b):
    return lax.dot_general(a, b, (((1,), (1,)), ((), ())), preferred_element_type=F32)


def _dot_tn(a, b):
    return lax.dot_general(a, b, (((0,), (0,)), ((), ())), preferred_element_type=F32)


def _gelu(x):
    return 0.5 * x * (1.0 + lax.erf(x * _SQRT_HALF))


def _gelu_and_grad(x):
    cdf = 0.5 * (1.0 + lax.erf(x * _SQRT_HALF))
    return x * cdf, cdf + x * jnp.exp(-0.5 * x * x) * _INV_SQRT_2PI


def _rms(x):
    r = lax.rsqrt(jnp.mean(x * x, axis=-1, keepdims=True) + EPS)
    return x * r, r


def _rms_bwd(dy, y, r):
    return r * (dy - y * jnp.mean(dy * y, axis=-1, keepdims=True))


def _row_spec(tm, cols):
    return pl.BlockSpec((tm, cols), lambda i: (i, 0))


def _vec_spec(cols):
    return pl.BlockSpec((1, cols), lambda i: (0, 0))


def _position():
    x, y, c = lax.axis_index("x"), lax.axis_index("y"), lax.axis_index("c")
    return x, y, c


def _linear(p):
    return 4 * p[0] + 2 * p[1] + p[2]


class _Comm:
    PEER_COPIES = N_DEV - 1

    def __init__(self, arrs):
        self.arrs = list(arrs)
        n = len(self.arrs)
        self.scratch_shapes = [pltpu.SemaphoreType.DMA((self.PEER_COPIES * n,)),
                               pltpu.SemaphoreType.DMA((self.PEER_COPIES * n,)),
                               pltpu.SemaphoreType.DMA((n,))]

    def bind(self, srcs, dsts, sems):
        self.srcs, self.dsts = srcs, dsts
        self.send_sems, self.recv_sems, self.local_sems = sems
        x, y, c = _position()
        self.me, self.sibling, self.core = (x, y, c), (x, y, 1 - c), c
        self.chips = [(1 - x, y), (x, 1 - y), (1 - x, 1 - y)]
        self.peers = [(1 - x if k & 4 else x, 1 - y if k & 2 else y, 1 - c if k & 1 else c)
                      for k in range(1, N_DEV)]

    def relay(self):
        pass


class _GatherTwoLevel(_Comm):
    def __init__(self, arrs):
        super().__init__(arrs)
        self.out_shape = [jax.ShapeDtypeStruct((N_DEV,) + a.shape, a.dtype) for a in self.arrs]

    def _copy(self, a, k, block, to, from_input=False):
        rows = self.dsts[a].at[_linear(block)]
        return pltpu.make_async_remote_copy(
            src_ref=self.srcs[a] if from_input else rows, dst_ref=rows,
            send_sem=self.send_sems.at[self.PEER_COPIES * a + k], recv_sem=self.recv_sems.at[self.PEER_COPIES * a + k],
            device_id=to, device_id_type=MESH_ID)

    def _local(self, a):
        return pltpu.make_async_copy(self.srcs[a], self.dsts[a].at[_linear(self.me)], self.local_sems.at[a])

    def _first(self, a):
        return [self._copy(a, 0, self.me, self.sibling, True)] + [
            self._copy(a, 1 + j, self.me, (*chip, self.core), True) for j, chip in enumerate(self.chips)]

    def _passed(self, a):
        return [self._copy(a, 4 + j, (*chip, self.core), self.sibling) for j, chip in enumerate(self.chips)]

    def start(self):
        for a in range(len(self.arrs)):
            self._local(a).start()
            for cp in self._first(a):
                cp.start()

    def relay(self):
        for a in range(len(self.arrs)):
            for j, chip in enumerate(self.chips):
                self._copy(a, 1 + j, (*chip, self.core), self.me).wait_recv()
                self._passed(a)[j].start()

    def finish(self):
        for a in range(len(self.arrs)):
            self._copy(a, 0, self.sibling, self.me).wait_recv()
            for j, chip in enumerate(self.chips):
                self._copy(a, 4 + j, (*chip, 1 - self.core), self.me).wait_recv()
        for a in range(len(self.arrs)):
            for cp in self._first(a) + self._passed(a):
                cp.wait_send()
            self._local(a).wait()


class _Direct(_Comm):
    def __init__(self, arrs, broadcast):
        super().__init__(arrs)
        self.broadcast = broadcast
        self.out_shape = [jax.ShapeDtypeStruct(((N_DEV,) + a.shape) if broadcast else a.shape, a.dtype)
                          for a in self.arrs]

    def _outgoing(self, a, to):
        return self.srcs[a] if self.broadcast else self.srcs[a].at[_linear(to)]

    def _copy(self, a, k, sender, to):
        return pltpu.make_async_remote_copy(
            src_ref=self._outgoing(a, to), dst_ref=self.dsts[a].at[_linear(sender)],
            send_sem=self.send_sems.at[self.PEER_COPIES * a + k], recv_sem=self.recv_sems.at[self.PEER_COPIES * a + k],
            device_id=to, device_id_type=MESH_ID)

    def _local(self, a):
        return pltpu.make_async_copy(self._outgoing(a, self.me), self.dsts[a].at[_linear(self.me)],
                                     self.local_sems.at[a])

    def start(self):
        for a in range(len(self.arrs)):
            self._local(a).start()
            for k, peer in enumerate(self.peers):
                self._copy(a, k, self.me, peer).start()

    def finish(self):
        for a in range(len(self.arrs)):
            for k, peer in enumerate(self.peers):
                self._copy(a, k, peer, self.me).wait_recv()
        for a in range(len(self.arrs)):
            for k, peer in enumerate(self.peers):
                self._copy(a, k, self.me, peer).wait_send()
            self._local(a).wait()


def _run_comm(comm, name):
    n = len(comm.arrs)

    def body(*refs):
        comm.bind(refs[:n], refs[n:2 * n], refs[2 * n:])
        comm.start()
        comm.relay()
        comm.finish()

    return _pc(body, name=name, out_shape=comm.out_shape, in_specs=[ANY] * n, out_specs=[ANY] * n,
               scratch_shapes=comm.scratch_shapes)(*comm.arrs)


def _call(body, inputs, *, name, grid, in_specs, out_specs, out_shape, scratch_shapes=(), semantics,
          comm=None, relay_at=0.5):
    if comm is None:
        out = _pc(body, name=name, grid=grid, in_specs=in_specs, out_specs=out_specs, out_shape=out_shape,
                  scratch_shapes=list(scratch_shapes), compiler_params=_params(semantics))(*inputs)
        return list(out), []
    n_in, n_out, n_sc, k = len(in_specs), len(out_specs), len(scratch_shapes), len(comm.arrs)
    steps = math.prod(grid)
    relay_step = min(steps - 1, int(steps * relay_at))

    def carrier(*refs):
        ins, rest = refs[:n_in], refs[n_in:]
        csrc, rest = rest[:k], rest[k:]
        outs, rest = rest[:n_out], rest[n_out:]
        cdst, rest = rest[:k], rest[k:]
        scratch, csems = rest[:n_sc], rest[n_sc:]
        step = 0
        for axis, size in enumerate(grid):
            step = step * size + pl.program_id(axis)
        comm.bind(csrc, cdst, csems)
        pl.when(step == 0)(comm.start)
        pl.when(step == relay_step)(comm.relay)
        body(*ins, *outs, *scratch)
        pl.when(step == steps - 1)(comm.finish)

    out = _pc(carrier, name=name, grid=grid, in_specs=list(in_specs) + [ANY] * k,
              out_specs=list(out_specs) + [ANY] * k, out_shape=list(out_shape) + comm.out_shape,
              scratch_shapes=list(scratch_shapes) + comm.scratch_shapes,
              compiler_params=_params(("arbitrary",) * len(grid)))(*inputs, *comm.arrs)
    return list(out[:n_out]), list(out[n_out:])


def _matmul(a, b, mode, tm, tn, tk, out_dtype, name, comm=None):
    if mode == "nn":
        (m, kk), nn = a.shape, b.shape[1]
        a_spec = pl.BlockSpec((tm, tk), lambda i, j, k: (i, k))
        b_spec = pl.BlockSpec((tk, tn), lambda i, j, k: (k, j))
        dot = _dot
    elif mode == "nt":
        (m, kk), nn = a.shape, b.shape[0]
        a_spec = pl.BlockSpec((tm, tk), lambda i, j, k: (i, k))
        b_spec = pl.BlockSpec((tn, tk), lambda i, j, k: (j, k))
        dot = _dot_nt
    else:
        (kk, m), nn = a.shape, b.shape[1]
        a_spec = pl.BlockSpec((tk, tm), lambda i, j, k: (k, i))
        b_spec = pl.BlockSpec((tk, tn), lambda i, j, k: (k, j))
        dot = _dot_tn
    assert m % tm == 0 and nn % tn == 0 and kk % tk == 0, (a.shape, b.shape, tm, tn, tk)
    nk = kk // tk
    assert nk == 1 or out_dtype == F32

    def body(a_ref, b_ref, o_ref):
        p = dot(a_ref[...], b_ref[...])
        if nk == 1:
            o_ref[...] = p.astype(o_ref.dtype)
        else:
            k = pl.program_id(2)

            @pl.when(k == 0)
            def _():
                o_ref[...] = p

            @pl.when(k > 0)
            def _():
                o_ref[...] += p

    out, moved = _call(
        body, (a, b), name=name, grid=(m // tm, nn // tn, nk),
        in_specs=[a_spec, b_spec], out_specs=[pl.BlockSpec((tm, tn), lambda i, j, k: (i, j))],
        out_shape=[jax.ShapeDtypeStruct((m, nn), out_dtype)],
        semantics=("parallel", "parallel", "arbitrary"), comm=comm)
    return out[0] if comm is None else (out[0], moved)


def _norm_fwd(x_prev, f_prev, gate_prev, coef, g, sc, sh, name, tm=256):
    t, d = x_prev.shape
    residual = f_prev is not None

    def body(*refs):
        if residual:
            xp_ref, fp_ref, gp_ref, g_ref, sc_ref, sh_ref, x_ref, h_ref = refs
            x = xp_ref[...] + coef * gp_ref[...] * fp_ref[...]
            x_ref[...] = x
        else:
            xp_ref, g_ref, sc_ref, sh_ref, h_ref = refs
            x = xp_ref[...]
        y, _ = _rms(x)
        h_ref[...] = ((y * g_ref[...]) * (1.0 + sc_ref[...]) + sh_ref[...]).astype(BF16)

    row, vec = _row_spec(tm, d), _vec_spec(d)
    if residual:
        ins, in_specs = (x_prev, f_prev, gate_prev, g, sc, sh), [row, row, vec, vec, vec, vec]
        out_shape = [jax.ShapeDtypeStruct((t, d), F32), jax.ShapeDtypeStruct((t, d), BF16)]
        out_specs = [row, row]
    else:
        ins, in_specs = (x_prev, g, sc, sh), [row, vec, vec, vec]
        out_shape = [jax.ShapeDtypeStruct((t, d), BF16)]
        out_specs = [row]
    out = _pc(body, name=name, grid=(t // tm,), in_specs=in_specs, out_specs=out_specs, out_shape=out_shape,
              compiler_params=_params(("parallel",)))(*ins)
    return out if residual else out[0]


def _norm_bwd(dh, x, dxo, fo, coef_o, g, sc, gate_prev, coef_prev, name, tm=256, comm=None):
    t, d = x.shape
    with_prev = gate_prev is not None

    def body(*refs):
        if with_prev:
            dh_ref, x_ref, dxo_ref, fo_ref, g_ref, sc_ref, gp_ref, dx_ref, dfp_ref, sums_ref = refs
        else:
            dh_ref, x_ref, dxo_ref, fo_ref, g_ref, sc_ref, dx_ref, sums_ref = refs
        dh_v, dxo_v = dh_ref[...], dxo_ref[...]
        y, r = _rms(x_ref[...])
        n = y * g_ref[...]
        dn = dh_v * (1.0 + sc_ref[...])
        dx = dxo_v + _rms_bwd(dn * g_ref[...], y, r)
        dx_ref[...] = dx
        if with_prev:
            dfp_ref[...] = (coef_prev * gp_ref[...] * dx).astype(BF16)
        @pl.when(pl.program_id(0) == 0)
        def _():
            sums_ref[...] = jnp.zeros_like(sums_ref)

        sums_ref[0:1, :] += jnp.sum(dh_v * n, axis=0, keepdims=True)
        sums_ref[1:2, :] += jnp.sum(dh_v, axis=0, keepdims=True)
        sums_ref[2:3, :] += jnp.sum(dn * y, axis=0, keepdims=True)
        sums_ref[3:4, :] += jnp.sum(coef_o * dxo_v * fo_ref[...], axis=0, keepdims=True)

    row, vec = _row_spec(tm, d), _vec_spec(d)
    sums_spec = pl.BlockSpec((8, d), lambda i: (0, 0))
    ins, in_specs = [dh, x, dxo, fo, g, sc], [row, row, row, row, vec, vec]
    out_shape, out_specs = [jax.ShapeDtypeStruct((t, d), F32)], [row]
    if with_prev:
        ins.append(gate_prev)
        in_specs.append(vec)
        out_shape.append(jax.ShapeDtypeStruct((t, d), BF16))
        out_specs.append(row)
    out_shape.append(jax.ShapeDtypeStruct((8, d), F32))
    out_specs.append(sums_spec)
    out, moved = _call(body, ins, name=name, grid=(t // tm,), in_specs=in_specs, out_specs=out_specs,
                       out_shape=out_shape, semantics=("arbitrary",), comm=comm)
    return out if comm is None else (out, moved)


def _ffn_up(h, w1, w3, name, tf=512, comm=None, relay_at=0.5):
    t, d = h.shape
    ff = w1.shape[1]
    tm = min(FFN_ROWS, t)

    def body(h_ref, w1_ref, w3_ref, a_ref, b_ref, s_ref):
        hv = h_ref[...]
        a = _dot(hv, w1_ref[...])
        b = _dot(hv, w3_ref[...])
        a_ref[...] = a
        b_ref[...] = b
        s_ref[...] = (a * jax.nn.sigmoid(a) * b).astype(BF16)

    tile = pl.BlockSpec((tm, tf), lambda i, j: (i, j))
    cols = pl.BlockSpec((d, tf), lambda i, j: (0, j))
    out, moved = _call(
        body, (h, w1, w3), name=name, grid=(t // tm, ff // tf),
        in_specs=[pl.BlockSpec((tm, d), lambda i, j: (i, 0)), cols, cols], out_specs=[tile, tile, tile],
        out_shape=[jax.ShapeDtypeStruct((t, ff), F32), jax.ShapeDtypeStruct((t, ff), F32),
                   jax.ShapeDtypeStruct((t, ff), BF16)],
        semantics=("parallel", "parallel"), comm=comm, relay_at=relay_at)
    return out if comm is None else (out, moved)


def _ffn_dact(df, a, b, w2, name, tf=512, comm=None):
    t, d = df.shape
    ff = a.shape[1]
    tm = min(FFN_ROWS, t)

    def body(df_ref, a_ref, b_ref, w2_ref, da_ref, db_ref):
        ds = _dot_nt(df_ref[...], w2_ref[...])
        av, bv = a_ref[...], b_ref[...]
        sg = jax.nn.sigmoid(av)
        da_ref[...] = (ds * bv * (sg * (1.0 + av * (1.0 - sg)))).astype(BF16)
        db_ref[...] = (ds * (av * sg)).astype(BF16)

    tile = pl.BlockSpec((tm, tf), lambda i, j: (i, j))
    out, moved = _call(
        body, (df, a, b, w2), name=name, grid=(t // tm, ff // tf),
        in_specs=[pl.BlockSpec((tm, d), lambda i, j: (i, 0)), tile, tile, pl.BlockSpec((tf, d), lambda i, j: (j, 0))],
        out_specs=[tile, tile], out_shape=[jax.ShapeDtypeStruct((t, ff), BF16)] * 2,
        semantics=("parallel", "parallel"), comm=comm)
    return out if comm is None else (out, moved)


def _ffn_dh(da, db, w1, w3, name, tn=512, comm=None):
    t, ff = da.shape
    d = w1.shape[0]
    tm, tk = min(FFN_ROWS, t), ff // 2

    def body(da_ref, db_ref, w1_ref, w3_ref, dh_ref):
        p = _dot_nt(da_ref[...], w1_ref[...]) + _dot_nt(db_ref[...], w3_ref[...])
        k = pl.program_id(2)

        @pl.when(k == 0)
        def _():
            dh_ref[...] = p

        @pl.when(k > 0)
        def _():
            dh_ref[...] += p

    act = pl.BlockSpec((tm, tk), lambda i, j, k: (i, k))
    wgt = pl.BlockSpec((tn, tk), lambda i, j, k: (j, k))
    out, moved = _call(
        body, (da, db, w1, w3), name=name, grid=(t // tm, d // tn, 2),
        in_specs=[act, act, wgt, wgt], out_specs=[pl.BlockSpec((tm, tn), lambda i, j, k: (i, j))],
        out_shape=[jax.ShapeDtypeStruct((t, d), F32)],
        semantics=("parallel", "parallel", "arbitrary"), comm=comm)
    return out[0] if comm is None else (out[0], moved)


def _loss_head(x_prev, f_prev, gate_prev, coef, target, name, tm=256):
    t, d = x_prev.shape
    steps = t // tm

    def body(xp_ref, fp_ref, gp_ref, tg_ref, dy_ref, df_ref, loss_ref, acc_ref):
        i = pl.program_id(0)
        e = xp_ref[...] + coef * gp_ref[...] * fp_ref[...] - tg_ref[...]
        dy = e * (1.0 / d)
        dy_ref[...] = dy
        df_ref[...] = (coef * gp_ref[...] * dy).astype(BF16)
        part = jnp.sum(e * e, axis=0, keepdims=True)

        @pl.when(i == 0)
        def _():
            acc_ref[...] = part

        @pl.when(i > 0)
        def _():
            acc_ref[...] += part

        @pl.when(i == steps - 1)
        def _():
            loss_ref[...] = jnp.sum(acc_ref[...], axis=1, keepdims=True) * (0.5 / d)

    row, vec = _row_spec(tm, d), _vec_spec(d)
    return _pc(
        body, name=name, grid=(steps,),
        in_specs=[row, row, vec, row], out_specs=[row, row, pl.BlockSpec((1, 1), lambda i: (0, 0))],
        out_shape=[jax.ShapeDtypeStruct((t, d), F32), jax.ShapeDtypeStruct((t, d), BF16),
                   jax.ShapeDtypeStruct((1, 1), F32)],
        scratch_shapes=[pltpu.VMEM((1, d), F32)],
        compiler_params=_params(("arbitrary",)),
    )(x_prev, f_prev, gate_prev, target)


def _split3(x):
    hi = x.astype(BF16)
    r1 = x - hi.astype(F32)
    mid = r1.astype(BF16)
    lo = (r1 - mid.astype(F32)).astype(BF16)
    return hi, mid, lo


def _select_matmul(a, onehot, name):
    m, n = a.shape[0], onehot.shape[1]

    def body(a_ref, oh_ref, o_ref):
        hi, mid, lo = _split3(a_ref[...])
        oh = oh_ref[...]
        o_ref[...] = (_dot(hi, oh) + _dot(mid, oh)) + _dot(lo, oh)

    return _pc(body, name=name, out_shape=jax.ShapeDtypeStruct((m, n), F32), compiler_params=_params())(a, onehot)


def _bucket_onehot():
    qi = np.arange(BLOCK)[:, None]
    kj = np.arange(2 * BLOCK)[None, :]
    dist = np.clip(qi + BLOCK - kj, 0, None)
    nf = np.maximum(dist, 1).astype(np.float32)
    large = 16 + (np.log(nf / np.float32(16)) / np.float32(math.log(128 / 16)) * np.float32(16)).astype(np.int32)
    bucket = np.where(dist < 16, dist, np.minimum(large, N_BUCKETS - 1)).reshape(-1)
    return (bucket[None, :] == np.arange(N_BUCKETS)[:, None]).astype(np.float32)


def _window_mask(n):
    row = lax.broadcasted_iota(jnp.int32, (GROUP * BLOCK, 2 * BLOCK), 0)
    qi = row & (BLOCK - 1)
    kj = lax.broadcasted_iota(jnp.int32, (GROUP * BLOCK, 2 * BLOCK), 1)
    return (kj > qi) & (kj <= qi + BLOCK) & ((kj >= BLOCK) | (n > 0))


def _kv_band(zc_ref, zp_ref, kvh):
    lo, hi = kvh * HEAD_DIM, (kvh + 1) * HEAD_DIM
    k_raw = jnp.concatenate([zp_ref[:, lo:hi], zc_ref[:, K_OFF + lo:K_OFF + hi]], axis=0)
    v_raw = jnp.concatenate([zp_ref[:, KV_WIDTH + lo:KV_WIDTH + hi], zc_ref[:, V_OFF + lo:V_OFF + hi]], axis=0)
    return k_raw, v_raw


def _stack_heads(ref, off):
    return jnp.concatenate([ref[:, off + g * HEAD_DIM:off + (g + 1) * HEAD_DIM] for g in range(GROUP)], axis=0)


def _unstack_heads(ref, off, stacked):
    for g in range(GROUP):
        ref[:, off + g * HEAD_DIM:off + (g + 1) * HEAD_DIM] = stacked[g * BLOCK:(g + 1) * BLOCK]


def _group_softmax(qn, kband, bias_ref, sink_ref, kvh, valid):
    bias = bias_ref[kvh * GROUP:(kvh + 1) * GROUP].reshape(GROUP * BLOCK, 2 * BLOCK)
    s = _dot_nt(qn, kband) * (HEAD_DIM ** -0.5) + bias
    s = jnp.where(valid, s, MASK_VALUE)
    sink = jnp.concatenate([jnp.full((BLOCK, 1), sink_ref[0, kvh * GROUP + g], F32) for g in range(GROUP)], axis=0)
    m = jnp.maximum(jnp.max(s, axis=-1, keepdims=True), sink)
    p = jnp.exp(s - m)
    esink = jnp.exp(sink - m)
    inv = 1.0 / (jnp.sum(p, axis=-1, keepdims=True) + esink)
    return p * inv, esink * inv, inv


def _mix_specs(nb, order):
    cur = pl.BlockSpec((BLOCK, IN_COLS), lambda n: (order(n), 0))
    prev = pl.BlockSpec((BLOCK, 2 * KV_WIDTH), lambda n: (jnp.maximum(order(n) - 1, 0), K_OFF // (2 * KV_WIDTH)))
    full = lambda shape: pl.BlockSpec(shape, lambda n: (0,) * len(shape))
    params = [full((A_HEADS, BLOCK, BLOCK)), full((BLOCK, A_HEADS)), full((A_HEADS, BLOCK)),
              full((1, HEAD_DIM)), full((1, HEAD_DIM)), pl.BlockSpec(memory_space=pltpu.SMEM),
              full((B_HEADS, BLOCK, 2 * BLOCK))]
    return cur, prev, params, full


def _mix_fwd(z, sw, sb_t, gv, gq, gk, sinks, bias, name, comm=None, relay_at=0.5):
    t = z.shape[0]
    nb = t // BLOCK

    def body(zc_ref, zp_ref, sw_ref, sbt_ref, gv_ref, gq_ref, gk_ref, sink_ref, bias_ref, y_ref):
        n = pl.program_id(0)
        ri = lax.broadcasted_iota(jnp.int32, (BLOCK, BLOCK), 0)
        ci = lax.broadcasted_iota(jnp.int32, (BLOCK, BLOCK), 1)
        tril = ri >= ci
        for h in range(A_HEADS):
            u = _gelu(zc_ref[:, h * BLOCK:(h + 1) * BLOCK])
            vv = _gelu(zc_ref[:, A_WIDTH + h * BLOCK:A_WIDTH + (h + 1) * BLOCK])
            vhat, _ = _rms(vv)
            vn = (vhat * gv_ref[h:h + 1, :]).astype(BF16)
            w = jnp.where(tril, sw_ref[h], 0.0).astype(BF16)
            mixed = _dot(w, vn) + sbt_ref[:, h:h + 1]
            y_ref[:, h * BLOCK:(h + 1) * BLOCK] = (u * mixed).astype(BF16)
        valid = _window_mask(n)
        for kvh in range(B_KV_HEADS):
            k_raw, v_raw = _kv_band(zc_ref, zp_ref, kvh)
            khat, _ = _rms(k_raw)
            kband = (khat * gk_ref[...]).astype(BF16)
            vband = v_raw.astype(BF16)
            qhat, _ = _rms(_stack_heads(zc_ref, Q_OFF + kvh * GROUP * HEAD_DIM))
            qn = (qhat * gq_ref[...]).astype(BF16)
            w, _, _ = _group_softmax(qn, kband, bias_ref, sink_ref, kvh, valid)
            o = _dot(w.astype(BF16), vband)
            _unstack_heads(y_ref, A_WIDTH + kvh * GROUP * HEAD_DIM, o.astype(BF16))

    cur, prev, params, _ = _mix_specs(nb, lambda n: n)
    out, moved = _call(
        body, (z, z, sw, sb_t, gv, gq, gk, sinks, bias), name=name, grid=(nb,), in_specs=[cur, prev] + params,
        out_specs=[pl.BlockSpec((BLOCK, A_WIDTH + B_WIDTH), lambda n: (n, 0))],
        out_shape=[jax.ShapeDtypeStruct((t, A_WIDTH + B_WIDTH), BF16)],
        semantics=("parallel",), comm=comm, relay_at=relay_at)
    return out[0] if comm is None else (out[0], moved)


def _mix_bwd(z, dy, sw, sb_t, gv, gq, gk, sinks, bias, name, comm=None):
    t = z.shape[0]
    nb = t // BLOCK

    def body(zc_ref, zp_ref, dy_ref, sw_ref, sbt_ref, gv_ref, gq_ref, gk_ref, sink_ref, bias_ref,
             dz_ref, dsw_ref, dsb_ref, dgv_ref, dgq_ref, dgk_ref, dsink_ref, dbias_ref, carry_ref):
        step = pl.program_id(0)
        n = nb - 1 - step

        @pl.when(step == 0)
        def _():
            dsw_ref[...] = jnp.zeros_like(dsw_ref)
            dsb_ref[...] = jnp.zeros_like(dsb_ref)
            dgv_ref[...] = jnp.zeros_like(dgv_ref)
            dgq_ref[...] = jnp.zeros_like(dgq_ref)
            dgk_ref[...] = jnp.zeros_like(dgk_ref)
            dsink_ref[...] = jnp.zeros_like(dsink_ref)
            dbias_ref[...] = jnp.zeros_like(dbias_ref)
            carry_ref[...] = jnp.zeros_like(carry_ref)

        ri = lax.broadcasted_iota(jnp.int32, (BLOCK, BLOCK), 0)
        ci = lax.broadcasted_iota(jnp.int32, (BLOCK, BLOCK), 1)
        tril = ri >= ci
        dsb = jnp.zeros((BLOCK, BLOCK), F32)
        for h in range(A_HEADS):
            uo, vo = h * BLOCK, A_WIDTH + h * BLOCK
            u_raw, v_raw = zc_ref[:, uo:uo + BLOCK], zc_ref[:, vo:vo + BLOCK]
            (u, du_raw), (vv, dv_raw) = _gelu_and_grad(u_raw), _gelu_and_grad(v_raw)
            vhat, r = _rms(vv)
            vn = (vhat * gv_ref[h:h + 1, :]).astype(BF16)
            w = jnp.where(tril, sw_ref[h], 0.0).astype(BF16)
            mixed = _dot(w, vn) + sbt_ref[:, h:h + 1]
            dya = dy_ref[:, uo:uo + BLOCK]
            dmixed = dya * u
            dm16 = dmixed.astype(BF16)
            dsw_ref[h] += jnp.where(tril, _dot_nt(dm16, vn), 0.0)
            dsb = dsb + jnp.where(ci == h, jnp.sum(dmixed, axis=1, keepdims=True), 0.0)
            dvn = _dot_tn(w, dm16)
            dgv_ref[h:h + 1, :] += jnp.sum(dvn * vhat, axis=0, keepdims=True)
            dvv = _rms_bwd(dvn * gv_ref[h:h + 1, :], vhat, r)
            dz_ref[:, uo:uo + BLOCK] = (dya * mixed * du_raw).astype(BF16)
            dz_ref[:, vo:vo + BLOCK] = (dvv * dv_raw).astype(BF16)
        dsb_ref[...] += dsb

        valid = _window_mask(n)
        lane = lax.broadcasted_iota(jnp.int32, (1, BLOCK), 1)
        dsink = jnp.zeros((1, BLOCK), F32)
        dgq = jnp.zeros((1, HEAD_DIM), F32)
        for kvh in range(B_KV_HEADS):
            ko, vo = K_OFF + kvh * HEAD_DIM, V_OFF + kvh * HEAD_DIM
            k_raw, v_raw = _kv_band(zc_ref, zp_ref, kvh)
            khat, kr = _rms(k_raw)
            kband = (khat * gk_ref[...]).astype(BF16)
            vband = v_raw.astype(BF16)
            qo = Q_OFF + kvh * GROUP * HEAD_DIM
            qhat, qr = _rms(_stack_heads(zc_ref, qo))
            qn = (qhat * gq_ref[...]).astype(BF16)
            w, wsink, _ = _group_softmax(qn, kband, bias_ref, sink_ref, kvh, valid)
            do = _stack_heads(dy_ref, A_WIDTH + kvh * GROUP * HEAD_DIM).astype(BF16)
            dvb = _dot_tn(w.astype(BF16), do)
            dw = _dot_nt(do, vband)
            rowdot = jnp.sum(w * dw, axis=-1, keepdims=True)
            ds = w * (dw - rowdot)
            dsink_rows = -wsink * rowdot
            for g in range(GROUP):
                head_sum = jnp.sum(dsink_rows[g * BLOCK:(g + 1) * BLOCK], axis=0, keepdims=True)
                dsink = dsink + jnp.where(lane == kvh * GROUP + g, head_sum, 0.0)
            dbias_ref[kvh * GROUP:(kvh + 1) * GROUP] += ds.reshape(GROUP, BLOCK, 2 * BLOCK)
            ds16 = (ds * (HEAD_DIM ** -0.5)).astype(BF16)
            dqn = _dot(ds16, kband)
            dkn = _dot_tn(ds16, qn)
            dgq = dgq + jnp.sum(dqn * qhat, axis=0, keepdims=True)
            _unstack_heads(dz_ref, qo, _rms_bwd(dqn * gq_ref[...], qhat, qr).astype(BF16))
            dgk_ref[...] += jnp.sum(dkn * khat, axis=0, keepdims=True)
            dk = _rms_bwd(dkn * gk_ref[...], khat, kr)
            co = kvh * HEAD_DIM
            dz_ref[:, ko:ko + HEAD_DIM] = (dk[BLOCK:] + carry_ref[:, co:co + HEAD_DIM]).astype(BF16)
            dz_ref[:, vo:vo + HEAD_DIM] = (dvb[BLOCK:] + carry_ref[:, KV_WIDTH + co:KV_WIDTH + co + HEAD_DIM]).astype(BF16)
            carry_ref[:, co:co + HEAD_DIM] = dk[:BLOCK]
            carry_ref[:, KV_WIDTH + co:KV_WIDTH + co + HEAD_DIM] = dvb[:BLOCK]
        dgq_ref[...] += dgq
        dsink_ref[...] += dsink

    order = lambda n: nb - 1 - n
    cur, prev, params, full = _mix_specs(nb, order)
    dy_spec = pl.BlockSpec((BLOCK, A_WIDTH + B_WIDTH), lambda n: (order(n), 0))
    out, moved = _call(
        body, (z, z, dy, sw, sb_t, gv, gq, gk, sinks, bias), name=name, grid=(nb,),
        in_specs=[cur, prev, dy_spec] + params,
        out_specs=[cur, full((A_HEADS, BLOCK, BLOCK)), full((BLOCK, BLOCK)), full((A_HEADS, BLOCK)),
                   full((1, HEAD_DIM)), full((1, HEAD_DIM)), full((1, BLOCK)), full((B_HEADS, BLOCK, 2 * BLOCK))],
        out_shape=[jax.ShapeDtypeStruct((t, IN_COLS), BF16), jax.ShapeDtypeStruct((A_HEADS, BLOCK, BLOCK), F32),
                   jax.ShapeDtypeStruct((BLOCK, BLOCK), F32), jax.ShapeDtypeStruct((A_HEADS, BLOCK), F32),
                   jax.ShapeDtypeStruct((1, HEAD_DIM), F32), jax.ShapeDtypeStruct((1, HEAD_DIM), F32),
                   jax.ShapeDtypeStruct((1, BLOCK), F32), jax.ShapeDtypeStruct((B_HEADS, BLOCK, 2 * BLOCK), F32)],
        scratch_shapes=[pltpu.VMEM((BLOCK, 2 * KV_WIDTH), F32)],
        semantics=("arbitrary",), comm=comm)
    return out if comm is None else (out, moved)


def _ada_fwd(c_all, w_ada, b_cols, name, tn=768):
    nb, d = c_all.shape
    cols = w_ada.shape[1]

    def body(c_ref, w_ref, b_ref, act_ref, mod_ref):
        cv = c_ref[...]
        act = cv * jax.nn.sigmoid(cv)
        act_ref[...] = act
        mod_ref[...] = _dot(act.astype(BF16), w_ref[...].astype(BF16)) + b_ref[...]

    return _pc(
        body, name=name, grid=(cols // tn,),
        in_specs=[pl.BlockSpec((nb, d), lambda j: (0, 0)), pl.BlockSpec((d, tn), lambda j: (0, j)),
                  pl.BlockSpec((1, tn), lambda j: (0, j))],
        out_specs=[pl.BlockSpec((nb, d), lambda j: (0, 0)), pl.BlockSpec((nb, tn), lambda j: (0, j))],
        out_shape=[jax.ShapeDtypeStruct((nb, d), F32), jax.ShapeDtypeStruct((nb, cols), F32)],
        compiler_params=_params(("arbitrary",)),
    )(c_all, w_ada, b_cols)


def _adamw(w, g, m, v):
    m = ADAM_B1 * m + (1.0 - ADAM_B1) * g
    v = ADAM_B2 * v + (1.0 - ADAM_B2) * (g * g)
    m_hat = m / (1.0 - ADAM_B1 ** ADAM_STEP)
    v_hat = v / (1.0 - ADAM_B2 ** ADAM_STEP)
    delta = -ADAM_LR * (m_hat / (jnp.sqrt(v_hat) + ADAM_EPS) + ADAM_WD * w)
    return delta, m, v


def _adam_outer(act_t, dmod, w, m, v, name, tr=256):
    rows, cols = w.shape
    nb = dmod.shape[0]

    def body(act_ref, dm_ref, w_ref, m_ref, v_ref, g_ref, d_ref, nm_ref, nv_ref):
        act = act_ref[...].astype(BF16).astype(F32)
        dm = dm_ref[...].astype(BF16).astype(F32)
        g = act[:, 0:1] * dm[0:1, :]
        for b in range(1, nb):
            g = g + act[:, b:b + 1] * dm[b:b + 1, :]
        g_ref[...] = g
        d_ref[...], nm_ref[...], nv_ref[...] = _adamw(w_ref[...], g, m_ref[...], v_ref[...])

    tile = pl.BlockSpec((tr, cols), lambda i: (i, 0))
    return _pc(
        body, name=name, grid=(rows // tr,),
        in_specs=[pl.BlockSpec((tr, nb), lambda i: (i, 0)), pl.BlockSpec((nb, cols), lambda i: (0, 0)), tile, tile, tile],
        out_specs=[tile] * 4, out_shape=[jax.ShapeDtypeStruct((rows, cols), F32)] * 4,
        compiler_params=_params(("parallel",)),
    )(act_t, dmod, w, m, v)


def _adam_reduce(parts, w, m, v, name, tr):
    rows, cols = w.shape

    def body(p_ref, w_ref, m_ref, v_ref, g_ref, d_ref, nm_ref, nv_ref):
        g = p_ref[0].astype(F32)
        for i in range(1, N_DEV):
            g = g + p_ref[i].astype(F32)
        g_ref[...] = g
        d_ref[...], nm_ref[...], nv_ref[...] = _adamw(w_ref[...], g, m_ref[...], v_ref[...])

    tile = pl.BlockSpec((tr, cols), lambda i: (i, 0))
    return _pc(
        body, name=name, grid=(rows // tr,),
        in_specs=[pl.BlockSpec((N_DEV, tr, cols), lambda i: (0, i, 0)), tile, tile, tile],
        out_specs=[tile] * 4, out_shape=[jax.ShapeDtypeStruct((rows, cols), F32)] * 4,
        compiler_params=_params(("parallel",)),
    )(parts, w, m, v)


def _ungather_cols(g):
    n, r, c = g.shape
    return jnp.transpose(g, (1, 0, 2)).reshape(r, n * c)


def _scatter_cols(w):
    r, c = w.shape
    return jnp.transpose(w.reshape(r, N_DEV, c // N_DEV), (1, 0, 2)).astype(BF16)


def _pad_rows(v):
    flat = v.reshape(-1)
    pad = (-flat.shape[0]) % (8 * BLOCK)
    return jnp.pad(flat, (0, pad)).reshape(-1, BLOCK)


def kernel(x, c, w_ada, b_ada, g_ffn1, w1_ffn1, w3_ffn1, w2_ffn1, g_mix, w_in, spatial_w, spatial_b, g_v, g_q, g_k, sinks, rel_bias, w_out, g_ffn2, w1_ffn2, w3_ffn2, w2_ffn2, loss_target, m_w_ada, m_b_ada, m_g_ffn1, m_w1_ffn1, m_w3_ffn1, m_w2_ffn1, m_g_mix, m_w_in, m_spatial_w, m_spatial_b, m_g_v, m_g_q, m_g_k, m_sinks, m_rel_bias, m_w_out, m_g_ffn2, m_w1_ffn2, m_w3_ffn2, m_w2_ffn2, v_w_ada, v_b_ada, v_g_ffn1, v_w1_ffn1, v_w3_ffn1, v_w2_ffn1, v_g_mix, v_w_in, v_spatial_w, v_spatial_b, v_g_v, v_g_q, v_g_k, v_sinks, v_rel_bias, v_w_out, v_g_ffn2, v_w1_ffn2, v_w3_ffn2, v_w2_ffn2):
    me = _linear(_position())
    d = D_MODEL
    x0 = x[0]
    target = loss_target[0]
    t = x0.shape[0]

    big = dict(w1a=w1_ffn1[0], w3a=w3_ffn1[0], w2a=w2_ffn1[0], w_in=w_in[0], w_out=w_out[0],
               w1b=w1_ffn2[0], w3b=w3_ffn2[0], w2b=w2_ffn2[0])
    col_sharded = ("w1a", "w3a", "w_in", "w1b", "w3b")
    full = {}

    def gather(keys):
        return _GatherTwoLevel([big[k].astype(BF16) for k in keys])

    def unpack(keys, gathered):
        for k, gth in zip(keys, gathered):
            full[k] = _ungather_cols(gth) if k in col_sharded else gth.reshape(-1, gth.shape[-1])

    def share(k):
        gk_ = grads[k]
        if k in col_sharded:
            return _scatter_cols(gk_)
        return gk_.reshape(N_DEV, gk_.shape[0] // N_DEV, gk_.shape[1]).astype(BF16)

    unpack(("w1a", "w3a"), _run_comm(gather(("w1a", "w3a")), "gather_ffn1"))

    c_all = _run_comm(_Direct([c], True), "gather_c")[0].reshape(N_DEV, d)
    b_cols = lax.dynamic_slice(b_ada, (0, me * MOD_COLS), (1, MOD_COLS))
    act_all, mod_cols = _ada_fwd(c_all, w_ada[0], b_cols, "ada_fwd")
    mod = _run_comm(_Direct([mod_cols.reshape(N_DEV, 1, MOD_COLS)], False), "scatter_mod")[0]
    mod = mod.reshape(N_MOD, 1, d)
    sh1, sc1, gt1, sh2, sc2, gt2, sh3, sc3, gt3 = [mod[i] for i in range(N_MOD)]

    tm = min(FFN_ROWS, t)
    h1 = _norm_fwd(x0, None, None, 0.0, g_ffn1, sc1, sh1, "norm1_fwd")
    (a1, b1, s1), gathered = _ffn_up(h1, full["w1a"], full["w3a"], "ffn1_up",
                                     comm=gather(("w2a", "w_in", "w_out")), relay_at=0.6)
    unpack(("w2a", "w_in", "w_out"), gathered)
    f1, gathered = _matmul(s1, full["w2a"], "nn", tm, 512, D_FF, F32, "ffn1_down", comm=gather(("w1b",)))
    unpack(("w1b",), gathered)
    x1, h2 = _norm_fwd(x0, f1, gt1, 0.5, g_mix, sc2, sh2, "norm2_fwd")
    z = _matmul(h2, full["w_in"], "nn", 512, IN_COLS // 2, d, F32, "mix_in_fwd")
    onehot = _bucket_onehot()
    bias = _select_matmul(rel_bias.T, jnp.asarray(onehot, BF16), "bias_table").reshape(B_HEADS, BLOCK, 2 * BLOCK)
    sb_t = spatial_b[0].T
    mix_params = (spatial_w[0], sb_t, g_v[0], g_q, g_k, sinks, bias)
    ycat, gathered = _mix_fwd(z, *mix_params, "mix_core_fwd", comm=gather(("w3b",)), relay_at=0.6)
    unpack(("w3b",), gathered)
    y = _matmul(ycat, full["w_out"], "nn", 512, d, d, F32, "mix_out_fwd")
    x2, h3 = _norm_fwd(x1, y, gt2, 1.0, g_ffn2, sc3, sh3, "norm3_fwd")
    (a2, b2, s2), gathered = _ffn_up(h3, full["w1b"], full["w3b"], "ffn2_up", comm=gather(("w2b",)), relay_at=0.6)
    unpack(("w2b",), gathered)
    f2 = _matmul(s2, full["w2b"], "nn", tm, 512, D_FF, F32, "ffn2_down")
    dx3, df2, loss_part = _loss_head(x2, f2, gt3, 0.5, target, "loss_head")
    loss = lax.psum(loss_part[0, 0], ("x", "y", "c"))

    grads, parts = {}, {}
    da, db = _ffn_dact(df2, a2, b2, full["w2b"], "ffn2_dact")
    dh3 = _ffn_dh(da, db, full["w1b"], full["w3b"], "ffn2_dh")
    tk = min(DW_TOKENS, t)
    grads["w1b"] = _matmul(h3, da, "tn", 1024, 512, tk, F32, "ffn2_dw1")
    grads["w3b"] = _matmul(h3, db, "tn", 1024, 512, tk, F32, "ffn2_dw3")
    grads["w2b"] = _matmul(s2, df2, "tn", 512, 1024, tk, F32, "ffn2_dw2")
    dx2, dyg, sums3 = _norm_bwd(dh3, x2, dx3, f2, 0.5, g_ffn2, sc3, gt2, 1.0, "norm3_bwd")
    dycat = _matmul(dyg, full["w_out"], "nt", 512, d, d, F32, "mix_out_bwd")
    mix_grads, moved = _mix_bwd(z, dycat, *mix_params, "mix_core_bwd",
                                comm=_Direct([share("w1b"), share("w3b")], False))
    parts["w1b"], parts["w3b"] = moved
    dz, d_sw, d_sb, d_gv, d_gq, d_gk, d_sink, d_bias = mix_grads
    d_rel = _select_matmul(d_bias.reshape(B_HEADS, BLOCK * 2 * BLOCK), jnp.asarray(onehot.T, BF16), "bias_table_bwd")
    dh2 = _matmul(dz, full["w_in"], "nt", 512, d, IN_COLS // 2, F32, "mix_in_bwd")
    dx1, df1, sums2 = _norm_bwd(dh2, x1, dx2, y, 1.0, g_mix, sc2, gt1, 0.5, "norm2_bwd")
    (da, db), moved = _ffn_dact(df1, a1, b1, full["w2a"], "ffn1_dact", comm=_Direct([share("w2b")], False))
    parts["w2b"] = moved[0]
    dh1 = _ffn_dh(da, db, full["w1a"], full["w3a"], "ffn1_dh")
    grads["w1a"] = _matmul(h1, da, "tn", 1024, 512, tk, F32, "ffn1_dw1")
    grads["w3a"], moved = _matmul(h1, db, "tn", 1024, 512, tk, F32, "ffn1_dw3", comm=_Direct([share("w1a")], False))
    parts["w1a"] = moved[0]
    grads["w2a"], moved = _matmul(s1, df1, "tn", 512, 1024, tk, F32, "ffn1_dw2", comm=_Direct([share("w3a")], False))
    parts["w3a"] = moved[0]
    grads["w_in"], moved = _matmul(h2, dz, "tn", 1024, 256, tk, F32, "mix_in_dw", comm=_Direct([share("w2a")], False))
    parts["w2a"] = moved[0]
    grads["w_out"], moved = _matmul(ycat, dyg, "tn", 1024, 512, tk, F32, "mix_out_dw",
                                    comm=_Direct([share("w_in")], False))
    parts["w_in"] = moved[0]
    (dx0, sums1), moved = _norm_bwd(dh1, x0, dx1, f1, 0.5, g_ffn1, sc1, None, 0.0, "norm1_bwd",
                                    comm=_Direct([share("w_out")], False))
    parts["w_out"] = moved[0]

    moments = dict(w1a=(m_w1_ffn1, v_w1_ffn1), w3a=(m_w3_ffn1, v_w3_ffn1), w2a=(m_w2_ffn1, v_w2_ffn1),
                   w_in=(m_w_in, v_w_in), w_out=(m_w_out, v_w_out),
                   w1b=(m_w1_ffn2, v_w1_ffn2), w3b=(m_w3_ffn2, v_w3_ffn2), w2b=(m_w2_ffn2, v_w2_ffn2))
    upd = {}
    for k in big:
        rows = big[k].shape[0]
        tr = 256 if rows % 256 == 0 else rows // 4
        upd[k] = [o[None] for o in _adam_reduce(parts[k], big[k], moments[k][0][0], moments[k][1][0], "adam_" + k, tr)]

    dmod = jnp.concatenate([sums1[1:2], sums1[0:1], sums1[3:4], sums2[1:2], sums2[0:1], sums2[3:4],
                            sums3[1:2], sums3[0:1], sums3[3:4]], axis=1)
    small = [("b_ada", dmod, b_ada, m_b_ada, v_b_ada),
             ("g_ffn1", sums1[2:3], g_ffn1, m_g_ffn1, v_g_ffn1),
             ("g_mix", sums2[2:3], g_mix, m_g_mix, v_g_mix),
             ("spatial_w", d_sw[None], spatial_w, m_spatial_w, v_spatial_w),
             ("spatial_b", d_sb[:, :A_HEADS].T[None], spatial_b, m_spatial_b, v_spatial_b),
             ("g_v", d_gv[None], g_v, m_g_v, v_g_v),
             ("g_q", d_gq, g_q, m_g_q, v_g_q),
             ("g_k", d_gk, g_k, m_g_k, v_g_k),
             ("sinks", d_sink[:, :B_HEADS], sinks, m_sinks, v_sinks),
             ("rel_bias", d_rel.T, rel_bias, m_rel_bias, v_rel_bias),
             ("g_ffn2", sums3[2:3], g_ffn2, m_g_ffn2, v_g_ffn2)]
    packed = [jnp.concatenate([_pad_rows(item[i]) for item in small], axis=0) for i in range(1, 5)]
    g_all = _run_comm(_Direct([packed[0]], True), "gather_small")[0]
    small_out = _adam_reduce(g_all, packed[1], packed[2], packed[3], "adam_small", packed[0].shape[0])
    off = 0
    for name, g_, w_, _, _ in small:
        n_rows = _pad_rows(w_).shape[0]
        upd[name] = [o[off:off + n_rows].reshape(-1)[:w_.size].reshape(w_.shape) for o in small_out]
        off += n_rows

    dmod_rows = g_all[:, :N_MOD * d // BLOCK, :].reshape(N_DEV, N_MOD * d)
    dmod_cols = lax.dynamic_slice(dmod_rows, (0, me * MOD_COLS), (N_DEV, MOD_COLS))
    upd["w_ada"] = [o[None] for o in _adam_outer(act_all.T, dmod_cols, w_ada[0], m_w_ada[0], v_w_ada[0], "adam_w_ada")]

    order = [("w_ada", "w_ada"), ("b_ada", "b_ada"), ("g_ffn1", "g_ffn1"), ("w1_ffn1", "w1a"), ("w3_ffn1", "w3a"),
             ("w2_ffn1", "w2a"), ("g_mix", "g_mix"), ("w_in", "w_in"), ("spatial_w", "spatial_w"),
             ("spatial_b", "spatial_b"), ("g_v", "g_v"), ("g_q", "g_q"), ("g_k", "g_k"), ("sinks", "sinks"),
             ("rel_bias", "rel_bias"), ("w_out", "w_out"), ("g_ffn2", "g_ffn2"), ("w1_ffn2", "w1b"),
             ("w3_ffn2", "w3b"), ("w2_ffn2", "w2b")]
    outs = [loss, dx0[None]]
    for i in range(4):
        outs += [upd[key][i] for _, key in order]
    return tuple(outs)
```

```python
import functools
import math

import numpy as np
import jax
import jax.numpy as jnp
from jax import lax
from jax.experimental import pallas as pl
from jax.experimental.pallas import tpu as pltpu

F32 = jnp.float32
BF16 = jnp.bfloat16

D_MODEL = 2048
D_FF = 5632
BLOCK = 128
A_HEADS = 8
A_WIDTH = 1024
B_HEADS = 16
B_KV_HEADS = 2
GROUP = B_HEADS // B_KV_HEADS
HEAD_DIM = 64
B_WIDTH = 1024
KV_WIDTH = 128
IN_COLS = 3328
Q_OFF = 2 * A_WIDTH
K_OFF = Q_OFF + B_WIDTH
V_OFF = K_OFF + KV_WIDTH
N_BUCKETS = 32
N_MOD = 9
EPS = 1e-6
N_DEV = 8
MOD_COLS = N_MOD * D_MODEL // N_DEV

ADAM_LR = 0.001
ADAM_B1 = 0.9
ADAM_B2 = 0.999
ADAM_EPS = 1e-08
ADAM_WD = 0.01
ADAM_STEP = 10

FFN_ROWS = 1024
DW_TOKENS = 4096
MASK_VALUE = -1e30
VMEM_LIMIT = 56 * 1024 * 1024
MESH_ID = pl.DeviceIdType.MESH
ANY = pl.BlockSpec(memory_space=pl.ANY)

_SQRT_HALF = 0.7071067811865476
_INV_SQRT_2PI = 0.3989422804014327


def _pc(body, **kw):
    return pl.pallas_call(body, **kw)


def _params(sem=None):
    return pltpu.CompilerParams(dimension_semantics=sem, vmem_limit_bytes=VMEM_LIMIT)


def _dot(a, b):
    return lax.dot_general(a, b, (((1,), (0,)), ((), ())), preferred_element_type=F32)


def _dot_nt(a, b):
    return lax.dot_general(a, b, (((1,), (1,)), ((), ())), preferred_element_type=F32)


def _dot_tn(a, b):
    return lax.dot_general(a, b, (((0,), (0,)), ((), ())), preferred_element_type=F32)


def _gelu(x):
    return 0.5 * x * (1.0 + lax.erf(x * _SQRT_HALF))


def _gelu_and_grad(x):
    cdf = 0.5 * (1.0 + lax.erf(x * _SQRT_HALF))
    return x * cdf, cdf + x * jnp.exp(-0.5 * x * x) * _INV_SQRT_2PI


def _rms(x):
    r = lax.rsqrt(jnp.mean(x * x, axis=-1, keepdims=True) + EPS)
    return x * r, r


def _rms_bwd(dy, y, r):
    return r * (dy - y * jnp.mean(dy * y, axis=-1, keepdims=True))


def _row_spec(tm, cols):
    return pl.BlockSpec((tm, cols), lambda i: (i, 0))


def _vec_spec(cols):
    return pl.BlockSpec((1, cols), lambda i: (0, 0))


def _position():
    x, y, c = lax.axis_index("x"), lax.axis_index("y"), lax.axis_index("c")
    return x, y, c


def _linear(p):
    return 4 * p[0] + 2 * p[1] + p[2]


class _Comm:
    PEER_COPIES = N_DEV - 1

    def __init__(self, arrs):
        self.arrs = list(arrs)
        n = len(self.arrs)
        self.scratch_shapes = [pltpu.SemaphoreType.DMA((self.PEER_COPIES * n,)),
                               pltpu.SemaphoreType.DMA((self.PEER_COPIES * n,)),
                               pltpu.SemaphoreType.DMA((n,))]

    def bind(self, srcs, dsts, sems):
        self.srcs, self.dsts = srcs, dsts
        self.send_sems, self.recv_sems, self.local_sems = sems
        x, y, c = _position()
        self.me, self.sibling, self.core = (x, y, c), (x, y, 1 - c), c
        self.chips = [(1 - x, y), (x, 1 - y), (1 - x, 1 - y)]
        self.peers = [(1 - x if k & 4 else x, 1 - y if k & 2 else y, 1 - c if k & 1 else c)
                      for k in range(1, N_DEV)]

    def relay(self):
        pass


class _GatherTwoLevel(_Comm):
    def __init__(self, arrs):
        super().__init__(arrs)
        self.out_shape = [jax.ShapeDtypeStruct((N_DEV,) + a.shape, a.dtype) for a in self.arrs]

    def _copy(self, a, k, block, to, from_input=False):
        rows = self.dsts[a].at[_linear(block)]
        return pltpu.make_async_remote_copy(
            src_ref=self.srcs[a] if from_input else rows, dst_ref=rows,
            send_sem=self.send_sems.at[self.PEER_COPIES * a + k], recv_sem=self.recv_sems.at[self.PEER_COPIES * a + k],
            device_id=to, device_id_type=MESH_ID)

    def _local(self, a):
        return pltpu.make_async_copy(self.srcs[a], self.dsts[a].at[_linear(self.me)], self.local_sems.at[a])

    def _first(self, a):
        return [self._copy(a, 0, self.me, self.sibling, True)] + [
            self._copy(a, 1 + j, self.me, (*chip, self.core), True) for j, chip in enumerate(self.chips)]

    def _passed(self, a):
        return [self._copy(a, 4 + j, (*chip, self.core), self.sibling) for j, chip in enumerate(self.chips)]

    def start(self):
        for a in range(len(self.arrs)):
            self._local(a).start()
            for cp in self._first(a):
                cp.start()

    def relay(self):
        for a in range(len(self.arrs)):
            for j, chip in enumerate(self.chips):
                self._copy(a, 1 + j, (*chip, self.core), self.me).wait_recv()
                self._passed(a)[j].start()

    def finish(self):
        for a in range(len(self.arrs)):
            self._copy(a, 0, self.sibling, self.me).wait_recv()
            for j, chip in enumerate(self.chips):
                self._copy(a, 4 + j, (*chip, 1 - self.core), self.me).wait_recv()
        for a in range(len(self.arrs)):
            for cp in self._first(a) + self._passed(a):
                cp.wait_send()
            self._local(a).wait()


class _Direct(_Comm):
    def __init__(self, arrs, broadcast):
        super().__init__(arrs)
        self.broadcast = broadcast
        self.out_shape = [jax.ShapeDtypeStruct(((N_DEV,) + a.shape) if broadcast else a.shape, a.dtype)
                          for a in self.arrs]

    def _outgoing(self, a, to):
        return self.srcs[a] if self.broadcast else self.srcs[a].at[_linear(to)]

    def _copy(self, a, k, sender, to):
        return pltpu.make_async_remote_copy(
            src_ref=self._outgoing(a, to), dst_ref=self.dsts[a].at[_linear(sender)],
            send_sem=self.send_sems.at[self.PEER_COPIES * a + k], recv_sem=self.recv_sems.at[self.PEER_COPIES * a + k],
            device_id=to, device_id_type=MESH_ID)

    def _local(self, a):
        return pltpu.make_async_copy(self._outgoing(a, self.me), self.dsts[a].at[_linear(self.me)],
                                     self.local_sems.at[a])

    def start(self):
        for a in range(len(self.arrs)):
            self._local(a).start()
            for k, peer in enumerate(self.peers):
                self._copy(a, k, self.me, peer).start()

    def finish(self):
        for a in range(len(self.arrs)):
            for k, peer in enumerate(self.peers):
                self._copy(a, k, peer, self.me).wait_recv()
        for a in range(len(self.arrs)):
            for k, peer in enumerate(self.peers):
                self._copy(a, k, self.me, peer).wait_send()
            self._local(a).wait()


def _run_comm(comm, name):
    n = len(comm.arrs)

    def body(*refs):
        comm.bind(refs[:n], refs[n:2 * n], refs[2 * n:])
        comm.start()
        comm.relay()
        comm.finish()

    return _pc(body, name=name, out_shape=comm.out_shape, in_specs=[ANY] * n, out_specs=[ANY] * n,
               scratch_shapes=comm.scratch_shapes)(*comm.arrs)


def _call(body, inputs, *, name, grid, in_specs, out_specs, out_shape, scratch_shapes=(), semantics,
          comm=None, relay_at=0.5):
    if comm is None:
        out = _pc(body, name=name, grid=grid, in_specs=in_specs, out_specs=out_specs, out_shape=out_shape,
                  scratch_shapes=list(scratch_shapes), compiler_params=_params(semantics))(*inputs)
        return list(out), []
    n_in, n_out, n_sc, k = len(in_specs), len(out_specs), len(scratch_shapes), len(comm.arrs)
    steps = math.prod(grid)
    relay_step = min(steps - 1, int(steps * relay_at))

    def carrier(*refs):
        ins, rest = refs[:n_in], refs[n_in:]
        csrc, rest = rest[:k], rest[k:]
        outs, rest = rest[:n_out], rest[n_out:]
        cdst, rest = rest[:k], rest[k:]
        scratch, csems = rest[:n_sc], rest[n_sc:]
        step = 0
        for axis, size in enumerate(grid):
            step = step * size + pl.program_id(axis)
        comm.bind(csrc, cdst, csems)
        pl.when(step == 0)(comm.start)
        pl.when(step == relay_step)(comm.relay)
        body(*ins, *outs, *scratch)
        pl.when(step == steps - 1)(comm.finish)

    out = _pc(carrier, name=name, grid=grid, in_specs=list(in_specs) + [ANY] * k,
              out_specs=list(out_specs) + [ANY] * k, out_shape=list(out_shape) + comm.out_shape,
              scratch_shapes=list(scratch_shapes) + comm.scratch_shapes,
              compiler_params=_params(("arbitrary",) * len(grid)))(*inputs, *comm.arrs)
    return list(out[:n_out]), list(out[n_out:])


def _matmul(a, b, mode, tm, tn, tk, out_dtype, name, comm=None, relay_at=0.5):
    if mode == "nn":
        (m, kk), nn = a.shape, b.shape[1]
        a_spec = pl.BlockSpec((tm, tk), lambda i, j, k: (i, k))
        b_spec = pl.BlockSpec((tk, tn), lambda i, j, k: (k, j))
        dot = _dot
    elif mode == "nt":
        (m, kk), nn = a.shape, b.shape[0]
        a_spec = pl.BlockSpec((tm, tk), lambda i, j, k: (i, k))
        b_spec = pl.BlockSpec((tn, tk), lambda i, j, k: (j, k))
        dot = _dot_nt
    else:
        (kk, m), nn = a.shape, b.shape[1]
        a_spec = pl.BlockSpec((tk, tm), lambda i, j, k: (k, i))
        b_spec = pl.BlockSpec((tk, tn), lambda i, j, k: (k, j))
        dot = _dot_tn
    assert m % tm == 0 and nn % tn == 0 and kk % tk == 0, (a.shape, b.shape, tm, tn, tk)
    nk = kk // tk
    narrow = nk > 1 and out_dtype != F32

    def body(a_ref, b_ref, o_ref, *acc):
        p = dot(a_ref[...], b_ref[...])
        if nk == 1:
            o_ref[...] = p.astype(o_ref.dtype)
            return
        k = pl.program_id(2)
        acc_ref = acc[0] if narrow else o_ref

        @pl.when(k == 0)
        def _():
            acc_ref[...] = p

        @pl.when((k > 0) & (k < nk - 1) if narrow else k > 0)
        def _():
            acc_ref[...] += p

        if narrow:
            @pl.when(k == nk - 1)
            def _():
                o_ref[...] = (acc_ref[...] + p).astype(o_ref.dtype)

    out, moved = _call(
        body, (a, b), name=name, grid=(m // tm, nn // tn, nk),
        in_specs=[a_spec, b_spec], out_specs=[pl.BlockSpec((tm, tn), lambda i, j, k: (i, j))],
        out_shape=[jax.ShapeDtypeStruct((m, nn), out_dtype)],
        scratch_shapes=[pltpu.VMEM((tm, tn), F32)] if narrow else [],
        semantics=("parallel", "parallel", "arbitrary"), comm=comm, relay_at=relay_at)
    return out[0] if comm is None else (out[0], moved)


def _norm_fwd(x_prev, f_prev, gate_prev, coef, g, sc, sh, name, tm=256):
    t, d = x_prev.shape
    residual = f_prev is not None

    def body(*refs):
        if residual:
            xp_ref, fp_ref, gp_ref, g_ref, sc_ref, sh_ref, x_ref, h_ref = refs
            x = xp_ref[...] + coef * gp_ref[...] * fp_ref[...]
            x_ref[...] = x
        else:
            xp_ref, g_ref, sc_ref, sh_ref, h_ref = refs
            x = xp_ref[...]
        y, _ = _rms(x)
        h_ref[...] = ((y * g_ref[...]) * (1.0 + sc_ref[...]) + sh_ref[...]).astype(BF16)

    row, vec = _row_spec(tm, d), _vec_spec(d)
    if residual:
        ins, in_specs = (x_prev, f_prev, gate_prev, g, sc, sh), [row, row, vec, vec, vec, vec]
        out_shape = [jax.ShapeDtypeStruct((t, d), F32), jax.ShapeDtypeStruct((t, d), BF16)]
        out_specs = [row, row]
    else:
        ins, in_specs = (x_prev, g, sc, sh), [row, vec, vec, vec]
        out_shape = [jax.ShapeDtypeStruct((t, d), BF16)]
        out_specs = [row]
    out = _pc(body, name=name, grid=(t // tm,), in_specs=in_specs, out_specs=out_specs, out_shape=out_shape,
              compiler_params=_params(("parallel",)))(*ins)
    return out if residual else out[0]


def _norm_bwd(dh, x, dxo, fo, coef_o, g, sc, gate_prev, coef_prev, name, tm=256, comm=None):
    t, d = x.shape
    with_prev = gate_prev is not None

    def body(*refs):
        if with_prev:
            dh_ref, x_ref, dxo_ref, fo_ref, g_ref, sc_ref, gp_ref, dx_ref, dfp_ref, sums_ref = refs
        else:
            dh_ref, x_ref, dxo_ref, fo_ref, g_ref, sc_ref, dx_ref, sums_ref = refs
        dh_v, dxo_v = dh_ref[...], dxo_ref[...]
        y, r = _rms(x_ref[...])
        n = y * g_ref[...]
        dn = dh_v * (1.0 + sc_ref[...])
        dx = dxo_v + _rms_bwd(dn * g_ref[...], y, r)
        dx_ref[...] = dx
        if with_prev:
            dfp_ref[...] = (coef_prev * gp_ref[...] * dx).astype(BF16)
        @pl.when(pl.program_id(0) == 0)
        def _():
            sums_ref[...] = jnp.zeros_like(sums_ref)

        sums_ref[0:1, :] += jnp.sum(dh_v * n, axis=0, keepdims=True)
        sums_ref[1:2, :] += jnp.sum(dh_v, axis=0, keepdims=True)
        sums_ref[2:3, :] += jnp.sum(dn * y, axis=0, keepdims=True)
        sums_ref[3:4, :] += jnp.sum(coef_o * dxo_v * fo_ref[...], axis=0, keepdims=True)

    row, vec = _row_spec(tm, d), _vec_spec(d)
    sums_spec = pl.BlockSpec((8, d), lambda i: (0, 0))
    ins, in_specs = [dh, x, dxo, fo, g, sc], [row, row, row, row, vec, vec]
    out_shape, out_specs = [jax.ShapeDtypeStruct((t, d), F32)], [row]
    if with_prev:
        ins.append(gate_prev)
        in_specs.append(vec)
        out_shape.append(jax.ShapeDtypeStruct((t, d), BF16))
        out_specs.append(row)
    out_shape.append(jax.ShapeDtypeStruct((8, d), F32))
    out_specs.append(sums_spec)
    out, moved = _call(body, ins, name=name, grid=(t // tm,), in_specs=in_specs, out_specs=out_specs,
                       out_shape=out_shape, semantics=("arbitrary",), comm=comm)
    return out if comm is None else (out, moved)


def _ffn_up(h, w1, w3, name, tf=512, comm=None, relay_at=0.5):
    t, d = h.shape
    ff = w1.shape[1]
    tm = min(FFN_ROWS, t)

    def body(h_ref, w1_ref, w3_ref, ga_ref, gb_ref, s_ref):
        hv = h_ref[...]
        a = _dot(hv, w1_ref[...])
        b = _dot(hv, w3_ref[...])
        sg = jax.nn.sigmoid(a)
        sil = a * sg
        ga_ref[...] = b * (sg * (1.0 + a * (1.0 - sg)))
        gb_ref[...] = sil
        s_ref[...] = (sil * b).astype(BF16)

    tile = pl.BlockSpec((tm, tf), lambda i, j: (i, j))
    cols = pl.BlockSpec((d, tf), lambda i, j: (0, j))
    out, moved = _call(
        body, (h, w1, w3), name=name, grid=(t // tm, ff // tf),
        in_specs=[pl.BlockSpec((tm, d), lambda i, j: (i, 0)), cols, cols], out_specs=[tile, tile, tile],
        out_shape=[jax.ShapeDtypeStruct((t, ff), F32), jax.ShapeDtypeStruct((t, ff), F32),
                   jax.ShapeDtypeStruct((t, ff), BF16)],
        semantics=("parallel", "parallel"), comm=comm, relay_at=relay_at)
    return out if comm is None else (out, moved)


def _ffn_dact(df, ga, gb, w2, name, tf=512, comm=None):
    t, d = df.shape
    ff = ga.shape[1]
    tm = min(FFN_ROWS, t)

    def body(df_ref, ga_ref, gb_ref, w2_ref, da_ref, db_ref):
        ds = _dot_nt(df_ref[...], w2_ref[...])
        da_ref[...] = (ds * ga_ref[...]).astype(BF16)
        db_ref[...] = (ds * gb_ref[...]).astype(BF16)

    tile = pl.BlockSpec((tm, tf), lambda i, j: (i, j))
    out, moved = _call(
        body, (df, ga, gb, w2), name=name, grid=(t // tm, ff // tf),
        in_specs=[pl.BlockSpec((tm, d), lambda i, j: (i, 0)), tile, tile, pl.BlockSpec((tf, d), lambda i, j: (j, 0))],
        out_specs=[tile, tile], out_shape=[jax.ShapeDtypeStruct((t, ff), BF16)] * 2,
        semantics=("parallel", "parallel"), comm=comm)
    return out if comm is None else (out, moved)


def _ffn_dh(da, db, w1, w3, name, tn=512, comm=None):
    t, ff = da.shape
    d = w1.shape[0]
    tm, tk = min(FFN_ROWS, t), ff // 2

    def body(da_ref, db_ref, w1_ref, w3_ref, dh_ref):
        p = _dot_nt(da_ref[...], w1_ref[...]) + _dot_nt(db_ref[...], w3_ref[...])
        k = pl.program_id(2)

        @pl.when(k == 0)
        def _():
            dh_ref[...] = p

        @pl.when(k > 0)
        def _():
            dh_ref[...] += p

    act = pl.BlockSpec((tm, tk), lambda i, j, k: (i, k))
    wgt = pl.BlockSpec((tn, tk), lambda i, j, k: (j, k))
    out, moved = _call(
        body, (da, db, w1, w3), name=name, grid=(t // tm, d // tn, 2),
        in_specs=[act, act, wgt, wgt], out_specs=[pl.BlockSpec((tm, tn), lambda i, j, k: (i, j))],
        out_shape=[jax.ShapeDtypeStruct((t, d), F32)],
        semantics=("parallel", "parallel", "arbitrary"), comm=comm)
    return out[0] if comm is None else (out[0], moved)


def _loss_head(x_prev, f_prev, gate_prev, coef, target, name, tm=256):
    t, d = x_prev.shape
    steps = t // tm

    def body(xp_ref, fp_ref, gp_ref, tg_ref, dy_ref, df_ref, loss_ref, acc_ref):
        i = pl.program_id(0)
        e = xp_ref[...] + coef * gp_ref[...] * fp_ref[...] - tg_ref[...]
        dy = e * (1.0 / d)
        dy_ref[...] = dy
        df_ref[...] = (coef * gp_ref[...] * dy).astype(BF16)
        part = jnp.sum(e * e, axis=0, keepdims=True)

        @pl.when(i == 0)
        def _():
            acc_ref[...] = part

        @pl.when(i > 0)
        def _():
            acc_ref[...] += part

        @pl.when(i == steps - 1)
        def _():
            loss_ref[...] = jnp.sum(acc_ref[...], axis=1, keepdims=True) * (0.5 / d)

    row, vec = _row_spec(tm, d), _vec_spec(d)
    return _pc(
        body, name=name, grid=(steps,),
        in_specs=[row, row, vec, row], out_specs=[row, row, pl.BlockSpec((1, 1), lambda i: (0, 0))],
        out_shape=[jax.ShapeDtypeStruct((t, d), F32), jax.ShapeDtypeStruct((t, d), BF16),
                   jax.ShapeDtypeStruct((1, 1), F32)],
        scratch_shapes=[pltpu.VMEM((1, d), F32)],
        compiler_params=_params(("arbitrary",)),
    )(x_prev, f_prev, gate_prev, target)


def _split3(x):
    hi = x.astype(BF16)
    r1 = x - hi.astype(F32)
    mid = r1.astype(BF16)
    lo = (r1 - mid.astype(F32)).astype(BF16)
    return hi, mid, lo


def _select_matmul(a, onehot, name):
    m, n = a.shape[0], onehot.shape[1]

    def body(a_ref, oh_ref, o_ref):
        hi, mid, lo = _split3(a_ref[...])
        oh = oh_ref[...]
        o_ref[...] = (_dot(hi, oh) + _dot(mid, oh)) + _dot(lo, oh)

    return _pc(body, name=name, out_shape=jax.ShapeDtypeStruct((m, n), F32), compiler_params=_params())(a, onehot)


def _bucket_onehot():
    qi = np.arange(BLOCK)[:, None]
    kj = np.arange(2 * BLOCK)[None, :]
    dist = np.clip(qi + BLOCK - kj, 0, None)
    nf = np.maximum(dist, 1).astype(np.float32)
    large = 16 + (np.log(nf / np.float32(16)) / np.float32(math.log(128 / 16)) * np.float32(16)).astype(np.int32)
    bucket = np.where(dist < 16, dist, np.minimum(large, N_BUCKETS - 1)).reshape(-1)
    return (bucket[None, :] == np.arange(N_BUCKETS)[:, None]).astype(np.float32)


def _window_mask(n):
    row = lax.broadcasted_iota(jnp.int32, (GROUP * BLOCK, 2 * BLOCK), 0)
    qi = row & (BLOCK - 1)
    kj = lax.broadcasted_iota(jnp.int32, (GROUP * BLOCK, 2 * BLOCK), 1)
    return (kj > qi) & (kj <= qi + BLOCK) & ((kj >= BLOCK) | (n > 0))


def _kv_band(zc_ref, zp_ref, kvh):
    lo, hi = kvh * HEAD_DIM, (kvh + 1) * HEAD_DIM
    k_raw = jnp.concatenate([zp_ref[:, lo:hi], zc_ref[:, K_OFF + lo:K_OFF + hi]], axis=0)
    v_raw = jnp.concatenate([zp_ref[:, KV_WIDTH + lo:KV_WIDTH + hi], zc_ref[:, V_OFF + lo:V_OFF + hi]], axis=0)
    return k_raw, v_raw


def _stack_heads(ref, off):
    return jnp.concatenate([ref[:, off + g * HEAD_DIM:off + (g + 1) * HEAD_DIM] for g in range(GROUP)], axis=0)


def _unstack_heads(ref, off, stacked):
    for g in range(GROUP):
        ref[:, off + g * HEAD_DIM:off + (g + 1) * HEAD_DIM] = stacked[g * BLOCK:(g + 1) * BLOCK]


def _group_softmax(qn, kband, bias_ref, sink_ref, kvh, valid):
    bias = bias_ref[kvh * GROUP:(kvh + 1) * GROUP].reshape(GROUP * BLOCK, 2 * BLOCK)
    s = _dot_nt(qn, kband) * (HEAD_DIM ** -0.5) + bias
    s = jnp.where(valid, s, MASK_VALUE)
    sink = jnp.concatenate([jnp.full((BLOCK, 1), sink_ref[0, kvh * GROUP + g], F32) for g in range(GROUP)], axis=0)
    m = jnp.maximum(jnp.max(s, axis=-1, keepdims=True), sink)
    p = jnp.exp(s - m)
    esink = jnp.exp(sink - m)
    inv = 1.0 / (jnp.sum(p, axis=-1, keepdims=True) + esink)
    return p * inv, esink * inv, inv


def _mix_specs(nb, order):
    cur = pl.BlockSpec((BLOCK, IN_COLS), lambda n: (order(n), 0))
    prev = pl.BlockSpec((BLOCK, 2 * KV_WIDTH), lambda n: (jnp.maximum(order(n) - 1, 0), K_OFF // (2 * KV_WIDTH)))
    full = lambda shape: pl.BlockSpec(shape, lambda n: (0,) * len(shape))
    params = [full((A_HEADS, BLOCK, BLOCK)), full((BLOCK, A_HEADS)), full((A_HEADS, BLOCK)),
              full((1, HEAD_DIM)), full((1, HEAD_DIM)), pl.BlockSpec(memory_space=pltpu.SMEM),
              full((B_HEADS, BLOCK, 2 * BLOCK))]
    return cur, prev, params, full


def _mix_fwd(z, sw, sb_t, gv, gq, gk, sinks, bias, name, comm=None, relay_at=0.5):
    t = z.shape[0]
    nb = t // BLOCK

    def body(zc_ref, zp_ref, sw_ref, sbt_ref, gv_ref, gq_ref, gk_ref, sink_ref, bias_ref, y_ref):
        n = pl.program_id(0)
        ri = lax.broadcasted_iota(jnp.int32, (BLOCK, BLOCK), 0)
        ci = lax.broadcasted_iota(jnp.int32, (BLOCK, BLOCK), 1)
        tril = ri >= ci
        for h in range(A_HEADS):
            u = _gelu(zc_ref[:, h * BLOCK:(h + 1) * BLOCK])
            vv = _gelu(zc_ref[:, A_WIDTH + h * BLOCK:A_WIDTH + (h + 1) * BLOCK])
            vhat, _ = _rms(vv)
            vn = (vhat * gv_ref[h:h + 1, :]).astype(BF16)
            w = jnp.where(tril, sw_ref[h], 0.0).astype(BF16)
            mixed = _dot(w, vn) + sbt_ref[:, h:h + 1]
            y_ref[:, h * BLOCK:(h + 1) * BLOCK] = (u * mixed).astype(BF16)
        valid = _window_mask(n)
        for kvh in range(B_KV_HEADS):
            k_raw, v_raw = _kv_band(zc_ref, zp_ref, kvh)
            khat, _ = _rms(k_raw)
            kband = (khat * gk_ref[...]).astype(BF16)
            vband = v_raw.astype(BF16)
            qhat, _ = _rms(_stack_heads(zc_ref, Q_OFF + kvh * GROUP * HEAD_DIM))
            qn = (qhat * gq_ref[...]).astype(BF16)
            w, _, _ = _group_softmax(qn, kband, bias_ref, sink_ref, kvh, valid)
            o = _dot(w.astype(BF16), vband)
            _unstack_heads(y_ref, A_WIDTH + kvh * GROUP * HEAD_DIM, o.astype(BF16))

    cur, prev, params, _ = _mix_specs(nb, lambda n: n)
    out, moved = _call(
        body, (z, z, sw, sb_t, gv, gq, gk, sinks, bias), name=name, grid=(nb,), in_specs=[cur, prev] + params,
        out_specs=[pl.BlockSpec((BLOCK, A_WIDTH + B_WIDTH), lambda n: (n, 0))],
        out_shape=[jax.ShapeDtypeStruct((t, A_WIDTH + B_WIDTH), BF16)],
        semantics=("parallel",), comm=comm, relay_at=relay_at)
    return out[0] if comm is None else (out[0], moved)


def _mix_bwd(z, dy, sw, sb_t, gv, gq, gk, sinks, bias, name, comm=None):
    t = z.shape[0]
    nb = t // BLOCK

    def body(zc_ref, zp_ref, dy_ref, sw_ref, sbt_ref, gv_ref, gq_ref, gk_ref, sink_ref, bias_ref,
             dz_ref, dsw_ref, dsb_ref, dgv_ref, dgq_ref, dgk_ref, dsink_ref, dbias_ref, carry_ref):
        step = pl.program_id(0)
        n = nb - 1 - step

        @pl.when(step == 0)
        def _():
            dsw_ref[...] = jnp.zeros_like(dsw_ref)
            dsb_ref[...] = jnp.zeros_like(dsb_ref)
            dgv_ref[...] = jnp.zeros_like(dgv_ref)
            dgq_ref[...] = jnp.zeros_like(dgq_ref)
            dgk_ref[...] = jnp.zeros_like(dgk_ref)
            dsink_ref[...] = jnp.zeros_like(dsink_ref)
            dbias_ref[...] = jnp.zeros_like(dbias_ref)
            carry_ref[...] = jnp.zeros_like(carry_ref)

        ri = lax.broadcasted_iota(jnp.int32, (BLOCK, BLOCK), 0)
        ci = lax.broadcasted_iota(jnp.int32, (BLOCK, BLOCK), 1)
        tril = ri >= ci
        dsb = jnp.zeros((BLOCK, BLOCK), F32)
        for h in range(A_HEADS):
            uo, vo = h * BLOCK, A_WIDTH + h * BLOCK
            u_raw, v_raw = zc_ref[:, uo:uo + BLOCK], zc_ref[:, vo:vo + BLOCK]
            (u, du_raw), (vv, dv_raw) = _gelu_and_grad(u_raw), _gelu_and_grad(v_raw)
            vhat, r = _rms(vv)
            vn = (vhat * gv_ref[h:h + 1, :]).astype(BF16)
            w = jnp.where(tril, sw_ref[h], 0.0).astype(BF16)
            mixed = _dot(w, vn) + sbt_ref[:, h:h + 1]
            dya = dy_ref[:, uo:uo + BLOCK]
            dmixed = dya * u
            dm16 = dmixed.astype(BF16)
            dsw_ref[h] += jnp.where(tril, _dot_nt(dm16, vn), 0.0)
            dsb = dsb + jnp.where(ci == h, jnp.sum(dmixed, axis=1, keepdims=True), 0.0)
            dvn = _dot_tn(w, dm16)
            dgv_ref[h:h + 1, :] += jnp.sum(dvn * vhat, axis=0, keepdims=True)
            dvv = _rms_bwd(dvn * gv_ref[h:h + 1, :], vhat, r)
            dz_ref[:, uo:uo + BLOCK] = (dya * mixed * du_raw).astype(BF16)
            dz_ref[:, vo:vo + BLOCK] = (dvv * dv_raw).astype(BF16)
        dsb_ref[...] += dsb

        valid = _window_mask(n)
        lane = lax.broadcasted_iota(jnp.int32, (1, BLOCK), 1)
        dsink = jnp.zeros((1, BLOCK), F32)
        dgq = jnp.zeros((1, HEAD_DIM), F32)
        for kvh in range(B_KV_HEADS):
            ko, vo = K_OFF + kvh * HEAD_DIM, V_OFF + kvh * HEAD_DIM
            k_raw, v_raw = _kv_band(zc_ref, zp_ref, kvh)
            khat, kr = _rms(k_raw)
            kband = (khat * gk_ref[...]).astype(BF16)
            vband = v_raw.astype(BF16)
            qo = Q_OFF + kvh * GROUP * HEAD_DIM
            qhat, qr = _rms(_stack_heads(zc_ref, qo))
            qn = (qhat * gq_ref[...]).astype(BF16)
            w, wsink, _ = _group_softmax(qn, kband, bias_ref, sink_ref, kvh, valid)
            do = _stack_heads(dy_ref, A_WIDTH + kvh * GROUP * HEAD_DIM).astype(BF16)
            dvb = _dot_tn(w.astype(BF16), do)
            dw = _dot_nt(do, vband)
            rowdot = jnp.sum(w * dw, axis=-1, keepdims=True)
            ds = w * (dw - rowdot)
            dsink_rows = -wsink * rowdot
            for g in range(GROUP):
                head_sum = jnp.sum(dsink_rows[g * BLOCK:(g + 1) * BLOCK], axis=0, keepdims=True)
                dsink = dsink + jnp.where(lane == kvh * GROUP + g, head_sum, 0.0)
            dbias_ref[kvh * GROUP:(kvh + 1) * GROUP] += ds.reshape(GROUP, BLOCK, 2 * BLOCK)
            ds16 = (ds * (HEAD_DIM ** -0.5)).astype(BF16)
            dqn = _dot(ds16, kband)
            dkn = _dot_tn(ds16, qn)
            dgq = dgq + jnp.sum(dqn * qhat, axis=0, keepdims=True)
            _unstack_heads(dz_ref, qo, _rms_bwd(dqn * gq_ref[...], qhat, qr).astype(BF16))
            dgk_ref[...] += jnp.sum(dkn * khat, axis=0, keepdims=True)
            dk = _rms_bwd(dkn * gk_ref[...], khat, kr)
            co = kvh * HEAD_DIM
            dz_ref[:, ko:ko + HEAD_DIM] = (dk[BLOCK:] + carry_ref[:, co:co + HEAD_DIM]).astype(BF16)
            dz_ref[:, vo:vo + HEAD_DIM] = (dvb[BLOCK:] + carry_ref[:, KV_WIDTH + co:KV_WIDTH + co + HEAD_DIM]).astype(BF16)
            carry_ref[:, co:co + HEAD_DIM] = dk[:BLOCK]
            carry_ref[:, KV_WIDTH + co:KV_WIDTH + co + HEAD_DIM] = dvb[:BLOCK]
        dgq_ref[...] += dgq
        dsink_ref[...] += dsink

    order = lambda n: nb - 1 - n
    cur, prev, params, full = _mix_specs(nb, order)
    dy_spec = pl.BlockSpec((BLOCK, A_WIDTH + B_WIDTH), lambda n: (order(n), 0))
    out, moved = _call(
        body, (z, z, dy, sw, sb_t, gv, gq, gk, sinks, bias), name=name, grid=(nb,),
        in_specs=[cur, prev, dy_spec] + params,
        out_specs=[cur, full((A_HEADS, BLOCK, BLOCK)), full((BLOCK, BLOCK)), full((A_HEADS, BLOCK)),
                   full((1, HEAD_DIM)), full((1, HEAD_DIM)), full((1, BLOCK)), full((B_HEADS, BLOCK, 2 * BLOCK))],
        out_shape=[jax.ShapeDtypeStruct((t, IN_COLS), BF16), jax.ShapeDtypeStruct((A_HEADS, BLOCK, BLOCK), F32),
                   jax.ShapeDtypeStruct((BLOCK, BLOCK), F32), jax.ShapeDtypeStruct((A_HEADS, BLOCK), F32),
                   jax.ShapeDtypeStruct((1, HEAD_DIM), F32), jax.ShapeDtypeStruct((1, HEAD_DIM), F32),
                   jax.ShapeDtypeStruct((1, BLOCK), F32), jax.ShapeDtypeStruct((B_HEADS, BLOCK, 2 * BLOCK), F32)],
        scratch_shapes=[pltpu.VMEM((BLOCK, 2 * KV_WIDTH), F32)],
        semantics=("arbitrary",), comm=comm)
    return out if comm is None else (out, moved)


def _ada_fwd(c_all, w_ada, b_cols, name, tn=768):
    nb, d = c_all.shape
    cols = w_ada.shape[1]

    def body(c_ref, w_ref, b_ref, act_ref, mod_ref):
        cv = c_ref[...]
        act = cv * jax.nn.sigmoid(cv)
        act_ref[...] = act
        mod_ref[...] = _dot(act.astype(BF16), w_ref[...].astype(BF16)) + b_ref[...]

    return _pc(
        body, name=name, grid=(cols // tn,),
        in_specs=[pl.BlockSpec((nb, d), lambda j: (0, 0)), pl.BlockSpec((d, tn), lambda j: (0, j)),
                  pl.BlockSpec((1, tn), lambda j: (0, j))],
        out_specs=[pl.BlockSpec((nb, d), lambda j: (0, 0)), pl.BlockSpec((nb, tn), lambda j: (0, j))],
        out_shape=[jax.ShapeDtypeStruct((nb, d), F32), jax.ShapeDtypeStruct((nb, cols), F32)],
        compiler_params=_params(("arbitrary",)),
    )(c_all, w_ada, b_cols)


def _adamw(w, g, m, v):
    m = ADAM_B1 * m + (1.0 - ADAM_B1) * g
    v = ADAM_B2 * v + (1.0 - ADAM_B2) * (g * g)
    m_hat = m / (1.0 - ADAM_B1 ** ADAM_STEP)
    v_hat = v / (1.0 - ADAM_B2 ** ADAM_STEP)
    delta = -ADAM_LR * (m_hat / (jnp.sqrt(v_hat) + ADAM_EPS) + ADAM_WD * w)
    return delta, m, v


def _adam_outer(act_t, dmod, w, m, v, name, tr=256):
    rows, cols = w.shape
    nb = dmod.shape[0]

    def body(act_ref, dm_ref, w_ref, m_ref, v_ref, g_ref, d_ref, nm_ref, nv_ref):
        act = act_ref[...].astype(BF16).astype(F32)
        dm = dm_ref[...].astype(BF16).astype(F32)
        g = act[:, 0:1] * dm[0:1, :]
        for b in range(1, nb):
            g = g + act[:, b:b + 1] * dm[b:b + 1, :]
        g_ref[...] = g
        d_ref[...], nm_ref[...], nv_ref[...] = _adamw(w_ref[...], g, m_ref[...], v_ref[...])

    tile = pl.BlockSpec((tr, cols), lambda i: (i, 0))
    return _pc(
        body, name=name, grid=(rows // tr,),
        in_specs=[pl.BlockSpec((tr, nb), lambda i: (i, 0)), pl.BlockSpec((nb, cols), lambda i: (0, 0)), tile, tile, tile],
        out_specs=[tile] * 4, out_shape=[jax.ShapeDtypeStruct((rows, cols), F32)] * 4,
        compiler_params=_params(("parallel",)),
    )(act_t, dmod, w, m, v)


def _adam_reduce(parts, w, m, v, name, tr):
    rows, cols = w.shape

    def body(p_ref, w_ref, m_ref, v_ref, g_ref, d_ref, nm_ref, nv_ref):
        g = p_ref[0].astype(F32)
        for i in range(1, N_DEV):
            g = g + p_ref[i].astype(F32)
        g_ref[...] = g
        d_ref[...], nm_ref[...], nv_ref[...] = _adamw(w_ref[...], g, m_ref[...], v_ref[...])

    tile = pl.BlockSpec((tr, cols), lambda i: (i, 0))
    return _pc(
        body, name=name, grid=(rows // tr,),
        in_specs=[pl.BlockSpec((N_DEV, tr, cols), lambda i: (0, i, 0)), tile, tile, tile],
        out_specs=[tile] * 4, out_shape=[jax.ShapeDtypeStruct((rows, cols), F32)] * 4,
        compiler_params=_params(("parallel",)),
    )(parts, w, m, v)


def _ungather_cols(g):
    n, r, c = g.shape
    return jnp.transpose(g, (1, 0, 2)).reshape(r, n * c)


def _scatter_cols(w):
    r, c = w.shape
    return jnp.transpose(w.reshape(r, N_DEV, c // N_DEV), (1, 0, 2)).astype(BF16)


def _pad_rows(v):
    flat = v.reshape(-1)
    pad = (-flat.shape[0]) % (8 * BLOCK)
    return jnp.pad(flat, (0, pad)).reshape(-1, BLOCK)


def kernel(x, c, w_ada, b_ada, g_ffn1, w1_ffn1, w3_ffn1, w2_ffn1, g_mix, w_in, spatial_w, spatial_b, g_v, g_q, g_k, sinks, rel_bias, w_out, g_ffn2, w1_ffn2, w3_ffn2, w2_ffn2, loss_target, m_w_ada, m_b_ada, m_g_ffn1, m_w1_ffn1, m_w3_ffn1, m_w2_ffn1, m_g_mix, m_w_in, m_spatial_w, m_spatial_b, m_g_v, m_g_q, m_g_k, m_sinks, m_rel_bias, m_w_out, m_g_ffn2, m_w1_ffn2, m_w3_ffn2, m_w2_ffn2, v_w_ada, v_b_ada, v_g_ffn1, v_w1_ffn1, v_w3_ffn1, v_w2_ffn1, v_g_mix, v_w_in, v_spatial_w, v_spatial_b, v_g_v, v_g_q, v_g_k, v_sinks, v_rel_bias, v_w_out, v_g_ffn2, v_w1_ffn2, v_w3_ffn2, v_w2_ffn2):
    me = _linear(_position())
    d = D_MODEL
    x0 = x[0]
    target = loss_target[0]
    t = x0.shape[0]

    big = dict(w1a=w1_ffn1[0], w3a=w3_ffn1[0], w2a=w2_ffn1[0], w_in=w_in[0], w_out=w_out[0],
               w1b=w1_ffn2[0], w3b=w3_ffn2[0], w2b=w2_ffn2[0])
    col_sharded = ("w1a", "w3a", "w_in", "w1b", "w3b")
    full = {}

    def gather(keys):
        return _GatherTwoLevel([big[k].astype(BF16) for k in keys])

    def unpack(keys, gathered):
        for k, gth in zip(keys, gathered):
            full[k] = _ungather_cols(gth) if k in col_sharded else gth.reshape(-1, gth.shape[-1])

    def share(k):
        gk_ = grads[k]
        if k in col_sharded:
            return _scatter_cols(gk_)
        return gk_.reshape(N_DEV, gk_.shape[0] // N_DEV, gk_.shape[1]).astype(BF16)

    unpack(("w1a", "w3a"), _run_comm(gather(("w1a", "w3a")), "gather_ffn1"))

    c_all = _run_comm(_Direct([c], True), "gather_c")[0].reshape(N_DEV, d)
    b_cols = lax.dynamic_slice(b_ada, (0, me * MOD_COLS), (1, MOD_COLS))
    act_all, mod_cols = _ada_fwd(c_all, w_ada[0], b_cols, "ada_fwd")
    mod = _run_comm(_Direct([mod_cols.reshape(N_DEV, 1, MOD_COLS)], False), "scatter_mod")[0]
    mod = mod.reshape(N_MOD, 1, d)
    sh1, sc1, gt1, sh2, sc2, gt2, sh3, sc3, gt3 = [mod[i] for i in range(N_MOD)]

    tm = min(FFN_ROWS, t)
    h1 = _norm_fwd(x0, None, None, 0.0, g_ffn1, sc1, sh1, "norm1_fwd")
    (a1, b1, s1), gathered = _ffn_up(h1, full["w1a"], full["w3a"], "ffn1_up",
                                     comm=gather(("w2a", "w_in", "w_out")), relay_at=0.6)
    unpack(("w2a", "w_in", "w_out"), gathered)
    f1, gathered = _matmul(s1, full["w2a"], "nn", tm, 512, D_FF, F32, "ffn1_down", comm=gather(("w1b",)),
                           relay_at=0.7)
    unpack(("w1b",), gathered)
    x1, h2 = _norm_fwd(x0, f1, gt1, 0.5, g_mix, sc2, sh2, "norm2_fwd")
    z = _matmul(h2, full["w_in"], "nn", 512, IN_COLS // 2, d, F32, "mix_in_fwd")
    onehot = _bucket_onehot()
    bias = _select_matmul(rel_bias.T, jnp.asarray(onehot, BF16), "bias_table").reshape(B_HEADS, BLOCK, 2 * BLOCK)
    sb_t = spatial_b[0].T
    mix_params = (spatial_w[0], sb_t, g_v[0], g_q, g_k, sinks, bias)
    ycat, gathered = _mix_fwd(z, *mix_params, "mix_core_fwd", comm=gather(("w3b",)), relay_at=0.6)
    unpack(("w3b",), gathered)
    y = _matmul(ycat, full["w_out"], "nn", 512, d, d, F32, "mix_out_fwd")
    x2, h3 = _norm_fwd(x1, y, gt2, 1.0, g_ffn2, sc3, sh3, "norm3_fwd")
    (a2, b2, s2), gathered = _ffn_up(h3, full["w1b"], full["w3b"], "ffn2_up", comm=gather(("w2b",)), relay_at=0.6)
    unpack(("w2b",), gathered)
    f2 = _matmul(s2, full["w2b"], "nn", tm, 512, D_FF, F32, "ffn2_down")
    dx3, df2, loss_part = _loss_head(x2, f2, gt3, 0.5, target, "loss_head")
    loss = lax.psum(loss_part[0, 0], ("x", "y", "c"))

    grads, parts = {}, {}
    da, db = _ffn_dact(df2, a2, b2, full["w2b"], "ffn2_dact")
    dh3 = _ffn_dh(da, db, full["w1b"], full["w3b"], "ffn2_dh")
    tk = min(DW_TOKENS, t)
    grads["w1b"] = _matmul(h3, da, "tn", 1024, 512, tk, BF16, "ffn2_dw1")
    grads["w3b"] = _matmul(h3, db, "tn", 1024, 512, tk, BF16, "ffn2_dw3")
    grads["w2b"] = _matmul(s2, df2, "tn", 512, 1024, tk, BF16, "ffn2_dw2")
    dx2, dyg, sums3 = _norm_bwd(dh3, x2, dx3, f2, 0.5, g_ffn2, sc3, gt2, 1.0, "norm3_bwd")
    dycat = _matmul(dyg, full["w_out"], "nt", 512, d, d, F32, "mix_out_bwd")
    mix_grads, moved = _mix_bwd(z, dycat, *mix_params, "mix_core_bwd",
                                comm=_Direct([share("w1b"), share("w3b")], False))
    parts["w1b"], parts["w3b"] = moved
    dz, d_sw, d_sb, d_gv, d_gq, d_gk, d_sink, d_bias = mix_grads
    d_rel = _select_matmul(d_bias.reshape(B_HEADS, BLOCK * 2 * BLOCK), jnp.asarray(onehot.T, BF16), "bias_table_bwd")
    dh2 = _matmul(dz, full["w_in"], "nt", 512, d, IN_COLS // 2, F32, "mix_in_bwd")
    dx1, df1, sums2 = _norm_bwd(dh2, x1, dx2, y, 1.0, g_mix, sc2, gt1, 0.5, "norm2_bwd")
    (da, db), moved = _ffn_dact(df1, a1, b1, full["w2a"], "ffn1_dact", comm=_Direct([share("w2b")], False))
    parts["w2b"] = moved[0]
    dh1 = _ffn_dh(da, db, full["w1a"], full["w3a"], "ffn1_dh")
    grads["w1a"] = _matmul(h1, da, "tn", 1024, 512, tk, BF16, "ffn1_dw1")
    grads["w3a"], moved = _matmul(h1, db, "tn", 1024, 512, tk, BF16, "ffn1_dw3", comm=_Direct([share("w1a")], False))
    parts["w1a"] = moved[0]
    grads["w2a"], moved = _matmul(s1, df1, "tn", 512, 1024, tk, BF16, "ffn1_dw2", comm=_Direct([share("w3a")], False))
    parts["w3a"] = moved[0]
    grads["w_in"], moved = _matmul(h2, dz, "tn", 1024, 256, tk, BF16, "mix_in_dw", comm=_Direct([share("w2a")], False))
    parts["w2a"] = moved[0]
    grads["w_out"], moved = _matmul(ycat, dyg, "tn", 1024, 512, tk, BF16, "mix_out_dw",
                                    comm=_Direct([share("w_in")], False))
    parts["w_in"] = moved[0]
    (dx0, sums1), moved = _norm_bwd(dh1, x0, dx1, f1, 0.5, g_ffn1, sc1, None, 0.0, "norm1_bwd",
                                    comm=_Direct([share("w_out")], False))
    parts["w_out"] = moved[0]

    moments = dict(w1a=(m_w1_ffn1, v_w1_ffn1), w3a=(m_w3_ffn1, v_w3_ffn1), w2a=(m_w2_ffn1, v_w2_ffn1),
                   w_in=(m_w_in, v_w_in), w_out=(m_w_out, v_w_out),
                   w1b=(m_w1_ffn2, v_w1_ffn2), w3b=(m_w3_ffn2, v_w3_ffn2), w2b=(m_w2_ffn2, v_w2_ffn2))
    upd = {}
    for k in big:
        rows = big[k].shape[0]
        tr = 256 if rows % 256 == 0 else rows // 4
        upd[k] = [o[None] for o in _adam_reduce(parts[k], big[k], moments[k][0][0], moments[k][1][0], "adam_" + k, tr)]

    dmod = jnp.concatenate([sums1[1:2], sums1[0:1], sums1[3:4], sums2[1:2], sums2[0:1], sums2[3:4],
                            sums3[1:2], sums3[0:1], sums3[3:4]], axis=1)
    small = [("b_ada", dmod, b_ada, m_b_ada, v_b_ada),
             ("g_ffn1", sums1[2:3], g_ffn1, m_g_ffn1, v_g_ffn1),
             ("g_mix", sums2[2:3], g_mix, m_g_mix, v_g_mix),
             ("spatial_w", d_sw[None], spatial_w, m_spatial_w, v_spatial_w),
             ("spatial_b", d_sb[:, :A_HEADS].T[None], spatial_b, m_spatial_b, v_spatial_b),
             ("g_v", d_gv[None], g_v, m_g_v, v_g_v),
             ("g_q", d_gq, g_q, m_g_q, v_g_q),
             ("g_k", d_gk, g_k, m_g_k, v_g_k),
             ("sinks", d_sink[:, :B_HEADS], sinks, m_sinks, v_sinks),
             ("rel_bias", d_rel.T, rel_bias, m_rel_bias, v_rel_bias),
             ("g_ffn2", sums3[2:3], g_ffn2, m_g_ffn2, v_g_ffn2)]
    packed = [jnp.concatenate([_pad_rows(item[i]) for item in small], axis=0) for i in range(1, 5)]
    g_all = _run_comm(_Direct([packed[0]], True), "gather_small")[0]
    small_out = _adam_reduce(g_all, packed[1], packed[2], packed[3], "adam_small", packed[0].shape[0])
    off = 0
    for name, g_, w_, _, _ in small:
        n_rows = _pad_rows(w_).shape[0]
        upd[name] = [o[off:off + n_rows].reshape(-1)[:w_.size].reshape(w_.shape) for o in small_out]
        off += n_rows

    dmod_rows = g_all[:, :N_MOD * d // BLOCK, :].reshape(N_DEV, N_MOD * d)
    dmod_cols = lax.dynamic_slice(dmod_rows, (0, me * MOD_COLS), (N_DEV, MOD_COLS))
    upd["w_ada"] = [o[None] for o in _adam_outer(act_all.T, dmod_cols, w_ada[0], m_w_ada[0], v_w_ada[0], "adam_w_ada")]

    order = [("w_ada", "w_ada"), ("b_ada", "b_ada"), ("g_ffn1", "g_ffn1"), ("w1_ffn1", "w1a"), ("w3_ffn1", "w3a"),
             ("w2_ffn1", "w2a"), ("g_mix", "g_mix"), ("w_in", "w_in"), ("spatial_w", "spatial_w"),
             ("spatial_b", "spatial_b"), ("g_v", "g_v"), ("g_q", "g_q"), ("g_k", "g_k"), ("sinks", "sinks"),
             ("rel_bias", "rel_bias"), ("w_out", "w_out"), ("g_ffn2", "g_ffn2"), ("w1_ffn2", "w1b"),
             ("w3_ffn2", "w3b"), ("w2_ffn2", "w2b")]
    outs = [loss, dx0[None]]
    for i in range(4):
        outs += [upd[key][i] for _, key in order]
    return tuple(outs)
```

```python
import functools
import math

import numpy as np
import jax
import jax.numpy as jnp
from jax import lax
from jax.experimental import pallas as pl
from jax.experimental.pallas import tpu as pltpu

F32 = jnp.float32
BF16 = jnp.bfloat16

D_MODEL = 2048
D_FF = 5632
BLOCK = 128
A_HEADS = 8
A_WIDTH = 1024
B_HEADS = 16
B_KV_HEADS = 2
GROUP = B_HEADS // B_KV_HEADS
HEAD_DIM = 64
B_WIDTH = 1024
KV_WIDTH = 128
IN_COLS = 3328
Q_OFF = 2 * A_WIDTH
K_OFF = Q_OFF + B_WIDTH
V_OFF = K_OFF + KV_WIDTH
N_BUCKETS = 32
N_MOD = 9
EPS = 1e-6
N_DEV = 8
MOD_COLS = N_MOD * D_MODEL // N_DEV

ADAM_LR = 0.001
ADAM_B1 = 0.9
ADAM_B2 = 0.999
ADAM_EPS = 1e-08
ADAM_WD = 0.01
ADAM_STEP = 10

FFN_ROWS = 1024
DW_TOKENS = 4096
MASK_VALUE = -1e30
VMEM_LIMIT = 56 * 1024 * 1024
MESH_ID = pl.DeviceIdType.MESH
ANY = pl.BlockSpec(memory_space=pl.ANY)

_SQRT_HALF = 0.7071067811865476
_INV_SQRT_2PI = 0.3989422804014327


def _pc(body, **kw):
    return pl.pallas_call(body, **kw)


def _params(sem=None):
    return pltpu.CompilerParams(dimension_semantics=sem, vmem_limit_bytes=VMEM_LIMIT)


def _dot(a, b):
    return lax.dot_general(a, b, (((1,), (0,)), ((), ())), preferred_element_type=F32)


def _dot_nt(a, b):
    return lax.dot_general(a, b, (((1,), (1,)), ((), ())), preferred_element_type=F32)


def _dot_tn(a, b):
    return lax.dot_general(a, b, (((0,), (0,)), ((), ())), preferred_element_type=F32)


def _gelu(x):
    return 0.5 * x * (1.0 + lax.erf(x * _SQRT_HALF))


def _gelu_and_grad(x):
    cdf = 0.5 * (1.0 + lax.erf(x * _SQRT_HALF))
    return x * cdf, cdf + x * jnp.exp(-0.5 * x * x) * _INV_SQRT_2PI


def _rms(x):
    r = lax.rsqrt(jnp.mean(x * x, axis=-1, keepdims=True) + EPS)
    return x * r, r


def _rms_bwd(dy, y, r):
    return r * (dy - y * jnp.mean(dy * y, axis=-1, keepdims=True))


def _row_spec(tm, cols):
    return pl.BlockSpec((tm, cols), lambda i: (i, 0))


def _vec_spec(cols):
    return pl.BlockSpec((1, cols), lambda i: (0, 0))


def _position():
    x, y, c = lax.axis_index("x"), lax.axis_index("y"), lax.axis_index("c")
    return x, y, c


def _linear(p):
    return 4 * p[0] + 2 * p[1] + p[2]


class _Comm:
    PEER_COPIES = N_DEV - 1

    def __init__(self, arrs):
        self.arrs = list(arrs)
        n = len(self.arrs)
        self.scratch_shapes = [pltpu.SemaphoreType.DMA((self.PEER_COPIES * n,)),
                               pltpu.SemaphoreType.DMA((self.PEER_COPIES * n,)),
                               pltpu.SemaphoreType.DMA((n,))]

    def bind(self, srcs, dsts, sems):
        self.srcs, self.dsts = srcs, dsts
        self.send_sems, self.recv_sems, self.local_sems = sems
        x, y, c = _position()
        self.me, self.sibling, self.core = (x, y, c), (x, y, 1 - c), c
        self.chips = [(1 - x, y), (x, 1 - y), (1 - x, 1 - y)]
        self.peers = [(1 - x if k & 4 else x, 1 - y if k & 2 else y, 1 - c if k & 1 else c)
                      for k in range(1, N_DEV)]

    def relay(self):
        pass


class _GatherTwoLevel(_Comm):
    def __init__(self, arrs):
        super().__init__(arrs)
        self.out_shape = [jax.ShapeDtypeStruct((N_DEV,) + a.shape, a.dtype) for a in self.arrs]

    def _copy(self, a, k, block, to, from_input=False):
        rows = self.dsts[a].at[_linear(block)]
        return pltpu.make_async_remote_copy(
            src_ref=self.srcs[a] if from_input else rows, dst_ref=rows,
            send_sem=self.send_sems.at[self.PEER_COPIES * a + k], recv_sem=self.recv_sems.at[self.PEER_COPIES * a + k],
            device_id=to, device_id_type=MESH_ID)

    def _local(self, a):
        return pltpu.make_async_copy(self.srcs[a], self.dsts[a].at[_linear(self.me)], self.local_sems.at[a])

    def _first(self, a):
        return [self._copy(a, 0, self.me, self.sibling, True)] + [
            self._copy(a, 1 + j, self.me, (*chip, self.core), True) for j, chip in enumerate(self.chips)]

    def _passed(self, a):
        return [self._copy(a, 4 + j, (*chip, self.core), self.sibling) for j, chip in enumerate(self.chips)]

    def start(self):
        for a in range(len(self.arrs)):
            self._local(a).start()
            for cp in self._first(a):
                cp.start()

    def relay(self):
        for a in range(len(self.arrs)):
            for j, chip in enumerate(self.chips):
                self._copy(a, 1 + j, (*chip, self.core), self.me).wait_recv()
                self._passed(a)[j].start()

    def finish(self):
        for a in range(len(self.arrs)):
            self._copy(a, 0, self.sibling, self.me).wait_recv()
            for j, chip in enumerate(self.chips):
                self._copy(a, 4 + j, (*chip, 1 - self.core), self.me).wait_recv()
        for a in range(len(self.arrs)):
            for cp in self._first(a) + self._passed(a):
                cp.wait_send()
            self._local(a).wait()


class _Direct(_Comm):
    def __init__(self, arrs, broadcast):
        super().__init__(arrs)
        self.broadcast = broadcast
        self.out_shape = [jax.ShapeDtypeStruct(((N_DEV,) + a.shape) if broadcast else a.shape, a.dtype)
                          for a in self.arrs]

    def _outgoing(self, a, to):
        return self.srcs[a] if self.broadcast else self.srcs[a].at[_linear(to)]

    def _copy(self, a, k, sender, to):
        return pltpu.make_async_remote_copy(
            src_ref=self._outgoing(a, to), dst_ref=self.dsts[a].at[_linear(sender)],
            send_sem=self.send_sems.at[self.PEER_COPIES * a + k], recv_sem=self.recv_sems.at[self.PEER_COPIES * a + k],
            device_id=to, device_id_type=MESH_ID)

    def _local(self, a):
        return pltpu.make_async_copy(self._outgoing(a, self.me), self.dsts[a].at[_linear(self.me)],
                                     self.local_sems.at[a])

    def start(self):
        for a in range(len(self.arrs)):
            self._local(a).start()
            for k, peer in enumerate(self.peers):
                self._copy(a, k, self.me, peer).start()

    def finish(self):
        for a in range(len(self.arrs)):
            for k, peer in enumerate(self.peers):
                self._copy(a, k, peer, self.me).wait_recv()
        for a in range(len(self.arrs)):
            for k, peer in enumerate(self.peers):
                self._copy(a, k, self.me, peer).wait_send()
            self._local(a).wait()


def _run_comm(comm, name):
    n = len(comm.arrs)

    def body(*refs):
        comm.bind(refs[:n], refs[n:2 * n], refs[2 * n:])
        comm.start()
        comm.relay()
        comm.finish()

    return _pc(body, name=name, out_shape=comm.out_shape, in_specs=[ANY] * n, out_specs=[ANY] * n,
               scratch_shapes=comm.scratch_shapes)(*comm.arrs)


def _call(body, inputs, *, name, grid, in_specs, out_specs, out_shape, scratch_shapes=(), semantics,
          comm=None, relay_at=0.5):
    if comm is None:
        out = _pc(body, name=name, grid=grid, in_specs=in_specs, out_specs=out_specs, out_shape=out_shape,
                  scratch_shapes=list(scratch_shapes), compiler_params=_params(semantics))(*inputs)
        return list(out), []
    n_in, n_out, n_sc, k = len(in_specs), len(out_specs), len(scratch_shapes), len(comm.arrs)
    steps = math.prod(grid)
    relay_step = min(steps - 1, int(steps * relay_at))

    def carrier(*refs):
        ins, rest = refs[:n_in], refs[n_in:]
        csrc, rest = rest[:k], rest[k:]
        outs, rest = rest[:n_out], rest[n_out:]
        cdst, rest = rest[:k], rest[k:]
        scratch, csems = rest[:n_sc], rest[n_sc:]
        step = 0
        for axis, size in enumerate(grid):
            step = step * size + pl.program_id(axis)
        comm.bind(csrc, cdst, csems)
        pl.when(step == 0)(comm.start)
        pl.when(step == relay_step)(comm.relay)
        body(*ins, *outs, *scratch)
        pl.when(step == steps - 1)(comm.finish)

    out = _pc(carrier, name=name, grid=grid, in_specs=list(in_specs) + [ANY] * k,
              out_specs=list(out_specs) + [ANY] * k, out_shape=list(out_shape) + comm.out_shape,
              scratch_shapes=list(scratch_shapes) + comm.scratch_shapes,
              compiler_params=_params(("arbitrary",) * len(grid)))(*inputs, *comm.arrs)
    return list(out[:n_out]), list(out[n_out:])


def _matmul(a, b, mode, tm, tn, tk, out_dtype, name, comm=None, relay_at=0.5):
    if mode == "nn":
        (m, kk), nn = a.shape, b.shape[1]
        a_spec = pl.BlockSpec((tm, tk), lambda i, j, k: (i, k))
        b_spec = pl.BlockSpec((tk, tn), lambda i, j, k: (k, j))
        dot = _dot
    elif mode == "nt":
        (m, kk), nn = a.shape, b.shape[0]
        a_spec = pl.BlockSpec((tm, tk), lambda i, j, k: (i, k))
        b_spec = pl.BlockSpec((tn, tk), lambda i, j, k: (j, k))
        dot = _dot_nt
    else:
        (kk, m), nn = a.shape, b.shape[1]
        a_spec = pl.BlockSpec((tk, tm), lambda i, j, k: (k, i))
        b_spec = pl.BlockSpec((tk, tn), lambda i, j, k: (k, j))
        dot = _dot_tn
    assert m % tm == 0 and nn % tn == 0 and kk % tk == 0, (a.shape, b.shape, tm, tn, tk)
    nk = kk // tk
    narrow = nk > 1 and out_dtype != F32

    def body(a_ref, b_ref, o_ref, *acc):
        p = dot(a_ref[...], b_ref[...])
        if nk == 1:
            o_ref[...] = p.astype(o_ref.dtype)
            return
        k = pl.program_id(2)
        acc_ref = acc[0] if narrow else o_ref

        @pl.when(k == 0)
        def _():
            acc_ref[...] = p

        @pl.when((k > 0) & (k < nk - 1) if narrow else k > 0)
        def _():
            acc_ref[...] += p

        if narrow:
            @pl.when(k == nk - 1)
            def _():
                o_ref[...] = (acc_ref[...] + p).astype(o_ref.dtype)

    out, moved = _call(
        body, (a, b), name=name, grid=(m // tm, nn // tn, nk),
        in_specs=[a_spec, b_spec], out_specs=[pl.BlockSpec((tm, tn), lambda i, j, k: (i, j))],
        out_shape=[jax.ShapeDtypeStruct((m, nn), out_dtype)],
        scratch_shapes=[pltpu.VMEM((tm, tn), F32)] if narrow else [],
        semantics=("parallel", "parallel", "arbitrary"), comm=comm, relay_at=relay_at)
    return out[0] if comm is None else (out[0], moved)


def _norm_fwd(x_prev, f_prev, gate_prev, coef, g, sc, sh, name, tm=256):
    t, d = x_prev.shape
    residual = f_prev is not None

    def body(*refs):
        if residual:
            xp_ref, fp_ref, gp_ref, g_ref, sc_ref, sh_ref, x_ref, h_ref = refs
            x = xp_ref[...] + coef * gp_ref[...] * fp_ref[...]
            x_ref[...] = x
        else:
            xp_ref, g_ref, sc_ref, sh_ref, h_ref = refs
            x = xp_ref[...]
        y, _ = _rms(x)
        h_ref[...] = ((y * g_ref[...]) * (1.0 + sc_ref[...]) + sh_ref[...]).astype(BF16)

    row, vec = _row_spec(tm, d), _vec_spec(d)
    if residual:
        ins, in_specs = (x_prev, f_prev, gate_prev, g, sc, sh), [row, row, vec, vec, vec, vec]
        out_shape = [jax.ShapeDtypeStruct((t, d), F32), jax.ShapeDtypeStruct((t, d), BF16)]
        out_specs = [row, row]
    else:
        ins, in_specs = (x_prev, g, sc, sh), [row, vec, vec, vec]
        out_shape = [jax.ShapeDtypeStruct((t, d), BF16)]
        out_specs = [row]
    out = _pc(body, name=name, grid=(t // tm,), in_specs=in_specs, out_specs=out_specs, out_shape=out_shape,
              compiler_params=_params(("parallel",)))(*ins)
    return out if residual else out[0]


def _norm_bwd(dh, x, dxo, fo, coef_o, g, sc, gate_prev, coef_prev, name, tm=256, comm=None):
    t, d = x.shape
    with_prev = gate_prev is not None

    def body(*refs):
        if with_prev:
            dh_ref, x_ref, dxo_ref, fo_ref, g_ref, sc_ref, gp_ref, dx_ref, dfp_ref, sums_ref = refs
        else:
            dh_ref, x_ref, dxo_ref, fo_ref, g_ref, sc_ref, dx_ref, sums_ref = refs
        dh_v, dxo_v = dh_ref[...], dxo_ref[...]
        y, r = _rms(x_ref[...])
        n = y * g_ref[...]
        dn = dh_v * (1.0 + sc_ref[...])
        dx = dxo_v + _rms_bwd(dn * g_ref[...], y, r)
        dx_ref[...] = dx
        if with_prev:
            dfp_ref[...] = (coef_prev * gp_ref[...] * dx).astype(BF16)
        @pl.when(pl.program_id(0) == 0)
        def _():
            sums_ref[...] = jnp.zeros_like(sums_ref)

        sums_ref[0:1, :] += jnp.sum(dh_v * n, axis=0, keepdims=True)
        sums_ref[1:2, :] += jnp.sum(dh_v, axis=0, keepdims=True)
        sums_ref[2:3, :] += jnp.sum(dn * y, axis=0, keepdims=True)
        sums_ref[3:4, :] += jnp.sum(coef_o * dxo_v * fo_ref[...], axis=0, keepdims=True)

    row, vec = _row_spec(tm, d), _vec_spec(d)
    sums_spec = pl.BlockSpec((8, d), lambda i: (0, 0))
    ins, in_specs = [dh, x, dxo, fo, g, sc], [row, row, row, row, vec, vec]
    out_shape, out_specs = [jax.ShapeDtypeStruct((t, d), F32)], [row]
    if with_prev:
        ins.append(gate_prev)
        in_specs.append(vec)
        out_shape.append(jax.ShapeDtypeStruct((t, d), BF16))
        out_specs.append(row)
    out_shape.append(jax.ShapeDtypeStruct((8, d), F32))
    out_specs.append(sums_spec)
    out, moved = _call(body, ins, name=name, grid=(t // tm,), in_specs=in_specs, out_specs=out_specs,
                       out_shape=out_shape, semantics=("arbitrary",), comm=comm)
    return out if comm is None else (out, moved)


def _ffn_up(h, w1, w3, name, tf=512, comm=None, relay_at=0.5):
    t, d = h.shape
    ff = w1.shape[1]
    tm = min(FFN_ROWS, t)

    def body(h_ref, w1_ref, w3_ref, ga_ref, gb_ref, s_ref):
        hv = h_ref[...]
        a = _dot(hv, w1_ref[...])
        b = _dot(hv, w3_ref[...])
        sg = jax.nn.sigmoid(a)
        sil = a * sg
        ga_ref[...] = (b * (sg * (1.0 + a * (1.0 - sg)))).astype(BF16)
        gb_ref[...] = sil.astype(BF16)
        s_ref[...] = (sil * b).astype(BF16)

    tile = pl.BlockSpec((tm, tf), lambda i, j: (i, j))
    cols = pl.BlockSpec((d, tf), lambda i, j: (0, j))
    out, moved = _call(
        body, (h, w1, w3), name=name, grid=(t // tm, ff // tf),
        in_specs=[pl.BlockSpec((tm, d), lambda i, j: (i, 0)), cols, cols], out_specs=[tile, tile, tile],
        out_shape=[jax.ShapeDtypeStruct((t, ff), BF16)] * 3,
        semantics=("parallel", "parallel"), comm=comm, relay_at=relay_at)
    return out if comm is None else (out, moved)


def _ffn_dact(df, ga, gb, w2, name, tf=512, comm=None):
    t, d = df.shape
    ff = ga.shape[1]
    tm = min(FFN_ROWS, t)

    def body(df_ref, ga_ref, gb_ref, w2_ref, da_ref, db_ref):
        ds = _dot_nt(df_ref[...], w2_ref[...])
        da_ref[...] = (ds * ga_ref[...]).astype(BF16)
        db_ref[...] = (ds * gb_ref[...]).astype(BF16)

    tile = pl.BlockSpec((tm, tf), lambda i, j: (i, j))
    out, moved = _call(
        body, (df, ga, gb, w2), name=name, grid=(t // tm, ff // tf),
        in_specs=[pl.BlockSpec((tm, d), lambda i, j: (i, 0)), tile, tile, pl.BlockSpec((tf, d), lambda i, j: (j, 0))],
        out_specs=[tile, tile], out_shape=[jax.ShapeDtypeStruct((t, ff), BF16)] * 2,
        semantics=("parallel", "parallel"), comm=comm)
    return out if comm is None else (out, moved)


def _ffn_dh(da, db, w1, w3, name, tn=512, comm=None):
    t, ff = da.shape
    d = w1.shape[0]
    tm, tk = min(FFN_ROWS, t), ff // 2

    def body(da_ref, db_ref, w1_ref, w3_ref, dh_ref):
        p = _dot_nt(da_ref[...], w1_ref[...]) + _dot_nt(db_ref[...], w3_ref[...])
        k = pl.program_id(2)

        @pl.when(k == 0)
        def _():
            dh_ref[...] = p

        @pl.when(k > 0)
        def _():
            dh_ref[...] += p

    act = pl.BlockSpec((tm, tk), lambda i, j, k: (i, k))
    wgt = pl.BlockSpec((tn, tk), lambda i, j, k: (j, k))
    out, moved = _call(
        body, (da, db, w1, w3), name=name, grid=(t // tm, d // tn, 2),
        in_specs=[act, act, wgt, wgt], out_specs=[pl.BlockSpec((tm, tn), lambda i, j, k: (i, j))],
        out_shape=[jax.ShapeDtypeStruct((t, d), F32)],
        semantics=("parallel", "parallel", "arbitrary"), comm=comm)
    return out[0] if comm is None else (out[0], moved)


def _loss_head(x_prev, f_prev, gate_prev, coef, target, name, tm=256):
    t, d = x_prev.shape
    steps = t // tm

    def body(xp_ref, fp_ref, gp_ref, tg_ref, dy_ref, df_ref, loss_ref, acc_ref):
        i = pl.program_id(0)
        e = xp_ref[...] + coef * gp_ref[...] * fp_ref[...] - tg_ref[...]
        dy = e * (1.0 / d)
        dy_ref[...] = dy
        df_ref[...] = (coef * gp_ref[...] * dy).astype(BF16)
        part = jnp.sum(e * e, axis=0, keepdims=True)

        @pl.when(i == 0)
        def _():
            acc_ref[...] = part

        @pl.when(i > 0)
        def _():
            acc_ref[...] += part

        @pl.when(i == steps - 1)
        def _():
            loss_ref[...] = jnp.sum(acc_ref[...], axis=1, keepdims=True) * (0.5 / d)

    row, vec = _row_spec(tm, d), _vec_spec(d)
    return _pc(
        body, name=name, grid=(steps,),
        in_specs=[row, row, vec, row], out_specs=[row, row, pl.BlockSpec((1, 1), lambda i: (0, 0))],
        out_shape=[jax.ShapeDtypeStruct((t, d), F32), jax.ShapeDtypeStruct((t, d), BF16),
                   jax.ShapeDtypeStruct((1, 1), F32)],
        scratch_shapes=[pltpu.VMEM((1, d), F32)],
        compiler_params=_params(("arbitrary",)),
    )(x_prev, f_prev, gate_prev, target)


def _split3(x):
    hi = x.astype(BF16)
    r1 = x - hi.astype(F32)
    mid = r1.astype(BF16)
    lo = (r1 - mid.astype(F32)).astype(BF16)
    return hi, mid, lo


def _select_matmul(a, onehot, name):
    m, n = a.shape[0], onehot.shape[1]

    def body(a_ref, oh_ref, o_ref):
        hi, mid, lo = _split3(a_ref[...])
        oh = oh_ref[...]
        o_ref[...] = (_dot(hi, oh) + _dot(mid, oh)) + _dot(lo, oh)

    return _pc(body, name=name, out_shape=jax.ShapeDtypeStruct((m, n), F32), compiler_params=_params())(a, onehot)


def _bucket_onehot():
    qi = np.arange(BLOCK)[:, None]
    kj = np.arange(2 * BLOCK)[None, :]
    dist = np.clip(qi + BLOCK - kj, 0, None)
    nf = np.maximum(dist, 1).astype(np.float32)
    large = 16 + (np.log(nf / np.float32(16)) / np.float32(math.log(128 / 16)) * np.float32(16)).astype(np.int32)
    bucket = np.where(dist < 16, dist, np.minimum(large, N_BUCKETS - 1)).reshape(-1)
    return (bucket[None, :] == np.arange(N_BUCKETS)[:, None]).astype(np.float32)


def _window_mask(n):
    row = lax.broadcasted_iota(jnp.int32, (GROUP * BLOCK, 2 * BLOCK), 0)
    qi = row & (BLOCK - 1)
    kj = lax.broadcasted_iota(jnp.int32, (GROUP * BLOCK, 2 * BLOCK), 1)
    return (kj > qi) & (kj <= qi + BLOCK) & ((kj >= BLOCK) | (n > 0))


def _kv_band(zc_ref, zp_ref, kvh):
    lo, hi = kvh * HEAD_DIM, (kvh + 1) * HEAD_DIM
    k_raw = jnp.concatenate([zp_ref[:, lo:hi], zc_ref[:, K_OFF + lo:K_OFF + hi]], axis=0)
    v_raw = jnp.concatenate([zp_ref[:, KV_WIDTH + lo:KV_WIDTH + hi], zc_ref[:, V_OFF + lo:V_OFF + hi]], axis=0)
    return k_raw, v_raw


def _stack_heads(ref, off):
    return jnp.concatenate([ref[:, off + g * HEAD_DIM:off + (g + 1) * HEAD_DIM] for g in range(GROUP)], axis=0)


def _unstack_heads(ref, off, stacked):
    for g in range(GROUP):
        ref[:, off + g * HEAD_DIM:off + (g + 1) * HEAD_DIM] = stacked[g * BLOCK:(g + 1) * BLOCK]


def _group_softmax(qn, kband, bias_ref, sink_ref, kvh, valid):
    bias = bias_ref[kvh * GROUP:(kvh + 1) * GROUP].reshape(GROUP * BLOCK, 2 * BLOCK)
    s = _dot_nt(qn, kband) * (HEAD_DIM ** -0.5) + bias
    s = jnp.where(valid, s, MASK_VALUE)
    sink = jnp.concatenate([jnp.full((BLOCK, 1), sink_ref[0, kvh * GROUP + g], F32) for g in range(GROUP)], axis=0)
    m = jnp.maximum(jnp.max(s, axis=-1, keepdims=True), sink)
    p = jnp.exp(s - m)
    esink = jnp.exp(sink - m)
    inv = 1.0 / (jnp.sum(p, axis=-1, keepdims=True) + esink)
    return p * inv, esink * inv, inv


def _mix_specs(nb, order):
    cur = pl.BlockSpec((BLOCK, IN_COLS), lambda n: (order(n), 0))
    prev = pl.BlockSpec((BLOCK, 2 * KV_WIDTH), lambda n: (jnp.maximum(order(n) - 1, 0), K_OFF // (2 * KV_WIDTH)))
    full = lambda shape: pl.BlockSpec(shape, lambda n: (0,) * len(shape))
    params = [full((A_HEADS, BLOCK, BLOCK)), full((BLOCK, A_HEADS)), full((A_HEADS, BLOCK)),
              full((1, HEAD_DIM)), full((1, HEAD_DIM)), pl.BlockSpec(memory_space=pltpu.SMEM),
              full((B_HEADS, BLOCK, 2 * BLOCK))]
    return cur, prev, params, full


def _mix_fwd(z, sw, sb_t, gv, gq, gk, sinks, bias, name, comm=None, relay_at=0.5):
    t = z.shape[0]
    nb = t // BLOCK

    def body(zc_ref, zp_ref, sw_ref, sbt_ref, gv_ref, gq_ref, gk_ref, sink_ref, bias_ref, y_ref):
        n = pl.program_id(0)
        ri = lax.broadcasted_iota(jnp.int32, (BLOCK, BLOCK), 0)
        ci = lax.broadcasted_iota(jnp.int32, (BLOCK, BLOCK), 1)
        tril = ri >= ci
        for h in range(A_HEADS):
            u = _gelu(zc_ref[:, h * BLOCK:(h + 1) * BLOCK])
            vv = _gelu(zc_ref[:, A_WIDTH + h * BLOCK:A_WIDTH + (h + 1) * BLOCK])
            vhat, _ = _rms(vv)
            vn = (vhat * gv_ref[h:h + 1, :]).astype(BF16)
            w = jnp.where(tril, sw_ref[h], 0.0).astype(BF16)
            mixed = _dot(w, vn) + sbt_ref[:, h:h + 1]
            y_ref[:, h * BLOCK:(h + 1) * BLOCK] = (u * mixed).astype(BF16)
        valid = _window_mask(n)
        for kvh in range(B_KV_HEADS):
            k_raw, v_raw = _kv_band(zc_ref, zp_ref, kvh)
            khat, _ = _rms(k_raw)
            kband = (khat * gk_ref[...]).astype(BF16)
            vband = v_raw.astype(BF16)
            qhat, _ = _rms(_stack_heads(zc_ref, Q_OFF + kvh * GROUP * HEAD_DIM))
            qn = (qhat * gq_ref[...]).astype(BF16)
            w, _, _ = _group_softmax(qn, kband, bias_ref, sink_ref, kvh, valid)
            o = _dot(w.astype(BF16), vband)
            _unstack_heads(y_ref, A_WIDTH + kvh * GROUP * HEAD_DIM, o.astype(BF16))

    cur, prev, params, _ = _mix_specs(nb, lambda n: n)
    out, moved = _call(
        body, (z, z, sw, sb_t, gv, gq, gk, sinks, bias), name=name, grid=(nb,), in_specs=[cur, prev] + params,
        out_specs=[pl.BlockSpec((BLOCK, A_WIDTH + B_WIDTH), lambda n: (n, 0))],
        out_shape=[jax.ShapeDtypeStruct((t, A_WIDTH + B_WIDTH), BF16)],
        semantics=("parallel",), comm=comm, relay_at=relay_at)
    return out[0] if comm is None else (out[0], moved)


def _mix_bwd(z, dy, sw, sb_t, gv, gq, gk, sinks, bias, name, comm=None):
    t = z.shape[0]
    nb = t // BLOCK

    def body(zc_ref, zp_ref, dy_ref, sw_ref, sbt_ref, gv_ref, gq_ref, gk_ref, sink_ref, bias_ref,
             dz_ref, dsw_ref, dsb_ref, dgv_ref, dgq_ref, dgk_ref, dsink_ref, dbias_ref, carry_ref):
        step = pl.program_id(0)
        n = nb - 1 - step

        @pl.when(step == 0)
        def _():
            dsw_ref[...] = jnp.zeros_like(dsw_ref)
            dsb_ref[...] = jnp.zeros_like(dsb_ref)
            dgv_ref[...] = jnp.zeros_like(dgv_ref)
            dgq_ref[...] = jnp.zeros_like(dgq_ref)
            dgk_ref[...] = jnp.zeros_like(dgk_ref)
            dsink_ref[...] = jnp.zeros_like(dsink_ref)
            dbias_ref[...] = jnp.zeros_like(dbias_ref)
            carry_ref[...] = jnp.zeros_like(carry_ref)

        ri = lax.broadcasted_iota(jnp.int32, (BLOCK, BLOCK), 0)
        ci = lax.broadcasted_iota(jnp.int32, (BLOCK, BLOCK), 1)
        tril = ri >= ci
        dsb = jnp.zeros((BLOCK, BLOCK), F32)
        for h in range(A_HEADS):
            uo, vo = h * BLOCK, A_WIDTH + h * BLOCK
            u_raw, v_raw = zc_ref[:, uo:uo + BLOCK], zc_ref[:, vo:vo + BLOCK]
            (u, du_raw), (vv, dv_raw) = _gelu_and_grad(u_raw), _gelu_and_grad(v_raw)
            vhat, r = _rms(vv)
            vn = (vhat * gv_ref[h:h + 1, :]).astype(BF16)
            w = jnp.where(tril, sw_ref[h], 0.0).astype(BF16)
            mixed = _dot(w, vn) + sbt_ref[:, h:h + 1]
            dya = dy_ref[:, uo:uo + BLOCK]
            dmixed = dya * u
            dm16 = dmixed.astype(BF16)
            dsw_ref[h] += jnp.where(tril, _dot_nt(dm16, vn), 0.0)
            dsb = dsb + jnp.where(ci == h, jnp.sum(dmixed, axis=1, keepdims=True), 0.0)
            dvn = _dot_tn(w, dm16)
            dgv_ref[h:h + 1, :] += jnp.sum(dvn * vhat, axis=0, keepdims=True)
            dvv = _rms_bwd(dvn * gv_ref[h:h + 1, :], vhat, r)
            dz_ref[:, uo:uo + BLOCK] = (dya * mixed * du_raw).astype(BF16)
            dz_ref[:, vo:vo + BLOCK] = (dvv * dv_raw).astype(BF16)
        dsb_ref[...] += dsb

        valid = _window_mask(n)
        lane = lax.broadcasted_iota(jnp.int32, (1, BLOCK), 1)
        dsink = jnp.zeros((1, BLOCK), F32)
        dgq = jnp.zeros((1, HEAD_DIM), F32)
        for kvh in range(B_KV_HEADS):
            ko, vo = K_OFF + kvh * HEAD_DIM, V_OFF + kvh * HEAD_DIM
            k_raw, v_raw = _kv_band(zc_ref, zp_ref, kvh)
            khat, kr = _rms(k_raw)
            kband = (khat * gk_ref[...]).astype(BF16)
            vband = v_raw.astype(BF16)
            qo = Q_OFF + kvh * GROUP * HEAD_DIM
            qhat, qr = _rms(_stack_heads(zc_ref, qo))
            qn = (qhat * gq_ref[...]).astype(BF16)
            w, wsink, _ = _group_softmax(qn, kband, bias_ref, sink_ref, kvh, valid)
            do = _stack_heads(dy_ref, A_WIDTH + kvh * GROUP * HEAD_DIM).astype(BF16)
            dvb = _dot_tn(w.astype(BF16), do)
            dw = _dot_nt(do, vband)
            rowdot = jnp.sum(w * dw, axis=-1, keepdims=True)
            ds = w * (dw - rowdot)
            dsink_rows = -wsink * rowdot
            for g in range(GROUP):
                head_sum = jnp.sum(dsink_rows[g * BLOCK:(g + 1) * BLOCK], axis=0, keepdims=True)
                dsink = dsink + jnp.where(lane == kvh * GROUP + g, head_sum, 0.0)
            dbias_ref[kvh * GROUP:(kvh + 1) * GROUP] += ds.reshape(GROUP, BLOCK, 2 * BLOCK)
            ds16 = (ds * (HEAD_DIM ** -0.5)).astype(BF16)
            dqn = _dot(ds16, kband)
            dkn = _dot_tn(ds16, qn)
            dgq = dgq + jnp.sum(dqn * qhat, axis=0, keepdims=True)
            _unstack_heads(dz_ref, qo, _rms_bwd(dqn * gq_ref[...], qhat, qr).astype(BF16))
            dgk_ref[...] += jnp.sum(dkn * khat, axis=0, keepdims=True)
            dk = _rms_bwd(dkn * gk_ref[...], khat, kr)
            co = kvh * HEAD_DIM
            dz_ref[:, ko:ko + HEAD_DIM] = (dk[BLOCK:] + carry_ref[:, co:co + HEAD_DIM]).astype(BF16)
            dz_ref[:, vo:vo + HEAD_DIM] = (dvb[BLOCK:] + carry_ref[:, KV_WIDTH + co:KV_WIDTH + co + HEAD_DIM]).astype(BF16)
            carry_ref[:, co:co + HEAD_DIM] = dk[:BLOCK]
            carry_ref[:, KV_WIDTH + co:KV_WIDTH + co + HEAD_DIM] = dvb[:BLOCK]
        dgq_ref[...] += dgq
        dsink_ref[...] += dsink

    order = lambda n: nb - 1 - n
    cur, prev, params, full = _mix_specs(nb, order)
    dy_spec = pl.BlockSpec((BLOCK, A_WIDTH + B_WIDTH), lambda n: (order(n), 0))
    out, moved = _call(
        body, (z, z, dy, sw, sb_t, gv, gq, gk, sinks, bias), name=name, grid=(nb,),
        in_specs=[cur, prev, dy_spec] + params,
        out_specs=[cur, full((A_HEADS, BLOCK, BLOCK)), full((BLOCK, BLOCK)), full((A_HEADS, BLOCK)),
                   full((1, HEAD_DIM)), full((1, HEAD_DIM)), full((1, BLOCK)), full((B_HEADS, BLOCK, 2 * BLOCK))],
        out_shape=[jax.ShapeDtypeStruct((t, IN_COLS), BF16), jax.ShapeDtypeStruct((A_HEADS, BLOCK, BLOCK), F32),
                   jax.ShapeDtypeStruct((BLOCK, BLOCK), F32), jax.ShapeDtypeStruct((A_HEADS, BLOCK), F32),
                   jax.ShapeDtypeStruct((1, HEAD_DIM), F32), jax.ShapeDtypeStruct((1, HEAD_DIM), F32),
                   jax.ShapeDtypeStruct((1, BLOCK), F32), jax.ShapeDtypeStruct((B_HEADS, BLOCK, 2 * BLOCK), F32)],
        scratch_shapes=[pltpu.VMEM((BLOCK, 2 * KV_WIDTH), F32)],
        semantics=("arbitrary",), comm=comm)
    return out if comm is None else (out, moved)


def _ada_fwd(c_all, w_ada, b_cols, name, tn=768):
    nb, d = c_all.shape
    cols = w_ada.shape[1]

    def body(c_ref, w_ref, b_ref, act_ref, mod_ref):
        cv = c_ref[...]
        act = cv * jax.nn.sigmoid(cv)
        act_ref[...] = act
        mod_ref[...] = _dot(act.astype(BF16), w_ref[...].astype(BF16)) + b_ref[...]

    return _pc(
        body, name=name, grid=(cols // tn,),
        in_specs=[pl.BlockSpec((nb, d), lambda j: (0, 0)), pl.BlockSpec((d, tn), lambda j: (0, j)),
                  pl.BlockSpec((1, tn), lambda j: (0, j))],
        out_specs=[pl.BlockSpec((nb, d), lambda j: (0, 0)), pl.BlockSpec((nb, tn), lambda j: (0, j))],
        out_shape=[jax.ShapeDtypeStruct((nb, d), F32), jax.ShapeDtypeStruct((nb, cols), F32)],
        compiler_params=_params(("arbitrary",)),
    )(c_all, w_ada, b_cols)


def _adamw(w, g, m, v):
    m = ADAM_B1 * m + (1.0 - ADAM_B1) * g
    v = ADAM_B2 * v + (1.0 - ADAM_B2) * (g * g)
    m_hat = m / (1.0 - ADAM_B1 ** ADAM_STEP)
    v_hat = v / (1.0 - ADAM_B2 ** ADAM_STEP)
    delta = -ADAM_LR * (m_hat / (jnp.sqrt(v_hat) + ADAM_EPS) + ADAM_WD * w)
    return delta, m, v


def _adam_outer(act_t, dmod, w, m, v, name, tr=256):
    rows, cols = w.shape
    nb = dmod.shape[0]

    def body(act_ref, dm_ref, w_ref, m_ref, v_ref, g_ref, d_ref, nm_ref, nv_ref):
        act = act_ref[...].astype(BF16).astype(F32)
        dm = dm_ref[...].astype(BF16).astype(F32)
        g = act[:, 0:1] * dm[0:1, :]
        for b in range(1, nb):
            g = g + act[:, b:b + 1] * dm[b:b + 1, :]
        g_ref[...] = g
        d_ref[...], nm_ref[...], nv_ref[...] = _adamw(w_ref[...], g, m_ref[...], v_ref[...])

    tile = pl.BlockSpec((tr, cols), lambda i: (i, 0))
    return _pc(
        body, name=name, grid=(rows // tr,),
        in_specs=[pl.BlockSpec((tr, nb), lambda i: (i, 0)), pl.BlockSpec((nb, cols), lambda i: (0, 0)), tile, tile, tile],
        out_specs=[tile] * 4, out_shape=[jax.ShapeDtypeStruct((rows, cols), F32)] * 4,
        compiler_params=_params(("parallel",)),
    )(act_t, dmod, w, m, v)


def _adam_reduce(parts, w, m, v, name, tr):
    rows, cols = w.shape

    def body(p_ref, w_ref, m_ref, v_ref, g_ref, d_ref, nm_ref, nv_ref):
        g = p_ref[0].astype(F32)
        for i in range(1, N_DEV):
            g = g + p_ref[i].astype(F32)
        g_ref[...] = g
        d_ref[...], nm_ref[...], nv_ref[...] = _adamw(w_ref[...], g, m_ref[...], v_ref[...])

    tile = pl.BlockSpec((tr, cols), lambda i: (i, 0))
    return _pc(
        body, name=name, grid=(rows // tr,),
        in_specs=[pl.BlockSpec((N_DEV, tr, cols), lambda i: (0, i, 0)), tile, tile, tile],
        out_specs=[tile] * 4, out_shape=[jax.ShapeDtypeStruct((rows, cols), F32)] * 4,
        compiler_params=_params(("parallel",)),
    )(parts, w, m, v)


def _ungather_cols(g):
    n, r, c = g.shape
    return jnp.transpose(g, (1, 0, 2)).reshape(r, n * c)


def _scatter_cols(w):
    r, c = w.shape
    return jnp.transpose(w.reshape(r, N_DEV, c // N_DEV), (1, 0, 2)).astype(BF16)


def _pad_rows(v):
    flat = v.reshape(-1)
    pad = (-flat.shape[0]) % (8 * BLOCK)
    return jnp.pad(flat, (0, pad)).reshape(-1, BLOCK)


def kernel(x, c, w_ada, b_ada, g_ffn1, w1_ffn1, w3_ffn1, w2_ffn1, g_mix, w_in, spatial_w, spatial_b, g_v, g_q, g_k, sinks, rel_bias, w_out, g_ffn2, w1_ffn2, w3_ffn2, w2_ffn2, loss_target, m_w_ada, m_b_ada, m_g_ffn1, m_w1_ffn1, m_w3_ffn1, m_w2_ffn1, m_g_mix, m_w_in, m_spatial_w, m_spatial_b, m_g_v, m_g_q, m_g_k, m_sinks, m_rel_bias, m_w_out, m_g_ffn2, m_w1_ffn2, m_w3_ffn2, m_w2_ffn2, v_w_ada, v_b_ada, v_g_ffn1, v_w1_ffn1, v_w3_ffn1, v_w2_ffn1, v_g_mix, v_w_in, v_spatial_w, v_spatial_b, v_g_v, v_g_q, v_g_k, v_sinks, v_rel_bias, v_w_out, v_g_ffn2, v_w1_ffn2, v_w3_ffn2, v_w2_ffn2):
    me = _linear(_position())
    d = D_MODEL
    x0 = x[0]
    target = loss_target[0]
    t = x0.shape[0]

    big = dict(w1a=w1_ffn1[0], w3a=w3_ffn1[0], w2a=w2_ffn1[0], w_in=w_in[0], w_out=w_out[0],
               w1b=w1_ffn2[0], w3b=w3_ffn2[0], w2b=w2_ffn2[0])
    col_sharded = ("w1a", "w3a", "w_in", "w1b", "w3b")
    full = {}

    def gather(keys):
        return _GatherTwoLevel([big[k].astype(BF16) for k in keys])

    def unpack(keys, gathered):
        for k, gth in zip(keys, gathered):
            full[k] = _ungather_cols(gth) if k in col_sharded else gth.reshape(-1, gth.shape[-1])

    def share(k):
        gk_ = grads[k]
        if k in col_sharded:
            return _scatter_cols(gk_)
        return gk_.reshape(N_DEV, gk_.shape[0] // N_DEV, gk_.shape[1]).astype(BF16)

    unpack(("w1a", "w3a"), _run_comm(gather(("w1a", "w3a")), "gather_ffn1"))

    c_all = _run_comm(_Direct([c], True), "gather_c")[0].reshape(N_DEV, d)
    b_cols = lax.dynamic_slice(b_ada, (0, me * MOD_COLS), (1, MOD_COLS))
    act_all, mod_cols = _ada_fwd(c_all, w_ada[0], b_cols, "ada_fwd")
    mod = _run_comm(_Direct([mod_cols.reshape(N_DEV, 1, MOD_COLS)], False), "scatter_mod")[0]
    mod = mod.reshape(N_MOD, 1, d)
    sh1, sc1, gt1, sh2, sc2, gt2, sh3, sc3, gt3 = [mod[i] for i in range(N_MOD)]

    tm = min(FFN_ROWS, t)
    h1 = _norm_fwd(x0, None, None, 0.0, g_ffn1, sc1, sh1, "norm1_fwd")
    (a1, b1, s1), gathered = _ffn_up(h1, full["w1a"], full["w3a"], "ffn1_up",
                                     comm=gather(("w2a", "w_in", "w_out")), relay_at=0.6)
    unpack(("w2a", "w_in", "w_out"), gathered)
    f1, gathered = _matmul(s1, full["w2a"], "nn", tm, 512, D_FF, F32, "ffn1_down", comm=gather(("w1b",)),
                           relay_at=0.7)
    unpack(("w1b",), gathered)
    x1, h2 = _norm_fwd(x0, f1, gt1, 0.5, g_mix, sc2, sh2, "norm2_fwd")
    z = _matmul(h2, full["w_in"], "nn", tm, IN_COLS // 2, d, F32, "mix_in_fwd")
    onehot = _bucket_onehot()
    bias = _select_matmul(rel_bias.T, jnp.asarray(onehot, BF16), "bias_table").reshape(B_HEADS, BLOCK, 2 * BLOCK)
    sb_t = spatial_b[0].T
    mix_params = (spatial_w[0], sb_t, g_v[0], g_q, g_k, sinks, bias)
    ycat, gathered = _mix_fwd(z, *mix_params, "mix_core_fwd", comm=gather(("w3b",)), relay_at=0.6)
    unpack(("w3b",), gathered)
    y = _matmul(ycat, full["w_out"], "nn", 512, d, d, F32, "mix_out_fwd")
    x2, h3 = _norm_fwd(x1, y, gt2, 1.0, g_ffn2, sc3, sh3, "norm3_fwd")
    (a2, b2, s2), gathered = _ffn_up(h3, full["w1b"], full["w3b"], "ffn2_up", comm=gather(("w2b",)), relay_at=0.6)
    unpack(("w2b",), gathered)
    f2 = _matmul(s2, full["w2b"], "nn", tm, 512, D_FF, F32, "ffn2_down")
    dx3, df2, loss_part = _loss_head(x2, f2, gt3, 0.5, target, "loss_head")
    loss = lax.psum(loss_part[0, 0], ("x", "y", "c"))

    grads, parts = {}, {}
    da, db = _ffn_dact(df2, a2, b2, full["w2b"], "ffn2_dact")
    dh3 = _ffn_dh(da, db, full["w1b"], full["w3b"], "ffn2_dh")
    tk = min(DW_TOKENS, t)
    grads["w1b"] = _matmul(h3, da, "tn", 1024, 512, tk, BF16, "ffn2_dw1")
    grads["w3b"] = _matmul(h3, db, "tn", 1024, 512, tk, BF16, "ffn2_dw3")
    grads["w2b"] = _matmul(s2, df2, "tn", 512, 1024, tk, BF16, "ffn2_dw2")
    dx2, dyg, sums3 = _norm_bwd(dh3, x2, dx3, f2, 0.5, g_ffn2, sc3, gt2, 1.0, "norm3_bwd")
    dycat = _matmul(dyg, full["w_out"], "nt", 512, d, d, F32, "mix_out_bwd")
    grads["w_out"] = _matmul(ycat, dyg, "tn", 1024, 512, tk, BF16, "mix_out_dw")
    mix_grads, moved = _mix_bwd(z, dycat, *mix_params, "mix_core_bwd",
                                comm=_Direct([share("w1b"), share("w3b")], False))
    parts["w1b"], parts["w3b"] = moved
    dz, d_sw, d_sb, d_gv, d_gq, d_gk, d_sink, d_bias = mix_grads
    d_rel = _select_matmul(d_bias.reshape(B_HEADS, BLOCK * 2 * BLOCK), jnp.asarray(onehot.T, BF16), "bias_table_bwd")
    dh2 = _matmul(dz, full["w_in"], "nt", tm, d, IN_COLS // 2, F32, "mix_in_bwd")
    grads["w_in"] = _matmul(h2, dz, "tn", 512, IN_COLS // 2, tk, BF16, "mix_in_dw")
    dx1, df1, sums2 = _norm_bwd(dh2, x1, dx2, y, 1.0, g_mix, sc2, gt1, 0.5, "norm2_bwd")
    (da, db), moved = _ffn_dact(df1, a1, b1, full["w2a"], "ffn1_dact", comm=_Direct([share("w2b")], False))
    parts["w2b"] = moved[0]
    dh1, moved = _ffn_dh(da, db, full["w1a"], full["w3a"], "ffn1_dh",
                         comm=_Direct([share("w_in"), share("w_out")], False))
    parts["w_in"], parts["w_out"] = moved
    grads["w1a"] = _matmul(h1, da, "tn", 1024, 512, tk, BF16, "ffn1_dw1")
    grads["w3a"], moved = _matmul(h1, db, "tn", 1024, 512, tk, BF16, "ffn1_dw3", comm=_Direct([share("w1a")], False))
    parts["w1a"] = moved[0]
    grads["w2a"], moved = _matmul(s1, df1, "tn", 512, 1024, tk, BF16, "ffn1_dw2", comm=_Direct([share("w3a")], False))
    parts["w3a"] = moved[0]
    (dx0, sums1), moved = _norm_bwd(dh1, x0, dx1, f1, 0.5, g_ffn1, sc1, None, 0.0, "norm1_bwd",
                                    comm=_Direct([share("w2a")], False))
    parts["w2a"] = moved[0]

    moments = dict(w1a=(m_w1_ffn1, v_w1_ffn1), w3a=(m_w3_ffn1, v_w3_ffn1), w2a=(m_w2_ffn1, v_w2_ffn1),
                   w_in=(m_w_in, v_w_in), w_out=(m_w_out, v_w_out),
                   w1b=(m_w1_ffn2, v_w1_ffn2), w3b=(m_w3_ffn2, v_w3_ffn2), w2b=(m_w2_ffn2, v_w2_ffn2))
    upd = {}
    for k in big:
        rows = big[k].shape[0]
        tr = 256 if rows % 256 == 0 else rows // 4
        upd[k] = [o[None] for o in _adam_reduce(parts[k], big[k], moments[k][0][0], moments[k][1][0], "adam_" + k, tr)]

    dmod = jnp.concatenate([sums1[1:2], sums1[0:1], sums1[3:4], sums2[1:2], sums2[0:1], sums2[3:4],
                            sums3[1:2], sums3[0:1], sums3[3:4]], axis=1)
    small = [("b_ada", dmod, b_ada, m_b_ada, v_b_ada),
             ("g_ffn1", sums1[2:3], g_ffn1, m_g_ffn1, v_g_ffn1),
             ("g_mix", sums2[2:3], g_mix, m_g_mix, v_g_mix),
             ("spatial_w", d_sw[None], spatial_w, m_spatial_w, v_spatial_w),
             ("spatial_b", d_sb[:, :A_HEADS].T[None], spatial_b, m_spatial_b, v_spatial_b),
             ("g_v", d_gv[None], g_v, m_g_v, v_g_v),
             ("g_q", d_gq, g_q, m_g_q, v_g_q),
             ("g_k", d_gk, g_k, m_g_k, v_g_k),
             ("sinks", d_sink[:, :B_HEADS], sinks, m_sinks, v_sinks),
             ("rel_bias", d_rel.T, rel_bias, m_rel_bias, v_rel_bias),
             ("g_ffn2", sums3[2:3], g_ffn2, m_g_ffn2, v_g_ffn2)]
    packed = [jnp.concatenate([_pad_rows(item[i]) for item in small], axis=0) for i in range(1, 5)]
    g_all = _run_comm(_Direct([packed[0]], True), "gather_small")[0]
    small_out = _adam_reduce(g_all, packed[1], packed[2], packed[3], "adam_small", packed[0].shape[0])
    off = 0
    for name, g_, w_, _, _ in small:
        n_rows = _pad_rows(w_).shape[0]
        upd[name] = [o[off:off + n_rows].reshape(-1)[:w_.size].reshape(w_.shape) for o in small_out]
        off += n_rows

    dmod_rows = g_all[:, :N_MOD * d // BLOCK, :].reshape(N_DEV, N_MOD * d)
    dmod_cols = lax.dynamic_slice(dmod_rows, (0, me * MOD_COLS), (N_DEV, MOD_COLS))
    upd["w_ada"] = [o[None] for o in _adam_outer(act_all.T, dmod_cols, w_ada[0], m_w_ada[0], v_w_ada[0], "adam_w_ada")]

    order = [("w_ada", "w_ada"), ("b_ada", "b_ada"), ("g_ffn1", "g_ffn1"), ("w1_ffn1", "w1a"), ("w3_ffn1", "w3a"),
             ("w2_ffn1", "w2a"), ("g_mix", "g_mix"), ("w_in", "w_in"), ("spatial_w", "spatial_w"),
             ("spatial_b", "spatial_b"), ("g_v", "g_v"), ("g_q", "g_q"), ("g_k", "g_k"), ("sinks", "sinks"),
             ("rel_bias", "rel_bias"), ("w_out", "w_out"), ("g_ffn2", "g_ffn2"), ("w1_ffn2", "w1b"),
             ("w3_ffn2", "w3b"), ("w2_ffn2", "w2b")]
    outs = [loss, dx0[None]]
    for i in range(4):
        outs += [upd[key][i] for _, key in order]
    return tuple(outs)
```

```python
import functools
import math

import numpy as np
import jax
import jax.numpy as jnp
from jax import lax
from jax.experimental import pallas as pl
from jax.experimental.pallas import tpu as pltpu

F32 = jnp.float32
BF16 = jnp.bfloat16

D_MODEL = 2048
D_FF = 5632
BLOCK = 128
A_HEADS = 8
A_WIDTH = 1024
B_HEADS = 16
B_KV_HEADS = 2
GROUP = B_HEADS // B_KV_HEADS
HEAD_DIM = 64
B_WIDTH = 1024
KV_WIDTH = 128
IN_COLS = 3328
Q_OFF = 2 * A_WIDTH
K_OFF = Q_OFF + B_WIDTH
V_OFF = K_OFF + KV_WIDTH
N_BUCKETS = 32
N_MOD = 9
EPS = 1e-6
N_DEV = 8
MOD_COLS = N_MOD * D_MODEL // N_DEV

ADAM_LR = 0.001
ADAM_B1 = 0.9
ADAM_B2 = 0.999
ADAM_EPS = 1e-08
ADAM_WD = 0.01
ADAM_STEP = 10

FFN_ROWS = 1024
DW_TOKENS = 4096
MASK_VALUE = -1e30
VMEM_LIMIT = 56 * 1024 * 1024
MESH_ID = pl.DeviceIdType.MESH
ANY = pl.BlockSpec(memory_space=pl.ANY)

_SQRT_HALF = 0.7071067811865476
_INV_SQRT_2PI = 0.3989422804014327


def _pc(body, **kw):
    return pl.pallas_call(body, **kw)


def _params(sem=None):
    return pltpu.CompilerParams(dimension_semantics=sem, vmem_limit_bytes=VMEM_LIMIT)


def _dot(a, b):
    return lax.dot_general(a, b, (((1,), (0,)), ((), ())), preferred_element_type=F32)


def _dot_nt(a, b):
    return lax.dot_general(a, b, (((1,), (1,)), ((), ())), preferred_element_type=F32)


def _dot_tn(a, b):
    return lax.dot_general(a, b, (((0,), (0,)), ((), ())), preferred_element_type=F32)


def _gelu(x):
    return 0.5 * x * (1.0 + lax.erf(x * _SQRT_HALF))


def _gelu_and_grad(x):
    cdf = 0.5 * (1.0 + lax.erf(x * _SQRT_HALF))
    return x * cdf, cdf + x * jnp.exp(-0.5 * x * x) * _INV_SQRT_2PI


def _rms(x):
    r = lax.rsqrt(jnp.mean(x * x, axis=-1, keepdims=True) + EPS)
    return x * r, r


def _rms_bwd(dy, y, r):
    return r * (dy - y * jnp.mean(dy * y, axis=-1, keepdims=True))


def _row_spec(tm, cols):
    return pl.BlockSpec((tm, cols), lambda i: (i, 0))


def _vec_spec(cols):
    return pl.BlockSpec((1, cols), lambda i: (0, 0))


def _position():
    x, y, c = lax.axis_index("x"), lax.axis_index("y"), lax.axis_index("c")
    return x, y, c


def _linear(p):
    return 4 * p[0] + 2 * p[1] + p[2]


class _Comm:
    PEER_COPIES = N_DEV - 1

    def __init__(self, arrs):
        self.arrs = list(arrs)
        n = len(self.arrs)
        self.scratch_shapes = [pltpu.SemaphoreType.DMA((self.PEER_COPIES * n,)),
                               pltpu.SemaphoreType.DMA((self.PEER_COPIES * n,)),
                               pltpu.SemaphoreType.DMA((n,))]

    def bind(self, srcs, dsts, sems):
        self.srcs, self.dsts = srcs, dsts
        self.send_sems, self.recv_sems, self.local_sems = sems
        x, y, c = _position()
        self.me, self.sibling, self.core = (x, y, c), (x, y, 1 - c), c
        self.chips = [(1 - x, y), (x, 1 - y), (1 - x, 1 - y)]
        self.peers = [(1 - x if k & 4 else x, 1 - y if k & 2 else y, 1 - c if k & 1 else c)
                      for k in range(1, N_DEV)]

    def relay(self):
        pass


class _GatherTwoLevel(_Comm):
    def __init__(self, arrs):
        super().__init__(arrs)
        self.out_shape = [jax.ShapeDtypeStruct((N_DEV,) + a.shape, a.dtype) for a in self.arrs]

    def _copy(self, a, k, block, to, from_input=False):
        rows = self.dsts[a].at[_linear(block)]
        return pltpu.make_async_remote_copy(
            src_ref=self.srcs[a] if from_input else rows, dst_ref=rows,
            send_sem=self.send_sems.at[self.PEER_COPIES * a + k], recv_sem=self.recv_sems.at[self.PEER_COPIES * a + k],
            device_id=to, device_id_type=MESH_ID)

    def _local(self, a):
        return pltpu.make_async_copy(self.srcs[a], self.dsts[a].at[_linear(self.me)], self.local_sems.at[a])

    def _first(self, a):
        return [self._copy(a, 0, self.me, self.sibling, True)] + [
            self._copy(a, 1 + j, self.me, (*chip, self.core), True) for j, chip in enumerate(self.chips)]

    def _passed(self, a):
        return [self._copy(a, 4 + j, (*chip, self.core), self.sibling) for j, chip in enumerate(self.chips)]

    def start(self):
        for a in range(len(self.arrs)):
            self._local(a).start()
            for cp in self._first(a):
                cp.start()

    def relay(self):
        for a in range(len(self.arrs)):
            for j, chip in enumerate(self.chips):
                self._copy(a, 1 + j, (*chip, self.core), self.me).wait_recv()
                self._passed(a)[j].start()

    def finish(self):
        for a in range(len(self.arrs)):
            self._copy(a, 0, self.sibling, self.me).wait_recv()
            for j, chip in enumerate(self.chips):
                self._copy(a, 4 + j, (*chip, 1 - self.core), self.me).wait_recv()
        for a in range(len(self.arrs)):
            for cp in self._first(a) + self._passed(a):
                cp.wait_send()
            self._local(a).wait()


class _Direct(_Comm):
    def __init__(self, arrs, broadcast):
        super().__init__(arrs)
        self.broadcast = broadcast
        self.out_shape = [jax.ShapeDtypeStruct(((N_DEV,) + a.shape) if broadcast else a.shape, a.dtype)
                          for a in self.arrs]

    def _outgoing(self, a, to):
        return self.srcs[a] if self.broadcast else self.srcs[a].at[_linear(to)]

    def _copy(self, a, k, sender, to):
        return pltpu.make_async_remote_copy(
            src_ref=self._outgoing(a, to), dst_ref=self.dsts[a].at[_linear(sender)],
            send_sem=self.send_sems.at[self.PEER_COPIES * a + k], recv_sem=self.recv_sems.at[self.PEER_COPIES * a + k],
            device_id=to, device_id_type=MESH_ID)

    def _local(self, a):
        return pltpu.make_async_copy(self._outgoing(a, self.me), self.dsts[a].at[_linear(self.me)],
                                     self.local_sems.at[a])

    def start(self):
        for a in range(len(self.arrs)):
            self._local(a).start()
            for k, peer in enumerate(self.peers):
                self._copy(a, k, self.me, peer).start()

    def finish(self):
        for a in range(len(self.arrs)):
            for k, peer in enumerate(self.peers):
                self._copy(a, k, peer, self.me).wait_recv()
        for a in range(len(self.arrs)):
            for k, peer in enumerate(self.peers):
                self._copy(a, k, self.me, peer).wait_send()
            self._local(a).wait()


def _run_comm(comm, name):
    n = len(comm.arrs)

    def body(*refs):
        comm.bind(refs[:n], refs[n:2 * n], refs[2 * n:])
        comm.start()
        comm.relay()
        comm.finish()

    return _pc(body, name=name, out_shape=comm.out_shape, in_specs=[ANY] * n, out_specs=[ANY] * n,
               scratch_shapes=comm.scratch_shapes)(*comm.arrs)


def _call(body, inputs, *, name, grid, in_specs, out_specs, out_shape, scratch_shapes=(), semantics,
          comm=None, relay_at=0.5):
    if comm is None:
        out = _pc(body, name=name, grid=grid, in_specs=in_specs, out_specs=out_specs, out_shape=out_shape,
                  scratch_shapes=list(scratch_shapes), compiler_params=_params(semantics))(*inputs)
        return list(out), []
    n_in, n_out, n_sc, k = len(in_specs), len(out_specs), len(scratch_shapes), len(comm.arrs)
    steps = math.prod(grid)
    relay_step = min(steps - 1, int(steps * relay_at))

    def carrier(*refs):
        ins, rest = refs[:n_in], refs[n_in:]
        csrc, rest = rest[:k], rest[k:]
        outs, rest = rest[:n_out], rest[n_out:]
        cdst, rest = rest[:k], rest[k:]
        scratch, csems = rest[:n_sc], rest[n_sc:]
        step = 0
        for axis, size in enumerate(grid):
            step = step * size + pl.program_id(axis)
        comm.bind(csrc, cdst, csems)
        pl.when(step == 0)(comm.start)
        pl.when(step == relay_step)(comm.relay)
        body(*ins, *outs, *scratch)
        pl.when(step == steps - 1)(comm.finish)

    out = _pc(carrier, name=name, grid=grid, in_specs=list(in_specs) + [ANY] * k,
              out_specs=list(out_specs) + [ANY] * k, out_shape=list(out_shape) + comm.out_shape,
              scratch_shapes=list(scratch_shapes) + comm.scratch_shapes,
              compiler_params=_params(("arbitrary",) * len(grid)))(*inputs, *comm.arrs)
    return list(out[:n_out]), list(out[n_out:])


def _matmul(a, b, mode, tm, tn, tk, out_dtype, name, comm=None, relay_at=0.5):
    if mode == "nn":
        (m, kk), nn = a.shape, b.shape[1]
        a_spec = pl.BlockSpec((tm, tk), lambda i, j, k: (i, k))
        b_spec = pl.BlockSpec((tk, tn), lambda i, j, k: (k, j))
        dot = _dot
    elif mode == "nt":
        (m, kk), nn = a.shape, b.shape[0]
        a_spec = pl.BlockSpec((tm, tk), lambda i, j, k: (i, k))
        b_spec = pl.BlockSpec((tn, tk), lambda i, j, k: (j, k))
        dot = _dot_nt
    else:
        (kk, m), nn = a.shape, b.shape[1]
        a_spec = pl.BlockSpec((tk, tm), lambda i, j, k: (k, i))
        b_spec = pl.BlockSpec((tk, tn), lambda i, j, k: (k, j))
        dot = _dot_tn
    assert m % tm == 0 and nn % tn == 0 and kk % tk == 0, (a.shape, b.shape, tm, tn, tk)
    nk = kk // tk
    narrow = nk > 1 and out_dtype != F32

    def body(a_ref, b_ref, o_ref, *acc):
        p = dot(a_ref[...], b_ref[...])
        if nk == 1:
            o_ref[...] = p.astype(o_ref.dtype)
            return
        k = pl.program_id(2)
        acc_ref = acc[0] if narrow else o_ref

        @pl.when(k == 0)
        def _():
            acc_ref[...] = p

        @pl.when((k > 0) & (k < nk - 1) if narrow else k > 0)
        def _():
            acc_ref[...] += p

        if narrow:
            @pl.when(k == nk - 1)
            def _():
                o_ref[...] = (acc_ref[...] + p).astype(o_ref.dtype)

    out, moved = _call(
        body, (a, b), name=name, grid=(m // tm, nn // tn, nk),
        in_specs=[a_spec, b_spec], out_specs=[pl.BlockSpec((tm, tn), lambda i, j, k: (i, j))],
        out_shape=[jax.ShapeDtypeStruct((m, nn), out_dtype)],
        scratch_shapes=[pltpu.VMEM((tm, tn), F32)] if narrow else [],
        semantics=("parallel", "parallel", "arbitrary"), comm=comm, relay_at=relay_at)
    return out[0] if comm is None else (out[0], moved)


def _norm_fwd(x_prev, f_prev, gate_prev, coef, g, sc, sh, name, tm=256):
    t, d = x_prev.shape
    residual = f_prev is not None

    def body(*refs):
        if residual:
            xp_ref, fp_ref, gp_ref, g_ref, sc_ref, sh_ref, x_ref, h_ref = refs
            x = xp_ref[...] + coef * gp_ref[...] * fp_ref[...]
            x_ref[...] = x
        else:
            xp_ref, g_ref, sc_ref, sh_ref, h_ref = refs
            x = xp_ref[...]
        y, _ = _rms(x)
        h_ref[...] = ((y * g_ref[...]) * (1.0 + sc_ref[...]) + sh_ref[...]).astype(BF16)

    row, vec = _row_spec(tm, d), _vec_spec(d)
    if residual:
        ins, in_specs = (x_prev, f_prev, gate_prev, g, sc, sh), [row, row, vec, vec, vec, vec]
        out_shape = [jax.ShapeDtypeStruct((t, d), F32), jax.ShapeDtypeStruct((t, d), BF16)]
        out_specs = [row, row]
    else:
        ins, in_specs = (x_prev, g, sc, sh), [row, vec, vec, vec]
        out_shape = [jax.ShapeDtypeStruct((t, d), BF16)]
        out_specs = [row]
    out = _pc(body, name=name, grid=(t // tm,), in_specs=in_specs, out_specs=out_specs, out_shape=out_shape,
              compiler_params=_params(("parallel",)))(*ins)
    return out if residual else out[0]


def _norm_bwd(dh, x, dxo, fo, coef_o, g, sc, gate_prev, coef_prev, name, tm=256, comm=None):
    t, d = x.shape
    with_prev = gate_prev is not None

    def body(*refs):
        if with_prev:
            dh_ref, x_ref, dxo_ref, fo_ref, g_ref, sc_ref, gp_ref, dx_ref, dfp_ref, sums_ref = refs
        else:
            dh_ref, x_ref, dxo_ref, fo_ref, g_ref, sc_ref, dx_ref, sums_ref = refs
        dh_v, dxo_v = dh_ref[...], dxo_ref[...]
        y, r = _rms(x_ref[...])
        n = y * g_ref[...]
        dn = dh_v * (1.0 + sc_ref[...])
        dx = dxo_v + _rms_bwd(dn * g_ref[...], y, r)
        dx_ref[...] = dx
        if with_prev:
            dfp_ref[...] = (coef_prev * gp_ref[...] * dx).astype(BF16)
        @pl.when(pl.program_id(0) == 0)
        def _():
            sums_ref[...] = jnp.zeros_like(sums_ref)

        sums_ref[0:1, :] += jnp.sum(dh_v * n, axis=0, keepdims=True)
        sums_ref[1:2, :] += jnp.sum(dh_v, axis=0, keepdims=True)
        sums_ref[2:3, :] += jnp.sum(dn * y, axis=0, keepdims=True)
        sums_ref[3:4, :] += jnp.sum(coef_o * dxo_v * fo_ref[...], axis=0, keepdims=True)

    row, vec = _row_spec(tm, d), _vec_spec(d)
    sums_spec = pl.BlockSpec((8, d), lambda i: (0, 0))
    ins, in_specs = [dh, x, dxo, fo, g, sc], [row, row, row, row, vec, vec]
    out_shape, out_specs = [jax.ShapeDtypeStruct((t, d), F32)], [row]
    if with_prev:
        ins.append(gate_prev)
        in_specs.append(vec)
        out_shape.append(jax.ShapeDtypeStruct((t, d), BF16))
        out_specs.append(row)
    out_shape.append(jax.ShapeDtypeStruct((8, d), F32))
    out_specs.append(sums_spec)
    out, moved = _call(body, ins, name=name, grid=(t // tm,), in_specs=in_specs, out_specs=out_specs,
                       out_shape=out_shape, semantics=("arbitrary",), comm=comm)
    return out if comm is None else (out, moved)


def _ffn_up(h, w1t, w3t, name, tf=512, comm=None, relay_at=0.5):
    t, d = h.shape
    ff = w1t.shape[0]
    tm = min(FFN_ROWS, t)

    def body(h_ref, w1_ref, w3_ref, ga_ref, gb_ref, s_ref):
        hv = h_ref[...]
        a = _dot_nt(hv, w1_ref[...])
        b = _dot_nt(hv, w3_ref[...])
        sg = jax.nn.sigmoid(a)
        sil = a * sg
        ga_ref[...] = (b * (sg * (1.0 + a * (1.0 - sg)))).astype(BF16)
        gb_ref[...] = sil.astype(BF16)
        s_ref[...] = (sil * b).astype(BF16)

    tile = pl.BlockSpec((tm, tf), lambda i, j: (i, j))
    cols = pl.BlockSpec((tf, d), lambda i, j: (j, 0))
    out, moved = _call(
        body, (h, w1t, w3t), name=name, grid=(t // tm, ff // tf),
        in_specs=[pl.BlockSpec((tm, d), lambda i, j: (i, 0)), cols, cols], out_specs=[tile, tile, tile],
        out_shape=[jax.ShapeDtypeStruct((t, ff), BF16)] * 3,
        semantics=("parallel", "parallel"), comm=comm, relay_at=relay_at)
    return out if comm is None else (out, moved)


def _ffn_dact(df, ga, gb, w2, name, tf=512, comm=None):
    t, d = df.shape
    ff = ga.shape[1]
    tm = min(FFN_ROWS, t)

    def body(df_ref, ga_ref, gb_ref, w2_ref, da_ref, db_ref):
        ds = _dot_nt(df_ref[...], w2_ref[...])
        da_ref[...] = (ds * ga_ref[...]).astype(BF16)
        db_ref[...] = (ds * gb_ref[...]).astype(BF16)

    tile = pl.BlockSpec((tm, tf), lambda i, j: (i, j))
    out, moved = _call(
        body, (df, ga, gb, w2), name=name, grid=(t // tm, ff // tf),
        in_specs=[pl.BlockSpec((tm, d), lambda i, j: (i, 0)), tile, tile, pl.BlockSpec((tf, d), lambda i, j: (j, 0))],
        out_specs=[tile, tile], out_shape=[jax.ShapeDtypeStruct((t, ff), BF16)] * 2,
        semantics=("parallel", "parallel"), comm=comm)
    return out if comm is None else (out, moved)


def _ffn_dh(da, db, w1t, w3t, name, tn=512, comm=None):
    t, ff = da.shape
    d = w1t.shape[1]
    tm, tk = min(FFN_ROWS, t), ff // 2

    def body(da_ref, db_ref, w1_ref, w3_ref, dh_ref):
        p = _dot(da_ref[...], w1_ref[...]) + _dot(db_ref[...], w3_ref[...])
        k = pl.program_id(2)

        @pl.when(k == 0)
        def _():
            dh_ref[...] = p

        @pl.when(k > 0)
        def _():
            dh_ref[...] += p

    act = pl.BlockSpec((tm, tk), lambda i, j, k: (i, k))
    wgt = pl.BlockSpec((tk, tn), lambda i, j, k: (k, j))
    out, moved = _call(
        body, (da, db, w1t, w3t), name=name, grid=(t // tm, d // tn, 2),
        in_specs=[act, act, wgt, wgt], out_specs=[pl.BlockSpec((tm, tn), lambda i, j, k: (i, j))],
        out_shape=[jax.ShapeDtypeStruct((t, d), F32)],
        semantics=("parallel", "parallel", "arbitrary"), comm=comm)
    return out[0] if comm is None else (out[0], moved)


def _loss_head(x_prev, f_prev, gate_prev, coef, target, name, tm=256):
    t, d = x_prev.shape
    steps = t // tm

    def body(xp_ref, fp_ref, gp_ref, tg_ref, dy_ref, df_ref, loss_ref, acc_ref):
        i = pl.program_id(0)
        e = xp_ref[...] + coef * gp_ref[...] * fp_ref[...] - tg_ref[...]
        dy = e * (1.0 / d)
        dy_ref[...] = dy
        df_ref[...] = (coef * gp_ref[...] * dy).astype(BF16)
        part = jnp.sum(e * e, axis=0, keepdims=True)

        @pl.when(i == 0)
        def _():
            acc_ref[...] = part

        @pl.when(i > 0)
        def _():
            acc_ref[...] += part

        @pl.when(i == steps - 1)
        def _():
            loss_ref[...] = jnp.sum(acc_ref[...], axis=1, keepdims=True) * (0.5 / d)

    row, vec = _row_spec(tm, d), _vec_spec(d)
    return _pc(
        body, name=name, grid=(steps,),
        in_specs=[row, row, vec, row], out_specs=[row, row, pl.BlockSpec((1, 1), lambda i: (0, 0))],
        out_shape=[jax.ShapeDtypeStruct((t, d), F32), jax.ShapeDtypeStruct((t, d), BF16),
                   jax.ShapeDtypeStruct((1, 1), F32)],
        scratch_shapes=[pltpu.VMEM((1, d), F32)],
        compiler_params=_params(("arbitrary",)),
    )(x_prev, f_prev, gate_prev, target)


def _split3(x):
    hi = x.astype(BF16)
    r1 = x - hi.astype(F32)
    mid = r1.astype(BF16)
    lo = (r1 - mid.astype(F32)).astype(BF16)
    return hi, mid, lo


def _select_matmul(a, onehot, name):
    m, n = a.shape[0], onehot.shape[1]

    def body(a_ref, oh_ref, o_ref):
        hi, mid, lo = _split3(a_ref[...])
        oh = oh_ref[...]
        o_ref[...] = (_dot(hi, oh) + _dot(mid, oh)) + _dot(lo, oh)

    return _pc(body, name=name, out_shape=jax.ShapeDtypeStruct((m, n), F32), compiler_params=_params())(a, onehot)


def _bucket_onehot():
    qi = np.arange(BLOCK)[:, None]
    kj = np.arange(2 * BLOCK)[None, :]
    dist = np.clip(qi + BLOCK - kj, 0, None)
    nf = np.maximum(dist, 1).astype(np.float32)
    large = 16 + (np.log(nf / np.float32(16)) / np.float32(math.log(128 / 16)) * np.float32(16)).astype(np.int32)
    bucket = np.where(dist < 16, dist, np.minimum(large, N_BUCKETS - 1)).reshape(-1)
    return (bucket[None, :] == np.arange(N_BUCKETS)[:, None]).astype(np.float32)


def _window_mask(n):
    row = lax.broadcasted_iota(jnp.int32, (GROUP * BLOCK, 2 * BLOCK), 0)
    qi = row & (BLOCK - 1)
    kj = lax.broadcasted_iota(jnp.int32, (GROUP * BLOCK, 2 * BLOCK), 1)
    return (kj > qi) & (kj <= qi + BLOCK) & ((kj >= BLOCK) | (n > 0))


def _kv_band(zc_ref, zp_ref, kvh):
    lo, hi = kvh * HEAD_DIM, (kvh + 1) * HEAD_DIM
    k_raw = jnp.concatenate([zp_ref[:, lo:hi], zc_ref[:, K_OFF + lo:K_OFF + hi]], axis=0)
    v_raw = jnp.concatenate([zp_ref[:, KV_WIDTH + lo:KV_WIDTH + hi], zc_ref[:, V_OFF + lo:V_OFF + hi]], axis=0)
    return k_raw, v_raw


def _stack_heads(ref, off):
    return jnp.concatenate([ref[:, off + g * HEAD_DIM:off + (g + 1) * HEAD_DIM] for g in range(GROUP)], axis=0)


def _unstack_heads(ref, off, stacked):
    for g in range(GROUP):
        ref[:, off + g * HEAD_DIM:off + (g + 1) * HEAD_DIM] = stacked[g * BLOCK:(g + 1) * BLOCK]


def _group_softmax(qn, kband, bias_ref, sink_ref, kvh, valid):
    bias = bias_ref[kvh * GROUP:(kvh + 1) * GROUP].reshape(GROUP * BLOCK, 2 * BLOCK)
    s = _dot_nt(qn, kband) * (HEAD_DIM ** -0.5) + bias
    s = jnp.where(valid, s, MASK_VALUE)
    sink = jnp.concatenate([jnp.full((BLOCK, 1), sink_ref[0, kvh * GROUP + g], F32) for g in range(GROUP)], axis=0)
    m = jnp.maximum(jnp.max(s, axis=-1, keepdims=True), sink)
    p = jnp.exp(s - m)
    esink = jnp.exp(sink - m)
    inv = 1.0 / (jnp.sum(p, axis=-1, keepdims=True) + esink)
    return p * inv, esink * inv, inv


def _mix_specs(nb, order):
    cur = pl.BlockSpec((BLOCK, IN_COLS), lambda n: (order(n), 0))
    prev = pl.BlockSpec((BLOCK, 2 * KV_WIDTH), lambda n: (jnp.maximum(order(n) - 1, 0), K_OFF // (2 * KV_WIDTH)))
    full = lambda shape: pl.BlockSpec(shape, lambda n: (0,) * len(shape))
    params = [full((A_HEADS, BLOCK, BLOCK)), full((BLOCK, A_HEADS)), full((A_HEADS, BLOCK)),
              full((1, HEAD_DIM)), full((1, HEAD_DIM)), pl.BlockSpec(memory_space=pltpu.SMEM),
              full((B_HEADS, BLOCK, 2 * BLOCK))]
    return cur, prev, params, full


def _mix_fwd(z, sw, sb_t, gv, gq, gk, sinks, bias, name, comm=None, relay_at=0.5):
    t = z.shape[0]
    nb = t // BLOCK

    def body(zc_ref, zp_ref, sw_ref, sbt_ref, gv_ref, gq_ref, gk_ref, sink_ref, bias_ref, y_ref):
        n = pl.program_id(0)
        ri = lax.broadcasted_iota(jnp.int32, (BLOCK, BLOCK), 0)
        ci = lax.broadcasted_iota(jnp.int32, (BLOCK, BLOCK), 1)
        tril = ri >= ci
        for h in range(A_HEADS):
            u = _gelu(zc_ref[:, h * BLOCK:(h + 1) * BLOCK])
            vv = _gelu(zc_ref[:, A_WIDTH + h * BLOCK:A_WIDTH + (h + 1) * BLOCK])
            vhat, _ = _rms(vv)
            vn = (vhat * gv_ref[h:h + 1, :]).astype(BF16)
            w = jnp.where(tril, sw_ref[h], 0.0).astype(BF16)
            mixed = _dot(w, vn) + sbt_ref[:, h:h + 1]
            y_ref[:, h * BLOCK:(h + 1) * BLOCK] = (u * mixed).astype(BF16)
        valid = _window_mask(n)
        for kvh in range(B_KV_HEADS):
            k_raw, v_raw = _kv_band(zc_ref, zp_ref, kvh)
            khat, _ = _rms(k_raw)
            kband = (khat * gk_ref[...]).astype(BF16)
            vband = v_raw.astype(BF16)
            qhat, _ = _rms(_stack_heads(zc_ref, Q_OFF + kvh * GROUP * HEAD_DIM))
            qn = (qhat * gq_ref[...]).astype(BF16)
            w, _, _ = _group_softmax(qn, kband, bias_ref, sink_ref, kvh, valid)
            o = _dot(w.astype(BF16), vband)
            _unstack_heads(y_ref, A_WIDTH + kvh * GROUP * HEAD_DIM, o.astype(BF16))

    cur, prev, params, _ = _mix_specs(nb, lambda n: n)
    out, moved = _call(
        body, (z, z, sw, sb_t, gv, gq, gk, sinks, bias), name=name, grid=(nb,), in_specs=[cur, prev] + params,
        out_specs=[pl.BlockSpec((BLOCK, A_WIDTH + B_WIDTH), lambda n: (n, 0))],
        out_shape=[jax.ShapeDtypeStruct((t, A_WIDTH + B_WIDTH), BF16)],
        semantics=("parallel",), comm=comm, relay_at=relay_at)
    return out[0] if comm is None else (out[0], moved)


def _mix_bwd(z, dy, sw, sb_t, gv, gq, gk, sinks, bias, name, comm=None):
    t = z.shape[0]
    nb = t // BLOCK

    def body(zc_ref, zp_ref, dy_ref, sw_ref, sbt_ref, gv_ref, gq_ref, gk_ref, sink_ref, bias_ref,
             dz_ref, dsw_ref, dsb_ref, dgv_ref, dgq_ref, dgk_ref, dsink_ref, dbias_ref, carry_ref):
        step = pl.program_id(0)
        n = nb - 1 - step

        @pl.when(step == 0)
        def _():
            dsw_ref[...] = jnp.zeros_like(dsw_ref)
            dsb_ref[...] = jnp.zeros_like(dsb_ref)
            dgv_ref[...] = jnp.zeros_like(dgv_ref)
            dgq_ref[...] = jnp.zeros_like(dgq_ref)
            dgk_ref[...] = jnp.zeros_like(dgk_ref)
            dsink_ref[...] = jnp.zeros_like(dsink_ref)
            dbias_ref[...] = jnp.zeros_like(dbias_ref)
            carry_ref[...] = jnp.zeros_like(carry_ref)

        ri = lax.broadcasted_iota(jnp.int32, (BLOCK, BLOCK), 0)
        ci = lax.broadcasted_iota(jnp.int32, (BLOCK, BLOCK), 1)
        tril = ri >= ci
        dsb = jnp.zeros((BLOCK, BLOCK), F32)
        for h in range(A_HEADS):
            uo, vo = h * BLOCK, A_WIDTH + h * BLOCK
            u_raw, v_raw = zc_ref[:, uo:uo + BLOCK], zc_ref[:, vo:vo + BLOCK]
            (u, du_raw), (vv, dv_raw) = _gelu_and_grad(u_raw), _gelu_and_grad(v_raw)
            vhat, r = _rms(vv)
            vn = (vhat * gv_ref[h:h + 1, :]).astype(BF16)
            w = jnp.where(tril, sw_ref[h], 0.0).astype(BF16)
            mixed = _dot(w, vn) + sbt_ref[:, h:h + 1]
            dya = dy_ref[:, uo:uo + BLOCK]
            dmixed = dya * u
            dm16 = dmixed.astype(BF16)
            dsw_ref[h] += jnp.where(tril, _dot_nt(dm16, vn), 0.0)
            dsb = dsb + jnp.where(ci == h, jnp.sum(dmixed, axis=1, keepdims=True), 0.0)
            dvn = _dot_tn(w, dm16)
            dgv_ref[h:h + 1, :] += jnp.sum(dvn * vhat, axis=0, keepdims=True)
            dvv = _rms_bwd(dvn * gv_ref[h:h + 1, :], vhat, r)
            dz_ref[:, uo:uo + BLOCK] = (dya * mixed * du_raw).astype(BF16)
            dz_ref[:, vo:vo + BLOCK] = (dvv * dv_raw).astype(BF16)
        dsb_ref[...] += dsb

        valid = _window_mask(n)
        lane = lax.broadcasted_iota(jnp.int32, (1, BLOCK), 1)
        dsink = jnp.zeros((1, BLOCK), F32)
        dgq = jnp.zeros((1, HEAD_DIM), F32)
        for kvh in range(B_KV_HEADS):
            ko, vo = K_OFF + kvh * HEAD_DIM, V_OFF + kvh * HEAD_DIM
            k_raw, v_raw = _kv_band(zc_ref, zp_ref, kvh)
            khat, kr = _rms(k_raw)
            kband = (khat * gk_ref[...]).astype(BF16)
            vband = v_raw.astype(BF16)
            qo = Q_OFF + kvh * GROUP * HEAD_DIM
            qhat, qr = _rms(_stack_heads(zc_ref, qo))
            qn = (qhat * gq_ref[...]).astype(BF16)
            w, wsink, _ = _group_softmax(qn, kband, bias_ref, sink_ref, kvh, valid)
            do = _stack_heads(dy_ref, A_WIDTH + kvh * GROUP * HEAD_DIM).astype(BF16)
            dvb = _dot_tn(w.astype(BF16), do)
            dw = _dot_nt(do, vband)
            rowdot = jnp.sum(w * dw, axis=-1, keepdims=True)
            ds = w * (dw - rowdot)
            dsink_rows = -wsink * rowdot
            for g in range(GROUP):
                head_sum = jnp.sum(dsink_rows[g * BLOCK:(g + 1) * BLOCK], axis=0, keepdims=True)
                dsink = dsink + jnp.where(lane == kvh * GROUP + g, head_sum, 0.0)
            dbias_ref[kvh * GROUP:(kvh + 1) * GROUP] += ds.reshape(GROUP, BLOCK, 2 * BLOCK)
            ds16 = (ds * (HEAD_DIM ** -0.5)).astype(BF16)
            dqn = _dot(ds16, kband)
            dkn = _dot_tn(ds16, qn)
            dgq = dgq + jnp.sum(dqn * qhat, axis=0, keepdims=True)
            _unstack_heads(dz_ref, qo, _rms_bwd(dqn * gq_ref[...], qhat, qr).astype(BF16))
            dgk_ref[...] += jnp.sum(dkn * khat, axis=0, keepdims=True)
            dk = _rms_bwd(dkn * gk_ref[...], khat, kr)
            co = kvh * HEAD_DIM
            dz_ref[:, ko:ko + HEAD_DIM] = (dk[BLOCK:] + carry_ref[:, co:co + HEAD_DIM]).astype(BF16)
            dz_ref[:, vo:vo + HEAD_DIM] = (dvb[BLOCK:] + carry_ref[:, KV_WIDTH + co:KV_WIDTH + co + HEAD_DIM]).astype(BF16)
            carry_ref[:, co:co + HEAD_DIM] = dk[:BLOCK]
            carry_ref[:, KV_WIDTH + co:KV_WIDTH + co + HEAD_DIM] = dvb[:BLOCK]
        dgq_ref[...] += dgq
        dsink_ref[...] += dsink

    order = lambda n: nb - 1 - n
    cur, prev, params, full = _mix_specs(nb, order)
    dy_spec = pl.BlockSpec((BLOCK, A_WIDTH + B_WIDTH), lambda n: (order(n), 0))
    out, moved = _call(
        body, (z, z, dy, sw, sb_t, gv, gq, gk, sinks, bias), name=name, grid=(nb,),
        in_specs=[cur, prev, dy_spec] + params,
        out_specs=[cur, full((A_HEADS, BLOCK, BLOCK)), full((BLOCK, BLOCK)), full((A_HEADS, BLOCK)),
                   full((1, HEAD_DIM)), full((1, HEAD_DIM)), full((1, BLOCK)), full((B_HEADS, BLOCK, 2 * BLOCK))],
        out_shape=[jax.ShapeDtypeStruct((t, IN_COLS), BF16), jax.ShapeDtypeStruct((A_HEADS, BLOCK, BLOCK), F32),
                   jax.ShapeDtypeStruct((BLOCK, BLOCK), F32), jax.ShapeDtypeStruct((A_HEADS, BLOCK), F32),
                   jax.ShapeDtypeStruct((1, HEAD_DIM), F32), jax.ShapeDtypeStruct((1, HEAD_DIM), F32),
                   jax.ShapeDtypeStruct((1, BLOCK), F32), jax.ShapeDtypeStruct((B_HEADS, BLOCK, 2 * BLOCK), F32)],
        scratch_shapes=[pltpu.VMEM((BLOCK, 2 * KV_WIDTH), F32)],
        semantics=("arbitrary",), comm=comm)
    return out if comm is None else (out, moved)


def _ada_fwd(c_all, w_ada, b_cols, name, tn=768):
    nb, d = c_all.shape
    cols = w_ada.shape[1]

    def body(c_ref, w_ref, b_ref, act_ref, mod_ref):
        cv = c_ref[...]
        act = cv * jax.nn.sigmoid(cv)
        act_ref[...] = act
        mod_ref[...] = _dot(act.astype(BF16), w_ref[...].astype(BF16)) + b_ref[...]

    return _pc(
        body, name=name, grid=(cols // tn,),
        in_specs=[pl.BlockSpec((nb, d), lambda j: (0, 0)), pl.BlockSpec((d, tn), lambda j: (0, j)),
                  pl.BlockSpec((1, tn), lambda j: (0, j))],
        out_specs=[pl.BlockSpec((nb, d), lambda j: (0, 0)), pl.BlockSpec((nb, tn), lambda j: (0, j))],
        out_shape=[jax.ShapeDtypeStruct((nb, d), F32), jax.ShapeDtypeStruct((nb, cols), F32)],
        compiler_params=_params(("arbitrary",)),
    )(c_all, w_ada, b_cols)


def _adamw(w, g, m, v):
    m = ADAM_B1 * m + (1.0 - ADAM_B1) * g
    v = ADAM_B2 * v + (1.0 - ADAM_B2) * (g * g)
    m_hat = m / (1.0 - ADAM_B1 ** ADAM_STEP)
    v_hat = v / (1.0 - ADAM_B2 ** ADAM_STEP)
    delta = -ADAM_LR * (m_hat / (jnp.sqrt(v_hat) + ADAM_EPS) + ADAM_WD * w)
    return delta, m, v


def _adam_outer(act_t, dmod, w, m, v, name, tr=256):
    rows, cols = w.shape
    nb = dmod.shape[0]

    def body(act_ref, dm_ref, w_ref, m_ref, v_ref, g_ref, d_ref, nm_ref, nv_ref):
        act = act_ref[...].astype(BF16).astype(F32)
        dm = dm_ref[...].astype(BF16).astype(F32)
        g = act[:, 0:1] * dm[0:1, :]
        for b in range(1, nb):
            g = g + act[:, b:b + 1] * dm[b:b + 1, :]
        g_ref[...] = g
        d_ref[...], nm_ref[...], nv_ref[...] = _adamw(w_ref[...], g, m_ref[...], v_ref[...])

    tile = pl.BlockSpec((tr, cols), lambda i: (i, 0))
    return _pc(
        body, name=name, grid=(rows // tr,),
        in_specs=[pl.BlockSpec((tr, nb), lambda i: (i, 0)), pl.BlockSpec((nb, cols), lambda i: (0, 0)), tile, tile, tile],
        out_specs=[tile] * 4, out_shape=[jax.ShapeDtypeStruct((rows, cols), F32)] * 4,
        compiler_params=_params(("parallel",)),
    )(act_t, dmod, w, m, v)


def _adam_reduce(parts, w, m, v, name, tr):
    rows, cols = w.shape

    def body(p_ref, w_ref, m_ref, v_ref, g_ref, d_ref, nm_ref, nv_ref):
        g = p_ref[0].astype(F32)
        for i in range(1, N_DEV):
            g = g + p_ref[i].astype(F32)
        g_ref[...] = g
        d_ref[...], nm_ref[...], nv_ref[...] = _adamw(w_ref[...], g, m_ref[...], v_ref[...])

    tile = pl.BlockSpec((tr, cols), lambda i: (i, 0))
    return _pc(
        body, name=name, grid=(rows // tr,),
        in_specs=[pl.BlockSpec((N_DEV, tr, cols), lambda i: (0, i, 0)), tile, tile, tile],
        out_specs=[tile] * 4, out_shape=[jax.ShapeDtypeStruct((rows, cols), F32)] * 4,
        compiler_params=_params(("parallel",)),
    )(parts, w, m, v)


def _scatter_cols(w):
    r, c = w.shape
    return jnp.transpose(w.reshape(r, N_DEV, c // N_DEV), (1, 0, 2)).astype(BF16)


def _pad_rows(v):
    flat = v.reshape(-1)
    pad = (-flat.shape[0]) % (8 * BLOCK)
    return jnp.pad(flat, (0, pad)).reshape(-1, BLOCK)


def kernel(x, c, w_ada, b_ada, g_ffn1, w1_ffn1, w3_ffn1, w2_ffn1, g_mix, w_in, spatial_w, spatial_b, g_v, g_q, g_k, sinks, rel_bias, w_out, g_ffn2, w1_ffn2, w3_ffn2, w2_ffn2, loss_target, m_w_ada, m_b_ada, m_g_ffn1, m_w1_ffn1, m_w3_ffn1, m_w2_ffn1, m_g_mix, m_w_in, m_spatial_w, m_spatial_b, m_g_v, m_g_q, m_g_k, m_sinks, m_rel_bias, m_w_out, m_g_ffn2, m_w1_ffn2, m_w3_ffn2, m_w2_ffn2, v_w_ada, v_b_ada, v_g_ffn1, v_w1_ffn1, v_w3_ffn1, v_w2_ffn1, v_g_mix, v_w_in, v_spatial_w, v_spatial_b, v_g_v, v_g_q, v_g_k, v_sinks, v_rel_bias, v_w_out, v_g_ffn2, v_w1_ffn2, v_w3_ffn2, v_w2_ffn2):
    me = _linear(_position())
    d = D_MODEL
    x0 = x[0]
    target = loss_target[0]
    t = x0.shape[0]

    big = dict(w1a=w1_ffn1[0], w3a=w3_ffn1[0], w2a=w2_ffn1[0], w_in=w_in[0], w_out=w_out[0],
               w1b=w1_ffn2[0], w3b=w3_ffn2[0], w2b=w2_ffn2[0])
    col_sharded = ("w1a", "w3a", "w_in", "w1b", "w3b")
    full = {}

    def gather(keys):
        return _GatherTwoLevel([(big[k].T if k in col_sharded else big[k]).astype(BF16) for k in keys])

    def unpack(keys, gathered):
        for k, gth in zip(keys, gathered):
            full[k] = gth.reshape(-1, gth.shape[-1])

    def share(k):
        gk_ = grads[k]
        if k in col_sharded:
            return _scatter_cols(gk_)
        return gk_.reshape(N_DEV, gk_.shape[0] // N_DEV, gk_.shape[1]).astype(BF16)

    unpack(("w1a", "w3a"), _run_comm(gather(("w1a", "w3a")), "gather_ffn1"))

    c_all = _run_comm(_Direct([c], True), "gather_c")[0].reshape(N_DEV, d)
    b_cols = lax.dynamic_slice(b_ada, (0, me * MOD_COLS), (1, MOD_COLS))
    act_all, mod_cols = _ada_fwd(c_all, w_ada[0], b_cols, "ada_fwd")
    mod = _run_comm(_Direct([mod_cols.reshape(N_DEV, 1, MOD_COLS)], False), "scatter_mod")[0]
    mod = mod.reshape(N_MOD, 1, d)
    sh1, sc1, gt1, sh2, sc2, gt2, sh3, sc3, gt3 = [mod[i] for i in range(N_MOD)]

    tm = min(FFN_ROWS, t)
    h1 = _norm_fwd(x0, None, None, 0.0, g_ffn1, sc1, sh1, "norm1_fwd")
    (a1, b1, s1), gathered = _ffn_up(h1, full["w1a"], full["w3a"], "ffn1_up",
                                     comm=gather(("w2a", "w_in", "w_out")), relay_at=0.6)
    unpack(("w2a", "w_in", "w_out"), gathered)
    f1, gathered = _matmul(s1, full["w2a"], "nn", tm, 512, D_FF, F32, "ffn1_down", comm=gather(("w1b",)),
                           relay_at=0.7)
    unpack(("w1b",), gathered)
    x1, h2 = _norm_fwd(x0, f1, gt1, 0.5, g_mix, sc2, sh2, "norm2_fwd")
    z = _matmul(h2, full["w_in"], "nt", tm, IN_COLS // 2, d, F32, "mix_in_fwd")
    onehot = _bucket_onehot()
    bias = _select_matmul(rel_bias.T, jnp.asarray(onehot, BF16), "bias_table").reshape(B_HEADS, BLOCK, 2 * BLOCK)
    sb_t = spatial_b[0].T
    mix_params = (spatial_w[0], sb_t, g_v[0], g_q, g_k, sinks, bias)
    ycat, gathered = _mix_fwd(z, *mix_params, "mix_core_fwd", comm=gather(("w3b",)), relay_at=0.6)
    unpack(("w3b",), gathered)
    y = _matmul(ycat, full["w_out"], "nn", 512, d, d, F32, "mix_out_fwd")
    x2, h3 = _norm_fwd(x1, y, gt2, 1.0, g_ffn2, sc3, sh3, "norm3_fwd")
    (a2, b2, s2), gathered = _ffn_up(h3, full["w1b"], full["w3b"], "ffn2_up", comm=gather(("w2b",)), relay_at=0.6)
    unpack(("w2b",), gathered)
    f2 = _matmul(s2, full["w2b"], "nn", tm, 512, D_FF, F32, "ffn2_down")
    dx3, df2, loss_part = _loss_head(x2, f2, gt3, 0.5, target, "loss_head")
    loss = lax.psum(loss_part[0, 0], ("x", "y", "c"))

    grads, parts = {}, {}
    da, db = _ffn_dact(df2, a2, b2, full["w2b"], "ffn2_dact")
    dh3 = _ffn_dh(da, db, full["w1b"], full["w3b"], "ffn2_dh")
    tk = min(DW_TOKENS, t)
    grads["w1b"] = _matmul(h3, da, "tn", 1024, 512, tk, BF16, "ffn2_dw1")
    grads["w3b"] = _matmul(h3, db, "tn", 1024, 512, tk, BF16, "ffn2_dw3")
    grads["w2b"] = _matmul(s2, df2, "tn", 512, 1024, tk, BF16, "ffn2_dw2")
    dx2, dyg, sums3 = _norm_bwd(dh3, x2, dx3, f2, 0.5, g_ffn2, sc3, gt2, 1.0, "norm3_bwd")
    dycat = _matmul(dyg, full["w_out"], "nt", 512, d, d, F32, "mix_out_bwd")
    grads["w_out"] = _matmul(ycat, dyg, "tn", 1024, 512, tk, BF16, "mix_out_dw")
    mix_grads, moved = _mix_bwd(z, dycat, *mix_params, "mix_core_bwd",
                                comm=_Direct([share("w1b"), share("w3b")], False))
    parts["w1b"], parts["w3b"] = moved
    dz, d_sw, d_sb, d_gv, d_gq, d_gk, d_sink, d_bias = mix_grads
    d_rel = _select_matmul(d_bias.reshape(B_HEADS, BLOCK * 2 * BLOCK), jnp.asarray(onehot.T, BF16), "bias_table_bwd")
    dh2 = _matmul(dz, full["w_in"], "nn", tm, d, IN_COLS // 2, F32, "mix_in_bwd")
    grads["w_in"] = _matmul(h2, dz, "tn", 512, IN_COLS // 2, tk, BF16, "mix_in_dw")
    dx1, df1, sums2 = _norm_bwd(dh2, x1, dx2, y, 1.0, g_mix, sc2, gt1, 0.5, "norm2_bwd")
    (da, db), moved = _ffn_dact(df1, a1, b1, full["w2a"], "ffn1_dact", comm=_Direct([share("w2b")], False))
    parts["w2b"] = moved[0]
    dh1, moved = _ffn_dh(da, db, full["w1a"], full["w3a"], "ffn1_dh",
                         comm=_Direct([share("w_in"), share("w_out")], False))
    parts["w_in"], parts["w_out"] = moved
    grads["w1a"] = _matmul(h1, da, "tn", 1024, 512, tk, BF16, "ffn1_dw1")
    grads["w3a"], moved = _matmul(h1, db, "tn", 1024, 512, tk, BF16, "ffn1_dw3", comm=_Direct([share("w1a")], False))
    parts["w1a"] = moved[0]
    grads["w2a"], moved = _matmul(s1, df1, "tn", 512, 1024, tk, BF16, "ffn1_dw2", comm=_Direct([share("w3a")], False))
    parts["w3a"] = moved[0]
    (dx0, sums1), moved = _norm_bwd(dh1, x0, dx1, f1, 0.5, g_ffn1, sc1, None, 0.0, "norm1_bwd",
                                    comm=_Direct([share("w2a")], False))
    parts["w2a"] = moved[0]

    moments = dict(w1a=(m_w1_ffn1, v_w1_ffn1), w3a=(m_w3_ffn1, v_w3_ffn1), w2a=(m_w2_ffn1, v_w2_ffn1),
                   w_in=(m_w_in, v_w_in), w_out=(m_w_out, v_w_out),
                   w1b=(m_w1_ffn2, v_w1_ffn2), w3b=(m_w3_ffn2, v_w3_ffn2), w2b=(m_w2_ffn2, v_w2_ffn2))
    upd = {}
    for k in big:
        rows = big[k].shape[0]
        tr = 256 if rows % 256 == 0 else rows // 4
        upd[k] = [o[None] for o in _adam_reduce(parts[k], big[k], moments[k][0][0], moments[k][1][0], "adam_" + k, tr)]

    dmod = jnp.concatenate([sums1[1:2], sums1[0:1], sums1[3:4], sums2[1:2], sums2[0:1], sums2[3:4],
                            sums3[1:2], sums3[0:1], sums3[3:4]], axis=1)
    small = [("b_ada", dmod, b_ada, m_b_ada, v_b_ada),
             ("g_ffn1", sums1[2:3], g_ffn1, m_g_ffn1, v_g_ffn1),
             ("g_mix", sums2[2:3], g_mix, m_g_mix, v_g_mix),
             ("spatial_w", d_sw[None], spatial_w, m_spatial_w, v_spatial_w),
             ("spatial_b", d_sb[:, :A_HEADS].T[None], spatial_b, m_spatial_b, v_spatial_b),
             ("g_v", d_gv[None], g_v, m_g_v, v_g_v),
             ("g_q", d_gq, g_q, m_g_q, v_g_q),
             ("g_k", d_gk, g_k, m_g_k, v_g_k),
             ("sinks", d_sink[:, :B_HEADS], sinks, m_sinks, v_sinks),
             ("rel_bias", d_rel.T, rel_bias, m_rel_bias, v_rel_bias),
             ("g_ffn2", sums3[2:3], g_ffn2, m_g_ffn2, v_g_ffn2)]
    packed = [jnp.concatenate([_pad_rows(item[i]) for item in small], axis=0) for i in range(1, 5)]
    g_all = _run_comm(_Direct([packed[0]], True), "gather_small")[0]
    small_out = _adam_reduce(g_all, packed[1], packed[2], packed[3], "adam_small", packed[0].shape[0])
    off = 0
    for name, g_, w_, _, _ in small:
        n_rows = _pad_rows(w_).shape[0]
        upd[name] = [o[off:off + n_rows].reshape(-1)[:w_.size].reshape(w_.shape) for o in small_out]
        off += n_rows

    dmod_rows = g_all[:, :N_MOD * d // BLOCK, :].reshape(N_DEV, N_MOD * d)
    dmod_cols = lax.dynamic_slice(dmod_rows, (0, me * MOD_COLS), (N_DEV, MOD_COLS))
    upd["w_ada"] = [o[None] for o in _adam_outer(act_all.T, dmod_cols, w_ada[0], m_w_ada[0], v_w_ada[0], "adam_w_ada")]

    order = [("w_ada", "w_ada"), ("b_ada", "b_ada"), ("g_ffn1", "g_ffn1"), ("w1_ffn1", "w1a"), ("w3_ffn1", "w3a"),
             ("w2_ffn1", "w2a"), ("g_mix", "g_mix"), ("w_in", "w_in"), ("spatial_w", "spatial_w"),
             ("spatial_b", "spatial_b"), ("g_v", "g_v"), ("g_q", "g_q"), ("g_k", "g_k"), ("sinks", "sinks"),
             ("rel_bias", "rel_bias"), ("w_out", "w_out"), ("g_ffn2", "g_ffn2"), ("w1_ffn2", "w1b"),
             ("w3_ffn2", "w3b"), ("w2_ffn2", "w2b")]
    outs = [loss, dx0[None]]
    for i in range(4):
        outs += [upd[key][i] for _, key in order]
    return tuple(outs)
```

```python
import functools
import math

import numpy as np
import jax
import jax.numpy as jnp
from jax import lax
from jax.experimental import pallas as pl
from jax.experimental.pallas import tpu as pltpu

F32 = jnp.float32
BF16 = jnp.bfloat16

D_MODEL = 2048
D_FF = 5632
BLOCK = 128
A_HEADS = 8
A_WIDTH = 1024
B_HEADS = 16
B_KV_HEADS = 2
GROUP = B_HEADS // B_KV_HEADS
HEAD_DIM = 64
B_WIDTH = 1024
KV_WIDTH = 128
IN_COLS = 3328
Q_OFF = 2 * A_WIDTH
K_OFF = Q_OFF + B_WIDTH
V_OFF = K_OFF + KV_WIDTH
N_BUCKETS = 32
N_MOD = 9
EPS = 1e-6
N_DEV = 8
MOD_COLS = N_MOD * D_MODEL // N_DEV

ADAM_LR = 0.001
ADAM_B1 = 0.9
ADAM_B2 = 0.999
ADAM_EPS = 1e-08
ADAM_WD = 0.01
ADAM_STEP = 10

FFN_ROWS = 1024
DW_TOKENS = 4096
MASK_VALUE = -1e30
VMEM_LIMIT = 56 * 1024 * 1024
MESH_ID = pl.DeviceIdType.MESH
ANY = pl.BlockSpec(memory_space=pl.ANY)

_SQRT_HALF = 0.7071067811865476
_INV_SQRT_2PI = 0.3989422804014327


def _pc(body, **kw):
    return pl.pallas_call(body, **kw)


def _params(sem=None):
    return pltpu.CompilerParams(dimension_semantics=sem, vmem_limit_bytes=VMEM_LIMIT)


def _dot(a, b):
    return lax.dot_general(a, b, (((1,), (0,)), ((), ())), preferred_element_type=F32)


def _dot_nt(a, b):
    return lax.dot_general(a, b, (((1,), (1,)), ((), ())), preferred_element_type=F32)


def _dot_tn(a, b):
    return lax.dot_general(a, b, (((0,), (0,)), ((), ())), preferred_element_type=F32)


def _gelu(x):
    return 0.5 * x * (1.0 + lax.erf(x * _SQRT_HALF))


def _gelu_and_grad(x):
    cdf = 0.5 * (1.0 + lax.erf(x * _SQRT_HALF))
    return x * cdf, cdf + x * jnp.exp(-0.5 * x * x) * _INV_SQRT_2PI


def _rms(x):
    r = lax.rsqrt(jnp.mean(x * x, axis=-1, keepdims=True) + EPS)
    return x * r, r


def _rms_bwd(dy, y, r):
    return r * (dy - y * jnp.mean(dy * y, axis=-1, keepdims=True))


def _row_spec(tm, cols):
    return pl.BlockSpec((tm, cols), lambda i: (i, 0))


def _vec_spec(cols):
    return pl.BlockSpec((1, cols), lambda i: (0, 0))


def _position():
    x, y, c = lax.axis_index("x"), lax.axis_index("y"), lax.axis_index("c")
    return x, y, c


def _linear(p):
    return 4 * p[0] + 2 * p[1] + p[2]


class _Comm:
    PEER_COPIES = N_DEV - 1

    def __init__(self, arrs):
        self.arrs = list(arrs)
        n = len(self.arrs)
        self.scratch_shapes = [pltpu.SemaphoreType.DMA((self.PEER_COPIES * n,)),
                               pltpu.SemaphoreType.DMA((self.PEER_COPIES * n,)),
                               pltpu.SemaphoreType.DMA((n,))]

    def bind(self, srcs, dsts, sems):
        self.srcs, self.dsts = srcs, dsts
        self.send_sems, self.recv_sems, self.local_sems = sems
        x, y, c = _position()
        self.me, self.sibling, self.core = (x, y, c), (x, y, 1 - c), c
        self.chips = [(1 - x, y), (x, 1 - y), (1 - x, 1 - y)]
        self.peers = [(1 - x if k & 4 else x, 1 - y if k & 2 else y, 1 - c if k & 1 else c)
                      for k in range(1, N_DEV)]

    def relay(self):
        pass


class _GatherTwoLevel(_Comm):
    def __init__(self, arrs):
        super().__init__(arrs)
        self.out_shape = [jax.ShapeDtypeStruct((N_DEV,) + a.shape, a.dtype) for a in self.arrs]

    def _copy(self, a, k, block, to, from_input=False):
        rows = self.dsts[a].at[_linear(block)]
        return pltpu.make_async_remote_copy(
            src_ref=self.srcs[a] if from_input else rows, dst_ref=rows,
            send_sem=self.send_sems.at[self.PEER_COPIES * a + k], recv_sem=self.recv_sems.at[self.PEER_COPIES * a + k],
            device_id=to, device_id_type=MESH_ID)

    def _local(self, a):
        return pltpu.make_async_copy(self.srcs[a], self.dsts[a].at[_linear(self.me)], self.local_sems.at[a])

    def _first(self, a):
        return [self._copy(a, 0, self.me, self.sibling, True)] + [
            self._copy(a, 1 + j, self.me, (*chip, self.core), True) for j, chip in enumerate(self.chips)]

    def _passed(self, a):
        return [self._copy(a, 4 + j, (*chip, self.core), self.sibling) for j, chip in enumerate(self.chips)]

    def start(self):
        for a in range(len(self.arrs)):
            self._local(a).start()
            for cp in self._first(a):
                cp.start()

    def relay(self):
        for a in range(len(self.arrs)):
            for j, chip in enumerate(self.chips):
                self._copy(a, 1 + j, (*chip, self.core), self.me).wait_recv()
                self._passed(a)[j].start()

    def finish(self):
        for a in range(len(self.arrs)):
            self._copy(a, 0, self.sibling, self.me).wait_recv()
            for j, chip in enumerate(self.chips):
                self._copy(a, 4 + j, (*chip, 1 - self.core), self.me).wait_recv()
        for a in range(len(self.arrs)):
            for cp in self._first(a) + self._passed(a):
                cp.wait_send()
            self._local(a).wait()


class _Direct(_Comm):
    def __init__(self, arrs, broadcast):
        super().__init__(arrs)
        self.broadcast = broadcast
        self.out_shape = [jax.ShapeDtypeStruct(((N_DEV,) + a.shape) if broadcast else a.shape, a.dtype)
                          for a in self.arrs]

    def _outgoing(self, a, to):
        return self.srcs[a] if self.broadcast else self.srcs[a].at[_linear(to)]

    def _copy(self, a, k, sender, to):
        return pltpu.make_async_remote_copy(
            src_ref=self._outgoing(a, to), dst_ref=self.dsts[a].at[_linear(sender)],
            send_sem=self.send_sems.at[self.PEER_COPIES * a + k], recv_sem=self.recv_sems.at[self.PEER_COPIES * a + k],
            device_id=to, device_id_type=MESH_ID)

    def _local(self, a):
        return pltpu.make_async_copy(self._outgoing(a, self.me), self.dsts[a].at[_linear(self.me)],
                                     self.local_sems.at[a])

    def start(self):
        for a in range(len(self.arrs)):
            self._local(a).start()
            for k, peer in enumerate(self.peers):
                self._copy(a, k, self.me, peer).start()

    def finish(self):
        for a in range(len(self.arrs)):
            for k, peer in enumerate(self.peers):
                self._copy(a, k, peer, self.me).wait_recv()
        for a in range(len(self.arrs)):
            for k, peer in enumerate(self.peers):
                self._copy(a, k, self.me, peer).wait_send()
            self._local(a).wait()


def _run_comm(comm, name):
    n = len(comm.arrs)

    def body(*refs):
        comm.bind(refs[:n], refs[n:2 * n], refs[2 * n:])
        comm.start()
        comm.relay()
        comm.finish()

    return _pc(body, name=name, out_shape=comm.out_shape, in_specs=[ANY] * n, out_specs=[ANY] * n,
               scratch_shapes=comm.scratch_shapes)(*comm.arrs)


def _call(body, inputs, *, name, grid, in_specs, out_specs, out_shape, scratch_shapes=(), semantics,
          comm=None, relay_at=0.5):
    if comm is None:
        out = _pc(body, name=name, grid=grid, in_specs=in_specs, out_specs=out_specs, out_shape=out_shape,
                  scratch_shapes=list(scratch_shapes), compiler_params=_params(semantics))(*inputs)
        return list(out), []
    n_in, n_out, n_sc, k = len(in_specs), len(out_specs), len(scratch_shapes), len(comm.arrs)
    steps = math.prod(grid)
    relay_step = min(steps - 1, int(steps * relay_at))

    def carrier(*refs):
        ins, rest = refs[:n_in], refs[n_in:]
        csrc, rest = rest[:k], rest[k:]
        outs, rest = rest[:n_out], rest[n_out:]
        cdst, rest = rest[:k], rest[k:]
        scratch, csems = rest[:n_sc], rest[n_sc:]
        step = 0
        for axis, size in enumerate(grid):
            step = step * size + pl.program_id(axis)
        comm.bind(csrc, cdst, csems)
        pl.when(step == 0)(comm.start)
        pl.when(step == relay_step)(comm.relay)
        body(*ins, *outs, *scratch)
        pl.when(step == steps - 1)(comm.finish)

    out = _pc(carrier, name=name, grid=grid, in_specs=list(in_specs) + [ANY] * k,
              out_specs=list(out_specs) + [ANY] * k, out_shape=list(out_shape) + comm.out_shape,
              scratch_shapes=list(scratch_shapes) + comm.scratch_shapes,
              compiler_params=_params(("arbitrary",) * len(grid)))(*inputs, *comm.arrs)
    return list(out[:n_out]), list(out[n_out:])


def _matmul(a, b, mode, tm, tn, tk, out_dtype, name, comm=None, relay_at=0.5):
    if mode == "nn":
        (m, kk), nn = a.shape, b.shape[1]
        a_spec = pl.BlockSpec((tm, tk), lambda i, j, k: (i, k))
        b_spec = pl.BlockSpec((tk, tn), lambda i, j, k: (k, j))
        dot = _dot
    elif mode == "nt":
        (m, kk), nn = a.shape, b.shape[0]
        a_spec = pl.BlockSpec((tm, tk), lambda i, j, k: (i, k))
        b_spec = pl.BlockSpec((tn, tk), lambda i, j, k: (j, k))
        dot = _dot_nt
    else:
        (kk, m), nn = a.shape, b.shape[1]
        a_spec = pl.BlockSpec((tk, tm), lambda i, j, k: (k, i))
        b_spec = pl.BlockSpec((tk, tn), lambda i, j, k: (k, j))
        dot = _dot_tn
    assert m % tm == 0 and nn % tn == 0 and kk % tk == 0, (a.shape, b.shape, tm, tn, tk)
    nk = kk // tk
    narrow = nk > 1 and out_dtype != F32

    def body(a_ref, b_ref, o_ref, *acc):
        p = dot(a_ref[...], b_ref[...])
        if nk == 1:
            o_ref[...] = p.astype(o_ref.dtype)
            return
        k = pl.program_id(2)
        acc_ref = acc[0] if narrow else o_ref

        @pl.when(k == 0)
        def _():
            acc_ref[...] = p

        @pl.when((k > 0) & (k < nk - 1) if narrow else k > 0)
        def _():
            acc_ref[...] += p

        if narrow:
            @pl.when(k == nk - 1)
            def _():
                o_ref[...] = (acc_ref[...] + p).astype(o_ref.dtype)

    out, moved = _call(
        body, (a, b), name=name, grid=(m // tm, nn // tn, nk),
        in_specs=[a_spec, b_spec], out_specs=[pl.BlockSpec((tm, tn), lambda i, j, k: (i, j))],
        out_shape=[jax.ShapeDtypeStruct((m, nn), out_dtype)],
        scratch_shapes=[pltpu.VMEM((tm, tn), F32)] if narrow else [],
        semantics=("parallel", "parallel", "arbitrary"), comm=comm, relay_at=relay_at)
    return out[0] if comm is None else (out[0], moved)


def _norm_fwd(x_prev, f_prev, gate_prev, coef, g, sc, sh, name, tm=256):
    t, d = x_prev.shape
    residual = f_prev is not None

    def body(*refs):
        if residual:
            xp_ref, fp_ref, gp_ref, g_ref, sc_ref, sh_ref, x_ref, h_ref = refs
            x = xp_ref[...] + coef * gp_ref[...] * fp_ref[...]
            x_ref[...] = x
        else:
            xp_ref, g_ref, sc_ref, sh_ref, h_ref = refs
            x = xp_ref[...]
        y, _ = _rms(x)
        h_ref[...] = ((y * g_ref[...]) * (1.0 + sc_ref[...]) + sh_ref[...]).astype(BF16)

    row, vec = _row_spec(tm, d), _vec_spec(d)
    if residual:
        ins, in_specs = (x_prev, f_prev, gate_prev, g, sc, sh), [row, row, vec, vec, vec, vec]
        out_shape = [jax.ShapeDtypeStruct((t, d), F32), jax.ShapeDtypeStruct((t, d), BF16)]
        out_specs = [row, row]
    else:
        ins, in_specs = (x_prev, g, sc, sh), [row, vec, vec, vec]
        out_shape = [jax.ShapeDtypeStruct((t, d), BF16)]
        out_specs = [row]
    out = _pc(body, name=name, grid=(t // tm,), in_specs=in_specs, out_specs=out_specs, out_shape=out_shape,
              compiler_params=_params(("parallel",)))(*ins)
    return out if residual else out[0]


def _norm_bwd(dh, x, dxo, fo, coef_o, g, sc, gate_prev, coef_prev, name, tm=256, comm=None):
    t, d = x.shape
    with_prev = gate_prev is not None

    def body(*refs):
        if with_prev:
            dh_ref, x_ref, dxo_ref, fo_ref, g_ref, sc_ref, gp_ref, dx_ref, dfp_ref, sums_ref = refs
        else:
            dh_ref, x_ref, dxo_ref, fo_ref, g_ref, sc_ref, dx_ref, sums_ref = refs
        dh_v, dxo_v = dh_ref[...], dxo_ref[...]
        y, r = _rms(x_ref[...])
        n = y * g_ref[...]
        dn = dh_v * (1.0 + sc_ref[...])
        dx = dxo_v + _rms_bwd(dn * g_ref[...], y, r)
        dx_ref[...] = dx
        if with_prev:
            dfp_ref[...] = (coef_prev * gp_ref[...] * dx).astype(BF16)
        @pl.when(pl.program_id(0) == 0)
        def _():
            sums_ref[...] = jnp.zeros_like(sums_ref)

        sums_ref[0:1, :] += jnp.sum(dh_v * n, axis=0, keepdims=True)
        sums_ref[1:2, :] += jnp.sum(dh_v, axis=0, keepdims=True)
        sums_ref[2:3, :] += jnp.sum(dn * y, axis=0, keepdims=True)
        sums_ref[3:4, :] += jnp.sum(coef_o * dxo_v * fo_ref[...], axis=0, keepdims=True)

    row, vec = _row_spec(tm, d), _vec_spec(d)
    sums_spec = pl.BlockSpec((8, d), lambda i: (0, 0))
    ins, in_specs = [dh, x, dxo, fo, g, sc], [row, row, row, row, vec, vec]
    out_shape, out_specs = [jax.ShapeDtypeStruct((t, d), F32)], [row]
    if with_prev:
        ins.append(gate_prev)
        in_specs.append(vec)
        out_shape.append(jax.ShapeDtypeStruct((t, d), BF16))
        out_specs.append(row)
    out_shape.append(jax.ShapeDtypeStruct((8, d), F32))
    out_specs.append(sums_spec)
    out, moved = _call(body, ins, name=name, grid=(t // tm,), in_specs=in_specs, out_specs=out_specs,
                       out_shape=out_shape, semantics=("arbitrary",), comm=comm)
    return out if comm is None else (out, moved)


def _ffn_up(h, w1t, w3t, name, tf=512, comm=None, relay_at=0.5):
    t, d = h.shape
    ff = w1t.shape[0]
    tm = min(FFN_ROWS, t)

    def body(h_ref, w1_ref, w3_ref, ga_ref, gb_ref, s_ref):
        hv = h_ref[...]
        a = _dot_nt(hv, w1_ref[...])
        b = _dot_nt(hv, w3_ref[...])
        sg = jax.nn.sigmoid(a)
        sil = a * sg
        ga_ref[...] = (b * (sg * (1.0 + a * (1.0 - sg)))).astype(BF16)
        gb_ref[...] = sil.astype(BF16)
        s_ref[...] = (sil * b).astype(BF16)

    tile = pl.BlockSpec((tm, tf), lambda i, j: (i, j))
    cols = pl.BlockSpec((tf, d), lambda i, j: (j, 0))
    out, moved = _call(
        body, (h, w1t, w3t), name=name, grid=(t // tm, ff // tf),
        in_specs=[pl.BlockSpec((tm, d), lambda i, j: (i, 0)), cols, cols], out_specs=[tile, tile, tile],
        out_shape=[jax.ShapeDtypeStruct((t, ff), BF16)] * 3,
        semantics=("parallel", "parallel"), comm=comm, relay_at=relay_at)
    return out if comm is None else (out, moved)


def _ffn_dact(df, ga, gb, w2, name, tf=1408, comm=None):
    t, d = df.shape
    ff = ga.shape[1]
    tm = min(FFN_ROWS, t)

    def body(df_ref, ga_ref, gb_ref, w2_ref, da_ref, db_ref):
        ds = _dot_nt(df_ref[...], w2_ref[...])
        da_ref[...] = (ds * ga_ref[...]).astype(BF16)
        db_ref[...] = (ds * gb_ref[...]).astype(BF16)

    tile = pl.BlockSpec((tm, tf), lambda i, j: (i, j))
    out, moved = _call(
        body, (df, ga, gb, w2), name=name, grid=(t // tm, ff // tf),
        in_specs=[pl.BlockSpec((tm, d), lambda i, j: (i, 0)), tile, tile, pl.BlockSpec((tf, d), lambda i, j: (j, 0))],
        out_specs=[tile, tile], out_shape=[jax.ShapeDtypeStruct((t, ff), BF16)] * 2,
        semantics=("parallel", "parallel"), comm=comm)
    return out if comm is None else (out, moved)


def _ffn_dh(da, db, w1t, w3t, name, tn=512, comm=None):
    t, ff = da.shape
    d = w1t.shape[1]
    tm, tk = min(FFN_ROWS, t), ff // 2

    def body(da_ref, db_ref, w1_ref, w3_ref, dh_ref):
        p = _dot(da_ref[...], w1_ref[...]) + _dot(db_ref[...], w3_ref[...])
        k = pl.program_id(2)

        @pl.when(k == 0)
        def _():
            dh_ref[...] = p

        @pl.when(k > 0)
        def _():
            dh_ref[...] += p

    act = pl.BlockSpec((tm, tk), lambda i, j, k: (i, k))
    wgt = pl.BlockSpec((tk, tn), lambda i, j, k: (k, j))
    out, moved = _call(
        body, (da, db, w1t, w3t), name=name, grid=(t // tm, d // tn, 2),
        in_specs=[act, act, wgt, wgt], out_specs=[pl.BlockSpec((tm, tn), lambda i, j, k: (i, j))],
        out_shape=[jax.ShapeDtypeStruct((t, d), F32)],
        semantics=("parallel", "parallel", "arbitrary"), comm=comm)
    return out[0] if comm is None else (out[0], moved)


def _loss_head(x_prev, f_prev, gate_prev, coef, target, name, tm=256):
    t, d = x_prev.shape
    steps = t // tm

    def body(xp_ref, fp_ref, gp_ref, tg_ref, dy_ref, df_ref, loss_ref, acc_ref):
        i = pl.program_id(0)
        e = xp_ref[...] + coef * gp_ref[...] * fp_ref[...] - tg_ref[...]
        dy = e * (1.0 / d)
        dy_ref[...] = dy
        df_ref[...] = (coef * gp_ref[...] * dy).astype(BF16)
        part = jnp.sum(e * e, axis=0, keepdims=True)

        @pl.when(i == 0)
        def _():
            acc_ref[...] = part

        @pl.when(i > 0)
        def _():
            acc_ref[...] += part

        @pl.when(i == steps - 1)
        def _():
            loss_ref[...] = jnp.sum(acc_ref[...], axis=1, keepdims=True) * (0.5 / d)

    row, vec = _row_spec(tm, d), _vec_spec(d)
    return _pc(
        body, name=name, grid=(steps,),
        in_specs=[row, row, vec, row], out_specs=[row, row, pl.BlockSpec((1, 1), lambda i: (0, 0))],
        out_shape=[jax.ShapeDtypeStruct((t, d), F32), jax.ShapeDtypeStruct((t, d), BF16),
                   jax.ShapeDtypeStruct((1, 1), F32)],
        scratch_shapes=[pltpu.VMEM((1, d), F32)],
        compiler_params=_params(("arbitrary",)),
    )(x_prev, f_prev, gate_prev, target)


def _split3(x):
    hi = x.astype(BF16)
    r1 = x - hi.astype(F32)
    mid = r1.astype(BF16)
    lo = (r1 - mid.astype(F32)).astype(BF16)
    return hi, mid, lo


def _select_matmul(a, onehot, name):
    m, n = a.shape[0], onehot.shape[1]

    def body(a_ref, oh_ref, o_ref):
        hi, mid, lo = _split3(a_ref[...])
        oh = oh_ref[...]
        o_ref[...] = (_dot(hi, oh) + _dot(mid, oh)) + _dot(lo, oh)

    return _pc(body, name=name, out_shape=jax.ShapeDtypeStruct((m, n), F32), compiler_params=_params())(a, onehot)


def _bucket_onehot():
    qi = np.arange(BLOCK)[:, None]
    kj = np.arange(2 * BLOCK)[None, :]
    dist = np.clip(qi + BLOCK - kj, 0, None)
    nf = np.maximum(dist, 1).astype(np.float32)
    large = 16 + (np.log(nf / np.float32(16)) / np.float32(math.log(128 / 16)) * np.float32(16)).astype(np.int32)
    bucket = np.where(dist < 16, dist, np.minimum(large, N_BUCKETS - 1)).reshape(-1)
    return (bucket[None, :] == np.arange(N_BUCKETS)[:, None]).astype(np.float32)


def _window_mask(n):
    row = lax.broadcasted_iota(jnp.int32, (GROUP * BLOCK, 2 * BLOCK), 0)
    qi = row & (BLOCK - 1)
    kj = lax.broadcasted_iota(jnp.int32, (GROUP * BLOCK, 2 * BLOCK), 1)
    return (kj > qi) & (kj <= qi + BLOCK) & ((kj >= BLOCK) | (n > 0))


def _kv_band(zc_ref, zp_ref, kvh):
    lo, hi = kvh * HEAD_DIM, (kvh + 1) * HEAD_DIM
    k_raw = jnp.concatenate([zp_ref[:, lo:hi], zc_ref[:, K_OFF + lo:K_OFF + hi]], axis=0)
    v_raw = jnp.concatenate([zp_ref[:, KV_WIDTH + lo:KV_WIDTH + hi], zc_ref[:, V_OFF + lo:V_OFF + hi]], axis=0)
    return k_raw, v_raw


def _stack_heads(ref, off):
    return jnp.concatenate([ref[:, off + g * HEAD_DIM:off + (g + 1) * HEAD_DIM] for g in range(GROUP)], axis=0)


def _unstack_heads(ref, off, stacked):
    for g in range(GROUP):
        ref[:, off + g * HEAD_DIM:off + (g + 1) * HEAD_DIM] = stacked[g * BLOCK:(g + 1) * BLOCK]


def _group_softmax(qn, kband, bias_ref, sink_ref, kvh, valid):
    bias = bias_ref[kvh * GROUP:(kvh + 1) * GROUP].reshape(GROUP * BLOCK, 2 * BLOCK)
    s = _dot_nt(qn, kband) * (HEAD_DIM ** -0.5) + bias
    s = jnp.where(valid, s, MASK_VALUE)
    sink = jnp.concatenate([jnp.full((BLOCK, 1), sink_ref[0, kvh * GROUP + g], F32) for g in range(GROUP)], axis=0)
    m = jnp.maximum(jnp.max(s, axis=-1, keepdims=True), sink)
    p = jnp.exp(s - m)
    esink = jnp.exp(sink - m)
    inv = 1.0 / (jnp.sum(p, axis=-1, keepdims=True) + esink)
    return p * inv, esink * inv, inv


def _mix_specs(nb, order):
    cur = pl.BlockSpec((BLOCK, IN_COLS), lambda n: (order(n), 0))
    prev = pl.BlockSpec((BLOCK, 2 * KV_WIDTH), lambda n: (jnp.maximum(order(n) - 1, 0), K_OFF // (2 * KV_WIDTH)))
    full = lambda shape: pl.BlockSpec(shape, lambda n: (0,) * len(shape))
    params = [full((A_HEADS, BLOCK, BLOCK)), full((BLOCK, A_HEADS)), full((A_HEADS, BLOCK)),
              full((1, HEAD_DIM)), full((1, HEAD_DIM)), pl.BlockSpec(memory_space=pltpu.SMEM),
              full((B_HEADS, BLOCK, 2 * BLOCK))]
    return cur, prev, params, full


def _mix_fwd(z, sw, sb_t, gv, gq, gk, sinks, bias, name, comm=None, relay_at=0.5):
    t = z.shape[0]
    nb = t // BLOCK

    def body(zc_ref, zp_ref, sw_ref, sbt_ref, gv_ref, gq_ref, gk_ref, sink_ref, bias_ref, y_ref):
        n = pl.program_id(0)
        ri = lax.broadcasted_iota(jnp.int32, (BLOCK, BLOCK), 0)
        ci = lax.broadcasted_iota(jnp.int32, (BLOCK, BLOCK), 1)
        tril = ri >= ci
        for h in range(A_HEADS):
            u = _gelu(zc_ref[:, h * BLOCK:(h + 1) * BLOCK])
            vv = _gelu(zc_ref[:, A_WIDTH + h * BLOCK:A_WIDTH + (h + 1) * BLOCK])
            vhat, _ = _rms(vv)
            vn = (vhat * gv_ref[h:h + 1, :]).astype(BF16)
            w = jnp.where(tril, sw_ref[h], 0.0).astype(BF16)
            mixed = _dot(w, vn) + sbt_ref[:, h:h + 1]
            y_ref[:, h * BLOCK:(h + 1) * BLOCK] = (u * mixed).astype(BF16)
        valid = _window_mask(n)
        for kvh in range(B_KV_HEADS):
            k_raw, v_raw = _kv_band(zc_ref, zp_ref, kvh)
            khat, _ = _rms(k_raw)
            kband = (khat * gk_ref[...]).astype(BF16)
            vband = v_raw.astype(BF16)
            qhat, _ = _rms(_stack_heads(zc_ref, Q_OFF + kvh * GROUP * HEAD_DIM))
            qn = (qhat * gq_ref[...]).astype(BF16)
            w, _, _ = _group_softmax(qn, kband, bias_ref, sink_ref, kvh, valid)
            o = _dot(w.astype(BF16), vband)
            _unstack_heads(y_ref, A_WIDTH + kvh * GROUP * HEAD_DIM, o.astype(BF16))

    cur, prev, params, _ = _mix_specs(nb, lambda n: n)
    out, moved = _call(
        body, (z, z, sw, sb_t, gv, gq, gk, sinks, bias), name=name, grid=(nb,), in_specs=[cur, prev] + params,
        out_specs=[pl.BlockSpec((BLOCK, A_WIDTH + B_WIDTH), lambda n: (n, 0))],
        out_shape=[jax.ShapeDtypeStruct((t, A_WIDTH + B_WIDTH), BF16)],
        semantics=("parallel",), comm=comm, relay_at=relay_at)
    return out[0] if comm is None else (out[0], moved)


def _mix_bwd(z, dy, sw, sb_t, gv, gq, gk, sinks, bias, name, comm=None):
    t = z.shape[0]
    nb = t // BLOCK

    def body(zc_ref, zp_ref, dy_ref, sw_ref, sbt_ref, gv_ref, gq_ref, gk_ref, sink_ref, bias_ref,
             dz_ref, dsw_ref, dsb_ref, dgv_ref, dgq_ref, dgk_ref, dsink_ref, dbias_ref, carry_ref):
        step = pl.program_id(0)
        n = nb - 1 - step

        @pl.when(step == 0)
        def _():
            dsw_ref[...] = jnp.zeros_like(dsw_ref)
            dsb_ref[...] = jnp.zeros_like(dsb_ref)
            dgv_ref[...] = jnp.zeros_like(dgv_ref)
            dgq_ref[...] = jnp.zeros_like(dgq_ref)
            dgk_ref[...] = jnp.zeros_like(dgk_ref)
            dsink_ref[...] = jnp.zeros_like(dsink_ref)
            dbias_ref[...] = jnp.zeros_like(dbias_ref)
            carry_ref[...] = jnp.zeros_like(carry_ref)

        ri = lax.broadcasted_iota(jnp.int32, (BLOCK, BLOCK), 0)
        ci = lax.broadcasted_iota(jnp.int32, (BLOCK, BLOCK), 1)
        tril = ri >= ci
        dsb = jnp.zeros((BLOCK, BLOCK), F32)
        for h in range(A_HEADS):
            uo, vo = h * BLOCK, A_WIDTH + h * BLOCK
            u_raw, v_raw = zc_ref[:, uo:uo + BLOCK], zc_ref[:, vo:vo + BLOCK]
            (u, du_raw), (vv, dv_raw) = _gelu_and_grad(u_raw), _gelu_and_grad(v_raw)
            vhat, r = _rms(vv)
            vn = (vhat * gv_ref[h:h + 1, :]).astype(BF16)
            w = jnp.where(tril, sw_ref[h], 0.0).astype(BF16)
            mixed = _dot(w, vn) + sbt_ref[:, h:h + 1]
            dya = dy_ref[:, uo:uo + BLOCK]
            dmixed = dya * u
            dm16 = dmixed.astype(BF16)
            dsw_ref[h] += jnp.where(tril, _dot_nt(dm16, vn), 0.0)
            dsb = dsb + jnp.where(ci == h, jnp.sum(dmixed, axis=1, keepdims=True), 0.0)
            dvn = _dot_tn(w, dm16)
            dgv_ref[h:h + 1, :] += jnp.sum(dvn * vhat, axis=0, keepdims=True)
            dvv = _rms_bwd(dvn * gv_ref[h:h + 1, :], vhat, r)
            dz_ref[:, uo:uo + BLOCK] = (dya * mixed * du_raw).astype(BF16)
            dz_ref[:, vo:vo + BLOCK] = (dvv * dv_raw).astype(BF16)
        dsb_ref[...] += dsb

        valid = _window_mask(n)
        lane = lax.broadcasted_iota(jnp.int32, (1, BLOCK), 1)
        dsink = jnp.zeros((1, BLOCK), F32)
        dgq = jnp.zeros((1, HEAD_DIM), F32)
        for kvh in range(B_KV_HEADS):
            ko, vo = K_OFF + kvh * HEAD_DIM, V_OFF + kvh * HEAD_DIM
            k_raw, v_raw = _kv_band(zc_ref, zp_ref, kvh)
            khat, kr = _rms(k_raw)
            kband = (khat * gk_ref[...]).astype(BF16)
            vband = v_raw.astype(BF16)
            qo = Q_OFF + kvh * GROUP * HEAD_DIM
            qhat, qr = _rms(_stack_heads(zc_ref, qo))
            qn = (qhat * gq_ref[...]).astype(BF16)
            w, wsink, _ = _group_softmax(qn, kband, bias_ref, sink_ref, kvh, valid)
            do = _stack_heads(dy_ref, A_WIDTH + kvh * GROUP * HEAD_DIM).astype(BF16)
            dvb = _dot_tn(w.astype(BF16), do)
            dw = _dot_nt(do, vband)
            rowdot = jnp.sum(w * dw, axis=-1, keepdims=True)
            ds = w * (dw - rowdot)
            dsink_rows = -wsink * rowdot
            for g in range(GROUP):
                head_sum = jnp.sum(dsink_rows[g * BLOCK:(g + 1) * BLOCK], axis=0, keepdims=True)
                dsink = dsink + jnp.where(lane == kvh * GROUP + g, head_sum, 0.0)
            dbias_ref[kvh * GROUP:(kvh + 1) * GROUP] += ds.reshape(GROUP, BLOCK, 2 * BLOCK)
            ds16 = (ds * (HEAD_DIM ** -0.5)).astype(BF16)
            dqn = _dot(ds16, kband)
            dkn = _dot_tn(ds16, qn)
            dgq = dgq + jnp.sum(dqn * qhat, axis=0, keepdims=True)
            _unstack_heads(dz_ref, qo, _rms_bwd(dqn * gq_ref[...], qhat, qr).astype(BF16))
            dgk_ref[...] += jnp.sum(dkn * khat, axis=0, keepdims=True)
            dk = _rms_bwd(dkn * gk_ref[...], khat, kr)
            co = kvh * HEAD_DIM
            dz_ref[:, ko:ko + HEAD_DIM] = (dk[BLOCK:] + carry_ref[:, co:co + HEAD_DIM]).astype(BF16)
            dz_ref[:, vo:vo + HEAD_DIM] = (dvb[BLOCK:] + carry_ref[:, KV_WIDTH + co:KV_WIDTH + co + HEAD_DIM]).astype(BF16)
            carry_ref[:, co:co + HEAD_DIM] = dk[:BLOCK]
            carry_ref[:, KV_WIDTH + co:KV_WIDTH + co + HEAD_DIM] = dvb[:BLOCK]
        dgq_ref[...] += dgq
        dsink_ref[...] += dsink

    order = lambda n: nb - 1 - n
    cur, prev, params, full = _mix_specs(nb, order)
    dy_spec = pl.BlockSpec((BLOCK, A_WIDTH + B_WIDTH), lambda n: (order(n), 0))
    out, moved = _call(
        body, (z, z, dy, sw, sb_t, gv, gq, gk, sinks, bias), name=name, grid=(nb,),
        in_specs=[cur, prev, dy_spec] + params,
        out_specs=[cur, full((A_HEADS, BLOCK, BLOCK)), full((BLOCK, BLOCK)), full((A_HEADS, BLOCK)),
                   full((1, HEAD_DIM)), full((1, HEAD_DIM)), full((1, BLOCK)), full((B_HEADS, BLOCK, 2 * BLOCK))],
        out_shape=[jax.ShapeDtypeStruct((t, IN_COLS), BF16), jax.ShapeDtypeStruct((A_HEADS, BLOCK, BLOCK), F32),
                   jax.ShapeDtypeStruct((BLOCK, BLOCK), F32), jax.ShapeDtypeStruct((A_HEADS, BLOCK), F32),
                   jax.ShapeDtypeStruct((1, HEAD_DIM), F32), jax.ShapeDtypeStruct((1, HEAD_DIM), F32),
                   jax.ShapeDtypeStruct((1, BLOCK), F32), jax.ShapeDtypeStruct((B_HEADS, BLOCK, 2 * BLOCK), F32)],
        scratch_shapes=[pltpu.VMEM((BLOCK, 2 * KV_WIDTH), F32)],
        semantics=("arbitrary",), comm=comm)
    return out if comm is None else (out, moved)


def _ada_fwd(c_all, w_ada, b_cols, name, tn=768):
    nb, d = c_all.shape
    cols = w_ada.shape[1]

    def body(c_ref, w_ref, b_ref, act_ref, mod_ref):
        cv = c_ref[...]
        act = cv * jax.nn.sigmoid(cv)
        act_ref[...] = act
        mod_ref[...] = _dot(act.astype(BF16), w_ref[...].astype(BF16)) + b_ref[...]

    return _pc(
        body, name=name, grid=(cols // tn,),
        in_specs=[pl.BlockSpec((nb, d), lambda j: (0, 0)), pl.BlockSpec((d, tn), lambda j: (0, j)),
                  pl.BlockSpec((1, tn), lambda j: (0, j))],
        out_specs=[pl.BlockSpec((nb, d), lambda j: (0, 0)), pl.BlockSpec((nb, tn), lambda j: (0, j))],
        out_shape=[jax.ShapeDtypeStruct((nb, d), F32), jax.ShapeDtypeStruct((nb, cols), F32)],
        compiler_params=_params(("arbitrary",)),
    )(c_all, w_ada, b_cols)


def _adamw(w, g, m, v):
    m = ADAM_B1 * m + (1.0 - ADAM_B1) * g
    v = ADAM_B2 * v + (1.0 - ADAM_B2) * (g * g)
    m_hat = m / (1.0 - ADAM_B1 ** ADAM_STEP)
    v_hat = v / (1.0 - ADAM_B2 ** ADAM_STEP)
    delta = -ADAM_LR * (m_hat / (jnp.sqrt(v_hat) + ADAM_EPS) + ADAM_WD * w)
    return delta, m, v


def _adam_outer(act_t, dmod, w, m, v, name, tr=256):
    rows, cols = w.shape
    nb = dmod.shape[0]

    def body(act_ref, dm_ref, w_ref, m_ref, v_ref, g_ref, d_ref, nm_ref, nv_ref):
        act = act_ref[...].astype(BF16).astype(F32)
        dm = dm_ref[...].astype(BF16).astype(F32)
        g = act[:, 0:1] * dm[0:1, :]
        for b in range(1, nb):
            g = g + act[:, b:b + 1] * dm[b:b + 1, :]
        g_ref[...] = g
        d_ref[...], nm_ref[...], nv_ref[...] = _adamw(w_ref[...], g, m_ref[...], v_ref[...])

    tile = pl.BlockSpec((tr, cols), lambda i: (i, 0))
    return _pc(
        body, name=name, grid=(rows // tr,),
        in_specs=[pl.BlockSpec((tr, nb), lambda i: (i, 0)), pl.BlockSpec((nb, cols), lambda i: (0, 0)), tile, tile, tile],
        out_specs=[tile] * 4, out_shape=[jax.ShapeDtypeStruct((rows, cols), F32)] * 4,
        compiler_params=_params(("parallel",)),
    )(act_t, dmod, w, m, v)


def _adam_reduce(parts, w, m, v, name, tr):
    rows, cols = w.shape

    def body(p_ref, w_ref, m_ref, v_ref, g_ref, d_ref, nm_ref, nv_ref):
        g = p_ref[0].astype(F32)
        for i in range(1, N_DEV):
            g = g + p_ref[i].astype(F32)
        g_ref[...] = g
        d_ref[...], nm_ref[...], nv_ref[...] = _adamw(w_ref[...], g, m_ref[...], v_ref[...])

    tile = pl.BlockSpec((tr, cols), lambda i: (i, 0))
    return _pc(
        body, name=name, grid=(rows // tr,),
        in_specs=[pl.BlockSpec((N_DEV, tr, cols), lambda i: (0, i, 0)), tile, tile, tile],
        out_specs=[tile] * 4, out_shape=[jax.ShapeDtypeStruct((rows, cols), F32)] * 4,
        compiler_params=_params(("parallel",)),
    )(parts, w, m, v)


def _scatter_cols(w):
    r, c = w.shape
    return jnp.transpose(w.reshape(r, N_DEV, c // N_DEV), (1, 0, 2)).astype(BF16)


def _pad_rows(v):
    flat = v.reshape(-1)
    pad = (-flat.shape[0]) % (8 * BLOCK)
    return jnp.pad(flat, (0, pad)).reshape(-1, BLOCK)


def kernel(x, c, w_ada, b_ada, g_ffn1, w1_ffn1, w3_ffn1, w2_ffn1, g_mix, w_in, spatial_w, spatial_b, g_v, g_q, g_k, sinks, rel_bias, w_out, g_ffn2, w1_ffn2, w3_ffn2, w2_ffn2, loss_target, m_w_ada, m_b_ada, m_g_ffn1, m_w1_ffn1, m_w3_ffn1, m_w2_ffn1, m_g_mix, m_w_in, m_spatial_w, m_spatial_b, m_g_v, m_g_q, m_g_k, m_sinks, m_rel_bias, m_w_out, m_g_ffn2, m_w1_ffn2, m_w3_ffn2, m_w2_ffn2, v_w_ada, v_b_ada, v_g_ffn1, v_w1_ffn1, v_w3_ffn1, v_w2_ffn1, v_g_mix, v_w_in, v_spatial_w, v_spatial_b, v_g_v, v_g_q, v_g_k, v_sinks, v_rel_bias, v_w_out, v_g_ffn2, v_w1_ffn2, v_w3_ffn2, v_w2_ffn2):
    me = _linear(_position())
    d = D_MODEL
    x0 = x[0]
    target = loss_target[0]
    t = x0.shape[0]

    big = dict(w1a=w1_ffn1[0], w3a=w3_ffn1[0], w2a=w2_ffn1[0], w_in=w_in[0], w_out=w_out[0],
               w1b=w1_ffn2[0], w3b=w3_ffn2[0], w2b=w2_ffn2[0])
    col_sharded = ("w1a", "w3a", "w_in", "w1b", "w3b")
    full = {}

    def gather(keys):
        return _GatherTwoLevel([(big[k].T if k in col_sharded else big[k]).astype(BF16) for k in keys])

    def unpack(keys, gathered):
        for k, gth in zip(keys, gathered):
            full[k] = gth.reshape(-1, gth.shape[-1])

    def share(k):
        gk_ = grads[k]
        if k in col_sharded:
            return _scatter_cols(gk_)
        return gk_.reshape(N_DEV, gk_.shape[0] // N_DEV, gk_.shape[1]).astype(BF16)

    unpack(("w1a", "w3a"), _run_comm(gather(("w1a", "w3a")), "gather_ffn1"))

    c_all = _run_comm(_Direct([c], True), "gather_c")[0].reshape(N_DEV, d)
    b_cols = lax.dynamic_slice(b_ada, (0, me * MOD_COLS), (1, MOD_COLS))
    act_all, mod_cols = _ada_fwd(c_all, w_ada[0], b_cols, "ada_fwd")
    mod = _run_comm(_Direct([mod_cols.reshape(N_DEV, 1, MOD_COLS)], False), "scatter_mod")[0]
    mod = mod.reshape(N_MOD, 1, d)
    sh1, sc1, gt1, sh2, sc2, gt2, sh3, sc3, gt3 = [mod[i] for i in range(N_MOD)]

    tm = min(FFN_ROWS, t)
    h1 = _norm_fwd(x0, None, None, 0.0, g_ffn1, sc1, sh1, "norm1_fwd")
    (ga1, gb1, s1), gathered = _ffn_up(h1, full["w1a"], full["w3a"], "ffn1_up",
                                     comm=gather(("w2a", "w_in", "w_out")), relay_at=0.6)
    unpack(("w2a", "w_in", "w_out"), gathered)
    f1, gathered = _matmul(s1, full["w2a"], "nn", tm, 512, D_FF, F32, "ffn1_down", comm=gather(("w1b",)),
                           relay_at=0.7)
    unpack(("w1b",), gathered)
    x1, h2 = _norm_fwd(x0, f1, gt1, 0.5, g_mix, sc2, sh2, "norm2_fwd")
    z = _matmul(h2, full["w_in"], "nt", tm, IN_COLS // 2, d, F32, "mix_in_fwd")
    onehot = _bucket_onehot()
    bias = _select_matmul(rel_bias.T, jnp.asarray(onehot, BF16), "bias_table").reshape(B_HEADS, BLOCK, 2 * BLOCK)
    sb_t = spatial_b[0].T
    mix_params = (spatial_w[0], sb_t, g_v[0], g_q, g_k, sinks, bias)
    ycat, gathered = _mix_fwd(z, *mix_params, "mix_core_fwd", comm=gather(("w3b",)), relay_at=0.6)
    unpack(("w3b",), gathered)
    y = _matmul(ycat, full["w_out"], "nn", 512, d, d, F32, "mix_out_fwd")
    x2, h3 = _norm_fwd(x1, y, gt2, 1.0, g_ffn2, sc3, sh3, "norm3_fwd")
    (ga2, gb2, s2), gathered = _ffn_up(h3, full["w1b"], full["w3b"], "ffn2_up", comm=gather(("w2b",)), relay_at=0.6)
    unpack(("w2b",), gathered)
    f2 = _matmul(s2, full["w2b"], "nn", tm, 512, D_FF, F32, "ffn2_down")
    dx3, df2, loss_part = _loss_head(x2, f2, gt3, 0.5, target, "loss_head")
    loss = lax.psum(loss_part[0, 0], ("x", "y", "c"))

    grads, parts = {}, {}
    tk = min(DW_TOKENS, t)

    def exchange(k):
        return _Direct([share(k)], False)

    grads["w2b"] = _matmul(s2, df2, "tn", 512, 1024, tk, BF16, "ffn2_dw2")
    (da, db), moved = _ffn_dact(df2, ga2, gb2, full["w2b"], "ffn2_dact", comm=exchange("w2b"))
    parts["w2b"] = moved[0]
    grads["w1b"] = _matmul(h3, da, "tn", 1024, 512, tk, BF16, "ffn2_dw1")
    grads["w3b"], moved = _matmul(h3, db, "tn", 1024, 512, tk, BF16, "ffn2_dw3", comm=exchange("w1b"))
    parts["w1b"] = moved[0]
    dh3, moved = _ffn_dh(da, db, full["w1b"], full["w3b"], "ffn2_dh", comm=exchange("w3b"))
    parts["w3b"] = moved[0]
    dx2, dyg, sums3 = _norm_bwd(dh3, x2, dx3, f2, 0.5, g_ffn2, sc3, gt2, 1.0, "norm3_bwd")
    dycat = _matmul(dyg, full["w_out"], "nt", 512, d, d, F32, "mix_out_bwd")
    grads["w_out"] = _matmul(ycat, dyg, "tn", 1024, 512, tk, BF16, "mix_out_dw")
    mix_grads, moved = _mix_bwd(z, dycat, *mix_params, "mix_core_bwd", comm=exchange("w_out"))
    parts["w_out"] = moved[0]
    dz, d_sw, d_sb, d_gv, d_gq, d_gk, d_sink, d_bias = mix_grads
    d_rel = _select_matmul(d_bias.reshape(B_HEADS, BLOCK * 2 * BLOCK), jnp.asarray(onehot.T, BF16), "bias_table_bwd")
    dh2 = _matmul(dz, full["w_in"], "nn", tm, d, IN_COLS // 2, F32, "mix_in_bwd")
    grads["w_in"] = _matmul(h2, dz, "tn", 512, IN_COLS // 2, tk, BF16, "mix_in_dw")
    dx1, df1, sums2 = _norm_bwd(dh2, x1, dx2, y, 1.0, g_mix, sc2, gt1, 0.5, "norm2_bwd")
    grads["w2a"], moved = _matmul(s1, df1, "tn", 512, 1024, tk, BF16, "ffn1_dw2", comm=exchange("w_in"))
    parts["w_in"] = moved[0]
    (da, db), moved = _ffn_dact(df1, ga1, gb1, full["w2a"], "ffn1_dact", comm=exchange("w2a"))
    parts["w2a"] = moved[0]
    grads["w1a"] = _matmul(h1, da, "tn", 1024, 512, tk, BF16, "ffn1_dw1")
    grads["w3a"], moved = _matmul(h1, db, "tn", 1024, 512, tk, BF16, "ffn1_dw3", comm=exchange("w1a"))
    parts["w1a"] = moved[0]
    dh1, moved = _ffn_dh(da, db, full["w1a"], full["w3a"], "ffn1_dh", comm=exchange("w3a"))
    parts["w3a"] = moved[0]
    dx0, sums1 = _norm_bwd(dh1, x0, dx1, f1, 0.5, g_ffn1, sc1, None, 0.0, "norm1_bwd")

    moments = dict(w1a=(m_w1_ffn1, v_w1_ffn1), w3a=(m_w3_ffn1, v_w3_ffn1), w2a=(m_w2_ffn1, v_w2_ffn1),
                   w_in=(m_w_in, v_w_in), w_out=(m_w_out, v_w_out),
                   w1b=(m_w1_ffn2, v_w1_ffn2), w3b=(m_w3_ffn2, v_w3_ffn2), w2b=(m_w2_ffn2, v_w2_ffn2))
    upd = {}
    for k in big:
        rows = big[k].shape[0]
        tr = 256 if rows % 256 == 0 else rows // 4
        upd[k] = [o[None] for o in _adam_reduce(parts[k], big[k], moments[k][0][0], moments[k][1][0], "adam_" + k, tr)]

    dmod = jnp.concatenate([sums1[1:2], sums1[0:1], sums1[3:4], sums2[1:2], sums2[0:1], sums2[3:4],
                            sums3[1:2], sums3[0:1], sums3[3:4]], axis=1)
    small = [("b_ada", dmod, b_ada, m_b_ada, v_b_ada),
             ("g_ffn1", sums1[2:3], g_ffn1, m_g_ffn1, v_g_ffn1),
             ("g_mix", sums2[2:3], g_mix, m_g_mix, v_g_mix),
             ("spatial_w", d_sw[None], spatial_w, m_spatial_w, v_spatial_w),
             ("spatial_b", d_sb[:, :A_HEADS].T[None], spatial_b, m_spatial_b, v_spatial_b),
             ("g_v", d_gv[None], g_v, m_g_v, v_g_v),
             ("g_q", d_gq, g_q, m_g_q, v_g_q),
             ("g_k", d_gk, g_k, m_g_k, v_g_k),
             ("sinks", d_sink[:, :B_HEADS], sinks, m_sinks, v_sinks),
             ("rel_bias", d_rel.T, rel_bias, m_rel_bias, v_rel_bias),
             ("g_ffn2", sums3[2:3], g_ffn2, m_g_ffn2, v_g_ffn2)]
    packed = [jnp.concatenate([_pad_rows(item[i]) for item in small], axis=0) for i in range(1, 5)]
    g_all = _run_comm(_Direct([packed[0]], True), "gather_small")[0]
    small_out = _adam_reduce(g_all, packed[1], packed[2], packed[3], "adam_small", packed[0].shape[0])
    off = 0
    for name, g_, w_, _, _ in small:
        n_rows = _pad_rows(w_).shape[0]
        upd[name] = [o[off:off + n_rows].reshape(-1)[:w_.size].reshape(w_.shape) for o in small_out]
        off += n_rows

    dmod_rows = g_all[:, :N_MOD * d // BLOCK, :].reshape(N_DEV, N_MOD * d)
    dmod_cols = lax.dynamic_slice(dmod_rows, (0, me * MOD_COLS), (N_DEV, MOD_COLS))
    upd["w_ada"] = [o[None] for o in _adam_outer(act_all.T, dmod_cols, w_ada[0], m_w_ada[0], v_w_ada[0], "adam_w_ada")]

    order = [("w_ada", "w_ada"), ("b_ada", "b_ada"), ("g_ffn1", "g_ffn1"), ("w1_ffn1", "w1a"), ("w3_ffn1", "w3a"),
             ("w2_ffn1", "w2a"), ("g_mix", "g_mix"), ("w_in", "w_in"), ("spatial_w", "spatial_w"),
             ("spatial_b", "spatial_b"), ("g_v", "g_v"), ("g_q", "g_q"), ("g_k", "g_k"), ("sinks", "sinks"),
             ("rel_bias", "rel_bias"), ("w_out", "w_out"), ("g_ffn2", "g_ffn2"), ("w1_ffn2", "w1b"),
             ("w3_ffn2", "w3b"), ("w2_ffn2", "w2b")]
    outs = [loss, dx0[None]]
    for i in range(4):
        outs += [upd[key][i] for _, key in order]
    return tuple(outs)
```

```python
import functools
import math

import numpy as np
import jax
import jax.numpy as jnp
from jax import lax
from jax.experimental import pallas as pl
from jax.experimental.pallas import tpu as pltpu

F32 = jnp.float32
BF16 = jnp.bfloat16

D_MODEL = 2048
D_FF = 5632
BLOCK = 128
A_HEADS = 8
A_WIDTH = 1024
B_HEADS = 16
B_KV_HEADS = 2
GROUP = B_HEADS // B_KV_HEADS
HEAD_DIM = 64
B_WIDTH = 1024
KV_WIDTH = 128
IN_COLS = 3328
Q_OFF = 2 * A_WIDTH
K_OFF = Q_OFF + B_WIDTH
V_OFF = K_OFF + KV_WIDTH
N_BUCKETS = 32
N_MOD = 9
EPS = 1e-6
N_DEV = 8
MOD_COLS = N_MOD * D_MODEL // N_DEV

ADAM_LR = 0.001
ADAM_B1 = 0.9
ADAM_B2 = 0.999
ADAM_EPS = 1e-08
ADAM_WD = 0.01
ADAM_STEP = 10

FFN_ROWS = 1024
DW_TOKENS = 4096
MASK_VALUE = -1e30
VMEM_LIMIT = 56 * 1024 * 1024
MESH_ID = pl.DeviceIdType.MESH
ANY = pl.BlockSpec(memory_space=pl.ANY)

_SQRT_HALF = 0.7071067811865476
_INV_SQRT_2PI = 0.3989422804014327


def _pc(body, **kw):
    return pl.pallas_call(body, **kw)


def _params(sem=None):
    return pltpu.CompilerParams(dimension_semantics=sem, vmem_limit_bytes=VMEM_LIMIT)


def _dot(a, b):
    return lax.dot_general(a, b, (((1,), (0,)), ((), ())), preferred_element_type=F32)


def _dot_nt(a, b):
    return lax.dot_general(a, b, (((1,), (1,)), ((), ())), preferred_element_type=F32)


def _dot_tn(a, b):
    return lax.dot_general(a, b, (((0,), (0,)), ((), ())), preferred_element_type=F32)


def _gelu(x):
    return 0.5 * x * (1.0 + lax.erf(x * _SQRT_HALF))


def _gelu_and_grad(x):
    cdf = 0.5 * (1.0 + lax.erf(x * _SQRT_HALF))
    return x * cdf, cdf + x * jnp.exp(-0.5 * x * x) * _INV_SQRT_2PI


def _rms(x):
    r = lax.rsqrt(jnp.mean(x * x, axis=-1, keepdims=True) + EPS)
    return x * r, r


def _rms_bwd(dy, y, r):
    return r * (dy - y * jnp.mean(dy * y, axis=-1, keepdims=True))


def _row_spec(tm, cols):
    return pl.BlockSpec((tm, cols), lambda i: (i, 0))


def _vec_spec(cols):
    return pl.BlockSpec((1, cols), lambda i: (0, 0))


def _position():
    x, y, c = lax.axis_index("x"), lax.axis_index("y"), lax.axis_index("c")
    return x, y, c


def _linear(p):
    return 4 * p[0] + 2 * p[1] + p[2]


class _Comm:
    PEER_COPIES = N_DEV - 1

    def __init__(self, arrs):
        self.arrs = list(arrs)
        n = len(self.arrs)
        self.scratch_shapes = [pltpu.SemaphoreType.DMA((self.PEER_COPIES * n,)),
                               pltpu.SemaphoreType.DMA((self.PEER_COPIES * n,)),
                               pltpu.SemaphoreType.DMA((n,))]

    def bind(self, srcs, dsts, sems):
        self.srcs, self.dsts = srcs, dsts
        self.send_sems, self.recv_sems, self.local_sems = sems
        x, y, c = _position()
        self.me, self.sibling, self.core = (x, y, c), (x, y, 1 - c), c
        self.chips = [(1 - x, y), (x, 1 - y), (1 - x, 1 - y)]
        self.peers = [(1 - x if k & 4 else x, 1 - y if k & 2 else y, 1 - c if k & 1 else c)
                      for k in range(1, N_DEV)]

    def relay(self):
        pass


class _GatherTwoLevel(_Comm):
    def __init__(self, arrs):
        super().__init__(arrs)
        self.out_shape = [jax.ShapeDtypeStruct((N_DEV,) + a.shape, a.dtype) for a in self.arrs]

    def _copy(self, a, k, block, to, from_input=False):
        rows = self.dsts[a].at[_linear(block)]
        return pltpu.make_async_remote_copy(
            src_ref=self.srcs[a] if from_input else rows, dst_ref=rows,
            send_sem=self.send_sems.at[self.PEER_COPIES * a + k], recv_sem=self.recv_sems.at[self.PEER_COPIES * a + k],
            device_id=to, device_id_type=MESH_ID)

    def _local(self, a):
        return pltpu.make_async_copy(self.srcs[a], self.dsts[a].at[_linear(self.me)], self.local_sems.at[a])

    def _first(self, a):
        return [self._copy(a, 0, self.me, self.sibling, True)] + [
            self._copy(a, 1 + j, self.me, (*chip, self.core), True) for j, chip in enumerate(self.chips)]

    def _passed(self, a):
        return [self._copy(a, 4 + j, (*chip, self.core), self.sibling) for j, chip in enumerate(self.chips)]

    def start(self):
        for a in range(len(self.arrs)):
            self._local(a).start()
            for cp in self._first(a):
                cp.start()

    def relay(self):
        for a in range(len(self.arrs)):
            for j, chip in enumerate(self.chips):
                self._copy(a, 1 + j, (*chip, self.core), self.me).wait_recv()
                self._passed(a)[j].start()

    def finish(self):
        for a in range(len(self.arrs)):
            self._copy(a, 0, self.sibling, self.me).wait_recv()
            for j, chip in enumerate(self.chips):
                self._copy(a, 4 + j, (*chip, 1 - self.core), self.me).wait_recv()
        for a in range(len(self.arrs)):
            for cp in self._first(a) + self._passed(a):
                cp.wait_send()
            self._local(a).wait()


class _Direct(_Comm):
    def __init__(self, arrs, broadcast):
        super().__init__(arrs)
        self.broadcast = broadcast
        self.out_shape = [jax.ShapeDtypeStruct(((N_DEV,) + a.shape) if broadcast else a.shape, a.dtype)
                          for a in self.arrs]

    def _outgoing(self, a, to):
        return self.srcs[a] if self.broadcast else self.srcs[a].at[_linear(to)]

    def _copy(self, a, k, sender, to):
        return pltpu.make_async_remote_copy(
            src_ref=self._outgoing(a, to), dst_ref=self.dsts[a].at[_linear(sender)],
            send_sem=self.send_sems.at[self.PEER_COPIES * a + k], recv_sem=self.recv_sems.at[self.PEER_COPIES * a + k],
            device_id=to, device_id_type=MESH_ID)

    def _local(self, a):
        return pltpu.make_async_copy(self._outgoing(a, self.me), self.dsts[a].at[_linear(self.me)],
                                     self.local_sems.at[a])

    def start(self):
        for a in range(len(self.arrs)):
            self._local(a).start()
            for k, peer in enumerate(self.peers):
                self._copy(a, k, self.me, peer).start()

    def finish(self):
        for a in range(len(self.arrs)):
            for k, peer in enumerate(self.peers):
                self._copy(a, k, peer, self.me).wait_recv()
        for a in range(len(self.arrs)):
            for k, peer in enumerate(self.peers):
                self._copy(a, k, self.me, peer).wait_send()
            self._local(a).wait()


def _run_comm(comm, name):
    n = len(comm.arrs)

    def body(*refs):
        comm.bind(refs[:n], refs[n:2 * n], refs[2 * n:])
        comm.start()
        comm.relay()
        comm.finish()

    return _pc(body, name=name, out_shape=comm.out_shape, in_specs=[ANY] * n, out_specs=[ANY] * n,
               scratch_shapes=comm.scratch_shapes)(*comm.arrs)


def _call(body, inputs, *, name, grid, in_specs, out_specs, out_shape, scratch_shapes=(), semantics,
          comm=None, relay_at=0.5):
    if comm is None:
        out = _pc(body, name=name, grid=grid, in_specs=in_specs, out_specs=out_specs, out_shape=out_shape,
                  scratch_shapes=list(scratch_shapes), compiler_params=_params(semantics))(*inputs)
        return list(out), []
    n_in, n_out, n_sc, k = len(in_specs), len(out_specs), len(scratch_shapes), len(comm.arrs)
    steps = math.prod(grid)
    relay_step = min(steps - 1, int(steps * relay_at))

    def carrier(*refs):
        ins, rest = refs[:n_in], refs[n_in:]
        csrc, rest = rest[:k], rest[k:]
        outs, rest = rest[:n_out], rest[n_out:]
        cdst, rest = rest[:k], rest[k:]
        scratch, csems = rest[:n_sc], rest[n_sc:]
        step = 0
        for axis, size in enumerate(grid):
            step = step * size + pl.program_id(axis)
        comm.bind(csrc, cdst, csems)
        pl.when(step == 0)(comm.start)
        pl.when(step == relay_step)(comm.relay)
        body(*ins, *outs, *scratch)
        pl.when(step == steps - 1)(comm.finish)

    out = _pc(carrier, name=name, grid=grid, in_specs=list(in_specs) + [ANY] * k,
              out_specs=list(out_specs) + [ANY] * k, out_shape=list(out_shape) + comm.out_shape,
              scratch_shapes=list(scratch_shapes) + comm.scratch_shapes,
              compiler_params=_params(("arbitrary",) * len(grid)))(*inputs, *comm.arrs)
    return list(out[:n_out]), list(out[n_out:])


def _matmul(a, b, mode, tm, tn, tk, out_dtype, name, comm=None, relay_at=0.5):
    if mode == "nn":
        (m, kk), nn = a.shape, b.shape[1]
        a_spec = pl.BlockSpec((tm, tk), lambda i, j, k: (i, k))
        b_spec = pl.BlockSpec((tk, tn), lambda i, j, k: (k, j))
        dot = _dot
    elif mode == "nt":
        (m, kk), nn = a.shape, b.shape[0]
        a_spec = pl.BlockSpec((tm, tk), lambda i, j, k: (i, k))
        b_spec = pl.BlockSpec((tn, tk), lambda i, j, k: (j, k))
        dot = _dot_nt
    else:
        (kk, m), nn = a.shape, b.shape[1]
        a_spec = pl.BlockSpec((tk, tm), lambda i, j, k: (k, i))
        b_spec = pl.BlockSpec((tk, tn), lambda i, j, k: (k, j))
        dot = _dot_tn
    assert m % tm == 0 and nn % tn == 0 and kk % tk == 0, (a.shape, b.shape, tm, tn, tk)
    nk = kk // tk
    narrow = nk > 1 and out_dtype != F32

    def body(a_ref, b_ref, o_ref, *acc):
        p = dot(a_ref[...], b_ref[...])
        if nk == 1:
            o_ref[...] = p.astype(o_ref.dtype)
            return
        k = pl.program_id(2)
        acc_ref = acc[0] if narrow else o_ref

        @pl.when(k == 0)
        def _():
            acc_ref[...] = p

        @pl.when((k > 0) & (k < nk - 1) if narrow else k > 0)
        def _():
            acc_ref[...] += p

        if narrow:
            @pl.when(k == nk - 1)
            def _():
                o_ref[...] = (acc_ref[...] + p).astype(o_ref.dtype)

    out, moved = _call(
        body, (a, b), name=name, grid=(m // tm, nn // tn, nk),
        in_specs=[a_spec, b_spec], out_specs=[pl.BlockSpec((tm, tn), lambda i, j, k: (i, j))],
        out_shape=[jax.ShapeDtypeStruct((m, nn), out_dtype)],
        scratch_shapes=[pltpu.VMEM((tm, tn), F32)] if narrow else [],
        semantics=("parallel", "parallel", "arbitrary"), comm=comm, relay_at=relay_at)
    return out[0] if comm is None else (out[0], moved)


def _norm_fwd(x_prev, f_prev, gate_prev, coef, g, sc, sh, name, tm=256):
    t, d = x_prev.shape
    residual = f_prev is not None

    def body(*refs):
        if residual:
            xp_ref, fp_ref, gp_ref, g_ref, sc_ref, sh_ref, x_ref, h_ref = refs
            x = xp_ref[...] + coef * gp_ref[...] * fp_ref[...]
            x_ref[...] = x
        else:
            xp_ref, g_ref, sc_ref, sh_ref, h_ref = refs
            x = xp_ref[...]
        y, _ = _rms(x)
        h_ref[...] = ((y * g_ref[...]) * (1.0 + sc_ref[...]) + sh_ref[...]).astype(BF16)

    row, vec = _row_spec(tm, d), _vec_spec(d)
    if residual:
        ins, in_specs = (x_prev, f_prev, gate_prev, g, sc, sh), [row, row, vec, vec, vec, vec]
        out_shape = [jax.ShapeDtypeStruct((t, d), F32), jax.ShapeDtypeStruct((t, d), BF16)]
        out_specs = [row, row]
    else:
        ins, in_specs = (x_prev, g, sc, sh), [row, vec, vec, vec]
        out_shape = [jax.ShapeDtypeStruct((t, d), BF16)]
        out_specs = [row]
    out = _pc(body, name=name, grid=(t // tm,), in_specs=in_specs, out_specs=out_specs, out_shape=out_shape,
              compiler_params=_params(("parallel",)))(*ins)
    return out if residual else out[0]


def _norm_bwd(dh, x, dxo, fo, coef_o, g, sc, gate_prev, coef_prev, name, tm=256, comm=None):
    t, d = x.shape
    with_prev = gate_prev is not None

    def body(*refs):
        if with_prev:
            dh_ref, x_ref, dxo_ref, fo_ref, g_ref, sc_ref, gp_ref, dx_ref, dfp_ref, sums_ref = refs
        else:
            dh_ref, x_ref, dxo_ref, fo_ref, g_ref, sc_ref, dx_ref, sums_ref = refs
        dh_v, dxo_v = dh_ref[...], dxo_ref[...]
        y, r = _rms(x_ref[...])
        n = y * g_ref[...]
        dn = dh_v * (1.0 + sc_ref[...])
        dx = dxo_v + _rms_bwd(dn * g_ref[...], y, r)
        dx_ref[...] = dx
        if with_prev:
            dfp_ref[...] = (coef_prev * gp_ref[...] * dx).astype(BF16)
        @pl.when(pl.program_id(0) == 0)
        def _():
            sums_ref[...] = jnp.zeros_like(sums_ref)

        sums_ref[0:1, :] += jnp.sum(dh_v * n, axis=0, keepdims=True)
        sums_ref[1:2, :] += jnp.sum(dh_v, axis=0, keepdims=True)
        sums_ref[2:3, :] += jnp.sum(dn * y, axis=0, keepdims=True)
        sums_ref[3:4, :] += jnp.sum(coef_o * dxo_v * fo_ref[...], axis=0, keepdims=True)

    row, vec = _row_spec(tm, d), _vec_spec(d)
    sums_spec = pl.BlockSpec((8, d), lambda i: (0, 0))
    ins, in_specs = [dh, x, dxo, fo, g, sc], [row, row, row, row, vec, vec]
    out_shape, out_specs = [jax.ShapeDtypeStruct((t, d), F32)], [row]
    if with_prev:
        ins.append(gate_prev)
        in_specs.append(vec)
        out_shape.append(jax.ShapeDtypeStruct((t, d), BF16))
        out_specs.append(row)
    out_shape.append(jax.ShapeDtypeStruct((8, d), F32))
    out_specs.append(sums_spec)
    out, moved = _call(body, ins, name=name, grid=(t // tm,), in_specs=in_specs, out_specs=out_specs,
                       out_shape=out_shape, semantics=("arbitrary",), comm=comm)
    return out if comm is None else (out, moved)


def _ffn_up(h, w1t, w3t, name, tf=512, comm=None, relay_at=0.5):
    t, d = h.shape
    ff = w1t.shape[0]
    tm = min(FFN_ROWS, t)

    def body(h_ref, w1_ref, w3_ref, ga_ref, gb_ref, s_ref):
        hv = h_ref[...]
        a = _dot_nt(hv, w1_ref[...])
        b = _dot_nt(hv, w3_ref[...])
        sg = jax.nn.sigmoid(a)
        sil = a * sg
        ga_ref[...] = (b * (sg * (1.0 + a * (1.0 - sg)))).astype(BF16)
        gb_ref[...] = sil.astype(BF16)
        s_ref[...] = (sil * b).astype(BF16)

    tile = pl.BlockSpec((tm, tf), lambda i, j: (i, j))
    cols = pl.BlockSpec((tf, d), lambda i, j: (j, 0))
    out, moved = _call(
        body, (h, w1t, w3t), name=name, grid=(t // tm, ff // tf),
        in_specs=[pl.BlockSpec((tm, d), lambda i, j: (i, 0)), cols, cols], out_specs=[tile, tile, tile],
        out_shape=[jax.ShapeDtypeStruct((t, ff), BF16)] * 3,
        semantics=("parallel", "parallel"), comm=comm, relay_at=relay_at)
    return out if comm is None else (out, moved)


def _ffn_dact(df, ga, gb, w2, name, tf=1408, comm=None):
    t, d = df.shape
    ff = ga.shape[1]
    tm = min(FFN_ROWS, t)

    def body(df_ref, ga_ref, gb_ref, w2_ref, da_ref, db_ref):
        ds = _dot_nt(df_ref[...], w2_ref[...])
        da_ref[...] = (ds * ga_ref[...]).astype(BF16)
        db_ref[...] = (ds * gb_ref[...]).astype(BF16)

    tile = pl.BlockSpec((tm, tf), lambda i, j: (i, j))
    out, moved = _call(
        body, (df, ga, gb, w2), name=name, grid=(t // tm, ff // tf),
        in_specs=[pl.BlockSpec((tm, d), lambda i, j: (i, 0)), tile, tile, pl.BlockSpec((tf, d), lambda i, j: (j, 0))],
        out_specs=[tile, tile], out_shape=[jax.ShapeDtypeStruct((t, ff), BF16)] * 2,
        semantics=("parallel", "parallel"), comm=comm)
    return out if comm is None else (out, moved)


def _ffn_dh(da, db, w1t, w3t, name, tn=512, comm=None):
    t, ff = da.shape
    d = w1t.shape[1]
    tm, tk = min(FFN_ROWS, t), ff // 2

    def body(da_ref, db_ref, w1_ref, w3_ref, dh_ref):
        p = _dot(da_ref[...], w1_ref[...]) + _dot(db_ref[...], w3_ref[...])
        k = pl.program_id(2)

        @pl.when(k == 0)
        def _():
            dh_ref[...] = p

        @pl.when(k > 0)
        def _():
            dh_ref[...] += p

    act = pl.BlockSpec((tm, tk), lambda i, j, k: (i, k))
    wgt = pl.BlockSpec((tk, tn), lambda i, j, k: (k, j))
    out, moved = _call(
        body, (da, db, w1t, w3t), name=name, grid=(t // tm, d // tn, 2),
        in_specs=[act, act, wgt, wgt], out_specs=[pl.BlockSpec((tm, tn), lambda i, j, k: (i, j))],
        out_shape=[jax.ShapeDtypeStruct((t, d), F32)],
        semantics=("parallel", "parallel", "arbitrary"), comm=comm)
    return out[0] if comm is None else (out[0], moved)


def _loss_head(x_prev, f_prev, gate_prev, coef, target, name, tm=256):
    t, d = x_prev.shape
    steps = t // tm

    def body(xp_ref, fp_ref, gp_ref, tg_ref, dy_ref, df_ref, loss_ref, acc_ref):
        i = pl.program_id(0)
        e = xp_ref[...] + coef * gp_ref[...] * fp_ref[...] - tg_ref[...]
        dy = e * (1.0 / d)
        dy_ref[...] = dy
        df_ref[...] = (coef * gp_ref[...] * dy).astype(BF16)
        part = jnp.sum(e * e, axis=0, keepdims=True)

        @pl.when(i == 0)
        def _():
            acc_ref[...] = part

        @pl.when(i > 0)
        def _():
            acc_ref[...] += part

        @pl.when(i == steps - 1)
        def _():
            loss_ref[...] = jnp.sum(acc_ref[...], axis=1, keepdims=True) * (0.5 / d)

    row, vec = _row_spec(tm, d), _vec_spec(d)
    return _pc(
        body, name=name, grid=(steps,),
        in_specs=[row, row, vec, row], out_specs=[row, row, pl.BlockSpec((1, 1), lambda i: (0, 0))],
        out_shape=[jax.ShapeDtypeStruct((t, d), F32), jax.ShapeDtypeStruct((t, d), BF16),
                   jax.ShapeDtypeStruct((1, 1), F32)],
        scratch_shapes=[pltpu.VMEM((1, d), F32)],
        compiler_params=_params(("arbitrary",)),
    )(x_prev, f_prev, gate_prev, target)


def _split3(x):
    hi = x.astype(BF16)
    r1 = x - hi.astype(F32)
    mid = r1.astype(BF16)
    lo = (r1 - mid.astype(F32)).astype(BF16)
    return hi, mid, lo


def _select_matmul(a, onehot, name):
    m, n = a.shape[0], onehot.shape[1]

    def body(a_ref, oh_ref, o_ref):
        hi, mid, lo = _split3(a_ref[...])
        oh = oh_ref[...]
        o_ref[...] = (_dot(hi, oh) + _dot(mid, oh)) + _dot(lo, oh)

    return _pc(body, name=name, out_shape=jax.ShapeDtypeStruct((m, n), F32), compiler_params=_params())(a, onehot)


def _bucket_onehot():
    qi = np.arange(BLOCK)[:, None]
    kj = np.arange(2 * BLOCK)[None, :]
    dist = np.clip(qi + BLOCK - kj, 0, None)
    nf = np.maximum(dist, 1).astype(np.float32)
    large = 16 + (np.log(nf / np.float32(16)) / np.float32(math.log(128 / 16)) * np.float32(16)).astype(np.int32)
    bucket = np.where(dist < 16, dist, np.minimum(large, N_BUCKETS - 1)).reshape(-1)
    return (bucket[None, :] == np.arange(N_BUCKETS)[:, None]).astype(np.float32)


def _window_mask(n):
    row = lax.broadcasted_iota(jnp.int32, (GROUP * BLOCK, 2 * BLOCK), 0)
    qi = row & (BLOCK - 1)
    kj = lax.broadcasted_iota(jnp.int32, (GROUP * BLOCK, 2 * BLOCK), 1)
    return (kj > qi) & (kj <= qi + BLOCK) & ((kj >= BLOCK) | (n > 0))


def _kv_band(zc_ref, zp_ref, kvh):
    lo, hi = kvh * HEAD_DIM, (kvh + 1) * HEAD_DIM
    k_raw = jnp.concatenate([zp_ref[:, lo:hi], zc_ref[:, K_OFF + lo:K_OFF + hi]], axis=0)
    v_raw = jnp.concatenate([zp_ref[:, KV_WIDTH + lo:KV_WIDTH + hi], zc_ref[:, V_OFF + lo:V_OFF + hi]], axis=0)
    return k_raw, v_raw


def _stack_heads(ref, off):
    return jnp.concatenate([ref[:, off + g * HEAD_DIM:off + (g + 1) * HEAD_DIM] for g in range(GROUP)], axis=0)


def _unstack_heads(ref, off, stacked):
    for g in range(GROUP):
        ref[:, off + g * HEAD_DIM:off + (g + 1) * HEAD_DIM] = stacked[g * BLOCK:(g + 1) * BLOCK]


def _group_softmax(qn, kband, bias_ref, sink_ref, kvh, valid):
    bias = bias_ref[kvh * GROUP:(kvh + 1) * GROUP].reshape(GROUP * BLOCK, 2 * BLOCK)
    s = _dot_nt(qn, kband) * (HEAD_DIM ** -0.5) + bias
    s = jnp.where(valid, s, MASK_VALUE)
    sink = jnp.concatenate([jnp.full((BLOCK, 1), sink_ref[0, kvh * GROUP + g], F32) for g in range(GROUP)], axis=0)
    m = jnp.maximum(jnp.max(s, axis=-1, keepdims=True), sink)
    p = jnp.exp(s - m)
    esink = jnp.exp(sink - m)
    inv = 1.0 / (jnp.sum(p, axis=-1, keepdims=True) + esink)
    return p * inv, esink * inv, inv


def _mix_specs(nb, order):
    cur = pl.BlockSpec((BLOCK, IN_COLS), lambda n: (order(n), 0))
    prev = pl.BlockSpec((BLOCK, 2 * KV_WIDTH), lambda n: (jnp.maximum(order(n) - 1, 0), K_OFF // (2 * KV_WIDTH)))
    full = lambda shape: pl.BlockSpec(shape, lambda n: (0,) * len(shape))
    params = [full((A_HEADS, BLOCK, BLOCK)), full((BLOCK, A_HEADS)), full((A_HEADS, BLOCK)),
              full((1, HEAD_DIM)), full((1, HEAD_DIM)), pl.BlockSpec(memory_space=pltpu.SMEM),
              full((B_HEADS, BLOCK, 2 * BLOCK))]
    return cur, prev, params, full


def _mix_fwd(z, sw, sb_t, gv, gq, gk, sinks, bias, name, comm=None, relay_at=0.5):
    t = z.shape[0]
    nb = t // BLOCK

    def body(zc_ref, zp_ref, sw_ref, sbt_ref, gv_ref, gq_ref, gk_ref, sink_ref, bias_ref, y_ref):
        n = pl.program_id(0)
        ri = lax.broadcasted_iota(jnp.int32, (BLOCK, BLOCK), 0)
        ci = lax.broadcasted_iota(jnp.int32, (BLOCK, BLOCK), 1)
        tril = ri >= ci
        for h in range(A_HEADS):
            u = _gelu(zc_ref[:, h * BLOCK:(h + 1) * BLOCK])
            vv = _gelu(zc_ref[:, A_WIDTH + h * BLOCK:A_WIDTH + (h + 1) * BLOCK])
            vhat, _ = _rms(vv)
            vn = (vhat * gv_ref[h:h + 1, :]).astype(BF16)
            w = jnp.where(tril, sw_ref[h], 0.0).astype(BF16)
            mixed = _dot(w, vn) + sbt_ref[:, h:h + 1]
            y_ref[:, h * BLOCK:(h + 1) * BLOCK] = (u * mixed).astype(BF16)
        valid = _window_mask(n)
        for kvh in range(B_KV_HEADS):
            k_raw, v_raw = _kv_band(zc_ref, zp_ref, kvh)
            khat, _ = _rms(k_raw)
            kband = (khat * gk_ref[...]).astype(BF16)
            vband = v_raw.astype(BF16)
            qhat, _ = _rms(_stack_heads(zc_ref, Q_OFF + kvh * GROUP * HEAD_DIM))
            qn = (qhat * gq_ref[...]).astype(BF16)
            w, _, _ = _group_softmax(qn, kband, bias_ref, sink_ref, kvh, valid)
            o = _dot(w.astype(BF16), vband)
            _unstack_heads(y_ref, A_WIDTH + kvh * GROUP * HEAD_DIM, o.astype(BF16))

    cur, prev, params, _ = _mix_specs(nb, lambda n: n)
    out, moved = _call(
        body, (z, z, sw, sb_t, gv, gq, gk, sinks, bias), name=name, grid=(nb,), in_specs=[cur, prev] + params,
        out_specs=[pl.BlockSpec((BLOCK, A_WIDTH + B_WIDTH), lambda n: (n, 0))],
        out_shape=[jax.ShapeDtypeStruct((t, A_WIDTH + B_WIDTH), BF16)],
        semantics=("parallel",), comm=comm, relay_at=relay_at)
    return out[0] if comm is None else (out[0], moved)


def _mix_bwd(z, dy, sw, sb_t, gv, gq, gk, sinks, bias, name, comm=None):
    t = z.shape[0]
    nb = t // BLOCK

    def body(zc_ref, zp_ref, dy_ref, sw_ref, sbt_ref, gv_ref, gq_ref, gk_ref, sink_ref, bias_ref,
             dz_ref, dsw_ref, dsb_ref, dgv_ref, dgq_ref, dgk_ref, dsink_ref, dbias_ref, carry_ref):
        step = pl.program_id(0)
        n = nb - 1 - step

        @pl.when(step == 0)
        def _():
            dsw_ref[...] = jnp.zeros_like(dsw_ref)
            dsb_ref[...] = jnp.zeros_like(dsb_ref)
            dgv_ref[...] = jnp.zeros_like(dgv_ref)
            dgq_ref[...] = jnp.zeros_like(dgq_ref)
            dgk_ref[...] = jnp.zeros_like(dgk_ref)
            dsink_ref[...] = jnp.zeros_like(dsink_ref)
            dbias_ref[...] = jnp.zeros_like(dbias_ref)
            carry_ref[...] = jnp.zeros_like(carry_ref)

        ri = lax.broadcasted_iota(jnp.int32, (BLOCK, BLOCK), 0)
        ci = lax.broadcasted_iota(jnp.int32, (BLOCK, BLOCK), 1)
        tril = ri >= ci
        dsb = jnp.zeros((BLOCK, BLOCK), F32)
        for h in range(A_HEADS):
            uo, vo = h * BLOCK, A_WIDTH + h * BLOCK
            u_raw, v_raw = zc_ref[:, uo:uo + BLOCK], zc_ref[:, vo:vo + BLOCK]
            (u, du_raw), (vv, dv_raw) = _gelu_and_grad(u_raw), _gelu_and_grad(v_raw)
            vhat, r = _rms(vv)
            vn = (vhat * gv_ref[h:h + 1, :]).astype(BF16)
            w = jnp.where(tril, sw_ref[h], 0.0).astype(BF16)
            mixed = _dot(w, vn) + sbt_ref[:, h:h + 1]
            dya = dy_ref[:, uo:uo + BLOCK]
            dmixed = dya * u
            dm16 = dmixed.astype(BF16)
            dsw_ref[h] += jnp.where(tril, _dot_nt(dm16, vn), 0.0)
            dsb = dsb + jnp.where(ci == h, jnp.sum(dmixed, axis=1, keepdims=True), 0.0)
            dvn = _dot_tn(w, dm16)
            dgv_ref[h:h + 1, :] += jnp.sum(dvn * vhat, axis=0, keepdims=True)
            dvv = _rms_bwd(dvn * gv_ref[h:h + 1, :], vhat, r)
            dz_ref[:, uo:uo + BLOCK] = (dya * mixed * du_raw).astype(BF16)
            dz_ref[:, vo:vo + BLOCK] = (dvv * dv_raw).astype(BF16)
        dsb_ref[...] += dsb

        valid = _window_mask(n)
        lane = lax.broadcasted_iota(jnp.int32, (1, BLOCK), 1)
        dsink = jnp.zeros((1, BLOCK), F32)
        dgq = jnp.zeros((1, HEAD_DIM), F32)
        for kvh in range(B_KV_HEADS):
            ko, vo = K_OFF + kvh * HEAD_DIM, V_OFF + kvh * HEAD_DIM
            k_raw, v_raw = _kv_band(zc_ref, zp_ref, kvh)
            khat, kr = _rms(k_raw)
            kband = (khat * gk_ref[...]).astype(BF16)
            vband = v_raw.astype(BF16)
            qo = Q_OFF + kvh * GROUP * HEAD_DIM
            qhat, qr = _rms(_stack_heads(zc_ref, qo))
            qn = (qhat * gq_ref[...]).astype(BF16)
            w, wsink, _ = _group_softmax(qn, kband, bias_ref, sink_ref, kvh, valid)
            do = _stack_heads(dy_ref, A_WIDTH + kvh * GROUP * HEAD_DIM).astype(BF16)
            dvb = _dot_tn(w.astype(BF16), do)
            dw = _dot_nt(do, vband)
            rowdot = jnp.sum(w * dw, axis=-1, keepdims=True)
            ds = w * (dw - rowdot)
            dsink_rows = -wsink * rowdot
            for g in range(GROUP):
                head_sum = jnp.sum(dsink_rows[g * BLOCK:(g + 1) * BLOCK], axis=0, keepdims=True)
                dsink = dsink + jnp.where(lane == kvh * GROUP + g, head_sum, 0.0)
            dbias_ref[kvh * GROUP:(kvh + 1) * GROUP] += ds.reshape(GROUP, BLOCK, 2 * BLOCK)
            ds16 = (ds * (HEAD_DIM ** -0.5)).astype(BF16)
            dqn = _dot(ds16, kband)
            dkn = _dot_tn(ds16, qn)
            dgq = dgq + jnp.sum(dqn * qhat, axis=0, keepdims=True)
            _unstack_heads(dz_ref, qo, _rms_bwd(dqn * gq_ref[...], qhat, qr).astype(BF16))
            dgk_ref[...] += jnp.sum(dkn * khat, axis=0, keepdims=True)
            dk = _rms_bwd(dkn * gk_ref[...], khat, kr)
            co = kvh * HEAD_DIM
            dz_ref[:, ko:ko + HEAD_DIM] = (dk[BLOCK:] + carry_ref[:, co:co + HEAD_DIM]).astype(BF16)
            dz_ref[:, vo:vo + HEAD_DIM] = (dvb[BLOCK:] + carry_ref[:, KV_WIDTH + co:KV_WIDTH + co + HEAD_DIM]).astype(BF16)
            carry_ref[:, co:co + HEAD_DIM] = dk[:BLOCK]
            carry_ref[:, KV_WIDTH + co:KV_WIDTH + co + HEAD_DIM] = dvb[:BLOCK]
        dgq_ref[...] += dgq
        dsink_ref[...] += dsink

    order = lambda n: nb - 1 - n
    cur, prev, params, full = _mix_specs(nb, order)
    dy_spec = pl.BlockSpec((BLOCK, A_WIDTH + B_WIDTH), lambda n: (order(n), 0))
    out, moved = _call(
        body, (z, z, dy, sw, sb_t, gv, gq, gk, sinks, bias), name=name, grid=(nb,),
        in_specs=[cur, prev, dy_spec] + params,
        out_specs=[cur, full((A_HEADS, BLOCK, BLOCK)), full((BLOCK, BLOCK)), full((A_HEADS, BLOCK)),
                   full((1, HEAD_DIM)), full((1, HEAD_DIM)), full((1, BLOCK)), full((B_HEADS, BLOCK, 2 * BLOCK))],
        out_shape=[jax.ShapeDtypeStruct((t, IN_COLS), BF16), jax.ShapeDtypeStruct((A_HEADS, BLOCK, BLOCK), F32),
                   jax.ShapeDtypeStruct((BLOCK, BLOCK), F32), jax.ShapeDtypeStruct((A_HEADS, BLOCK), F32),
                   jax.ShapeDtypeStruct((1, HEAD_DIM), F32), jax.ShapeDtypeStruct((1, HEAD_DIM), F32),
                   jax.ShapeDtypeStruct((1, BLOCK), F32), jax.ShapeDtypeStruct((B_HEADS, BLOCK, 2 * BLOCK), F32)],
        scratch_shapes=[pltpu.VMEM((BLOCK, 2 * KV_WIDTH), F32)],
        semantics=("arbitrary",), comm=comm)
    return out if comm is None else (out, moved)


def _ada_fwd(c_all, w_ada, b_cols, name, tn=768):
    nb, d = c_all.shape
    cols = w_ada.shape[1]

    def body(c_ref, w_ref, b_ref, act_ref, mod_ref):
        cv = c_ref[...]
        act = cv * jax.nn.sigmoid(cv)
        act_ref[...] = act
        mod_ref[...] = _dot(act.astype(BF16), w_ref[...].astype(BF16)) + b_ref[...]

    return _pc(
        body, name=name, grid=(cols // tn,),
        in_specs=[pl.BlockSpec((nb, d), lambda j: (0, 0)), pl.BlockSpec((d, tn), lambda j: (0, j)),
                  pl.BlockSpec((1, tn), lambda j: (0, j))],
        out_specs=[pl.BlockSpec((nb, d), lambda j: (0, 0)), pl.BlockSpec((nb, tn), lambda j: (0, j))],
        out_shape=[jax.ShapeDtypeStruct((nb, d), F32), jax.ShapeDtypeStruct((nb, cols), F32)],
        compiler_params=_params(("arbitrary",)),
    )(c_all, w_ada, b_cols)


def _adamw(w, g, m, v):
    m = ADAM_B1 * m + (1.0 - ADAM_B1) * g
    v = ADAM_B2 * v + (1.0 - ADAM_B2) * (g * g)
    m_hat = m / (1.0 - ADAM_B1 ** ADAM_STEP)
    v_hat = v / (1.0 - ADAM_B2 ** ADAM_STEP)
    delta = -ADAM_LR * (m_hat / (jnp.sqrt(v_hat) + ADAM_EPS) + ADAM_WD * w)
    return delta, m, v


def _adam_outer(act_t, dmod, w, m, v, name, tr=256):
    rows, cols = w.shape
    nb = dmod.shape[0]

    def body(act_ref, dm_ref, w_ref, m_ref, v_ref, g_ref, d_ref, nm_ref, nv_ref):
        act = act_ref[...].astype(BF16).astype(F32)
        dm = dm_ref[...].astype(BF16).astype(F32)
        g = act[:, 0:1] * dm[0:1, :]
        for b in range(1, nb):
            g = g + act[:, b:b + 1] * dm[b:b + 1, :]
        g_ref[...] = g
        d_ref[...], nm_ref[...], nv_ref[...] = _adamw(w_ref[...], g, m_ref[...], v_ref[...])

    tile = pl.BlockSpec((tr, cols), lambda i: (i, 0))
    return _pc(
        body, name=name, grid=(rows // tr,),
        in_specs=[pl.BlockSpec((tr, nb), lambda i: (i, 0)), pl.BlockSpec((nb, cols), lambda i: (0, 0)), tile, tile, tile],
        out_specs=[tile] * 4, out_shape=[jax.ShapeDtypeStruct((rows, cols), F32)] * 4,
        compiler_params=_params(("parallel",)),
    )(act_t, dmod, w, m, v)


def _adam_reduce(parts, w, m, v, name, tr, transposed=False):
    rows, cols = w.shape

    def body(p_ref, w_ref, m_ref, v_ref, g_ref, d_ref, nm_ref, nv_ref):
        g = p_ref[0].astype(F32)
        for i in range(1, N_DEV):
            g = g + p_ref[i].astype(F32)
        if transposed:
            g = g.T
        g_ref[...] = g
        d_ref[...], nm_ref[...], nv_ref[...] = _adamw(w_ref[...], g, m_ref[...], v_ref[...])

    tile = pl.BlockSpec((tr, cols), lambda i: (i, 0))
    parts_spec = (pl.BlockSpec((N_DEV, cols, tr), lambda i: (0, 0, i)) if transposed
                  else pl.BlockSpec((N_DEV, tr, cols), lambda i: (0, i, 0)))
    return _pc(
        body, name=name, grid=(rows // tr,),
        in_specs=[parts_spec, tile, tile, tile],
        out_specs=[tile] * 4, out_shape=[jax.ShapeDtypeStruct((rows, cols), F32)] * 4,
        compiler_params=_params(("parallel",)),
    )(parts, w, m, v)


def _pad_rows(v):
    flat = v.reshape(-1)
    pad = (-flat.shape[0]) % (8 * BLOCK)
    return jnp.pad(flat, (0, pad)).reshape(-1, BLOCK)


def kernel(x, c, w_ada, b_ada, g_ffn1, w1_ffn1, w3_ffn1, w2_ffn1, g_mix, w_in, spatial_w, spatial_b, g_v, g_q, g_k, sinks, rel_bias, w_out, g_ffn2, w1_ffn2, w3_ffn2, w2_ffn2, loss_target, m_w_ada, m_b_ada, m_g_ffn1, m_w1_ffn1, m_w3_ffn1, m_w2_ffn1, m_g_mix, m_w_in, m_spatial_w, m_spatial_b, m_g_v, m_g_q, m_g_k, m_sinks, m_rel_bias, m_w_out, m_g_ffn2, m_w1_ffn2, m_w3_ffn2, m_w2_ffn2, v_w_ada, v_b_ada, v_g_ffn1, v_w1_ffn1, v_w3_ffn1, v_w2_ffn1, v_g_mix, v_w_in, v_spatial_w, v_spatial_b, v_g_v, v_g_q, v_g_k, v_sinks, v_rel_bias, v_w_out, v_g_ffn2, v_w1_ffn2, v_w3_ffn2, v_w2_ffn2):
    me = _linear(_position())
    d = D_MODEL
    x0 = x[0]
    target = loss_target[0]
    t = x0.shape[0]

    big = dict(w1a=w1_ffn1[0], w3a=w3_ffn1[0], w2a=w2_ffn1[0], w_in=w_in[0], w_out=w_out[0],
               w1b=w1_ffn2[0], w3b=w3_ffn2[0], w2b=w2_ffn2[0])
    col_sharded = ("w1a", "w3a", "w_in", "w1b", "w3b")
    full = {}

    def gather(keys):
        return _GatherTwoLevel([(big[k].T if k in col_sharded else big[k]).astype(BF16) for k in keys])

    def unpack(keys, gathered):
        for k, gth in zip(keys, gathered):
            full[k] = gth.reshape(-1, gth.shape[-1])

    def share(k):
        gk_ = grads[k]
        return gk_.reshape(N_DEV, gk_.shape[0] // N_DEV, gk_.shape[1])

    unpack(("w1a", "w3a"), _run_comm(gather(("w1a", "w3a")), "gather_ffn1"))

    c_all = _run_comm(_Direct([c], True), "gather_c")[0].reshape(N_DEV, d)
    b_cols = lax.dynamic_slice(b_ada, (0, me * MOD_COLS), (1, MOD_COLS))
    act_all, mod_cols = _ada_fwd(c_all, w_ada[0], b_cols, "ada_fwd")
    mod = _run_comm(_Direct([mod_cols.reshape(N_DEV, 1, MOD_COLS)], False), "scatter_mod")[0]
    mod = mod.reshape(N_MOD, 1, d)
    sh1, sc1, gt1, sh2, sc2, gt2, sh3, sc3, gt3 = [mod[i] for i in range(N_MOD)]

    tm = min(FFN_ROWS, t)
    h1 = _norm_fwd(x0, None, None, 0.0, g_ffn1, sc1, sh1, "norm1_fwd")
    (ga1, gb1, s1), gathered = _ffn_up(h1, full["w1a"], full["w3a"], "ffn1_up",
                                     comm=gather(("w2a", "w_in", "w_out")), relay_at=0.6)
    unpack(("w2a", "w_in", "w_out"), gathered)
    f1, gathered = _matmul(s1, full["w2a"], "nn", tm, 512, D_FF, F32, "ffn1_down", comm=gather(("w1b",)),
                           relay_at=0.7)
    unpack(("w1b",), gathered)
    x1, h2 = _norm_fwd(x0, f1, gt1, 0.5, g_mix, sc2, sh2, "norm2_fwd")
    z = _matmul(h2, full["w_in"], "nt", tm, IN_COLS // 2, d, F32, "mix_in_fwd")
    onehot = _bucket_onehot()
    bias = _select_matmul(rel_bias.T, jnp.asarray(onehot, BF16), "bias_table").reshape(B_HEADS, BLOCK, 2 * BLOCK)
    sb_t = spatial_b[0].T
    mix_params = (spatial_w[0], sb_t, g_v[0], g_q, g_k, sinks, bias)
    ycat, gathered = _mix_fwd(z, *mix_params, "mix_core_fwd", comm=gather(("w3b",)), relay_at=0.6)
    unpack(("w3b",), gathered)
    y = _matmul(ycat, full["w_out"], "nn", 512, d, d, F32, "mix_out_fwd")
    x2, h3 = _norm_fwd(x1, y, gt2, 1.0, g_ffn2, sc3, sh3, "norm3_fwd")
    (ga2, gb2, s2), gathered = _ffn_up(h3, full["w1b"], full["w3b"], "ffn2_up", comm=gather(("w2b",)), relay_at=0.6)
    unpack(("w2b",), gathered)
    f2 = _matmul(s2, full["w2b"], "nn", tm, 512, D_FF, F32, "ffn2_down")
    dx3, df2, loss_part = _loss_head(x2, f2, gt3, 0.5, target, "loss_head")
    loss = lax.psum(loss_part[0, 0], ("x", "y", "c"))

    grads, parts = {}, {}
    tk = min(DW_TOKENS, t)

    def exchange(k):
        return _Direct([share(k)], False)

    grads["w2b"] = _matmul(s2, df2, "tn", 512, 1024, tk, BF16, "ffn2_dw2")
    (da, db), moved = _ffn_dact(df2, ga2, gb2, full["w2b"], "ffn2_dact", comm=exchange("w2b"))
    parts["w2b"] = moved[0]
    grads["w1b"] = _matmul(da, h3, "tn", 512, 1024, tk, BF16, "ffn2_dw1")
    grads["w3b"], moved = _matmul(db, h3, "tn", 512, 1024, tk, BF16, "ffn2_dw3", comm=exchange("w1b"))
    parts["w1b"] = moved[0]
    dh3, moved = _ffn_dh(da, db, full["w1b"], full["w3b"], "ffn2_dh", comm=exchange("w3b"))
    parts["w3b"] = moved[0]
    dx2, dyg, sums3 = _norm_bwd(dh3, x2, dx3, f2, 0.5, g_ffn2, sc3, gt2, 1.0, "norm3_bwd")
    dycat = _matmul(dyg, full["w_out"], "nt", 512, d, d, F32, "mix_out_bwd")
    grads["w_out"] = _matmul(ycat, dyg, "tn", 1024, 512, tk, BF16, "mix_out_dw")
    mix_grads, moved = _mix_bwd(z, dycat, *mix_params, "mix_core_bwd", comm=exchange("w_out"))
    parts["w_out"] = moved[0]
    dz, d_sw, d_sb, d_gv, d_gq, d_gk, d_sink, d_bias = mix_grads
    d_rel = _select_matmul(d_bias.reshape(B_HEADS, BLOCK * 2 * BLOCK), jnp.asarray(onehot.T, BF16), "bias_table_bwd")
    dh2 = _matmul(dz, full["w_in"], "nn", tm, d, IN_COLS // 2, F32, "mix_in_bwd")
    grads["w_in"] = _matmul(dz, h2, "tn", IN_COLS // 2, 512, tk, BF16, "mix_in_dw")
    dx1, df1, sums2 = _norm_bwd(dh2, x1, dx2, y, 1.0, g_mix, sc2, gt1, 0.5, "norm2_bwd")
    grads["w2a"], moved = _matmul(s1, df1, "tn", 512, 1024, tk, BF16, "ffn1_dw2", comm=exchange("w_in"))
    parts["w_in"] = moved[0]
    (da, db), moved = _ffn_dact(df1, ga1, gb1, full["w2a"], "ffn1_dact", comm=exchange("w2a"))
    parts["w2a"] = moved[0]
    early = [("g_mix", sums2[2:3], g_mix, m_g_mix, v_g_mix),
             ("spatial_w", d_sw[None], spatial_w, m_spatial_w, v_spatial_w),
             ("spatial_b", d_sb[:, :A_HEADS].T[None], spatial_b, m_spatial_b, v_spatial_b),
             ("g_v", d_gv[None], g_v, m_g_v, v_g_v),
             ("g_q", d_gq, g_q, m_g_q, v_g_q),
             ("g_k", d_gk, g_k, m_g_k, v_g_k),
             ("sinks", d_sink[:, :B_HEADS], sinks, m_sinks, v_sinks),
             ("rel_bias", d_rel.T, rel_bias, m_rel_bias, v_rel_bias),
             ("g_ffn2", sums3[2:3], g_ffn2, m_g_ffn2, v_g_ffn2)]

    def pack(items, i):
        return jnp.concatenate([_pad_rows(item[i]) for item in items], axis=0)

    grads["w1a"], moved = _matmul(da, h1, "tn", 512, 1024, tk, BF16, "ffn1_dw1", comm=_Direct([pack(early, 1)], True))
    early_all = moved[0]
    grads["w3a"], moved = _matmul(db, h1, "tn", 512, 1024, tk, BF16, "ffn1_dw3", comm=exchange("w1a"))
    parts["w1a"] = moved[0]
    dh1, moved = _ffn_dh(da, db, full["w1a"], full["w3a"], "ffn1_dh", comm=exchange("w3a"))
    parts["w3a"] = moved[0]
    dx0, sums1 = _norm_bwd(dh1, x0, dx1, f1, 0.5, g_ffn1, sc1, None, 0.0, "norm1_bwd")

    moments = dict(w1a=(m_w1_ffn1, v_w1_ffn1), w3a=(m_w3_ffn1, v_w3_ffn1), w2a=(m_w2_ffn1, v_w2_ffn1),
                   w_in=(m_w_in, v_w_in), w_out=(m_w_out, v_w_out),
                   w1b=(m_w1_ffn2, v_w1_ffn2), w3b=(m_w3_ffn2, v_w3_ffn2), w2b=(m_w2_ffn2, v_w2_ffn2))
    upd = {}
    for k in big:
        rows = big[k].shape[0]
        tr = 256 if rows % 256 == 0 else rows // 4
        upd[k] = [o[None] for o in _adam_reduce(parts[k], big[k], moments[k][0][0], moments[k][1][0], "adam_" + k, tr,
                                                transposed=k in col_sharded)]

    dmod = jnp.concatenate([sums1[1:2], sums1[0:1], sums1[3:4], sums2[1:2], sums2[0:1], sums2[3:4],
                            sums3[1:2], sums3[0:1], sums3[3:4]], axis=1)
    late = [("b_ada", dmod, b_ada, m_b_ada, v_b_ada), ("g_ffn1", sums1[2:3], g_ffn1, m_g_ffn1, v_g_ffn1)]
    late_all = _run_comm(_Direct([pack(late, 1)], True), "gather_small")[0]
    for items, g_all, name in ((early, early_all, "adam_small_early"), (late, late_all, "adam_small_late")):
        small_out = _adam_reduce(g_all, pack(items, 2), pack(items, 3), pack(items, 4), name, g_all.shape[1])
        off = 0
        for key, _, w_, _, _ in items:
            n_rows = _pad_rows(w_).shape[0]
            upd[key] = [o[off:off + n_rows].reshape(-1)[:w_.size].reshape(w_.shape) for o in small_out]
            off += n_rows

    dmod_rows = late_all[:, :N_MOD * d // BLOCK, :].reshape(N_DEV, N_MOD * d)
    dmod_cols = lax.dynamic_slice(dmod_rows, (0, me * MOD_COLS), (N_DEV, MOD_COLS))
    upd["w_ada"] = [o[None] for o in _adam_outer(act_all.T, dmod_cols, w_ada[0], m_w_ada[0], v_w_ada[0], "adam_w_ada")]

    order = [("w_ada", "w_ada"), ("b_ada", "b_ada"), ("g_ffn1", "g_ffn1"), ("w1_ffn1", "w1a"), ("w3_ffn1", "w3a"),
             ("w2_ffn1", "w2a"), ("g_mix", "g_mix"), ("w_in", "w_in"), ("spatial_w", "spatial_w"),
             ("spatial_b", "spatial_b"), ("g_v", "g_v"), ("g_q", "g_q"), ("g_k", "g_k"), ("sinks", "sinks"),
             ("rel_bias", "rel_bias"), ("w_out", "w_out"), ("g_ffn2", "g_ffn2"), ("w1_ffn2", "w1b"),
             ("w3_ffn2", "w3b"), ("w2_ffn2", "w2b")]
    outs = [loss, dx0[None]]
    for i in range(4):
        outs += [upd[key][i] for _, key in order]
    return tuple(outs)
```

```python
import functools
import math

import numpy as np
import jax
import jax.numpy as jnp
from jax import lax
from jax.experimental import pallas as pl
from jax.experimental.pallas import tpu as pltpu

F32 = jnp.float32
BF16 = jnp.bfloat16

D_MODEL = 2048
D_FF = 5632
BLOCK = 128
A_HEADS = 8
A_WIDTH = 1024
B_HEADS = 16
B_KV_HEADS = 2
GROUP = B_HEADS // B_KV_HEADS
HEAD_DIM = 64
B_WIDTH = 1024
KV_WIDTH = 128
IN_COLS = 3328
Q_OFF = 2 * A_WIDTH
K_OFF = Q_OFF + B_WIDTH
V_OFF = K_OFF + KV_WIDTH
N_BUCKETS = 32
N_MOD = 9
EPS = 1e-6
N_DEV = 8
MOD_COLS = N_MOD * D_MODEL // N_DEV

ADAM_LR = 0.001
ADAM_B1 = 0.9
ADAM_B2 = 0.999
ADAM_EPS = 1e-08
ADAM_WD = 0.01
ADAM_STEP = 10

FFN_ROWS = 1024
DW_TOKENS = 4096
MASK_VALUE = -1e30
VMEM_LIMIT = 56 * 1024 * 1024
MESH_ID = pl.DeviceIdType.MESH
ANY = pl.BlockSpec(memory_space=pl.ANY)

_SQRT_HALF = 0.7071067811865476
_INV_SQRT_2PI = 0.3989422804014327


def _pc(body, **kw):
    return pl.pallas_call(body, **kw)


def _params(sem=None):
    return pltpu.CompilerParams(dimension_semantics=sem, vmem_limit_bytes=VMEM_LIMIT)


def _dot(a, b):
    return lax.dot_general(a, b, (((1,), (0,)), ((), ())), preferred_element_type=F32)


def _dot_nt(a, b):
    return lax.dot_general(a, b, (((1,), (1,)), ((), ())), preferred_element_type=F32)


def _dot_tn(a, b):
    return lax.dot_general(a, b, (((0,), (0,)), ((), ())), preferred_element_type=F32)


def _gelu(x):
    return 0.5 * x * (1.0 + lax.erf(x * _SQRT_HALF))


def _gelu_and_grad(x):
    cdf = 0.5 * (1.0 + lax.erf(x * _SQRT_HALF))
    return x * cdf, cdf + x * jnp.exp(-0.5 * x * x) * _INV_SQRT_2PI


def _rms(x):
    r = lax.rsqrt(jnp.mean(x * x, axis=-1, keepdims=True) + EPS)
    return x * r, r


def _rms_bwd(dy, y, r):
    return r * (dy - y * jnp.mean(dy * y, axis=-1, keepdims=True))


def _row_spec(tm, cols):
    return pl.BlockSpec((tm, cols), lambda i: (i, 0))


def _vec_spec(cols):
    return pl.BlockSpec((1, cols), lambda i: (0, 0))


def _position():
    x, y, c = lax.axis_index("x"), lax.axis_index("y"), lax.axis_index("c")
    return x, y, c


def _linear(p):
    return 4 * p[0] + 2 * p[1] + p[2]


class _Comm:
    PEER_COPIES = N_DEV - 1

    def __init__(self, arrs):
        self.arrs = list(arrs)
        n = len(self.arrs)
        self.scratch_shapes = [pltpu.SemaphoreType.DMA((self.PEER_COPIES * n,)),
                               pltpu.SemaphoreType.DMA((self.PEER_COPIES * n,)),
                               pltpu.SemaphoreType.DMA((n,))]

    def bind(self, srcs, dsts, sems):
        self.srcs, self.dsts = srcs, dsts
        self.send_sems, self.recv_sems, self.local_sems = sems
        x, y, c = _position()
        self.me, self.sibling, self.core = (x, y, c), (x, y, 1 - c), c
        self.chips = [(1 - x, y), (x, 1 - y), (1 - x, 1 - y)]
        self.peers = [(1 - x if k & 4 else x, 1 - y if k & 2 else y, 1 - c if k & 1 else c)
                      for k in range(1, N_DEV)]

    def relay(self):
        pass


class _GatherTwoLevel(_Comm):
    def __init__(self, arrs):
        super().__init__(arrs)
        self.out_shape = [jax.ShapeDtypeStruct((N_DEV,) + a.shape, a.dtype) for a in self.arrs]

    def _copy(self, a, k, block, to, from_input=False):
        rows = self.dsts[a].at[_linear(block)]
        return pltpu.make_async_remote_copy(
            src_ref=self.srcs[a] if from_input else rows, dst_ref=rows,
            send_sem=self.send_sems.at[self.PEER_COPIES * a + k], recv_sem=self.recv_sems.at[self.PEER_COPIES * a + k],
            device_id=to, device_id_type=MESH_ID)

    def _local(self, a):
        return pltpu.make_async_copy(self.srcs[a], self.dsts[a].at[_linear(self.me)], self.local_sems.at[a])

    def _first(self, a):
        return [self._copy(a, 0, self.me, self.sibling, True)] + [
            self._copy(a, 1 + j, self.me, (*chip, self.core), True) for j, chip in enumerate(self.chips)]

    def _passed(self, a):
        return [self._copy(a, 4 + j, (*chip, self.core), self.sibling) for j, chip in enumerate(self.chips)]

    def start(self):
        for a in range(len(self.arrs)):
            self._local(a).start()
            for cp in self._first(a):
                cp.start()

    def relay(self):
        for a in range(len(self.arrs)):
            for j, chip in enumerate(self.chips):
                self._copy(a, 1 + j, (*chip, self.core), self.me).wait_recv()
                self._passed(a)[j].start()

    def finish(self):
        for a in range(len(self.arrs)):
            self._copy(a, 0, self.sibling, self.me).wait_recv()
            for j, chip in enumerate(self.chips):
                self._copy(a, 4 + j, (*chip, 1 - self.core), self.me).wait_recv()
        for a in range(len(self.arrs)):
            for cp in self._first(a) + self._passed(a):
                cp.wait_send()
            self._local(a).wait()


class _Direct(_Comm):
    def __init__(self, arrs, broadcast):
        super().__init__(arrs)
        self.broadcast = broadcast
        self.out_shape = [jax.ShapeDtypeStruct(((N_DEV,) + a.shape) if broadcast else a.shape, a.dtype)
                          for a in self.arrs]

    def _outgoing(self, a, to):
        return self.srcs[a] if self.broadcast else self.srcs[a].at[_linear(to)]

    def _copy(self, a, k, sender, to):
        return pltpu.make_async_remote_copy(
            src_ref=self._outgoing(a, to), dst_ref=self.dsts[a].at[_linear(sender)],
            send_sem=self.send_sems.at[self.PEER_COPIES * a + k], recv_sem=self.recv_sems.at[self.PEER_COPIES * a + k],
            device_id=to, device_id_type=MESH_ID)

    def _local(self, a):
        return pltpu.make_async_copy(self._outgoing(a, self.me), self.dsts[a].at[_linear(self.me)],
                                     self.local_sems.at[a])

    def start(self):
        for a in range(len(self.arrs)):
            self._local(a).start()
            for k, peer in enumerate(self.peers):
                self._copy(a, k, self.me, peer).start()

    def finish(self):
        for a in range(len(self.arrs)):
            for k, peer in enumerate(self.peers):
                self._copy(a, k, peer, self.me).wait_recv()
        for a in range(len(self.arrs)):
            for k, peer in enumerate(self.peers):
                self._copy(a, k, self.me, peer).wait_send()
            self._local(a).wait()


def _run_comm(comm, name):
    n = len(comm.arrs)

    def body(*refs):
        comm.bind(refs[:n], refs[n:2 * n], refs[2 * n:])
        comm.start()
        comm.relay()
        comm.finish()

    return _pc(body, name=name, out_shape=comm.out_shape, in_specs=[ANY] * n, out_specs=[ANY] * n,
               scratch_shapes=comm.scratch_shapes)(*comm.arrs)


def _call(body, inputs, *, name, grid, in_specs, out_specs, out_shape, scratch_shapes=(), semantics,
          comm=None, relay_at=0.5):
    if comm is None:
        out = _pc(body, name=name, grid=grid, in_specs=in_specs, out_specs=out_specs, out_shape=out_shape,
                  scratch_shapes=list(scratch_shapes), compiler_params=_params(semantics))(*inputs)
        return list(out), []
    n_in, n_out, n_sc, k = len(in_specs), len(out_specs), len(scratch_shapes), len(comm.arrs)
    steps = math.prod(grid)
    relay_step = min(steps - 1, int(steps * relay_at))

    def carrier(*refs):
        ins, rest = refs[:n_in], refs[n_in:]
        csrc, rest = rest[:k], rest[k:]
        outs, rest = rest[:n_out], rest[n_out:]
        cdst, rest = rest[:k], rest[k:]
        scratch, csems = rest[:n_sc], rest[n_sc:]
        step = 0
        for axis, size in enumerate(grid):
            step = step * size + pl.program_id(axis)
        comm.bind(csrc, cdst, csems)
        pl.when(step == 0)(comm.start)
        pl.when(step == relay_step)(comm.relay)
        body(*ins, *outs, *scratch)
        pl.when(step == steps - 1)(comm.finish)

    out = _pc(carrier, name=name, grid=grid, in_specs=list(in_specs) + [ANY] * k,
              out_specs=list(out_specs) + [ANY] * k, out_shape=list(out_shape) + comm.out_shape,
              scratch_shapes=list(scratch_shapes) + comm.scratch_shapes,
              compiler_params=_params(("arbitrary",) * len(grid)))(*inputs, *comm.arrs)
    return list(out[:n_out]), list(out[n_out:])


def _matmul(a, b, mode, tm, tn, tk, out_dtype, name, comm=None, relay_at=0.5):
    if mode == "nn":
        (m, kk), nn = a.shape, b.shape[1]
        a_spec = pl.BlockSpec((tm, tk), lambda i, j, k: (i, k))
        b_spec = pl.BlockSpec((tk, tn), lambda i, j, k: (k, j))
        dot = _dot
    elif mode == "nt":
        (m, kk), nn = a.shape, b.shape[0]
        a_spec = pl.BlockSpec((tm, tk), lambda i, j, k: (i, k))
        b_spec = pl.BlockSpec((tn, tk), lambda i, j, k: (j, k))
        dot = _dot_nt
    else:
        (kk, m), nn = a.shape, b.shape[1]
        a_spec = pl.BlockSpec((tk, tm), lambda i, j, k: (k, i))
        b_spec = pl.BlockSpec((tk, tn), lambda i, j, k: (k, j))
        dot = _dot_tn
    assert m % tm == 0 and nn % tn == 0 and kk % tk == 0, (a.shape, b.shape, tm, tn, tk)
    nk = kk // tk
    narrow = nk > 1 and out_dtype != F32

    def body(a_ref, b_ref, o_ref, *acc):
        p = dot(a_ref[...], b_ref[...])
        if nk == 1:
            o_ref[...] = p.astype(o_ref.dtype)
            return
        k = pl.program_id(2)
        acc_ref = acc[0] if narrow else o_ref

        @pl.when(k == 0)
        def _():
            acc_ref[...] = p

        @pl.when((k > 0) & (k < nk - 1) if narrow else k > 0)
        def _():
            acc_ref[...] += p

        if narrow:
            @pl.when(k == nk - 1)
            def _():
                o_ref[...] = (acc_ref[...] + p).astype(o_ref.dtype)

    out, moved = _call(
        body, (a, b), name=name, grid=(m // tm, nn // tn, nk),
        in_specs=[a_spec, b_spec], out_specs=[pl.BlockSpec((tm, tn), lambda i, j, k: (i, j))],
        out_shape=[jax.ShapeDtypeStruct((m, nn), out_dtype)],
        scratch_shapes=[pltpu.VMEM((tm, tn), F32)] if narrow else [],
        semantics=("parallel", "parallel", "arbitrary"), comm=comm, relay_at=relay_at)
    return out[0] if comm is None else (out[0], moved)


def _norm_fwd(x_prev, f_prev, gate_prev, coef, g, sc, sh, name, tm=512):
    t, d = x_prev.shape
    residual = f_prev is not None

    def body(*refs):
        if residual:
            xp_ref, fp_ref, gp_ref, g_ref, sc_ref, sh_ref, x_ref, h_ref = refs
            x = xp_ref[...] + coef * gp_ref[...] * fp_ref[...]
            x_ref[...] = x
        else:
            xp_ref, g_ref, sc_ref, sh_ref, h_ref = refs
            x = xp_ref[...]
        y, _ = _rms(x)
        h_ref[...] = ((y * g_ref[...]) * (1.0 + sc_ref[...]) + sh_ref[...]).astype(BF16)

    row, vec = _row_spec(tm, d), _vec_spec(d)
    if residual:
        ins, in_specs = (x_prev, f_prev, gate_prev, g, sc, sh), [row, row, vec, vec, vec, vec]
        out_shape = [jax.ShapeDtypeStruct((t, d), F32), jax.ShapeDtypeStruct((t, d), BF16)]
        out_specs = [row, row]
    else:
        ins, in_specs = (x_prev, g, sc, sh), [row, vec, vec, vec]
        out_shape = [jax.ShapeDtypeStruct((t, d), BF16)]
        out_specs = [row]
    out = _pc(body, name=name, grid=(t // tm,), in_specs=in_specs, out_specs=out_specs, out_shape=out_shape,
              compiler_params=_params(("parallel",)))(*ins)
    return out if residual else out[0]


def _norm_bwd(dh, x, dxo, fo, coef_o, g, sc, gate_prev, coef_prev, name, tm=256, comm=None):
    t, d = x.shape
    with_prev = gate_prev is not None

    def body(*refs):
        if with_prev:
            dh_ref, x_ref, dxo_ref, fo_ref, g_ref, sc_ref, gp_ref, dx_ref, dfp_ref, sums_ref = refs
        else:
            dh_ref, x_ref, dxo_ref, fo_ref, g_ref, sc_ref, dx_ref, sums_ref = refs
        dh_v, dxo_v = dh_ref[...], dxo_ref[...]
        y, r = _rms(x_ref[...])
        n = y * g_ref[...]
        dn = dh_v * (1.0 + sc_ref[...])
        dx = dxo_v + _rms_bwd(dn * g_ref[...], y, r)
        dx_ref[...] = dx
        if with_prev:
            dfp_ref[...] = (coef_prev * gp_ref[...] * dx).astype(BF16)
        @pl.when(pl.program_id(0) == 0)
        def _():
            sums_ref[...] = jnp.zeros_like(sums_ref)

        sums_ref[0:1, :] += jnp.sum(dh_v * n, axis=0, keepdims=True)
        sums_ref[1:2, :] += jnp.sum(dh_v, axis=0, keepdims=True)
        sums_ref[2:3, :] += jnp.sum(dn * y, axis=0, keepdims=True)
        sums_ref[3:4, :] += jnp.sum(coef_o * dxo_v * fo_ref[...], axis=0, keepdims=True)

    row, vec = _row_spec(tm, d), _vec_spec(d)
    sums_spec = pl.BlockSpec((8, d), lambda i: (0, 0))
    ins, in_specs = [dh, x, dxo, fo, g, sc], [row, row, row, row, vec, vec]
    out_shape, out_specs = [jax.ShapeDtypeStruct((t, d), F32)], [row]
    if with_prev:
        ins.append(gate_prev)
        in_specs.append(vec)
        out_shape.append(jax.ShapeDtypeStruct((t, d), BF16))
        out_specs.append(row)
    out_shape.append(jax.ShapeDtypeStruct((8, d), F32))
    out_specs.append(sums_spec)
    out, moved = _call(body, ins, name=name, grid=(t // tm,), in_specs=in_specs, out_specs=out_specs,
                       out_shape=out_shape, semantics=("arbitrary",), comm=comm)
    return out if comm is None else (out, moved)


def _ffn_up(h, w1t, w3t, name, tf=512, comm=None, relay_at=0.5):
    t, d = h.shape
    ff = w1t.shape[0]
    tm = min(FFN_ROWS, t)

    def body(h_ref, w1_ref, w3_ref, ga_ref, gb_ref, s_ref):
        hv = h_ref[...]
        a = _dot_nt(hv, w1_ref[...])
        b = _dot_nt(hv, w3_ref[...])
        sg = jax.nn.sigmoid(a)
        sil = a * sg
        ga_ref[...] = (b * (sg * (1.0 + a * (1.0 - sg)))).astype(BF16)
        gb_ref[...] = sil.astype(BF16)
        s_ref[...] = (sil * b).astype(BF16)

    tile = pl.BlockSpec((tm, tf), lambda i, j: (i, j))
    cols = pl.BlockSpec((tf, d), lambda i, j: (j, 0))
    out, moved = _call(
        body, (h, w1t, w3t), name=name, grid=(t // tm, ff // tf),
        in_specs=[pl.BlockSpec((tm, d), lambda i, j: (i, 0)), cols, cols], out_specs=[tile, tile, tile],
        out_shape=[jax.ShapeDtypeStruct((t, ff), BF16)] * 3,
        semantics=("parallel", "parallel"), comm=comm, relay_at=relay_at)
    return out if comm is None else (out, moved)


def _ffn_dact(df, ga, gb, w2, name, tf=512, comm=None):
    t, d = df.shape
    ff = ga.shape[1]
    tm = min(2 * FFN_ROWS, t)

    def body(df_ref, ga_ref, gb_ref, w2_ref, da_ref, db_ref):
        ds = _dot_nt(df_ref[...], w2_ref[...])
        da_ref[...] = (ds * ga_ref[...]).astype(BF16)
        db_ref[...] = (ds * gb_ref[...]).astype(BF16)

    tile = pl.BlockSpec((tm, tf), lambda i, j: (i, j))
    out, moved = _call(
        body, (df, ga, gb, w2), name=name, grid=(t // tm, ff // tf),
        in_specs=[pl.BlockSpec((tm, d), lambda i, j: (i, 0)), tile, tile, pl.BlockSpec((tf, d), lambda i, j: (j, 0))],
        out_specs=[tile, tile], out_shape=[jax.ShapeDtypeStruct((t, ff), BF16)] * 2,
        semantics=("parallel", "parallel"), comm=comm)
    return out if comm is None else (out, moved)


def _ffn_dh(da, db, w1t, w3t, name, tn=512, comm=None):
    t, ff = da.shape
    d = w1t.shape[1]
    tm, tk = min(FFN_ROWS, t), ff // 2

    def body(da_ref, db_ref, w1_ref, w3_ref, dh_ref):
        p = _dot(da_ref[...], w1_ref[...]) + _dot(db_ref[...], w3_ref[...])
        k = pl.program_id(2)

        @pl.when(k == 0)
        def _():
            dh_ref[...] = p

        @pl.when(k > 0)
        def _():
            dh_ref[...] += p

    act = pl.BlockSpec((tm, tk), lambda i, j, k: (i, k))
    wgt = pl.BlockSpec((tk, tn), lambda i, j, k: (k, j))
    out, moved = _call(
        body, (da, db, w1t, w3t), name=name, grid=(t // tm, d // tn, 2),
        in_specs=[act, act, wgt, wgt], out_specs=[pl.BlockSpec((tm, tn), lambda i, j, k: (i, j))],
        out_shape=[jax.ShapeDtypeStruct((t, d), F32)],
        semantics=("parallel", "parallel", "arbitrary"), comm=comm)
    return out[0] if comm is None else (out[0], moved)


def _loss_head(x_prev, f_prev, gate_prev, coef, target, name, tm=512):
    t, d = x_prev.shape
    steps = t // tm

    def body(xp_ref, fp_ref, gp_ref, tg_ref, dy_ref, df_ref, loss_ref, acc_ref):
        i = pl.program_id(0)
        e = xp_ref[...] + coef * gp_ref[...] * fp_ref[...] - tg_ref[...]
        dy = e * (1.0 / d)
        dy_ref[...] = dy
        df_ref[...] = (coef * gp_ref[...] * dy).astype(BF16)
        part = jnp.sum(e * e, axis=0, keepdims=True)

        @pl.when(i == 0)
        def _():
            acc_ref[...] = part

        @pl.when(i > 0)
        def _():
            acc_ref[...] += part

        @pl.when(i == steps - 1)
        def _():
            loss_ref[...] = jnp.sum(acc_ref[...], axis=1, keepdims=True) * (0.5 / d)

    row, vec = _row_spec(tm, d), _vec_spec(d)
    return _pc(
        body, name=name, grid=(steps,),
        in_specs=[row, row, vec, row], out_specs=[row, row, pl.BlockSpec((1, 1), lambda i: (0, 0))],
        out_shape=[jax.ShapeDtypeStruct((t, d), F32), jax.ShapeDtypeStruct((t, d), BF16),
                   jax.ShapeDtypeStruct((1, 1), F32)],
        scratch_shapes=[pltpu.VMEM((1, d), F32)],
        compiler_params=_params(("arbitrary",)),
    )(x_prev, f_prev, gate_prev, target)


def _split3(x):
    hi = x.astype(BF16)
    r1 = x - hi.astype(F32)
    mid = r1.astype(BF16)
    lo = (r1 - mid.astype(F32)).astype(BF16)
    return hi, mid, lo


def _select_matmul(a, onehot, name):
    m, n = a.shape[0], onehot.shape[1]

    def body(a_ref, oh_ref, o_ref):
        hi, mid, lo = _split3(a_ref[...])
        oh = oh_ref[...]
        o_ref[...] = (_dot(hi, oh) + _dot(mid, oh)) + _dot(lo, oh)

    return _pc(body, name=name, out_shape=jax.ShapeDtypeStruct((m, n), F32), compiler_params=_params())(a, onehot)


def _bucket_onehot():
    qi = np.arange(BLOCK)[:, None]
    kj = np.arange(2 * BLOCK)[None, :]
    dist = np.clip(qi + BLOCK - kj, 0, None)
    nf = np.maximum(dist, 1).astype(np.float32)
    large = 16 + (np.log(nf / np.float32(16)) / np.float32(math.log(128 / 16)) * np.float32(16)).astype(np.int32)
    bucket = np.where(dist < 16, dist, np.minimum(large, N_BUCKETS - 1)).reshape(-1)
    return (bucket[None, :] == np.arange(N_BUCKETS)[:, None]).astype(np.float32)


def _window_mask(n):
    row = lax.broadcasted_iota(jnp.int32, (GROUP * BLOCK, 2 * BLOCK), 0)
    qi = row & (BLOCK - 1)
    kj = lax.broadcasted_iota(jnp.int32, (GROUP * BLOCK, 2 * BLOCK), 1)
    return (kj > qi) & (kj <= qi + BLOCK) & ((kj >= BLOCK) | (n > 0))


def _kv_band(zc_ref, zp_ref, kvh):
    lo, hi = kvh * HEAD_DIM, (kvh + 1) * HEAD_DIM
    k_raw = jnp.concatenate([zp_ref[:, lo:hi], zc_ref[:, K_OFF + lo:K_OFF + hi]], axis=0)
    v_raw = jnp.concatenate([zp_ref[:, KV_WIDTH + lo:KV_WIDTH + hi], zc_ref[:, V_OFF + lo:V_OFF + hi]], axis=0)
    return k_raw, v_raw


def _stack_heads(ref, off):
    return jnp.concatenate([ref[:, off + g * HEAD_DIM:off + (g + 1) * HEAD_DIM] for g in range(GROUP)], axis=0)


def _unstack_heads(ref, off, stacked):
    for g in range(GROUP):
        ref[:, off + g * HEAD_DIM:off + (g + 1) * HEAD_DIM] = stacked[g * BLOCK:(g + 1) * BLOCK]


def _group_softmax(qn, kband, bias_ref, sink_ref, kvh, valid):
    bias = bias_ref[kvh * GROUP:(kvh + 1) * GROUP].reshape(GROUP * BLOCK, 2 * BLOCK)
    s = _dot_nt(qn, kband) * (HEAD_DIM ** -0.5) + bias
    s = jnp.where(valid, s, MASK_VALUE)
    sink = jnp.concatenate([jnp.full((BLOCK, 1), sink_ref[0, kvh * GROUP + g], F32) for g in range(GROUP)], axis=0)
    m = jnp.maximum(jnp.max(s, axis=-1, keepdims=True), sink)
    p = jnp.exp(s - m)
    esink = jnp.exp(sink - m)
    inv = 1.0 / (jnp.sum(p, axis=-1, keepdims=True) + esink)
    return p * inv, esink * inv, inv


def _mix_specs(nb, order):
    cur = pl.BlockSpec((BLOCK, IN_COLS), lambda n: (order(n), 0))
    prev = pl.BlockSpec((BLOCK, 2 * KV_WIDTH), lambda n: (jnp.maximum(order(n) - 1, 0), K_OFF // (2 * KV_WIDTH)))
    full = lambda shape: pl.BlockSpec(shape, lambda n: (0,) * len(shape))
    params = [full((A_HEADS, BLOCK, BLOCK)), full((BLOCK, A_HEADS)), full((A_HEADS, BLOCK)),
              full((1, HEAD_DIM)), full((1, HEAD_DIM)), pl.BlockSpec(memory_space=pltpu.SMEM),
              full((B_HEADS, BLOCK, 2 * BLOCK))]
    return cur, prev, params, full


def _mix_fwd(z, sw, sb_t, gv, gq, gk, sinks, bias, name, comm=None, relay_at=0.5):
    t = z.shape[0]
    nb = t // BLOCK

    def body(zc_ref, zp_ref, sw_ref, sbt_ref, gv_ref, gq_ref, gk_ref, sink_ref, bias_ref, y_ref):
        n = pl.program_id(0)
        ri = lax.broadcasted_iota(jnp.int32, (BLOCK, BLOCK), 0)
        ci = lax.broadcasted_iota(jnp.int32, (BLOCK, BLOCK), 1)
        tril = ri >= ci
        for h in range(A_HEADS):
            u = _gelu(zc_ref[:, h * BLOCK:(h + 1) * BLOCK])
            vv = _gelu(zc_ref[:, A_WIDTH + h * BLOCK:A_WIDTH + (h + 1) * BLOCK])
            vhat, _ = _rms(vv)
            vn = (vhat * gv_ref[h:h + 1, :]).astype(BF16)
            w = jnp.where(tril, sw_ref[h], 0.0).astype(BF16)
            mixed = _dot(w, vn) + sbt_ref[:, h:h + 1]
            y_ref[:, h * BLOCK:(h + 1) * BLOCK] = (u * mixed).astype(BF16)
        valid = _window_mask(n)
        for kvh in range(B_KV_HEADS):
            k_raw, v_raw = _kv_band(zc_ref, zp_ref, kvh)
            khat, _ = _rms(k_raw)
            kband = (khat * gk_ref[...]).astype(BF16)
            vband = v_raw.astype(BF16)
            qhat, _ = _rms(_stack_heads(zc_ref, Q_OFF + kvh * GROUP * HEAD_DIM))
            qn = (qhat * gq_ref[...]).astype(BF16)
            w, _, _ = _group_softmax(qn, kband, bias_ref, sink_ref, kvh, valid)
            o = _dot(w.astype(BF16), vband)
            _unstack_heads(y_ref, A_WIDTH + kvh * GROUP * HEAD_DIM, o.astype(BF16))

    cur, prev, params, _ = _mix_specs(nb, lambda n: n)
    out, moved = _call(
        body, (z, z, sw, sb_t, gv, gq, gk, sinks, bias), name=name, grid=(nb,), in_specs=[cur, prev] + params,
        out_specs=[pl.BlockSpec((BLOCK, A_WIDTH + B_WIDTH), lambda n: (n, 0))],
        out_shape=[jax.ShapeDtypeStruct((t, A_WIDTH + B_WIDTH), BF16)],
        semantics=("parallel",), comm=comm, relay_at=relay_at)
    return out[0] if comm is None else (out[0], moved)


def _mix_bwd(z, dy, sw, sb_t, gv, gq, gk, sinks, bias, name, comm=None):
    t = z.shape[0]
    nb = t // BLOCK

    def body(zc_ref, zp_ref, dy_ref, sw_ref, sbt_ref, gv_ref, gq_ref, gk_ref, sink_ref, bias_ref,
             dz_ref, dsw_ref, dsb_ref, dgv_ref, dgq_ref, dgk_ref, dsink_ref, dbias_ref, carry_ref):
        step = pl.program_id(0)
        n = nb - 1 - step

        @pl.when(step == 0)
        def _():
            dsw_ref[...] = jnp.zeros_like(dsw_ref)
            dsb_ref[...] = jnp.zeros_like(dsb_ref)
            dgv_ref[...] = jnp.zeros_like(dgv_ref)
            dgq_ref[...] = jnp.zeros_like(dgq_ref)
            dgk_ref[...] = jnp.zeros_like(dgk_ref)
            dsink_ref[...] = jnp.zeros_like(dsink_ref)
            dbias_ref[...] = jnp.zeros_like(dbias_ref)
            carry_ref[...] = jnp.zeros_like(carry_ref)

        ri = lax.broadcasted_iota(jnp.int32, (BLOCK, BLOCK), 0)
        ci = lax.broadcasted_iota(jnp.int32, (BLOCK, BLOCK), 1)
        tril = ri >= ci
        dsb = jnp.zeros((BLOCK, BLOCK), F32)
        for h in range(A_HEADS):
            uo, vo = h * BLOCK, A_WIDTH + h * BLOCK
            u_raw, v_raw = zc_ref[:, uo:uo + BLOCK], zc_ref[:, vo:vo + BLOCK]
            (u, du_raw), (vv, dv_raw) = _gelu_and_grad(u_raw), _gelu_and_grad(v_raw)
            vhat, r = _rms(vv)
            vn = (vhat * gv_ref[h:h + 1, :]).astype(BF16)
            w = jnp.where(tril, sw_ref[h], 0.0).astype(BF16)
            mixed = _dot(w, vn) + sbt_ref[:, h:h + 1]
            dya = dy_ref[:, uo:uo + BLOCK]
            dmixed = dya * u
            dm16 = dmixed.astype(BF16)
            dsw_ref[h] += jnp.where(tril, _dot_nt(dm16, vn), 0.0)
            dsb = dsb + jnp.where(ci == h, jnp.sum(dmixed, axis=1, keepdims=True), 0.0)
            dvn = _dot_tn(w, dm16)
            dgv_ref[h:h + 1, :] += jnp.sum(dvn * vhat, axis=0, keepdims=True)
            dvv = _rms_bwd(dvn * gv_ref[h:h + 1, :], vhat, r)
            dz_ref[:, uo:uo + BLOCK] = (dya * mixed * du_raw).astype(BF16)
            dz_ref[:, vo:vo + BLOCK] = (dvv * dv_raw).astype(BF16)
        dsb_ref[...] += dsb

        valid = _window_mask(n)
        lane = lax.broadcasted_iota(jnp.int32, (1, BLOCK), 1)
        dsink = jnp.zeros((1, BLOCK), F32)
        dgq = jnp.zeros((1, HEAD_DIM), F32)
        for kvh in range(B_KV_HEADS):
            ko, vo = K_OFF + kvh * HEAD_DIM, V_OFF + kvh * HEAD_DIM
            k_raw, v_raw = _kv_band(zc_ref, zp_ref, kvh)
            khat, kr = _rms(k_raw)
            kband = (khat * gk_ref[...]).astype(BF16)
            vband = v_raw.astype(BF16)
            qo = Q_OFF + kvh * GROUP * HEAD_DIM
            qhat, qr = _rms(_stack_heads(zc_ref, qo))
            qn = (qhat * gq_ref[...]).astype(BF16)
            w, wsink, _ = _group_softmax(qn, kband, bias_ref, sink_ref, kvh, valid)
            do = _stack_heads(dy_ref, A_WIDTH + kvh * GROUP * HEAD_DIM).astype(BF16)
            dvb = _dot_tn(w.astype(BF16), do)
            dw = _dot_nt(do, vband)
            rowdot = jnp.sum(w * dw, axis=-1, keepdims=True)
            ds = w * (dw - rowdot)
            dsink_rows = -wsink * rowdot
            for g in range(GROUP):
                head_sum = jnp.sum(dsink_rows[g * BLOCK:(g + 1) * BLOCK], axis=0, keepdims=True)
                dsink = dsink + jnp.where(lane == kvh * GROUP + g, head_sum, 0.0)
            dbias_ref[kvh * GROUP:(kvh + 1) * GROUP] += ds.reshape(GROUP, BLOCK, 2 * BLOCK)
            ds16 = (ds * (HEAD_DIM ** -0.5)).astype(BF16)
            dqn = _dot(ds16, kband)
            dkn = _dot_tn(ds16, qn)
            dgq = dgq + jnp.sum(dqn * qhat, axis=0, keepdims=True)
            _unstack_heads(dz_ref, qo, _rms_bwd(dqn * gq_ref[...], qhat, qr).astype(BF16))
            dgk_ref[...] += jnp.sum(dkn * khat, axis=0, keepdims=True)
            dk = _rms_bwd(dkn * gk_ref[...], khat, kr)
            co = kvh * HEAD_DIM
            dz_ref[:, ko:ko + HEAD_DIM] = (dk[BLOCK:] + carry_ref[:, co:co + HEAD_DIM]).astype(BF16)
            dz_ref[:, vo:vo + HEAD_DIM] = (dvb[BLOCK:] + carry_ref[:, KV_WIDTH + co:KV_WIDTH + co + HEAD_DIM]).astype(BF16)
            carry_ref[:, co:co + HEAD_DIM] = dk[:BLOCK]
            carry_ref[:, KV_WIDTH + co:KV_WIDTH + co + HEAD_DIM] = dvb[:BLOCK]
        dgq_ref[...] += dgq
        dsink_ref[...] += dsink

    order = lambda n: nb - 1 - n
    cur, prev, params, full = _mix_specs(nb, order)
    dy_spec = pl.BlockSpec((BLOCK, A_WIDTH + B_WIDTH), lambda n: (order(n), 0))
    out, moved = _call(
        body, (z, z, dy, sw, sb_t, gv, gq, gk, sinks, bias), name=name, grid=(nb,),
        in_specs=[cur, prev, dy_spec] + params,
        out_specs=[cur, full((A_HEADS, BLOCK, BLOCK)), full((BLOCK, BLOCK)), full((A_HEADS, BLOCK)),
                   full((1, HEAD_DIM)), full((1, HEAD_DIM)), full((1, BLOCK)), full((B_HEADS, BLOCK, 2 * BLOCK))],
        out_shape=[jax.ShapeDtypeStruct((t, IN_COLS), BF16), jax.ShapeDtypeStruct((A_HEADS, BLOCK, BLOCK), F32),
                   jax.ShapeDtypeStruct((BLOCK, BLOCK), F32), jax.ShapeDtypeStruct((A_HEADS, BLOCK), F32),
                   jax.ShapeDtypeStruct((1, HEAD_DIM), F32), jax.ShapeDtypeStruct((1, HEAD_DIM), F32),
                   jax.ShapeDtypeStruct((1, BLOCK), F32), jax.ShapeDtypeStruct((B_HEADS, BLOCK, 2 * BLOCK), F32)],
        scratch_shapes=[pltpu.VMEM((BLOCK, 2 * KV_WIDTH), F32)],
        semantics=("arbitrary",), comm=comm)
    return out if comm is None else (out, moved)


def _ada_fwd(c_all, w_ada, b_cols, name, tn=768):
    nb, d = c_all.shape
    cols = w_ada.shape[1]

    def body(c_ref, w_ref, b_ref, act_ref, mod_ref):
        cv = c_ref[...]
        act = cv * jax.nn.sigmoid(cv)
        act_ref[...] = act
        mod_ref[...] = _dot(act.astype(BF16), w_ref[...].astype(BF16)) + b_ref[...]

    return _pc(
        body, name=name, grid=(cols // tn,),
        in_specs=[pl.BlockSpec((nb, d), lambda j: (0, 0)), pl.BlockSpec((d, tn), lambda j: (0, j)),
                  pl.BlockSpec((1, tn), lambda j: (0, j))],
        out_specs=[pl.BlockSpec((nb, d), lambda j: (0, 0)), pl.BlockSpec((nb, tn), lambda j: (0, j))],
        out_shape=[jax.ShapeDtypeStruct((nb, d), F32), jax.ShapeDtypeStruct((nb, cols), F32)],
        compiler_params=_params(("arbitrary",)),
    )(c_all, w_ada, b_cols)


def _adamw(w, g, m, v):
    m = ADAM_B1 * m + (1.0 - ADAM_B1) * g
    v = ADAM_B2 * v + (1.0 - ADAM_B2) * (g * g)
    m_hat = m / (1.0 - ADAM_B1 ** ADAM_STEP)
    v_hat = v / (1.0 - ADAM_B2 ** ADAM_STEP)
    delta = -ADAM_LR * (m_hat / (jnp.sqrt(v_hat) + ADAM_EPS) + ADAM_WD * w)
    return delta, m, v


def _adam_outer(act_t, dmod, w, m, v, name, tr=256):
    rows, cols = w.shape
    nb = dmod.shape[0]

    def body(act_ref, dm_ref, w_ref, m_ref, v_ref, g_ref, d_ref, nm_ref, nv_ref):
        act = act_ref[...].astype(BF16).astype(F32)
        dm = dm_ref[...].astype(BF16).astype(F32)
        g = act[:, 0:1] * dm[0:1, :]
        for b in range(1, nb):
            g = g + act[:, b:b + 1] * dm[b:b + 1, :]
        g_ref[...] = g
        d_ref[...], nm_ref[...], nv_ref[...] = _adamw(w_ref[...], g, m_ref[...], v_ref[...])

    tile = pl.BlockSpec((tr, cols), lambda i: (i, 0))
    return _pc(
        body, name=name, grid=(rows // tr,),
        in_specs=[pl.BlockSpec((tr, nb), lambda i: (i, 0)), pl.BlockSpec((nb, cols), lambda i: (0, 0)), tile, tile, tile],
        out_specs=[tile] * 4, out_shape=[jax.ShapeDtypeStruct((rows, cols), F32)] * 4,
        compiler_params=_params(("parallel",)),
    )(act_t, dmod, w, m, v)


def _adam_reduce(parts, w, m, v, name, tr, transposed=False):
    rows, cols = w.shape

    def body(p_ref, w_ref, m_ref, v_ref, g_ref, d_ref, nm_ref, nv_ref):
        g = p_ref[0].astype(F32)
        for i in range(1, N_DEV):
            g = g + p_ref[i].astype(F32)
        if transposed:
            g = g.T
        g_ref[...] = g
        d_ref[...], nm_ref[...], nv_ref[...] = _adamw(w_ref[...], g, m_ref[...], v_ref[...])

    tile = pl.BlockSpec((tr, cols), lambda i: (i, 0))
    parts_spec = (pl.BlockSpec((N_DEV, cols, tr), lambda i: (0, 0, i)) if transposed
                  else pl.BlockSpec((N_DEV, tr, cols), lambda i: (0, i, 0)))
    return _pc(
        body, name=name, grid=(rows // tr,),
        in_specs=[parts_spec, tile, tile, tile],
        out_specs=[tile] * 4, out_shape=[jax.ShapeDtypeStruct((rows, cols), F32)] * 4,
        compiler_params=_params(("parallel",)),
    )(parts, w, m, v)


def _pad_rows(v):
    flat = v.reshape(-1)
    pad = (-flat.shape[0]) % (8 * BLOCK)
    return jnp.pad(flat, (0, pad)).reshape(-1, BLOCK)


def kernel(x, c, w_ada, b_ada, g_ffn1, w1_ffn1, w3_ffn1, w2_ffn1, g_mix, w_in, spatial_w, spatial_b, g_v, g_q, g_k, sinks, rel_bias, w_out, g_ffn2, w1_ffn2, w3_ffn2, w2_ffn2, loss_target, m_w_ada, m_b_ada, m_g_ffn1, m_w1_ffn1, m_w3_ffn1, m_w2_ffn1, m_g_mix, m_w_in, m_spatial_w, m_spatial_b, m_g_v, m_g_q, m_g_k, m_sinks, m_rel_bias, m_w_out, m_g_ffn2, m_w1_ffn2, m_w3_ffn2, m_w2_ffn2, v_w_ada, v_b_ada, v_g_ffn1, v_w1_ffn1, v_w3_ffn1, v_w2_ffn1, v_g_mix, v_w_in, v_spatial_w, v_spatial_b, v_g_v, v_g_q, v_g_k, v_sinks, v_rel_bias, v_w_out, v_g_ffn2, v_w1_ffn2, v_w3_ffn2, v_w2_ffn2):
    me = _linear(_position())
    d = D_MODEL
    x0 = x[0]
    target = loss_target[0]
    t = x0.shape[0]

    big = dict(w1a=w1_ffn1[0], w3a=w3_ffn1[0], w2a=w2_ffn1[0], w_in=w_in[0], w_out=w_out[0],
               w1b=w1_ffn2[0], w3b=w3_ffn2[0], w2b=w2_ffn2[0])
    col_sharded = ("w1a", "w3a", "w_in", "w1b", "w3b")
    full = {}

    def gather(keys):
        return _GatherTwoLevel([(big[k].T if k in col_sharded else big[k]).astype(BF16) for k in keys])

    def unpack(keys, gathered):
        for k, gth in zip(keys, gathered):
            full[k] = gth.reshape(-1, gth.shape[-1])

    def share(k):
        gk_ = grads[k]
        return gk_.reshape(N_DEV, gk_.shape[0] // N_DEV, gk_.shape[1])

    unpack(("w1a", "w3a"), _run_comm(gather(("w1a", "w3a")), "gather_ffn1"))

    c_all = _run_comm(_Direct([c], True), "gather_c")[0].reshape(N_DEV, d)
    b_cols = lax.dynamic_slice(b_ada, (0, me * MOD_COLS), (1, MOD_COLS))
    act_all, mod_cols = _ada_fwd(c_all, w_ada[0], b_cols, "ada_fwd")
    mod = _run_comm(_Direct([mod_cols.reshape(N_DEV, 1, MOD_COLS)], False), "scatter_mod")[0]
    mod = mod.reshape(N_MOD, 1, d)
    sh1, sc1, gt1, sh2, sc2, gt2, sh3, sc3, gt3 = [mod[i] for i in range(N_MOD)]

    tm = min(FFN_ROWS, t)
    h1 = _norm_fwd(x0, None, None, 0.0, g_ffn1, sc1, sh1, "norm1_fwd")
    (ga1, gb1, s1), gathered = _ffn_up(h1, full["w1a"], full["w3a"], "ffn1_up",
                                     comm=gather(("w2a", "w_in", "w_out")), relay_at=0.6)
    unpack(("w2a", "w_in", "w_out"), gathered)
    f1, gathered = _matmul(s1, full["w2a"], "nn", tm, 512, D_FF, F32, "ffn1_down", comm=gather(("w1b",)),
                           relay_at=0.7)
    unpack(("w1b",), gathered)
    x1, h2 = _norm_fwd(x0, f1, gt1, 0.5, g_mix, sc2, sh2, "norm2_fwd")
    z = _matmul(h2, full["w_in"], "nt", tm, IN_COLS // 2, d, F32, "mix_in_fwd")
    onehot = _bucket_onehot()
    bias = _select_matmul(rel_bias.T, jnp.asarray(onehot, BF16), "bias_table").reshape(B_HEADS, BLOCK, 2 * BLOCK)
    sb_t = spatial_b[0].T
    mix_params = (spatial_w[0], sb_t, g_v[0], g_q, g_k, sinks, bias)
    ycat, gathered = _mix_fwd(z, *mix_params, "mix_core_fwd", comm=gather(("w3b",)), relay_at=0.6)
    unpack(("w3b",), gathered)
    y = _matmul(ycat, full["w_out"], "nn", 512, d, d, F32, "mix_out_fwd")
    x2, h3 = _norm_fwd(x1, y, gt2, 1.0, g_ffn2, sc3, sh3, "norm3_fwd")
    (ga2, gb2, s2), gathered = _ffn_up(h3, full["w1b"], full["w3b"], "ffn2_up", comm=gather(("w2b",)), relay_at=0.6)
    unpack(("w2b",), gathered)
    f2 = _matmul(s2, full["w2b"], "nn", tm, 512, D_FF, F32, "ffn2_down")
    dx3, df2, loss_part = _loss_head(x2, f2, gt3, 0.5, target, "loss_head")
    loss = lax.psum(loss_part[0, 0], ("x", "y", "c"))

    grads, parts = {}, {}
    tk = min(DW_TOKENS, t)

    def exchange(k):
        return _Direct([share(k)], False)

    grads["w2b"] = _matmul(s2, df2, "tn", 512, 1024, tk, BF16, "ffn2_dw2")
    (da, db), moved = _ffn_dact(df2, ga2, gb2, full["w2b"], "ffn2_dact", comm=exchange("w2b"))
    parts["w2b"] = moved[0]
    grads["w1b"] = _matmul(da, h3, "tn", 512, 1024, tk, BF16, "ffn2_dw1")
    dh3, moved = _ffn_dh(da, db, full["w1b"], full["w3b"], "ffn2_dh", comm=exchange("w1b"))
    parts["w1b"] = moved[0]
    grads["w3b"] = _matmul(db, h3, "tn", 512, 1024, tk, BF16, "ffn2_dw3")
    dx2, dyg, sums3 = _norm_bwd(dh3, x2, dx3, f2, 0.5, g_ffn2, sc3, gt2, 1.0, "norm3_bwd")
    dycat = _matmul(dyg, full["w_out"], "nt", 512, d, d, F32, "mix_out_bwd")
    grads["w_out"] = _matmul(ycat, dyg, "tn", 1024, 512, tk, BF16, "mix_out_dw")
    mix_grads, moved = _mix_bwd(z, dycat, *mix_params, "mix_core_bwd",
                                comm=_Direct([share("w3b"), share("w_out")], False))
    parts["w3b"], parts["w_out"] = moved
    dz, d_sw, d_sb, d_gv, d_gq, d_gk, d_sink, d_bias = mix_grads
    d_rel = _select_matmul(d_bias.reshape(B_HEADS, BLOCK * 2 * BLOCK), jnp.asarray(onehot.T, BF16), "bias_table_bwd")
    dh2 = _matmul(dz, full["w_in"], "nn", tm, d, IN_COLS // 2, F32, "mix_in_bwd")
    grads["w_in"] = _matmul(dz, h2, "tn", IN_COLS // 2, 512, tk, BF16, "mix_in_dw")
    dx1, df1, sums2 = _norm_bwd(dh2, x1, dx2, y, 1.0, g_mix, sc2, gt1, 0.5, "norm2_bwd")
    grads["w2a"], moved = _matmul(s1, df1, "tn", 512, 1024, tk, BF16, "ffn1_dw2", comm=exchange("w_in"))
    parts["w_in"] = moved[0]
    (da, db), moved = _ffn_dact(df1, ga1, gb1, full["w2a"], "ffn1_dact", comm=exchange("w2a"))
    parts["w2a"] = moved[0]
    early = [("g_mix", sums2[2:3], g_mix, m_g_mix, v_g_mix),
             ("spatial_w", d_sw[None], spatial_w, m_spatial_w, v_spatial_w),
             ("spatial_b", d_sb[:, :A_HEADS].T[None], spatial_b, m_spatial_b, v_spatial_b),
             ("g_v", d_gv[None], g_v, m_g_v, v_g_v),
             ("g_q", d_gq, g_q, m_g_q, v_g_q),
             ("g_k", d_gk, g_k, m_g_k, v_g_k),
             ("sinks", d_sink[:, :B_HEADS], sinks, m_sinks, v_sinks),
             ("rel_bias", d_rel.T, rel_bias, m_rel_bias, v_rel_bias),
             ("g_ffn2", sums3[2:3], g_ffn2, m_g_ffn2, v_g_ffn2)]

    def pack(items, i):
        return jnp.concatenate([_pad_rows(item[i]) for item in items], axis=0)

    grads["w1a"], moved = _matmul(da, h1, "tn", 512, 1024, tk, BF16, "ffn1_dw1", comm=_Direct([pack(early, 1)], True))
    early_all = moved[0]
    grads["w3a"], moved = _matmul(db, h1, "tn", 512, 1024, tk, BF16, "ffn1_dw3", comm=exchange("w1a"))
    parts["w1a"] = moved[0]
    dh1, moved = _ffn_dh(da, db, full["w1a"], full["w3a"], "ffn1_dh", comm=exchange("w3a"))
    parts["w3a"] = moved[0]
    dx0, sums1 = _norm_bwd(dh1, x0, dx1, f1, 0.5, g_ffn1, sc1, None, 0.0, "norm1_bwd")

    moments = dict(w1a=(m_w1_ffn1, v_w1_ffn1), w3a=(m_w3_ffn1, v_w3_ffn1), w2a=(m_w2_ffn1, v_w2_ffn1),
                   w_in=(m_w_in, v_w_in), w_out=(m_w_out, v_w_out),
                   w1b=(m_w1_ffn2, v_w1_ffn2), w3b=(m_w3_ffn2, v_w3_ffn2), w2b=(m_w2_ffn2, v_w2_ffn2))
    upd = {}
    for k in big:
        rows = big[k].shape[0]
        tr = 256 if rows % 256 == 0 else rows // 4
        upd[k] = [o[None] for o in _adam_reduce(parts[k], big[k], moments[k][0][0], moments[k][1][0], "adam_" + k, tr,
                                                transposed=k in col_sharded)]

    dmod = jnp.concatenate([sums1[1:2], sums1[0:1], sums1[3:4], sums2[1:2], sums2[0:1], sums2[3:4],
                            sums3[1:2], sums3[0:1], sums3[3:4]], axis=1)
    late = [("b_ada", dmod, b_ada, m_b_ada, v_b_ada), ("g_ffn1", sums1[2:3], g_ffn1, m_g_ffn1, v_g_ffn1)]
    late_all = _run_comm(_Direct([pack(late, 1)], True), "gather_small")[0]
    for items, g_all, name in ((early, early_all, "adam_small_early"), (late, late_all, "adam_small_late")):
        small_out = _adam_reduce(g_all, pack(items, 2), pack(items, 3), pack(items, 4), name, g_all.shape[1])
        off = 0
        for key, _, w_, _, _ in items:
            n_rows = _pad_rows(w_).shape[0]
            upd[key] = [o[off:off + n_rows].reshape(-1)[:w_.size].reshape(w_.shape) for o in small_out]
            off += n_rows

    dmod_rows = late_all[:, :N_MOD * d // BLOCK, :].reshape(N_DEV, N_MOD * d)
    dmod_cols = lax.dynamic_slice(dmod_rows, (0, me * MOD_COLS), (N_DEV, MOD_COLS))
    upd["w_ada"] = [o[None] for o in _adam_outer(act_all.T, dmod_cols, w_ada[0], m_w_ada[0], v_w_ada[0], "adam_w_ada")]

    order = [("w_ada", "w_ada"), ("b_ada", "b_ada"), ("g_ffn1", "g_ffn1"), ("w1_ffn1", "w1a"), ("w3_ffn1", "w3a"),
             ("w2_ffn1", "w2a"), ("g_mix", "g_mix"), ("w_in", "w_in"), ("spatial_w", "spatial_w"),
             ("spatial_b", "spatial_b"), ("g_v", "g_v"), ("g_q", "g_q"), ("g_k", "g_k"), ("sinks", "sinks"),
             ("rel_bias", "rel_bias"), ("w_out", "w_out"), ("g_ffn2", "g_ffn2"), ("w1_ffn2", "w1b"),
             ("w3_ffn2", "w3b"), ("w2_ffn2", "w2b")]
    outs = [loss, dx0[None]]
    for i in range(4):
        outs += [upd[key][i] for _, key in order]
    return tuple(outs)
```

```python
import math

import numpy as np
import jax
import jax.numpy as jnp
from jax import lax
from jax.experimental import pallas as pl
from jax.experimental.pallas import tpu as pltpu

F32 = jnp.float32
BF16 = jnp.bfloat16

D_MODEL = 2048
D_FF = 5632
BLOCK = 128
A_HEADS = 8
A_WIDTH = 1024
B_HEADS = 16
B_KV_HEADS = 2
GROUP = B_HEADS // B_KV_HEADS
HEAD_DIM = 64
B_WIDTH = 1024
KV_WIDTH = 128
IN_COLS = 3328
Q_OFF = 2 * A_WIDTH
K_OFF = Q_OFF + B_WIDTH
V_OFF = K_OFF + KV_WIDTH
N_BUCKETS = 32
N_MOD = 9
EPS = 1e-6
N_DEV = 8
MOD_COLS = N_MOD * D_MODEL // N_DEV

ADAM_LR = 0.001
ADAM_B1 = 0.9
ADAM_B2 = 0.999
ADAM_EPS = 1e-08
ADAM_WD = 0.01
ADAM_STEP = 10

FFN_ROWS = 1024
DW_TOKENS = 4096
MASK_VALUE = -1e30
VMEM_LIMIT = 56 * 1024 * 1024
MESH_ID = pl.DeviceIdType.MESH
ANY = pl.BlockSpec(memory_space=pl.ANY)

_SQRT_HALF = 0.7071067811865476
_INV_SQRT_2PI = 0.3989422804014327


def _pc(body, **kw):
    return pl.pallas_call(body, **kw)


def _params(sem=None):
    return pltpu.CompilerParams(dimension_semantics=sem, vmem_limit_bytes=VMEM_LIMIT)


def _dot(a, b):
    return lax.dot_general(a, b, (((1,), (0,)), ((), ())), preferred_element_type=F32)


def _dot_nt(a, b):
    return lax.dot_general(a, b, (((1,), (1,)), ((), ())), preferred_element_type=F32)


def _dot_tn(a, b):
    return lax.dot_general(a, b, (((0,), (0,)), ((), ())), preferred_element_type=F32)


def _gelu(x):
    return 0.5 * x * (1.0 + lax.erf(x * _SQRT_HALF))


def _gelu_and_grad(x):
    cdf = 0.5 * (1.0 + lax.erf(x * _SQRT_HALF))
    return x * cdf, cdf + x * jnp.exp(-0.5 * x * x) * _INV_SQRT_2PI


def _rms(x):
    r = lax.rsqrt(jnp.mean(x * x, axis=-1, keepdims=True) + EPS)
    return x * r, r


def _rms_bwd(dy, y, r):
    return r * (dy - y * jnp.mean(dy * y, axis=-1, keepdims=True))


def _row_spec(tm, cols):
    return pl.BlockSpec((tm, cols), lambda i: (i, 0))


def _vec_spec(cols):
    return pl.BlockSpec((1, cols), lambda i: (0, 0))


def _position():
    x, y, c = lax.axis_index("x"), lax.axis_index("y"), lax.axis_index("c")
    return x, y, c


def _linear(p):
    return 4 * p[0] + 2 * p[1] + p[2]


class _Comm:
    PEER_COPIES = N_DEV - 1

    def __init__(self, arrs):
        self.arrs = list(arrs)
        n = len(self.arrs)
        self.scratch_shapes = [pltpu.SemaphoreType.DMA((self.PEER_COPIES * n,)),
                               pltpu.SemaphoreType.DMA((self.PEER_COPIES * n,)),
                               pltpu.SemaphoreType.DMA((n,))]

    def bind(self, srcs, dsts, sems):
        self.srcs, self.dsts = srcs, dsts
        self.send_sems, self.recv_sems, self.local_sems = sems
        x, y, c = _position()
        self.me, self.sibling, self.core = (x, y, c), (x, y, 1 - c), c
        self.chips = [(1 - x, y), (x, 1 - y), (1 - x, 1 - y)]
        self.peers = [(1 - x if k & 4 else x, 1 - y if k & 2 else y, 1 - c if k & 1 else c)
                      for k in range(1, N_DEV)]

    def relay(self):
        pass


class _GatherTwoLevel(_Comm):
    def __init__(self, arrs):
        super().__init__(arrs)
        self.out_shape = [jax.ShapeDtypeStruct((N_DEV,) + a.shape, a.dtype) for a in self.arrs]

    def _copy(self, a, k, block, to, from_input=False):
        rows = self.dsts[a].at[_linear(block)]
        return pltpu.make_async_remote_copy(
            src_ref=self.srcs[a] if from_input else rows, dst_ref=rows,
            send_sem=self.send_sems.at[self.PEER_COPIES * a + k], recv_sem=self.recv_sems.at[self.PEER_COPIES * a + k],
            device_id=to, device_id_type=MESH_ID)

    def _local(self, a):
        return pltpu.make_async_copy(self.srcs[a], self.dsts[a].at[_linear(self.me)], self.local_sems.at[a])

    def _first(self, a):
        return [self._copy(a, 0, self.me, self.sibling, True)] + [
            self._copy(a, 1 + j, self.me, (*chip, self.core), True) for j, chip in enumerate(self.chips)]

    def _passed(self, a):
        return [self._copy(a, 4 + j, (*chip, self.core), self.sibling) for j, chip in enumerate(self.chips)]

    def start(self):
        for a in range(len(self.arrs)):
            self._local(a).start()
            for cp in self._first(a):
                cp.start()

    def relay(self):
        for a in range(len(self.arrs)):
            for j, chip in enumerate(self.chips):
                self._copy(a, 1 + j, (*chip, self.core), self.me).wait_recv()
                self._passed(a)[j].start()

    def finish(self):
        for a in range(len(self.arrs)):
            self._copy(a, 0, self.sibling, self.me).wait_recv()
            for j, chip in enumerate(self.chips):
                self._copy(a, 4 + j, (*chip, 1 - self.core), self.me).wait_recv()
        for a in range(len(self.arrs)):
            for cp in self._first(a) + self._passed(a):
                cp.wait_send()
            self._local(a).wait()


class _GatherRelayed(_GatherTwoLevel):
    def _first(self, a):
        return [self._copy(a, 0, self.me, self.sibling, True)] + [
            self._copy(a, 1 + j, self.me, (*self.chips[j], self.core), True) for j in range(2)]

    def _onward(self, a):
        x, y, c = self.me
        north = c == 1
        source = (jnp.where(north, 1 - x, x), jnp.where(north, y, 1 - y), c)
        target = (jnp.where(north, x, 1 - x), jnp.where(north, 1 - y, y), c)
        return self._copy(a, 3, source, target)

    def relay(self):
        for a in range(len(self.arrs)):
            for j in range(2):
                self._copy(a, 1 + j, (*self.chips[j], self.core), self.me).wait_recv()
                self._passed(a)[j].start()
            self._onward(a).start()
        for a in range(len(self.arrs)):
            self._copy(a, 3, (*self.chips[2], self.core), self.me).wait_recv()
            self._passed(a)[2].start()

    def finish(self):
        for a in range(len(self.arrs)):
            self._copy(a, 0, self.sibling, self.me).wait_recv()
            for j, chip in enumerate(self.chips):
                self._copy(a, 4 + j, (*chip, 1 - self.core), self.me).wait_recv()
        for a in range(len(self.arrs)):
            for cp in self._first(a) + [self._onward(a)] + self._passed(a):
                cp.wait_send()
            self._local(a).wait()


class _Direct(_Comm):
    def __init__(self, arrs, broadcast):
        super().__init__(arrs)
        self.broadcast = broadcast
        self.out_shape = [jax.ShapeDtypeStruct(((N_DEV,) + a.shape) if broadcast else a.shape, a.dtype)
                          for a in self.arrs]

    def _outgoing(self, a, to):
        return self.srcs[a] if self.broadcast else self.srcs[a].at[_linear(to)]

    def _copy(self, a, k, sender, to):
        return pltpu.make_async_remote_copy(
            src_ref=self._outgoing(a, to), dst_ref=self.dsts[a].at[_linear(sender)],
            send_sem=self.send_sems.at[self.PEER_COPIES * a + k], recv_sem=self.recv_sems.at[self.PEER_COPIES * a + k],
            device_id=to, device_id_type=MESH_ID)

    def _local(self, a):
        return pltpu.make_async_copy(self._outgoing(a, self.me), self.dsts[a].at[_linear(self.me)],
                                     self.local_sems.at[a])

    def start(self):
        for a in range(len(self.arrs)):
            self._local(a).start()
            for k, peer in enumerate(self.peers):
                self._copy(a, k, self.me, peer).start()

    def finish(self):
        for a in range(len(self.arrs)):
            for k, peer in enumerate(self.peers):
                self._copy(a, k, peer, self.me).wait_recv()
        for a in range(len(self.arrs)):
            for k, peer in enumerate(self.peers):
                self._copy(a, k, self.me, peer).wait_send()
            self._local(a).wait()


def _run_comm(comm, name):
    n = len(comm.arrs)

    def body(*refs):
        comm.bind(refs[:n], refs[n:2 * n], refs[2 * n:])
        comm.start()
        comm.relay()
        comm.finish()

    return _pc(body, name=name, out_shape=comm.out_shape, in_specs=[ANY] * n, out_specs=[ANY] * n,
               scratch_shapes=comm.scratch_shapes)(*comm.arrs)


def _call(body, inputs, *, name, grid, in_specs, out_specs, out_shape, scratch_shapes=(), semantics,
          comm=None, relay_at=0.5):
    if comm is None:
        out = _pc(body, name=name, grid=grid, in_specs=in_specs, out_specs=out_specs, out_shape=out_shape,
                  scratch_shapes=list(scratch_shapes), compiler_params=_params(semantics))(*inputs)
        return list(out), []
    n_in, n_out, n_sc, k = len(in_specs), len(out_specs), len(scratch_shapes), len(comm.arrs)
    steps = math.prod(grid)
    relay_step = min(steps - 1, int(steps * relay_at))

    def carrier(*refs):
        ins, rest = refs[:n_in], refs[n_in:]
        csrc, rest = rest[:k], rest[k:]
        outs, rest = rest[:n_out], rest[n_out:]
        cdst, rest = rest[:k], rest[k:]
        scratch, csems = rest[:n_sc], rest[n_sc:]
        step = 0
        for axis, size in enumerate(grid):
            step = step * size + pl.program_id(axis)
        comm.bind(csrc, cdst, csems)
        pl.when(step == 0)(comm.start)
        pl.when(step == relay_step)(comm.relay)
        body(*ins, *outs, *scratch)
        pl.when(step == steps - 1)(comm.finish)

    out = _pc(carrier, name=name, grid=grid, in_specs=list(in_specs) + [ANY] * k,
              out_specs=list(out_specs) + [ANY] * k, out_shape=list(out_shape) + comm.out_shape,
              scratch_shapes=list(scratch_shapes) + comm.scratch_shapes,
              compiler_params=_params(("arbitrary",) * len(grid)))(*inputs, *comm.arrs)
    return list(out[:n_out]), list(out[n_out:])


def _matmul(a, b, mode, tm, tn, tk, out_dtype, name, comm=None, relay_at=0.5):
    if mode == "nn":
        (m, kk), nn = a.shape, b.shape[1]
        a_spec = pl.BlockSpec((tm, tk), lambda i, j, k: (i, k))
        b_spec = pl.BlockSpec((tk, tn), lambda i, j, k: (k, j))
        dot = _dot
    elif mode == "nt":
        (m, kk), nn = a.shape, b.shape[0]
        a_spec = pl.BlockSpec((tm, tk), lambda i, j, k: (i, k))
        b_spec = pl.BlockSpec((tn, tk), lambda i, j, k: (j, k))
        dot = _dot_nt
    else:
        (kk, m), nn = a.shape, b.shape[1]
        a_spec = pl.BlockSpec((tk, tm), lambda i, j, k: (k, i))
        b_spec = pl.BlockSpec((tk, tn), lambda i, j, k: (k, j))
        dot = _dot_tn
    assert m % tm == 0 and nn % tn == 0 and kk % tk == 0, (a.shape, b.shape, tm, tn, tk)
    nk = kk // tk
    narrow = nk > 1 and out_dtype != F32

    def body(a_ref, b_ref, o_ref, *acc):
        p = dot(a_ref[...], b_ref[...])
        if nk == 1:
            o_ref[...] = p.astype(o_ref.dtype)
            return
        k = pl.program_id(2)
        acc_ref = acc[0] if narrow else o_ref

        @pl.when(k == 0)
        def _():
            acc_ref[...] = p

        @pl.when((k > 0) & (k < nk - 1) if narrow else k > 0)
        def _():
            acc_ref[...] += p

        if narrow:
            @pl.when(k == nk - 1)
            def _():
                o_ref[...] = (acc_ref[...] + p).astype(o_ref.dtype)

    out, moved = _call(
        body, (a, b), name=name, grid=(m // tm, nn // tn, nk),
        in_specs=[a_spec, b_spec], out_specs=[pl.BlockSpec((tm, tn), lambda i, j, k: (i, j))],
        out_shape=[jax.ShapeDtypeStruct((m, nn), out_dtype)],
        scratch_shapes=[pltpu.VMEM((tm, tn), F32)] if narrow else [],
        semantics=("parallel", "parallel", "arbitrary"), comm=comm, relay_at=relay_at)
    return out[0] if comm is None else (out[0], moved)


def _norm_fwd(x_prev, f_prev, gate_prev, coef, g, sc, sh, name, tm=512, comm=None, relay_at=0.5):
    t, d = x_prev.shape
    residual = f_prev is not None

    def body(*refs):
        if residual:
            xp_ref, fp_ref, gp_ref, g_ref, sc_ref, sh_ref, x_ref, h_ref = refs
            x = xp_ref[...] + coef * gp_ref[...] * fp_ref[...]
            x_ref[...] = x
        else:
            xp_ref, g_ref, sc_ref, sh_ref, h_ref = refs
            x = xp_ref[...]
        y, _ = _rms(x)
        h_ref[...] = ((y * g_ref[...]) * (1.0 + sc_ref[...]) + sh_ref[...]).astype(BF16)

    row, vec = _row_spec(tm, d), _vec_spec(d)
    if residual:
        ins, in_specs = (x_prev, f_prev, gate_prev, g, sc, sh), [row, row, vec, vec, vec, vec]
        out_shape = [jax.ShapeDtypeStruct((t, d), F32), jax.ShapeDtypeStruct((t, d), BF16)]
        out_specs = [row, row]
    else:
        ins, in_specs = (x_prev, g, sc, sh), [row, vec, vec, vec]
        out_shape = [jax.ShapeDtypeStruct((t, d), BF16)]
        out_specs = [row]
    out, moved = _call(body, ins, name=name, grid=(t // tm,), in_specs=in_specs, out_specs=out_specs,
                       out_shape=out_shape, semantics=("parallel",), comm=comm, relay_at=relay_at)
    out = out if residual else out[0]
    return out if comm is None else (out, moved)


def _norm_bwd(dh, x, dxo, fo, coef_o, g, sc, gate_prev, coef_prev, name, tm=256, comm=None):
    t, d = x.shape
    with_prev = gate_prev is not None

    def body(*refs):
        if with_prev:
            dh_ref, x_ref, dxo_ref, fo_ref, g_ref, sc_ref, gp_ref, dx_ref, dfp_ref, sums_ref = refs
        else:
            dh_ref, x_ref, dxo_ref, fo_ref, g_ref, sc_ref, dx_ref, sums_ref = refs
        dh_v, dxo_v = dh_ref[...], dxo_ref[...]
        y, r = _rms(x_ref[...])
        n = y * g_ref[...]
        dn = dh_v * (1.0 + sc_ref[...])
        dx = dxo_v + _rms_bwd(dn * g_ref[...], y, r)
        dx_ref[...] = dx
        if with_prev:
            dfp_ref[...] = (coef_prev * gp_ref[...] * dx).astype(BF16)
        @pl.when(pl.program_id(0) == 0)
        def _():
            sums_ref[...] = jnp.zeros_like(sums_ref)

        sums_ref[0:1, :] += jnp.sum(dh_v * n, axis=0, keepdims=True)
        sums_ref[1:2, :] += jnp.sum(dh_v, axis=0, keepdims=True)
        sums_ref[2:3, :] += jnp.sum(dn * y, axis=0, keepdims=True)
        sums_ref[3:4, :] += jnp.sum(coef_o * dxo_v * fo_ref[...], axis=0, keepdims=True)

    row, vec = _row_spec(tm, d), _vec_spec(d)
    sums_spec = pl.BlockSpec((8, d), lambda i: (0, 0))
    ins, in_specs = [dh, x, dxo, fo, g, sc], [row, row, row, row, vec, vec]
    out_shape, out_specs = [jax.ShapeDtypeStruct((t, d), F32)], [row]
    if with_prev:
        ins.append(gate_prev)
        in_specs.append(vec)
        out_shape.append(jax.ShapeDtypeStruct((t, d), BF16))
        out_specs.append(row)
    out_shape.append(jax.ShapeDtypeStruct((8, d), F32))
    out_specs.append(sums_spec)
    out, moved = _call(body, ins, name=name, grid=(t // tm,), in_specs=in_specs, out_specs=out_specs,
                       out_shape=out_shape, semantics=("arbitrary",), comm=comm)
    return out if comm is None else (out, moved)


def _ffn_up(h, w1t, w3t, name, tf=512, comm=None, relay_at=0.5):
    t, d = h.shape
    ff = w1t.shape[0]
    tm = min(FFN_ROWS, t)

    def body(h_ref, w1_ref, w3_ref, ga_ref, gb_ref, s_ref):
        hv = h_ref[...]
        a = _dot_nt(hv, w1_ref[...])
        b = _dot_nt(hv, w3_ref[...])
        sg = jax.nn.sigmoid(a)
        sil = a * sg
        ga_ref[...] = (b * (sg * (1.0 + a * (1.0 - sg)))).astype(BF16)
        gb_ref[...] = sil.astype(BF16)
        s_ref[...] = (sil * b).astype(BF16)

    tile = pl.BlockSpec((tm, tf), lambda i, j: (i, j))
    cols = pl.BlockSpec((tf, d), lambda i, j: (j, 0))
    out, moved = _call(
        body, (h, w1t, w3t), name=name, grid=(t // tm, ff // tf),
        in_specs=[pl.BlockSpec((tm, d), lambda i, j: (i, 0)), cols, cols], out_specs=[tile, tile, tile],
        out_shape=[jax.ShapeDtypeStruct((t, ff), BF16)] * 3,
        semantics=("parallel", "parallel"), comm=comm, relay_at=relay_at)
    return out if comm is None else (out, moved)


def _ffn_dact(df, ga, gb, w2, name, tf=512, comm=None):
    t, d = df.shape
    ff = ga.shape[1]
    tm = min(2 * FFN_ROWS, t)

    def body(df_ref, ga_ref, gb_ref, w2_ref, da_ref, db_ref):
        ds = _dot_nt(df_ref[...], w2_ref[...])
        da_ref[...] = (ds * ga_ref[...]).astype(BF16)
        db_ref[...] = (ds * gb_ref[...]).astype(BF16)

    tile = pl.BlockSpec((tm, tf), lambda i, j: (i, j))
    out, moved = _call(
        body, (df, ga, gb, w2), name=name, grid=(t // tm, ff // tf),
        in_specs=[pl.BlockSpec((tm, d), lambda i, j: (i, 0)), tile, tile, pl.BlockSpec((tf, d), lambda i, j: (j, 0))],
        out_specs=[tile, tile], out_shape=[jax.ShapeDtypeStruct((t, ff), BF16)] * 2,
        semantics=("parallel", "parallel"), comm=comm)
    return out if comm is None else (out, moved)


def _ffn_dh(da, db, w1t, w3t, name, tn=512, comm=None):
    t, ff = da.shape
    d = w1t.shape[1]
    tm, tk = min(FFN_ROWS, t), ff // 2

    def body(da_ref, db_ref, w1_ref, w3_ref, dh_ref):
        p = _dot(da_ref[...], w1_ref[...]) + _dot(db_ref[...], w3_ref[...])
        k = pl.program_id(2)

        @pl.when(k == 0)
        def _():
            dh_ref[...] = p

        @pl.when(k > 0)
        def _():
            dh_ref[...] += p

    act = pl.BlockSpec((tm, tk), lambda i, j, k: (i, k))
    wgt = pl.BlockSpec((tk, tn), lambda i, j, k: (k, j))
    out, moved = _call(
        body, (da, db, w1t, w3t), name=name, grid=(t // tm, d // tn, 2),
        in_specs=[act, act, wgt, wgt], out_specs=[pl.BlockSpec((tm, tn), lambda i, j, k: (i, j))],
        out_shape=[jax.ShapeDtypeStruct((t, d), F32)],
        semantics=("parallel", "parallel", "arbitrary"), comm=comm)
    return out[0] if comm is None else (out[0], moved)


def _loss_head(x_prev, f_prev, gate_prev, coef, target, name, tm=512):
    t, d = x_prev.shape
    steps = t // tm

    def body(xp_ref, fp_ref, gp_ref, tg_ref, dy_ref, df_ref, loss_ref, acc_ref):
        i = pl.program_id(0)
        e = xp_ref[...] + coef * gp_ref[...] * fp_ref[...] - tg_ref[...]
        dy = e * (1.0 / d)
        dy_ref[...] = dy
        df_ref[...] = (coef * gp_ref[...] * dy).astype(BF16)
        part = jnp.sum(e * e, axis=0, keepdims=True)

        @pl.when(i == 0)
        def _():
            acc_ref[...] = part

        @pl.when(i > 0)
        def _():
            acc_ref[...] += part

        @pl.when(i == steps - 1)
        def _():
            loss_ref[...] = jnp.sum(acc_ref[...], axis=1, keepdims=True) * (0.5 / d)

    row, vec = _row_spec(tm, d), _vec_spec(d)
    return _pc(
        body, name=name, grid=(steps,),
        in_specs=[row, row, vec, row], out_specs=[row, row, pl.BlockSpec((1, 1), lambda i: (0, 0))],
        out_shape=[jax.ShapeDtypeStruct((t, d), F32), jax.ShapeDtypeStruct((t, d), BF16),
                   jax.ShapeDtypeStruct((1, 1), F32)],
        scratch_shapes=[pltpu.VMEM((1, d), F32)],
        compiler_params=_params(("arbitrary",)),
    )(x_prev, f_prev, gate_prev, target)


def _split3(x):
    hi = x.astype(BF16)
    r1 = x - hi.astype(F32)
    mid = r1.astype(BF16)
    lo = (r1 - mid.astype(F32)).astype(BF16)
    return hi, mid, lo


def _select_matmul(a, onehot, name):
    m, n = a.shape[0], onehot.shape[1]

    def body(a_ref, oh_ref, o_ref):
        hi, mid, lo = _split3(a_ref[...])
        oh = oh_ref[...]
        o_ref[...] = (_dot(hi, oh) + _dot(mid, oh)) + _dot(lo, oh)

    return _pc(body, name=name, out_shape=jax.ShapeDtypeStruct((m, n), F32), compiler_params=_params())(a, onehot)


def _bucket_onehot():
    qi = np.arange(BLOCK)[:, None]
    kj = np.arange(2 * BLOCK)[None, :]
    dist = np.clip(qi + BLOCK - kj, 0, None)
    nf = np.maximum(dist, 1).astype(np.float32)
    large = 16 + (np.log(nf / np.float32(16)) / np.float32(math.log(128 / 16)) * np.float32(16)).astype(np.int32)
    bucket = np.where(dist < 16, dist, np.minimum(large, N_BUCKETS - 1)).reshape(-1)
    return (bucket[None, :] == np.arange(N_BUCKETS)[:, None]).astype(np.float32)


def _window_mask(n):
    row = lax.broadcasted_iota(jnp.int32, (GROUP * BLOCK, 2 * BLOCK), 0)
    qi = row & (BLOCK - 1)
    kj = lax.broadcasted_iota(jnp.int32, (GROUP * BLOCK, 2 * BLOCK), 1)
    return (kj > qi) & (kj <= qi + BLOCK) & ((kj >= BLOCK) | (n > 0))


def _kv_band(zc_ref, zp_ref, kvh):
    lo, hi = kvh * HEAD_DIM, (kvh + 1) * HEAD_DIM
    k_raw = jnp.concatenate([zp_ref[:, lo:hi], zc_ref[:, K_OFF + lo:K_OFF + hi]], axis=0)
    v_raw = jnp.concatenate([zp_ref[:, KV_WIDTH + lo:KV_WIDTH + hi], zc_ref[:, V_OFF + lo:V_OFF + hi]], axis=0)
    return k_raw, v_raw


def _stack_heads(ref, off):
    return jnp.concatenate([ref[:, off + g * HEAD_DIM:off + (g + 1) * HEAD_DIM] for g in range(GROUP)], axis=0)


def _unstack_heads(ref, off, stacked):
    for g in range(GROUP):
        ref[:, off + g * HEAD_DIM:off + (g + 1) * HEAD_DIM] = stacked[g * BLOCK:(g + 1) * BLOCK]


def _group_softmax(qn, kband, bias_ref, sink_ref, kvh, valid):
    bias = bias_ref[kvh * GROUP:(kvh + 1) * GROUP].reshape(GROUP * BLOCK, 2 * BLOCK)
    s = _dot_nt(qn, kband) * (HEAD_DIM ** -0.5) + bias
    s = jnp.where(valid, s, MASK_VALUE)
    sink = jnp.concatenate([jnp.full((BLOCK, 1), sink_ref[0, kvh * GROUP + g], F32) for g in range(GROUP)], axis=0)
    m = jnp.maximum(jnp.max(s, axis=-1, keepdims=True), sink)
    p = jnp.exp(s - m)
    esink = jnp.exp(sink - m)
    inv = 1.0 / (jnp.sum(p, axis=-1, keepdims=True) + esink)
    return p * inv, esink * inv, inv


def _mix_specs(nb, order):
    cur = pl.BlockSpec((BLOCK, IN_COLS), lambda n: (order(n), 0))
    prev = pl.BlockSpec((BLOCK, 2 * KV_WIDTH), lambda n: (jnp.maximum(order(n) - 1, 0), K_OFF // (2 * KV_WIDTH)))
    full = lambda shape: pl.BlockSpec(shape, lambda n: (0,) * len(shape))
    params = [full((A_HEADS, BLOCK, BLOCK)), full((BLOCK, A_HEADS)), full((A_HEADS, BLOCK)),
              full((1, HEAD_DIM)), full((1, HEAD_DIM)), pl.BlockSpec(memory_space=pltpu.SMEM),
              full((B_HEADS, BLOCK, 2 * BLOCK))]
    return cur, prev, params, full


def _mix_fwd(z, sw, sb_t, gv, gq, gk, sinks, bias, name, comm=None, relay_at=0.5):
    t = z.shape[0]
    nb = t // BLOCK

    def body(zc_ref, zp_ref, sw_ref, sbt_ref, gv_ref, gq_ref, gk_ref, sink_ref, bias_ref, y_ref):
        n = pl.program_id(0)
        ri = lax.broadcasted_iota(jnp.int32, (BLOCK, BLOCK), 0)
        ci = lax.broadcasted_iota(jnp.int32, (BLOCK, BLOCK), 1)
        tril = ri >= ci
        for h in range(A_HEADS):
            u = _gelu(zc_ref[:, h * BLOCK:(h + 1) * BLOCK])
            vv = _gelu(zc_ref[:, A_WIDTH + h * BLOCK:A_WIDTH + (h + 1) * BLOCK])
            vhat, _ = _rms(vv)
            vn = (vhat * gv_ref[h:h + 1, :]).astype(BF16)
            w = jnp.where(tril, sw_ref[h], 0.0).astype(BF16)
            mixed = _dot(w, vn) + sbt_ref[:, h:h + 1]
            y_ref[:, h * BLOCK:(h + 1) * BLOCK] = (u * mixed).astype(BF16)
        valid = _window_mask(n)
        for kvh in range(B_KV_HEADS):
            k_raw, v_raw = _kv_band(zc_ref, zp_ref, kvh)
            khat, _ = _rms(k_raw)
            kband = (khat * gk_ref[...]).astype(BF16)
            vband = v_raw.astype(BF16)
            qhat, _ = _rms(_stack_heads(zc_ref, Q_OFF + kvh * GROUP * HEAD_DIM))
            qn = (qhat * gq_ref[...]).astype(BF16)
            w, _, _ = _group_softmax(qn, kband, bias_ref, sink_ref, kvh, valid)
            o = _dot(w.astype(BF16), vband)
            _unstack_heads(y_ref, A_WIDTH + kvh * GROUP * HEAD_DIM, o.astype(BF16))

    cur, prev, params, _ = _mix_specs(nb, lambda n: n)
    out, moved = _call(
        body, (z, z, sw, sb_t, gv, gq, gk, sinks, bias), name=name, grid=(nb,), in_specs=[cur, prev] + params,
        out_specs=[pl.BlockSpec((BLOCK, A_WIDTH + B_WIDTH), lambda n: (n, 0))],
        out_shape=[jax.ShapeDtypeStruct((t, A_WIDTH + B_WIDTH), BF16)],
        semantics=("parallel",), comm=comm, relay_at=relay_at)
    return out[0] if comm is None else (out[0], moved)


def _mix_bwd(z, dy, sw, sb_t, gv, gq, gk, sinks, bias, name, comm=None):
    t = z.shape[0]
    nb = t // BLOCK

    def body(zc_ref, zp_ref, dy_ref, sw_ref, sbt_ref, gv_ref, gq_ref, gk_ref, sink_ref, bias_ref,
             dz_ref, dsw_ref, dsb_ref, dgv_ref, dgq_ref, dgk_ref, dsink_ref, dbias_ref, carry_ref):
        step = pl.program_id(0)
        n = nb - 1 - step

        @pl.when(step == 0)
        def _():
            dsw_ref[...] = jnp.zeros_like(dsw_ref)
            dsb_ref[...] = jnp.zeros_like(dsb_ref)
            dgv_ref[...] = jnp.zeros_like(dgv_ref)
            dgq_ref[...] = jnp.zeros_like(dgq_ref)
            dgk_ref[...] = jnp.zeros_like(dgk_ref)
            dsink_ref[...] = jnp.zeros_like(dsink_ref)
            dbias_ref[...] = jnp.zeros_like(dbias_ref)
            carry_ref[...] = jnp.zeros_like(carry_ref)

        ri = lax.broadcasted_iota(jnp.int32, (BLOCK, BLOCK), 0)
        ci = lax.broadcasted_iota(jnp.int32, (BLOCK, BLOCK), 1)
        tril = ri >= ci
        dsb = jnp.zeros((BLOCK, BLOCK), F32)
        for h in range(A_HEADS):
            uo, vo = h * BLOCK, A_WIDTH + h * BLOCK
            u_raw, v_raw = zc_ref[:, uo:uo + BLOCK], zc_ref[:, vo:vo + BLOCK]
            (u, du_raw), (vv, dv_raw) = _gelu_and_grad(u_raw), _gelu_and_grad(v_raw)
            vhat, r = _rms(vv)
            vn = (vhat * gv_ref[h:h + 1, :]).astype(BF16)
            w = jnp.where(tril, sw_ref[h], 0.0).astype(BF16)
            mixed = _dot(w, vn) + sbt_ref[:, h:h + 1]
            dya = dy_ref[:, uo:uo + BLOCK]
            dmixed = dya * u
            dm16 = dmixed.astype(BF16)
            dsw_ref[h] += jnp.where(tril, _dot_nt(dm16, vn), 0.0)
            dsb = dsb + jnp.where(ci == h, jnp.sum(dmixed, axis=1, keepdims=True), 0.0)
            dvn = _dot_tn(w, dm16)
            dgv_ref[h:h + 1, :] += jnp.sum(dvn * vhat, axis=0, keepdims=True)
            dvv = _rms_bwd(dvn * gv_ref[h:h + 1, :], vhat, r)
            dz_ref[:, uo:uo + BLOCK] = (dya * mixed * du_raw).astype(BF16)
            dz_ref[:, vo:vo + BLOCK] = (dvv * dv_raw).astype(BF16)
        dsb_ref[...] += dsb

        valid = _window_mask(n)
        lane = lax.broadcasted_iota(jnp.int32, (1, BLOCK), 1)
        dsink = jnp.zeros((1, BLOCK), F32)
        dgq = jnp.zeros((1, HEAD_DIM), F32)
        for kvh in range(B_KV_HEADS):
            ko, vo = K_OFF + kvh * HEAD_DIM, V_OFF + kvh * HEAD_DIM
            k_raw, v_raw = _kv_band(zc_ref, zp_ref, kvh)
            khat, kr = _rms(k_raw)
            kband = (khat * gk_ref[...]).astype(BF16)
            vband = v_raw.astype(BF16)
            qo = Q_OFF + kvh * GROUP * HEAD_DIM
            qhat, qr = _rms(_stack_heads(zc_ref, qo))
            qn = (qhat * gq_ref[...]).astype(BF16)
            w, wsink, _ = _group_softmax(qn, kband, bias_ref, sink_ref, kvh, valid)
            do = _stack_heads(dy_ref, A_WIDTH + kvh * GROUP * HEAD_DIM).astype(BF16)
            dvb = _dot_tn(w.astype(BF16), do)
            dw = _dot_nt(do, vband)
            rowdot = jnp.sum(w * dw, axis=-1, keepdims=True)
            ds = w * (dw - rowdot)
            dsink_rows = -wsink * rowdot
            for g in range(GROUP):
                head_sum = jnp.sum(dsink_rows[g * BLOCK:(g + 1) * BLOCK], axis=0, keepdims=True)
                dsink = dsink + jnp.where(lane == kvh * GROUP + g, head_sum, 0.0)
            dbias_ref[kvh * GROUP:(kvh + 1) * GROUP] += ds.reshape(GROUP, BLOCK, 2 * BLOCK)
            ds16 = (ds * (HEAD_DIM ** -0.5)).astype(BF16)
            dqn = _dot(ds16, kband)
            dkn = _dot_tn(ds16, qn)
            dgq = dgq + jnp.sum(dqn * qhat, axis=0, keepdims=True)
            _unstack_heads(dz_ref, qo, _rms_bwd(dqn * gq_ref[...], qhat, qr).astype(BF16))
            dgk_ref[...] += jnp.sum(dkn * khat, axis=0, keepdims=True)
            dk = _rms_bwd(dkn * gk_ref[...], khat, kr)
            co = kvh * HEAD_DIM
            dz_ref[:, ko:ko + HEAD_DIM] = (dk[BLOCK:] + carry_ref[:, co:co + HEAD_DIM]).astype(BF16)
            dz_ref[:, vo:vo + HEAD_DIM] = (dvb[BLOCK:] + carry_ref[:, KV_WIDTH + co:KV_WIDTH + co + HEAD_DIM]).astype(BF16)
            carry_ref[:, co:co + HEAD_DIM] = dk[:BLOCK]
            carry_ref[:, KV_WIDTH + co:KV_WIDTH + co + HEAD_DIM] = dvb[:BLOCK]
        dgq_ref[...] += dgq
        dsink_ref[...] += dsink

    order = lambda n: nb - 1 - n
    cur, prev, params, full = _mix_specs(nb, order)
    dy_spec = pl.BlockSpec((BLOCK, A_WIDTH + B_WIDTH), lambda n: (order(n), 0))
    out, moved = _call(
        body, (z, z, dy, sw, sb_t, gv, gq, gk, sinks, bias), name=name, grid=(nb,),
        in_specs=[cur, prev, dy_spec] + params,
        out_specs=[cur, full((A_HEADS, BLOCK, BLOCK)), full((BLOCK, BLOCK)), full((A_HEADS, BLOCK)),
                   full((1, HEAD_DIM)), full((1, HEAD_DIM)), full((1, BLOCK)), full((B_HEADS, BLOCK, 2 * BLOCK))],
        out_shape=[jax.ShapeDtypeStruct((t, IN_COLS), BF16), jax.ShapeDtypeStruct((A_HEADS, BLOCK, BLOCK), F32),
                   jax.ShapeDtypeStruct((BLOCK, BLOCK), F32), jax.ShapeDtypeStruct((A_HEADS, BLOCK), F32),
                   jax.ShapeDtypeStruct((1, HEAD_DIM), F32), jax.ShapeDtypeStruct((1, HEAD_DIM), F32),
                   jax.ShapeDtypeStruct((1, BLOCK), F32), jax.ShapeDtypeStruct((B_HEADS, BLOCK, 2 * BLOCK), F32)],
        scratch_shapes=[pltpu.VMEM((BLOCK, 2 * KV_WIDTH), F32)],
        semantics=("arbitrary",), comm=comm)
    return out if comm is None else (out, moved)


def _ada_fwd(c_all, w_ada, b_cols, name, tn=768):
    nb, d = c_all.shape
    cols = w_ada.shape[1]

    def body(c_ref, w_ref, b_ref, act_ref, mod_ref):
        cv = c_ref[...]
        act = cv * jax.nn.sigmoid(cv)
        act_ref[...] = act
        mod_ref[...] = _dot(act.astype(BF16), w_ref[...].astype(BF16)) + b_ref[...]

    return _pc(
        body, name=name, grid=(cols // tn,),
        in_specs=[pl.BlockSpec((nb, d), lambda j: (0, 0)), pl.BlockSpec((d, tn), lambda j: (0, j)),
                  pl.BlockSpec((1, tn), lambda j: (0, j))],
        out_specs=[pl.BlockSpec((nb, d), lambda j: (0, 0)), pl.BlockSpec((nb, tn), lambda j: (0, j))],
        out_shape=[jax.ShapeDtypeStruct((nb, d), F32), jax.ShapeDtypeStruct((nb, cols), F32)],
        compiler_params=_params(("arbitrary",)),
    )(c_all, w_ada, b_cols)


def _adamw(w, g, m, v):
    m = ADAM_B1 * m + (1.0 - ADAM_B1) * g
    v = ADAM_B2 * v + (1.0 - ADAM_B2) * (g * g)
    m_hat = m / (1.0 - ADAM_B1 ** ADAM_STEP)
    v_hat = v / (1.0 - ADAM_B2 ** ADAM_STEP)
    delta = -ADAM_LR * (m_hat / (jnp.sqrt(v_hat) + ADAM_EPS) + ADAM_WD * w)
    return delta, m, v


def _adam_outer(act_t, dmod, w, m, v, name, tr=256):
    rows, cols = w.shape
    nb = dmod.shape[0]

    def body(act_ref, dm_ref, w_ref, m_ref, v_ref, g_ref, d_ref, nm_ref, nv_ref):
        act = act_ref[...].astype(BF16).astype(F32)
        dm = dm_ref[...].astype(BF16).astype(F32)
        g = act[:, 0:1] * dm[0:1, :]
        for b in range(1, nb):
            g = g + act[:, b:b + 1] * dm[b:b + 1, :]
        g_ref[...] = g
        d_ref[...], nm_ref[...], nv_ref[...] = _adamw(w_ref[...], g, m_ref[...], v_ref[...])

    tile = pl.BlockSpec((tr, cols), lambda i: (i, 0))
    return _pc(
        body, name=name, grid=(rows // tr,),
        in_specs=[pl.BlockSpec((tr, nb), lambda i: (i, 0)), pl.BlockSpec((nb, cols), lambda i: (0, 0)), tile, tile, tile],
        out_specs=[tile] * 4, out_shape=[jax.ShapeDtypeStruct((rows, cols), F32)] * 4,
        compiler_params=_params(("parallel",)),
    )(act_t, dmod, w, m, v)


def _adam_reduce(parts, w, m, v, name, tr, transposed=False):
    rows, cols = w.shape

    def body(p_ref, w_ref, m_ref, v_ref, g_ref, d_ref, nm_ref, nv_ref):
        g = p_ref[0].astype(F32)
        for i in range(1, N_DEV):
            g = g + p_ref[i].astype(F32)
        if transposed:
            g = g.T
        g_ref[...] = g
        d_ref[...], nm_ref[...], nv_ref[...] = _adamw(w_ref[...], g, m_ref[...], v_ref[...])

    tile = pl.BlockSpec((tr, cols), lambda i: (i, 0))
    parts_spec = (pl.BlockSpec((N_DEV, cols, tr), lambda i: (0, 0, i)) if transposed
                  else pl.BlockSpec((N_DEV, tr, cols), lambda i: (0, i, 0)))
    return _pc(
        body, name=name, grid=(rows // tr,),
        in_specs=[parts_spec, tile, tile, tile],
        out_specs=[tile] * 4, out_shape=[jax.ShapeDtypeStruct((rows, cols), F32)] * 4,
        compiler_params=_params(("parallel",)),
    )(parts, w, m, v)


def _pad_rows(v):
    flat = v.reshape(-1)
    pad = (-flat.shape[0]) % (8 * BLOCK)
    return jnp.pad(flat, (0, pad)).reshape(-1, BLOCK)


def kernel(x, c, w_ada, b_ada, g_ffn1, w1_ffn1, w3_ffn1, w2_ffn1, g_mix, w_in, spatial_w, spatial_b, g_v, g_q, g_k, sinks, rel_bias, w_out, g_ffn2, w1_ffn2, w3_ffn2, w2_ffn2, loss_target, m_w_ada, m_b_ada, m_g_ffn1, m_w1_ffn1, m_w3_ffn1, m_w2_ffn1, m_g_mix, m_w_in, m_spatial_w, m_spatial_b, m_g_v, m_g_q, m_g_k, m_sinks, m_rel_bias, m_w_out, m_g_ffn2, m_w1_ffn2, m_w3_ffn2, m_w2_ffn2, v_w_ada, v_b_ada, v_g_ffn1, v_w1_ffn1, v_w3_ffn1, v_w2_ffn1, v_g_mix, v_w_in, v_spatial_w, v_spatial_b, v_g_v, v_g_q, v_g_k, v_sinks, v_rel_bias, v_w_out, v_g_ffn2, v_w1_ffn2, v_w3_ffn2, v_w2_ffn2):
    me = _linear(_position())
    d = D_MODEL
    x0 = x[0]
    target = loss_target[0]
    t = x0.shape[0]

    big = dict(w1a=w1_ffn1[0], w3a=w3_ffn1[0], w2a=w2_ffn1[0], w_in=w_in[0], w_out=w_out[0],
               w1b=w1_ffn2[0], w3b=w3_ffn2[0], w2b=w2_ffn2[0])
    col_sharded = ("w1a", "w3a", "w_in", "w1b", "w3b")
    full = {}

    def gather(keys, scheme=_GatherTwoLevel):
        return scheme([(big[k].T if k in col_sharded else big[k]).astype(BF16) for k in keys])

    def unpack(keys, gathered):
        for k, gth in zip(keys, gathered):
            full[k] = gth.reshape(-1, gth.shape[-1])

    def share(k):
        gk_ = grads[k]
        return gk_.reshape(N_DEV, gk_.shape[0] // N_DEV, gk_.shape[1])

    c_all = _run_comm(_Direct([c], True), "gather_c")[0].reshape(N_DEV, d)
    b_cols = lax.dynamic_slice(b_ada, (0, me * MOD_COLS), (1, MOD_COLS))
    act_all, mod_cols = _ada_fwd(c_all, w_ada[0], b_cols, "ada_fwd")
    mod = _run_comm(_Direct([mod_cols.reshape(N_DEV, 1, MOD_COLS)], False), "scatter_mod")[0]
    mod = mod.reshape(N_MOD, 1, d)
    sh1, sc1, gt1, sh2, sc2, gt2, sh3, sc3, gt3 = [mod[i] for i in range(N_MOD)]

    tm = min(FFN_ROWS, t)
    h1, gathered = _norm_fwd(x0, None, None, 0.0, g_ffn1, sc1, sh1, "norm1_fwd",
                             comm=gather(("w1a", "w3a"), _GatherRelayed), relay_at=0.9)
    unpack(("w1a", "w3a"), gathered)
    (ga1, gb1, s1), gathered = _ffn_up(h1, full["w1a"], full["w3a"], "ffn1_up",
                                     comm=gather(("w2a", "w_in", "w_out")), relay_at=0.6)
    unpack(("w2a", "w_in", "w_out"), gathered)
    f1, gathered = _matmul(s1, full["w2a"], "nn", tm, 512, D_FF, F32, "ffn1_down", comm=gather(("w1b",)),
                           relay_at=0.7)
    unpack(("w1b",), gathered)
    x1, h2 = _norm_fwd(x0, f1, gt1, 0.5, g_mix, sc2, sh2, "norm2_fwd")
    z = _matmul(h2, full["w_in"], "nt", tm, IN_COLS // 2, d, F32, "mix_in_fwd")
    onehot = _bucket_onehot()
    bias = _select_matmul(rel_bias.T, jnp.asarray(onehot, BF16), "bias_table").reshape(B_HEADS, BLOCK, 2 * BLOCK)
    sb_t = spatial_b[0].T
    mix_params = (spatial_w[0], sb_t, g_v[0], g_q, g_k, sinks, bias)
    ycat, gathered = _mix_fwd(z, *mix_params, "mix_core_fwd", comm=gather(("w3b",)), relay_at=0.6)
    unpack(("w3b",), gathered)
    y = _matmul(ycat, full["w_out"], "nn", 512, d, d, F32, "mix_out_fwd")
    x2, h3 = _norm_fwd(x1, y, gt2, 1.0, g_ffn2, sc3, sh3, "norm3_fwd")
    (ga2, gb2, s2), gathered = _ffn_up(h3, full["w1b"], full["w3b"], "ffn2_up", comm=gather(("w2b",)), relay_at=0.6)
    unpack(("w2b",), gathered)
    f2 = _matmul(s2, full["w2b"], "nn", tm, 512, D_FF, F32, "ffn2_down")
    dx3, df2, loss_part = _loss_head(x2, f2, gt3, 0.5, target, "loss_head")
    loss = lax.psum(loss_part[0, 0], ("x", "y", "c"))

    grads, parts = {}, {}
    tk = min(DW_TOKENS, t)

    def exchange(k):
        return _Direct([share(k)], False)

    grads["w2b"] = _matmul(s2, df2, "tn", 512, 1024, tk, BF16, "ffn2_dw2")
    (da, db), moved = _ffn_dact(df2, ga2, gb2, full["w2b"], "ffn2_dact", comm=exchange("w2b"))
    parts["w2b"] = moved[0]
    grads["w1b"] = _matmul(da, h3, "tn", 512, 1024, tk, BF16, "ffn2_dw1")
    dh3, moved = _ffn_dh(da, db, full["w1b"], full["w3b"], "ffn2_dh", comm=exchange("w1b"))
    parts["w1b"] = moved[0]
    grads["w3b"] = _matmul(db, h3, "tn", 512, 1024, tk, BF16, "ffn2_dw3")
    dx2, dyg, sums3 = _norm_bwd(dh3, x2, dx3, f2, 0.5, g_ffn2, sc3, gt2, 1.0, "norm3_bwd")
    dycat = _matmul(dyg, full["w_out"], "nt", 512, d, d, F32, "mix_out_bwd")
    grads["w_out"] = _matmul(ycat, dyg, "tn", 1024, 512, tk, BF16, "mix_out_dw")
    mix_grads, moved = _mix_bwd(z, dycat, *mix_params, "mix_core_bwd",
                                comm=_Direct([share("w3b"), share("w_out")], False))
    parts["w3b"], parts["w_out"] = moved
    dz, d_sw, d_sb, d_gv, d_gq, d_gk, d_sink, d_bias = mix_grads
    d_rel = _select_matmul(d_bias.reshape(B_HEADS, BLOCK * 2 * BLOCK), jnp.asarray(onehot.T, BF16), "bias_table_bwd")
    dh2 = _matmul(dz, full["w_in"], "nn", tm, d, IN_COLS // 2, F32, "mix_in_bwd")
    grads["w_in"] = _matmul(dz, h2, "tn", IN_COLS // 2, 512, tk, BF16, "mix_in_dw")
    dx1, df1, sums2 = _norm_bwd(dh2, x1, dx2, y, 1.0, g_mix, sc2, gt1, 0.5, "norm2_bwd")
    grads["w2a"], moved = _matmul(s1, df1, "tn", 512, 1024, tk, BF16, "ffn1_dw2", comm=exchange("w_in"))
    parts["w_in"] = moved[0]
    (da, db), moved = _ffn_dact(df1, ga1, gb1, full["w2a"], "ffn1_dact", comm=exchange("w2a"))
    parts["w2a"] = moved[0]
    early = [("g_mix", sums2[2:3], g_mix, m_g_mix, v_g_mix),
             ("spatial_w", d_sw[None], spatial_w, m_spatial_w, v_spatial_w),
             ("spatial_b", d_sb[:, :A_HEADS].T[None], spatial_b, m_spatial_b, v_spatial_b),
             ("g_v", d_gv[None], g_v, m_g_v, v_g_v),
             ("g_q", d_gq, g_q, m_g_q, v_g_q),
             ("g_k", d_gk, g_k, m_g_k, v_g_k),
             ("sinks", d_sink[:, :B_HEADS], sinks, m_sinks, v_sinks),
             ("rel_bias", d_rel.T, rel_bias, m_rel_bias, v_rel_bias),
             ("g_ffn2", sums3[2:3], g_ffn2, m_g_ffn2, v_g_ffn2)]

    def pack(items, i):
        return jnp.concatenate([_pad_rows(item[i]) for item in items], axis=0)

    grads["w1a"], moved = _matmul(da, h1, "tn", 512, 1024, tk, BF16, "ffn1_dw1", comm=_Direct([pack(early, 1)], True))
    early_all = moved[0]
    grads["w3a"], moved = _matmul(db, h1, "tn", 512, 1024, tk, BF16, "ffn1_dw3", comm=exchange("w1a"))
    parts["w1a"] = moved[0]
    dh1, moved = _ffn_dh(da, db, full["w1a"], full["w3a"], "ffn1_dh", comm=exchange("w3a"))
    parts["w3a"] = moved[0]
    dx0, sums1 = _norm_bwd(dh1, x0, dx1, f1, 0.5, g_ffn1, sc1, None, 0.0, "norm1_bwd")

    moments = dict(w1a=(m_w1_ffn1, v_w1_ffn1), w3a=(m_w3_ffn1, v_w3_ffn1), w2a=(m_w2_ffn1, v_w2_ffn1),
                   w_in=(m_w_in, v_w_in), w_out=(m_w_out, v_w_out),
                   w1b=(m_w1_ffn2, v_w1_ffn2), w3b=(m_w3_ffn2, v_w3_ffn2), w2b=(m_w2_ffn2, v_w2_ffn2))
    upd = {}
    for k in big:
        rows = big[k].shape[0]
        tr = 256 if rows % 256 == 0 else rows // 4
        upd[k] = [o[None] for o in _adam_reduce(parts[k], big[k], moments[k][0][0], moments[k][1][0], "adam_" + k, tr,
                                                transposed=k in col_sharded)]

    dmod = jnp.concatenate([sums1[1:2], sums1[0:1], sums1[3:4], sums2[1:2], sums2[0:1], sums2[3:4],
                            sums3[1:2], sums3[0:1], sums3[3:4]], axis=1)
    late = [("b_ada", dmod, b_ada, m_b_ada, v_b_ada), ("g_ffn1", sums1[2:3], g_ffn1, m_g_ffn1, v_g_ffn1)]
    late_all = _run_comm(_Direct([pack(late, 1)], True), "gather_small")[0]
    for items, g_all, name in ((early, early_all, "adam_small_early"), (late, late_all, "adam_small_late")):
        small_out = _adam_reduce(g_all, pack(items, 2), pack(items, 3), pack(items, 4), name, g_all.shape[1])
        off = 0
        for key, _, w_, _, _ in items:
            n_rows = _pad_rows(w_).shape[0]
            upd[key] = [o[off:off + n_rows].reshape(-1)[:w_.size].reshape(w_.shape) for o in small_out]
            off += n_rows

    dmod_rows = late_all[:, :N_MOD * d // BLOCK, :].reshape(N_DEV, N_MOD * d)
    dmod_cols = lax.dynamic_slice(dmod_rows, (0, me * MOD_COLS), (N_DEV, MOD_COLS))
    upd["w_ada"] = [o[None] for o in _adam_outer(act_all.T, dmod_cols, w_ada[0], m_w_ada[0], v_w_ada[0], "adam_w_ada")]

    order = [("w_ada", "w_ada"), ("b_ada", "b_ada"), ("g_ffn1", "g_ffn1"), ("w1_ffn1", "w1a"), ("w3_ffn1", "w3a"),
             ("w2_ffn1", "w2a"), ("g_mix", "g_mix"), ("w_in", "w_in"), ("spatial_w", "spatial_w"),
             ("spatial_b", "spatial_b"), ("g_v", "g_v"), ("g_q", "g_q"), ("g_k", "g_k"), ("sinks", "sinks"),
             ("rel_bias", "rel_bias"), ("w_out", "w_out"), ("g_ffn2", "g_ffn2"), ("w1_ffn2", "w1b"),
             ("w3_ffn2", "w3b"), ("w2_ffn2", "w2b")]
    outs = [loss, dx0[None]]
    for i in range(4):
        outs += [upd[key][i] for _, key in order]
    return tuple(outs)
```

```python
import math

import numpy as np
import jax
import jax.numpy as jnp
from jax import lax
from jax.experimental import pallas as pl
from jax.experimental.pallas import tpu as pltpu

F32 = jnp.float32
BF16 = jnp.bfloat16

D_MODEL = 2048
D_FF = 5632
BLOCK = 128
A_HEADS = 8
A_WIDTH = 1024
B_HEADS = 16
B_KV_HEADS = 2
GROUP = B_HEADS // B_KV_HEADS
HEAD_DIM = 64
B_WIDTH = 1024
KV_WIDTH = 128
IN_COLS = 3328
Q_OFF = 2 * A_WIDTH
K_OFF = Q_OFF + B_WIDTH
V_OFF = K_OFF + KV_WIDTH
N_BUCKETS = 32
N_MOD = 9
EPS = 1e-6
N_DEV = 8
MOD_COLS = N_MOD * D_MODEL // N_DEV

ADAM_LR = 0.001
ADAM_B1 = 0.9
ADAM_B2 = 0.999
ADAM_EPS = 1e-08
ADAM_WD = 0.01
ADAM_STEP = 10

FFN_ROWS = 1024
DW_TOKENS = 4096
MASK_VALUE = -1e30
VMEM_LIMIT = 56 * 1024 * 1024
MESH_ID = pl.DeviceIdType.MESH
ANY = pl.BlockSpec(memory_space=pl.ANY)

_SQRT_HALF = 0.7071067811865476
_INV_SQRT_2PI = 0.3989422804014327


def _pc(body, **kw):
    return pl.pallas_call(body, **kw)


def _params(sem=None):
    return pltpu.CompilerParams(dimension_semantics=sem, vmem_limit_bytes=VMEM_LIMIT)


def _dot(a, b):
    return lax.dot_general(a, b, (((1,), (0,)), ((), ())), preferred_element_type=F32)


def _dot_nt(a, b):
    return lax.dot_general(a, b, (((1,), (1,)), ((), ())), preferred_element_type=F32)


def _dot_tn(a, b):
    return lax.dot_general(a, b, (((0,), (0,)), ((), ())), preferred_element_type=F32)


def _gelu(x):
    return 0.5 * x * (1.0 + lax.erf(x * _SQRT_HALF))


def _gelu_and_grad(x):
    cdf = 0.5 * (1.0 + lax.erf(x * _SQRT_HALF))
    return x * cdf, cdf + x * jnp.exp(-0.5 * x * x) * _INV_SQRT_2PI


def _rms(x):
    r = lax.rsqrt(jnp.mean(x * x, axis=-1, keepdims=True) + EPS)
    return x * r, r


def _rms_bwd(dy, y, r):
    return r * (dy - y * jnp.mean(dy * y, axis=-1, keepdims=True))


def _row_spec(tm, cols):
    return pl.BlockSpec((tm, cols), lambda i: (i, 0))


def _vec_spec(cols):
    return pl.BlockSpec((1, cols), lambda i: (0, 0))


def _position():
    x, y, c = lax.axis_index("x"), lax.axis_index("y"), lax.axis_index("c")
    return x, y, c


def _linear(p):
    return 4 * p[0] + 2 * p[1] + p[2]


class _Comm:
    PEER_COPIES = N_DEV - 1

    def __init__(self, arrs):
        self.arrs = list(arrs)
        n = len(self.arrs)
        self.scratch_shapes = [pltpu.SemaphoreType.DMA((self.PEER_COPIES * n,)),
                               pltpu.SemaphoreType.DMA((self.PEER_COPIES * n,)),
                               pltpu.SemaphoreType.DMA((n,))]

    def bind(self, srcs, dsts, sems):
        self.srcs, self.dsts = srcs, dsts
        self.send_sems, self.recv_sems, self.local_sems = sems
        x, y, c = _position()
        self.me, self.sibling, self.core = (x, y, c), (x, y, 1 - c), c
        self.chips = [(1 - x, y), (x, 1 - y), (1 - x, 1 - y)]
        self.peers = [(1 - x if k & 4 else x, 1 - y if k & 2 else y, 1 - c if k & 1 else c)
                      for k in range(1, N_DEV)]

    def relay(self):
        pass


class _GatherTwoLevel(_Comm):
    def __init__(self, arrs):
        super().__init__(arrs)
        self.out_shape = [jax.ShapeDtypeStruct((N_DEV,) + a.shape, a.dtype) for a in self.arrs]

    def _copy(self, a, k, block, to, from_input=False):
        rows = self.dsts[a].at[_linear(block)]
        return pltpu.make_async_remote_copy(
            src_ref=self.srcs[a] if from_input else rows, dst_ref=rows,
            send_sem=self.send_sems.at[self.PEER_COPIES * a + k], recv_sem=self.recv_sems.at[self.PEER_COPIES * a + k],
            device_id=to, device_id_type=MESH_ID)

    def _local(self, a):
        return pltpu.make_async_copy(self.srcs[a], self.dsts[a].at[_linear(self.me)], self.local_sems.at[a])

    def _first(self, a):
        return [self._copy(a, 0, self.me, self.sibling, True)] + [
            self._copy(a, 1 + j, self.me, (*chip, self.core), True) for j, chip in enumerate(self.chips)]

    def _passed(self, a):
        return [self._copy(a, 4 + j, (*chip, self.core), self.sibling) for j, chip in enumerate(self.chips)]

    def start(self):
        for a in range(len(self.arrs)):
            self._local(a).start()
            for cp in self._first(a):
                cp.start()

    def relay(self):
        for a in range(len(self.arrs)):
            for j, chip in enumerate(self.chips):
                self._copy(a, 1 + j, (*chip, self.core), self.me).wait_recv()
                self._passed(a)[j].start()

    def finish(self):
        for a in range(len(self.arrs)):
            self._copy(a, 0, self.sibling, self.me).wait_recv()
            for j, chip in enumerate(self.chips):
                self._copy(a, 4 + j, (*chip, 1 - self.core), self.me).wait_recv()
        for a in range(len(self.arrs)):
            for cp in self._first(a) + self._passed(a):
                cp.wait_send()
            self._local(a).wait()


class _GatherRelayed(_GatherTwoLevel):
    def _first(self, a):
        return [self._copy(a, 0, self.me, self.sibling, True)] + [
            self._copy(a, 1 + j, self.me, (*self.chips[j], self.core), True) for j in range(2)]

    def _onward(self, a):
        x, y, c = self.me
        north = c == 1
        source = (jnp.where(north, 1 - x, x), jnp.where(north, y, 1 - y), c)
        target = (jnp.where(north, x, 1 - x), jnp.where(north, 1 - y, y), c)
        return self._copy(a, 3, source, target)

    def relay(self):
        for a in range(len(self.arrs)):
            for j in range(2):
                self._copy(a, 1 + j, (*self.chips[j], self.core), self.me).wait_recv()
                self._passed(a)[j].start()
            self._onward(a).start()
        for a in range(len(self.arrs)):
            self._copy(a, 3, (*self.chips[2], self.core), self.me).wait_recv()
            self._passed(a)[2].start()

    def finish(self):
        for a in range(len(self.arrs)):
            self._copy(a, 0, self.sibling, self.me).wait_recv()
            for j, chip in enumerate(self.chips):
                self._copy(a, 4 + j, (*chip, 1 - self.core), self.me).wait_recv()
        for a in range(len(self.arrs)):
            for cp in self._first(a) + [self._onward(a)] + self._passed(a):
                cp.wait_send()
            self._local(a).wait()


class _Direct(_Comm):
    def __init__(self, arrs, broadcast):
        super().__init__(arrs)
        self.broadcast = broadcast
        self.out_shape = [jax.ShapeDtypeStruct(((N_DEV,) + a.shape) if broadcast else a.shape, a.dtype)
                          for a in self.arrs]

    def _outgoing(self, a, to):
        return self.srcs[a] if self.broadcast else self.srcs[a].at[_linear(to)]

    def _copy(self, a, k, sender, to):
        return pltpu.make_async_remote_copy(
            src_ref=self._outgoing(a, to), dst_ref=self.dsts[a].at[_linear(sender)],
            send_sem=self.send_sems.at[self.PEER_COPIES * a + k], recv_sem=self.recv_sems.at[self.PEER_COPIES * a + k],
            device_id=to, device_id_type=MESH_ID)

    def _local(self, a):
        return pltpu.make_async_copy(self._outgoing(a, self.me), self.dsts[a].at[_linear(self.me)],
                                     self.local_sems.at[a])

    def start(self):
        for a in range(len(self.arrs)):
            self._local(a).start()
            for k, peer in enumerate(self.peers):
                self._copy(a, k, self.me, peer).start()

    def finish(self):
        for a in range(len(self.arrs)):
            for k, peer in enumerate(self.peers):
                self._copy(a, k, peer, self.me).wait_recv()
        for a in range(len(self.arrs)):
            for k, peer in enumerate(self.peers):
                self._copy(a, k, self.me, peer).wait_send()
            self._local(a).wait()


def _run_comm(comm, name):
    n = len(comm.arrs)

    def body(*refs):
        comm.bind(refs[:n], refs[n:2 * n], refs[2 * n:])
        comm.start()
        comm.relay()
        comm.finish()

    return _pc(body, name=name, out_shape=comm.out_shape, in_specs=[ANY] * n, out_specs=[ANY] * n,
               scratch_shapes=comm.scratch_shapes)(*comm.arrs)


def _call(body, inputs, *, name, grid, in_specs, out_specs, out_shape, scratch_shapes=(), semantics,
          comm=None, relay_at=0.5):
    if comm is None:
        out = _pc(body, name=name, grid=grid, in_specs=in_specs, out_specs=out_specs, out_shape=out_shape,
                  scratch_shapes=list(scratch_shapes), compiler_params=_params(semantics))(*inputs)
        return list(out), []
    n_in, n_out, n_sc, k = len(in_specs), len(out_specs), len(scratch_shapes), len(comm.arrs)
    steps = math.prod(grid)
    relay_step = min(steps - 1, int(steps * relay_at))

    def carrier(*refs):
        ins, rest = refs[:n_in], refs[n_in:]
        csrc, rest = rest[:k], rest[k:]
        outs, rest = rest[:n_out], rest[n_out:]
        cdst, rest = rest[:k], rest[k:]
        scratch, csems = rest[:n_sc], rest[n_sc:]
        step = 0
        for axis, size in enumerate(grid):
            step = step * size + pl.program_id(axis)
        comm.bind(csrc, cdst, csems)
        pl.when(step == 0)(comm.start)
        pl.when(step == relay_step)(comm.relay)
        body(*ins, *outs, *scratch)
        pl.when(step == steps - 1)(comm.finish)

    out = _pc(carrier, name=name, grid=grid, in_specs=list(in_specs) + [ANY] * k,
              out_specs=list(out_specs) + [ANY] * k, out_shape=list(out_shape) + comm.out_shape,
              scratch_shapes=list(scratch_shapes) + comm.scratch_shapes,
              compiler_params=_params(("arbitrary",) * len(grid)))(*inputs, *comm.arrs)
    return list(out[:n_out]), list(out[n_out:])


def _matmul(a, b, mode, tm, tn, tk, out_dtype, name, comm=None, relay_at=0.5):
    if mode == "nn":
        (m, kk), nn = a.shape, b.shape[1]
        a_spec = pl.BlockSpec((tm, tk), lambda i, k, j: (i, k))
        b_spec = pl.BlockSpec((tk, tn), lambda i, k, j: (k, j))
        dot = _dot
    elif mode == "nt":
        (m, kk), nn = a.shape, b.shape[0]
        a_spec = pl.BlockSpec((tm, tk), lambda i, k, j: (i, k))
        b_spec = pl.BlockSpec((tn, tk), lambda i, k, j: (j, k))
        dot = _dot_nt
    else:
        (kk, m), nn = a.shape, b.shape[1]
        a_spec = pl.BlockSpec((tk, tm), lambda i, k, j: (k, i))
        b_spec = pl.BlockSpec((tk, tn), lambda i, k, j: (k, j))
        dot = _dot_tn
    assert m % tm == 0 and nn % tn == 0 and kk % tk == 0, (a.shape, b.shape, tm, tn, tk)
    nk, nj = kk // tk, nn // tn

    def body(a_ref, b_ref, o_ref, *acc):
        _accumulate_over_k(dot(a_ref[...], b_ref[...]), o_ref, acc, nk)

    out, moved = _call(
        body, (a, b), name=name, grid=(m // tm, nk, nj),
        in_specs=[a_spec, b_spec], out_specs=[_out_tile_spec(tm, tn, nk)],
        out_shape=[jax.ShapeDtypeStruct((m, nn), out_dtype)],
        scratch_shapes=[pltpu.VMEM((nj, tm, tn), F32)] if nk > 1 else [],
        semantics=("parallel", "arbitrary", "arbitrary"), comm=comm, relay_at=relay_at)
    return out[0] if comm is None else (out[0], moved)


def _out_tile_spec(tm, tn, nk):
    return pl.BlockSpec((tm, tn), lambda i, k, j: (i, jnp.where(k == nk - 1, j, 0)))


def _accumulate_over_k(p, o_ref, acc, nk):
    if nk == 1:
        o_ref[...] = p.astype(o_ref.dtype)
        return
    k, j = pl.program_id(1), pl.program_id(2)
    acc_ref = acc[0]

    @pl.when(k == 0)
    def _():
        acc_ref[j] = p

    @pl.when((k > 0) & (k < nk - 1))
    def _():
        acc_ref[j] += p

    @pl.when(k == nk - 1)
    def _():
        o_ref[...] = (acc_ref[j] + p).astype(o_ref.dtype)


def _norm_fwd(x_prev, f_prev, gate_prev, coef, g, sc, sh, name, tm=512, comm=None, relay_at=0.5):
    t, d = x_prev.shape
    residual = f_prev is not None

    def body(*refs):
        if residual:
            xp_ref, fp_ref, gp_ref, g_ref, sc_ref, sh_ref, x_ref, h_ref = refs
            x = xp_ref[...] + coef * gp_ref[...] * fp_ref[...]
            x_ref[...] = x
        else:
            xp_ref, g_ref, sc_ref, sh_ref, h_ref = refs
            x = xp_ref[...]
        y, _ = _rms(x)
        h_ref[...] = ((y * g_ref[...]) * (1.0 + sc_ref[...]) + sh_ref[...]).astype(BF16)

    row, vec = _row_spec(tm, d), _vec_spec(d)
    if residual:
        ins, in_specs = (x_prev, f_prev, gate_prev, g, sc, sh), [row, row, vec, vec, vec, vec]
        out_shape = [jax.ShapeDtypeStruct((t, d), F32), jax.ShapeDtypeStruct((t, d), BF16)]
        out_specs = [row, row]
    else:
        ins, in_specs = (x_prev, g, sc, sh), [row, vec, vec, vec]
        out_shape = [jax.ShapeDtypeStruct((t, d), BF16)]
        out_specs = [row]
    out, moved = _call(body, ins, name=name, grid=(t // tm,), in_specs=in_specs, out_specs=out_specs,
                       out_shape=out_shape, semantics=("parallel",), comm=comm, relay_at=relay_at)
    out = out if residual else out[0]
    return out if comm is None else (out, moved)


def _norm_bwd(dh, x, dxo, fo, coef_o, g, sc, gate_prev, coef_prev, name, tm=256, comm=None):
    t, d = x.shape
    with_prev = gate_prev is not None

    def body(*refs):
        if with_prev:
            dh_ref, x_ref, dxo_ref, fo_ref, g_ref, sc_ref, gp_ref, dx_ref, dfp_ref, sums_ref = refs
        else:
            dh_ref, x_ref, dxo_ref, fo_ref, g_ref, sc_ref, dx_ref, sums_ref = refs
        dh_v, dxo_v = dh_ref[...], dxo_ref[...]
        y, r = _rms(x_ref[...])
        n = y * g_ref[...]
        dn = dh_v * (1.0 + sc_ref[...])
        dx = dxo_v + _rms_bwd(dn * g_ref[...], y, r)
        dx_ref[...] = dx
        if with_prev:
            dfp_ref[...] = (coef_prev * gp_ref[...] * dx).astype(BF16)
        @pl.when(pl.program_id(0) == 0)
        def _():
            sums_ref[...] = jnp.zeros_like(sums_ref)

        sums_ref[0:1, :] += jnp.sum(dh_v * n, axis=0, keepdims=True)
        sums_ref[1:2, :] += jnp.sum(dh_v, axis=0, keepdims=True)
        sums_ref[2:3, :] += jnp.sum(dn * y, axis=0, keepdims=True)
        sums_ref[3:4, :] += jnp.sum(coef_o * dxo_v * fo_ref[...], axis=0, keepdims=True)

    row, vec = _row_spec(tm, d), _vec_spec(d)
    sums_spec = pl.BlockSpec((8, d), lambda i: (0, 0))
    ins, in_specs = [dh, x, dxo, fo, g, sc], [row, row, row, row, vec, vec]
    out_shape, out_specs = [jax.ShapeDtypeStruct((t, d), F32)], [row]
    if with_prev:
        ins.append(gate_prev)
        in_specs.append(vec)
        out_shape.append(jax.ShapeDtypeStruct((t, d), BF16))
        out_specs.append(row)
    out_shape.append(jax.ShapeDtypeStruct((8, d), F32))
    out_specs.append(sums_spec)
    out, moved = _call(body, ins, name=name, grid=(t // tm,), in_specs=in_specs, out_specs=out_specs,
                       out_shape=out_shape, semantics=("arbitrary",), comm=comm)
    return out if comm is None else (out, moved)


def _ffn_up(h, w1t, w3t, name, tf=512, comm=None, relay_at=0.5):
    t, d = h.shape
    ff = w1t.shape[0]
    tm = min(FFN_ROWS, t)

    def body(h_ref, w1_ref, w3_ref, ga_ref, gb_ref, s_ref):
        hv = h_ref[...]
        a = _dot_nt(hv, w1_ref[...])
        b = _dot_nt(hv, w3_ref[...])
        sg = jax.nn.sigmoid(a)
        sil = a * sg
        ga_ref[...] = (b * (sg * (1.0 + a * (1.0 - sg)))).astype(BF16)
        gb_ref[...] = sil.astype(BF16)
        s_ref[...] = (sil * b).astype(BF16)

    tile = pl.BlockSpec((tm, tf), lambda i, j: (i, j))
    cols = pl.BlockSpec((tf, d), lambda i, j: (j, 0))
    out, moved = _call(
        body, (h, w1t, w3t), name=name, grid=(t // tm, ff // tf),
        in_specs=[pl.BlockSpec((tm, d), lambda i, j: (i, 0)), cols, cols], out_specs=[tile, tile, tile],
        out_shape=[jax.ShapeDtypeStruct((t, ff), BF16)] * 3,
        semantics=("parallel", "parallel"), comm=comm, relay_at=relay_at)
    return out if comm is None else (out, moved)


def _ffn_dact(df, ga, gb, w2, name, tf=512, comm=None):
    t, d = df.shape
    ff = ga.shape[1]
    tm = min(2 * FFN_ROWS, t)

    def body(df_ref, ga_ref, gb_ref, w2_ref, da_ref, db_ref):
        ds = _dot_nt(df_ref[...], w2_ref[...])
        da_ref[...] = (ds * ga_ref[...]).astype(BF16)
        db_ref[...] = (ds * gb_ref[...]).astype(BF16)

    tile = pl.BlockSpec((tm, tf), lambda i, j: (i, j))
    out, moved = _call(
        body, (df, ga, gb, w2), name=name, grid=(t // tm, ff // tf),
        in_specs=[pl.BlockSpec((tm, d), lambda i, j: (i, 0)), tile, tile, pl.BlockSpec((tf, d), lambda i, j: (j, 0))],
        out_specs=[tile, tile], out_shape=[jax.ShapeDtypeStruct((t, ff), BF16)] * 2,
        semantics=("parallel", "parallel"), comm=comm)
    return out if comm is None else (out, moved)


def _ffn_dh(da, db, w1t, w3t, name, tn=512, comm=None):
    t, ff = da.shape
    d = w1t.shape[1]
    tm, tk = min(FFN_ROWS, t), ff // 2

    def body(da_ref, db_ref, w1_ref, w3_ref, dh_ref, acc_ref):
        p = _dot(da_ref[...], w1_ref[...]) + _dot(db_ref[...], w3_ref[...])
        _accumulate_over_k(p, dh_ref, (acc_ref,), 2)

    act = pl.BlockSpec((tm, tk), lambda i, k, j: (i, k))
    wgt = pl.BlockSpec((tk, tn), lambda i, k, j: (k, j))
    out, moved = _call(
        body, (da, db, w1t, w3t), name=name, grid=(t // tm, 2, d // tn),
        in_specs=[act, act, wgt, wgt], out_specs=[_out_tile_spec(tm, tn, 2)],
        out_shape=[jax.ShapeDtypeStruct((t, d), F32)],
        scratch_shapes=[pltpu.VMEM((d // tn, tm, tn), F32)],
        semantics=("parallel", "arbitrary", "arbitrary"), comm=comm)
    return out[0] if comm is None else (out[0], moved)


def _loss_head(x_prev, f_prev, gate_prev, coef, target, name, tm=512):
    t, d = x_prev.shape
    steps = t // tm

    def body(xp_ref, fp_ref, gp_ref, tg_ref, dy_ref, df_ref, loss_ref, acc_ref):
        i = pl.program_id(0)
        e = xp_ref[...] + coef * gp_ref[...] * fp_ref[...] - tg_ref[...]
        dy = e * (1.0 / d)
        dy_ref[...] = dy
        df_ref[...] = (coef * gp_ref[...] * dy).astype(BF16)
        part = jnp.sum(e * e, axis=0, keepdims=True)

        @pl.when(i == 0)
        def _():
            acc_ref[...] = part

        @pl.when(i > 0)
        def _():
            acc_ref[...] += part

        @pl.when(i == steps - 1)
        def _():
            loss_ref[...] = jnp.sum(acc_ref[...], axis=1, keepdims=True) * (0.5 / d)

    row, vec = _row_spec(tm, d), _vec_spec(d)
    return _pc(
        body, name=name, grid=(steps,),
        in_specs=[row, row, vec, row], out_specs=[row, row, pl.BlockSpec((1, 1), lambda i: (0, 0))],
        out_shape=[jax.ShapeDtypeStruct((t, d), F32), jax.ShapeDtypeStruct((t, d), BF16),
                   jax.ShapeDtypeStruct((1, 1), F32)],
        scratch_shapes=[pltpu.VMEM((1, d), F32)],
        compiler_params=_params(("arbitrary",)),
    )(x_prev, f_prev, gate_prev, target)


def _split3(x):
    hi = x.astype(BF16)
    r1 = x - hi.astype(F32)
    mid = r1.astype(BF16)
    lo = (r1 - mid.astype(F32)).astype(BF16)
    return hi, mid, lo


def _select_matmul(a, onehot, name):
    m, n = a.shape[0], onehot.shape[1]

    def body(a_ref, oh_ref, o_ref):
        hi, mid, lo = _split3(a_ref[...])
        oh = oh_ref[...]
        o_ref[...] = (_dot(hi, oh) + _dot(mid, oh)) + _dot(lo, oh)

    return _pc(body, name=name, out_shape=jax.ShapeDtypeStruct((m, n), F32), compiler_params=_params())(a, onehot)


def _bucket_onehot():
    qi = np.arange(BLOCK)[:, None]
    kj = np.arange(2 * BLOCK)[None, :]
    dist = np.clip(qi + BLOCK - kj, 0, None)
    nf = np.maximum(dist, 1).astype(np.float32)
    large = 16 + (np.log(nf / np.float32(16)) / np.float32(math.log(128 / 16)) * np.float32(16)).astype(np.int32)
    bucket = np.where(dist < 16, dist, np.minimum(large, N_BUCKETS - 1)).reshape(-1)
    return (bucket[None, :] == np.arange(N_BUCKETS)[:, None]).astype(np.float32)


def _window_mask(n):
    row = lax.broadcasted_iota(jnp.int32, (GROUP * BLOCK, 2 * BLOCK), 0)
    qi = row & (BLOCK - 1)
    kj = lax.broadcasted_iota(jnp.int32, (GROUP * BLOCK, 2 * BLOCK), 1)
    return (kj > qi) & (kj <= qi + BLOCK) & ((kj >= BLOCK) | (n > 0))


def _kv_band(zc_ref, zp_ref, kvh):
    lo, hi = kvh * HEAD_DIM, (kvh + 1) * HEAD_DIM
    k_raw = jnp.concatenate([zp_ref[:, lo:hi], zc_ref[:, K_OFF + lo:K_OFF + hi]], axis=0)
    v_raw = jnp.concatenate([zp_ref[:, KV_WIDTH + lo:KV_WIDTH + hi], zc_ref[:, V_OFF + lo:V_OFF + hi]], axis=0)
    return k_raw, v_raw


def _stack_heads(ref, off):
    return jnp.concatenate([ref[:, off + g * HEAD_DIM:off + (g + 1) * HEAD_DIM] for g in range(GROUP)], axis=0)


def _unstack_heads(ref, off, stacked):
    for g in range(GROUP):
        ref[:, off + g * HEAD_DIM:off + (g + 1) * HEAD_DIM] = stacked[g * BLOCK:(g + 1) * BLOCK]


def _group_softmax(qn, kband, bias_ref, sink_ref, kvh, valid):
    bias = bias_ref[kvh * GROUP:(kvh + 1) * GROUP].reshape(GROUP * BLOCK, 2 * BLOCK)
    s = _dot_nt(qn, kband) * (HEAD_DIM ** -0.5) + bias
    s = jnp.where(valid, s, MASK_VALUE)
    sink = jnp.concatenate([jnp.full((BLOCK, 1), sink_ref[0, kvh * GROUP + g], F32) for g in range(GROUP)], axis=0)
    m = jnp.maximum(jnp.max(s, axis=-1, keepdims=True), sink)
    p = jnp.exp(s - m)
    esink = jnp.exp(sink - m)
    inv = 1.0 / (jnp.sum(p, axis=-1, keepdims=True) + esink)
    return p * inv, esink * inv, inv


def _mix_specs(nb, order):
    cur = pl.BlockSpec((BLOCK, IN_COLS), lambda n: (order(n), 0))
    prev = pl.BlockSpec((BLOCK, 2 * KV_WIDTH), lambda n: (jnp.maximum(order(n) - 1, 0), K_OFF // (2 * KV_WIDTH)))
    full = lambda shape: pl.BlockSpec(shape, lambda n: (0,) * len(shape))
    params = [full((A_HEADS, BLOCK, BLOCK)), full((BLOCK, A_HEADS)), full((A_HEADS, BLOCK)),
              full((1, HEAD_DIM)), full((1, HEAD_DIM)), pl.BlockSpec(memory_space=pltpu.SMEM),
              full((B_HEADS, BLOCK, 2 * BLOCK))]
    return cur, prev, params, full


def _mix_fwd(z, sw, sb_t, gv, gq, gk, sinks, bias, name, comm=None, relay_at=0.5):
    t = z.shape[0]
    nb = t // BLOCK

    def body(zc_ref, zp_ref, sw_ref, sbt_ref, gv_ref, gq_ref, gk_ref, sink_ref, bias_ref, y_ref):
        n = pl.program_id(0)
        ri = lax.broadcasted_iota(jnp.int32, (BLOCK, BLOCK), 0)
        ci = lax.broadcasted_iota(jnp.int32, (BLOCK, BLOCK), 1)
        tril = ri >= ci
        for h in range(A_HEADS):
            u = _gelu(zc_ref[:, h * BLOCK:(h + 1) * BLOCK])
            vv = _gelu(zc_ref[:, A_WIDTH + h * BLOCK:A_WIDTH + (h + 1) * BLOCK])
            vhat, _ = _rms(vv)
            vn = (vhat * gv_ref[h:h + 1, :]).astype(BF16)
            w = jnp.where(tril, sw_ref[h], 0.0).astype(BF16)
            mixed = _dot(w, vn) + sbt_ref[:, h:h + 1]
            y_ref[:, h * BLOCK:(h + 1) * BLOCK] = (u * mixed).astype(BF16)
        valid = _window_mask(n)
        for kvh in range(B_KV_HEADS):
            k_raw, v_raw = _kv_band(zc_ref, zp_ref, kvh)
            khat, _ = _rms(k_raw)
            kband = (khat * gk_ref[...]).astype(BF16)
            vband = v_raw.astype(BF16)
            qhat, _ = _rms(_stack_heads(zc_ref, Q_OFF + kvh * GROUP * HEAD_DIM))
            qn = (qhat * gq_ref[...]).astype(BF16)
            w, _, _ = _group_softmax(qn, kband, bias_ref, sink_ref, kvh, valid)
            o = _dot(w.astype(BF16), vband)
            _unstack_heads(y_ref, A_WIDTH + kvh * GROUP * HEAD_DIM, o.astype(BF16))

    cur, prev, params, _ = _mix_specs(nb, lambda n: n)
    out, moved = _call(
        body, (z, z, sw, sb_t, gv, gq, gk, sinks, bias), name=name, grid=(nb,), in_specs=[cur, prev] + params,
        out_specs=[pl.BlockSpec((BLOCK, A_WIDTH + B_WIDTH), lambda n: (n, 0))],
        out_shape=[jax.ShapeDtypeStruct((t, A_WIDTH + B_WIDTH), BF16)],
        semantics=("parallel",), comm=comm, relay_at=relay_at)
    return out[0] if comm is None else (out[0], moved)


def _mix_bwd(z, dy, sw, sb_t, gv, gq, gk, sinks, bias, name, comm=None):
    t = z.shape[0]
    nb = t // BLOCK

    def body(zc_ref, zp_ref, dy_ref, sw_ref, sbt_ref, gv_ref, gq_ref, gk_ref, sink_ref, bias_ref,
             dz_ref, dsw_ref, dsb_ref, dgv_ref, dgq_ref, dgk_ref, dsink_ref, dbias_ref, carry_ref):
        step = pl.program_id(0)
        n = nb - 1 - step

        @pl.when(step == 0)
        def _():
            dsw_ref[...] = jnp.zeros_like(dsw_ref)
            dsb_ref[...] = jnp.zeros_like(dsb_ref)
            dgv_ref[...] = jnp.zeros_like(dgv_ref)
            dgq_ref[...] = jnp.zeros_like(dgq_ref)
            dgk_ref[...] = jnp.zeros_like(dgk_ref)
            dsink_ref[...] = jnp.zeros_like(dsink_ref)
            dbias_ref[...] = jnp.zeros_like(dbias_ref)
            carry_ref[...] = jnp.zeros_like(carry_ref)

        ri = lax.broadcasted_iota(jnp.int32, (BLOCK, BLOCK), 0)
        ci = lax.broadcasted_iota(jnp.int32, (BLOCK, BLOCK), 1)
        tril = ri >= ci
        dsb = jnp.zeros((BLOCK, BLOCK), F32)
        for h in range(A_HEADS):
            uo, vo = h * BLOCK, A_WIDTH + h * BLOCK
            u_raw, v_raw = zc_ref[:, uo:uo + BLOCK], zc_ref[:, vo:vo + BLOCK]
            (u, du_raw), (vv, dv_raw) = _gelu_and_grad(u_raw), _gelu_and_grad(v_raw)
            vhat, r = _rms(vv)
            vn = (vhat * gv_ref[h:h + 1, :]).astype(BF16)
            w = jnp.where(tril, sw_ref[h], 0.0).astype(BF16)
            mixed = _dot(w, vn) + sbt_ref[:, h:h + 1]
            dya = dy_ref[:, uo:uo + BLOCK]
            dmixed = dya * u
            dm16 = dmixed.astype(BF16)
            dsw_ref[h] += jnp.where(tril, _dot_nt(dm16, vn), 0.0)
            dsb = dsb + jnp.where(ci == h, jnp.sum(dmixed, axis=1, keepdims=True), 0.0)
            dvn = _dot_tn(w, dm16)
            dgv_ref[h:h + 1, :] += jnp.sum(dvn * vhat, axis=0, keepdims=True)
            dvv = _rms_bwd(dvn * gv_ref[h:h + 1, :], vhat, r)
            dz_ref[:, uo:uo + BLOCK] = (dya * mixed * du_raw).astype(BF16)
            dz_ref[:, vo:vo + BLOCK] = (dvv * dv_raw).astype(BF16)
        dsb_ref[...] += dsb

        valid = _window_mask(n)
        lane = lax.broadcasted_iota(jnp.int32, (1, BLOCK), 1)
        dsink = jnp.zeros((1, BLOCK), F32)
        dgq = jnp.zeros((1, HEAD_DIM), F32)
        for kvh in range(B_KV_HEADS):
            ko, vo = K_OFF + kvh * HEAD_DIM, V_OFF + kvh * HEAD_DIM
            k_raw, v_raw = _kv_band(zc_ref, zp_ref, kvh)
            khat, kr = _rms(k_raw)
            kband = (khat * gk_ref[...]).astype(BF16)
            vband = v_raw.astype(BF16)
            qo = Q_OFF + kvh * GROUP * HEAD_DIM
            qhat, qr = _rms(_stack_heads(zc_ref, qo))
            qn = (qhat * gq_ref[...]).astype(BF16)
            w, wsink, _ = _group_softmax(qn, kband, bias_ref, sink_ref, kvh, valid)
            do = _stack_heads(dy_ref, A_WIDTH + kvh * GROUP * HEAD_DIM).astype(BF16)
            dvb = _dot_tn(w.astype(BF16), do)
            dw = _dot_nt(do, vband)
            rowdot = jnp.sum(w * dw, axis=-1, keepdims=True)
            ds = w * (dw - rowdot)
            dsink_rows = -wsink * rowdot
            for g in range(GROUP):
                head_sum = jnp.sum(dsink_rows[g * BLOCK:(g + 1) * BLOCK], axis=0, keepdims=True)
                dsink = dsink + jnp.where(lane == kvh * GROUP + g, head_sum, 0.0)
            dbias_ref[kvh * GROUP:(kvh + 1) * GROUP] += ds.reshape(GROUP, BLOCK, 2 * BLOCK)
            ds16 = (ds * (HEAD_DIM ** -0.5)).astype(BF16)
            dqn = _dot(ds16, kband)
            dkn = _dot_tn(ds16, qn)
            dgq = dgq + jnp.sum(dqn * qhat, axis=0, keepdims=True)
            _unstack_heads(dz_ref, qo, _rms_bwd(dqn * gq_ref[...], qhat, qr).astype(BF16))
            dgk_ref[...] += jnp.sum(dkn * khat, axis=0, keepdims=True)
            dk = _rms_bwd(dkn * gk_ref[...], khat, kr)
            co = kvh * HEAD_DIM
            dz_ref[:, ko:ko + HEAD_DIM] = (dk[BLOCK:] + carry_ref[:, co:co + HEAD_DIM]).astype(BF16)
            dz_ref[:, vo:vo + HEAD_DIM] = (dvb[BLOCK:] + carry_ref[:, KV_WIDTH + co:KV_WIDTH + co + HEAD_DIM]).astype(BF16)
            carry_ref[:, co:co + HEAD_DIM] = dk[:BLOCK]
            carry_ref[:, KV_WIDTH + co:KV_WIDTH + co + HEAD_DIM] = dvb[:BLOCK]
        dgq_ref[...] += dgq
        dsink_ref[...] += dsink

    order = lambda n: nb - 1 - n
    cur, prev, params, full = _mix_specs(nb, order)
    dy_spec = pl.BlockSpec((BLOCK, A_WIDTH + B_WIDTH), lambda n: (order(n), 0))
    out, moved = _call(
        body, (z, z, dy, sw, sb_t, gv, gq, gk, sinks, bias), name=name, grid=(nb,),
        in_specs=[cur, prev, dy_spec] + params,
        out_specs=[cur, full((A_HEADS, BLOCK, BLOCK)), full((BLOCK, BLOCK)), full((A_HEADS, BLOCK)),
                   full((1, HEAD_DIM)), full((1, HEAD_DIM)), full((1, BLOCK)), full((B_HEADS, BLOCK, 2 * BLOCK))],
        out_shape=[jax.ShapeDtypeStruct((t, IN_COLS), BF16), jax.ShapeDtypeStruct((A_HEADS, BLOCK, BLOCK), F32),
                   jax.ShapeDtypeStruct((BLOCK, BLOCK), F32), jax.ShapeDtypeStruct((A_HEADS, BLOCK), F32),
                   jax.ShapeDtypeStruct((1, HEAD_DIM), F32), jax.ShapeDtypeStruct((1, HEAD_DIM), F32),
                   jax.ShapeDtypeStruct((1, BLOCK), F32), jax.ShapeDtypeStruct((B_HEADS, BLOCK, 2 * BLOCK), F32)],
        scratch_shapes=[pltpu.VMEM((BLOCK, 2 * KV_WIDTH), F32)],
        semantics=("arbitrary",), comm=comm)
    return out if comm is None else (out, moved)


def _ada_fwd(c_all, w_ada, b_cols, name, tn=768):
    nb, d = c_all.shape
    cols = w_ada.shape[1]

    def body(c_ref, w_ref, b_ref, act_ref, mod_ref):
        cv = c_ref[...]
        act = cv * jax.nn.sigmoid(cv)
        act_ref[...] = act
        mod_ref[...] = _dot(act.astype(BF16), w_ref[...].astype(BF16)) + b_ref[...]

    return _pc(
        body, name=name, grid=(cols // tn,),
        in_specs=[pl.BlockSpec((nb, d), lambda j: (0, 0)), pl.BlockSpec((d, tn), lambda j: (0, j)),
                  pl.BlockSpec((1, tn), lambda j: (0, j))],
        out_specs=[pl.BlockSpec((nb, d), lambda j: (0, 0)), pl.BlockSpec((nb, tn), lambda j: (0, j))],
        out_shape=[jax.ShapeDtypeStruct((nb, d), F32), jax.ShapeDtypeStruct((nb, cols), F32)],
        compiler_params=_params(("arbitrary",)),
    )(c_all, w_ada, b_cols)


def _adamw(w, g, m, v):
    m = ADAM_B1 * m + (1.0 - ADAM_B1) * g
    v = ADAM_B2 * v + (1.0 - ADAM_B2) * (g * g)
    m_hat = m / (1.0 - ADAM_B1 ** ADAM_STEP)
    v_hat = v / (1.0 - ADAM_B2 ** ADAM_STEP)
    delta = -ADAM_LR * (m_hat / (jnp.sqrt(v_hat) + ADAM_EPS) + ADAM_WD * w)
    return delta, m, v


def _adam_outer(act_t, dmod, w, m, v, name, tr=256):
    rows, cols = w.shape
    nb = dmod.shape[0]

    def body(act_ref, dm_ref, w_ref, m_ref, v_ref, g_ref, d_ref, nm_ref, nv_ref):
        act = act_ref[...].astype(BF16).astype(F32)
        dm = dm_ref[...].astype(BF16).astype(F32)
        g = act[:, 0:1] * dm[0:1, :]
        for b in range(1, nb):
            g = g + act[:, b:b + 1] * dm[b:b + 1, :]
        g_ref[...] = g
        d_ref[...], nm_ref[...], nv_ref[...] = _adamw(w_ref[...], g, m_ref[...], v_ref[...])

    tile = pl.BlockSpec((tr, cols), lambda i: (i, 0))
    return _pc(
        body, name=name, grid=(rows // tr,),
        in_specs=[pl.BlockSpec((tr, nb), lambda i: (i, 0)), pl.BlockSpec((nb, cols), lambda i: (0, 0)), tile, tile, tile],
        out_specs=[tile] * 4, out_shape=[jax.ShapeDtypeStruct((rows, cols), F32)] * 4,
        compiler_params=_params(("parallel",)),
    )(act_t, dmod, w, m, v)


def _adam_reduce(parts, w, m, v, name, tr, transposed=False):
    rows, cols = w.shape

    def body(p_ref, w_ref, m_ref, v_ref, g_ref, d_ref, nm_ref, nv_ref):
        g = p_ref[0].astype(F32)
        for i in range(1, N_DEV):
            g = g + p_ref[i].astype(F32)
        if transposed:
            g = g.T
        g_ref[...] = g
        d_ref[...], nm_ref[...], nv_ref[...] = _adamw(w_ref[...], g, m_ref[...], v_ref[...])

    tile = pl.BlockSpec((tr, cols), lambda i: (i, 0))
    parts_spec = (pl.BlockSpec((N_DEV, cols, tr), lambda i: (0, 0, i)) if transposed
                  else pl.BlockSpec((N_DEV, tr, cols), lambda i: (0, i, 0)))
    return _pc(
        body, name=name, grid=(rows // tr,),
        in_specs=[parts_spec, tile, tile, tile],
        out_specs=[tile] * 4, out_shape=[jax.ShapeDtypeStruct((rows, cols), F32)] * 4,
        compiler_params=_params(("parallel",)),
    )(parts, w, m, v)


def _pad_rows(v):
    flat = v.reshape(-1)
    pad = (-flat.shape[0]) % (8 * BLOCK)
    return jnp.pad(flat, (0, pad)).reshape(-1, BLOCK)


def kernel(x, c, w_ada, b_ada, g_ffn1, w1_ffn1, w3_ffn1, w2_ffn1, g_mix, w_in, spatial_w, spatial_b, g_v, g_q, g_k, sinks, rel_bias, w_out, g_ffn2, w1_ffn2, w3_ffn2, w2_ffn2, loss_target, m_w_ada, m_b_ada, m_g_ffn1, m_w1_ffn1, m_w3_ffn1, m_w2_ffn1, m_g_mix, m_w_in, m_spatial_w, m_spatial_b, m_g_v, m_g_q, m_g_k, m_sinks, m_rel_bias, m_w_out, m_g_ffn2, m_w1_ffn2, m_w3_ffn2, m_w2_ffn2, v_w_ada, v_b_ada, v_g_ffn1, v_w1_ffn1, v_w3_ffn1, v_w2_ffn1, v_g_mix, v_w_in, v_spatial_w, v_spatial_b, v_g_v, v_g_q, v_g_k, v_sinks, v_rel_bias, v_w_out, v_g_ffn2, v_w1_ffn2, v_w3_ffn2, v_w2_ffn2):
    me = _linear(_position())
    d = D_MODEL
    x0 = x[0]
    target = loss_target[0]
    t = x0.shape[0]

    big = dict(w1a=w1_ffn1[0], w3a=w3_ffn1[0], w2a=w2_ffn1[0], w_in=w_in[0], w_out=w_out[0],
               w1b=w1_ffn2[0], w3b=w3_ffn2[0], w2b=w2_ffn2[0])
    col_sharded = ("w1a", "w3a", "w_in", "w1b", "w3b")
    full = {}

    def gather(keys, scheme=_GatherTwoLevel):
        return scheme([(big[k].T if k in col_sharded else big[k]).astype(BF16) for k in keys])

    def unpack(keys, gathered):
        for k, gth in zip(keys, gathered):
            full[k] = gth.reshape(-1, gth.shape[-1])

    def share(k):
        gk_ = grads[k]
        return gk_.reshape(N_DEV, gk_.shape[0] // N_DEV, gk_.shape[1])

    c_all = _run_comm(_Direct([c], True), "gather_c")[0].reshape(N_DEV, d)
    b_cols = lax.dynamic_slice(b_ada, (0, me * MOD_COLS), (1, MOD_COLS))
    act_all, mod_cols = _ada_fwd(c_all, w_ada[0], b_cols, "ada_fwd")
    mod = _run_comm(_Direct([mod_cols.reshape(N_DEV, 1, MOD_COLS)], False), "scatter_mod")[0]
    mod = mod.reshape(N_MOD, 1, d)
    sh1, sc1, gt1, sh2, sc2, gt2, sh3, sc3, gt3 = [mod[i] for i in range(N_MOD)]

    tm = min(FFN_ROWS, t)
    h1, gathered = _norm_fwd(x0, None, None, 0.0, g_ffn1, sc1, sh1, "norm1_fwd",
                             comm=gather(("w1a", "w3a"), _GatherRelayed), relay_at=0.9)
    unpack(("w1a", "w3a"), gathered)
    (ga1, gb1, s1), gathered = _ffn_up(h1, full["w1a"], full["w3a"], "ffn1_up",
                                     comm=gather(("w2a", "w_in", "w_out")), relay_at=0.6)
    unpack(("w2a", "w_in", "w_out"), gathered)
    f1, gathered = _matmul(s1, full["w2a"], "nn", tm, 512, D_FF, F32, "ffn1_down", comm=gather(("w1b",)),
                           relay_at=0.7)
    unpack(("w1b",), gathered)
    x1, h2 = _norm_fwd(x0, f1, gt1, 0.5, g_mix, sc2, sh2, "norm2_fwd")
    z = _matmul(h2, full["w_in"], "nt", tm, IN_COLS // 2, d, F32, "mix_in_fwd")
    onehot = _bucket_onehot()
    bias = _select_matmul(rel_bias.T, jnp.asarray(onehot, BF16), "bias_table").reshape(B_HEADS, BLOCK, 2 * BLOCK)
    sb_t = spatial_b[0].T
    mix_params = (spatial_w[0], sb_t, g_v[0], g_q, g_k, sinks, bias)
    ycat, gathered = _mix_fwd(z, *mix_params, "mix_core_fwd", comm=gather(("w3b",)), relay_at=0.6)
    unpack(("w3b",), gathered)
    y = _matmul(ycat, full["w_out"], "nn", 512, d, d, F32, "mix_out_fwd")
    x2, h3 = _norm_fwd(x1, y, gt2, 1.0, g_ffn2, sc3, sh3, "norm3_fwd")
    (ga2, gb2, s2), gathered = _ffn_up(h3, full["w1b"], full["w3b"], "ffn2_up", comm=gather(("w2b",)), relay_at=0.6)
    unpack(("w2b",), gathered)
    f2 = _matmul(s2, full["w2b"], "nn", tm, 512, D_FF, F32, "ffn2_down")
    dx3, df2, loss_part = _loss_head(x2, f2, gt3, 0.5, target, "loss_head")
    loss = lax.psum(loss_part[0, 0], ("x", "y", "c"))

    grads, parts = {}, {}
    tk = min(DW_TOKENS, t)

    def exchange(k):
        return _Direct([share(k)], False)

    grads["w2b"] = _matmul(s2, df2, "tn", D_FF // 4, 512, tk, BF16, "ffn2_dw2")
    (da, db), moved = _ffn_dact(df2, ga2, gb2, full["w2b"], "ffn2_dact", comm=exchange("w2b"))
    parts["w2b"] = moved[0]
    grads["w1b"] = _matmul(da, h3, "tn", D_FF // 4, 512, tk, BF16, "ffn2_dw1")
    dh3, moved = _ffn_dh(da, db, full["w1b"], full["w3b"], "ffn2_dh", comm=exchange("w1b"))
    parts["w1b"] = moved[0]
    grads["w3b"] = _matmul(db, h3, "tn", D_FF // 4, 512, tk, BF16, "ffn2_dw3")
    dx2, dyg, sums3 = _norm_bwd(dh3, x2, dx3, f2, 0.5, g_ffn2, sc3, gt2, 1.0, "norm3_bwd")
    dycat = _matmul(dyg, full["w_out"], "nt", 512, d, d, F32, "mix_out_bwd")
    grads["w_out"] = _matmul(ycat, dyg, "tn", 1024, 512, tk, BF16, "mix_out_dw")
    mix_grads, moved = _mix_bwd(z, dycat, *mix_params, "mix_core_bwd",
                                comm=_Direct([share("w3b"), share("w_out")], False))
    parts["w3b"], parts["w_out"] = moved
    dz, d_sw, d_sb, d_gv, d_gq, d_gk, d_sink, d_bias = mix_grads
    d_rel = _select_matmul(d_bias.reshape(B_HEADS, BLOCK * 2 * BLOCK), jnp.asarray(onehot.T, BF16), "bias_table_bwd")
    dh2 = _matmul(dz, full["w_in"], "nn", tm, d, IN_COLS // 2, F32, "mix_in_bwd")
    grads["w_in"] = _matmul(dz, h2, "tn", IN_COLS // 2, 512, tk // 2, BF16, "mix_in_dw")
    dx1, df1, sums2 = _norm_bwd(dh2, x1, dx2, y, 1.0, g_mix, sc2, gt1, 0.5, "norm2_bwd")
    grads["w2a"], moved = _matmul(s1, df1, "tn", D_FF // 4, 512, tk, BF16, "ffn1_dw2", comm=exchange("w_in"))
    parts["w_in"] = moved[0]
    (da, db), moved = _ffn_dact(df1, ga1, gb1, full["w2a"], "ffn1_dact", comm=exchange("w2a"))
    parts["w2a"] = moved[0]
    early = [("g_mix", sums2[2:3], g_mix, m_g_mix, v_g_mix),
             ("spatial_w", d_sw[None], spatial_w, m_spatial_w, v_spatial_w),
             ("spatial_b", d_sb[:, :A_HEADS].T[None], spatial_b, m_spatial_b, v_spatial_b),
             ("g_v", d_gv[None], g_v, m_g_v, v_g_v),
             ("g_q", d_gq, g_q, m_g_q, v_g_q),
             ("g_k", d_gk, g_k, m_g_k, v_g_k),
             ("sinks", d_sink[:, :B_HEADS], sinks, m_sinks, v_sinks),
             ("rel_bias", d_rel.T, rel_bias, m_rel_bias, v_rel_bias),
             ("g_ffn2", sums3[2:3], g_ffn2, m_g_ffn2, v_g_ffn2)]

    def pack(items, i):
        return jnp.concatenate([_pad_rows(item[i]) for item in items], axis=0)

    grads["w1a"], moved = _matmul(da, h1, "tn", D_FF // 4, 512, tk, BF16, "ffn1_dw1", comm=_Direct([pack(early, 1)], True))
    early_all = moved[0]
    grads["w3a"], moved = _matmul(db, h1, "tn", D_FF // 4, 512, tk, BF16, "ffn1_dw3", comm=exchange("w1a"))
    parts["w1a"] = moved[0]
    dh1, moved = _ffn_dh(da, db, full["w1a"], full["w3a"], "ffn1_dh", comm=exchange("w3a"))
    parts["w3a"] = moved[0]
    dx0, sums1 = _norm_bwd(dh1, x0, dx1, f1, 0.5, g_ffn1, sc1, None, 0.0, "norm1_bwd")

    moments = dict(w1a=(m_w1_ffn1, v_w1_ffn1), w3a=(m_w3_ffn1, v_w3_ffn1), w2a=(m_w2_ffn1, v_w2_ffn1),
                   w_in=(m_w_in, v_w_in), w_out=(m_w_out, v_w_out),
                   w1b=(m_w1_ffn2, v_w1_ffn2), w3b=(m_w3_ffn2, v_w3_ffn2), w2b=(m_w2_ffn2, v_w2_ffn2))
    upd = {}
    for k in big:
        rows = big[k].shape[0]
        tr = 256 if rows % 256 == 0 else rows // 4
        upd[k] = [o[None] for o in _adam_reduce(parts[k], big[k], moments[k][0][0], moments[k][1][0], "adam_" + k, tr,
                                                transposed=k in col_sharded)]

    dmod = jnp.concatenate([sums1[1:2], sums1[0:1], sums1[3:4], sums2[1:2], sums2[0:1], sums2[3:4],
                            sums3[1:2], sums3[0:1], sums3[3:4]], axis=1)
    late = [("b_ada", dmod, b_ada, m_b_ada, v_b_ada), ("g_ffn1", sums1[2:3], g_ffn1, m_g_ffn1, v_g_ffn1)]
    late_all = _run_comm(_Direct([pack(late, 1)], True), "gather_small")[0]
    for items, g_all, name in ((early, early_all, "adam_small_early"), (late, late_all, "adam_small_late")):
        small_out = _adam_reduce(g_all, pack(items, 2), pack(items, 3), pack(items, 4), name, g_all.shape[1])
        off = 0
        for key, _, w_, _, _ in items:
            n_rows = _pad_rows(w_).shape[0]
            upd[key] = [o[off:off + n_rows].reshape(-1)[:w_.size].reshape(w_.shape) for o in small_out]
            off += n_rows

    dmod_rows = late_all[:, :N_MOD * d // BLOCK, :].reshape(N_DEV, N_MOD * d)
    dmod_cols = lax.dynamic_slice(dmod_rows, (0, me * MOD_COLS), (N_DEV, MOD_COLS))
    upd["w_ada"] = [o[None] for o in _adam_outer(act_all.T, dmod_cols, w_ada[0], m_w_ada[0], v_w_ada[0], "adam_w_ada")]

    order = [("w_ada", "w_ada"), ("b_ada", "b_ada"), ("g_ffn1", "g_ffn1"), ("w1_ffn1", "w1a"), ("w3_ffn1", "w3a"),
             ("w2_ffn1", "w2a"), ("g_mix", "g_mix"), ("w_in", "w_in"), ("spatial_w", "spatial_w"),
             ("spatial_b", "spatial_b"), ("g_v", "g_v"), ("g_q", "g_q"), ("g_k", "g_k"), ("sinks", "sinks"),
             ("rel_bias", "rel_bias"), ("w_out", "w_out"), ("g_ffn2", "g_ffn2"), ("w1_ffn2", "w1b"),
             ("w3_ffn2", "w3b"), ("w2_ffn2", "w2b")]
    outs = [loss, dx0[None]]
    for i in range(4):
        outs += [upd[key][i] for _, key in order]
    return tuple(outs)
```

```python
import math

import numpy as np
import jax
import jax.numpy as jnp
from jax import lax
from jax.experimental import pallas as pl
from jax.experimental.pallas import tpu as pltpu

F32 = jnp.float32
BF16 = jnp.bfloat16

D_MODEL = 2048
D_FF = 5632
BLOCK = 128
A_HEADS = 8
A_WIDTH = 1024
B_HEADS = 16
B_KV_HEADS = 2
GROUP = B_HEADS // B_KV_HEADS
HEAD_DIM = 64
B_WIDTH = 1024
KV_WIDTH = 128
IN_COLS = 3328
Q_OFF = 2 * A_WIDTH
K_OFF = Q_OFF + B_WIDTH
V_OFF = K_OFF + KV_WIDTH
N_BUCKETS = 32
N_MOD = 9
EPS = 1e-6
N_DEV = 8
MOD_COLS = N_MOD * D_MODEL // N_DEV

ADAM_LR = 0.001
ADAM_B1 = 0.9
ADAM_B2 = 0.999
ADAM_EPS = 1e-08
ADAM_WD = 0.01
ADAM_STEP = 10

FFN_ROWS = 1024
DW_TOKENS = 4096
MASK_VALUE = -1e30
VMEM_LIMIT = 56 * 1024 * 1024
MESH_ID = pl.DeviceIdType.MESH
ANY = pl.BlockSpec(memory_space=pl.ANY)

_SQRT_HALF = 0.7071067811865476
_INV_SQRT_2PI = 0.3989422804014327


def _pc(body, **kw):
    return pl.pallas_call(body, **kw)


def _params(sem=None):
    return pltpu.CompilerParams(dimension_semantics=sem, vmem_limit_bytes=VMEM_LIMIT)


def _dot(a, b):
    return lax.dot_general(a, b, (((1,), (0,)), ((), ())), preferred_element_type=F32)


def _dot_nt(a, b):
    return lax.dot_general(a, b, (((1,), (1,)), ((), ())), preferred_element_type=F32)


def _dot_tn(a, b):
    return lax.dot_general(a, b, (((0,), (0,)), ((), ())), preferred_element_type=F32)


def _gelu(x):
    return 0.5 * x * (1.0 + lax.erf(x * _SQRT_HALF))


def _gelu_and_grad(x):
    cdf = 0.5 * (1.0 + lax.erf(x * _SQRT_HALF))
    return x * cdf, cdf + x * jnp.exp(-0.5 * x * x) * _INV_SQRT_2PI


def _rms(x):
    r = lax.rsqrt(jnp.mean(x * x, axis=-1, keepdims=True) + EPS)
    return x * r, r


def _rms_bwd(dy, y, r):
    return r * (dy - y * jnp.mean(dy * y, axis=-1, keepdims=True))


def _row_spec(tm, cols):
    return pl.BlockSpec((tm, cols), lambda i: (i, 0))


def _vec_spec(cols):
    return pl.BlockSpec((1, cols), lambda i: (0, 0))


def _position():
    x, y, c = lax.axis_index("x"), lax.axis_index("y"), lax.axis_index("c")
    return x, y, c


def _linear(p):
    return 4 * p[0] + 2 * p[1] + p[2]


class _Comm:
    PEER_COPIES = N_DEV - 1

    def __init__(self, arrs):
        self.arrs = list(arrs)
        n = len(self.arrs)
        self.scratch_shapes = [pltpu.SemaphoreType.DMA((self.PEER_COPIES * n,)),
                               pltpu.SemaphoreType.DMA((self.PEER_COPIES * n,)),
                               pltpu.SemaphoreType.DMA((n,))]

    def bind(self, srcs, dsts, sems):
        self.srcs, self.dsts = srcs, dsts
        self.send_sems, self.recv_sems, self.local_sems = sems
        x, y, c = _position()
        self.me, self.sibling, self.core = (x, y, c), (x, y, 1 - c), c
        self.chips = [(1 - x, y), (x, 1 - y), (1 - x, 1 - y)]
        self.peers = [(1 - x if k & 4 else x, 1 - y if k & 2 else y, 1 - c if k & 1 else c)
                      for k in range(1, N_DEV)]

    def relay(self):
        pass


class _GatherTwoLevel(_Comm):
    def __init__(self, arrs):
        super().__init__(arrs)
        self.out_shape = [jax.ShapeDtypeStruct((N_DEV,) + a.shape, a.dtype) for a in self.arrs]

    def _copy(self, a, k, block, to, from_input=False):
        rows = self.dsts[a].at[_linear(block)]
        return pltpu.make_async_remote_copy(
            src_ref=self.srcs[a] if from_input else rows, dst_ref=rows,
            send_sem=self.send_sems.at[self.PEER_COPIES * a + k], recv_sem=self.recv_sems.at[self.PEER_COPIES * a + k],
            device_id=to, device_id_type=MESH_ID)

    def _local(self, a):
        return pltpu.make_async_copy(self.srcs[a], self.dsts[a].at[_linear(self.me)], self.local_sems.at[a])

    def _first(self, a):
        return [self._copy(a, 0, self.me, self.sibling, True)] + [
            self._copy(a, 1 + j, self.me, (*chip, self.core), True) for j, chip in enumerate(self.chips)]

    def _passed(self, a):
        return [self._copy(a, 4 + j, (*chip, self.core), self.sibling) for j, chip in enumerate(self.chips)]

    def start(self):
        for a in range(len(self.arrs)):
            self._local(a).start()
            for cp in self._first(a):
                cp.start()

    def relay(self):
        for a in range(len(self.arrs)):
            for j, chip in enumerate(self.chips):
                self._copy(a, 1 + j, (*chip, self.core), self.me).wait_recv()
                self._passed(a)[j].start()

    def finish(self):
        for a in range(len(self.arrs)):
            self._copy(a, 0, self.sibling, self.me).wait_recv()
            for j, chip in enumerate(self.chips):
                self._copy(a, 4 + j, (*chip, 1 - self.core), self.me).wait_recv()
        for a in range(len(self.arrs)):
            for cp in self._first(a) + self._passed(a):
                cp.wait_send()
            self._local(a).wait()


class _GatherRelayed(_GatherTwoLevel):
    def _first(self, a):
        return [self._copy(a, 0, self.me, self.sibling, True)] + [
            self._copy(a, 1 + j, self.me, (*self.chips[j], self.core), True) for j in range(2)]

    def _onward(self, a):
        x, y, c = self.me
        north = c == 1
        source = (jnp.where(north, 1 - x, x), jnp.where(north, y, 1 - y), c)
        target = (jnp.where(north, x, 1 - x), jnp.where(north, 1 - y, y), c)
        return self._copy(a, 3, source, target)

    def relay(self):
        for a in range(len(self.arrs)):
            for j in range(2):
                self._copy(a, 1 + j, (*self.chips[j], self.core), self.me).wait_recv()
                self._passed(a)[j].start()
            self._onward(a).start()
        for a in range(len(self.arrs)):
            self._copy(a, 3, (*self.chips[2], self.core), self.me).wait_recv()
            self._passed(a)[2].start()

    def finish(self):
        for a in range(len(self.arrs)):
            self._copy(a, 0, self.sibling, self.me).wait_recv()
            for j, chip in enumerate(self.chips):
                self._copy(a, 4 + j, (*chip, 1 - self.core), self.me).wait_recv()
        for a in range(len(self.arrs)):
            for cp in self._first(a) + [self._onward(a)] + self._passed(a):
                cp.wait_send()
            self._local(a).wait()


class _Direct(_Comm):
    def __init__(self, arrs, broadcast, rows=None):
        super().__init__(arrs)
        self.broadcast = broadcast
        self.rows = rows or [None] * len(self.arrs)
        self.out_shape = []
        for a, window in zip(self.arrs, self.rows):
            shape = ((N_DEV,) + a.shape) if broadcast else a.shape
            if window is not None:
                shape = (shape[0], window[1]) + shape[2:]
            self.out_shape.append(jax.ShapeDtypeStruct(shape, a.dtype))

    def _outgoing(self, a, to):
        if self.broadcast:
            return self.srcs[a]
        block = self.srcs[a].at[_linear(to)]
        return block if self.rows[a] is None else block.at[pl.ds(*self.rows[a])]

    def _copy(self, a, k, sender, to):
        return pltpu.make_async_remote_copy(
            src_ref=self._outgoing(a, to), dst_ref=self.dsts[a].at[_linear(sender)],
            send_sem=self.send_sems.at[self.PEER_COPIES * a + k], recv_sem=self.recv_sems.at[self.PEER_COPIES * a + k],
            device_id=to, device_id_type=MESH_ID)

    def _local(self, a):
        return pltpu.make_async_copy(self._outgoing(a, self.me), self.dsts[a].at[_linear(self.me)],
                                     self.local_sems.at[a])

    def start(self):
        for a in range(len(self.arrs)):
            self._local(a).start()
            for k, peer in enumerate(self.peers):
                self._copy(a, k, self.me, peer).start()

    def finish(self):
        for a in range(len(self.arrs)):
            for k, peer in enumerate(self.peers):
                self._copy(a, k, peer, self.me).wait_recv()
        for a in range(len(self.arrs)):
            for k, peer in enumerate(self.peers):
                self._copy(a, k, self.me, peer).wait_send()
            self._local(a).wait()


def _run_comm(comm, name):
    n = len(comm.arrs)

    def body(*refs):
        comm.bind(refs[:n], refs[n:2 * n], refs[2 * n:])
        comm.start()
        comm.relay()
        comm.finish()

    return _pc(body, name=name, out_shape=comm.out_shape, in_specs=[ANY] * n, out_specs=[ANY] * n,
               scratch_shapes=comm.scratch_shapes)(*comm.arrs)


def _call(body, inputs, *, name, grid, in_specs, out_specs, out_shape, scratch_shapes=(), semantics,
          comm=None, relay_at=0.5):
    if comm is None:
        out = _pc(body, name=name, grid=grid, in_specs=in_specs, out_specs=out_specs, out_shape=out_shape,
                  scratch_shapes=list(scratch_shapes), compiler_params=_params(semantics))(*inputs)
        return list(out), []
    n_in, n_out, n_sc, k = len(in_specs), len(out_specs), len(scratch_shapes), len(comm.arrs)
    steps = math.prod(grid)
    relay_step = min(steps - 1, int(steps * relay_at))

    def carrier(*refs):
        ins, rest = refs[:n_in], refs[n_in:]
        csrc, rest = rest[:k], rest[k:]
        outs, rest = rest[:n_out], rest[n_out:]
        cdst, rest = rest[:k], rest[k:]
        scratch, csems = rest[:n_sc], rest[n_sc:]
        step = 0
        for axis, size in enumerate(grid):
            step = step * size + pl.program_id(axis)
        comm.bind(csrc, cdst, csems)
        pl.when(step == 0)(comm.start)
        pl.when(step == relay_step)(comm.relay)
        body(*ins, *outs, *scratch)
        pl.when(step == steps - 1)(comm.finish)

    out = _pc(carrier, name=name, grid=grid, in_specs=list(in_specs) + [ANY] * k,
              out_specs=list(out_specs) + [ANY] * k, out_shape=list(out_shape) + comm.out_shape,
              scratch_shapes=list(scratch_shapes) + comm.scratch_shapes,
              compiler_params=_params(("arbitrary",) * len(grid)))(*inputs, *comm.arrs)
    return list(out[:n_out]), list(out[n_out:])


def _matmul(a, b, mode, tm, tn, tk, out_dtype, name, comm=None, relay_at=0.5):
    if mode == "nn":
        (m, kk), nn = a.shape, b.shape[1]
        a_spec = pl.BlockSpec((tm, tk), lambda i, k, j: (i, k))
        b_spec = pl.BlockSpec((tk, tn), lambda i, k, j: (k, j))
        dot = _dot
    elif mode == "nt":
        (m, kk), nn = a.shape, b.shape[0]
        a_spec = pl.BlockSpec((tm, tk), lambda i, k, j: (i, k))
        b_spec = pl.BlockSpec((tn, tk), lambda i, k, j: (j, k))
        dot = _dot_nt
    else:
        (kk, m), nn = a.shape, b.shape[1]
        a_spec = pl.BlockSpec((tk, tm), lambda i, k, j: (k, i))
        b_spec = pl.BlockSpec((tk, tn), lambda i, k, j: (k, j))
        dot = _dot_tn
    assert m % tm == 0 and nn % tn == 0 and kk % tk == 0, (a.shape, b.shape, tm, tn, tk)
    nk, nj = kk // tk, nn // tn

    def body(a_ref, b_ref, o_ref, *acc):
        _accumulate_over_k(dot(a_ref[...], b_ref[...]), o_ref, acc, nk)

    out, moved = _call(
        body, (a, b), name=name, grid=(m // tm, nk, nj),
        in_specs=[a_spec, b_spec], out_specs=[_out_tile_spec(tm, tn, nk)],
        out_shape=[jax.ShapeDtypeStruct((m, nn), out_dtype)],
        scratch_shapes=[pltpu.VMEM((nj, tm, tn), F32)] if nk > 1 else [],
        semantics=("parallel", "arbitrary", "arbitrary"), comm=comm, relay_at=relay_at)
    return out[0] if comm is None else (out[0], moved)


def _out_tile_spec(tm, tn, nk):
    return pl.BlockSpec((tm, tn), lambda i, k, j: (i, jnp.where(k == nk - 1, j, 0)))


def _accumulate_over_k(p, o_ref, acc, nk):
    if nk == 1:
        o_ref[...] = p.astype(o_ref.dtype)
        return
    k, j = pl.program_id(1), pl.program_id(2)
    acc_ref = acc[0]

    @pl.when(k == 0)
    def _():
        acc_ref[j] = p

    @pl.when((k > 0) & (k < nk - 1))
    def _():
        acc_ref[j] += p

    @pl.when(k == nk - 1)
    def _():
        o_ref[...] = (acc_ref[j] + p).astype(o_ref.dtype)


def _norm_fwd(x_prev, f_prev, gate_prev, coef, g, sc, sh, name, tm=512, comm=None, relay_at=0.5):
    t, d = x_prev.shape
    residual = f_prev is not None

    def body(*refs):
        if residual:
            xp_ref, fp_ref, gp_ref, g_ref, sc_ref, sh_ref, x_ref, h_ref = refs
            x = xp_ref[...] + coef * gp_ref[...] * fp_ref[...]
            x_ref[...] = x
        else:
            xp_ref, g_ref, sc_ref, sh_ref, h_ref = refs
            x = xp_ref[...]
        y, _ = _rms(x)
        h_ref[...] = ((y * g_ref[...]) * (1.0 + sc_ref[...]) + sh_ref[...]).astype(BF16)

    row, vec = _row_spec(tm, d), _vec_spec(d)
    if residual:
        ins, in_specs = (x_prev, f_prev, gate_prev, g, sc, sh), [row, row, vec, vec, vec, vec]
        out_shape = [jax.ShapeDtypeStruct((t, d), F32), jax.ShapeDtypeStruct((t, d), BF16)]
        out_specs = [row, row]
    else:
        ins, in_specs = (x_prev, g, sc, sh), [row, vec, vec, vec]
        out_shape = [jax.ShapeDtypeStruct((t, d), BF16)]
        out_specs = [row]
    out, moved = _call(body, ins, name=name, grid=(t // tm,), in_specs=in_specs, out_specs=out_specs,
                       out_shape=out_shape, semantics=("parallel",), comm=comm, relay_at=relay_at)
    out = out if residual else out[0]
    return out if comm is None else (out, moved)


def _norm_bwd(dh, x, dxo, fo, coef_o, g, sc, gate_prev, coef_prev, name, tm=256, comm=None):
    t, d = x.shape
    with_prev = gate_prev is not None

    def body(*refs):
        if with_prev:
            dh_ref, x_ref, dxo_ref, fo_ref, g_ref, sc_ref, gp_ref, dx_ref, dfp_ref, sums_ref = refs
        else:
            dh_ref, x_ref, dxo_ref, fo_ref, g_ref, sc_ref, dx_ref, sums_ref = refs
        dh_v, dxo_v = dh_ref[...], dxo_ref[...]
        y, r = _rms(x_ref[...])
        n = y * g_ref[...]
        dn = dh_v * (1.0 + sc_ref[...])
        dx = dxo_v + _rms_bwd(dn * g_ref[...], y, r)
        dx_ref[...] = dx
        if with_prev:
            dfp_ref[...] = (coef_prev * gp_ref[...] * dx).astype(BF16)
        @pl.when(pl.program_id(0) == 0)
        def _():
            sums_ref[...] = jnp.zeros_like(sums_ref)

        sums_ref[0:1, :] += jnp.sum(dh_v * n, axis=0, keepdims=True)
        sums_ref[1:2, :] += jnp.sum(dh_v, axis=0, keepdims=True)
        sums_ref[2:3, :] += jnp.sum(dn * y, axis=0, keepdims=True)
        sums_ref[3:4, :] += jnp.sum(coef_o * dxo_v * fo_ref[...], axis=0, keepdims=True)

    row, vec = _row_spec(tm, d), _vec_spec(d)
    sums_spec = pl.BlockSpec((8, d), lambda i: (0, 0))
    ins, in_specs = [dh, x, dxo, fo, g, sc], [row, row, row, row, vec, vec]
    out_shape, out_specs = [jax.ShapeDtypeStruct((t, d), F32)], [row]
    if with_prev:
        ins.append(gate_prev)
        in_specs.append(vec)
        out_shape.append(jax.ShapeDtypeStruct((t, d), BF16))
        out_specs.append(row)
    out_shape.append(jax.ShapeDtypeStruct((8, d), F32))
    out_specs.append(sums_spec)
    out, moved = _call(body, ins, name=name, grid=(t // tm,), in_specs=in_specs, out_specs=out_specs,
                       out_shape=out_shape, semantics=("arbitrary",), comm=comm)
    return out if comm is None else (out, moved)


def _ffn_up(h, w1t, w3t, name, tf=512, comm=None, relay_at=0.5):
    t, d = h.shape
    ff = w1t.shape[0]
    tm = min(FFN_ROWS, t)

    def body(h_ref, w1_ref, w3_ref, ga_ref, gb_ref, s_ref):
        hv = h_ref[...]
        a = _dot_nt(hv, w1_ref[...])
        b = _dot_nt(hv, w3_ref[...])
        sg = jax.nn.sigmoid(a)
        sil = a * sg
        ga_ref[...] = (b * (sg * (1.0 + a * (1.0 - sg)))).astype(BF16)
        gb_ref[...] = sil.astype(BF16)
        s_ref[...] = (sil * b).astype(BF16)

    tile = pl.BlockSpec((tm, tf), lambda i, j: (i, j))
    cols = pl.BlockSpec((tf, d), lambda i, j: (j, 0))
    out, moved = _call(
        body, (h, w1t, w3t), name=name, grid=(t // tm, ff // tf),
        in_specs=[pl.BlockSpec((tm, d), lambda i, j: (i, 0)), cols, cols], out_specs=[tile, tile, tile],
        out_shape=[jax.ShapeDtypeStruct((t, ff), BF16)] * 3,
        semantics=("parallel", "parallel"), comm=comm, relay_at=relay_at)
    return out if comm is None else (out, moved)


def _ffn_dact(df, ga, gb, w2, name, tf=512, comm=None):
    t, d = df.shape
    ff = ga.shape[1]
    tm = min(2 * FFN_ROWS, t)

    def body(df_ref, ga_ref, gb_ref, w2_ref, da_ref, db_ref):
        ds = _dot_nt(df_ref[...], w2_ref[...])
        da_ref[...] = (ds * ga_ref[...]).astype(BF16)
        db_ref[...] = (ds * gb_ref[...]).astype(BF16)

    tile = pl.BlockSpec((tm, tf), lambda i, j: (i, j))
    out, moved = _call(
        body, (df, ga, gb, w2), name=name, grid=(t // tm, ff // tf),
        in_specs=[pl.BlockSpec((tm, d), lambda i, j: (i, 0)), tile, tile, pl.BlockSpec((tf, d), lambda i, j: (j, 0))],
        out_specs=[tile, tile], out_shape=[jax.ShapeDtypeStruct((t, ff), BF16)] * 2,
        semantics=("parallel", "parallel"), comm=comm)
    return out if comm is None else (out, moved)


def _ffn_dh(da, db, w1t, w3t, name, tn=512, comm=None):
    t, ff = da.shape
    d = w1t.shape[1]
    tm, tk = min(FFN_ROWS, t), ff // 2

    def body(da_ref, db_ref, w1_ref, w3_ref, dh_ref, acc_ref):
        p = _dot(da_ref[...], w1_ref[...]) + _dot(db_ref[...], w3_ref[...])
        _accumulate_over_k(p, dh_ref, (acc_ref,), 2)

    act = pl.BlockSpec((tm, tk), lambda i, k, j: (i, k))
    wgt = pl.BlockSpec((tk, tn), lambda i, k, j: (k, j))
    out, moved = _call(
        body, (da, db, w1t, w3t), name=name, grid=(t // tm, 2, d // tn),
        in_specs=[act, act, wgt, wgt], out_specs=[_out_tile_spec(tm, tn, 2)],
        out_shape=[jax.ShapeDtypeStruct((t, d), F32)],
        scratch_shapes=[pltpu.VMEM((d // tn, tm, tn), F32)],
        semantics=("parallel", "arbitrary", "arbitrary"), comm=comm)
    return out[0] if comm is None else (out[0], moved)


def _loss_head(x_prev, f_prev, gate_prev, coef, target, name, tm=512):
    t, d = x_prev.shape
    steps = t // tm

    def body(xp_ref, fp_ref, gp_ref, tg_ref, dy_ref, df_ref, loss_ref, acc_ref):
        i = pl.program_id(0)
        e = xp_ref[...] + coef * gp_ref[...] * fp_ref[...] - tg_ref[...]
        dy = e * (1.0 / d)
        dy_ref[...] = dy
        df_ref[...] = (coef * gp_ref[...] * dy).astype(BF16)
        part = jnp.sum(e * e, axis=0, keepdims=True)

        @pl.when(i == 0)
        def _():
            acc_ref[...] = part

        @pl.when(i > 0)
        def _():
            acc_ref[...] += part

        @pl.when(i == steps - 1)
        def _():
            loss_ref[...] = jnp.sum(acc_ref[...], axis=1, keepdims=True) * (0.5 / d)

    row, vec = _row_spec(tm, d), _vec_spec(d)
    return _pc(
        body, name=name, grid=(steps,),
        in_specs=[row, row, vec, row], out_specs=[row, row, pl.BlockSpec((1, 1), lambda i: (0, 0))],
        out_shape=[jax.ShapeDtypeStruct((t, d), F32), jax.ShapeDtypeStruct((t, d), BF16),
                   jax.ShapeDtypeStruct((1, 1), F32)],
        scratch_shapes=[pltpu.VMEM((1, d), F32)],
        compiler_params=_params(("arbitrary",)),
    )(x_prev, f_prev, gate_prev, target)


def _split3(x):
    hi = x.astype(BF16)
    r1 = x - hi.astype(F32)
    mid = r1.astype(BF16)
    lo = (r1 - mid.astype(F32)).astype(BF16)
    return hi, mid, lo


def _select_matmul(a, onehot, name):
    m, n = a.shape[0], onehot.shape[1]

    def body(a_ref, oh_ref, o_ref):
        hi, mid, lo = _split3(a_ref[...])
        oh = oh_ref[...]
        o_ref[...] = (_dot(hi, oh) + _dot(mid, oh)) + _dot(lo, oh)

    return _pc(body, name=name, out_shape=jax.ShapeDtypeStruct((m, n), F32), compiler_params=_params())(a, onehot)


def _bucket_onehot():
    qi = np.arange(BLOCK)[:, None]
    kj = np.arange(2 * BLOCK)[None, :]
    dist = np.clip(qi + BLOCK - kj, 0, None)
    nf = np.maximum(dist, 1).astype(np.float32)
    large = 16 + (np.log(nf / np.float32(16)) / np.float32(math.log(128 / 16)) * np.float32(16)).astype(np.int32)
    bucket = np.where(dist < 16, dist, np.minimum(large, N_BUCKETS - 1)).reshape(-1)
    return (bucket[None, :] == np.arange(N_BUCKETS)[:, None]).astype(np.float32)


def _window_mask(n):
    row = lax.broadcasted_iota(jnp.int32, (GROUP * BLOCK, 2 * BLOCK), 0)
    qi = row & (BLOCK - 1)
    kj = lax.broadcasted_iota(jnp.int32, (GROUP * BLOCK, 2 * BLOCK), 1)
    return (kj > qi) & (kj <= qi + BLOCK) & ((kj >= BLOCK) | (n > 0))


def _kv_band(zc_ref, zp_ref, kvh):
    lo, hi = kvh * HEAD_DIM, (kvh + 1) * HEAD_DIM
    k_raw = jnp.concatenate([zp_ref[:, lo:hi], zc_ref[:, K_OFF + lo:K_OFF + hi]], axis=0)
    v_raw = jnp.concatenate([zp_ref[:, KV_WIDTH + lo:KV_WIDTH + hi], zc_ref[:, V_OFF + lo:V_OFF + hi]], axis=0)
    return k_raw, v_raw


def _stack_heads(ref, off):
    return jnp.concatenate([ref[:, off + g * HEAD_DIM:off + (g + 1) * HEAD_DIM] for g in range(GROUP)], axis=0)


def _unstack_heads(ref, off, stacked):
    for g in range(GROUP):
        ref[:, off + g * HEAD_DIM:off + (g + 1) * HEAD_DIM] = stacked[g * BLOCK:(g + 1) * BLOCK]


def _group_softmax(qn, kband, bias_ref, sink_ref, kvh, valid):
    bias = bias_ref[kvh * GROUP:(kvh + 1) * GROUP].reshape(GROUP * BLOCK, 2 * BLOCK)
    s = _dot_nt(qn, kband) * (HEAD_DIM ** -0.5) + bias
    s = jnp.where(valid, s, MASK_VALUE)
    sink = jnp.concatenate([jnp.full((BLOCK, 1), sink_ref[0, kvh * GROUP + g], F32) for g in range(GROUP)], axis=0)
    m = jnp.maximum(jnp.max(s, axis=-1, keepdims=True), sink)
    p = jnp.exp(s - m)
    esink = jnp.exp(sink - m)
    inv = 1.0 / (jnp.sum(p, axis=-1, keepdims=True) + esink)
    return p * inv, esink * inv, inv


def _mix_specs(nb, order):
    cur = pl.BlockSpec((BLOCK, IN_COLS), lambda n: (order(n), 0))
    prev = pl.BlockSpec((BLOCK, 2 * KV_WIDTH), lambda n: (jnp.maximum(order(n) - 1, 0), K_OFF // (2 * KV_WIDTH)))
    full = lambda shape: pl.BlockSpec(shape, lambda n: (0,) * len(shape))
    params = [full((A_HEADS, BLOCK, BLOCK)), full((BLOCK, A_HEADS)), full((A_HEADS, BLOCK)),
              full((1, HEAD_DIM)), full((1, HEAD_DIM)), pl.BlockSpec(memory_space=pltpu.SMEM),
              full((B_HEADS, BLOCK, 2 * BLOCK))]
    return cur, prev, params, full


def _mix_fwd(z, sw, sb_t, gv, gq, gk, sinks, bias, name, comm=None, relay_at=0.5):
    t = z.shape[0]
    nb = t // BLOCK

    def body(zc_ref, zp_ref, sw_ref, sbt_ref, gv_ref, gq_ref, gk_ref, sink_ref, bias_ref, y_ref):
        n = pl.program_id(0)
        ri = lax.broadcasted_iota(jnp.int32, (BLOCK, BLOCK), 0)
        ci = lax.broadcasted_iota(jnp.int32, (BLOCK, BLOCK), 1)
        tril = ri >= ci
        for h in range(A_HEADS):
            u = _gelu(zc_ref[:, h * BLOCK:(h + 1) * BLOCK])
            vv = _gelu(zc_ref[:, A_WIDTH + h * BLOCK:A_WIDTH + (h + 1) * BLOCK])
            vhat, _ = _rms(vv)
            vn = (vhat * gv_ref[h:h + 1, :]).astype(BF16)
            w = jnp.where(tril, sw_ref[h], 0.0).astype(BF16)
            mixed = _dot(w, vn) + sbt_ref[:, h:h + 1]
            y_ref[:, h * BLOCK:(h + 1) * BLOCK] = (u * mixed).astype(BF16)
        valid = _window_mask(n)
        for kvh in range(B_KV_HEADS):
            k_raw, v_raw = _kv_band(zc_ref, zp_ref, kvh)
            khat, _ = _rms(k_raw)
            kband = (khat * gk_ref[...]).astype(BF16)
            vband = v_raw.astype(BF16)
            qhat, _ = _rms(_stack_heads(zc_ref, Q_OFF + kvh * GROUP * HEAD_DIM))
            qn = (qhat * gq_ref[...]).astype(BF16)
            w, _, _ = _group_softmax(qn, kband, bias_ref, sink_ref, kvh, valid)
            o = _dot(w.astype(BF16), vband)
            _unstack_heads(y_ref, A_WIDTH + kvh * GROUP * HEAD_DIM, o.astype(BF16))

    cur, prev, params, _ = _mix_specs(nb, lambda n: n)
    out, moved = _call(
        body, (z, z, sw, sb_t, gv, gq, gk, sinks, bias), name=name, grid=(nb,), in_specs=[cur, prev] + params,
        out_specs=[pl.BlockSpec((BLOCK, A_WIDTH + B_WIDTH), lambda n: (n, 0))],
        out_shape=[jax.ShapeDtypeStruct((t, A_WIDTH + B_WIDTH), BF16)],
        semantics=("parallel",), comm=comm, relay_at=relay_at)
    return out[0] if comm is None else (out[0], moved)


def _mix_bwd(z, dy, sw, sb_t, gv, gq, gk, sinks, bias, name, comm=None):
    t = z.shape[0]
    nb = t // BLOCK

    def body(zc_ref, zp_ref, dy_ref, sw_ref, sbt_ref, gv_ref, gq_ref, gk_ref, sink_ref, bias_ref,
             dz_ref, dsw_ref, dsb_ref, dgv_ref, dgq_ref, dgk_ref, dsink_ref, dbias_ref, carry_ref):
        step = pl.program_id(0)
        n = nb - 1 - step

        @pl.when(step == 0)
        def _():
            dsw_ref[...] = jnp.zeros_like(dsw_ref)
            dsb_ref[...] = jnp.zeros_like(dsb_ref)
            dgv_ref[...] = jnp.zeros_like(dgv_ref)
            dgq_ref[...] = jnp.zeros_like(dgq_ref)
            dgk_ref[...] = jnp.zeros_like(dgk_ref)
            dsink_ref[...] = jnp.zeros_like(dsink_ref)
            dbias_ref[...] = jnp.zeros_like(dbias_ref)
            carry_ref[...] = jnp.zeros_like(carry_ref)

        ri = lax.broadcasted_iota(jnp.int32, (BLOCK, BLOCK), 0)
        ci = lax.broadcasted_iota(jnp.int32, (BLOCK, BLOCK), 1)
        tril = ri >= ci
        dsb = jnp.zeros((BLOCK, BLOCK), F32)
        for h in range(A_HEADS):
            uo, vo = h * BLOCK, A_WIDTH + h * BLOCK
            u_raw, v_raw = zc_ref[:, uo:uo + BLOCK], zc_ref[:, vo:vo + BLOCK]
            (u, du_raw), (vv, dv_raw) = _gelu_and_grad(u_raw), _gelu_and_grad(v_raw)
            vhat, r = _rms(vv)
            vn = (vhat * gv_ref[h:h + 1, :]).astype(BF16)
            w = jnp.where(tril, sw_ref[h], 0.0).astype(BF16)
            mixed = _dot(w, vn) + sbt_ref[:, h:h + 1]
            dya = dy_ref[:, uo:uo + BLOCK]
            dmixed = dya * u
            dm16 = dmixed.astype(BF16)
            dsw_ref[h] += jnp.where(tril, _dot_nt(dm16, vn), 0.0)
            dsb = dsb + jnp.where(ci == h, jnp.sum(dmixed, axis=1, keepdims=True), 0.0)
            dvn = _dot_tn(w, dm16)
            dgv_ref[h:h + 1, :] += jnp.sum(dvn * vhat, axis=0, keepdims=True)
            dvv = _rms_bwd(dvn * gv_ref[h:h + 1, :], vhat, r)
            dz_ref[:, uo:uo + BLOCK] = (dya * mixed * du_raw).astype(BF16)
            dz_ref[:, vo:vo + BLOCK] = (dvv * dv_raw).astype(BF16)
        dsb_ref[...] += dsb

        valid = _window_mask(n)
        lane = lax.broadcasted_iota(jnp.int32, (1, BLOCK), 1)
        dsink = jnp.zeros((1, BLOCK), F32)
        dgq = jnp.zeros((1, HEAD_DIM), F32)
        for kvh in range(B_KV_HEADS):
            ko, vo = K_OFF + kvh * HEAD_DIM, V_OFF + kvh * HEAD_DIM
            k_raw, v_raw = _kv_band(zc_ref, zp_ref, kvh)
            khat, kr = _rms(k_raw)
            kband = (khat * gk_ref[...]).astype(BF16)
            vband = v_raw.astype(BF16)
            qo = Q_OFF + kvh * GROUP * HEAD_DIM
            qhat, qr = _rms(_stack_heads(zc_ref, qo))
            qn = (qhat * gq_ref[...]).astype(BF16)
            w, wsink, _ = _group_softmax(qn, kband, bias_ref, sink_ref, kvh, valid)
            do = _stack_heads(dy_ref, A_WIDTH + kvh * GROUP * HEAD_DIM).astype(BF16)
            dvb = _dot_tn(w.astype(BF16), do)
            dw = _dot_nt(do, vband)
            rowdot = jnp.sum(w * dw, axis=-1, keepdims=True)
            ds = w * (dw - rowdot)
            dsink_rows = -wsink * rowdot
            for g in range(GROUP):
                head_sum = jnp.sum(dsink_rows[g * BLOCK:(g + 1) * BLOCK], axis=0, keepdims=True)
                dsink = dsink + jnp.where(lane == kvh * GROUP + g, head_sum, 0.0)
            dbias_ref[kvh * GROUP:(kvh + 1) * GROUP] += ds.reshape(GROUP, BLOCK, 2 * BLOCK)
            ds16 = (ds * (HEAD_DIM ** -0.5)).astype(BF16)
            dqn = _dot(ds16, kband)
            dkn = _dot_tn(ds16, qn)
            dgq = dgq + jnp.sum(dqn * qhat, axis=0, keepdims=True)
            _unstack_heads(dz_ref, qo, _rms_bwd(dqn * gq_ref[...], qhat, qr).astype(BF16))
            dgk_ref[...] += jnp.sum(dkn * khat, axis=0, keepdims=True)
            dk = _rms_bwd(dkn * gk_ref[...], khat, kr)
            co = kvh * HEAD_DIM
            dz_ref[:, ko:ko + HEAD_DIM] = (dk[BLOCK:] + carry_ref[:, co:co + HEAD_DIM]).astype(BF16)
            dz_ref[:, vo:vo + HEAD_DIM] = (dvb[BLOCK:] + carry_ref[:, KV_WIDTH + co:KV_WIDTH + co + HEAD_DIM]).astype(BF16)
            carry_ref[:, co:co + HEAD_DIM] = dk[:BLOCK]
            carry_ref[:, KV_WIDTH + co:KV_WIDTH + co + HEAD_DIM] = dvb[:BLOCK]
        dgq_ref[...] += dgq
        dsink_ref[...] += dsink

    order = lambda n: nb - 1 - n
    cur, prev, params, full = _mix_specs(nb, order)
    dy_spec = pl.BlockSpec((BLOCK, A_WIDTH + B_WIDTH), lambda n: (order(n), 0))
    out, moved = _call(
        body, (z, z, dy, sw, sb_t, gv, gq, gk, sinks, bias), name=name, grid=(nb,),
        in_specs=[cur, prev, dy_spec] + params,
        out_specs=[cur, full((A_HEADS, BLOCK, BLOCK)), full((BLOCK, BLOCK)), full((A_HEADS, BLOCK)),
                   full((1, HEAD_DIM)), full((1, HEAD_DIM)), full((1, BLOCK)), full((B_HEADS, BLOCK, 2 * BLOCK))],
        out_shape=[jax.ShapeDtypeStruct((t, IN_COLS), BF16), jax.ShapeDtypeStruct((A_HEADS, BLOCK, BLOCK), F32),
                   jax.ShapeDtypeStruct((BLOCK, BLOCK), F32), jax.ShapeDtypeStruct((A_HEADS, BLOCK), F32),
                   jax.ShapeDtypeStruct((1, HEAD_DIM), F32), jax.ShapeDtypeStruct((1, HEAD_DIM), F32),
                   jax.ShapeDtypeStruct((1, BLOCK), F32), jax.ShapeDtypeStruct((B_HEADS, BLOCK, 2 * BLOCK), F32)],
        scratch_shapes=[pltpu.VMEM((BLOCK, 2 * KV_WIDTH), F32)],
        semantics=("arbitrary",), comm=comm)
    return out if comm is None else (out, moved)


def _ada_fwd(c_all, w_ada, b_cols, name, tn=768):
    nb, d = c_all.shape
    cols = w_ada.shape[1]

    def body(c_ref, w_ref, b_ref, act_ref, mod_ref):
        cv = c_ref[...]
        act = cv * jax.nn.sigmoid(cv)
        act_ref[...] = act
        mod_ref[...] = _dot(act.astype(BF16), w_ref[...].astype(BF16)) + b_ref[...]

    return _pc(
        body, name=name, grid=(cols // tn,),
        in_specs=[pl.BlockSpec((nb, d), lambda j: (0, 0)), pl.BlockSpec((d, tn), lambda j: (0, j)),
                  pl.BlockSpec((1, tn), lambda j: (0, j))],
        out_specs=[pl.BlockSpec((nb, d), lambda j: (0, 0)), pl.BlockSpec((nb, tn), lambda j: (0, j))],
        out_shape=[jax.ShapeDtypeStruct((nb, d), F32), jax.ShapeDtypeStruct((nb, cols), F32)],
        compiler_params=_params(("arbitrary",)),
    )(c_all, w_ada, b_cols)


def _adamw(w, g, m, v):
    m = ADAM_B1 * m + (1.0 - ADAM_B1) * g
    v = ADAM_B2 * v + (1.0 - ADAM_B2) * (g * g)
    m_hat = m / (1.0 - ADAM_B1 ** ADAM_STEP)
    v_hat = v / (1.0 - ADAM_B2 ** ADAM_STEP)
    delta = -ADAM_LR * (m_hat / (jnp.sqrt(v_hat) + ADAM_EPS) + ADAM_WD * w)
    return delta, m, v


def _adam_outer(act_t, dmod, w, m, v, name, tr=256):
    rows, cols = w.shape
    nb = dmod.shape[0]

    def body(act_ref, dm_ref, w_ref, m_ref, v_ref, g_ref, d_ref, nm_ref, nv_ref):
        act = act_ref[...].astype(BF16).astype(F32)
        dm = dm_ref[...].astype(BF16).astype(F32)
        g = act[:, 0:1] * dm[0:1, :]
        for b in range(1, nb):
            g = g + act[:, b:b + 1] * dm[b:b + 1, :]
        g_ref[...] = g
        d_ref[...], nm_ref[...], nv_ref[...] = _adamw(w_ref[...], g, m_ref[...], v_ref[...])

    tile = pl.BlockSpec((tr, cols), lambda i: (i, 0))
    return _pc(
        body, name=name, grid=(rows // tr,),
        in_specs=[pl.BlockSpec((tr, nb), lambda i: (i, 0)), pl.BlockSpec((nb, cols), lambda i: (0, 0)), tile, tile, tile],
        out_specs=[tile] * 4, out_shape=[jax.ShapeDtypeStruct((rows, cols), F32)] * 4,
        compiler_params=_params(("parallel",)),
    )(act_t, dmod, w, m, v)


def _adam_reduce(parts, w, m, v, name, tr, transposed=False):
    rows, cols = w.shape
    pieces = parts if isinstance(parts, (tuple, list)) else (parts,)
    n = len(pieces)

    def total(p_ref):
        g = p_ref[0].astype(F32)
        for i in range(1, N_DEV):
            g = g + p_ref[i].astype(F32)
        return g

    def body(*refs):
        w_ref, m_ref, v_ref, g_ref, d_ref, nm_ref, nv_ref = refs[n:]
        g = total(refs[0]) if n == 1 else jnp.concatenate([total(r) for r in refs[:n]], axis=0)
        if transposed:
            g = g.T
        g_ref[...] = g
        d_ref[...], nm_ref[...], nv_ref[...] = _adamw(w_ref[...], g, m_ref[...], v_ref[...])

    tile = pl.BlockSpec((tr, cols), lambda i: (i, 0))
    if transposed:
        parts_specs = [pl.BlockSpec((N_DEV, p.shape[1], tr), lambda i: (0, 0, i)) for p in pieces]
    else:
        assert n == 1
        parts_specs = [pl.BlockSpec((N_DEV, tr, cols), lambda i: (0, i, 0))]
    return _pc(
        body, name=name, grid=(rows // tr,),
        in_specs=parts_specs + [tile, tile, tile],
        out_specs=[tile] * 4, out_shape=[jax.ShapeDtypeStruct((rows, cols), F32)] * 4,
        compiler_params=_params(("parallel",)),
    )(*pieces, w, m, v)


def _pad_rows(v):
    flat = v.reshape(-1)
    pad = (-flat.shape[0]) % (8 * BLOCK)
    return jnp.pad(flat, (0, pad)).reshape(-1, BLOCK)


def kernel(x, c, w_ada, b_ada, g_ffn1, w1_ffn1, w3_ffn1, w2_ffn1, g_mix, w_in, spatial_w, spatial_b, g_v, g_q, g_k, sinks, rel_bias, w_out, g_ffn2, w1_ffn2, w3_ffn2, w2_ffn2, loss_target, m_w_ada, m_b_ada, m_g_ffn1, m_w1_ffn1, m_w3_ffn1, m_w2_ffn1, m_g_mix, m_w_in, m_spatial_w, m_spatial_b, m_g_v, m_g_q, m_g_k, m_sinks, m_rel_bias, m_w_out, m_g_ffn2, m_w1_ffn2, m_w3_ffn2, m_w2_ffn2, v_w_ada, v_b_ada, v_g_ffn1, v_w1_ffn1, v_w3_ffn1, v_w2_ffn1, v_g_mix, v_w_in, v_spatial_w, v_spatial_b, v_g_v, v_g_q, v_g_k, v_sinks, v_rel_bias, v_w_out, v_g_ffn2, v_w1_ffn2, v_w3_ffn2, v_w2_ffn2):
    me = _linear(_position())
    d = D_MODEL
    x0 = x[0]
    target = loss_target[0]
    t = x0.shape[0]

    big = dict(w1a=w1_ffn1[0], w3a=w3_ffn1[0], w2a=w2_ffn1[0], w_in=w_in[0], w_out=w_out[0],
               w1b=w1_ffn2[0], w3b=w3_ffn2[0], w2b=w2_ffn2[0])
    col_sharded = ("w1a", "w3a", "w_in", "w1b", "w3b")
    full = {}

    def gather(keys, scheme=_GatherTwoLevel):
        return scheme([(big[k].T if k in col_sharded else big[k]).astype(BF16) for k in keys])

    def unpack(keys, gathered):
        for k, gth in zip(keys, gathered):
            full[k] = gth.reshape(-1, gth.shape[-1])

    def share(k):
        gk_ = grads[k]
        return gk_.reshape(N_DEV, gk_.shape[0] // N_DEV, gk_.shape[1])

    c_all = _run_comm(_Direct([c], True), "gather_c")[0].reshape(N_DEV, d)
    b_cols = lax.dynamic_slice(b_ada, (0, me * MOD_COLS), (1, MOD_COLS))
    act_all, mod_cols = _ada_fwd(c_all, w_ada[0], b_cols, "ada_fwd")
    mod = _run_comm(_Direct([mod_cols.reshape(N_DEV, 1, MOD_COLS)], False), "scatter_mod")[0]
    mod = mod.reshape(N_MOD, 1, d)
    sh1, sc1, gt1, sh2, sc2, gt2, sh3, sc3, gt3 = [mod[i] for i in range(N_MOD)]

    tm = min(FFN_ROWS, t)
    h1, gathered = _norm_fwd(x0, None, None, 0.0, g_ffn1, sc1, sh1, "norm1_fwd",
                             comm=gather(("w1a", "w3a"), _GatherRelayed), relay_at=0.9)
    unpack(("w1a", "w3a"), gathered)
    (ga1, gb1, s1), gathered = _ffn_up(h1, full["w1a"], full["w3a"], "ffn1_up",
                                     comm=gather(("w2a", "w_in", "w_out")), relay_at=0.6)
    unpack(("w2a", "w_in", "w_out"), gathered)
    f1, gathered = _matmul(s1, full["w2a"], "nn", tm, 512, D_FF, F32, "ffn1_down", comm=gather(("w1b",)),
                           relay_at=0.7)
    unpack(("w1b",), gathered)
    x1, h2 = _norm_fwd(x0, f1, gt1, 0.5, g_mix, sc2, sh2, "norm2_fwd")
    z = _matmul(h2, full["w_in"], "nt", tm, IN_COLS // 2, d, F32, "mix_in_fwd")
    onehot = _bucket_onehot()
    bias = _select_matmul(rel_bias.T, jnp.asarray(onehot, BF16), "bias_table").reshape(B_HEADS, BLOCK, 2 * BLOCK)
    sb_t = spatial_b[0].T
    mix_params = (spatial_w[0], sb_t, g_v[0], g_q, g_k, sinks, bias)
    ycat, gathered = _mix_fwd(z, *mix_params, "mix_core_fwd", comm=gather(("w3b",)), relay_at=0.6)
    unpack(("w3b",), gathered)
    y = _matmul(ycat, full["w_out"], "nn", 512, d, d, F32, "mix_out_fwd")
    x2, h3 = _norm_fwd(x1, y, gt2, 1.0, g_ffn2, sc3, sh3, "norm3_fwd")
    (ga2, gb2, s2), gathered = _ffn_up(h3, full["w1b"], full["w3b"], "ffn2_up", comm=gather(("w2b",)), relay_at=0.6)
    unpack(("w2b",), gathered)
    f2 = _matmul(s2, full["w2b"], "nn", tm, 512, D_FF, F32, "ffn2_down")
    dx3, df2, loss_part = _loss_head(x2, f2, gt3, 0.5, target, "loss_head")
    loss = lax.psum(loss_part[0, 0], ("x", "y", "c"))

    grads, parts = {}, {}
    tk = min(DW_TOKENS, t)

    def exchange(k):
        return _Direct([share(k)], False)

    grads["w2b"] = _matmul(s2, df2, "tn", D_FF // 4, 512, tk, BF16, "ffn2_dw2")
    (da, db), moved = _ffn_dact(df2, ga2, gb2, full["w2b"], "ffn2_dact", comm=exchange("w2b"))
    parts["w2b"] = moved[0]
    grads["w1b"] = _matmul(da, h3, "tn", D_FF // 4, 512, tk, BF16, "ffn2_dw1")
    dh3, moved = _ffn_dh(da, db, full["w1b"], full["w3b"], "ffn2_dh", comm=exchange("w1b"))
    parts["w1b"] = moved[0]
    grads["w3b"] = _matmul(db, h3, "tn", D_FF // 4, 512, tk, BF16, "ffn2_dw3")
    dx2, dyg, sums3 = _norm_bwd(dh3, x2, dx3, f2, 0.5, g_ffn2, sc3, gt2, 1.0, "norm3_bwd")
    dycat = _matmul(dyg, full["w_out"], "nt", 512, d, d, F32, "mix_out_bwd")
    grads["w_out"] = _matmul(ycat, dyg, "tn", 1024, 512, tk, BF16, "mix_out_dw")
    mix_grads, moved = _mix_bwd(z, dycat, *mix_params, "mix_core_bwd",
                                comm=_Direct([share("w3b"), share("w_out")], False))
    parts["w3b"], parts["w_out"] = moved
    dz, d_sw, d_sb, d_gv, d_gq, d_gk, d_sink, d_bias = mix_grads
    d_rel = _select_matmul(d_bias.reshape(B_HEADS, BLOCK * 2 * BLOCK), jnp.asarray(onehot.T, BF16), "bias_table_bwd")
    dh2 = _matmul(dz, full["w_in"], "nn", tm, d, IN_COLS // 2, F32, "mix_in_bwd")
    grads["w_in"] = _matmul(dz, h2, "tn", IN_COLS // 2, 512, tk // 2, BF16, "mix_in_dw")
    dx1, df1, sums2 = _norm_bwd(dh2, x1, dx2, y, 1.0, g_mix, sc2, gt1, 0.5, "norm2_bwd")
    grads["w2a"], moved = _matmul(s1, df1, "tn", D_FF // 4, 512, tk, BF16, "ffn1_dw2", comm=exchange("w_in"))
    parts["w_in"] = moved[0]
    (da, db), moved = _ffn_dact(df1, ga1, gb1, full["w2a"], "ffn1_dact", comm=exchange("w2a"))
    parts["w2a"] = moved[0]
    early = [("g_mix", sums2[2:3], g_mix, m_g_mix, v_g_mix),
             ("spatial_w", d_sw[None], spatial_w, m_spatial_w, v_spatial_w),
             ("spatial_b", d_sb[:, :A_HEADS].T[None], spatial_b, m_spatial_b, v_spatial_b),
             ("g_v", d_gv[None], g_v, m_g_v, v_g_v),
             ("g_q", d_gq, g_q, m_g_q, v_g_q),
             ("g_k", d_gk, g_k, m_g_k, v_g_k),
             ("sinks", d_sink[:, :B_HEADS], sinks, m_sinks, v_sinks),
             ("rel_bias", d_rel.T, rel_bias, m_rel_bias, v_rel_bias),
             ("g_ffn2", sums3[2:3], g_ffn2, m_g_ffn2, v_g_ffn2)]

    def pack(items, i):
        return jnp.concatenate([_pad_rows(item[i]) for item in items], axis=0)

    grads["w1a"], moved = _matmul(da, h1, "tn", D_FF // 4, 512, tk, BF16, "ffn1_dw1", comm=_Direct([pack(early, 1)], True))
    early_all = moved[0]
    half = share("w1a").shape[1] // 2
    grads["w3a"], moved = _matmul(db, h1, "tn", D_FF // 4, 512, tk, BF16, "ffn1_dw3",
                                  comm=_Direct([share("w1a")], False, rows=[(0, half)]))
    w1a_lo = moved[0]
    dh1, moved = _ffn_dh(da, db, full["w1a"], full["w3a"], "ffn1_dh",
                         comm=_Direct([share("w3a"), share("w1a")], False, rows=[None, (half, half)]))
    parts["w3a"], parts["w1a"] = moved[0], (w1a_lo, moved[1])
    dx0, sums1 = _norm_bwd(dh1, x0, dx1, f1, 0.5, g_ffn1, sc1, None, 0.0, "norm1_bwd")

    moments = dict(w1a=(m_w1_ffn1, v_w1_ffn1), w3a=(m_w3_ffn1, v_w3_ffn1), w2a=(m_w2_ffn1, v_w2_ffn1),
                   w_in=(m_w_in, v_w_in), w_out=(m_w_out, v_w_out),
                   w1b=(m_w1_ffn2, v_w1_ffn2), w3b=(m_w3_ffn2, v_w3_ffn2), w2b=(m_w2_ffn2, v_w2_ffn2))
    upd = {}
    for k in big:
        rows = big[k].shape[0]
        tr = 256 if rows % 256 == 0 else rows // 4
        upd[k] = [o[None] for o in _adam_reduce(parts[k], big[k], moments[k][0][0], moments[k][1][0], "adam_" + k, tr,
                                                transposed=k in col_sharded)]

    dmod = jnp.concatenate([sums1[1:2], sums1[0:1], sums1[3:4], sums2[1:2], sums2[0:1], sums2[3:4],
                            sums3[1:2], sums3[0:1], sums3[3:4]], axis=1)
    late = [("b_ada", dmod, b_ada, m_b_ada, v_b_ada), ("g_ffn1", sums1[2:3], g_ffn1, m_g_ffn1, v_g_ffn1)]
    late_all = _run_comm(_Direct([pack(late, 1)], True), "gather_small")[0]
    for items, g_all, name in ((early, early_all, "adam_small_early"), (late, late_all, "adam_small_late")):
        small_out = _adam_reduce(g_all, pack(items, 2), pack(items, 3), pack(items, 4), name, g_all.shape[1])
        off = 0
        for key, _, w_, _, _ in items:
            n_rows = _pad_rows(w_).shape[0]
            upd[key] = [o[off:off + n_rows].reshape(-1)[:w_.size].reshape(w_.shape) for o in small_out]
            off += n_rows

    dmod_rows = late_all[:, :N_MOD * d // BLOCK, :].reshape(N_DEV, N_MOD * d)
    dmod_cols = lax.dynamic_slice(dmod_rows, (0, me * MOD_COLS), (N_DEV, MOD_COLS))
    upd["w_ada"] = [o[None] for o in _adam_outer(act_all.T, dmod_cols, w_ada[0], m_w_ada[0], v_w_ada[0], "adam_w_ada")]

    order = [("w_ada", "w_ada"), ("b_ada", "b_ada"), ("g_ffn1", "g_ffn1"), ("w1_ffn1", "w1a"), ("w3_ffn1", "w3a"),
             ("w2_ffn1", "w2a"), ("g_mix", "g_mix"), ("w_in", "w_in"), ("spatial_w", "spatial_w"),
             ("spatial_b", "spatial_b"), ("g_v", "g_v"), ("g_q", "g_q"), ("g_k", "g_k"), ("sinks", "sinks"),
             ("rel_bias", "rel_bias"), ("w_out", "w_out"), ("g_ffn2", "g_ffn2"), ("w1_ffn2", "w1b"),
             ("w3_ffn2", "w3b"), ("w2_ffn2", "w2b")]
    outs = [loss, dx0[None]]
    for i in range(4):
        outs += [upd[key][i] for _, key in order]
    return tuple(outs)
```

```python
import math

import numpy as np
import jax
import jax.numpy as jnp
from jax import lax
from jax.experimental import pallas as pl
from jax.experimental.pallas import tpu as pltpu

F32 = jnp.float32
BF16 = jnp.bfloat16

D_MODEL = 2048
D_FF = 5632
BLOCK = 128
A_HEADS = 8
A_WIDTH = 1024
B_HEADS = 16
B_KV_HEADS = 2
GROUP = B_HEADS // B_KV_HEADS
HEAD_DIM = 64
B_WIDTH = 1024
KV_WIDTH = 128
IN_COLS = 3328
Q_OFF = 2 * A_WIDTH
K_OFF = Q_OFF + B_WIDTH
V_OFF = K_OFF + KV_WIDTH
N_BUCKETS = 32
N_MOD = 9
EPS = 1e-6
N_DEV = 8
MOD_COLS = N_MOD * D_MODEL // N_DEV

ADAM_LR = 0.001
ADAM_B1 = 0.9
ADAM_B2 = 0.999
ADAM_EPS = 1e-08
ADAM_WD = 0.01
ADAM_STEP = 10

FFN_ROWS = 1024
DW_TOKENS = 4096
MASK_VALUE = -1e30
VMEM_LIMIT = 56 * 1024 * 1024
MESH_ID = pl.DeviceIdType.MESH
ANY = pl.BlockSpec(memory_space=pl.ANY)

_SQRT_HALF = 0.7071067811865476
_INV_SQRT_2PI = 0.3989422804014327


def _pc(body, **kw):
    return pl.pallas_call(body, **kw)


def _params(sem=None):
    return pltpu.CompilerParams(dimension_semantics=sem, vmem_limit_bytes=VMEM_LIMIT)


def _dot(a, b):
    return lax.dot_general(a, b, (((1,), (0,)), ((), ())), preferred_element_type=F32)


def _dot_nt(a, b):
    return lax.dot_general(a, b, (((1,), (1,)), ((), ())), preferred_element_type=F32)


def _dot_tn(a, b):
    return lax.dot_general(a, b, (((0,), (0,)), ((), ())), preferred_element_type=F32)


def _gelu(x):
    return 0.5 * x * (1.0 + lax.erf(x * _SQRT_HALF))


def _gelu_and_grad(x):
    cdf = 0.5 * (1.0 + lax.erf(x * _SQRT_HALF))
    return x * cdf, cdf + x * jnp.exp(-0.5 * x * x) * _INV_SQRT_2PI


def _rms(x):
    r = lax.rsqrt(jnp.mean(x * x, axis=-1, keepdims=True) + EPS)
    return x * r, r


def _rms_bwd(dy, y, r):
    return r * (dy - y * jnp.mean(dy * y, axis=-1, keepdims=True))


def _row_spec(tm, cols):
    return pl.BlockSpec((tm, cols), lambda i: (i, 0))


def _vec_spec(cols):
    return pl.BlockSpec((1, cols), lambda i: (0, 0))


def _position():
    x, y, c = lax.axis_index("x"), lax.axis_index("y"), lax.axis_index("c")
    return x, y, c


def _linear(p):
    return 4 * p[0] + 2 * p[1] + p[2]


class _Comm:
    PEER_COPIES = N_DEV - 1

    def __init__(self, arrs):
        self.arrs = list(arrs)
        n = len(self.arrs)
        self.scratch_shapes = [pltpu.SemaphoreType.DMA((self.PEER_COPIES * n,)),
                               pltpu.SemaphoreType.DMA((self.PEER_COPIES * n,)),
                               pltpu.SemaphoreType.DMA((n,))]

    def bind(self, srcs, dsts, sems):
        self.srcs, self.dsts = srcs, dsts
        self.send_sems, self.recv_sems, self.local_sems = sems
        x, y, c = _position()
        self.me, self.sibling, self.core = (x, y, c), (x, y, 1 - c), c
        self.chips = [(1 - x, y), (x, 1 - y), (1 - x, 1 - y)]
        self.peers = [(1 - x if k & 4 else x, 1 - y if k & 2 else y, 1 - c if k & 1 else c)
                      for k in range(1, N_DEV)]

    def relay(self):
        pass


class _GatherTwoLevel(_Comm):
    def __init__(self, arrs):
        super().__init__(arrs)
        self.out_shape = [jax.ShapeDtypeStruct((N_DEV,) + a.shape, a.dtype) for a in self.arrs]

    def _copy(self, a, k, block, to, from_input=False):
        rows = self.dsts[a].at[_linear(block)]
        return pltpu.make_async_remote_copy(
            src_ref=self.srcs[a] if from_input else rows, dst_ref=rows,
            send_sem=self.send_sems.at[self.PEER_COPIES * a + k], recv_sem=self.recv_sems.at[self.PEER_COPIES * a + k],
            device_id=to, device_id_type=MESH_ID)

    def _local(self, a):
        return pltpu.make_async_copy(self.srcs[a], self.dsts[a].at[_linear(self.me)], self.local_sems.at[a])

    def _first(self, a):
        return [self._copy(a, 0, self.me, self.sibling, True)] + [
            self._copy(a, 1 + j, self.me, (*chip, self.core), True) for j, chip in enumerate(self.chips)]

    def _passed(self, a):
        return [self._copy(a, 4 + j, (*chip, self.core), self.sibling) for j, chip in enumerate(self.chips)]

    def start(self):
        for a in range(len(self.arrs)):
            self._local(a).start()
            for cp in self._first(a):
                cp.start()

    def relay(self):
        for a in range(len(self.arrs)):
            for j, chip in enumerate(self.chips):
                self._copy(a, 1 + j, (*chip, self.core), self.me).wait_recv()
                self._passed(a)[j].start()

    def finish(self):
        for a in range(len(self.arrs)):
            self._copy(a, 0, self.sibling, self.me).wait_recv()
            for j, chip in enumerate(self.chips):
                self._copy(a, 4 + j, (*chip, 1 - self.core), self.me).wait_recv()
        for a in range(len(self.arrs)):
            for cp in self._first(a) + self._passed(a):
                cp.wait_send()
            self._local(a).wait()


class _GatherRelayed(_GatherTwoLevel):
    def _first(self, a):
        return [self._copy(a, 0, self.me, self.sibling, True)] + [
            self._copy(a, 1 + j, self.me, (*self.chips[j], self.core), True) for j in range(2)]

    def _onward(self, a):
        x, y, c = self.me
        north = c == 1
        source = (jnp.where(north, 1 - x, x), jnp.where(north, y, 1 - y), c)
        target = (jnp.where(north, x, 1 - x), jnp.where(north, 1 - y, y), c)
        return self._copy(a, 3, source, target)

    def relay(self):
        for a in range(len(self.arrs)):
            for j in range(2):
                self._copy(a, 1 + j, (*self.chips[j], self.core), self.me).wait_recv()
                self._passed(a)[j].start()
            self._onward(a).start()
        for a in range(len(self.arrs)):
            self._copy(a, 3, (*self.chips[2], self.core), self.me).wait_recv()
            self._passed(a)[2].start()

    def finish(self):
        for a in range(len(self.arrs)):
            self._copy(a, 0, self.sibling, self.me).wait_recv()
            for j, chip in enumerate(self.chips):
                self._copy(a, 4 + j, (*chip, 1 - self.core), self.me).wait_recv()
        for a in range(len(self.arrs)):
            for cp in self._first(a) + [self._onward(a)] + self._passed(a):
                cp.wait_send()
            self._local(a).wait()


class _Direct(_Comm):
    def __init__(self, arrs, broadcast, rows=None):
        super().__init__(arrs)
        self.broadcast = broadcast
        self.rows = rows or [None] * len(self.arrs)
        self.out_shape = []
        for a, window in zip(self.arrs, self.rows):
            shape = ((N_DEV,) + a.shape) if broadcast else a.shape
            if window is not None:
                shape = (shape[0], window[1]) + shape[2:]
            self.out_shape.append(jax.ShapeDtypeStruct(shape, a.dtype))

    def _outgoing(self, a, to):
        if self.broadcast:
            return self.srcs[a]
        block = self.srcs[a].at[_linear(to)]
        return block if self.rows[a] is None else block.at[pl.ds(*self.rows[a])]

    def _copy(self, a, k, sender, to):
        return pltpu.make_async_remote_copy(
            src_ref=self._outgoing(a, to), dst_ref=self.dsts[a].at[_linear(sender)],
            send_sem=self.send_sems.at[self.PEER_COPIES * a + k], recv_sem=self.recv_sems.at[self.PEER_COPIES * a + k],
            device_id=to, device_id_type=MESH_ID)

    def _local(self, a):
        return pltpu.make_async_copy(self._outgoing(a, self.me), self.dsts[a].at[_linear(self.me)],
                                     self.local_sems.at[a])

    def start(self):
        for a in range(len(self.arrs)):
            self._local(a).start()
            for k, peer in enumerate(self.peers):
                self._copy(a, k, self.me, peer).start()

    def finish(self):
        for a in range(len(self.arrs)):
            for k, peer in enumerate(self.peers):
                self._copy(a, k, peer, self.me).wait_recv()
        for a in range(len(self.arrs)):
            for k, peer in enumerate(self.peers):
                self._copy(a, k, self.me, peer).wait_send()
            self._local(a).wait()


def _run_comm(comm, name):
    n = len(comm.arrs)

    def body(*refs):
        comm.bind(refs[:n], refs[n:2 * n], refs[2 * n:])
        comm.start()
        comm.relay()
        comm.finish()

    return _pc(body, name=name, out_shape=comm.out_shape, in_specs=[ANY] * n, out_specs=[ANY] * n,
               scratch_shapes=comm.scratch_shapes)(*comm.arrs)


def _call(body, inputs, *, name, grid, in_specs, out_specs, out_shape, scratch_shapes=(), semantics,
          comm=None, relay_at=0.5):
    if comm is None:
        out = _pc(body, name=name, grid=grid, in_specs=in_specs, out_specs=out_specs, out_shape=out_shape,
                  scratch_shapes=list(scratch_shapes), compiler_params=_params(semantics))(*inputs)
        return list(out), []
    n_in, n_out, n_sc, k = len(in_specs), len(out_specs), len(scratch_shapes), len(comm.arrs)
    steps = math.prod(grid)
    relay_step = min(steps - 1, int(steps * relay_at))

    def carrier(*refs):
        ins, rest = refs[:n_in], refs[n_in:]
        csrc, rest = rest[:k], rest[k:]
        outs, rest = rest[:n_out], rest[n_out:]
        cdst, rest = rest[:k], rest[k:]
        scratch, csems = rest[:n_sc], rest[n_sc:]
        step = 0
        for axis, size in enumerate(grid):
            step = step * size + pl.program_id(axis)
        comm.bind(csrc, cdst, csems)
        pl.when(step == 0)(comm.start)
        pl.when(step == relay_step)(comm.relay)
        body(*ins, *outs, *scratch)
        pl.when(step == steps - 1)(comm.finish)

    out = _pc(carrier, name=name, grid=grid, in_specs=list(in_specs) + [ANY] * k,
              out_specs=list(out_specs) + [ANY] * k, out_shape=list(out_shape) + comm.out_shape,
              scratch_shapes=list(scratch_shapes) + comm.scratch_shapes,
              compiler_params=_params(("arbitrary",) * len(grid)))(*inputs, *comm.arrs)
    return list(out[:n_out]), list(out[n_out:])


def _matmul(a, b, mode, tm, tn, tk, out_dtype, name, comm=None, relay_at=0.5):
    if mode == "nn":
        (m, kk), nn = a.shape, b.shape[1]
        a_spec = pl.BlockSpec((tm, tk), lambda i, k, j: (i, k))
        b_spec = pl.BlockSpec((tk, tn), lambda i, k, j: (k, j))
        dot = _dot
    elif mode == "nt":
        (m, kk), nn = a.shape, b.shape[0]
        a_spec = pl.BlockSpec((tm, tk), lambda i, k, j: (i, k))
        b_spec = pl.BlockSpec((tn, tk), lambda i, k, j: (j, k))
        dot = _dot_nt
    else:
        (kk, m), nn = a.shape, b.shape[1]
        a_spec = pl.BlockSpec((tk, tm), lambda i, k, j: (k, i))
        b_spec = pl.BlockSpec((tk, tn), lambda i, k, j: (k, j))
        dot = _dot_tn
    assert m % tm == 0 and nn % tn == 0 and kk % tk == 0, (a.shape, b.shape, tm, tn, tk)
    nk, nj = kk // tk, nn // tn

    def body(a_ref, b_ref, o_ref, *acc):
        _accumulate_over_k(dot(a_ref[...], b_ref[...]), o_ref, acc, nk)

    out, moved = _call(
        body, (a, b), name=name, grid=(m // tm, nk, nj),
        in_specs=[a_spec, b_spec], out_specs=[_out_tile_spec(tm, tn, nk)],
        out_shape=[jax.ShapeDtypeStruct((m, nn), out_dtype)],
        scratch_shapes=[pltpu.VMEM((nj, tm, tn), F32)] if nk > 1 else [],
        semantics=("parallel", "arbitrary", "arbitrary"), comm=comm, relay_at=relay_at)
    return out[0] if comm is None else (out[0], moved)


def _out_tile_spec(tm, tn, nk):
    return pl.BlockSpec((tm, tn), lambda i, k, j: (i, jnp.where(k == nk - 1, j, 0)))


def _accumulate_over_k(p, o_ref, acc, nk):
    if nk == 1:
        o_ref[...] = p.astype(o_ref.dtype)
        return
    k, j = pl.program_id(1), pl.program_id(2)
    acc_ref = acc[0]

    @pl.when(k == 0)
    def _():
        acc_ref[j] = p

    @pl.when((k > 0) & (k < nk - 1))
    def _():
        acc_ref[j] += p

    @pl.when(k == nk - 1)
    def _():
        o_ref[...] = (acc_ref[j] + p).astype(o_ref.dtype)


def _norm_fwd(x_prev, f_prev, gate_prev, coef, g, sc, sh, name, tm=512, comm=None, relay_at=0.5):
    t, d = x_prev.shape
    residual = f_prev is not None

    def body(*refs):
        if residual:
            xp_ref, fp_ref, gp_ref, g_ref, sc_ref, sh_ref, x_ref, h_ref = refs
            x = xp_ref[...] + coef * gp_ref[...] * fp_ref[...]
            x_ref[...] = x
        else:
            xp_ref, g_ref, sc_ref, sh_ref, h_ref = refs
            x = xp_ref[...]
        y, _ = _rms(x)
        h_ref[...] = ((y * g_ref[...]) * (1.0 + sc_ref[...]) + sh_ref[...]).astype(BF16)

    row, vec = _row_spec(tm, d), _vec_spec(d)
    if residual:
        ins, in_specs = (x_prev, f_prev, gate_prev, g, sc, sh), [row, row, vec, vec, vec, vec]
        out_shape = [jax.ShapeDtypeStruct((t, d), F32), jax.ShapeDtypeStruct((t, d), BF16)]
        out_specs = [row, row]
    else:
        ins, in_specs = (x_prev, g, sc, sh), [row, vec, vec, vec]
        out_shape = [jax.ShapeDtypeStruct((t, d), BF16)]
        out_specs = [row]
    out, moved = _call(body, ins, name=name, grid=(t // tm,), in_specs=in_specs, out_specs=out_specs,
                       out_shape=out_shape, semantics=("parallel",), comm=comm, relay_at=relay_at)
    out = out if residual else out[0]
    return out if comm is None else (out, moved)


def _norm_bwd(dh, x, dxo, fo, coef_o, g, sc, gate_prev, coef_prev, name, tm=256, comm=None):
    t, d = x.shape
    with_prev = gate_prev is not None

    def body(*refs):
        if with_prev:
            dh_ref, x_ref, dxo_ref, fo_ref, g_ref, sc_ref, gp_ref, dx_ref, dfp_ref, sums_ref = refs
        else:
            dh_ref, x_ref, dxo_ref, fo_ref, g_ref, sc_ref, dx_ref, sums_ref = refs
        dh_v, dxo_v = dh_ref[...], dxo_ref[...]
        y, r = _rms(x_ref[...])
        n = y * g_ref[...]
        dn = dh_v * (1.0 + sc_ref[...])
        dx = dxo_v + _rms_bwd(dn * g_ref[...], y, r)
        dx_ref[...] = dx
        if with_prev:
            dfp_ref[...] = (coef_prev * gp_ref[...] * dx).astype(BF16)
        @pl.when(pl.program_id(0) == 0)
        def _():
            sums_ref[...] = jnp.zeros_like(sums_ref)

        sums_ref[0:1, :] += jnp.sum(dh_v * n, axis=0, keepdims=True)
        sums_ref[1:2, :] += jnp.sum(dh_v, axis=0, keepdims=True)
        sums_ref[2:3, :] += jnp.sum(dn * y, axis=0, keepdims=True)
        sums_ref[3:4, :] += jnp.sum(coef_o * dxo_v * fo_ref[...], axis=0, keepdims=True)

    row, vec = _row_spec(tm, d), _vec_spec(d)
    sums_spec = pl.BlockSpec((8, d), lambda i: (0, 0))
    ins, in_specs = [dh, x, dxo, fo, g, sc], [row, row, row, row, vec, vec]
    out_shape, out_specs = [jax.ShapeDtypeStruct((t, d), F32)], [row]
    if with_prev:
        ins.append(gate_prev)
        in_specs.append(vec)
        out_shape.append(jax.ShapeDtypeStruct((t, d), BF16))
        out_specs.append(row)
    out_shape.append(jax.ShapeDtypeStruct((8, d), F32))
    out_specs.append(sums_spec)
    out, moved = _call(body, ins, name=name, grid=(t // tm,), in_specs=in_specs, out_specs=out_specs,
                       out_shape=out_shape, semantics=("arbitrary",), comm=comm)
    return out if comm is None else (out, moved)


def _ffn_up(h, w1t, w3t, name, tf=512, comm=None, relay_at=0.5, a=None):
    t, d = h.shape
    ff = w3t.shape[0]
    tm = min(FFN_ROWS, t)

    def body(h_ref, first_ref, w3_ref, ga_ref, gb_ref, s_ref):
        hv = h_ref[...]
        a = first_ref[...] if a_given else _dot_nt(hv, first_ref[...])
        b = _dot_nt(hv, w3_ref[...])
        sg = jax.nn.sigmoid(a)
        sil = a * sg
        ga_ref[...] = (b * (sg * (1.0 + a * (1.0 - sg)))).astype(BF16)
        gb_ref[...] = sil.astype(BF16)
        s_ref[...] = (sil * b).astype(BF16)

    a_given = a is not None
    tile = pl.BlockSpec((tm, tf), lambda i, j: (i, j))
    cols = pl.BlockSpec((tf, d), lambda i, j: (j, 0))
    out, moved = _call(
        body, (h, a if a_given else w1t, w3t), name=name, grid=(t // tm, ff // tf),
        in_specs=[pl.BlockSpec((tm, d), lambda i, j: (i, 0)), tile if a_given else cols, cols],
        out_specs=[tile, tile, tile],
        out_shape=[jax.ShapeDtypeStruct((t, ff), BF16)] * 3,
        semantics=("parallel", "parallel"), comm=comm, relay_at=relay_at)
    return out if comm is None else (out, moved)


def _ffn_dact(df, ga, gb, w2, name, tf=512, comm=None):
    t, d = df.shape
    ff = ga.shape[1]
    tm = min(2 * FFN_ROWS, t)

    def body(df_ref, ga_ref, gb_ref, w2_ref, da_ref, db_ref):
        ds = _dot_nt(df_ref[...], w2_ref[...])
        da_ref[...] = (ds * ga_ref[...]).astype(BF16)
        db_ref[...] = (ds * gb_ref[...]).astype(BF16)

    tile = pl.BlockSpec((tm, tf), lambda i, j: (i, j))
    out, moved = _call(
        body, (df, ga, gb, w2), name=name, grid=(t // tm, ff // tf),
        in_specs=[pl.BlockSpec((tm, d), lambda i, j: (i, 0)), tile, tile, pl.BlockSpec((tf, d), lambda i, j: (j, 0))],
        out_specs=[tile, tile], out_shape=[jax.ShapeDtypeStruct((t, ff), BF16)] * 2,
        semantics=("parallel", "parallel"), comm=comm)
    return out if comm is None else (out, moved)


def _ffn_dh(da, db, w1t, w3t, name, tn=512, comm=None):
    t, ff = da.shape
    d = w1t.shape[1]
    tm, tk = min(FFN_ROWS, t), ff // 2

    def body(da_ref, db_ref, w1_ref, w3_ref, dh_ref, acc_ref):
        p = _dot(da_ref[...], w1_ref[...]) + _dot(db_ref[...], w3_ref[...])
        _accumulate_over_k(p, dh_ref, (acc_ref,), 2)

    act = pl.BlockSpec((tm, tk), lambda i, k, j: (i, k))
    wgt = pl.BlockSpec((tk, tn), lambda i, k, j: (k, j))
    out, moved = _call(
        body, (da, db, w1t, w3t), name=name, grid=(t // tm, 2, d // tn),
        in_specs=[act, act, wgt, wgt], out_specs=[_out_tile_spec(tm, tn, 2)],
        out_shape=[jax.ShapeDtypeStruct((t, d), F32)],
        scratch_shapes=[pltpu.VMEM((d // tn, tm, tn), F32)],
        semantics=("parallel", "arbitrary", "arbitrary"), comm=comm)
    return out[0] if comm is None else (out[0], moved)


def _loss_head(x_prev, f_prev, gate_prev, coef, target, name, tm=512):
    t, d = x_prev.shape
    steps = t // tm

    def body(xp_ref, fp_ref, gp_ref, tg_ref, dy_ref, df_ref, loss_ref, acc_ref):
        i = pl.program_id(0)
        e = xp_ref[...] + coef * gp_ref[...] * fp_ref[...] - tg_ref[...]
        dy = e * (1.0 / d)
        dy_ref[...] = dy
        df_ref[...] = (coef * gp_ref[...] * dy).astype(BF16)
        part = jnp.sum(e * e, axis=0, keepdims=True)

        @pl.when(i == 0)
        def _():
            acc_ref[...] = part

        @pl.when(i > 0)
        def _():
            acc_ref[...] += part

        @pl.when(i == steps - 1)
        def _():
            loss_ref[...] = jnp.sum(acc_ref[...], axis=1, keepdims=True) * (0.5 / d)

    row, vec = _row_spec(tm, d), _vec_spec(d)
    return _pc(
        body, name=name, grid=(steps,),
        in_specs=[row, row, vec, row], out_specs=[row, row, pl.BlockSpec((1, 1), lambda i: (0, 0))],
        out_shape=[jax.ShapeDtypeStruct((t, d), F32), jax.ShapeDtypeStruct((t, d), BF16),
                   jax.ShapeDtypeStruct((1, 1), F32)],
        scratch_shapes=[pltpu.VMEM((1, d), F32)],
        compiler_params=_params(("arbitrary",)),
    )(x_prev, f_prev, gate_prev, target)


def _split3(x):
    hi = x.astype(BF16)
    r1 = x - hi.astype(F32)
    mid = r1.astype(BF16)
    lo = (r1 - mid.astype(F32)).astype(BF16)
    return hi, mid, lo


def _select_matmul(a, onehot, name):
    m, n = a.shape[0], onehot.shape[1]

    def body(a_ref, oh_ref, o_ref):
        hi, mid, lo = _split3(a_ref[...])
        oh = oh_ref[...]
        o_ref[...] = (_dot(hi, oh) + _dot(mid, oh)) + _dot(lo, oh)

    return _pc(body, name=name, out_shape=jax.ShapeDtypeStruct((m, n), F32), compiler_params=_params())(a, onehot)


def _bucket_onehot():
    qi = np.arange(BLOCK)[:, None]
    kj = np.arange(2 * BLOCK)[None, :]
    dist = np.clip(qi + BLOCK - kj, 0, None)
    nf = np.maximum(dist, 1).astype(np.float32)
    large = 16 + (np.log(nf / np.float32(16)) / np.float32(math.log(128 / 16)) * np.float32(16)).astype(np.int32)
    bucket = np.where(dist < 16, dist, np.minimum(large, N_BUCKETS - 1)).reshape(-1)
    return (bucket[None, :] == np.arange(N_BUCKETS)[:, None]).astype(np.float32)


def _window_mask(n):
    row = lax.broadcasted_iota(jnp.int32, (GROUP * BLOCK, 2 * BLOCK), 0)
    qi = row & (BLOCK - 1)
    kj = lax.broadcasted_iota(jnp.int32, (GROUP * BLOCK, 2 * BLOCK), 1)
    return (kj > qi) & (kj <= qi + BLOCK) & ((kj >= BLOCK) | (n > 0))


def _kv_band(zc_ref, zp_ref, kvh):
    lo, hi = kvh * HEAD_DIM, (kvh + 1) * HEAD_DIM
    k_raw = jnp.concatenate([zp_ref[:, lo:hi], zc_ref[:, K_OFF + lo:K_OFF + hi]], axis=0)
    v_raw = jnp.concatenate([zp_ref[:, KV_WIDTH + lo:KV_WIDTH + hi], zc_ref[:, V_OFF + lo:V_OFF + hi]], axis=0)
    return k_raw, v_raw


def _stack_heads(ref, off):
    return jnp.concatenate([ref[:, off + g * HEAD_DIM:off + (g + 1) * HEAD_DIM] for g in range(GROUP)], axis=0)


def _unstack_heads(ref, off, stacked):
    for g in range(GROUP):
        ref[:, off + g * HEAD_DIM:off + (g + 1) * HEAD_DIM] = stacked[g * BLOCK:(g + 1) * BLOCK]


def _group_softmax(qn, kband, bias_ref, sink_ref, kvh, valid):
    bias = bias_ref[kvh * GROUP:(kvh + 1) * GROUP].reshape(GROUP * BLOCK, 2 * BLOCK)
    s = _dot_nt(qn, kband) * (HEAD_DIM ** -0.5) + bias
    s = jnp.where(valid, s, MASK_VALUE)
    sink = jnp.concatenate([jnp.full((BLOCK, 1), sink_ref[0, kvh * GROUP + g], F32) for g in range(GROUP)], axis=0)
    m = jnp.maximum(jnp.max(s, axis=-1, keepdims=True), sink)
    p = jnp.exp(s - m)
    esink = jnp.exp(sink - m)
    inv = 1.0 / (jnp.sum(p, axis=-1, keepdims=True) + esink)
    return p * inv, esink * inv, inv


def _mix_specs(nb, order):
    cur = pl.BlockSpec((BLOCK, IN_COLS), lambda n: (order(n), 0))
    prev = pl.BlockSpec((BLOCK, 2 * KV_WIDTH), lambda n: (jnp.maximum(order(n) - 1, 0), K_OFF // (2 * KV_WIDTH)))
    full = lambda shape: pl.BlockSpec(shape, lambda n: (0,) * len(shape))
    params = [full((A_HEADS, BLOCK, BLOCK)), full((BLOCK, A_HEADS)), full((A_HEADS, BLOCK)),
              full((1, HEAD_DIM)), full((1, HEAD_DIM)), pl.BlockSpec(memory_space=pltpu.SMEM),
              full((B_HEADS, BLOCK, 2 * BLOCK))]
    return cur, prev, params, full


def _mix_fwd(z, sw, sb_t, gv, gq, gk, sinks, bias, name, comm=None, relay_at=0.5):
    t = z.shape[0]
    nb = t // BLOCK

    def body(zc_ref, zp_ref, sw_ref, sbt_ref, gv_ref, gq_ref, gk_ref, sink_ref, bias_ref, y_ref):
        n = pl.program_id(0)
        ri = lax.broadcasted_iota(jnp.int32, (BLOCK, BLOCK), 0)
        ci = lax.broadcasted_iota(jnp.int32, (BLOCK, BLOCK), 1)
        tril = ri >= ci
        for h in range(A_HEADS):
            u = _gelu(zc_ref[:, h * BLOCK:(h + 1) * BLOCK])
            vv = _gelu(zc_ref[:, A_WIDTH + h * BLOCK:A_WIDTH + (h + 1) * BLOCK])
            vhat, _ = _rms(vv)
            vn = (vhat * gv_ref[h:h + 1, :]).astype(BF16)
            w = jnp.where(tril, sw_ref[h], 0.0).astype(BF16)
            mixed = _dot(w, vn) + sbt_ref[:, h:h + 1]
            y_ref[:, h * BLOCK:(h + 1) * BLOCK] = (u * mixed).astype(BF16)
        valid = _window_mask(n)
        for kvh in range(B_KV_HEADS):
            k_raw, v_raw = _kv_band(zc_ref, zp_ref, kvh)
            khat, _ = _rms(k_raw)
            kband = (khat * gk_ref[...]).astype(BF16)
            vband = v_raw.astype(BF16)
            qhat, _ = _rms(_stack_heads(zc_ref, Q_OFF + kvh * GROUP * HEAD_DIM))
            qn = (qhat * gq_ref[...]).astype(BF16)
            w, _, _ = _group_softmax(qn, kband, bias_ref, sink_ref, kvh, valid)
            o = _dot(w.astype(BF16), vband)
            _unstack_heads(y_ref, A_WIDTH + kvh * GROUP * HEAD_DIM, o.astype(BF16))

    cur, prev, params, _ = _mix_specs(nb, lambda n: n)
    out, moved = _call(
        body, (z, z, sw, sb_t, gv, gq, gk, sinks, bias), name=name, grid=(nb,), in_specs=[cur, prev] + params,
        out_specs=[pl.BlockSpec((BLOCK, A_WIDTH + B_WIDTH), lambda n: (n, 0))],
        out_shape=[jax.ShapeDtypeStruct((t, A_WIDTH + B_WIDTH), BF16)],
        semantics=("parallel",), comm=comm, relay_at=relay_at)
    return out[0] if comm is None else (out[0], moved)


def _mix_bwd(z, dy, sw, sb_t, gv, gq, gk, sinks, bias, name, comm=None):
    t = z.shape[0]
    nb = t // BLOCK

    def body(zc_ref, zp_ref, dy_ref, sw_ref, sbt_ref, gv_ref, gq_ref, gk_ref, sink_ref, bias_ref,
             dz_ref, dsw_ref, dsb_ref, dgv_ref, dgq_ref, dgk_ref, dsink_ref, dbias_ref, carry_ref):
        step = pl.program_id(0)
        n = nb - 1 - step

        @pl.when(step == 0)
        def _():
            dsw_ref[...] = jnp.zeros_like(dsw_ref)
            dsb_ref[...] = jnp.zeros_like(dsb_ref)
            dgv_ref[...] = jnp.zeros_like(dgv_ref)
            dgq_ref[...] = jnp.zeros_like(dgq_ref)
            dgk_ref[...] = jnp.zeros_like(dgk_ref)
            dsink_ref[...] = jnp.zeros_like(dsink_ref)
            dbias_ref[...] = jnp.zeros_like(dbias_ref)
            carry_ref[...] = jnp.zeros_like(carry_ref)

        ri = lax.broadcasted_iota(jnp.int32, (BLOCK, BLOCK), 0)
        ci = lax.broadcasted_iota(jnp.int32, (BLOCK, BLOCK), 1)
        tril = ri >= ci
        dsb = jnp.zeros((BLOCK, BLOCK), F32)
        for h in range(A_HEADS):
            uo, vo = h * BLOCK, A_WIDTH + h * BLOCK
            u_raw, v_raw = zc_ref[:, uo:uo + BLOCK], zc_ref[:, vo:vo + BLOCK]
            (u, du_raw), (vv, dv_raw) = _gelu_and_grad(u_raw), _gelu_and_grad(v_raw)
            vhat, r = _rms(vv)
            vn = (vhat * gv_ref[h:h + 1, :]).astype(BF16)
            w = jnp.where(tril, sw_ref[h], 0.0).astype(BF16)
            mixed = _dot(w, vn) + sbt_ref[:, h:h + 1]
            dya = dy_ref[:, uo:uo + BLOCK]
            dmixed = dya * u
            dm16 = dmixed.astype(BF16)
            dsw_ref[h] += jnp.where(tril, _dot_nt(dm16, vn), 0.0)
            dsb = dsb + jnp.where(ci == h, jnp.sum(dmixed, axis=1, keepdims=True), 0.0)
            dvn = _dot_tn(w, dm16)
            dgv_ref[h:h + 1, :] += jnp.sum(dvn * vhat, axis=0, keepdims=True)
            dvv = _rms_bwd(dvn * gv_ref[h:h + 1, :], vhat, r)
            dz_ref[:, uo:uo + BLOCK] = (dya * mixed * du_raw).astype(BF16)
            dz_ref[:, vo:vo + BLOCK] = (dvv * dv_raw).astype(BF16)
        dsb_ref[...] += dsb

        valid = _window_mask(n)
        lane = lax.broadcasted_iota(jnp.int32, (1, BLOCK), 1)
        dsink = jnp.zeros((1, BLOCK), F32)
        dgq = jnp.zeros((1, HEAD_DIM), F32)
        for kvh in range(B_KV_HEADS):
            ko, vo = K_OFF + kvh * HEAD_DIM, V_OFF + kvh * HEAD_DIM
            k_raw, v_raw = _kv_band(zc_ref, zp_ref, kvh)
            khat, kr = _rms(k_raw)
            kband = (khat * gk_ref[...]).astype(BF16)
            vband = v_raw.astype(BF16)
            qo = Q_OFF + kvh * GROUP * HEAD_DIM
            qhat, qr = _rms(_stack_heads(zc_ref, qo))
            qn = (qhat * gq_ref[...]).astype(BF16)
            w, wsink, _ = _group_softmax(qn, kband, bias_ref, sink_ref, kvh, valid)
            do = _stack_heads(dy_ref, A_WIDTH + kvh * GROUP * HEAD_DIM).astype(BF16)
            dvb = _dot_tn(w.astype(BF16), do)
            dw = _dot_nt(do, vband)
            rowdot = jnp.sum(w * dw, axis=-1, keepdims=True)
            ds = w * (dw - rowdot)
            dsink_rows = -wsink * rowdot
            for g in range(GROUP):
                head_sum = jnp.sum(dsink_rows[g * BLOCK:(g + 1) * BLOCK], axis=0, keepdims=True)
                dsink = dsink + jnp.where(lane == kvh * GROUP + g, head_sum, 0.0)
            dbias_ref[kvh * GROUP:(kvh + 1) * GROUP] += ds.reshape(GROUP, BLOCK, 2 * BLOCK)
            ds16 = (ds * (HEAD_DIM ** -0.5)).astype(BF16)
            dqn = _dot(ds16, kband)
            dkn = _dot_tn(ds16, qn)
            dgq = dgq + jnp.sum(dqn * qhat, axis=0, keepdims=True)
            _unstack_heads(dz_ref, qo, _rms_bwd(dqn * gq_ref[...], qhat, qr).astype(BF16))
            dgk_ref[...] += jnp.sum(dkn * khat, axis=0, keepdims=True)
            dk = _rms_bwd(dkn * gk_ref[...], khat, kr)
            co = kvh * HEAD_DIM
            dz_ref[:, ko:ko + HEAD_DIM] = (dk[BLOCK:] + carry_ref[:, co:co + HEAD_DIM]).astype(BF16)
            dz_ref[:, vo:vo + HEAD_DIM] = (dvb[BLOCK:] + carry_ref[:, KV_WIDTH + co:KV_WIDTH + co + HEAD_DIM]).astype(BF16)
            carry_ref[:, co:co + HEAD_DIM] = dk[:BLOCK]
            carry_ref[:, KV_WIDTH + co:KV_WIDTH + co + HEAD_DIM] = dvb[:BLOCK]
        dgq_ref[...] += dgq
        dsink_ref[...] += dsink

    order = lambda n: nb - 1 - n
    cur, prev, params, full = _mix_specs(nb, order)
    dy_spec = pl.BlockSpec((BLOCK, A_WIDTH + B_WIDTH), lambda n: (order(n), 0))
    out, moved = _call(
        body, (z, z, dy, sw, sb_t, gv, gq, gk, sinks, bias), name=name, grid=(nb,),
        in_specs=[cur, prev, dy_spec] + params,
        out_specs=[cur, full((A_HEADS, BLOCK, BLOCK)), full((BLOCK, BLOCK)), full((A_HEADS, BLOCK)),
                   full((1, HEAD_DIM)), full((1, HEAD_DIM)), full((1, BLOCK)), full((B_HEADS, BLOCK, 2 * BLOCK))],
        out_shape=[jax.ShapeDtypeStruct((t, IN_COLS), BF16), jax.ShapeDtypeStruct((A_HEADS, BLOCK, BLOCK), F32),
                   jax.ShapeDtypeStruct((BLOCK, BLOCK), F32), jax.ShapeDtypeStruct((A_HEADS, BLOCK), F32),
                   jax.ShapeDtypeStruct((1, HEAD_DIM), F32), jax.ShapeDtypeStruct((1, HEAD_DIM), F32),
                   jax.ShapeDtypeStruct((1, BLOCK), F32), jax.ShapeDtypeStruct((B_HEADS, BLOCK, 2 * BLOCK), F32)],
        scratch_shapes=[pltpu.VMEM((BLOCK, 2 * KV_WIDTH), F32)],
        semantics=("arbitrary",), comm=comm)
    return out if comm is None else (out, moved)


def _ada_fwd(c_all, w_ada, b_cols, name, tn=768):
    nb, d = c_all.shape
    cols = w_ada.shape[1]

    def body(c_ref, w_ref, b_ref, act_ref, mod_ref):
        cv = c_ref[...]
        act = cv * jax.nn.sigmoid(cv)
        act_ref[...] = act
        mod_ref[...] = _dot(act.astype(BF16), w_ref[...].astype(BF16)) + b_ref[...]

    return _pc(
        body, name=name, grid=(cols // tn,),
        in_specs=[pl.BlockSpec((nb, d), lambda j: (0, 0)), pl.BlockSpec((d, tn), lambda j: (0, j)),
                  pl.BlockSpec((1, tn), lambda j: (0, j))],
        out_specs=[pl.BlockSpec((nb, d), lambda j: (0, 0)), pl.BlockSpec((nb, tn), lambda j: (0, j))],
        out_shape=[jax.ShapeDtypeStruct((nb, d), F32), jax.ShapeDtypeStruct((nb, cols), F32)],
        compiler_params=_params(("arbitrary",)),
    )(c_all, w_ada, b_cols)


def _adamw(w, g, m, v):
    m = ADAM_B1 * m + (1.0 - ADAM_B1) * g
    v = ADAM_B2 * v + (1.0 - ADAM_B2) * (g * g)
    m_hat = m / (1.0 - ADAM_B1 ** ADAM_STEP)
    v_hat = v / (1.0 - ADAM_B2 ** ADAM_STEP)
    delta = -ADAM_LR * (m_hat / (jnp.sqrt(v_hat) + ADAM_EPS) + ADAM_WD * w)
    return delta, m, v


def _adam_outer(act_t, dmod, w, m, v, name, tr=256):
    rows, cols = w.shape
    nb = dmod.shape[0]

    def body(act_ref, dm_ref, w_ref, m_ref, v_ref, g_ref, d_ref, nm_ref, nv_ref):
        act = act_ref[...].astype(BF16).astype(F32)
        dm = dm_ref[...].astype(BF16).astype(F32)
        g = act[:, 0:1] * dm[0:1, :]
        for b in range(1, nb):
            g = g + act[:, b:b + 1] * dm[b:b + 1, :]
        g_ref[...] = g
        d_ref[...], nm_ref[...], nv_ref[...] = _adamw(w_ref[...], g, m_ref[...], v_ref[...])

    tile = pl.BlockSpec((tr, cols), lambda i: (i, 0))
    return _pc(
        body, name=name, grid=(rows // tr,),
        in_specs=[pl.BlockSpec((tr, nb), lambda i: (i, 0)), pl.BlockSpec((nb, cols), lambda i: (0, 0)), tile, tile, tile],
        out_specs=[tile] * 4, out_shape=[jax.ShapeDtypeStruct((rows, cols), F32)] * 4,
        compiler_params=_params(("parallel",)),
    )(act_t, dmod, w, m, v)


def _adam_reduce(parts, w, m, v, name, tr, transposed=False):
    rows, cols = w.shape
    pieces = parts if isinstance(parts, (tuple, list)) else (parts,)
    n = len(pieces)

    def total(p_ref):
        g = p_ref[0].astype(F32)
        for i in range(1, N_DEV):
            g = g + p_ref[i].astype(F32)
        return g

    def body(*refs):
        w_ref, m_ref, v_ref, g_ref, d_ref, nm_ref, nv_ref = refs[n:]
        g = total(refs[0]) if n == 1 else jnp.concatenate([total(r) for r in refs[:n]], axis=0)
        if transposed:
            g = g.T
        g_ref[...] = g
        d_ref[...], nm_ref[...], nv_ref[...] = _adamw(w_ref[...], g, m_ref[...], v_ref[...])

    tile = pl.BlockSpec((tr, cols), lambda i: (i, 0))
    if transposed:
        parts_specs = [pl.BlockSpec((N_DEV, p.shape[1], tr), lambda i: (0, 0, i)) for p in pieces]
    else:
        assert n == 1
        parts_specs = [pl.BlockSpec((N_DEV, tr, cols), lambda i: (0, i, 0))]
    return _pc(
        body, name=name, grid=(rows // tr,),
        in_specs=parts_specs + [tile, tile, tile],
        out_specs=[tile] * 4, out_shape=[jax.ShapeDtypeStruct((rows, cols), F32)] * 4,
        compiler_params=_params(("parallel",)),
    )(*pieces, w, m, v)


def _pad_rows(v):
    flat = v.reshape(-1)
    pad = (-flat.shape[0]) % (8 * BLOCK)
    return jnp.pad(flat, (0, pad)).reshape(-1, BLOCK)


def kernel(x, c, w_ada, b_ada, g_ffn1, w1_ffn1, w3_ffn1, w2_ffn1, g_mix, w_in, spatial_w, spatial_b, g_v, g_q, g_k, sinks, rel_bias, w_out, g_ffn2, w1_ffn2, w3_ffn2, w2_ffn2, loss_target, m_w_ada, m_b_ada, m_g_ffn1, m_w1_ffn1, m_w3_ffn1, m_w2_ffn1, m_g_mix, m_w_in, m_spatial_w, m_spatial_b, m_g_v, m_g_q, m_g_k, m_sinks, m_rel_bias, m_w_out, m_g_ffn2, m_w1_ffn2, m_w3_ffn2, m_w2_ffn2, v_w_ada, v_b_ada, v_g_ffn1, v_w1_ffn1, v_w3_ffn1, v_w2_ffn1, v_g_mix, v_w_in, v_spatial_w, v_spatial_b, v_g_v, v_g_q, v_g_k, v_sinks, v_rel_bias, v_w_out, v_g_ffn2, v_w1_ffn2, v_w3_ffn2, v_w2_ffn2):
    me = _linear(_position())
    d = D_MODEL
    x0 = x[0]
    target = loss_target[0]
    t = x0.shape[0]

    big = dict(w1a=w1_ffn1[0], w3a=w3_ffn1[0], w2a=w2_ffn1[0], w_in=w_in[0], w_out=w_out[0],
               w1b=w1_ffn2[0], w3b=w3_ffn2[0], w2b=w2_ffn2[0])
    col_sharded = ("w1a", "w3a", "w_in", "w1b", "w3b")
    full = {}

    def gather(keys, scheme=_GatherTwoLevel):
        return scheme([(big[k].T if k in col_sharded else big[k]).astype(BF16) for k in keys])

    def unpack(keys, gathered):
        for k, gth in zip(keys, gathered):
            full[k] = gth.reshape(-1, gth.shape[-1])

    def share(k):
        gk_ = grads[k]
        return gk_.reshape(N_DEV, gk_.shape[0] // N_DEV, gk_.shape[1])

    c_all = _run_comm(_Direct([c], True), "gather_c")[0].reshape(N_DEV, d)
    b_cols = lax.dynamic_slice(b_ada, (0, me * MOD_COLS), (1, MOD_COLS))
    act_all, mod_cols = _ada_fwd(c_all, w_ada[0], b_cols, "ada_fwd")
    mod = _run_comm(_Direct([mod_cols.reshape(N_DEV, 1, MOD_COLS)], False), "scatter_mod")[0]
    mod = mod.reshape(N_MOD, 1, d)
    sh1, sc1, gt1, sh2, sc2, gt2, sh3, sc3, gt3 = [mod[i] for i in range(N_MOD)]

    tm = min(FFN_ROWS, t)
    h1, gathered = _norm_fwd(x0, None, None, 0.0, g_ffn1, sc1, sh1, "norm1_fwd",
                             comm=gather(("w1a",), _GatherRelayed), relay_at=0.9)
    unpack(("w1a",), gathered)
    a1, gathered = _matmul(h1, full["w1a"], "nt", tm, 512, d, F32, "ffn1_up_a", comm=gather(("w3a",)), relay_at=0.7)
    unpack(("w3a",), gathered)
    (ga1, gb1, s1), gathered = _ffn_up(h1, None, full["w3a"], "ffn1_up", comm=gather(("w2a", "w_in")),
                                     relay_at=0.7, a=a1)
    unpack(("w2a", "w_in"), gathered)
    f1, gathered = _matmul(s1, full["w2a"], "nn", tm, 512, D_FF, F32, "ffn1_down", comm=gather(("w_out", "w1b")),
                           relay_at=0.7)
    unpack(("w_out", "w1b"), gathered)
    x1, h2 = _norm_fwd(x0, f1, gt1, 0.5, g_mix, sc2, sh2, "norm2_fwd")
    z = _matmul(h2, full["w_in"], "nt", tm, IN_COLS // 2, d, F32, "mix_in_fwd")
    onehot = _bucket_onehot()
    bias = _select_matmul(rel_bias.T, jnp.asarray(onehot, BF16), "bias_table").reshape(B_HEADS, BLOCK, 2 * BLOCK)
    sb_t = spatial_b[0].T
    mix_params = (spatial_w[0], sb_t, g_v[0], g_q, g_k, sinks, bias)
    ycat, gathered = _mix_fwd(z, *mix_params, "mix_core_fwd", comm=gather(("w3b",)), relay_at=0.6)
    unpack(("w3b",), gathered)
    y = _matmul(ycat, full["w_out"], "nn", 512, d, d, F32, "mix_out_fwd")
    x2, h3 = _norm_fwd(x1, y, gt2, 1.0, g_ffn2, sc3, sh3, "norm3_fwd")
    (ga2, gb2, s2), gathered = _ffn_up(h3, full["w1b"], full["w3b"], "ffn2_up", comm=gather(("w2b",)), relay_at=0.6)
    unpack(("w2b",), gathered)
    f2 = _matmul(s2, full["w2b"], "nn", tm, 512, D_FF, F32, "ffn2_down")
    dx3, df2, loss_part = _loss_head(x2, f2, gt3, 0.5, target, "loss_head")
    loss = lax.psum(loss_part[0, 0], ("x", "y", "c"))

    grads, parts = {}, {}
    tk = min(DW_TOKENS, t)

    def exchange(k):
        return _Direct([share(k)], False)

    grads["w2b"] = _matmul(s2, df2, "tn", D_FF // 4, 512, tk, BF16, "ffn2_dw2")
    (da, db), moved = _ffn_dact(df2, ga2, gb2, full["w2b"], "ffn2_dact", comm=exchange("w2b"))
    parts["w2b"] = moved[0]
    grads["w1b"] = _matmul(da, h3, "tn", D_FF // 4, 512, tk, BF16, "ffn2_dw1")
    dh3, moved = _ffn_dh(da, db, full["w1b"], full["w3b"], "ffn2_dh", comm=exchange("w1b"))
    parts["w1b"] = moved[0]
    grads["w3b"] = _matmul(db, h3, "tn", D_FF // 4, 512, tk, BF16, "ffn2_dw3")
    dx2, dyg, sums3 = _norm_bwd(dh3, x2, dx3, f2, 0.5, g_ffn2, sc3, gt2, 1.0, "norm3_bwd")
    dycat = _matmul(dyg, full["w_out"], "nt", 512, d, d, F32, "mix_out_bwd")
    grads["w_out"] = _matmul(ycat, dyg, "tn", 1024, 512, tk, BF16, "mix_out_dw")
    mix_grads, moved = _mix_bwd(z, dycat, *mix_params, "mix_core_bwd",
                                comm=_Direct([share("w3b"), share("w_out")], False))
    parts["w3b"], parts["w_out"] = moved
    dz, d_sw, d_sb, d_gv, d_gq, d_gk, d_sink, d_bias = mix_grads
    d_rel = _select_matmul(d_bias.reshape(B_HEADS, BLOCK * 2 * BLOCK), jnp.asarray(onehot.T, BF16), "bias_table_bwd")
    dh2 = _matmul(dz, full["w_in"], "nn", tm, d, IN_COLS // 2, F32, "mix_in_bwd")
    grads["w_in"] = _matmul(dz, h2, "tn", IN_COLS // 2, 512, tk // 2, BF16, "mix_in_dw")
    dx1, df1, sums2 = _norm_bwd(dh2, x1, dx2, y, 1.0, g_mix, sc2, gt1, 0.5, "norm2_bwd")
    grads["w2a"], moved = _matmul(s1, df1, "tn", D_FF // 4, 512, tk, BF16, "ffn1_dw2", comm=exchange("w_in"))
    parts["w_in"] = moved[0]
    (da, db), moved = _ffn_dact(df1, ga1, gb1, full["w2a"], "ffn1_dact", comm=exchange("w2a"))
    parts["w2a"] = moved[0]
    early = [("g_mix", sums2[2:3], g_mix, m_g_mix, v_g_mix),
             ("spatial_w", d_sw[None], spatial_w, m_spatial_w, v_spatial_w),
             ("spatial_b", d_sb[:, :A_HEADS].T[None], spatial_b, m_spatial_b, v_spatial_b),
             ("g_v", d_gv[None], g_v, m_g_v, v_g_v),
             ("g_q", d_gq, g_q, m_g_q, v_g_q),
             ("g_k", d_gk, g_k, m_g_k, v_g_k),
             ("sinks", d_sink[:, :B_HEADS], sinks, m_sinks, v_sinks),
             ("rel_bias", d_rel.T, rel_bias, m_rel_bias, v_rel_bias),
             ("g_ffn2", sums3[2:3], g_ffn2, m_g_ffn2, v_g_ffn2)]

    def pack(items, i):
        return jnp.concatenate([_pad_rows(item[i]) for item in items], axis=0)

    grads["w1a"], moved = _matmul(da, h1, "tn", D_FF // 4, 512, tk, BF16, "ffn1_dw1", comm=_Direct([pack(early, 1)], True))
    early_all = moved[0]
    half = share("w1a").shape[1] // 2
    grads["w3a"], moved = _matmul(db, h1, "tn", D_FF // 4, 512, tk, BF16, "ffn1_dw3",
                                  comm=_Direct([share("w1a")], False, rows=[(0, half)]))
    w1a_lo = moved[0]
    dh1, moved = _ffn_dh(da, db, full["w1a"], full["w3a"], "ffn1_dh",
                         comm=_Direct([share("w3a"), share("w1a")], False, rows=[None, (half, half)]))
    parts["w3a"], parts["w1a"] = moved[0], (w1a_lo, moved[1])
    dx0, sums1 = _norm_bwd(dh1, x0, dx1, f1, 0.5, g_ffn1, sc1, None, 0.0, "norm1_bwd")

    moments = dict(w1a=(m_w1_ffn1, v_w1_ffn1), w3a=(m_w3_ffn1, v_w3_ffn1), w2a=(m_w2_ffn1, v_w2_ffn1),
                   w_in=(m_w_in, v_w_in), w_out=(m_w_out, v_w_out),
                   w1b=(m_w1_ffn2, v_w1_ffn2), w3b=(m_w3_ffn2, v_w3_ffn2), w2b=(m_w2_ffn2, v_w2_ffn2))
    upd = {}
    for k in big:
        rows = big[k].shape[0]
        tr = 256 if rows % 256 == 0 else rows // 4
        upd[k] = [o[None] for o in _adam_reduce(parts[k], big[k], moments[k][0][0], moments[k][1][0], "adam_" + k, tr,
                                                transposed=k in col_sharded)]

    dmod = jnp.concatenate([sums1[1:2], sums1[0:1], sums1[3:4], sums2[1:2], sums2[0:1], sums2[3:4],
                            sums3[1:2], sums3[0:1], sums3[3:4]], axis=1)
    late = [("b_ada", dmod, b_ada, m_b_ada, v_b_ada), ("g_ffn1", sums1[2:3], g_ffn1, m_g_ffn1, v_g_ffn1)]
    late_all = _run_comm(_Direct([pack(late, 1)], True), "gather_small")[0]
    for items, g_all, name in ((early, early_all, "adam_small_early"), (late, late_all, "adam_small_late")):
        small_out = _adam_reduce(g_all, pack(items, 2), pack(items, 3), pack(items, 4), name, g_all.shape[1])
        off = 0
        for key, _, w_, _, _ in items:
            n_rows = _pad_rows(w_).shape[0]
            upd[key] = [o[off:off + n_rows].reshape(-1)[:w_.size].reshape(w_.shape) for o in small_out]
            off += n_rows

    dmod_rows = late_all[:, :N_MOD * d // BLOCK, :].reshape(N_DEV, N_MOD * d)
    dmod_cols = lax.dynamic_slice(dmod_rows, (0, me * MOD_COLS), (N_DEV, MOD_COLS))
    upd["w_ada"] = [o[None] for o in _adam_outer(act_all.T, dmod_cols, w_ada[0], m_w_ada[0], v_w_ada[0], "adam_w_ada")]

    order = [("w_ada", "w_ada"), ("b_ada", "b_ada"), ("g_ffn1", "g_ffn1"), ("w1_ffn1", "w1a"), ("w3_ffn1", "w3a"),
             ("w2_ffn1", "w2a"), ("g_mix", "g_mix"), ("w_in", "w_in"), ("spatial_w", "spatial_w"),
             ("spatial_b", "spatial_b"), ("g_v", "g_v"), ("g_q", "g_q"), ("g_k", "g_k"), ("sinks", "sinks"),
             ("rel_bias", "rel_bias"), ("w_out", "w_out"), ("g_ffn2", "g_ffn2"), ("w1_ffn2", "w1b"),
             ("w3_ffn2", "w3b"), ("w2_ffn2", "w2b")]
    outs = [loss, dx0[None]]
    for i in range(4):
        outs += [upd[key][i] for _, key in order]
    return tuple(outs)
```

```python
import math

import numpy as np
import jax
import jax.numpy as jnp
from jax import lax
from jax.experimental import pallas as pl
from jax.experimental.pallas import tpu as pltpu

F32 = jnp.float32
BF16 = jnp.bfloat16

D_MODEL = 2048
D_FF = 5632
BLOCK = 128
A_HEADS = 8
A_WIDTH = 1024
B_HEADS = 16
B_KV_HEADS = 2
GROUP = B_HEADS // B_KV_HEADS
HEAD_DIM = 64
B_WIDTH = 1024
KV_WIDTH = 128
IN_COLS = 3328
Q_OFF = 2 * A_WIDTH
K_OFF = Q_OFF + B_WIDTH
V_OFF = K_OFF + KV_WIDTH
N_BUCKETS = 32
N_MOD = 9
EPS = 1e-6
N_DEV = 8
MOD_COLS = N_MOD * D_MODEL // N_DEV

ADAM_LR = 0.001
ADAM_B1 = 0.9
ADAM_B2 = 0.999
ADAM_EPS = 1e-08
ADAM_WD = 0.01
ADAM_STEP = 10

FFN_ROWS = 1024
STREAM_BUFFERS = 3
DW_TOKENS = 4096
MASK_VALUE = -1e30
VMEM_LIMIT = 56 * 1024 * 1024
MESH_ID = pl.DeviceIdType.MESH
ANY = pl.BlockSpec(memory_space=pl.ANY)

_SQRT_HALF = 0.7071067811865476
_INV_SQRT_2PI = 0.3989422804014327


def _pc(body, **kw):
    return pl.pallas_call(body, **kw)


def _params(sem=None):
    return pltpu.CompilerParams(dimension_semantics=sem, vmem_limit_bytes=VMEM_LIMIT)


def _dot(a, b):
    return lax.dot_general(a, b, (((1,), (0,)), ((), ())), preferred_element_type=F32)


def _dot_nt(a, b):
    return lax.dot_general(a, b, (((1,), (1,)), ((), ())), preferred_element_type=F32)


def _dot_tn(a, b):
    return lax.dot_general(a, b, (((0,), (0,)), ((), ())), preferred_element_type=F32)


def _gelu(x):
    return 0.5 * x * (1.0 + lax.erf(x * _SQRT_HALF))


def _gelu_and_grad(x):
    cdf = 0.5 * (1.0 + lax.erf(x * _SQRT_HALF))
    return x * cdf, cdf + x * jnp.exp(-0.5 * x * x) * _INV_SQRT_2PI


def _rms(x):
    r = lax.rsqrt(jnp.mean(x * x, axis=-1, keepdims=True) + EPS)
    return x * r, r


def _rms_bwd(dy, y, r):
    return r * (dy - y * jnp.mean(dy * y, axis=-1, keepdims=True))


def _row_spec(tm, cols):
    return pl.BlockSpec((tm, cols), lambda i: (i, 0))


def _vec_spec(cols):
    return pl.BlockSpec((1, cols), lambda i: (0, 0))


def _position():
    x, y, c = lax.axis_index("x"), lax.axis_index("y"), lax.axis_index("c")
    return x, y, c


def _linear(p):
    return 4 * p[0] + 2 * p[1] + p[2]


class _Comm:
    PEER_COPIES = N_DEV - 1

    def __init__(self, arrs):
        self.arrs = list(arrs)
        n = len(self.arrs)
        self.scratch_shapes = [pltpu.SemaphoreType.DMA((self.PEER_COPIES * n,)),
                               pltpu.SemaphoreType.DMA((self.PEER_COPIES * n,)),
                               pltpu.SemaphoreType.DMA((n,))]

    def bind(self, srcs, dsts, sems):
        self.srcs, self.dsts = srcs, dsts
        self.send_sems, self.recv_sems, self.local_sems = sems
        x, y, c = _position()
        self.me, self.sibling, self.core = (x, y, c), (x, y, 1 - c), c
        self.chips = [(1 - x, y), (x, 1 - y), (1 - x, 1 - y)]
        self.peers = [(1 - x if k & 4 else x, 1 - y if k & 2 else y, 1 - c if k & 1 else c)
                      for k in range(1, N_DEV)]

    def relay(self):
        pass


class _GatherTwoLevel(_Comm):
    def __init__(self, arrs):
        super().__init__(arrs)
        self.out_shape = [jax.ShapeDtypeStruct((N_DEV,) + a.shape, a.dtype) for a in self.arrs]

    def _copy(self, a, k, block, to, from_input=False):
        rows = self.dsts[a].at[_linear(block)]
        return pltpu.make_async_remote_copy(
            src_ref=self.srcs[a] if from_input else rows, dst_ref=rows,
            send_sem=self.send_sems.at[self.PEER_COPIES * a + k], recv_sem=self.recv_sems.at[self.PEER_COPIES * a + k],
            device_id=to, device_id_type=MESH_ID)

    def _local(self, a):
        return pltpu.make_async_copy(self.srcs[a], self.dsts[a].at[_linear(self.me)], self.local_sems.at[a])

    def _first(self, a):
        return [self._copy(a, 0, self.me, self.sibling, True)] + [
            self._copy(a, 1 + j, self.me, (*chip, self.core), True) for j, chip in enumerate(self.chips)]

    def _passed(self, a):
        return [self._copy(a, 4 + j, (*chip, self.core), self.sibling) for j, chip in enumerate(self.chips)]

    def start(self):
        for a in range(len(self.arrs)):
            self._local(a).start()
            for cp in self._first(a):
                cp.start()

    def relay(self):
        for a in range(len(self.arrs)):
            for j, chip in enumerate(self.chips):
                self._copy(a, 1 + j, (*chip, self.core), self.me).wait_recv()
                self._passed(a)[j].start()

    def finish(self):
        for a in range(len(self.arrs)):
            self._copy(a, 0, self.sibling, self.me).wait_recv()
            for j, chip in enumerate(self.chips):
                self._copy(a, 4 + j, (*chip, 1 - self.core), self.me).wait_recv()
        for a in range(len(self.arrs)):
            for cp in self._first(a) + self._passed(a):
                cp.wait_send()
            self._local(a).wait()


class _GatherRelayed(_GatherTwoLevel):
    def _first(self, a):
        return [self._copy(a, 0, self.me, self.sibling, True)] + [
            self._copy(a, 1 + j, self.me, (*self.chips[j], self.core), True) for j in range(2)]

    def _onward(self, a):
        x, y, c = self.me
        north = c == 1
        source = (jnp.where(north, 1 - x, x), jnp.where(north, y, 1 - y), c)
        target = (jnp.where(north, x, 1 - x), jnp.where(north, 1 - y, y), c)
        return self._copy(a, 3, source, target)

    def relay(self):
        for a in range(len(self.arrs)):
            for j in range(2):
                self._copy(a, 1 + j, (*self.chips[j], self.core), self.me).wait_recv()
                self._passed(a)[j].start()
            self._onward(a).start()
        for a in range(len(self.arrs)):
            self._copy(a, 3, (*self.chips[2], self.core), self.me).wait_recv()
            self._passed(a)[2].start()

    def finish(self):
        for a in range(len(self.arrs)):
            self._copy(a, 0, self.sibling, self.me).wait_recv()
            for j, chip in enumerate(self.chips):
                self._copy(a, 4 + j, (*chip, 1 - self.core), self.me).wait_recv()
        for a in range(len(self.arrs)):
            for cp in self._first(a) + [self._onward(a)] + self._passed(a):
                cp.wait_send()
            self._local(a).wait()


class _Direct(_Comm):
    def __init__(self, arrs, broadcast, rows=None):
        super().__init__(arrs)
        self.broadcast = broadcast
        self.rows = rows or [None] * len(self.arrs)
        self.out_shape = []
        for a, window in zip(self.arrs, self.rows):
            shape = ((N_DEV,) + a.shape) if broadcast else a.shape
            if window is not None:
                shape = (shape[0], window[1]) + shape[2:]
            self.out_shape.append(jax.ShapeDtypeStruct(shape, a.dtype))

    def _outgoing(self, a, to):
        if self.broadcast:
            return self.srcs[a]
        block = self.srcs[a].at[_linear(to)]
        return block if self.rows[a] is None else block.at[pl.ds(*self.rows[a])]

    def _copy(self, a, k, sender, to):
        return pltpu.make_async_remote_copy(
            src_ref=self._outgoing(a, to), dst_ref=self.dsts[a].at[_linear(sender)],
            send_sem=self.send_sems.at[self.PEER_COPIES * a + k], recv_sem=self.recv_sems.at[self.PEER_COPIES * a + k],
            device_id=to, device_id_type=MESH_ID)

    def _local(self, a):
        return pltpu.make_async_copy(self._outgoing(a, self.me), self.dsts[a].at[_linear(self.me)],
                                     self.local_sems.at[a])

    def start(self):
        for a in range(len(self.arrs)):
            self._local(a).start()
            for k, peer in enumerate(self.peers):
                self._copy(a, k, self.me, peer).start()

    def finish(self):
        for a in range(len(self.arrs)):
            for k, peer in enumerate(self.peers):
                self._copy(a, k, peer, self.me).wait_recv()
        for a in range(len(self.arrs)):
            for k, peer in enumerate(self.peers):
                self._copy(a, k, self.me, peer).wait_send()
            self._local(a).wait()


def _run_comm(comm, name):
    n = len(comm.arrs)

    def body(*refs):
        comm.bind(refs[:n], refs[n:2 * n], refs[2 * n:])
        comm.start()
        comm.relay()
        comm.finish()

    return _pc(body, name=name, out_shape=comm.out_shape, in_specs=[ANY] * n, out_specs=[ANY] * n,
               scratch_shapes=comm.scratch_shapes)(*comm.arrs)


def _call(body, inputs, *, name, grid, in_specs, out_specs, out_shape, scratch_shapes=(), semantics,
          comm=None, relay_at=0.5):
    if comm is None:
        out = _pc(body, name=name, grid=grid, in_specs=in_specs, out_specs=out_specs, out_shape=out_shape,
                  scratch_shapes=list(scratch_shapes), compiler_params=_params(semantics))(*inputs)
        return list(out), []
    n_in, n_out, n_sc, k = len(in_specs), len(out_specs), len(scratch_shapes), len(comm.arrs)
    steps = math.prod(grid)
    relay_step = min(steps - 1, int(steps * relay_at))

    def carrier(*refs):
        ins, rest = refs[:n_in], refs[n_in:]
        csrc, rest = rest[:k], rest[k:]
        outs, rest = rest[:n_out], rest[n_out:]
        cdst, rest = rest[:k], rest[k:]
        scratch, csems = rest[:n_sc], rest[n_sc:]
        step = 0
        for axis, size in enumerate(grid):
            step = step * size + pl.program_id(axis)
        comm.bind(csrc, cdst, csems)
        pl.when(step == 0)(comm.start)
        pl.when(step == relay_step)(comm.relay)
        body(*ins, *outs, *scratch)
        pl.when(step == steps - 1)(comm.finish)

    out = _pc(carrier, name=name, grid=grid, in_specs=list(in_specs) + [ANY] * k,
              out_specs=list(out_specs) + [ANY] * k, out_shape=list(out_shape) + comm.out_shape,
              scratch_shapes=list(scratch_shapes) + comm.scratch_shapes,
              compiler_params=_params(("arbitrary",) * len(grid)))(*inputs, *comm.arrs)
    return list(out[:n_out]), list(out[n_out:])


def _matmul(a, b, mode, tm, tn, tk, out_dtype, name, comm=None, relay_at=0.5):
    if mode == "nn":
        (m, kk), nn = a.shape, b.shape[1]
        a_spec = pl.BlockSpec((tm, tk), lambda i, k, j: (i, k))
        b_spec = pl.BlockSpec((tk, tn), lambda i, k, j: (k, j))
        dot = _dot
    elif mode == "nt":
        (m, kk), nn = a.shape, b.shape[0]
        a_spec = pl.BlockSpec((tm, tk), lambda i, k, j: (i, k))
        b_spec = pl.BlockSpec((tn, tk), lambda i, k, j: (j, k))
        dot = _dot_nt
    else:
        (kk, m), nn = a.shape, b.shape[1]
        a_spec = pl.BlockSpec((tk, tm), lambda i, k, j: (k, i))
        b_spec = pl.BlockSpec((tk, tn), lambda i, k, j: (k, j))
        dot = _dot_tn
    assert m % tm == 0 and nn % tn == 0 and kk % tk == 0, (a.shape, b.shape, tm, tn, tk)
    nk, nj = kk // tk, nn // tn

    def body(a_ref, b_ref, o_ref, *acc):
        _accumulate_over_k(dot(a_ref[...], b_ref[...]), o_ref, acc, nk)

    out, moved = _call(
        body, (a, b), name=name, grid=(m // tm, nk, nj),
        in_specs=[a_spec, b_spec], out_specs=[_out_tile_spec(tm, tn, nk)],
        out_shape=[jax.ShapeDtypeStruct((m, nn), out_dtype)],
        scratch_shapes=[pltpu.VMEM((nj, tm, tn), F32)] if nk > 1 else [],
        semantics=("parallel", "arbitrary", "arbitrary"), comm=comm, relay_at=relay_at)
    return out[0] if comm is None else (out[0], moved)


def _out_tile_spec(tm, tn, nk):
    return pl.BlockSpec((tm, tn), lambda i, k, j: (i, jnp.where(k == nk - 1, j, 0)))


def _accumulate_over_k(p, o_ref, acc, nk):
    if nk == 1:
        o_ref[...] = p.astype(o_ref.dtype)
        return
    k, j = pl.program_id(1), pl.program_id(2)
    acc_ref = acc[0]

    @pl.when(k == 0)
    def _():
        acc_ref[j] = p

    @pl.when((k > 0) & (k < nk - 1))
    def _():
        acc_ref[j] += p

    @pl.when(k == nk - 1)
    def _():
        o_ref[...] = (acc_ref[j] + p).astype(o_ref.dtype)


def _norm_fwd(x_prev, f_prev, gate_prev, coef, g, sc, sh, name, tm=512, comm=None, relay_at=0.5):
    t, d = x_prev.shape
    residual = f_prev is not None

    def body(*refs):
        if residual:
            xp_ref, fp_ref, gp_ref, g_ref, sc_ref, sh_ref, x_ref, h_ref = refs
            x = xp_ref[...] + coef * gp_ref[...] * fp_ref[...]
            x_ref[...] = x
        else:
            xp_ref, g_ref, sc_ref, sh_ref, h_ref = refs
            x = xp_ref[...]
        y, _ = _rms(x)
        h_ref[...] = ((y * g_ref[...]) * (1.0 + sc_ref[...]) + sh_ref[...]).astype(BF16)

    row, vec = _row_spec(tm, d), _vec_spec(d)
    if residual:
        ins, in_specs = (x_prev, f_prev, gate_prev, g, sc, sh), [row, row, vec, vec, vec, vec]
        out_shape = [jax.ShapeDtypeStruct((t, d), F32), jax.ShapeDtypeStruct((t, d), BF16)]
        out_specs = [row, row]
    else:
        ins, in_specs = (x_prev, g, sc, sh), [row, vec, vec, vec]
        out_shape = [jax.ShapeDtypeStruct((t, d), BF16)]
        out_specs = [row]
    out, moved = _call(body, ins, name=name, grid=(t // tm,), in_specs=in_specs, out_specs=out_specs,
                       out_shape=out_shape, semantics=("parallel",), comm=comm, relay_at=relay_at)
    out = out if residual else out[0]
    return out if comm is None else (out, moved)


def _norm_bwd(dh, x, dxo, fo, coef_o, g, sc, gate_prev, coef_prev, name, tm=256, comm=None):
    t, d = x.shape
    with_prev = gate_prev is not None

    def body(*refs):
        if with_prev:
            dh_ref, x_ref, dxo_ref, fo_ref, g_ref, sc_ref, gp_ref, dx_ref, dfp_ref, sums_ref = refs
        else:
            dh_ref, x_ref, dxo_ref, fo_ref, g_ref, sc_ref, dx_ref, sums_ref = refs
        dh_v, dxo_v = dh_ref[...], dxo_ref[...]
        y, r = _rms(x_ref[...])
        n = y * g_ref[...]
        dn = dh_v * (1.0 + sc_ref[...])
        dx = dxo_v + _rms_bwd(dn * g_ref[...], y, r)
        dx_ref[...] = dx
        if with_prev:
            dfp_ref[...] = (coef_prev * gp_ref[...] * dx).astype(BF16)
        @pl.when(pl.program_id(0) == 0)
        def _():
            sums_ref[...] = jnp.zeros_like(sums_ref)

        sums_ref[0:1, :] += jnp.sum(dh_v * n, axis=0, keepdims=True)
        sums_ref[1:2, :] += jnp.sum(dh_v, axis=0, keepdims=True)
        sums_ref[2:3, :] += jnp.sum(dn * y, axis=0, keepdims=True)
        sums_ref[3:4, :] += jnp.sum(coef_o * dxo_v * fo_ref[...], axis=0, keepdims=True)

    row, vec = _row_spec(tm, d), _vec_spec(d)
    sums_spec = pl.BlockSpec((8, d), lambda i: (0, 0))
    ins, in_specs = [dh, x, dxo, fo, g, sc], [row, row, row, row, vec, vec]
    out_shape, out_specs = [jax.ShapeDtypeStruct((t, d), F32)], [row]
    if with_prev:
        ins.append(gate_prev)
        in_specs.append(vec)
        out_shape.append(jax.ShapeDtypeStruct((t, d), BF16))
        out_specs.append(row)
    out_shape.append(jax.ShapeDtypeStruct((8, d), F32))
    out_specs.append(sums_spec)
    out, moved = _call(body, ins, name=name, grid=(t // tm,), in_specs=in_specs, out_specs=out_specs,
                       out_shape=out_shape, semantics=("arbitrary",), comm=comm)
    return out if comm is None else (out, moved)


def _ffn_up(h, w1t, w3t, name, tf=512, comm=None, relay_at=0.5):
    t, d = h.shape
    ff = w1t.shape[0]
    tm = min(FFN_ROWS, t)

    def body(h_ref, w1_ref, w3_ref, ga_ref, gb_ref, s_ref):
        hv = h_ref[...]
        a = _dot_nt(hv, w1_ref[...])
        b = _dot_nt(hv, w3_ref[...])
        sg = jax.nn.sigmoid(a)
        sil = a * sg
        ga_ref[...] = (b * (sg * (1.0 + a * (1.0 - sg)))).astype(BF16)
        gb_ref[...] = sil.astype(BF16)
        s_ref[...] = (sil * b).astype(BF16)

    tile = pl.BlockSpec((tm, tf), lambda i, j: (i, j))
    cols = pl.BlockSpec((tf, d), lambda i, j: (j, 0))
    out, moved = _call(
        body, (h, w1t, w3t), name=name, grid=(t // tm, ff // tf),
        in_specs=[pl.BlockSpec((tm, d), lambda i, j: (i, 0)), cols, cols], out_specs=[tile, tile, tile],
        out_shape=[jax.ShapeDtypeStruct((t, ff), BF16)] * 3,
        semantics=("parallel", "parallel"), comm=comm, relay_at=relay_at)
    return out if comm is None else (out, moved)


def _ffn_dact(df, ga, gb, w2, name, tf=512, comm=None):
    t, d = df.shape
    ff = ga.shape[1]
    tm = min(2 * FFN_ROWS, t)
    ni, nj = t // tm, ff // tf
    steps, ahead = ni * nj, STREAM_BUFFERS - 1

    def body(df_ref, ga_hbm, gb_hbm, w2_ref, da_ref, db_ref, ga_buf, gb_buf, ga_sem, gb_sem):
        step = pl.program_id(0) * nj + pl.program_id(1)

        def fetch(s):
            s = jnp.asarray(s, jnp.int32)
            slot = lax.rem(s, STREAM_BUFFERS)
            rows = pl.ds(pl.multiple_of(lax.div(s, nj) * tm, tm), tm)
            cols = pl.ds(pl.multiple_of(lax.rem(s, nj) * tf, tf), tf)
            return (pltpu.make_async_copy(ga_hbm.at[rows, cols], ga_buf.at[slot], ga_sem.at[slot]),
                    pltpu.make_async_copy(gb_hbm.at[rows, cols], gb_buf.at[slot], gb_sem.at[slot]))

        @pl.when(step == 0)
        def _():
            for s in range(min(ahead, steps)):
                for cp in fetch(s):
                    cp.start()

        @pl.when(step + ahead < steps)
        def _():
            for cp in fetch(step + ahead):
                cp.start()

        for cp in fetch(step):
            cp.wait()
        slot = lax.rem(step, STREAM_BUFFERS)
        ds = _dot_nt(df_ref[...], w2_ref[...])
        da_ref[...] = (ds * ga_buf[slot]).astype(BF16)
        db_ref[...] = (ds * gb_buf[slot]).astype(BF16)

    tile = pl.BlockSpec((tm, tf), lambda i, j: (i, j))
    out, moved = _call(
        body, (df, ga, gb, w2), name=name, grid=(ni, nj),
        in_specs=[pl.BlockSpec((tm, d), lambda i, j: (i, 0)), ANY, ANY, pl.BlockSpec((tf, d), lambda i, j: (j, 0))],
        out_specs=[tile, tile], out_shape=[jax.ShapeDtypeStruct((t, ff), BF16)] * 2,
        scratch_shapes=[pltpu.VMEM((STREAM_BUFFERS, tm, tf), BF16), pltpu.VMEM((STREAM_BUFFERS, tm, tf), BF16),
                        pltpu.SemaphoreType.DMA((STREAM_BUFFERS,)), pltpu.SemaphoreType.DMA((STREAM_BUFFERS,))],
        semantics=("arbitrary", "arbitrary"), comm=comm)
    return out if comm is None else (out, moved)


def _ffn_dh(da, db, w1t, w3t, name, tn=512, comm=None):
    t, ff = da.shape
    d = w1t.shape[1]
    tm, tk = min(FFN_ROWS, t), ff // 2

    def body(da_ref, db_ref, w1_ref, w3_ref, dh_ref, acc_ref):
        p = _dot(da_ref[...], w1_ref[...]) + _dot(db_ref[...], w3_ref[...])
        _accumulate_over_k(p, dh_ref, (acc_ref,), 2)

    act = pl.BlockSpec((tm, tk), lambda i, k, j: (i, k))
    wgt = pl.BlockSpec((tk, tn), lambda i, k, j: (k, j))
    out, moved = _call(
        body, (da, db, w1t, w3t), name=name, grid=(t // tm, 2, d // tn),
        in_specs=[act, act, wgt, wgt], out_specs=[_out_tile_spec(tm, tn, 2)],
        out_shape=[jax.ShapeDtypeStruct((t, d), F32)],
        scratch_shapes=[pltpu.VMEM((d // tn, tm, tn), F32)],
        semantics=("parallel", "arbitrary", "arbitrary"), comm=comm)
    return out[0] if comm is None else (out[0], moved)


def _loss_head(x_prev, f_prev, gate_prev, coef, target, name, tm=512):
    t, d = x_prev.shape
    steps = t // tm

    def body(xp_ref, fp_ref, gp_ref, tg_ref, dy_ref, df_ref, loss_ref, acc_ref):
        i = pl.program_id(0)
        e = xp_ref[...] + coef * gp_ref[...] * fp_ref[...] - tg_ref[...]
        dy = e * (1.0 / d)
        dy_ref[...] = dy
        df_ref[...] = (coef * gp_ref[...] * dy).astype(BF16)
        part = jnp.sum(e * e, axis=0, keepdims=True)

        @pl.when(i == 0)
        def _():
            acc_ref[...] = part

        @pl.when(i > 0)
        def _():
            acc_ref[...] += part

        @pl.when(i == steps - 1)
        def _():
            loss_ref[...] = jnp.sum(acc_ref[...], axis=1, keepdims=True) * (0.5 / d)

    row, vec = _row_spec(tm, d), _vec_spec(d)
    return _pc(
        body, name=name, grid=(steps,),
        in_specs=[row, row, vec, row], out_specs=[row, row, pl.BlockSpec((1, 1), lambda i: (0, 0))],
        out_shape=[jax.ShapeDtypeStruct((t, d), F32), jax.ShapeDtypeStruct((t, d), BF16),
                   jax.ShapeDtypeStruct((1, 1), F32)],
        scratch_shapes=[pltpu.VMEM((1, d), F32)],
        compiler_params=_params(("arbitrary",)),
    )(x_prev, f_prev, gate_prev, target)


def _split3(x):
    hi = x.astype(BF16)
    r1 = x - hi.astype(F32)
    mid = r1.astype(BF16)
    lo = (r1 - mid.astype(F32)).astype(BF16)
    return hi, mid, lo


def _select_matmul(a, onehot, name):
    m, n = a.shape[0], onehot.shape[1]

    def body(a_ref, oh_ref, o_ref):
        hi, mid, lo = _split3(a_ref[...])
        oh = oh_ref[...]
        o_ref[...] = (_dot(hi, oh) + _dot(mid, oh)) + _dot(lo, oh)

    return _pc(body, name=name, out_shape=jax.ShapeDtypeStruct((m, n), F32), compiler_params=_params())(a, onehot)


def _bucket_onehot():
    qi = np.arange(BLOCK)[:, None]
    kj = np.arange(2 * BLOCK)[None, :]
    dist = np.clip(qi + BLOCK - kj, 0, None)
    nf = np.maximum(dist, 1).astype(np.float32)
    large = 16 + (np.log(nf / np.float32(16)) / np.float32(math.log(128 / 16)) * np.float32(16)).astype(np.int32)
    bucket = np.where(dist < 16, dist, np.minimum(large, N_BUCKETS - 1)).reshape(-1)
    return (bucket[None, :] == np.arange(N_BUCKETS)[:, None]).astype(np.float32)


def _window_mask(n):
    row = lax.broadcasted_iota(jnp.int32, (GROUP * BLOCK, 2 * BLOCK), 0)
    qi = row & (BLOCK - 1)
    kj = lax.broadcasted_iota(jnp.int32, (GROUP * BLOCK, 2 * BLOCK), 1)
    return (kj > qi) & (kj <= qi + BLOCK) & ((kj >= BLOCK) | (n > 0))


def _kv_band(zc_ref, zp_ref, kvh):
    lo, hi = kvh * HEAD_DIM, (kvh + 1) * HEAD_DIM
    k_raw = jnp.concatenate([zp_ref[:, lo:hi], zc_ref[:, K_OFF + lo:K_OFF + hi]], axis=0)
    v_raw = jnp.concatenate([zp_ref[:, KV_WIDTH + lo:KV_WIDTH + hi], zc_ref[:, V_OFF + lo:V_OFF + hi]], axis=0)
    return k_raw, v_raw


def _stack_heads(ref, off):
    return jnp.concatenate([ref[:, off + g * HEAD_DIM:off + (g + 1) * HEAD_DIM] for g in range(GROUP)], axis=0)


def _unstack_heads(ref, off, stacked):
    for g in range(GROUP):
        ref[:, off + g * HEAD_DIM:off + (g + 1) * HEAD_DIM] = stacked[g * BLOCK:(g + 1) * BLOCK]


def _group_softmax(qn, kband, bias_ref, sink_ref, kvh, valid):
    bias = bias_ref[kvh * GROUP:(kvh + 1) * GROUP].reshape(GROUP * BLOCK, 2 * BLOCK)
    s = _dot_nt(qn, kband) * (HEAD_DIM ** -0.5) + bias
    s = jnp.where(valid, s, MASK_VALUE)
    sink = jnp.concatenate([jnp.full((BLOCK, 1), sink_ref[0, kvh * GROUP + g], F32) for g in range(GROUP)], axis=0)
    m = jnp.maximum(jnp.max(s, axis=-1, keepdims=True), sink)
    p = jnp.exp(s - m)
    esink = jnp.exp(sink - m)
    inv = 1.0 / (jnp.sum(p, axis=-1, keepdims=True) + esink)
    return p * inv, esink * inv, inv


def _mix_specs(nb, order):
    cur = pl.BlockSpec((BLOCK, IN_COLS), lambda n: (order(n), 0))
    prev = pl.BlockSpec((BLOCK, 2 * KV_WIDTH), lambda n: (jnp.maximum(order(n) - 1, 0), K_OFF // (2 * KV_WIDTH)))
    full = lambda shape: pl.BlockSpec(shape, lambda n: (0,) * len(shape))
    params = [full((A_HEADS, BLOCK, BLOCK)), full((BLOCK, A_HEADS)), full((A_HEADS, BLOCK)),
              full((1, HEAD_DIM)), full((1, HEAD_DIM)), pl.BlockSpec(memory_space=pltpu.SMEM),
              full((B_HEADS, BLOCK, 2 * BLOCK))]
    return cur, prev, params, full


def _mix_fwd(z, sw, sb_t, gv, gq, gk, sinks, bias, name, comm=None, relay_at=0.5):
    t = z.shape[0]
    nb = t // BLOCK

    def body(zc_ref, zp_ref, sw_ref, sbt_ref, gv_ref, gq_ref, gk_ref, sink_ref, bias_ref, y_ref):
        n = pl.program_id(0)
        ri = lax.broadcasted_iota(jnp.int32, (BLOCK, BLOCK), 0)
        ci = lax.broadcasted_iota(jnp.int32, (BLOCK, BLOCK), 1)
        tril = ri >= ci
        for h in range(A_HEADS):
            u = _gelu(zc_ref[:, h * BLOCK:(h + 1) * BLOCK])
            vv = _gelu(zc_ref[:, A_WIDTH + h * BLOCK:A_WIDTH + (h + 1) * BLOCK])
            vhat, _ = _rms(vv)
            vn = (vhat * gv_ref[h:h + 1, :]).astype(BF16)
            w = jnp.where(tril, sw_ref[h], 0.0).astype(BF16)
            mixed = _dot(w, vn) + sbt_ref[:, h:h + 1]
            y_ref[:, h * BLOCK:(h + 1) * BLOCK] = (u * mixed).astype(BF16)
        valid = _window_mask(n)
        for kvh in range(B_KV_HEADS):
            k_raw, v_raw = _kv_band(zc_ref, zp_ref, kvh)
            khat, _ = _rms(k_raw)
            kband = (khat * gk_ref[...]).astype(BF16)
            vband = v_raw.astype(BF16)
            qhat, _ = _rms(_stack_heads(zc_ref, Q_OFF + kvh * GROUP * HEAD_DIM))
            qn = (qhat * gq_ref[...]).astype(BF16)
            w, _, _ = _group_softmax(qn, kband, bias_ref, sink_ref, kvh, valid)
            o = _dot(w.astype(BF16), vband)
            _unstack_heads(y_ref, A_WIDTH + kvh * GROUP * HEAD_DIM, o.astype(BF16))

    cur, prev, params, _ = _mix_specs(nb, lambda n: n)
    out, moved = _call(
        body, (z, z, sw, sb_t, gv, gq, gk, sinks, bias), name=name, grid=(nb,), in_specs=[cur, prev] + params,
        out_specs=[pl.BlockSpec((BLOCK, A_WIDTH + B_WIDTH), lambda n: (n, 0))],
        out_shape=[jax.ShapeDtypeStruct((t, A_WIDTH + B_WIDTH), BF16)],
        semantics=("parallel",), comm=comm, relay_at=relay_at)
    return out[0] if comm is None else (out[0], moved)


def _mix_bwd(z, dy, sw, sb_t, gv, gq, gk, sinks, bias, name, comm=None):
    t = z.shape[0]
    nb = t // BLOCK

    def body(zc_ref, zp_ref, dy_ref, sw_ref, sbt_ref, gv_ref, gq_ref, gk_ref, sink_ref, bias_ref,
             dz_ref, dsw_ref, dsb_ref, dgv_ref, dgq_ref, dgk_ref, dsink_ref, dbias_ref, carry_ref):
        step = pl.program_id(0)
        n = nb - 1 - step

        @pl.when(step == 0)
        def _():
            dsw_ref[...] = jnp.zeros_like(dsw_ref)
            dsb_ref[...] = jnp.zeros_like(dsb_ref)
            dgv_ref[...] = jnp.zeros_like(dgv_ref)
            dgq_ref[...] = jnp.zeros_like(dgq_ref)
            dgk_ref[...] = jnp.zeros_like(dgk_ref)
            dsink_ref[...] = jnp.zeros_like(dsink_ref)
            dbias_ref[...] = jnp.zeros_like(dbias_ref)
            carry_ref[...] = jnp.zeros_like(carry_ref)

        ri = lax.broadcasted_iota(jnp.int32, (BLOCK, BLOCK), 0)
        ci = lax.broadcasted_iota(jnp.int32, (BLOCK, BLOCK), 1)
        tril = ri >= ci
        dsb = jnp.zeros((BLOCK, BLOCK), F32)
        for h in range(A_HEADS):
            uo, vo = h * BLOCK, A_WIDTH + h * BLOCK
            u_raw, v_raw = zc_ref[:, uo:uo + BLOCK], zc_ref[:, vo:vo + BLOCK]
            (u, du_raw), (vv, dv_raw) = _gelu_and_grad(u_raw), _gelu_and_grad(v_raw)
            vhat, r = _rms(vv)
            vn = (vhat * gv_ref[h:h + 1, :]).astype(BF16)
            w = jnp.where(tril, sw_ref[h], 0.0).astype(BF16)
            mixed = _dot(w, vn) + sbt_ref[:, h:h + 1]
            dya = dy_ref[:, uo:uo + BLOCK]
            dmixed = dya * u
            dm16 = dmixed.astype(BF16)
            dsw_ref[h] += jnp.where(tril, _dot_nt(dm16, vn), 0.0)
            dsb = dsb + jnp.where(ci == h, jnp.sum(dmixed, axis=1, keepdims=True), 0.0)
            dvn = _dot_tn(w, dm16)
            dgv_ref[h:h + 1, :] += jnp.sum(dvn * vhat, axis=0, keepdims=True)
            dvv = _rms_bwd(dvn * gv_ref[h:h + 1, :], vhat, r)
            dz_ref[:, uo:uo + BLOCK] = (dya * mixed * du_raw).astype(BF16)
            dz_ref[:, vo:vo + BLOCK] = (dvv * dv_raw).astype(BF16)
        dsb_ref[...] += dsb

        valid = _window_mask(n)
        lane = lax.broadcasted_iota(jnp.int32, (1, BLOCK), 1)
        dsink = jnp.zeros((1, BLOCK), F32)
        dgq = jnp.zeros((1, HEAD_DIM), F32)
        for kvh in range(B_KV_HEADS):
            ko, vo = K_OFF + kvh * HEAD_DIM, V_OFF + kvh * HEAD_DIM
            k_raw, v_raw = _kv_band(zc_ref, zp_ref, kvh)
            khat, kr = _rms(k_raw)
            kband = (khat * gk_ref[...]).astype(BF16)
            vband = v_raw.astype(BF16)
            qo = Q_OFF + kvh * GROUP * HEAD_DIM
            qhat, qr = _rms(_stack_heads(zc_ref, qo))
            qn = (qhat * gq_ref[...]).astype(BF16)
            w, wsink, _ = _group_softmax(qn, kband, bias_ref, sink_ref, kvh, valid)
            do = _stack_heads(dy_ref, A_WIDTH + kvh * GROUP * HEAD_DIM).astype(BF16)
            dvb = _dot_tn(w.astype(BF16), do)
            dw = _dot_nt(do, vband)
            rowdot = jnp.sum(w * dw, axis=-1, keepdims=True)
            ds = w * (dw - rowdot)
            dsink_rows = -wsink * rowdot
            for g in range(GROUP):
                head_sum = jnp.sum(dsink_rows[g * BLOCK:(g + 1) * BLOCK], axis=0, keepdims=True)
                dsink = dsink + jnp.where(lane == kvh * GROUP + g, head_sum, 0.0)
            dbias_ref[kvh * GROUP:(kvh + 1) * GROUP] += ds.reshape(GROUP, BLOCK, 2 * BLOCK)
            ds16 = (ds * (HEAD_DIM ** -0.5)).astype(BF16)
            dqn = _dot(ds16, kband)
            dkn = _dot_tn(ds16, qn)
            dgq = dgq + jnp.sum(dqn * qhat, axis=0, keepdims=True)
            _unstack_heads(dz_ref, qo, _rms_bwd(dqn * gq_ref[...], qhat, qr).astype(BF16))
            dgk_ref[...] += jnp.sum(dkn * khat, axis=0, keepdims=True)
            dk = _rms_bwd(dkn * gk_ref[...], khat, kr)
            co = kvh * HEAD_DIM
            dz_ref[:, ko:ko + HEAD_DIM] = (dk[BLOCK:] + carry_ref[:, co:co + HEAD_DIM]).astype(BF16)
            dz_ref[:, vo:vo + HEAD_DIM] = (dvb[BLOCK:] + carry_ref[:, KV_WIDTH + co:KV_WIDTH + co + HEAD_DIM]).astype(BF16)
            carry_ref[:, co:co + HEAD_DIM] = dk[:BLOCK]
            carry_ref[:, KV_WIDTH + co:KV_WIDTH + co + HEAD_DIM] = dvb[:BLOCK]
        dgq_ref[...] += dgq
        dsink_ref[...] += dsink

    order = lambda n: nb - 1 - n
    cur, prev, params, full = _mix_specs(nb, order)
    dy_spec = pl.BlockSpec((BLOCK, A_WIDTH + B_WIDTH), lambda n: (order(n), 0))
    out, moved = _call(
        body, (z, z, dy, sw, sb_t, gv, gq, gk, sinks, bias), name=name, grid=(nb,),
        in_specs=[cur, prev, dy_spec] + params,
        out_specs=[cur, full((A_HEADS, BLOCK, BLOCK)), full((BLOCK, BLOCK)), full((A_HEADS, BLOCK)),
                   full((1, HEAD_DIM)), full((1, HEAD_DIM)), full((1, BLOCK)), full((B_HEADS, BLOCK, 2 * BLOCK))],
        out_shape=[jax.ShapeDtypeStruct((t, IN_COLS), BF16), jax.ShapeDtypeStruct((A_HEADS, BLOCK, BLOCK), F32),
                   jax.ShapeDtypeStruct((BLOCK, BLOCK), F32), jax.ShapeDtypeStruct((A_HEADS, BLOCK), F32),
                   jax.ShapeDtypeStruct((1, HEAD_DIM), F32), jax.ShapeDtypeStruct((1, HEAD_DIM), F32),
                   jax.ShapeDtypeStruct((1, BLOCK), F32), jax.ShapeDtypeStruct((B_HEADS, BLOCK, 2 * BLOCK), F32)],
        scratch_shapes=[pltpu.VMEM((BLOCK, 2 * KV_WIDTH), F32)],
        semantics=("arbitrary",), comm=comm)
    return out if comm is None else (out, moved)


def _ada_fwd(c_all, w_ada, b_cols, name, tn=768):
    nb, d = c_all.shape
    cols = w_ada.shape[1]

    def body(c_ref, w_ref, b_ref, act_ref, mod_ref):
        cv = c_ref[...]
        act = cv * jax.nn.sigmoid(cv)
        act_ref[...] = act
        mod_ref[...] = _dot(act.astype(BF16), w_ref[...].astype(BF16)) + b_ref[...]

    return _pc(
        body, name=name, grid=(cols // tn,),
        in_specs=[pl.BlockSpec((nb, d), lambda j: (0, 0)), pl.BlockSpec((d, tn), lambda j: (0, j)),
                  pl.BlockSpec((1, tn), lambda j: (0, j))],
        out_specs=[pl.BlockSpec((nb, d), lambda j: (0, 0)), pl.BlockSpec((nb, tn), lambda j: (0, j))],
        out_shape=[jax.ShapeDtypeStruct((nb, d), F32), jax.ShapeDtypeStruct((nb, cols), F32)],
        compiler_params=_params(("arbitrary",)),
    )(c_all, w_ada, b_cols)


def _adamw(w, g, m, v):
    m = ADAM_B1 * m + (1.0 - ADAM_B1) * g
    v = ADAM_B2 * v + (1.0 - ADAM_B2) * (g * g)
    m_hat = m / (1.0 - ADAM_B1 ** ADAM_STEP)
    v_hat = v / (1.0 - ADAM_B2 ** ADAM_STEP)
    delta = -ADAM_LR * (m_hat / (jnp.sqrt(v_hat) + ADAM_EPS) + ADAM_WD * w)
    return delta, m, v


def _adam_outer(act_t, dmod, w, m, v, name, tr=256):
    rows, cols = w.shape
    nb = dmod.shape[0]

    def body(act_ref, dm_ref, w_ref, m_ref, v_ref, g_ref, d_ref, nm_ref, nv_ref):
        act = act_ref[...].astype(BF16).astype(F32)
        dm = dm_ref[...].astype(BF16).astype(F32)
        g = act[:, 0:1] * dm[0:1, :]
        for b in range(1, nb):
            g = g + act[:, b:b + 1] * dm[b:b + 1, :]
        g_ref[...] = g
        d_ref[...], nm_ref[...], nv_ref[...] = _adamw(w_ref[...], g, m_ref[...], v_ref[...])

    tile = pl.BlockSpec((tr, cols), lambda i: (i, 0))
    return _pc(
        body, name=name, grid=(rows // tr,),
        in_specs=[pl.BlockSpec((tr, nb), lambda i: (i, 0)), pl.BlockSpec((nb, cols), lambda i: (0, 0)), tile, tile, tile],
        out_specs=[tile] * 4, out_shape=[jax.ShapeDtypeStruct((rows, cols), F32)] * 4,
        compiler_params=_params(("parallel",)),
    )(act_t, dmod, w, m, v)


def _adam_reduce(parts, w, m, v, name, tr, transposed=False):
    rows, cols = w.shape
    pieces = parts if isinstance(parts, (tuple, list)) else (parts,)
    n = len(pieces)

    def total(p_ref):
        g = p_ref[0].astype(F32)
        for i in range(1, N_DEV):
            g = g + p_ref[i].astype(F32)
        return g

    def body(*refs):
        w_ref, m_ref, v_ref, g_ref, d_ref, nm_ref, nv_ref = refs[n:]
        g = total(refs[0]) if n == 1 else jnp.concatenate([total(r) for r in refs[:n]], axis=0)
        if transposed:
            g = g.T
        g_ref[...] = g
        d_ref[...], nm_ref[...], nv_ref[...] = _adamw(w_ref[...], g, m_ref[...], v_ref[...])

    tile = pl.BlockSpec((tr, cols), lambda i: (i, 0))
    if transposed:
        parts_specs = [pl.BlockSpec((N_DEV, p.shape[1], tr), lambda i: (0, 0, i)) for p in pieces]
    else:
        assert n == 1
        parts_specs = [pl.BlockSpec((N_DEV, tr, cols), lambda i: (0, i, 0))]
    return _pc(
        body, name=name, grid=(rows // tr,),
        in_specs=parts_specs + [tile, tile, tile],
        out_specs=[tile] * 4, out_shape=[jax.ShapeDtypeStruct((rows, cols), F32)] * 4,
        compiler_params=_params(("parallel",)),
    )(*pieces, w, m, v)


def _pad_rows(v):
    flat = v.reshape(-1)
    pad = (-flat.shape[0]) % (8 * BLOCK)
    return jnp.pad(flat, (0, pad)).reshape(-1, BLOCK)


def kernel(x, c, w_ada, b_ada, g_ffn1, w1_ffn1, w3_ffn1, w2_ffn1, g_mix, w_in, spatial_w, spatial_b, g_v, g_q, g_k, sinks, rel_bias, w_out, g_ffn2, w1_ffn2, w3_ffn2, w2_ffn2, loss_target, m_w_ada, m_b_ada, m_g_ffn1, m_w1_ffn1, m_w3_ffn1, m_w2_ffn1, m_g_mix, m_w_in, m_spatial_w, m_spatial_b, m_g_v, m_g_q, m_g_k, m_sinks, m_rel_bias, m_w_out, m_g_ffn2, m_w1_ffn2, m_w3_ffn2, m_w2_ffn2, v_w_ada, v_b_ada, v_g_ffn1, v_w1_ffn1, v_w3_ffn1, v_w2_ffn1, v_g_mix, v_w_in, v_spatial_w, v_spatial_b, v_g_v, v_g_q, v_g_k, v_sinks, v_rel_bias, v_w_out, v_g_ffn2, v_w1_ffn2, v_w3_ffn2, v_w2_ffn2):
    me = _linear(_position())
    d = D_MODEL
    x0 = x[0]
    target = loss_target[0]
    t = x0.shape[0]

    big = dict(w1a=w1_ffn1[0], w3a=w3_ffn1[0], w2a=w2_ffn1[0], w_in=w_in[0], w_out=w_out[0],
               w1b=w1_ffn2[0], w3b=w3_ffn2[0], w2b=w2_ffn2[0])
    col_sharded = ("w1a", "w3a", "w_in", "w1b", "w3b")
    full = {}

    def gather(keys, scheme=_GatherTwoLevel):
        return scheme([(big[k].T if k in col_sharded else big[k]).astype(BF16) for k in keys])

    def unpack(keys, gathered):
        for k, gth in zip(keys, gathered):
            full[k] = gth.reshape(-1, gth.shape[-1])

    def share(k):
        gk_ = grads[k]
        return gk_.reshape(N_DEV, gk_.shape[0] // N_DEV, gk_.shape[1])

    c_all = _run_comm(_Direct([c], True), "gather_c")[0].reshape(N_DEV, d)
    b_cols = lax.dynamic_slice(b_ada, (0, me * MOD_COLS), (1, MOD_COLS))
    act_all, mod_cols = _ada_fwd(c_all, w_ada[0], b_cols, "ada_fwd")
    mod = _run_comm(_Direct([mod_cols.reshape(N_DEV, 1, MOD_COLS)], False), "scatter_mod")[0]
    mod = mod.reshape(N_MOD, 1, d)
    sh1, sc1, gt1, sh2, sc2, gt2, sh3, sc3, gt3 = [mod[i] for i in range(N_MOD)]

    tm = min(FFN_ROWS, t)
    h1, gathered = _norm_fwd(x0, None, None, 0.0, g_ffn1, sc1, sh1, "norm1_fwd",
                             comm=gather(("w1a", "w3a"), _GatherRelayed), relay_at=0.9)
    unpack(("w1a", "w3a"), gathered)
    (ga1, gb1, s1), gathered = _ffn_up(h1, full["w1a"], full["w3a"], "ffn1_up",
                                     comm=gather(("w2a", "w_in", "w_out")), relay_at=0.6)
    unpack(("w2a", "w_in", "w_out"), gathered)
    f1, gathered = _matmul(s1, full["w2a"], "nn", tm, 512, D_FF, F32, "ffn1_down", comm=gather(("w1b",)),
                           relay_at=0.7)
    unpack(("w1b",), gathered)
    x1, h2 = _norm_fwd(x0, f1, gt1, 0.5, g_mix, sc2, sh2, "norm2_fwd")
    z = _matmul(h2, full["w_in"], "nt", tm, IN_COLS // 2, d, F32, "mix_in_fwd")
    onehot = _bucket_onehot()
    bias = _select_matmul(rel_bias.T, jnp.asarray(onehot, BF16), "bias_table").reshape(B_HEADS, BLOCK, 2 * BLOCK)
    sb_t = spatial_b[0].T
    mix_params = (spatial_w[0], sb_t, g_v[0], g_q, g_k, sinks, bias)
    ycat, gathered = _mix_fwd(z, *mix_params, "mix_core_fwd", comm=gather(("w3b",)), relay_at=0.6)
    unpack(("w3b",), gathered)
    y = _matmul(ycat, full["w_out"], "nn", 512, d, d, F32, "mix_out_fwd")
    x2, h3 = _norm_fwd(x1, y, gt2, 1.0, g_ffn2, sc3, sh3, "norm3_fwd")
    (ga2, gb2, s2), gathered = _ffn_up(h3, full["w1b"], full["w3b"], "ffn2_up", comm=gather(("w2b",)), relay_at=0.6)
    unpack(("w2b",), gathered)
    f2 = _matmul(s2, full["w2b"], "nn", tm, 512, D_FF, F32, "ffn2_down")
    dx3, df2, loss_part = _loss_head(x2, f2, gt3, 0.5, target, "loss_head")
    loss = lax.psum(loss_part[0, 0], ("x", "y", "c"))

    grads, parts = {}, {}
    tk = min(DW_TOKENS, t)

    def exchange(k):
        return _Direct([share(k)], False)

    grads["w2b"] = _matmul(s2, df2, "tn", D_FF // 4, 512, tk, BF16, "ffn2_dw2")
    (da, db), moved = _ffn_dact(df2, ga2, gb2, full["w2b"], "ffn2_dact", comm=exchange("w2b"))
    parts["w2b"] = moved[0]
    grads["w1b"] = _matmul(da, h3, "tn", D_FF // 4, 512, tk, BF16, "ffn2_dw1")
    dh3, moved = _ffn_dh(da, db, full["w1b"], full["w3b"], "ffn2_dh", comm=exchange("w1b"))
    parts["w1b"] = moved[0]
    grads["w3b"] = _matmul(db, h3, "tn", D_FF // 4, 512, tk, BF16, "ffn2_dw3")
    dx2, dyg, sums3 = _norm_bwd(dh3, x2, dx3, f2, 0.5, g_ffn2, sc3, gt2, 1.0, "norm3_bwd")
    dycat = _matmul(dyg, full["w_out"], "nt", 512, d, d, F32, "mix_out_bwd")
    grads["w_out"] = _matmul(ycat, dyg, "tn", 1024, 512, tk, BF16, "mix_out_dw")
    mix_grads, moved = _mix_bwd(z, dycat, *mix_params, "mix_core_bwd",
                                comm=_Direct([share("w3b"), share("w_out")], False))
    parts["w3b"], parts["w_out"] = moved
    dz, d_sw, d_sb, d_gv, d_gq, d_gk, d_sink, d_bias = mix_grads
    d_rel = _select_matmul(d_bias.reshape(B_HEADS, BLOCK * 2 * BLOCK), jnp.asarray(onehot.T, BF16), "bias_table_bwd")
    dh2 = _matmul(dz, full["w_in"], "nn", tm, d, IN_COLS // 2, F32, "mix_in_bwd")
    grads["w_in"] = _matmul(dz, h2, "tn", IN_COLS // 2, 512, tk // 2, BF16, "mix_in_dw")
    dx1, df1, sums2 = _norm_bwd(dh2, x1, dx2, y, 1.0, g_mix, sc2, gt1, 0.5, "norm2_bwd")
    grads["w2a"], moved = _matmul(s1, df1, "tn", D_FF // 4, 512, tk, BF16, "ffn1_dw2", comm=exchange("w_in"))
    parts["w_in"] = moved[0]
    (da, db), moved = _ffn_dact(df1, ga1, gb1, full["w2a"], "ffn1_dact", comm=exchange("w2a"))
    parts["w2a"] = moved[0]
    early = [("g_mix", sums2[2:3], g_mix, m_g_mix, v_g_mix),
             ("spatial_w", d_sw[None], spatial_w, m_spatial_w, v_spatial_w),
             ("spatial_b", d_sb[:, :A_HEADS].T[None], spatial_b, m_spatial_b, v_spatial_b),
             ("g_v", d_gv[None], g_v, m_g_v, v_g_v),
             ("g_q", d_gq, g_q, m_g_q, v_g_q),
             ("g_k", d_gk, g_k, m_g_k, v_g_k),
             ("sinks", d_sink[:, :B_HEADS], sinks, m_sinks, v_sinks),
             ("rel_bias", d_rel.T, rel_bias, m_rel_bias, v_rel_bias),
             ("g_ffn2", sums3[2:3], g_ffn2, m_g_ffn2, v_g_ffn2)]

    def pack(items, i):
        return jnp.concatenate([_pad_rows(item[i]) for item in items], axis=0)

    grads["w1a"], moved = _matmul(da, h1, "tn", D_FF // 4, 512, tk, BF16, "ffn1_dw1", comm=_Direct([pack(early, 1)], True))
    early_all = moved[0]
    half = share("w1a").shape[1] // 2
    grads["w3a"], moved = _matmul(db, h1, "tn", D_FF // 4, 512, tk, BF16, "ffn1_dw3",
                                  comm=_Direct([share("w1a")], False, rows=[(0, half)]))
    w1a_lo = moved[0]
    dh1, moved = _ffn_dh(da, db, full["w1a"], full["w3a"], "ffn1_dh",
                         comm=_Direct([share("w3a"), share("w1a")], False, rows=[None, (half, half)]))
    parts["w3a"], parts["w1a"] = moved[0], (w1a_lo, moved[1])
    dx0, sums1 = _norm_bwd(dh1, x0, dx1, f1, 0.5, g_ffn1, sc1, None, 0.0, "norm1_bwd")

    moments = dict(w1a=(m_w1_ffn1, v_w1_ffn1), w3a=(m_w3_ffn1, v_w3_ffn1), w2a=(m_w2_ffn1, v_w2_ffn1),
                   w_in=(m_w_in, v_w_in), w_out=(m_w_out, v_w_out),
                   w1b=(m_w1_ffn2, v_w1_ffn2), w3b=(m_w3_ffn2, v_w3_ffn2), w2b=(m_w2_ffn2, v_w2_ffn2))
    upd = {}
    for k in big:
        rows = big[k].shape[0]
        tr = 256 if rows % 256 == 0 else rows // 4
        upd[k] = [o[None] for o in _adam_reduce(parts[k], big[k], moments[k][0][0], moments[k][1][0], "adam_" + k, tr,
                                                transposed=k in col_sharded)]

    dmod = jnp.concatenate([sums1[1:2], sums1[0:1], sums1[3:4], sums2[1:2], sums2[0:1], sums2[3:4],
                            sums3[1:2], sums3[0:1], sums3[3:4]], axis=1)
    late = [("b_ada", dmod, b_ada, m_b_ada, v_b_ada), ("g_ffn1", sums1[2:3], g_ffn1, m_g_ffn1, v_g_ffn1)]
    late_all = _run_comm(_Direct([pack(late, 1)], True), "gather_small")[0]
    for items, g_all, name in ((early, early_all, "adam_small_early"), (late, late_all, "adam_small_late")):
        small_out = _adam_reduce(g_all, pack(items, 2), pack(items, 3), pack(items, 4), name, g_all.shape[1])
        off = 0
        for key, _, w_, _, _ in items:
            n_rows = _pad_rows(w_).shape[0]
            upd[key] = [o[off:off + n_rows].reshape(-1)[:w_.size].reshape(w_.shape) for o in small_out]
            off += n_rows

    dmod_rows = late_all[:, :N_MOD * d // BLOCK, :].reshape(N_DEV, N_MOD * d)
    dmod_cols = lax.dynamic_slice(dmod_rows, (0, me * MOD_COLS), (N_DEV, MOD_COLS))
    upd["w_ada"] = [o[None] for o in _adam_outer(act_all.T, dmod_cols, w_ada[0], m_w_ada[0], v_w_ada[0], "adam_w_ada")]

    order = [("w_ada", "w_ada"), ("b_ada", "b_ada"), ("g_ffn1", "g_ffn1"), ("w1_ffn1", "w1a"), ("w3_ffn1", "w3a"),
             ("w2_ffn1", "w2a"), ("g_mix", "g_mix"), ("w_in", "w_in"), ("spatial_w", "spatial_w"),
             ("spatial_b", "spatial_b"), ("g_v", "g_v"), ("g_q", "g_q"), ("g_k", "g_k"), ("sinks", "sinks"),
             ("rel_bias", "rel_bias"), ("w_out", "w_out"), ("g_ffn2", "g_ffn2"), ("w1_ffn2", "w1b"),
             ("w3_ffn2", "w3b"), ("w2_ffn2", "w2b")]
    outs = [loss, dx0[None]]
    for i in range(4):
        outs += [upd[key][i] for _, key in order]
    return tuple(outs)
```
